```python
import numpy as np
import jax
import jax.numpy as jnp
from jax import lax

D_MODEL = 1024
BATCH = 4
SEQ = 4096
DEPTH = 2

HEAD_DIM = 64
N_HEADS = D_MODEL // HEAD_DIM
A_HEADS = N_HEADS // 2
B_Q_HEADS = N_HEADS // 2
B_KV_HEADS = max(1, B_Q_HEADS // 4)
C_HEADS = N_HEADS
DIL_CONFIGS = ((128, 1), (512, 4), (2048, 16))
A_BLOCK = 128
B_WINDOW = 128
B_BLOCK = 128
GRID_W = 64
NA_ROWS = 8
NA_COLS = 16
NA_COL_BLOCK = 16
NA_COL_SPAN = 32
D_FF_DENSE = 128 * ((8 * D_MODEL // 3 + 127) // 128)
N_EXPERTS = 8
TOP_K = 2
D_FF_EXPERT = 7 * D_MODEL // 2
RMS_EPS = 1e-6
NEG_INF = -1e30
EVEN_SPLITS = (A_HEADS * HEAD_DIM, A_HEADS * HEAD_DIM, A_HEADS * HEAD_DIM,
               B_Q_HEADS * HEAD_DIM, B_KV_HEADS * HEAD_DIM, B_KV_HEADS * HEAD_DIM)
EVEN_IN = sum(EVEN_SPLITS)
EVEN_OUT = (A_HEADS + B_Q_HEADS) * HEAD_DIM
N_EVEN = (DEPTH + 1) // 2
N_ODD = DEPTH // 2

kernel_name = 'hybrid_dilated_swa_natten_moe_encoder'


def rms_norm(x, g):
    xf = x.astype(jnp.float32)
    y = xf * lax.rsqrt(jnp.mean(xf * xf, axis=-1, keepdims=True) + RMS_EPS)
    return (y * g.astype(jnp.float32)).astype(x.dtype)


def alibi_slopes():
    s = np.exp2(-8.0 * np.arange(1, N_HEADS + 1) / N_HEADS).astype(np.float32)
    return jnp.asarray(s[0::2][:A_HEADS]), jnp.asarray(s[1::2][:B_Q_HEADS])


def to_heads(t, n):
    b, s, _ = t.shape
    return t.reshape(b, s, n, HEAD_DIM).transpose(0, 2, 1, 3)


def banded_attention(q, k, v, radius, block, pos_scale, slopes, sink=None):
    n, hq, seq_len, hd = q.shape
    hkv = k.shape[1]
    g = hq // hkv
    blk = min(block, seq_len)
    nb = -(-seq_len // blk)
    lp = nb * blk
    span = blk + 2 * radius
    qb = jnp.pad(q, ((0, 0), (0, 0), (0, lp - seq_len), (0, 0))).reshape(n, hkv, g, nb, blk, hd)
    pad = ((0, 0), (0, 0), (radius, lp - seq_len + radius), (0, 0))
    idx = np.arange(nb)[:, None] * blk + np.arange(span)[None, :]
    kb = jnp.take(jnp.pad(k, pad), idx.reshape(-1), axis=2).reshape(n, hkv, nb, span, hd)
    vb = jnp.take(jnp.pad(v, pad), idx.reshape(-1), axis=2).reshape(n, hkv, nb, span, hd)
    qpos = np.arange(nb)[:, None] * blk + np.arange(blk)[None, :]
    kpos = idx - radius
    dist = np.abs(qpos[:, :, None] - kpos[:, None, :])
    valid = (dist <= radius) & (kpos[:, None, :] >= 0) & (kpos[:, None, :] < seq_len)
    bias = -slopes.astype(jnp.float32).reshape(hkv, g, 1, 1, 1) * jnp.asarray(dist * pos_scale, jnp.float32)
    s = jnp.einsum('nkgbqd,nkbsd->nkgbqs', qb, kb).astype(jnp.float32) * (hd ** -0.5) + bias
    s = jnp.where(jnp.asarray(valid), s, NEG_INF)
    m = jnp.max(s, axis=-1)
    if sink is not None:
        sk = sink.astype(jnp.float32).reshape(hkv, g, 1, 1)
        m = jnp.maximum(m, sk)
    p = jnp.exp(s - m[..., None])
    l = jnp.sum(p, axis=-1)
    if sink is not None:
        l = l + jnp.exp(sk - m)
    o = jnp.einsum('nkgbqs,nkbsd->nkgbqd', p, vb.astype(jnp.float32)) / l[..., None]
    o = o.reshape(n, hq, lp, hd)[:, :, :seq_len]
    lse = (m + jnp.log(l)).reshape(n, hq, lp)[:, :, :seq_len]
    return o, lse


def dilated_attention(q, k, v, slopes):
    b, h, s, hd = q.shape
    outs, lses = [], []
    for window, r in DIL_CONFIGS:
        seg = s // r

        def fold(t):
            return t.reshape(b, h, seg, r, hd).transpose(0, 3, 1, 2, 4).reshape(b * r, h, seg, hd)

        o, lse = banded_attention(fold(q), fold(k), fold(v), radius=window // (2 * r),
                                  block=A_BLOCK, pos_scale=r, slopes=slopes)
        outs.append(o.reshape(b, r, h, seg, hd).transpose(0, 2, 3, 1, 4).reshape(b, h, s, hd))
        lses.append(lse.reshape(b, r, h, seg).transpose(0, 2, 3, 1).reshape(b, h, s))
    w = jax.nn.softmax(jnp.stack(lses), axis=0)
    return sum(w[i][..., None] * outs[i] for i in range(len(outs)))


def even_mixer(h, w_in, qn_a, kn_a, qn_b, kn_b, sink_b, w_out):
    b, s, _ = h.shape
    proj = h @ w_in
    qa, ka, va, qb, kb, vb = jnp.split(proj, np.cumsum(EVEN_SPLITS)[:-1].tolist(), axis=-1)
    qa = rms_norm(to_heads(qa, A_HEADS), qn_a)
    ka = rms_norm(to_heads(ka, A_HEADS), kn_a)
    va = to_heads(va, A_HEADS)
    qb = rms_norm(to_heads(qb, B_Q_HEADS), qn_b)
    kb = rms_norm(to_heads(kb, B_KV_HEADS), kn_b)
    vb = to_heads(vb, B_KV_HEADS)
    slopes_a, slopes_b = alibi_slopes()
    oa = dilated_attention(qa, ka, va, slopes_a)
    ob, _ = banded_attention(qb, kb, vb, radius=B_WINDOW, block=B_BLOCK, pos_scale=1,
                             slopes=slopes_b, sink=sink_b)
    o = jnp.concatenate([oa, ob], axis=1).transpose(0, 2, 1, 3).reshape(b, s, EVEN_OUT)
    return o.astype(h.dtype) @ w_out


def neighbourhood_mixer(h, w_qkv, qn, kn, rpb, w_out):
    b, s, _ = h.shape
    rows = s // GRID_W
    kh = min(NA_ROWS, rows)
    q, k, v = jnp.split(h @ w_qkv, 3, axis=-1)
    q = rms_norm(to_heads(q, C_HEADS), qn)
    k = rms_norm(to_heads(k, C_HEADS), kn)
    v = to_heads(v, C_HEADS)
    qg = q.reshape(b, C_HEADS, rows, GRID_W, HEAD_DIM)
    kg = k.reshape(b, C_HEADS, rows, GRID_W, HEAD_DIM)
    vg = v.reshape(b, C_HEADS, rows, GRID_W, HEAD_DIM)
    n_cb = GRID_W // NA_COL_BLOCK
    qcol = np.arange(GRID_W).reshape(n_cb, NA_COL_BLOCK)
    kcol = np.clip(qcol[:, 0] - NA_COLS // 2, 0, GRID_W - NA_COL_SPAN)[:, None] + np.arange(NA_COL_SPAN)
    qstart = np.clip(qcol - NA_COLS // 2, 0, GRID_W - NA_COLS)
    col_valid = jnp.asarray((kcol[:, None, :] >= qstart[:, :, None]) &
                            (kcol[:, None, :] < qstart[:, :, None] + NA_COLS))
    col_off = np.clip(kcol[:, None, :] - qcol[:, :, None], -(NA_COLS - 1), NA_COLS - 1) + NA_COLS - 1
    rpb_cols = jnp.take(rpb.astype(jnp.float32), col_off.reshape(-1), axis=2).reshape(
        C_HEADS, 2 * NA_ROWS - 1, n_cb, NA_COL_BLOCK, NA_COL_SPAN)
    kcol_flat = kcol.reshape(-1)

    def row_step(i):
        rs = jnp.clip(i - kh // 2, 0, rows - kh)
        qi = lax.dynamic_index_in_dim(qg, i, axis=2, keepdims=False).reshape(
            b, C_HEADS, n_cb, NA_COL_BLOCK, HEAD_DIM)
        kr = lax.dynamic_slice_in_dim(kg, rs, kh, axis=2)
        vr = lax.dynamic_slice_in_dim(vg, rs, kh, axis=2)
        kc = jnp.take(kr, kcol_flat, axis=3).reshape(b, C_HEADS, kh, n_cb, NA_COL_SPAN, HEAD_DIM)
        vc = jnp.take(vr, kcol_flat, axis=3).reshape(b, C_HEADS, kh, n_cb, NA_COL_SPAN, HEAD_DIM)
        sc = jnp.einsum('bhcqd,bhrckd->bhcqrk', qi, kc).astype(jnp.float32) * (HEAD_DIM ** -0.5)
        row_off = rs + jnp.arange(kh) - i + NA_ROWS - 1
        bias = jnp.take(rpb_cols, row_off, axis=1).transpose(0, 2, 3, 1, 4)
        sc = jnp.where(col_valid[:, :, None, :], sc + bias, NEG_INF)
        p = jax.nn.softmax(sc.reshape(sc.shape[:4] + (kh * NA_COL_SPAN,)), axis=-1).reshape(sc.shape)
        return jnp.einsum('bhcqrk,bhrckd->bhcqd', p, vc.astype(jnp.float32))

    o = lax.map(row_step, jnp.arange(rows))
    o = o.transpose(1, 2, 0, 3, 4, 5).reshape(b, C_HEADS, s, HEAD_DIM)
    o = o.transpose(0, 2, 1, 3).reshape(b, s, C_HEADS * HEAD_DIM)
    return o.astype(h.dtype) @ w_out


def swiglu(h, w_gate, w_up, w_down):
    return (jax.nn.silu(h @ w_gate) * (h @ w_up)) @ w_down


def moe_swiglu(h, w_router, w_gate, w_up, w_down):
    b, s, d = h.shape
    t = h.reshape(b * s, d)
    logits = (t @ w_router).astype(jnp.float32)
    top_v, top_i = lax.top_k(logits, TOP_K)
    gates = jax.nn.softmax(top_v, axis=-1)
    combine = jnp.sum(jax.nn.one_hot(top_i, N_EXPERTS, dtype=jnp.float32) * gates[..., None], axis=1)
    out = jnp.zeros((b * s, d), jnp.float32)
    for e in range(N_EXPERTS):
        y = swiglu(t, w_gate[e], w_up[e], w_down[e]).astype(jnp.float32)
        out = out + combine[:, e:e + 1] * y
    return out.reshape(b, s, d).astype(h.dtype)


def setup_inputs(seed: int = 0) -> dict:
    key = jax.random.key(seed)
    ks = iter(jax.random.split(key, 32))

    def nrm(shape, scale):
        return scale * jax.random.normal(next(ks), shape, jnp.float32)

    def gain(shape):
        return 1.0 + 0.05 * jax.random.normal(next(ks), shape, jnp.float32)

    d = D_MODEL
    return {
        'x': nrm((BATCH, SEQ, d), 1.0),
        'ev_norm1': gain((N_EVEN, d)),
        'ev_w_in': nrm((N_EVEN, d, EVEN_IN), d ** -0.5),
        'ev_qn_a': gain((N_EVEN, HEAD_DIM)),
        'ev_kn_a': gain((N_EVEN, HEAD_DIM)),
        'ev_qn_b': gain((N_EVEN, HEAD_DIM)),
        'ev_kn_b': gain((N_EVEN, HEAD_DIM)),
        'ev_sink_b': nrm((N_EVEN, B_Q_HEADS), 1.0),
        'ev_w_out': nrm((N_EVEN, EVEN_OUT, d), EVEN_OUT ** -0.5),
        'ev_norm2': gain((N_EVEN, d)),
        'ev_ffn_gate': nrm((N_EVEN, d, D_FF_DENSE), d ** -0.5),
        'ev_ffn_up': nrm((N_EVEN, d, D_FF_DENSE), d ** -0.5),
        'ev_ffn_down': nrm((N_EVEN, D_FF_DENSE, d), D_FF_DENSE ** -0.5),
        'od_norm1': gain((N_ODD, d)),
        'od_w_qkv': nrm((N_ODD, d, 3 * C_HEADS * HEAD_DIM), d ** -0.5),
        'od_qn': gain((N_ODD, HEAD_DIM)),
        'od_kn': gain((N_ODD, HEAD_DIM)),
        'od_rpb': nrm((N_ODD, C_HEADS, 2 * NA_ROWS - 1, 2 * NA_COLS - 1), 0.5),
        'od_w_out': nrm((N_ODD, C_HEADS * HEAD_DIM, d), (C_HEADS * HEAD_DIM) ** -0.5),
        'od_norm2': gain((N_ODD, d)),
        'od_router': nrm((N_ODD, d, N_EXPERTS), d ** -0.5),
        'od_exp_gate': nrm((N_ODD, N_EXPERTS, d, D_FF_EXPERT), d ** -0.5),
        'od_exp_up': nrm((N_ODD, N_EXPERTS, d, D_FF_EXPERT), d ** -0.5),
        'od_exp_down': nrm((N_ODD, N_EXPERTS, D_FF_EXPERT, d), D_FF_EXPERT ** -0.5),
    }


def reference(x, ev_norm1, ev_w_in, ev_qn_a, ev_kn_a, ev_qn_b, ev_kn_b, ev_sink_b, ev_w_out,
              ev_norm2, ev_ffn_gate, ev_ffn_up, ev_ffn_down,
              od_norm1, od_w_qkv, od_qn, od_kn, od_rpb, od_w_out, od_norm2, od_router,
              od_exp_gate, od_exp_up, od_exp_down):
    for layer in range(DEPTH):
        j = layer // 2
        if layer % 2 == 0:
            h = rms_norm(x, ev_norm1[j])
            x = x + even_mixer(h, ev_w_in[j], ev_qn_a[j], ev_kn_a[j], ev_qn_b[j], ev_kn_b[j],
                               ev_sink_b[j], ev_w_out[j])
            h = rms_norm(x, ev_norm2[j])
            x = x + swiglu(h, ev_ffn_gate[j], ev_ffn_up[j], ev_ffn_down[j])
        else:
            h = rms_norm(x, od_norm1[j])
            x = x + neighbourhood_mixer(h, od_w_qkv[j], od_qn[j], od_kn[j], od_rpb[j], od_w_out[j])
            h = rms_norm(x, od_norm2[j])
            x = x + moe_swiglu(h, od_router[j], od_exp_gate[j], od_exp_up[j], od_exp_down[j])
    return x
```

```python
import functools

import numpy as np
import jax
import jax.numpy as jnp
from jax import lax
from jax.experimental import pallas as pl
from jax.experimental.pallas import tpu as pltpu

D_MODEL = 1024
HEAD_DIM = 64
N_HEADS = D_MODEL // HEAD_DIM
A_HEADS = N_HEADS // 2
B_Q_HEADS = N_HEADS // 2
B_KV_HEADS = max(1, B_Q_HEADS // 4)
C_HEADS = N_HEADS
DIL_CONFIGS = ((128, 1), (512, 4), (2048, 16))
A_BLOCK = 128
B_WINDOW = 128
B_BLOCK = 128
GRID_W = 64
NA_ROWS = 8
NA_COLS = 16
N_EXPERTS = 8
RMS_EPS = 1e-6
NEG_INF = -1e30

LANES = 128
MXU_N = 256
VMEM_LIMIT = 56 * 1024 * 1024

F32 = jnp.float32
BF16 = jnp.bfloat16


def _lo_mask(shape):
    return lax.broadcasted_iota(jnp.int32, shape, len(shape) - 1) < HEAD_DIM


def _rms_rows(x, g):
    ms = jnp.mean(x * x, axis=-1, keepdims=True)
    return x * lax.rsqrt(ms + RMS_EPS) * g


def _norm_proj_kernel(x_ref, g_ref, w_ref, cg_ref, o_ref, *, norm_chunks):
    h = _rms_rows(x_ref[...], g_ref[...]).astype(BF16)
    n_out = o_ref.shape[-1]
    lo = _lo_mask((1, LANES))
    for j in range(n_out // MXU_N):
        y = jnp.dot(h, w_ref[:, j * MXU_N:(j + 1) * MXU_N], preferred_element_type=F32)
        for half in range(MXU_N // LANES):
            c = j * (MXU_N // LANES) + half
            yc = y[:, half * LANES:(half + 1) * LANES]
            if norm_chunks[c]:
                sq = yc * yc
                s_lo = jnp.sum(jnp.where(lo, sq, 0.0), axis=-1, keepdims=True)
                s_hi = jnp.sum(jnp.where(lo, 0.0, sq), axis=-1, keepdims=True)
                inv = jnp.where(lo, lax.rsqrt(s_lo * (1.0 / HEAD_DIM) + RMS_EPS),
                                lax.rsqrt(s_hi * (1.0 / HEAD_DIM) + RMS_EPS))
                yc = yc * inv * cg_ref[:, c * LANES:(c + 1) * LANES]
            o_ref[:, c * LANES:(c + 1) * LANES] = yc.astype(o_ref.dtype)


def _norm_proj(x, g, w, col_gain, norm_chunks, tm=512):
    t, d = x.shape
    n = w.shape[1]
    return pl.pallas_call(
        functools.partial(_norm_proj_kernel, norm_chunks=norm_chunks),
        grid=(t // tm,),
        in_specs=[
            pl.BlockSpec((tm, d), lambda i: (i, 0)),
            pl.BlockSpec((1, d), lambda i: (0, 0)),
            pl.BlockSpec((d, n), lambda i: (0, 0)),
            pl.BlockSpec((1, n), lambda i: (0, 0)),
        ],
        out_specs=pl.BlockSpec((tm, n), lambda i: (i, 0)),
        out_shape=jax.ShapeDtypeStruct((t, n), BF16),
        compiler_params=pltpu.CompilerParams(
            dimension_semantics=("arbitrary",), vmem_limit_bytes=VMEM_LIMIT),
        name="norm_proj",
    )(x, g.reshape(1, d), w, col_gain.reshape(1, n))


def _band_kernel(par_ref, q_ref, k_ref, v_ref, *refs, seq, blk, radius, pos_scale,
                 has_sink, want_lse):
    if want_lse:
        o_ref, lse_ref, bias_scr = refs
    else:
        o_ref, bias_scr = refs
        lse_ref = None
    span = blk + 2 * radius
    c = pl.program_id(2)
    slopes = (par_ref[0, 2 * c], par_ref[0, 2 * c + 1])
    sinks = (par_ref[1, 2 * c], par_ref[1, 2 * c + 1])

    iq = lax.broadcasted_iota(jnp.int32, (blk, span), 0)
    ik = lax.broadcasted_iota(jnp.int32, (blk, span), 1)
    for var in range(3):
        dist = jnp.abs(iq + var * radius - ik)
        valid = dist <= radius
        dist_f = (dist * pos_scale).astype(F32)
        for hh in range(2):
            bias_scr[var, hh] = jnp.where(valid, -(slopes[hh] * dist_f), NEG_INF)

    lo_q = _lo_mask((1, LANES))

    def body(j, carry):
        a = pl.multiple_of(j * blk, blk)
        ks = pl.multiple_of(jnp.clip(a - radius, 0, seq - span), radius)
        var = (a - ks) // radius
        q = q_ref[pl.ds(a, blk), :]
        k = k_ref[pl.ds(ks, span), :]
        v = v_ref[pl.ds(ks, span), :]
        outs, lses = [], []
        for hh in range(2):
            mask = lo_q if hh == 0 else jnp.logical_not(lo_q)
            qh = jnp.where(mask, q, jnp.zeros_like(q))
            s = lax.dot_general(qh, k, (((1,), (1,)), ((), ())), preferred_element_type=F32)
            s = s + bias_scr[var, hh]
            m = jnp.max(s, axis=-1, keepdims=True)
            if has_sink:
                m = jnp.maximum(m, sinks[hh])
            p = jnp.exp(s - m)
            l = jnp.sum(p, axis=-1, keepdims=True)
            if has_sink:
                l = l + jnp.exp(sinks[hh] - m)
            o = jnp.dot(p.astype(BF16), v, preferred_element_type=F32)
            outs.append(o * (1.0 / l))
            lses.append(m + jnp.log(l))
        lo = _lo_mask((blk, LANES))
        o_ref[pl.ds(a, blk), :] = jnp.where(lo, outs[0], outs[1]).astype(o_ref.dtype)
        if want_lse:
            lse_ref[pl.ds(a, blk), :] = jnp.where(lo, lses[0], lses[1])
        return carry

    lax.fori_loop(0, seq // blk, body, 0)


def _band_attention(qkv, params, *, n_cols, q_blk0, k_blk0, v_blk0, kv_per_pair, n_pairs,
                    dil, radius, blk, has_sink, want_lse):
    b, s, _ = qkv.shape
    seq = s // dil
    cb = n_cols // LANES
    view = qkv.reshape(b, seq, dil * n_cols)
    out_cols = n_pairs * LANES
    span = blk + 2 * radius

    def in_map(blk0, per_pair):
        if per_pair:
            return lambda bi, p, c: (bi, 0, p * cb + blk0 + c)
        return lambda bi, p, c: (bi, 0, p * cb + blk0)

    out_map = lambda bi, p, c: (bi, 0, p * n_pairs + c)
    out_shape = [jax.ShapeDtypeStruct((b, seq, dil * out_cols), BF16)]
    out_specs = [pl.BlockSpec((None, seq, LANES), out_map)]
    if want_lse:
        out_shape.append(jax.ShapeDtypeStruct((b, seq, dil * out_cols), F32))
        out_specs.append(pl.BlockSpec((None, seq, LANES), out_map))
    res = pl.pallas_call(
        functools.partial(_band_kernel, seq=seq, blk=blk, radius=radius, pos_scale=dil,
                          has_sink=has_sink, want_lse=want_lse),
        grid=(b, dil, n_pairs),
        in_specs=[
            pl.BlockSpec(memory_space=pltpu.SMEM),
            pl.BlockSpec((None, seq, LANES), in_map(q_blk0, True)),
            pl.BlockSpec((None, seq, LANES), in_map(k_blk0, kv_per_pair)),
            pl.BlockSpec((None, seq, LANES), in_map(v_blk0, kv_per_pair)),
        ],
        out_specs=out_specs,
        out_shape=out_shape,
        scratch_shapes=[pltpu.VMEM((3, 2, blk, span), F32)],
        compiler_params=pltpu.CompilerParams(
            dimension_semantics=("arbitrary", "arbitrary", "arbitrary"),
            vmem_limit_bytes=VMEM_LIMIT),
        name=f"band_attn_d{dil}" + ("_sink" if has_sink else ""),
    )(params, view, view, view)
    return [r.reshape(b, s, out_cols) for r in res]


def _na_kernel(q_ref, k_ref, v_ref, bias_ref, o_ref, *, rows):
    kh = NA_ROWS
    lo_q = _lo_mask((1, LANES))
    lo = _lo_mask((GRID_W, LANES))

    def body(i, carry):
        rs = jnp.clip(i - kh // 2, 0, rows - kh)
        off = rs - i + (NA_ROWS - 1)
        q = q_ref[pl.ds(pl.multiple_of(i * GRID_W, GRID_W), GRID_W), :]
        ks = pl.multiple_of(rs * GRID_W, GRID_W)
        k = k_ref[pl.ds(ks, kh * GRID_W), :]
        v = v_ref[pl.ds(ks, kh * GRID_W), :]
        outs = []
        for hh in range(2):
            mask = lo_q if hh == 0 else jnp.logical_not(lo_q)
            qh = jnp.where(mask, q, jnp.zeros_like(q))
            s = lax.dot_general(qh, k, (((1,), (1,)), ((), ())), preferred_element_type=F32)
            s = s + bias_ref[off, hh]
            m = jnp.max(s, axis=-1, keepdims=True)
            p = jnp.exp(s - m)
            l = jnp.sum(p, axis=-1, keepdims=True)
            o = jnp.dot(p.astype(BF16), v, preferred_element_type=F32)
            outs.append(o * (1.0 / l))
        o_ref[pl.ds(pl.multiple_of(i * GRID_W, GRID_W), GRID_W), :] = jnp.where(
            lo, outs[0], outs[1]).astype(o_ref.dtype)
        return carry

    lax.fori_loop(0, rows, body, 0)


def _na_bias_table(rpb):
    qc = np.arange(GRID_W)[:, None]
    kc = np.arange(GRID_W)[None, :]
    qstart = np.clip(qc - NA_COLS // 2, 0, GRID_W - NA_COLS)
    valid = (kc >= qstart) & (kc < qstart + NA_COLS)
    coff = np.clip(kc - qc, -(NA_COLS - 1), NA_COLS - 1) + NA_COLS - 1
    row_idx = np.arange(NA_ROWS)[:, None] + np.arange(NA_ROWS)[None, :]
    t = rpb.astype(F32)[:, row_idx]
    t = t[:, :, :, coff]
    t = jnp.where(jnp.asarray(valid), t, NEG_INF)
    t = t.transpose(0, 1, 3, 2, 4).reshape(C_HEADS // 2, 2, NA_ROWS, GRID_W, NA_ROWS * GRID_W)
    return t.transpose(0, 2, 1, 3, 4)


def _na_attention(qkv, bias):
    b, s, n_cols = qkv.shape
    n_pairs = C_HEADS // 2
    rows = s // GRID_W
    return pl.pallas_call(
        functools.partial(_na_kernel, rows=rows),
        grid=(b, n_pairs),
        in_specs=[
            pl.BlockSpec((None, s, LANES), lambda bi, c: (bi, 0, c)),
            pl.BlockSpec((None, s, LANES), lambda bi, c: (bi, 0, n_pairs + c)),
            pl.BlockSpec((None, s, LANES), lambda bi, c: (bi, 0, 2 * n_pairs + c)),
            pl.BlockSpec((None, NA_ROWS, 2, GRID_W, NA_ROWS * GRID_W),
                         lambda bi, c: (c, 0, 0, 0, 0)),
        ],
        out_specs=pl.BlockSpec((None, s, LANES), lambda bi, c: (bi, 0, c)),
        out_shape=jax.ShapeDtypeStruct((b, s, n_pairs * LANES), BF16),
        compiler_params=pltpu.CompilerParams(
            dimension_semantics=("arbitrary", "arbitrary"), vmem_limit_bytes=VMEM_LIMIT),
        name="na_attn",
    )(qkv, qkv, qkv, bias)


def _even_out_kernel(x_ref, o1_ref, o2_ref, o3_ref, l1_ref, l2_ref, l3_ref, ob_ref,
                     wa_ref, wb_ref, y_ref):
    l1, l2, l3 = l1_ref[...], l2_ref[...], l3_ref[...]
    m = jnp.maximum(jnp.maximum(l1, l2), l3)
    e1, e2, e3 = jnp.exp(l1 - m), jnp.exp(l2 - m), jnp.exp(l3 - m)
    inv = 1.0 / (e1 + e2 + e3)
    oa = ((e1 * inv) * o1_ref[...].astype(F32) + (e2 * inv) * o2_ref[...].astype(F32)
          + (e3 * inv) * o3_ref[...].astype(F32))
    y = jnp.dot(oa.astype(BF16), wa_ref[...], preferred_element_type=F32)
    y = y + jnp.dot(ob_ref[...], wb_ref[...], preferred_element_type=F32)
    y_ref[...] = x_ref[...] + y


def _even_out(x, oas, lses, ob, wa, wb, tm=512):
    t, d = x.shape
    ha = wa.shape[0]
    row = lambda w: pl.BlockSpec((tm, w), lambda i: (i, 0))
    full = lambda a: pl.BlockSpec(a.shape, lambda i: (0, 0))
    return pl.pallas_call(
        _even_out_kernel,
        grid=(t // tm,),
        in_specs=[row(d)] + [row(ha)] * 3 + [row(ha)] * 3 + [row(ob.shape[1]), full(wa), full(wb)],
        out_specs=row(d),
        out_shape=jax.ShapeDtypeStruct((t, d), F32),
        compiler_params=pltpu.CompilerParams(
            dimension_semantics=("arbitrary",), vmem_limit_bytes=VMEM_LIMIT),
        name="even_out_proj",
    )(x, *oas, *lses, ob, wa, wb)


def _out_proj_kernel(x_ref, o_ref, w_ref, y_ref):
    y_ref[...] = x_ref[...] + jnp.dot(o_ref[...], w_ref[...], preferred_element_type=F32)


def _out_proj(x, o, w, tm=512):
    t, d = x.shape
    return pl.pallas_call(
        _out_proj_kernel,
        grid=(t // tm,),
        in_specs=[pl.BlockSpec((tm, d), lambda i: (i, 0)),
                  pl.BlockSpec((tm, o.shape[1]), lambda i: (i, 0)),
                  pl.BlockSpec(w.shape, lambda i: (0, 0))],
        out_specs=pl.BlockSpec((tm, d), lambda i: (i, 0)),
        out_shape=jax.ShapeDtypeStruct((t, d), F32),
        compiler_params=pltpu.CompilerParams(
            dimension_semantics=("arbitrary",), vmem_limit_bytes=VMEM_LIMIT),
        name="out_proj",
    )(x, o, w)


def _silu(g):
    return g * (1.0 / (1.0 + jnp.exp(-g)))


def _ffn_kernel(x_ref, g_ref, wg_ref, wu_ref, wd_ref, y_ref, a_scr):
    x = x_ref[...]
    h = _rms_rows(x, g_ref[...]).astype(BF16)
    f = wg_ref.shape[1]
    for j in range(f // MXU_N):
        sl = slice(j * MXU_N, (j + 1) * MXU_N)
        gate = jnp.dot(h, wg_ref[:, sl], preferred_element_type=F32)
        up = jnp.dot(h, wu_ref[:, sl], preferred_element_type=F32)
        a_scr[:, sl] = (_silu(gate) * up).astype(BF16)
    y_ref[...] = x + jnp.dot(a_scr[...], wd_ref[...], preferred_element_type=F32)


def _ffn(x, g, wg, wu, wd, tm=512):
    t, d = x.shape
    f = wg.shape[1]
    const = lambda a: pl.BlockSpec(a.shape, lambda i: (0, 0), pipeline_mode=pl.Buffered(1))
    return pl.pallas_call(
        _ffn_kernel,
        grid=(t // tm,),
        in_specs=[pl.BlockSpec((tm, d), lambda i: (i, 0)),
                  pl.BlockSpec((1, d), lambda i: (0, 0)),
                  const(wg), const(wu), const(wd)],
        out_specs=pl.BlockSpec((tm, d), lambda i: (i, 0)),
        out_shape=jax.ShapeDtypeStruct((t, d), F32),
        scratch_shapes=[pltpu.VMEM((tm, f), BF16)],
        compiler_params=pltpu.CompilerParams(
            dimension_semantics=("arbitrary",), vmem_limit_bytes=VMEM_LIMIT),
        name="dense_swiglu",
    )(x, g.reshape(1, d), wg, wu, wd)


def _moe_dense_kernel(x_ref, g_ref, wr_ref, wg_ref, wu_ref, wd_ref, y_ref,
                      h_scr, cw_scr, acc_scr):
    e = pl.program_id(1)
    f = pl.program_id(2)
    lane = lax.broadcasted_iota(jnp.int32, (1, LANES), 1)

    @pl.when((e == 0) & (f == 0))
    def _():
        x = x_ref[...]
        h = _rms_rows(x, g_ref[...])
        h_scr[...] = h.astype(BF16)
        logits = jnp.dot(h, wr_ref[...], preferred_element_type=F32,
                         precision=lax.Precision.HIGHEST)
        logits = jnp.where(lane < N_EXPERTS, logits, NEG_INF)
        v1 = jnp.max(logits, axis=-1, keepdims=True)
        i1 = jnp.min(jnp.where(logits == v1, lane, LANES), axis=-1, keepdims=True)
        rest = jnp.where(lane == i1, NEG_INF, logits)
        v2 = jnp.max(rest, axis=-1, keepdims=True)
        i2 = jnp.min(jnp.where(rest == v2, lane, LANES), axis=-1, keepdims=True)
        e2 = jnp.exp(v2 - v1)
        den = 1.0 / (1.0 + e2)
        cw_scr[...] = jnp.where(lane == i1, den, 0.0) + jnp.where(lane == i2, e2 * den, 0.0)
        acc_scr[...] = x

    h = h_scr[...]
    gate = jnp.dot(h, wg_ref[...], preferred_element_type=F32)
    up = jnp.dot(h, wu_ref[...], preferred_element_type=F32)
    a = (_silu(gate) * up).astype(BF16)
    y = jnp.dot(a, wd_ref[...], preferred_element_type=F32)
    cwe = jnp.sum(jnp.where(lane == e, cw_scr[...], 0.0), axis=-1, keepdims=True)
    acc_scr[...] += cwe * y

    @pl.when((e == pl.num_programs(1) - 1) & (f == pl.num_programs(2) - 1))
    def _():
        y_ref[...] = acc_scr[...]


def _moe_dense(x, g, wr, wg, wu, wd, tm=1024, tf=512):
    t, d = x.shape
    n_e, _, f = wg.shape
    return pl.pallas_call(
        _moe_dense_kernel,
        grid=(t // tm, n_e, f // tf),
        in_specs=[pl.BlockSpec((tm, d), lambda i, e, j: (i, 0)),
                  pl.BlockSpec((1, d), lambda i, e, j: (0, 0)),
                  pl.BlockSpec(wr.shape, lambda i, e, j: (0, 0)),
                  pl.BlockSpec((None, d, tf), lambda i, e, j: (e, 0, j)),
                  pl.BlockSpec((None, d, tf), lambda i, e, j: (e, 0, j)),
                  pl.BlockSpec((None, tf, d), lambda i, e, j: (e, j, 0))],
        out_specs=pl.BlockSpec((tm, d), lambda i, e, j: (i, 0)),
        out_shape=jax.ShapeDtypeStruct((t, d), F32),
        scratch_shapes=[pltpu.VMEM((tm, d), BF16), pltpu.VMEM((tm, LANES), F32),
                        pltpu.VMEM((tm, d), F32)],
        compiler_params=pltpu.CompilerParams(
            dimension_semantics=("arbitrary", "arbitrary", "arbitrary"),
            vmem_limit_bytes=VMEM_LIMIT),
        name="moe_dense",
    )(x, g.reshape(1, d), wr, wg, wu, wd)


def _alibi_slopes():
    s = np.exp2(-8.0 * np.arange(1, N_HEADS + 1) / N_HEADS).astype(np.float32)
    return s[0::2][:A_HEADS], s[1::2][:B_Q_HEADS]


_B_HEAD_ORDER = np.array([0, 4, 1, 5, 2, 6, 3, 7])


def _even_layer(x2d, batch, norm1, w_in, qn_a, kn_a, qn_b, kn_b, sink_b, w_out,
                norm2, w_gate, w_up, w_down):
    t, d = x2d.shape
    s = t // batch
    wa = A_HEADS * HEAD_DIM
    wq_b = B_Q_HEADS * HEAD_DIM
    wkv_b = B_KV_HEADS * HEAD_DIM
    qb0 = 3 * wa
    q_scale = HEAD_DIM ** -0.5

    w_qb = w_in[:, qb0:qb0 + wq_b].reshape(d, B_Q_HEADS, HEAD_DIM)[:, _B_HEAD_ORDER]
    w_in_p = jnp.concatenate(
        [w_in[:, :qb0], w_qb.reshape(d, wq_b), w_in[:, qb0 + wq_b:]], axis=1).astype(BF16)
    ones = jnp.ones((HEAD_DIM,), F32)
    col_gain = jnp.concatenate([
        jnp.tile(qn_a * q_scale, A_HEADS), jnp.tile(kn_a, A_HEADS), jnp.tile(ones, A_HEADS),
        jnp.tile(qn_b * q_scale, B_Q_HEADS), jnp.tile(kn_b, B_KV_HEADS),
        jnp.tile(ones, B_KV_HEADS)])
    cpl = wa // LANES
    norm_chunks = ((True,) * (2 * cpl) + (False,) * cpl + (True,) * (wq_b // LANES)
                   + (True,) * (wkv_b // LANES) + (False,) * (wkv_b // LANES))
    qkv = _norm_proj(x2d, norm1, w_in_p, col_gain, norm_chunks)
    n_cols = qkv.shape[1]
    qkv3 = qkv.reshape(batch, s, n_cols)

    slopes_a, slopes_b = _alibi_slopes()
    par_a = jnp.stack([jnp.asarray(slopes_a), jnp.zeros((A_HEADS,), F32)])
    oas, lses = [], []
    for window, dil in DIL_CONFIGS:
        o, lse = _band_attention(
            qkv3, par_a, n_cols=n_cols, q_blk0=0, k_blk0=cpl, v_blk0=2 * cpl,
            kv_per_pair=True, n_pairs=cpl, dil=dil, radius=window // (2 * dil),
            blk=A_BLOCK, has_sink=False, want_lse=True)
        oas.append(o.reshape(t, wa))
        lses.append(lse.reshape(t, wa))
    par_b = jnp.stack([jnp.asarray(slopes_b[_B_HEAD_ORDER]),
                       sink_b.astype(F32)[_B_HEAD_ORDER]])
    (ob,) = _band_attention(
        qkv3, par_b, n_cols=n_cols, q_blk0=3 * cpl, k_blk0=3 * cpl + wq_b // LANES,
        v_blk0=3 * cpl + wq_b // LANES + wkv_b // LANES, kv_per_pair=False,
        n_pairs=wq_b // LANES, dil=1, radius=B_WINDOW, blk=B_BLOCK, has_sink=True,
        want_lse=False)
    w_out_a = w_out[:wa].astype(BF16)
    w_out_b = w_out[wa:].reshape(B_Q_HEADS, HEAD_DIM, d)[_B_HEAD_ORDER].reshape(wq_b, d)
    x1 = _even_out(x2d, oas, lses, ob.reshape(t, wq_b), w_out_a, w_out_b.astype(BF16))
    return _ffn(x1, norm2, w_gate.astype(BF16), w_up.astype(BF16), w_down.astype(BF16))


def _odd_layer(x2d, batch, norm1, w_qkv, qn, kn, rpb, w_out, norm2, w_router,
               w_gate, w_up, w_down):
    t, d = x2d.shape
    s = t // batch
    wc = C_HEADS * HEAD_DIM
    ones = jnp.ones((HEAD_DIM,), F32)
    col_gain = jnp.concatenate([jnp.tile(qn * HEAD_DIM ** -0.5, C_HEADS),
                                jnp.tile(kn, C_HEADS), jnp.tile(ones, C_HEADS)])
    cpl = wc // LANES
    norm_chunks = (True,) * (2 * cpl) + (False,) * cpl
    qkv = _norm_proj(x2d, norm1, w_qkv.astype(BF16), col_gain, norm_chunks)
    o = _na_attention(qkv.reshape(batch, s, 3 * wc), _na_bias_table(rpb))
    x1 = _out_proj(x2d, o.reshape(t, wc), w_out.astype(BF16))
    wr = jnp.zeros((d, LANES), F32).at[:, :N_EXPERTS].set(w_router.astype(F32))
    return _moe_dense(x1, norm2, wr, w_gate.astype(BF16), w_up.astype(BF16),
                      w_down.astype(BF16))


def kernel(x, ev_norm1, ev_w_in, ev_qn_a, ev_kn_a, ev_qn_b, ev_kn_b, ev_sink_b, ev_w_out, ev_norm2, ev_ffn_gate, ev_ffn_up, ev_ffn_down, od_norm1, od_w_qkv, od_qn, od_kn, od_rpb, od_w_out, od_norm2, od_router, od_exp_gate, od_exp_up, od_exp_down):
    batch, s, d = x.shape
    depth = ev_norm1.shape[0] + od_norm1.shape[0]
    h = x.reshape(batch * s, d)
    for layer in range(depth):
        j = layer // 2
        if layer % 2 == 0:
            h = _even_layer(h, batch, ev_norm1[j], ev_w_in[j], ev_qn_a[j], ev_kn_a[j],
                            ev_qn_b[j], ev_kn_b[j], ev_sink_b[j], ev_w_out[j], ev_norm2[j],
                            ev_ffn_gate[j], ev_ffn_up[j], ev_ffn_down[j])
        else:
            h = _odd_layer(h, batch, od_norm1[j], od_w_qkv[j], od_qn[j], od_kn[j], od_rpb[j],
                           od_w_out[j], od_norm2[j], od_router[j], od_exp_gate[j],
                           od_exp_up[j], od_exp_down[j])
    return h.reshape(batch, s, d)
```

```python
import functools

import numpy as np
import jax
import jax.numpy as jnp
from jax import lax
from jax.experimental import pallas as pl
from jax.experimental.pallas import tpu as pltpu

D_MODEL = 1024
HEAD_DIM = 64
N_HEADS = D_MODEL // HEAD_DIM
A_HEADS = N_HEADS // 2
B_Q_HEADS = N_HEADS // 2
B_KV_HEADS = max(1, B_Q_HEADS // 4)
C_HEADS = N_HEADS
DIL_CONFIGS = ((128, 1), (512, 4), (2048, 16))
A_BLOCK = 128
B_WINDOW = 128
B_BLOCK = 128
GRID_W = 64
NA_ROWS = 8
NA_COLS = 16
N_EXPERTS = 8
RMS_EPS = 1e-6
NEG_INF = -1e30

LANES = 128
MXU_N = 256
VMEM_LIMIT = 56 * 1024 * 1024

ATTN_UNROLL = 4

F32 = jnp.float32
BF16 = jnp.bfloat16


def _lo_mask(shape):
    return lax.broadcasted_iota(jnp.int32, shape, len(shape) - 1) < HEAD_DIM


def _rms_rows(x, g):
    ms = jnp.mean(x * x, axis=-1, keepdims=True)
    return x * lax.rsqrt(ms + RMS_EPS) * g


def _norm_proj_kernel(x_ref, g_ref, w_ref, cg_ref, o_ref, *, norm_chunks):
    h = _rms_rows(x_ref[...], g_ref[...]).astype(BF16)
    n_out = o_ref.shape[-1]
    lo = _lo_mask((1, LANES))
    for j in range(n_out // MXU_N):
        y = jnp.dot(h, w_ref[:, j * MXU_N:(j + 1) * MXU_N], preferred_element_type=F32)
        for half in range(MXU_N // LANES):
            c = j * (MXU_N // LANES) + half
            yc = y[:, half * LANES:(half + 1) * LANES]
            if norm_chunks[c]:
                sq = yc * yc
                s_lo = jnp.sum(jnp.where(lo, sq, 0.0), axis=-1, keepdims=True)
                s_hi = jnp.sum(jnp.where(lo, 0.0, sq), axis=-1, keepdims=True)
                inv = jnp.where(lo, lax.rsqrt(s_lo * (1.0 / HEAD_DIM) + RMS_EPS),
                                lax.rsqrt(s_hi * (1.0 / HEAD_DIM) + RMS_EPS))
                yc = yc * inv * cg_ref[:, c * LANES:(c + 1) * LANES]
            o_ref[:, c * LANES:(c + 1) * LANES] = yc.astype(o_ref.dtype)


def _norm_proj(x, g, w, col_gain, norm_chunks, tm=512):
    t, d = x.shape
    n = w.shape[1]
    return pl.pallas_call(
        functools.partial(_norm_proj_kernel, norm_chunks=norm_chunks),
        grid=(t // tm,),
        in_specs=[
            pl.BlockSpec((tm, d), lambda i: (i, 0)),
            pl.BlockSpec((1, d), lambda i: (0, 0)),
            pl.BlockSpec((d, n), lambda i: (0, 0)),
            pl.BlockSpec((1, n), lambda i: (0, 0)),
        ],
        out_specs=pl.BlockSpec((tm, n), lambda i: (i, 0)),
        out_shape=jax.ShapeDtypeStruct((t, n), BF16),
        compiler_params=pltpu.CompilerParams(
            dimension_semantics=("arbitrary",), vmem_limit_bytes=VMEM_LIMIT),
        name="norm_proj",
    )(x, g.reshape(1, d), w, col_gain.reshape(1, n))


def _band_kernel(par_ref, q_ref, k_ref, v_ref, *refs, seq, blk, radius, pos_scale,
                 has_sink, want_lse):
    if want_lse:
        o_ref, lse_ref, bias_scr = refs
    else:
        o_ref, bias_scr = refs
        lse_ref = None
    span = blk + 2 * radius
    c = pl.program_id(2)
    group = min(ATTN_UNROLL, seq // blk)
    row_lo = lax.broadcasted_iota(jnp.int32, (2 * blk, 1), 0) < blk
    slope_col = jnp.where(row_lo, par_ref[0, 2 * c], par_ref[0, 2 * c + 1])
    sink_col = jnp.where(row_lo, par_ref[1, 2 * c], par_ref[1, 2 * c + 1])

    iq = lax.broadcasted_iota(jnp.int32, (2 * blk, span), 0) % blk
    ik = lax.broadcasted_iota(jnp.int32, (2 * blk, span), 1)
    for var in range(3):
        dist = jnp.abs(iq + var * radius - ik)
        dist_f = (dist * pos_scale).astype(F32)
        bias_scr[var] = jnp.where(dist <= radius, -(slope_col * dist_f), NEG_INF)

    lo_q = _lo_mask((1, LANES))
    lo = _lo_mask((blk, LANES))

    def body(g, carry):
        starts, variants, values, scores = [], [], [], []
        for u in range(group):
            a = pl.multiple_of((g * group + u) * blk, blk)
            ks = pl.multiple_of(jnp.clip(a - radius, 0, seq - span), radius)
            q = q_ref[pl.ds(a, blk), :]
            zero = jnp.zeros_like(q)
            qm = jnp.concatenate([jnp.where(lo_q, q, zero), jnp.where(lo_q, zero, q)], axis=0)
            k = k_ref[pl.ds(ks, span), :]
            scores.append(lax.dot_general(qm, k, (((1,), (1,)), ((), ())),
                                          preferred_element_type=F32))
            starts.append(a)
            variants.append((a - ks) // radius)
            values.append(v_ref[pl.ds(ks, span), :])
        probs, dens, maxes = [], [], []
        for u in range(group):
            s = scores[u] + bias_scr[variants[u]]
            m = jnp.max(s, axis=-1, keepdims=True)
            if has_sink:
                m = jnp.maximum(m, sink_col)
            p = jnp.exp(s - m)
            l = jnp.sum(p, axis=-1, keepdims=True)
            if has_sink:
                l = l + jnp.exp(sink_col - m)
            probs.append(p.astype(BF16))
            dens.append(l)
            maxes.append(m)
        for u in range(group):
            o = jnp.dot(probs[u], values[u], preferred_element_type=F32) * (1.0 / dens[u])
            o_ref[pl.ds(starts[u], blk), :] = jnp.where(lo, o[:blk], o[blk:]).astype(o_ref.dtype)
            if want_lse:
                lse = jnp.broadcast_to(maxes[u] + jnp.log(dens[u]), (2 * blk, LANES))
                lse_ref[pl.ds(starts[u], blk), :] = jnp.where(lo, lse[:blk], lse[blk:])
        return carry

    lax.fori_loop(0, seq // (blk * group), body, 0)


def _band_attention(qkv, params, *, n_cols, q_blk0, k_blk0, v_blk0, kv_per_pair, n_pairs,
                    dil, radius, blk, has_sink, want_lse):
    b, s, _ = qkv.shape
    seq = s // dil
    cb = n_cols // LANES
    view = qkv.reshape(b, seq, dil * n_cols)
    out_cols = n_pairs * LANES
    span = blk + 2 * radius

    def in_map(blk0, per_pair):
        if per_pair:
            return lambda bi, p, c: (bi, 0, p * cb + blk0 + c)
        return lambda bi, p, c: (bi, 0, p * cb + blk0)

    out_map = lambda bi, p, c: (bi, 0, p * n_pairs + c)
    out_shape = [jax.ShapeDtypeStruct((b, seq, dil * out_cols), BF16)]
    out_specs = [pl.BlockSpec((None, seq, LANES), out_map)]
    if want_lse:
        out_shape.append(jax.ShapeDtypeStruct((b, seq, dil * out_cols), F32))
        out_specs.append(pl.BlockSpec((None, seq, LANES), out_map))
    res = pl.pallas_call(
        functools.partial(_band_kernel, seq=seq, blk=blk, radius=radius, pos_scale=dil,
                          has_sink=has_sink, want_lse=want_lse),
        grid=(b, dil, n_pairs),
        in_specs=[
            pl.BlockSpec(memory_space=pltpu.SMEM),
            pl.BlockSpec((None, seq, LANES), in_map(q_blk0, True)),
            pl.BlockSpec((None, seq, LANES), in_map(k_blk0, kv_per_pair)),
            pl.BlockSpec((None, seq, LANES), in_map(v_blk0, kv_per_pair)),
        ],
        out_specs=out_specs,
        out_shape=out_shape,
        scratch_shapes=[pltpu.VMEM((3, 2 * blk, span), F32)],
        compiler_params=pltpu.CompilerParams(
            dimension_semantics=("arbitrary", "arbitrary", "arbitrary"),
            vmem_limit_bytes=VMEM_LIMIT),
        name=f"band_attn_d{dil}" + ("_sink" if has_sink else ""),
    )(params, view, view, view)
    return [r.reshape(b, s, out_cols) for r in res]


def _na_kernel(q_ref, k_ref, v_ref, bias_ref, o_ref, *, rows):
    kh = NA_ROWS
    group = ATTN_UNROLL
    lo_q = _lo_mask((1, LANES))
    lo = _lo_mask((GRID_W, LANES))

    def body(g, carry):
        starts, offs, values, scores = [], [], [], []
        for u in range(group):
            i = g * group + u
            rs = jnp.clip(i - kh // 2, 0, rows - kh)
            qs = pl.multiple_of(i * GRID_W, GRID_W)
            ks = pl.multiple_of(rs * GRID_W, GRID_W)
            q = q_ref[pl.ds(qs, GRID_W), :]
            zero = jnp.zeros_like(q)
            qm = jnp.concatenate([jnp.where(lo_q, q, zero), jnp.where(lo_q, zero, q)], axis=0)
            k = k_ref[pl.ds(ks, kh * GRID_W), :]
            scores.append(lax.dot_general(qm, k, (((1,), (1,)), ((), ())),
                                          preferred_element_type=F32))
            starts.append(qs)
            offs.append(rs - i + (NA_ROWS - 1))
            values.append(v_ref[pl.ds(ks, kh * GRID_W), :])
        probs, dens = [], []
        for u in range(group):
            s = scores[u] + bias_ref[offs[u]]
            m = jnp.max(s, axis=-1, keepdims=True)
            p = jnp.exp(s - m)
            dens.append(jnp.sum(p, axis=-1, keepdims=True))
            probs.append(p.astype(BF16))
        for u in range(group):
            o = jnp.dot(probs[u], values[u], preferred_element_type=F32) * (1.0 / dens[u])
            o_ref[pl.ds(starts[u], GRID_W), :] = jnp.where(
                lo, o[:GRID_W], o[GRID_W:]).astype(o_ref.dtype)
        return carry

    lax.fori_loop(0, rows // group, body, 0)


def _na_bias_table(rpb):
    qc = np.arange(GRID_W)[:, None]
    kc = np.arange(GRID_W)[None, :]
    qstart = np.clip(qc - NA_COLS // 2, 0, GRID_W - NA_COLS)
    valid = (kc >= qstart) & (kc < qstart + NA_COLS)
    coff = np.clip(kc - qc, -(NA_COLS - 1), NA_COLS - 1) + NA_COLS - 1
    row_idx = np.arange(NA_ROWS)[:, None] + np.arange(NA_ROWS)[None, :]
    t = rpb.astype(F32)[:, row_idx]
    t = t[:, :, :, coff]
    t = jnp.where(jnp.asarray(valid), t, NEG_INF)
    t = t.transpose(0, 1, 3, 2, 4).reshape(C_HEADS // 2, 2, NA_ROWS, GRID_W, NA_ROWS * GRID_W)
    return t.transpose(0, 2, 1, 3, 4).reshape(C_HEADS // 2, NA_ROWS, 2 * GRID_W, NA_ROWS * GRID_W)


def _na_attention(qkv, bias):
    b, s, n_cols = qkv.shape
    n_pairs = C_HEADS // 2
    rows = s // GRID_W
    return pl.pallas_call(
        functools.partial(_na_kernel, rows=rows),
        grid=(b, n_pairs),
        in_specs=[
            pl.BlockSpec((None, s, LANES), lambda bi, c: (bi, 0, c)),
            pl.BlockSpec((None, s, LANES), lambda bi, c: (bi, 0, n_pairs + c)),
            pl.BlockSpec((None, s, LANES), lambda bi, c: (bi, 0, 2 * n_pairs + c)),
            pl.BlockSpec((None, NA_ROWS, 2 * GRID_W, NA_ROWS * GRID_W),
                         lambda bi, c: (c, 0, 0, 0)),
        ],
        out_specs=pl.BlockSpec((None, s, LANES), lambda bi, c: (bi, 0, c)),
        out_shape=jax.ShapeDtypeStruct((b, s, n_pairs * LANES), BF16),
        compiler_params=pltpu.CompilerParams(
            dimension_semantics=("arbitrary", "arbitrary"), vmem_limit_bytes=VMEM_LIMIT),
        name="na_attn",
    )(qkv, qkv, qkv, bias)


def _even_out_kernel(x_ref, o1_ref, o2_ref, o3_ref, l1_ref, l2_ref, l3_ref, ob_ref,
                     wa_ref, wb_ref, y_ref):
    l1, l2, l3 = l1_ref[...], l2_ref[...], l3_ref[...]
    m = jnp.maximum(jnp.maximum(l1, l2), l3)
    e1, e2, e3 = jnp.exp(l1 - m), jnp.exp(l2 - m), jnp.exp(l3 - m)
    inv = 1.0 / (e1 + e2 + e3)
    oa = ((e1 * inv) * o1_ref[...].astype(F32) + (e2 * inv) * o2_ref[...].astype(F32)
          + (e3 * inv) * o3_ref[...].astype(F32))
    y = jnp.dot(oa.astype(BF16), wa_ref[...], preferred_element_type=F32)
    y = y + jnp.dot(ob_ref[...], wb_ref[...], preferred_element_type=F32)
    y_ref[...] = x_ref[...] + y


def _even_out(x, oas, lses, ob, wa, wb, tm=512):
    t, d = x.shape
    ha = wa.shape[0]
    row = lambda w: pl.BlockSpec((tm, w), lambda i: (i, 0))
    full = lambda a: pl.BlockSpec(a.shape, lambda i: (0, 0))
    return pl.pallas_call(
        _even_out_kernel,
        grid=(t // tm,),
        in_specs=[row(d)] + [row(ha)] * 3 + [row(ha)] * 3 + [row(ob.shape[1]), full(wa), full(wb)],
        out_specs=row(d),
        out_shape=jax.ShapeDtypeStruct((t, d), F32),
        compiler_params=pltpu.CompilerParams(
            dimension_semantics=("arbitrary",), vmem_limit_bytes=VMEM_LIMIT),
        name="even_out_proj",
    )(x, *oas, *lses, ob, wa, wb)


def _out_proj_kernel(x_ref, o_ref, w_ref, y_ref):
    y_ref[...] = x_ref[...] + jnp.dot(o_ref[...], w_ref[...], preferred_element_type=F32)


def _out_proj(x, o, w, tm=512):
    t, d = x.shape
    return pl.pallas_call(
        _out_proj_kernel,
        grid=(t // tm,),
        in_specs=[pl.BlockSpec((tm, d), lambda i: (i, 0)),
                  pl.BlockSpec((tm, o.shape[1]), lambda i: (i, 0)),
                  pl.BlockSpec(w.shape, lambda i: (0, 0))],
        out_specs=pl.BlockSpec((tm, d), lambda i: (i, 0)),
        out_shape=jax.ShapeDtypeStruct((t, d), F32),
        compiler_params=pltpu.CompilerParams(
            dimension_semantics=("arbitrary",), vmem_limit_bytes=VMEM_LIMIT),
        name="out_proj",
    )(x, o, w)


def _silu(g):
    return g * (1.0 / (1.0 + jnp.exp(-g)))


def _ffn_kernel(x_ref, g_ref, wg_ref, wu_ref, wd_ref, y_ref, a_scr):
    x = x_ref[...]
    h = _rms_rows(x, g_ref[...]).astype(BF16)
    f = wg_ref.shape[1]
    for j in range(f // MXU_N):
        sl = slice(j * MXU_N, (j + 1) * MXU_N)
        gate = jnp.dot(h, wg_ref[:, sl], preferred_element_type=F32)
        up = jnp.dot(h, wu_ref[:, sl], preferred_element_type=F32)
        a_scr[:, sl] = (_silu(gate) * up).astype(BF16)
    y_ref[...] = x + jnp.dot(a_scr[...], wd_ref[...], preferred_element_type=F32)


def _ffn(x, g, wg, wu, wd, tm=512):
    t, d = x.shape
    f = wg.shape[1]
    const = lambda a: pl.BlockSpec(a.shape, lambda i: (0, 0), pipeline_mode=pl.Buffered(1))
    return pl.pallas_call(
        _ffn_kernel,
        grid=(t // tm,),
        in_specs=[pl.BlockSpec((tm, d), lambda i: (i, 0)),
                  pl.BlockSpec((1, d), lambda i: (0, 0)),
                  const(wg), const(wu), const(wd)],
        out_specs=pl.BlockSpec((tm, d), lambda i: (i, 0)),
        out_shape=jax.ShapeDtypeStruct((t, d), F32),
        scratch_shapes=[pltpu.VMEM((tm, f), BF16)],
        compiler_params=pltpu.CompilerParams(
            dimension_semantics=("arbitrary",), vmem_limit_bytes=VMEM_LIMIT),
        name="dense_swiglu",
    )(x, g.reshape(1, d), wg, wu, wd)


TOP_K = 2
MOE_TB = 512
MOE_TM = 512
MOE_RC = 256
KEY_STRIDE = 65536.0


def _router_kernel(x_ref, g_ref, wr_ref, h_ref, info_ref, info_t_ref, cnt_ref, carry_scr):
    i = pl.program_id(0)
    tb = x_ref.shape[0]
    lane = lax.broadcasted_iota(jnp.int32, (1, LANES), 1)

    @pl.when(i == 0)
    def _():
        carry_scr[...] = jnp.zeros_like(carry_scr)

    h = _rms_rows(x_ref[...], g_ref[...])
    h_ref[...] = h.astype(BF16)
    logits = jnp.dot(h, wr_ref[...], preferred_element_type=F32,
                     precision=lax.Precision.HIGHEST)
    logits = jnp.where(lane < N_EXPERTS, logits, NEG_INF)
    v1 = jnp.max(logits, axis=-1, keepdims=True)
    i1 = jnp.min(jnp.where(logits == v1, lane, LANES), axis=-1, keepdims=True)
    rest = jnp.where(lane == i1, NEG_INF, logits)
    v2 = jnp.max(rest, axis=-1, keepdims=True)
    i2 = jnp.min(jnp.where(rest == v2, lane, LANES), axis=-1, keepdims=True)
    e2 = jnp.exp(v2 - v1)
    g1 = 1.0 / (1.0 + e2)
    g2 = e2 * g1

    pick1 = lane == i1
    pick2 = lane == i2
    picked = jnp.where(pick1, 1.0, jnp.where(pick2, 1.0, 0.0))
    r = lax.broadcasted_iota(jnp.int32, (tb, tb), 0)
    cidx = lax.broadcasted_iota(jnp.int32, (tb, tb), 1)
    ltri = jnp.where(cidx < r, 1.0, 0.0).astype(BF16)
    before = jnp.dot(ltri, picked.astype(BF16), preferred_element_type=F32) + carry_scr[...]
    rank1 = jnp.sum(jnp.where(pick1, before, 0.0), axis=-1, keepdims=True)
    rank2 = jnp.sum(jnp.where(pick2, before, 0.0), axis=-1, keepdims=True)
    key1 = i1.astype(F32) * KEY_STRIDE + rank1
    key2 = i2.astype(F32) * KEY_STRIDE + rank2
    info = jnp.where(lane == 0, key1, jnp.where(lane == 1, key2,
                     jnp.where(lane == 2, g1, jnp.where(lane == 3, g2, 0.0))))
    info_ref[...] = info
    info_t_ref[...] = info.T[:8, :]
    counts = jnp.sum(picked, axis=0, keepdims=True)
    cnt_ref[...] = jnp.broadcast_to(counts, cnt_ref.shape)
    carry_scr[...] += counts


def _router(x, g, wr):
    t, d = x.shape
    nb = t // MOE_TB
    return pl.pallas_call(
        _router_kernel,
        grid=(nb,),
        in_specs=[pl.BlockSpec((MOE_TB, d), lambda i: (i, 0)),
                  pl.BlockSpec((1, d), lambda i: (0, 0)),
                  pl.BlockSpec(wr.shape, lambda i: (0, 0))],
        out_specs=[pl.BlockSpec((MOE_TB, d), lambda i: (i, 0)),
                   pl.BlockSpec((MOE_TB, LANES), lambda i: (i, 0)),
                   pl.BlockSpec((8, MOE_TB), lambda i: (0, i)),
                   pl.BlockSpec((None, 8, LANES), lambda i: (i, 0, 0))],
        out_shape=[jax.ShapeDtypeStruct((t, d), BF16),
                   jax.ShapeDtypeStruct((t, LANES), F32),
                   jax.ShapeDtypeStruct((8, t), F32),
                   jax.ShapeDtypeStruct((nb, 8, LANES), F32)],
        scratch_shapes=[pltpu.VMEM((1, LANES), F32)],
        compiler_params=pltpu.CompilerParams(
            dimension_semantics=("arbitrary",), vmem_limit_bytes=VMEM_LIMIT),
        name="moe_router",
    )(x, g.reshape(1, d), wr)


def _moe_plan(block_counts, n_tokens):
    nb = block_counts.shape[0]
    n_tiles = (TOP_K * n_tokens + N_EXPERTS * (MOE_TM - 1)) // MOE_TM
    n_chunks = n_tiles * MOE_TM // MOE_RC
    n_pairs = ((n_chunks + N_EXPERTS * nb + 7) // 8) * 8
    tc = block_counts.astype(jnp.int32)
    cb = jnp.concatenate([jnp.zeros((1, N_EXPERTS), jnp.int32), jnp.cumsum(tc, axis=0)])
    counts = cb[-1]
    padded = ((counts + MOE_TM - 1) // MOE_TM) * MOE_TM
    ends = jnp.cumsum(padded)
    starts = ends - padded
    total = ends[-1]

    tiles = jnp.arange(n_tiles, dtype=jnp.int32)
    n_valid_tiles = total // MOE_TM
    t_valid = (tiles < n_valid_tiles).astype(jnp.int32)
    t_src = jnp.minimum(tiles, n_valid_tiles - 1)
    t_exp = jnp.minimum(jnp.sum((t_src * MOE_TM)[:, None] >= ends[None, :], axis=1),
                        N_EXPERTS - 1).astype(jnp.int32)

    r0 = jnp.arange(n_chunks, dtype=jnp.int32) * MOE_RC
    c_exp = jnp.minimum(jnp.sum(r0[:, None] >= ends[None, :], axis=1), N_EXPERTS - 1)
    c_live = r0 < total
    lo = r0 - starts[c_exp]
    base = (c_exp * int(KEY_STRIDE) + lo).astype(jnp.int32)
    cbe = cb[:, c_exp]
    real = c_live[None, :] & (cbe[:-1] < (lo + MOE_RC)[None, :]) & (cbe[1:] > lo[None, :])
    empty = c_live & jnp.logical_not(jnp.any(real, axis=0))
    ov = real | (empty[None, :] & (jnp.arange(nb)[:, None] == 0))
    n_act = jnp.sum(ov).astype(jnp.int32)
    pidx = jnp.arange(n_pairs, dtype=jnp.int32)
    act = pidx < n_act

    def pair_list(flat, inner):
        idx = jnp.nonzero(flat, size=n_pairs, fill_value=0)[0].astype(jnp.int32)
        idx = jnp.where(act, idx, idx[n_act - 1])
        return idx // inner, idx % inner

    def flags(major):
        first = act & ((pidx == 0) | (major != jnp.roll(major, 1)))
        last = act & ((pidx == n_act - 1) | (major != jnp.roll(major, -1)))
        return (act.astype(jnp.int32) + 2 * first.astype(jnp.int32)
                + 4 * last.astype(jnp.int32))

    g_c, g_b = pair_list(ov.T.reshape(-1), nb)
    c_b, c_c = pair_list(ov.reshape(-1), n_chunks)
    return dict(n_tiles=n_tiles, n_chunks=n_chunks, n_pairs=n_pairs,
                t_valid=t_valid, t_src=t_src, t_exp=t_exp, base=base,
                g_c=g_c, g_b=g_b, g_flags=flags(g_c), c_b=c_b, c_c=c_c, c_flags=flags(c_b))


def _gather_kernel(pc_ref, pb_ref, fl_ref, base_ref, info_t_ref, h_ref, xs_ref, gs_ref,
                   acc_scr, gsum_scr):
    p = pl.program_id(0)
    fl = fl_ref[p]
    tb = h_ref.shape[0]

    @pl.when(fl % 2 == 1)
    def _():
        base = base_ref[pc_ref[p]].astype(F32)
        row = lax.broadcasted_iota(jnp.int32, (MOE_RC, tb), 0).astype(F32)
        m1 = (info_t_ref[0:1, :] - base) == row
        m2 = (info_t_ref[1:2, :] - base) == row
        sel = jnp.where(m1, 1.0, jnp.where(m2, 1.0, 0.0)).astype(BF16)
        rows = jnp.dot(sel, h_ref[...], preferred_element_type=F32)
        gate = jnp.sum(jnp.where(m1, info_t_ref[2:3, :], jnp.where(m2, info_t_ref[3:4, :], 0.0)),
                       axis=-1, keepdims=True)

        @pl.when((fl // 2) % 2 == 1)
        def _():
            acc_scr[...] = rows
            gsum_scr[...] = gate

        @pl.when((fl // 2) % 2 == 0)
        def _():
            acc_scr[...] += rows
            gsum_scr[...] += gate

        @pl.when((fl // 4) % 2 == 1)
        def _():
            xs_ref[...] = acc_scr[...].astype(xs_ref.dtype)
            gs_ref[...] = jnp.broadcast_to(gsum_scr[...], gs_ref.shape)


def _moe_gather(plan, info_t, h):
    t, d = h.shape
    rows = plan["n_chunks"] * MOE_RC
    grid_spec = pltpu.PrefetchScalarGridSpec(
        num_scalar_prefetch=4,
        grid=(plan["n_pairs"],),
        in_specs=[pl.BlockSpec((8, MOE_TB), lambda p, pc, pb, fl, ba: (0, pb[p])),
                  pl.BlockSpec((MOE_TB, d), lambda p, pc, pb, fl, ba: (pb[p], 0))],
        out_specs=[pl.BlockSpec((MOE_RC, d), lambda p, pc, pb, fl, ba: (pc[p], 0)),
                   pl.BlockSpec((MOE_RC, LANES), lambda p, pc, pb, fl, ba: (pc[p], 0))],
        scratch_shapes=[pltpu.VMEM((MOE_RC, d), F32), pltpu.VMEM((MOE_RC, 1), F32)],
    )
    return pl.pallas_call(
        _gather_kernel,
        grid_spec=grid_spec,
        out_shape=[jax.ShapeDtypeStruct((rows, d), BF16),
                   jax.ShapeDtypeStruct((rows, LANES), F32)],
        compiler_params=pltpu.CompilerParams(
            dimension_semantics=("arbitrary",), vmem_limit_bytes=VMEM_LIMIT),
        name="moe_gather",
    )(plan["g_c"], plan["g_b"], plan["g_flags"], plan["base"], info_t, h)


def _experts_kernel(src_ref, exp_ref, val_ref, xs_ref, gs_ref, wg_ref, wu_ref, wd_ref, ys_ref,
                    acc_scr):
    i = pl.program_id(0)
    j = pl.program_id(1)

    @pl.when(val_ref[i] == 1)
    def _():
        x = xs_ref[...]
        gate = jnp.dot(x, wg_ref[...], preferred_element_type=F32)
        up = jnp.dot(x, wu_ref[...], preferred_element_type=F32)
        a = (_silu(gate) * up).astype(BF16)
        y = jnp.dot(a, wd_ref[...], preferred_element_type=F32)

        @pl.when(j == 0)
        def _():
            acc_scr[...] = y

        @pl.when(j > 0)
        def _():
            acc_scr[...] += y

        @pl.when(j == pl.num_programs(1) - 1)
        def _():
            ys_ref[...] = (acc_scr[...] * gs_ref[:, 0:1]).astype(ys_ref.dtype)


def _moe_experts(plan, xs, gs, wg, wu, wd, tf):
    rows, d = xs.shape
    f = wg.shape[2]
    nf = f // tf

    def wcol(i, j, src, ex, val):
        return (ex[i], 0, jnp.where(val[i] == 1, j, nf - 1))

    def wrow(i, j, src, ex, val):
        return (ex[i], jnp.where(val[i] == 1, j, nf - 1), 0)

    tile = lambda i, j, src, ex, val: (src[i], 0)
    grid_spec = pltpu.PrefetchScalarGridSpec(
        num_scalar_prefetch=3,
        grid=(plan["n_tiles"], nf),
        in_specs=[pl.BlockSpec((MOE_TM, d), tile),
                  pl.BlockSpec((MOE_TM, LANES), tile),
                  pl.BlockSpec((None, d, tf), wcol),
                  pl.BlockSpec((None, d, tf), wcol),
                  pl.BlockSpec((None, tf, d), wrow)],
        out_specs=pl.BlockSpec((MOE_TM, d), tile),
        scratch_shapes=[pltpu.VMEM((MOE_TM, d), F32)],
    )
    return pl.pallas_call(
        _experts_kernel,
        grid_spec=grid_spec,
        out_shape=jax.ShapeDtypeStruct((rows, d), BF16),
        compiler_params=pltpu.CompilerParams(
            dimension_semantics=("arbitrary", "arbitrary"), vmem_limit_bytes=VMEM_LIMIT),
        name="moe_experts",
    )(plan["t_src"], plan["t_exp"], plan["t_valid"], xs, gs, wg, wu, wd)


def _combine_kernel(pb_ref, pc_ref, fl_ref, base_ref, info_ref, ys_ref, x_ref, o_ref):
    p = pl.program_id(0)
    fl = fl_ref[p]
    tb = x_ref.shape[0]

    @pl.when(fl % 2 == 1)
    def _():
        base = base_ref[pc_ref[p]].astype(F32)
        lane = lax.broadcasted_iota(jnp.int32, (1, LANES), 1)
        info = info_ref[...]
        k1 = jnp.sum(jnp.where(lane == 0, info, 0.0), axis=-1, keepdims=True) - base
        k2 = jnp.sum(jnp.where(lane == 1, info, 0.0), axis=-1, keepdims=True) - base
        col = lax.broadcasted_iota(jnp.int32, (tb, MOE_RC), 1).astype(F32)
        sel = jnp.where(k1 == col, 1.0, jnp.where(k2 == col, 1.0, 0.0)).astype(BF16)
        rows = jnp.dot(sel, ys_ref[...], preferred_element_type=F32)

        @pl.when((fl // 2) % 2 == 1)
        def _():
            o_ref[...] = x_ref[...] + rows

        @pl.when((fl // 2) % 2 == 0)
        def _():
            o_ref[...] += rows


def _moe_combine(plan, info, ys, x):
    t, d = x.shape
    grid_spec = pltpu.PrefetchScalarGridSpec(
        num_scalar_prefetch=4,
        grid=(plan["n_pairs"],),
        in_specs=[pl.BlockSpec((MOE_TB, LANES), lambda p, pb, pc, fl, ba: (pb[p], 0)),
                  pl.BlockSpec((MOE_RC, d), lambda p, pb, pc, fl, ba: (pc[p], 0)),
                  pl.BlockSpec((MOE_TB, d), lambda p, pb, pc, fl, ba: (pb[p], 0))],
        out_specs=pl.BlockSpec((MOE_TB, d), lambda p, pb, pc, fl, ba: (pb[p], 0)),
    )
    return pl.pallas_call(
        _combine_kernel,
        grid_spec=grid_spec,
        out_shape=jax.ShapeDtypeStruct((t, d), F32),
        compiler_params=pltpu.CompilerParams(
            dimension_semantics=("arbitrary",), vmem_limit_bytes=VMEM_LIMIT),
        name="moe_combine",
    )(plan["c_b"], plan["c_c"], plan["c_flags"], plan["base"], info, ys, x)


def _moe(x, g, wr, wg, wu, wd, tf=1792):
    t, _ = x.shape
    h, info, info_t, block_counts = _router(x, g, wr)
    plan = _moe_plan(block_counts[:, 0, :N_EXPERTS], t)
    xs, gs = _moe_gather(plan, info_t, h)
    ys = _moe_experts(plan, xs, gs, wg, wu, wd, tf)
    return _moe_combine(plan, info, ys, x)


def _alibi_slopes():
    s = np.exp2(-8.0 * np.arange(1, N_HEADS + 1) / N_HEADS).astype(np.float32)
    return s[0::2][:A_HEADS], s[1::2][:B_Q_HEADS]


_B_HEAD_ORDER = np.array([0, 4, 1, 5, 2, 6, 3, 7])


def _even_layer(x2d, batch, norm1, w_in, qn_a, kn_a, qn_b, kn_b, sink_b, w_out,
                norm2, w_gate, w_up, w_down):
    t, d = x2d.shape
    s = t // batch
    wa = A_HEADS * HEAD_DIM
    wq_b = B_Q_HEADS * HEAD_DIM
    wkv_b = B_KV_HEADS * HEAD_DIM
    qb0 = 3 * wa
    q_scale = HEAD_DIM ** -0.5

    w_qb = w_in[:, qb0:qb0 + wq_b].reshape(d, B_Q_HEADS, HEAD_DIM)[:, _B_HEAD_ORDER]
    w_in_p = jnp.concatenate(
        [w_in[:, :qb0], w_qb.reshape(d, wq_b), w_in[:, qb0 + wq_b:]], axis=1).astype(BF16)
    ones = jnp.ones((HEAD_DIM,), F32)
    col_gain = jnp.concatenate([
        jnp.tile(qn_a * q_scale, A_HEADS), jnp.tile(kn_a, A_HEADS), jnp.tile(ones, A_HEADS),
        jnp.tile(qn_b * q_scale, B_Q_HEADS), jnp.tile(kn_b, B_KV_HEADS),
        jnp.tile(ones, B_KV_HEADS)])
    cpl = wa // LANES
    norm_chunks = ((True,) * (2 * cpl) + (False,) * cpl + (True,) * (wq_b // LANES)
                   + (True,) * (wkv_b // LANES) + (False,) * (wkv_b // LANES))
    qkv = _norm_proj(x2d, norm1, w_in_p, col_gain, norm_chunks)
    n_cols = qkv.shape[1]
    qkv3 = qkv.reshape(batch, s, n_cols)

    slopes_a, slopes_b = _alibi_slopes()
    par_a = jnp.stack([jnp.asarray(slopes_a), jnp.zeros((A_HEADS,), F32)])
    oas, lses = [], []
    for window, dil in DIL_CONFIGS:
        o, lse = _band_attention(
            qkv3, par_a, n_cols=n_cols, q_blk0=0, k_blk0=cpl, v_blk0=2 * cpl,
            kv_per_pair=True, n_pairs=cpl, dil=dil, radius=window // (2 * dil),
            blk=A_BLOCK, has_sink=False, want_lse=True)
        oas.append(o.reshape(t, wa))
        lses.append(lse.reshape(t, wa))
    par_b = jnp.stack([jnp.asarray(slopes_b[_B_HEAD_ORDER]),
                       sink_b.astype(F32)[_B_HEAD_ORDER]])
    (ob,) = _band_attention(
        qkv3, par_b, n_cols=n_cols, q_blk0=3 * cpl, k_blk0=3 * cpl + wq_b // LANES,
        v_blk0=3 * cpl + wq_b // LANES + wkv_b // LANES, kv_per_pair=False,
        n_pairs=wq_b // LANES, dil=1, radius=B_WINDOW, blk=B_BLOCK, has_sink=True,
        want_lse=False)
    w_out_a = w_out[:wa].astype(BF16)
    w_out_b = w_out[wa:].reshape(B_Q_HEADS, HEAD_DIM, d)[_B_HEAD_ORDER].reshape(wq_b, d)
    x1 = _even_out(x2d, oas, lses, ob.reshape(t, wq_b), w_out_a, w_out_b.astype(BF16))
    return _ffn(x1, norm2, w_gate.astype(BF16), w_up.astype(BF16), w_down.astype(BF16))


def _odd_layer(x2d, batch, norm1, w_qkv, qn, kn, rpb, w_out, norm2, w_router,
               w_gate, w_up, w_down):
    t, d = x2d.shape
    s = t // batch
    wc = C_HEADS * HEAD_DIM
    ones = jnp.ones((HEAD_DIM,), F32)
    col_gain = jnp.concatenate([jnp.tile(qn * HEAD_DIM ** -0.5, C_HEADS),
                                jnp.tile(kn, C_HEADS), jnp.tile(ones, C_HEADS)])
    cpl = wc // LANES
    norm_chunks = (True,) * (2 * cpl) + (False,) * cpl
    qkv = _norm_proj(x2d, norm1, w_qkv.astype(BF16), col_gain, norm_chunks)
    o = _na_attention(qkv.reshape(batch, s, 3 * wc), _na_bias_table(rpb))
    x1 = _out_proj(x2d, o.reshape(t, wc), w_out.astype(BF16))
    wr = jnp.zeros((d, LANES), F32).at[:, :N_EXPERTS].set(w_router.astype(F32))
    return _moe(x1, norm2, wr, w_gate.astype(BF16), w_up.astype(BF16), w_down.astype(BF16))


def kernel(x, ev_norm1, ev_w_in, ev_qn_a, ev_kn_a, ev_qn_b, ev_kn_b, ev_sink_b, ev_w_out, ev_norm2, ev_ffn_gate, ev_ffn_up, ev_ffn_down, od_norm1, od_w_qkv, od_qn, od_kn, od_rpb, od_w_out, od_norm2, od_router, od_exp_gate, od_exp_up, od_exp_down):
    batch, s, d = x.shape
    depth = ev_norm1.shape[0] + od_norm1.shape[0]
    h = x.reshape(batch * s, d)
    for layer in range(depth):
        j = layer // 2
        if layer % 2 == 0:
            h = _even_layer(h, batch, ev_norm1[j], ev_w_in[j], ev_qn_a[j], ev_kn_a[j],
                            ev_qn_b[j], ev_kn_b[j], ev_sink_b[j], ev_w_out[j], ev_norm2[j],
                            ev_ffn_gate[j], ev_ffn_up[j], ev_ffn_down[j])
        else:
            h = _odd_layer(h, batch, od_norm1[j], od_w_qkv[j], od_qn[j], od_kn[j], od_rpb[j],
                           od_w_out[j], od_norm2[j], od_router[j], od_exp_gate[j],
                           od_exp_up[j], od_exp_down[j])
    return h.reshape(batch, s, d)
```

```python
import functools

import numpy as np
import jax
import jax.numpy as jnp
from jax import lax
from jax.experimental import pallas as pl
from jax.experimental.pallas import tpu as pltpu

D_MODEL = 1024
HEAD_DIM = 64
N_HEADS = D_MODEL // HEAD_DIM
A_HEADS = N_HEADS // 2
B_Q_HEADS = N_HEADS // 2
B_KV_HEADS = max(1, B_Q_HEADS // 4)
C_HEADS = N_HEADS
DIL_CONFIGS = ((128, 1), (512, 4), (2048, 16))
A_BLOCK = 128
B_WINDOW = 128
B_BLOCK = 128
GRID_W = 64
NA_ROWS = 8
NA_COLS = 16
N_EXPERTS = 8
RMS_EPS = 1e-6
NEG_INF = -1e30

LANES = 128
MXU_N = 256
VMEM_LIMIT = 56 * 1024 * 1024

ATTN_UNROLL = 4

F32 = jnp.float32
BF16 = jnp.bfloat16


def _lo_mask(shape):
    return lax.broadcasted_iota(jnp.int32, shape, len(shape) - 1) < HEAD_DIM


def _rms_rows(x, g):
    ms = jnp.mean(x * x, axis=-1, keepdims=True)
    return x * lax.rsqrt(ms + RMS_EPS) * g


def _norm_proj_kernel(x_ref, g_ref, w_ref, cg_ref, o_ref, *, norm_chunks):
    h = _rms_rows(x_ref[...], g_ref[...]).astype(BF16)
    n_out = o_ref.shape[-1]
    lo = _lo_mask((1, LANES))
    for j in range(n_out // MXU_N):
        y = jnp.dot(h, w_ref[:, j * MXU_N:(j + 1) * MXU_N], preferred_element_type=F32)
        for half in range(MXU_N // LANES):
            c = j * (MXU_N // LANES) + half
            yc = y[:, half * LANES:(half + 1) * LANES]
            if norm_chunks[c]:
                sq = yc * yc
                s_lo = jnp.sum(jnp.where(lo, sq, 0.0), axis=-1, keepdims=True)
                s_hi = jnp.sum(jnp.where(lo, 0.0, sq), axis=-1, keepdims=True)
                inv = jnp.where(lo, lax.rsqrt(s_lo * (1.0 / HEAD_DIM) + RMS_EPS),
                                lax.rsqrt(s_hi * (1.0 / HEAD_DIM) + RMS_EPS))
                yc = yc * inv * cg_ref[:, c * LANES:(c + 1) * LANES]
            o_ref[:, c * LANES:(c + 1) * LANES] = yc.astype(o_ref.dtype)


def _norm_proj(x, g, w, col_gain, norm_chunks, tm=512):
    t, d = x.shape
    n = w.shape[1]
    return pl.pallas_call(
        functools.partial(_norm_proj_kernel, norm_chunks=norm_chunks),
        grid=(t // tm,),
        in_specs=[
            pl.BlockSpec((tm, d), lambda i: (i, 0)),
            pl.BlockSpec((1, d), lambda i: (0, 0)),
            pl.BlockSpec((d, n), lambda i: (0, 0)),
            pl.BlockSpec((1, n), lambda i: (0, 0)),
        ],
        out_specs=pl.BlockSpec((tm, n), lambda i: (i, 0)),
        out_shape=jax.ShapeDtypeStruct((t, n), BF16),
        compiler_params=pltpu.CompilerParams(
            dimension_semantics=("arbitrary",), vmem_limit_bytes=VMEM_LIMIT),
        name="norm_proj",
    )(x, g.reshape(1, d), w, col_gain.reshape(1, n))


def _window_kernel(par_ref, q_ref, k_ref, v_ref, o_ref, bias_scr, *, seq, blk, radius):
    span = blk + 2 * radius
    c = pl.program_id(1)
    group = ATTN_UNROLL
    row_lo = lax.broadcasted_iota(jnp.int32, (2 * blk, 1), 0) < blk
    slope_col = jnp.where(row_lo, par_ref[0, 2 * c], par_ref[0, 2 * c + 1])
    sink_col = jnp.where(row_lo, par_ref[1, 2 * c], par_ref[1, 2 * c + 1])

    iq = lax.broadcasted_iota(jnp.int32, (2 * blk, span), 0) % blk
    ik = lax.broadcasted_iota(jnp.int32, (2 * blk, span), 1)
    _fill_band_bias(bias_scr, iq, ik, radius, 1, slope_col)
    lo_q = _lo_mask((1, LANES))
    lo = _lo_mask((blk, LANES))

    def body(g, carry):
        starts, variants, values, scores = [], [], [], []
        for u in range(group):
            a = pl.multiple_of((g * group + u) * blk, blk)
            ks = pl.multiple_of(jnp.clip(a - radius, 0, seq - span), radius)
            q = q_ref[pl.ds(a, blk), :]
            zero = jnp.zeros_like(q)
            qm = jnp.concatenate([jnp.where(lo_q, q, zero), jnp.where(lo_q, zero, q)], axis=0)
            k = k_ref[pl.ds(ks, span), :]
            scores.append(lax.dot_general(qm, k, (((1,), (1,)), ((), ())),
                                          preferred_element_type=F32))
            starts.append(a)
            variants.append((a - ks) // radius)
            values.append(v_ref[pl.ds(ks, span), :])
        probs, dens = [], []
        for u in range(group):
            s = scores[u] + bias_scr[variants[u]]
            m = jnp.maximum(jnp.max(s, axis=-1, keepdims=True), sink_col)
            p = jnp.exp(s - m)
            dens.append(jnp.sum(p, axis=-1, keepdims=True) + jnp.exp(sink_col - m))
            probs.append(p.astype(BF16))
        for u in range(group):
            o = jnp.dot(probs[u], values[u], preferred_element_type=F32) * (1.0 / dens[u])
            o_ref[pl.ds(starts[u], blk), :] = jnp.where(lo, o[:blk], o[blk:]).astype(o_ref.dtype)
        return carry

    lax.fori_loop(0, seq // (blk * group), body, 0)


def _window_attention(qkv, params, *, q_blk0, k_blk, v_blk, n_pairs, radius, blk):
    b, s, _ = qkv.shape
    span = blk + 2 * radius
    return pl.pallas_call(
        functools.partial(_window_kernel, seq=s, blk=blk, radius=radius),
        grid=(b, n_pairs),
        in_specs=[
            pl.BlockSpec(memory_space=pltpu.SMEM),
            pl.BlockSpec((None, s, LANES), lambda bi, c: (bi, 0, q_blk0 + c)),
            pl.BlockSpec((None, s, LANES), lambda bi, c: (bi, 0, k_blk)),
            pl.BlockSpec((None, s, LANES), lambda bi, c: (bi, 0, v_blk)),
        ],
        out_specs=pl.BlockSpec((None, s, LANES), lambda bi, c: (bi, 0, c)),
        out_shape=jax.ShapeDtypeStruct((b, s, n_pairs * LANES), BF16),
        scratch_shapes=[pltpu.VMEM((3, 2 * blk, span), F32)],
        compiler_params=pltpu.CompilerParams(
            dimension_semantics=("arbitrary", "arbitrary"), vmem_limit_bytes=VMEM_LIMIT),
        name="window_attn",
    )(params, qkv, qkv, qkv)


PERM_CHUNK = 256
PHASES4 = 4
PIECE = PERM_CHUNK // PHASES4


def _stack_heads(q):
    lo = _lo_mask((1, LANES))
    zero = jnp.zeros_like(q)
    return jnp.concatenate([jnp.where(lo, q, zero), jnp.where(lo, zero, q)], axis=0)


def _merge_heads(x, blk):
    x = jnp.broadcast_to(x, (2 * blk, LANES))
    return jnp.where(_lo_mask((blk, LANES)), x[:blk], x[blk:])


def _attend_group(qs, ks, vs, biases):
    scores = [lax.dot_general(_stack_heads(q), k, (((1,), (1,)), ((), ())),
                              preferred_element_type=F32) for q, k in zip(qs, ks)]
    stats = []
    for s, bias in zip(scores, biases):
        s = s + bias
        m = jnp.max(s, axis=-1, keepdims=True)
        p = jnp.exp(s - m)
        stats.append((p.astype(BF16), m, jnp.sum(p, axis=-1, keepdims=True)))
    return [(jnp.dot(p, v, preferred_element_type=F32), m, l)
            for (p, m, l), v in zip(stats, vs)]


def _fill_band_bias(bias_scr, pos_q, pos_k, radius, pos_scale, slope_col):
    for var in range(3):
        dist = jnp.abs(pos_q + var * radius - pos_k)
        dist_f = (dist * pos_scale).astype(F32)
        bias_scr[var] = jnp.where(dist <= radius, -(slope_col * dist_f), NEG_INF)


def _dilated_kernel(par_ref, q_ref, k_ref, v_ref, o_ref, qp, kp, vp, acc_p, m_p, l_p,
                    bias1, bias4, bias16, *, seq, blk, radius):
    span = blk + 2 * radius
    n_chunk = seq // PERM_CHUNK
    group = ATTN_UNROLL
    c = pl.program_id(1)
    row_lo = lax.broadcasted_iota(jnp.int32, (2 * blk, 1), 0) < blk
    slope_col = jnp.where(row_lo, par_ref[0, 2 * c], par_ref[0, 2 * c + 1])

    def piece_pos(x):
        return PIECE * (x // PIECE) + PHASES4 * (x % 16) + (x % PIECE) // 16

    iq = lax.broadcasted_iota(jnp.int32, (2 * blk, span), 0) % blk
    ik = lax.broadcasted_iota(jnp.int32, (2 * blk, span), 1)
    _fill_band_bias(bias1, iq, ik, radius, 1, slope_col)
    _fill_band_bias(bias4, piece_pos(iq), piece_pos(ik), radius, 4, slope_col)
    _fill_band_bias(bias16, iq, ik, radius, 16, slope_col)

    rr = lax.broadcasted_iota(jnp.int32, (PERM_CHUNK, PERM_CHUNK), 0)
    cc = lax.broadcasted_iota(jnp.int32, (PERM_CHUNK, PERM_CHUNK), 1)
    to_phase = jnp.where(cc == 16 * (rr % 16) + PHASES4 * ((rr // 16) % 4) + rr // PIECE,
                         1.0, 0.0).astype(BF16)
    to_token = jnp.where(cc == PIECE * (rr % PHASES4) + 16 * ((rr // PHASES4) % 4) + rr // 16,
                         1.0, 0.0).astype(BF16)

    def permute_body(ch, carry):
        r0 = pl.multiple_of(ch * PERM_CHUNK, PERM_CHUNK)
        x = jnp.concatenate([q_ref[pl.ds(r0, PERM_CHUNK), :], k_ref[pl.ds(r0, PERM_CHUNK), :],
                             v_ref[pl.ds(r0, PERM_CHUNK), :]], axis=1)
        y = jnp.dot(to_phase, x, preferred_element_type=F32).astype(BF16)
        for p4 in range(PHASES4):
            rows = slice(PIECE * p4, PIECE * (p4 + 1))
            qp[p4, ch] = y[rows, 0:LANES]
            kp[p4, ch] = y[rows, LANES:2 * LANES]
            vp[p4, ch] = y[rows, 2 * LANES:3 * LANES]
        return carry

    lax.fori_loop(0, n_chunk, permute_body, 0)

    pieces_per_blk = blk // PIECE
    n_blk4 = seq // (PHASES4 * blk)

    def dil4_body(g, carry):
        p4 = g // (n_blk4 // group)
        j0 = (g % (n_blk4 // group)) * group
        qs, ks, vs, bs = [], [], [], []
        for u in range(group):
            pc = pieces_per_blk * (j0 + u)
            cs = jnp.clip(pc - radius // PIECE, 0, n_chunk - span // PIECE)
            qs.append(qp[p4, pl.ds(pc, pieces_per_blk)].reshape(blk, LANES))
            ks.append(kp[p4, pl.ds(cs, span // PIECE)].reshape(span, LANES))
            vs.append(vp[p4, pl.ds(cs, span // PIECE)].reshape(span, LANES))
            bs.append(bias4[(pc - cs) // (radius // PIECE)])
        for u, (o, m, l) in enumerate(_attend_group(qs, ks, vs, bs)):
            dst = (p4, pl.ds(pieces_per_blk * (j0 + u), pieces_per_blk))
            acc_p[dst] = _merge_heads(o, blk).reshape(pieces_per_blk, PIECE, LANES)
            m_p[dst] = _merge_heads(m, blk).reshape(pieces_per_blk, PIECE, LANES)
            l_p[dst] = _merge_heads(l, blk).reshape(pieces_per_blk, PIECE, LANES)
        return carry

    lax.fori_loop(0, PHASES4 * n_blk4 // group, dil4_body, 0)

    chunks_per_blk = blk // 16

    def dil16_body(g, carry):
        p4 = g // 2
        qs, ks, vs, bs, dsts = [], [], [], [], []
        for mm in range(2):
            off = pl.multiple_of((2 * (g % 2) + mm) * 16, 16)
            kk = kp[p4, :, pl.ds(off, 16), :].reshape(span, LANES)
            vv = vp[p4, :, pl.ds(off, 16), :].reshape(span, LANES)
            for jb in range(n_chunk // chunks_per_blk):
                dst = (p4, pl.ds(chunks_per_blk * jb, chunks_per_blk), pl.ds(off, 16), slice(None))
                qs.append(qp[dst].reshape(blk, LANES))
                ks.append(kk)
                vs.append(vv)
                bs.append(bias16[jb * (blk // radius)])
                dsts.append(dst)
        for dst, (o, m, l) in zip(dsts, _attend_group(qs, ks, vs, bs)):
            m_old = m_p[dst].reshape(blk, LANES)
            m_new = jnp.maximum(m_old, _merge_heads(m, blk))
            w_old = jnp.exp(m_old - m_new)
            w_new = jnp.exp(_merge_heads(m, blk) - m_new)
            acc = w_old * acc_p[dst].reshape(blk, LANES) + w_new * _merge_heads(o, blk)
            den = w_old * l_p[dst].reshape(blk, LANES) + w_new * _merge_heads(l, blk)
            acc_p[dst] = acc.reshape(chunks_per_blk, 16, LANES)
            l_p[dst] = den.reshape(chunks_per_blk, 16, LANES)
            m_p[dst] = m_new.reshape(chunks_per_blk, 16, LANES)
        return carry

    lax.fori_loop(0, PHASES4 * 2, dil16_body, 0)

    blk_per_chunk = PERM_CHUNK // blk

    def dil1_body(g, carry):
        merged = []
        for cc2 in range(group // blk_per_chunk):
            ch = g * (group // blk_per_chunk) + cc2
            acc = jnp.concatenate([acc_p[p4, ch] for p4 in range(PHASES4)], axis=0)
            den = jnp.concatenate([l_p[p4, ch] for p4 in range(PHASES4)], axis=0)
            mx = jnp.concatenate([m_p[p4, ch] for p4 in range(PHASES4)], axis=0)
            o_ph = (acc * (1.0 / den)).astype(BF16)
            lse = mx + jnp.log(den)
            hi = lse.astype(BF16)
            rem = lse - hi.astype(F32)
            mid = rem.astype(BF16)
            low = (rem - mid.astype(F32)).astype(BF16)
            back = jnp.dot(to_token, jnp.concatenate([o_ph, hi, mid, low], axis=1),
                           preferred_element_type=F32)
            merged.append((back[:, 0:LANES], back[:, LANES:2 * LANES]
                           + back[:, 2 * LANES:3 * LANES] + back[:, 3 * LANES:4 * LANES]))
        qs, ks, vs, bs, starts = [], [], [], [], []
        for u in range(group):
            a = pl.multiple_of((g * group + u) * blk, blk)
            kst = pl.multiple_of(jnp.clip(a - radius, 0, seq - span), radius)
            qs.append(q_ref[pl.ds(a, blk), :])
            ks.append(k_ref[pl.ds(kst, span), :])
            vs.append(v_ref[pl.ds(kst, span), :])
            bs.append(bias1[(a - kst) // radius])
            starts.append(a)
        for u, (o, m, l) in enumerate(_attend_group(qs, ks, vs, bs)):
            o1 = _merge_heads(o * (1.0 / l), blk)
            lse1 = _merge_heads(m + jnp.log(l), blk)
            o2, lse2 = merged[u // blk_per_chunk]
            half = slice((u % blk_per_chunk) * blk, (u % blk_per_chunk + 1) * blk)
            o2, lse2 = o2[half], lse2[half]
            top = jnp.maximum(lse1, lse2)
            e1 = jnp.exp(lse1 - top)
            e2 = jnp.exp(lse2 - top)
            o_ref[pl.ds(starts[u], blk), :] = ((e1 * o1 + e2 * o2)
                                               * (1.0 / (e1 + e2))).astype(o_ref.dtype)
        return carry

    lax.fori_loop(0, seq // (blk * group), dil1_body, 0)


def _dilated_attention(qkv, params, *, k_blk0, v_blk0, n_pairs, blk, radius):
    b, s, _ = qkv.shape
    span = blk + 2 * radius
    n_chunk = s // PERM_CHUNK
    assert s // 16 == span and s % PERM_CHUNK == 0 and blk % PIECE == 0
    phase = lambda dt: pltpu.VMEM((PHASES4, n_chunk, PIECE, LANES), dt)
    return pl.pallas_call(
        functools.partial(_dilated_kernel, seq=s, blk=blk, radius=radius),
        grid=(b, n_pairs),
        in_specs=[
            pl.BlockSpec(memory_space=pltpu.SMEM),
            pl.BlockSpec((None, s, LANES), lambda bi, c: (bi, 0, c)),
            pl.BlockSpec((None, s, LANES), lambda bi, c: (bi, 0, k_blk0 + c)),
            pl.BlockSpec((None, s, LANES), lambda bi, c: (bi, 0, v_blk0 + c)),
        ],
        out_specs=pl.BlockSpec((None, s, LANES), lambda bi, c: (bi, 0, c)),
        out_shape=jax.ShapeDtypeStruct((b, s, n_pairs * LANES), BF16),
        scratch_shapes=[phase(BF16), phase(BF16), phase(BF16), phase(F32), phase(F32), phase(F32)]
        + [pltpu.VMEM((3, 2 * blk, span), F32)] * 3,
        compiler_params=pltpu.CompilerParams(
            dimension_semantics=("arbitrary", "arbitrary"), vmem_limit_bytes=VMEM_LIMIT),
        name="dilated_attn",
    )(params, qkv, qkv, qkv)


def _na_kernel(q_ref, k_ref, v_ref, bias_ref, o_ref, *, rows):
    kh = NA_ROWS
    group = ATTN_UNROLL
    lo_q = _lo_mask((1, LANES))
    lo = _lo_mask((GRID_W, LANES))

    def body(g, carry):
        starts, offs, values, scores = [], [], [], []
        for u in range(group):
            i = g * group + u
            rs = jnp.clip(i - kh // 2, 0, rows - kh)
            qs = pl.multiple_of(i * GRID_W, GRID_W)
            ks = pl.multiple_of(rs * GRID_W, GRID_W)
            q = q_ref[pl.ds(qs, GRID_W), :]
            zero = jnp.zeros_like(q)
            qm = jnp.concatenate([jnp.where(lo_q, q, zero), jnp.where(lo_q, zero, q)], axis=0)
            k = k_ref[pl.ds(ks, kh * GRID_W), :]
            scores.append(lax.dot_general(qm, k, (((1,), (1,)), ((), ())),
                                          preferred_element_type=F32))
            starts.append(qs)
            offs.append(rs - i + (NA_ROWS - 1))
            values.append(v_ref[pl.ds(ks, kh * GRID_W), :])
        probs, dens = [], []
        for u in range(group):
            s = scores[u] + bias_ref[offs[u]]
            m = jnp.max(s, axis=-1, keepdims=True)
            p = jnp.exp(s - m)
            dens.append(jnp.sum(p, axis=-1, keepdims=True))
            probs.append(p.astype(BF16))
        for u in range(group):
            o = jnp.dot(probs[u], values[u], preferred_element_type=F32) * (1.0 / dens[u])
            o_ref[pl.ds(starts[u], GRID_W), :] = jnp.where(
                lo, o[:GRID_W], o[GRID_W:]).astype(o_ref.dtype)
        return carry

    lax.fori_loop(0, rows // group, body, 0)


def _na_bias_table(rpb):
    qc = np.arange(GRID_W)[:, None]
    kc = np.arange(GRID_W)[None, :]
    qstart = np.clip(qc - NA_COLS // 2, 0, GRID_W - NA_COLS)
    valid = (kc >= qstart) & (kc < qstart + NA_COLS)
    coff = np.clip(kc - qc, -(NA_COLS - 1), NA_COLS - 1) + NA_COLS - 1
    row_idx = np.arange(NA_ROWS)[:, None] + np.arange(NA_ROWS)[None, :]
    t = rpb.astype(F32)[:, row_idx]
    t = t[:, :, :, coff]
    t = jnp.where(jnp.asarray(valid), t, NEG_INF)
    t = t.transpose(0, 1, 3, 2, 4).reshape(C_HEADS // 2, 2, NA_ROWS, GRID_W, NA_ROWS * GRID_W)
    return t.transpose(0, 2, 1, 3, 4).reshape(C_HEADS // 2, NA_ROWS, 2 * GRID_W, NA_ROWS * GRID_W)


def _na_attention(qkv, bias):
    b, s, n_cols = qkv.shape
    n_pairs = C_HEADS // 2
    rows = s // GRID_W
    return pl.pallas_call(
        functools.partial(_na_kernel, rows=rows),
        grid=(b, n_pairs),
        in_specs=[
            pl.BlockSpec((None, s, LANES), lambda bi, c: (bi, 0, c)),
            pl.BlockSpec((None, s, LANES), lambda bi, c: (bi, 0, n_pairs + c)),
            pl.BlockSpec((None, s, LANES), lambda bi, c: (bi, 0, 2 * n_pairs + c)),
            pl.BlockSpec((None, NA_ROWS, 2 * GRID_W, NA_ROWS * GRID_W),
                         lambda bi, c: (c, 0, 0, 0)),
        ],
        out_specs=pl.BlockSpec((None, s, LANES), lambda bi, c: (bi, 0, c)),
        out_shape=jax.ShapeDtypeStruct((b, s, n_pairs * LANES), BF16),
        compiler_params=pltpu.CompilerParams(
            dimension_semantics=("arbitrary", "arbitrary"), vmem_limit_bytes=VMEM_LIMIT),
        name="na_attn",
    )(qkv, qkv, qkv, bias)


def _out_proj_kernel(x_ref, *refs):
    y_ref = refs[-1]
    n = (len(refs) - 1) // 2
    y = x_ref[...]
    for i in range(n):
        y = y + jnp.dot(refs[i][...], refs[n + i][...], preferred_element_type=F32)
    y_ref[...] = y


def _out_proj(x, os, ws, tm=512):
    t, d = x.shape
    row = lambda w: pl.BlockSpec((tm, w), lambda i: (i, 0))
    return pl.pallas_call(
        _out_proj_kernel,
        grid=(t // tm,),
        in_specs=[row(d)] + [row(o.shape[1]) for o in os]
        + [pl.BlockSpec(w.shape, lambda i: (0, 0)) for w in ws],
        out_specs=row(d),
        out_shape=jax.ShapeDtypeStruct((t, d), F32),
        compiler_params=pltpu.CompilerParams(
            dimension_semantics=("arbitrary",), vmem_limit_bytes=VMEM_LIMIT),
        name="out_proj",
    )(x, *os, *ws)


def _silu(g):
    return g * (1.0 / (1.0 + jnp.exp(-g)))


def _ffn_kernel(x_ref, g_ref, wg_ref, wu_ref, wd_ref, y_ref, a_scr):
    x = x_ref[...]
    h = _rms_rows(x, g_ref[...]).astype(BF16)
    f = wg_ref.shape[1]
    for j in range(f // MXU_N):
        sl = slice(j * MXU_N, (j + 1) * MXU_N)
        gate = jnp.dot(h, wg_ref[:, sl], preferred_element_type=F32)
        up = jnp.dot(h, wu_ref[:, sl], preferred_element_type=F32)
        a_scr[:, sl] = (_silu(gate) * up).astype(BF16)
    y_ref[...] = x + jnp.dot(a_scr[...], wd_ref[...], preferred_element_type=F32)


def _ffn(x, g, wg, wu, wd, tm=512):
    t, d = x.shape
    f = wg.shape[1]
    const = lambda a: pl.BlockSpec(a.shape, lambda i: (0, 0), pipeline_mode=pl.Buffered(1))
    return pl.pallas_call(
        _ffn_kernel,
        grid=(t // tm,),
        in_specs=[pl.BlockSpec((tm, d), lambda i: (i, 0)),
                  pl.BlockSpec((1, d), lambda i: (0, 0)),
                  const(wg), const(wu), const(wd)],
        out_specs=pl.BlockSpec((tm, d), lambda i: (i, 0)),
        out_shape=jax.ShapeDtypeStruct((t, d), F32),
        scratch_shapes=[pltpu.VMEM((tm, f), BF16)],
        compiler_params=pltpu.CompilerParams(
            dimension_semantics=("arbitrary",), vmem_limit_bytes=VMEM_LIMIT),
        name="dense_swiglu",
    )(x, g.reshape(1, d), wg, wu, wd)


TOP_K = 2
MOE_TB = 512
MOE_TM = 512
MOE_RC = 256
KEY_STRIDE = 65536.0


def _router_kernel(x_ref, g_ref, wr_ref, h_ref, info_ref, info_t_ref, cnt_ref, carry_scr):
    i = pl.program_id(0)
    tb = x_ref.shape[0]
    lane = lax.broadcasted_iota(jnp.int32, (1, LANES), 1)

    @pl.when(i == 0)
    def _():
        carry_scr[...] = jnp.zeros_like(carry_scr)

    h = _rms_rows(x_ref[...], g_ref[...])
    h_ref[...] = h.astype(BF16)
    logits = jnp.dot(h, wr_ref[...], preferred_element_type=F32,
                     precision=lax.Precision.HIGHEST)
    logits = jnp.where(lane < N_EXPERTS, logits, NEG_INF)
    v1 = jnp.max(logits, axis=-1, keepdims=True)
    i1 = jnp.min(jnp.where(logits == v1, lane, LANES), axis=-1, keepdims=True)
    rest = jnp.where(lane == i1, NEG_INF, logits)
    v2 = jnp.max(rest, axis=-1, keepdims=True)
    i2 = jnp.min(jnp.where(rest == v2, lane, LANES), axis=-1, keepdims=True)
    e2 = jnp.exp(v2 - v1)
    g1 = 1.0 / (1.0 + e2)
    g2 = e2 * g1

    pick1 = lane == i1
    pick2 = lane == i2
    picked = jnp.where(pick1, 1.0, jnp.where(pick2, 1.0, 0.0))
    r = lax.broadcasted_iota(jnp.int32, (tb, tb), 0)
    cidx = lax.broadcasted_iota(jnp.int32, (tb, tb), 1)
    ltri = jnp.where(cidx < r, 1.0, 0.0).astype(BF16)
    before = jnp.dot(ltri, picked.astype(BF16), preferred_element_type=F32) + carry_scr[...]
    rank1 = jnp.sum(jnp.where(pick1, before, 0.0), axis=-1, keepdims=True)
    rank2 = jnp.sum(jnp.where(pick2, before, 0.0), axis=-1, keepdims=True)
    key1 = i1.astype(F32) * KEY_STRIDE + rank1
    key2 = i2.astype(F32) * KEY_STRIDE + rank2
    info = jnp.where(lane == 0, key1, jnp.where(lane == 1, key2,
                     jnp.where(lane == 2, g1, jnp.where(lane == 3, g2, 0.0))))
    info_ref[...] = info
    info_t_ref[...] = info.T[:8, :]
    counts = jnp.sum(picked, axis=0, keepdims=True)
    cnt_ref[...] = jnp.broadcast_to(counts, cnt_ref.shape)
    carry_scr[...] += counts


def _router(x, g, wr):
    t, d = x.shape
    nb = t // MOE_TB
    return pl.pallas_call(
        _router_kernel,
        grid=(nb,),
        in_specs=[pl.BlockSpec((MOE_TB, d), lambda i: (i, 0)),
                  pl.BlockSpec((1, d), lambda i: (0, 0)),
                  pl.BlockSpec(wr.shape, lambda i: (0, 0))],
        out_specs=[pl.BlockSpec((MOE_TB, d), lambda i: (i, 0)),
                   pl.BlockSpec((MOE_TB, LANES), lambda i: (i, 0)),
                   pl.BlockSpec((8, MOE_TB), lambda i: (0, i)),
                   pl.BlockSpec((None, 8, LANES), lambda i: (i, 0, 0))],
        out_shape=[jax.ShapeDtypeStruct((t, d), BF16),
                   jax.ShapeDtypeStruct((t, LANES), F32),
                   jax.ShapeDtypeStruct((8, t), F32),
                   jax.ShapeDtypeStruct((nb, 8, LANES), F32)],
        scratch_shapes=[pltpu.VMEM((1, LANES), F32)],
        compiler_params=pltpu.CompilerParams(
            dimension_semantics=("arbitrary",), vmem_limit_bytes=VMEM_LIMIT),
        name="moe_router",
    )(x, g.reshape(1, d), wr)


def _moe_plan(block_counts, n_tokens):
    nb = block_counts.shape[0]
    n_tiles = (TOP_K * n_tokens + N_EXPERTS * (MOE_TM - 1)) // MOE_TM
    n_chunks = n_tiles * MOE_TM // MOE_RC
    n_pairs = ((n_chunks + N_EXPERTS * nb + 7) // 8) * 8
    tc = block_counts.astype(jnp.int32)
    cb = jnp.concatenate([jnp.zeros((1, N_EXPERTS), jnp.int32), jnp.cumsum(tc, axis=0)])
    counts = cb[-1]
    padded = ((counts + MOE_TM - 1) // MOE_TM) * MOE_TM
    ends = jnp.cumsum(padded)
    starts = ends - padded
    total = ends[-1]

    tiles = jnp.arange(n_tiles, dtype=jnp.int32)
    n_valid_tiles = total // MOE_TM
    t_valid = (tiles < n_valid_tiles).astype(jnp.int32)
    t_src = jnp.minimum(tiles, n_valid_tiles - 1)
    t_exp = jnp.minimum(jnp.sum((t_src * MOE_TM)[:, None] >= ends[None, :], axis=1),
                        N_EXPERTS - 1).astype(jnp.int32)

    r0 = jnp.arange(n_chunks, dtype=jnp.int32) * MOE_RC
    c_exp = jnp.minimum(jnp.sum(r0[:, None] >= ends[None, :], axis=1), N_EXPERTS - 1)
    c_live = r0 < total
    lo = r0 - starts[c_exp]
    base = (c_exp * int(KEY_STRIDE) + lo).astype(jnp.int32)
    cbe = cb[:, c_exp]
    real = c_live[None, :] & (cbe[:-1] < (lo + MOE_RC)[None, :]) & (cbe[1:] > lo[None, :])
    empty = c_live & jnp.logical_not(jnp.any(real, axis=0))
    ov = real | (empty[None, :] & (jnp.arange(nb)[:, None] == 0))
    n_act = jnp.sum(ov).astype(jnp.int32)
    pidx = jnp.arange(n_pairs, dtype=jnp.int32)
    act = pidx < n_act

    def pair_list(flat, inner):
        idx = jnp.nonzero(flat, size=n_pairs, fill_value=0)[0].astype(jnp.int32)
        idx = jnp.where(act, idx, idx[n_act - 1])
        return idx // inner, idx % inner

    def flags(major):
        first = act & ((pidx == 0) | (major != jnp.roll(major, 1)))
        last = act & ((pidx == n_act - 1) | (major != jnp.roll(major, -1)))
        return (act.astype(jnp.int32) + 2 * first.astype(jnp.int32)
                + 4 * last.astype(jnp.int32))

    g_c, g_b = pair_list(ov.T.reshape(-1), nb)
    c_b, c_c = pair_list(ov.reshape(-1), n_chunks)
    return dict(n_tiles=n_tiles, n_chunks=n_chunks, n_pairs=n_pairs,
                t_valid=t_valid, t_src=t_src, t_exp=t_exp, base=base,
                g_c=g_c, g_b=g_b, g_flags=flags(g_c), c_b=c_b, c_c=c_c, c_flags=flags(c_b))


def _gather_kernel(pc_ref, pb_ref, fl_ref, base_ref, info_t_ref, h_ref, xs_ref, gs_ref,
                   acc_scr, gsum_scr):
    p = pl.program_id(0)
    fl = fl_ref[p]
    tb = h_ref.shape[0]

    @pl.when((fl // 2) % 2 == 1)
    def _():
        acc_scr[...] = jnp.zeros_like(acc_scr)
        gsum_scr[...] = jnp.zeros_like(gsum_scr)

    @pl.when(fl % 2 == 1)
    def _():
        base = base_ref[pc_ref[p]].astype(F32)
        row = lax.broadcasted_iota(jnp.int32, (MOE_RC, tb), 0).astype(F32)
        m1 = (info_t_ref[0:1, :] - base) == row
        m2 = (info_t_ref[1:2, :] - base) == row
        sel = jnp.where(m1, 1.0, jnp.where(m2, 1.0, 0.0)).astype(BF16)
        gate = jnp.sum(jnp.where(m1, info_t_ref[2:3, :], jnp.where(m2, info_t_ref[3:4, :], 0.0)),
                       axis=-1, keepdims=True)
        acc = acc_scr[...] + jnp.dot(sel, h_ref[...], preferred_element_type=F32)
        gsum = gsum_scr[...] + gate
        acc_scr[...] = acc
        gsum_scr[...] = gsum
        xs_ref[...] = acc.astype(xs_ref.dtype)
        gs_ref[...] = jnp.broadcast_to(gsum, gs_ref.shape)


def _moe_gather(plan, info_t, h):
    t, d = h.shape
    rows = plan["n_chunks"] * MOE_RC
    grid_spec = pltpu.PrefetchScalarGridSpec(
        num_scalar_prefetch=4,
        grid=(plan["n_pairs"],),
        in_specs=[pl.BlockSpec((8, MOE_TB), lambda p, pc, pb, fl, ba: (0, pb[p])),
                  pl.BlockSpec((MOE_TB, d), lambda p, pc, pb, fl, ba: (pb[p], 0))],
        out_specs=[pl.BlockSpec((MOE_RC, d), lambda p, pc, pb, fl, ba: (pc[p], 0)),
                   pl.BlockSpec((MOE_RC, LANES), lambda p, pc, pb, fl, ba: (pc[p], 0))],
        scratch_shapes=[pltpu.VMEM((MOE_RC, d), F32), pltpu.VMEM((MOE_RC, 1), F32)],
    )
    return pl.pallas_call(
        _gather_kernel,
        grid_spec=grid_spec,
        out_shape=[jax.ShapeDtypeStruct((rows, d), BF16),
                   jax.ShapeDtypeStruct((rows, LANES), F32)],
        compiler_params=pltpu.CompilerParams(
            dimension_semantics=("arbitrary",), vmem_limit_bytes=VMEM_LIMIT),
        name="moe_gather",
    )(plan["g_c"], plan["g_b"], plan["g_flags"], plan["base"], info_t, h)


def _experts_kernel(src_ref, exp_ref, val_ref, xs_ref, gs_ref, wg_ref, wu_ref, wd_ref, ys_ref,
                    acc_scr):
    i = pl.program_id(0)
    j = pl.program_id(1)

    @pl.when(val_ref[i] == 1)
    def _():
        x = xs_ref[...]
        gate = jnp.dot(x, wg_ref[...], preferred_element_type=F32)
        up = jnp.dot(x, wu_ref[...], preferred_element_type=F32)
        a = (_silu(gate) * up).astype(BF16)
        y = jnp.dot(a, wd_ref[...], preferred_element_type=F32)

        @pl.when(j == 0)
        def _():
            acc_scr[...] = y

        @pl.when(j > 0)
        def _():
            acc_scr[...] += y

        @pl.when(j == pl.num_programs(1) - 1)
        def _():
            ys_ref[...] = (acc_scr[...] * gs_ref[:, 0:1]).astype(ys_ref.dtype)


def _moe_experts(plan, xs, gs, wg, wu, wd, tf):
    rows, d = xs.shape
    f = wg.shape[2]
    nf = f // tf

    def wcol(i, j, src, ex, val):
        return (ex[i], 0, jnp.where(val[i] == 1, j, nf - 1))

    def wrow(i, j, src, ex, val):
        return (ex[i], jnp.where(val[i] == 1, j, nf - 1), 0)

    tile = lambda i, j, src, ex, val: (src[i], 0)
    grid_spec = pltpu.PrefetchScalarGridSpec(
        num_scalar_prefetch=3,
        grid=(plan["n_tiles"], nf),
        in_specs=[pl.BlockSpec((MOE_TM, d), tile),
                  pl.BlockSpec((MOE_TM, LANES), tile),
                  pl.BlockSpec((None, d, tf), wcol),
                  pl.BlockSpec((None, d, tf), wcol),
                  pl.BlockSpec((None, tf, d), wrow)],
        out_specs=pl.BlockSpec((MOE_TM, d), tile),
        scratch_shapes=[pltpu.VMEM((MOE_TM, d), F32)],
    )
    return pl.pallas_call(
        _experts_kernel,
        grid_spec=grid_spec,
        out_shape=jax.ShapeDtypeStruct((rows, d), BF16),
        compiler_params=pltpu.CompilerParams(
            dimension_semantics=("arbitrary", "arbitrary"), vmem_limit_bytes=VMEM_LIMIT),
        name="moe_experts",
    )(plan["t_src"], plan["t_exp"], plan["t_valid"], xs, gs, wg, wu, wd)


def _combine_kernel(pb_ref, pc_ref, fl_ref, base_ref, info_ref, ys_ref, x_ref, o_ref):
    p = pl.program_id(0)
    fl = fl_ref[p]
    tb = x_ref.shape[0]

    @pl.when((fl // 2) % 2 == 1)
    def _():
        o_ref[...] = x_ref[...]

    @pl.when(fl % 2 == 1)
    def _():
        base = base_ref[pc_ref[p]].astype(F32)
        lane = lax.broadcasted_iota(jnp.int32, (1, LANES), 1)
        info = info_ref[...]
        k1 = jnp.sum(jnp.where(lane == 0, info, 0.0), axis=-1, keepdims=True) - base
        k2 = jnp.sum(jnp.where(lane == 1, info, 0.0), axis=-1, keepdims=True) - base
        col = lax.broadcasted_iota(jnp.int32, (tb, MOE_RC), 1).astype(F32)
        sel = jnp.where(k1 == col, 1.0, jnp.where(k2 == col, 1.0, 0.0)).astype(BF16)
        o_ref[...] += jnp.dot(sel, ys_ref[...], preferred_element_type=F32)


def _moe_combine(plan, info, ys, x):
    t, d = x.shape
    grid_spec = pltpu.PrefetchScalarGridSpec(
        num_scalar_prefetch=4,
        grid=(plan["n_pairs"],),
        in_specs=[pl.BlockSpec((MOE_TB, LANES), lambda p, pb, pc, fl, ba: (pb[p], 0)),
                  pl.BlockSpec((MOE_RC, d), lambda p, pb, pc, fl, ba: (pc[p], 0)),
                  pl.BlockSpec((MOE_TB, d), lambda p, pb, pc, fl, ba: (pb[p], 0))],
        out_specs=pl.BlockSpec((MOE_TB, d), lambda p, pb, pc, fl, ba: (pb[p], 0)),
    )
    return pl.pallas_call(
        _combine_kernel,
        grid_spec=grid_spec,
        out_shape=jax.ShapeDtypeStruct((t, d), F32),
        compiler_params=pltpu.CompilerParams(
            dimension_semantics=("arbitrary",), vmem_limit_bytes=VMEM_LIMIT),
        name="moe_combine",
    )(plan["c_b"], plan["c_c"], plan["c_flags"], plan["base"], info, ys, x)


def _moe(x, g, wr, wg, wu, wd, tf=1792):
    t, _ = x.shape
    h, info, info_t, block_counts = _router(x, g, wr)
    plan = _moe_plan(block_counts[:, 0, :N_EXPERTS], t)
    xs, gs = _moe_gather(plan, info_t, h)
    ys = _moe_experts(plan, xs, gs, wg, wu, wd, tf)
    return _moe_combine(plan, info, ys, x)


def _alibi_slopes():
    s = np.exp2(-8.0 * np.arange(1, N_HEADS + 1) / N_HEADS).astype(np.float32)
    return s[0::2][:A_HEADS], s[1::2][:B_Q_HEADS]


_B_HEAD_ORDER = np.array([0, 4, 1, 5, 2, 6, 3, 7])


def _even_layer(x2d, batch, norm1, w_in, qn_a, kn_a, qn_b, kn_b, sink_b, w_out,
                norm2, w_gate, w_up, w_down):
    t, d = x2d.shape
    s = t // batch
    wa = A_HEADS * HEAD_DIM
    wq_b = B_Q_HEADS * HEAD_DIM
    wkv_b = B_KV_HEADS * HEAD_DIM
    qb0 = 3 * wa
    q_scale = HEAD_DIM ** -0.5

    w_qb = w_in[:, qb0:qb0 + wq_b].reshape(d, B_Q_HEADS, HEAD_DIM)[:, _B_HEAD_ORDER]
    w_in_p = jnp.concatenate(
        [w_in[:, :qb0], w_qb.reshape(d, wq_b), w_in[:, qb0 + wq_b:]], axis=1).astype(BF16)
    ones = jnp.ones((HEAD_DIM,), F32)
    col_gain = jnp.concatenate([
        jnp.tile(qn_a * q_scale, A_HEADS), jnp.tile(kn_a, A_HEADS), jnp.tile(ones, A_HEADS),
        jnp.tile(qn_b * q_scale, B_Q_HEADS), jnp.tile(kn_b, B_KV_HEADS),
        jnp.tile(ones, B_KV_HEADS)])
    cpl = wa // LANES
    norm_chunks = ((True,) * (2 * cpl) + (False,) * cpl + (True,) * (wq_b // LANES)
                   + (True,) * (wkv_b // LANES) + (False,) * (wkv_b // LANES))
    qkv = _norm_proj(x2d, norm1, w_in_p, col_gain, norm_chunks)
    n_cols = qkv.shape[1]
    qkv3 = qkv.reshape(batch, s, n_cols)

    slopes_a, slopes_b = _alibi_slopes()
    par_a = jnp.asarray(slopes_a).reshape(1, A_HEADS)
    radii = {window // (2 * dil) for window, dil in DIL_CONFIGS}
    assert tuple(dil for _, dil in DIL_CONFIGS) == (1, 4, 16) and len(radii) == 1
    oa = _dilated_attention(qkv3, par_a, k_blk0=cpl, v_blk0=2 * cpl, n_pairs=cpl,
                            blk=A_BLOCK, radius=radii.pop())
    par_b = jnp.stack([jnp.asarray(slopes_b[_B_HEAD_ORDER]),
                       sink_b.astype(F32)[_B_HEAD_ORDER]])
    kb = 3 * cpl + wq_b // LANES
    ob = _window_attention(qkv3, par_b, q_blk0=3 * cpl, k_blk=kb, v_blk=kb + wkv_b // LANES,
                           n_pairs=wq_b // LANES, radius=B_WINDOW, blk=B_BLOCK)
    w_out_a = w_out[:wa].astype(BF16)
    w_out_b = w_out[wa:].reshape(B_Q_HEADS, HEAD_DIM, d)[_B_HEAD_ORDER].reshape(wq_b, d)
    x1 = _out_proj(x2d, [oa.reshape(t, wa), ob.reshape(t, wq_b)],
                   [w_out_a, w_out_b.astype(BF16)])
    return _ffn(x1, norm2, w_gate.astype(BF16), w_up.astype(BF16), w_down.astype(BF16))


def _odd_layer(x2d, batch, norm1, w_qkv, qn, kn, rpb, w_out, norm2, w_router,
               w_gate, w_up, w_down):
    t, d = x2d.shape
    s = t // batch
    wc = C_HEADS * HEAD_DIM
    ones = jnp.ones((HEAD_DIM,), F32)
    col_gain = jnp.concatenate([jnp.tile(qn * HEAD_DIM ** -0.5, C_HEADS),
                                jnp.tile(kn, C_HEADS), jnp.tile(ones, C_HEADS)])
    cpl = wc // LANES
    norm_chunks = (True,) * (2 * cpl) + (False,) * cpl
    qkv = _norm_proj(x2d, norm1, w_qkv.astype(BF16), col_gain, norm_chunks)
    o = _na_attention(qkv.reshape(batch, s, 3 * wc), _na_bias_table(rpb))
    x1 = _out_proj(x2d, [o.reshape(t, wc)], [w_out.astype(BF16)])
    wr = jnp.zeros((d, LANES), F32).at[:, :N_EXPERTS].set(w_router.astype(F32))
    return _moe(x1, norm2, wr, w_gate.astype(BF16), w_up.astype(BF16), w_down.astype(BF16))


def kernel(x, ev_norm1, ev_w_in, ev_qn_a, ev_kn_a, ev_qn_b, ev_kn_b, ev_sink_b, ev_w_out, ev_norm2, ev_ffn_gate, ev_ffn_up, ev_ffn_down, od_norm1, od_w_qkv, od_qn, od_kn, od_rpb, od_w_out, od_norm2, od_router, od_exp_gate, od_exp_up, od_exp_down):
    batch, s, d = x.shape
    depth = ev_norm1.shape[0] + od_norm1.shape[0]
    h = x.reshape(batch * s, d)
    for layer in range(depth):
        j = layer // 2
        if layer % 2 == 0:
            h = _even_layer(h, batch, ev_norm1[j], ev_w_in[j], ev_qn_a[j], ev_kn_a[j],
                            ev_qn_b[j], ev_kn_b[j], ev_sink_b[j], ev_w_out[j], ev_norm2[j],
                            ev_ffn_gate[j], ev_ffn_up[j], ev_ffn_down[j])
        else:
            h = _odd_layer(h, batch, od_norm1[j], od_w_qkv[j], od_qn[j], od_kn[j], od_rpb[j],
                           od_w_out[j], od_norm2[j], od_router[j], od_exp_gate[j],
                           od_exp_up[j], od_exp_down[j])
    return h.reshape(batch, s, d)
```

```python
import functools

import numpy as np
import jax
import jax.numpy as jnp
from jax import lax
from jax.experimental import pallas as pl
from jax.experimental.pallas import tpu as pltpu

D_MODEL = 1024
HEAD_DIM = 64
N_HEADS = D_MODEL // HEAD_DIM
A_HEADS = N_HEADS // 2
B_Q_HEADS = N_HEADS // 2
B_KV_HEADS = max(1, B_Q_HEADS // 4)
C_HEADS = N_HEADS
DIL_CONFIGS = ((128, 1), (512, 4), (2048, 16))
A_BLOCK = 128
B_WINDOW = 128
B_BLOCK = 128
GRID_W = 64
NA_ROWS = 8
NA_COLS = 16
N_EXPERTS = 8
RMS_EPS = 1e-6
NEG_INF = -1e30

LANES = 128
MXU_N = 256
VMEM_LIMIT = 56 * 1024 * 1024

ATTN_UNROLL = 4

F32 = jnp.float32
BF16 = jnp.bfloat16


def _lo_mask(shape):
    return lax.broadcasted_iota(jnp.int32, shape, len(shape) - 1) < HEAD_DIM


def _rms_rows(x, g):
    ms = jnp.mean(x * x, axis=-1, keepdims=True)
    return x * lax.rsqrt(ms + RMS_EPS) * g


def _norm_proj_kernel(x_ref, g_ref, w_ref, cg_ref, o_ref, *, norm_chunks):
    h = _rms_rows(x_ref[...], g_ref[...]).astype(BF16)
    n_out = o_ref.shape[-1]
    lo = _lo_mask((1, LANES))
    for j in range(n_out // MXU_N):
        y = jnp.dot(h, w_ref[:, j * MXU_N:(j + 1) * MXU_N], preferred_element_type=F32)
        for half in range(MXU_N // LANES):
            c = j * (MXU_N // LANES) + half
            yc = y[:, half * LANES:(half + 1) * LANES]
            if norm_chunks[c]:
                sq = yc * yc
                s_lo = jnp.sum(jnp.where(lo, sq, 0.0), axis=-1, keepdims=True)
                s_hi = jnp.sum(jnp.where(lo, 0.0, sq), axis=-1, keepdims=True)
                inv = jnp.where(lo, lax.rsqrt(s_lo * (1.0 / HEAD_DIM) + RMS_EPS),
                                lax.rsqrt(s_hi * (1.0 / HEAD_DIM) + RMS_EPS))
                yc = yc * inv * cg_ref[:, c * LANES:(c + 1) * LANES]
            o_ref[:, c * LANES:(c + 1) * LANES] = yc.astype(o_ref.dtype)


def _norm_proj(x, g, w, col_gain, norm_chunks, tm=512):
    t, d = x.shape
    n = w.shape[1]
    return pl.pallas_call(
        functools.partial(_norm_proj_kernel, norm_chunks=norm_chunks),
        grid=(t // tm,),
        in_specs=[
            pl.BlockSpec((tm, d), lambda i: (i, 0)),
            pl.BlockSpec((1, d), lambda i: (0, 0)),
            pl.BlockSpec((d, n), lambda i: (0, 0)),
            pl.BlockSpec((1, n), lambda i: (0, 0)),
        ],
        out_specs=pl.BlockSpec((tm, n), lambda i: (i, 0)),
        out_shape=jax.ShapeDtypeStruct((t, n), BF16),
        compiler_params=pltpu.CompilerParams(
            dimension_semantics=("arbitrary",), vmem_limit_bytes=VMEM_LIMIT),
        name="norm_proj",
    )(x, g.reshape(1, d), w, col_gain.reshape(1, n))


def _window_kernel(par_ref, q_ref, k_ref, v_ref, o_ref, bias_scr, *, seq, blk, radius):
    span = blk + 2 * radius
    c = pl.program_id(1)
    group = ATTN_UNROLL
    row_lo = lax.broadcasted_iota(jnp.int32, (2 * blk, 1), 0) < blk
    slope_col = jnp.where(row_lo, par_ref[0, 2 * c], par_ref[0, 2 * c + 1])
    sink_col = jnp.where(row_lo, par_ref[1, 2 * c], par_ref[1, 2 * c + 1])

    iq = lax.broadcasted_iota(jnp.int32, (2 * blk, span), 0) % blk
    ik = lax.broadcasted_iota(jnp.int32, (2 * blk, span), 1)
    _fill_band_bias(bias_scr, iq, ik, radius, 1, slope_col)
    lo_q = _lo_mask((1, LANES))
    lo = _lo_mask((blk, LANES))

    def body(g, carry):
        starts, variants, values, scores = [], [], [], []
        for u in range(group):
            a = pl.multiple_of((g * group + u) * blk, blk)
            ks = pl.multiple_of(jnp.clip(a - radius, 0, seq - span), radius)
            q = q_ref[pl.ds(a, blk), :]
            zero = jnp.zeros_like(q)
            qm = jnp.concatenate([jnp.where(lo_q, q, zero), jnp.where(lo_q, zero, q)], axis=0)
            k = k_ref[pl.ds(ks, span), :]
            scores.append(lax.dot_general(qm, k, (((1,), (1,)), ((), ())),
                                          preferred_element_type=F32))
            starts.append(a)
            variants.append((a - ks) // radius)
            values.append(v_ref[pl.ds(ks, span), :])
        probs, dens = [], []
        for u in range(group):
            s = scores[u] + bias_scr[variants[u]]
            m = jnp.maximum(jnp.max(s, axis=-1, keepdims=True), sink_col)
            p = jnp.exp(s - m)
            dens.append(jnp.sum(p, axis=-1, keepdims=True) + jnp.exp(sink_col - m))
            probs.append(p.astype(BF16))
        for u in range(group):
            o = jnp.dot(probs[u], values[u], preferred_element_type=F32) * (1.0 / dens[u])
            o_ref[pl.ds(starts[u], blk), :] = jnp.where(lo, o[:blk], o[blk:]).astype(o_ref.dtype)
        return carry

    lax.fori_loop(0, seq // (blk * group), body, 0)


def _window_attention(qkv, params, *, q_blk0, k_blk, v_blk, n_pairs, radius, blk):
    b, s, _ = qkv.shape
    span = blk + 2 * radius
    return pl.pallas_call(
        functools.partial(_window_kernel, seq=s, blk=blk, radius=radius),
        grid=(b, n_pairs),
        in_specs=[
            pl.BlockSpec(memory_space=pltpu.SMEM),
            pl.BlockSpec((None, s, LANES), lambda bi, c: (bi, 0, q_blk0 + c)),
            pl.BlockSpec((None, s, LANES), lambda bi, c: (bi, 0, k_blk)),
            pl.BlockSpec((None, s, LANES), lambda bi, c: (bi, 0, v_blk)),
        ],
        out_specs=pl.BlockSpec((None, s, LANES), lambda bi, c: (bi, 0, c)),
        out_shape=jax.ShapeDtypeStruct((b, s, n_pairs * LANES), BF16),
        scratch_shapes=[pltpu.VMEM((3, 2 * blk, span), F32)],
        compiler_params=pltpu.CompilerParams(
            dimension_semantics=("arbitrary", "arbitrary"), vmem_limit_bytes=VMEM_LIMIT),
        name="window_attn",
    )(params, qkv, qkv, qkv)


PERM_CHUNK = 256
PHASES4 = 4
PIECE = PERM_CHUNK // PHASES4


def _stack_heads(q):
    lo = _lo_mask((1, LANES))
    zero = jnp.zeros_like(q)
    return jnp.concatenate([jnp.where(lo, q, zero), jnp.where(lo, zero, q)], axis=0)


def _merge_heads(x, blk):
    x = jnp.broadcast_to(x, (2 * blk, LANES))
    return jnp.where(_lo_mask((blk, LANES)), x[:blk], x[blk:])


def _attend_group(qs, ks, vs, biases):
    scores = [lax.dot_general(_stack_heads(q), k, (((1,), (1,)), ((), ())),
                              preferred_element_type=F32) for q, k in zip(qs, ks)]
    stats = []
    for s, bias in zip(scores, biases):
        s = s + bias
        m = jnp.max(s, axis=-1, keepdims=True)
        p = jnp.exp(s - m)
        stats.append((p.astype(BF16), m, jnp.sum(p, axis=-1, keepdims=True)))
    return [(jnp.dot(p, v, preferred_element_type=F32), m, l)
            for (p, m, l), v in zip(stats, vs)]


def _fill_band_bias(bias_scr, pos_q, pos_k, radius, pos_scale, slope_col):
    for var in range(3):
        dist = jnp.abs(pos_q + var * radius - pos_k)
        dist_f = (dist * pos_scale).astype(F32)
        bias_scr[var] = jnp.where(dist <= radius, -(slope_col * dist_f), NEG_INF)


def _dilated_kernel(par_ref, q_ref, k_ref, v_ref, o_ref, qp, kp, vp, acc_p, m_p, l_p,
                    bias1, bias4, bias16, *, seq, blk, radius):
    span = blk + 2 * radius
    n_chunk = seq // PERM_CHUNK
    group = ATTN_UNROLL
    c = pl.program_id(1)
    row_lo = lax.broadcasted_iota(jnp.int32, (2 * blk, 1), 0) < blk
    slope_col = jnp.where(row_lo, par_ref[0, 2 * c], par_ref[0, 2 * c + 1])

    def piece_pos(x):
        return PIECE * (x // PIECE) + PHASES4 * (x % 16) + (x % PIECE) // 16

    iq = lax.broadcasted_iota(jnp.int32, (2 * blk, span), 0) % blk
    ik = lax.broadcasted_iota(jnp.int32, (2 * blk, span), 1)
    _fill_band_bias(bias1, iq, ik, radius, 1, slope_col)
    _fill_band_bias(bias4, piece_pos(iq), piece_pos(ik), radius, 4, slope_col)
    _fill_band_bias(bias16, iq, ik, radius, 16, slope_col)

    rr = lax.broadcasted_iota(jnp.int32, (PERM_CHUNK, PERM_CHUNK), 0)
    cc = lax.broadcasted_iota(jnp.int32, (PERM_CHUNK, PERM_CHUNK), 1)
    to_phase = jnp.where(cc == 16 * (rr % 16) + PHASES4 * ((rr // 16) % 4) + rr // PIECE,
                         1.0, 0.0).astype(BF16)
    to_token = jnp.where(cc == PIECE * (rr % PHASES4) + 16 * ((rr // PHASES4) % 4) + rr // 16,
                         1.0, 0.0).astype(BF16)

    def permute_body(ch, carry):
        r0 = pl.multiple_of(ch * PERM_CHUNK, PERM_CHUNK)
        x = jnp.concatenate([q_ref[pl.ds(r0, PERM_CHUNK), :], k_ref[pl.ds(r0, PERM_CHUNK), :],
                             v_ref[pl.ds(r0, PERM_CHUNK), :]], axis=1)
        y = jnp.dot(to_phase, x, preferred_element_type=F32).astype(BF16)
        for p4 in range(PHASES4):
            rows = slice(PIECE * p4, PIECE * (p4 + 1))
            qp[p4, ch] = y[rows, 0:LANES]
            kp[p4, ch] = y[rows, LANES:2 * LANES]
            vp[p4, ch] = y[rows, 2 * LANES:3 * LANES]
        return carry

    lax.fori_loop(0, n_chunk, permute_body, 0)

    pieces_per_blk = blk // PIECE
    n_blk4 = seq // (PHASES4 * blk)

    def dil4_body(g, carry):
        p4 = g // (n_blk4 // group)
        j0 = (g % (n_blk4 // group)) * group
        qs, ks, vs, bs = [], [], [], []
        for u in range(group):
            pc = pieces_per_blk * (j0 + u)
            cs = jnp.clip(pc - radius // PIECE, 0, n_chunk - span // PIECE)
            qs.append(qp[p4, pl.ds(pc, pieces_per_blk)].reshape(blk, LANES))
            ks.append(kp[p4, pl.ds(cs, span // PIECE)].reshape(span, LANES))
            vs.append(vp[p4, pl.ds(cs, span // PIECE)].reshape(span, LANES))
            bs.append(bias4[(pc - cs) // (radius // PIECE)])
        for u, (o, m, l) in enumerate(_attend_group(qs, ks, vs, bs)):
            dst = (p4, pl.ds(pieces_per_blk * (j0 + u), pieces_per_blk))
            acc_p[dst] = _merge_heads(o, blk).reshape(pieces_per_blk, PIECE, LANES)
            m_p[dst] = _merge_heads(m, blk).reshape(pieces_per_blk, PIECE, LANES)
            l_p[dst] = _merge_heads(l, blk).reshape(pieces_per_blk, PIECE, LANES)
        return carry

    lax.fori_loop(0, PHASES4 * n_blk4 // group, dil4_body, 0)

    chunks_per_blk = blk // 16

    def dil16_body(g, carry):
        p4 = g // 2
        qs, ks, vs, bs, dsts = [], [], [], [], []
        for mm in range(2):
            off = pl.multiple_of((2 * (g % 2) + mm) * 16, 16)
            kk = kp[p4, :, pl.ds(off, 16), :].reshape(span, LANES)
            vv = vp[p4, :, pl.ds(off, 16), :].reshape(span, LANES)
            for jb in range(n_chunk // chunks_per_blk):
                dst = (p4, pl.ds(chunks_per_blk * jb, chunks_per_blk), pl.ds(off, 16), slice(None))
                qs.append(qp[dst].reshape(blk, LANES))
                ks.append(kk)
                vs.append(vv)
                bs.append(bias16[jb * (blk // radius)])
                dsts.append(dst)
        for dst, (o, m, l) in zip(dsts, _attend_group(qs, ks, vs, bs)):
            m_old = m_p[dst].reshape(blk, LANES)
            m_new = jnp.maximum(m_old, _merge_heads(m, blk))
            w_old = jnp.exp(m_old - m_new)
            w_new = jnp.exp(_merge_heads(m, blk) - m_new)
            acc = w_old * acc_p[dst].reshape(blk, LANES) + w_new * _merge_heads(o, blk)
            den = w_old * l_p[dst].reshape(blk, LANES) + w_new * _merge_heads(l, blk)
            acc_p[dst] = acc.reshape(chunks_per_blk, 16, LANES)
            l_p[dst] = den.reshape(chunks_per_blk, 16, LANES)
            m_p[dst] = m_new.reshape(chunks_per_blk, 16, LANES)
        return carry

    lax.fori_loop(0, PHASES4 * 2, dil16_body, 0)

    blk_per_chunk = PERM_CHUNK // blk

    def dil1_body(g, carry):
        merged = []
        for cc2 in range(group // blk_per_chunk):
            ch = g * (group // blk_per_chunk) + cc2
            acc = jnp.concatenate([acc_p[p4, ch] for p4 in range(PHASES4)], axis=0)
            den = jnp.concatenate([l_p[p4, ch] for p4 in range(PHASES4)], axis=0)
            mx = jnp.concatenate([m_p[p4, ch] for p4 in range(PHASES4)], axis=0)
            o_ph = (acc * (1.0 / den)).astype(BF16)
            lse = mx + jnp.log(den)
            hi = lse.astype(BF16)
            rem = lse - hi.astype(F32)
            mid = rem.astype(BF16)
            low = (rem - mid.astype(F32)).astype(BF16)
            back = jnp.dot(to_token, jnp.concatenate([o_ph, hi, mid, low], axis=1),
                           preferred_element_type=F32)
            merged.append((back[:, 0:LANES], back[:, LANES:2 * LANES]
                           + back[:, 2 * LANES:3 * LANES] + back[:, 3 * LANES:4 * LANES]))
        qs, ks, vs, bs, starts = [], [], [], [], []
        for u in range(group):
            a = pl.multiple_of((g * group + u) * blk, blk)
            kst = pl.multiple_of(jnp.clip(a - radius, 0, seq - span), radius)
            qs.append(q_ref[pl.ds(a, blk), :])
            ks.append(k_ref[pl.ds(kst, span), :])
            vs.append(v_ref[pl.ds(kst, span), :])
            bs.append(bias1[(a - kst) // radius])
            starts.append(a)
        for u, (o, m, l) in enumerate(_attend_group(qs, ks, vs, bs)):
            o1 = _merge_heads(o * (1.0 / l), blk)
            lse1 = _merge_heads(m + jnp.log(l), blk)
            o2, lse2 = merged[u // blk_per_chunk]
            half = slice((u % blk_per_chunk) * blk, (u % blk_per_chunk + 1) * blk)
            o2, lse2 = o2[half], lse2[half]
            top = jnp.maximum(lse1, lse2)
            e1 = jnp.exp(lse1 - top)
            e2 = jnp.exp(lse2 - top)
            o_ref[pl.ds(starts[u], blk), :] = ((e1 * o1 + e2 * o2)
                                               * (1.0 / (e1 + e2))).astype(o_ref.dtype)
        return carry

    lax.fori_loop(0, seq // (blk * group), dil1_body, 0)


def _dilated_attention(qkv, params, *, k_blk0, v_blk0, n_pairs, blk, radius):
    b, s, _ = qkv.shape
    span = blk + 2 * radius
    n_chunk = s // PERM_CHUNK
    assert s // 16 == span and s % PERM_CHUNK == 0 and blk % PIECE == 0
    phase = lambda dt: pltpu.VMEM((PHASES4, n_chunk, PIECE, LANES), dt)
    return pl.pallas_call(
        functools.partial(_dilated_kernel, seq=s, blk=blk, radius=radius),
        grid=(b, n_pairs),
        in_specs=[
            pl.BlockSpec(memory_space=pltpu.SMEM),
            pl.BlockSpec((None, s, LANES), lambda bi, c: (bi, 0, c)),
            pl.BlockSpec((None, s, LANES), lambda bi, c: (bi, 0, k_blk0 + c)),
            pl.BlockSpec((None, s, LANES), lambda bi, c: (bi, 0, v_blk0 + c)),
        ],
        out_specs=pl.BlockSpec((None, s, LANES), lambda bi, c: (bi, 0, c)),
        out_shape=jax.ShapeDtypeStruct((b, s, n_pairs * LANES), BF16),
        scratch_shapes=[phase(BF16), phase(BF16), phase(BF16), phase(F32), phase(F32), phase(F32)]
        + [pltpu.VMEM((3, 2 * blk, span), F32)] * 3,
        compiler_params=pltpu.CompilerParams(
            dimension_semantics=("arbitrary", "arbitrary"), vmem_limit_bytes=VMEM_LIMIT),
        name="dilated_attn",
    )(params, qkv, qkv, qkv)


def _na_kernel(q_ref, k_ref, v_ref, bias_ref, o_ref, *, rows):
    kh = NA_ROWS
    group = ATTN_UNROLL
    lo_q = _lo_mask((1, LANES))
    lo = _lo_mask((GRID_W, LANES))

    def body(g, carry):
        starts, offs, values, scores = [], [], [], []
        for u in range(group):
            i = g * group + u
            rs = jnp.clip(i - kh // 2, 0, rows - kh)
            qs = pl.multiple_of(i * GRID_W, GRID_W)
            ks = pl.multiple_of(rs * GRID_W, GRID_W)
            q = q_ref[pl.ds(qs, GRID_W), :]
            zero = jnp.zeros_like(q)
            qm = jnp.concatenate([jnp.where(lo_q, q, zero), jnp.where(lo_q, zero, q)], axis=0)
            k = k_ref[pl.ds(ks, kh * GRID_W), :]
            scores.append(lax.dot_general(qm, k, (((1,), (1,)), ((), ())),
                                          preferred_element_type=F32))
            starts.append(qs)
            offs.append(rs - i + (NA_ROWS - 1))
            values.append(v_ref[pl.ds(ks, kh * GRID_W), :])
        probs, dens = [], []
        for u in range(group):
            s = scores[u] + jnp.concatenate(
                [bias_ref[offs[u] + 2 * j] for j in range(kh // 2)], axis=1)
            m = jnp.max(s, axis=-1, keepdims=True)
            p = jnp.exp(s - m)
            dens.append(jnp.sum(p, axis=-1, keepdims=True))
            probs.append(p.astype(BF16))
        for u in range(group):
            o = jnp.dot(probs[u], values[u], preferred_element_type=F32) * (1.0 / dens[u])
            o_ref[pl.ds(starts[u], GRID_W), :] = jnp.where(
                lo, o[:GRID_W], o[GRID_W:]).astype(o_ref.dtype)
        return carry

    lax.fori_loop(0, rows // group, body, 0)


def _na_bias_table(rpb):
    qc = np.arange(GRID_W)[:, None]
    kc = np.arange(GRID_W)[None, :]
    qstart = np.clip(qc - NA_COLS // 2, 0, GRID_W - NA_COLS)
    valid = (kc >= qstart) & (kc < qstart + NA_COLS)
    coff = np.clip(kc - qc, -(NA_COLS - 1), NA_COLS - 1) + NA_COLS - 1
    pick = (coff[:, :, None] == np.arange(2 * NA_COLS - 1)).astype(np.float32)
    t = jnp.einsum("hdj,qkj->hdqk", rpb.astype(F32), jnp.asarray(pick),
                   precision=lax.Precision.HIGHEST)
    t = jnp.where(jnp.asarray(valid), t, NEG_INF)
    t = jnp.concatenate([t[:, :-1], t[:, 1:]], axis=-1)
    n_off = 2 * NA_ROWS - 2
    t = t.reshape(C_HEADS // 2, 2, n_off, GRID_W, 2 * GRID_W).transpose(0, 2, 1, 3, 4)
    return t.reshape(C_HEADS // 2, n_off, 2 * GRID_W, 2 * GRID_W)


def _na_attention(qkv, bias):
    b, s, n_cols = qkv.shape
    n_pairs = C_HEADS // 2
    rows = s // GRID_W
    return pl.pallas_call(
        functools.partial(_na_kernel, rows=rows),
        grid=(b, n_pairs),
        in_specs=[
            pl.BlockSpec((None, s, LANES), lambda bi, c: (bi, 0, c)),
            pl.BlockSpec((None, s, LANES), lambda bi, c: (bi, 0, n_pairs + c)),
            pl.BlockSpec((None, s, LANES), lambda bi, c: (bi, 0, 2 * n_pairs + c)),
            pl.BlockSpec((None,) + bias.shape[1:], lambda bi, c: (c, 0, 0, 0)),
        ],
        out_specs=pl.BlockSpec((None, s, LANES), lambda bi, c: (bi, 0, c)),
        out_shape=jax.ShapeDtypeStruct((b, s, n_pairs * LANES), BF16),
        compiler_params=pltpu.CompilerParams(
            dimension_semantics=("arbitrary", "arbitrary"), vmem_limit_bytes=VMEM_LIMIT),
        name="na_attn",
    )(qkv, qkv, qkv, bias)


def _out_proj_kernel(x_ref, *refs):
    y_ref = refs[-1]
    n = (len(refs) - 1) // 2
    y = x_ref[...]
    for i in range(n):
        y = y + jnp.dot(refs[i][...], refs[n + i][...], preferred_element_type=F32)
    y_ref[...] = y


def _out_proj(x, os, ws, tm=512):
    t, d = x.shape
    row = lambda w: pl.BlockSpec((tm, w), lambda i: (i, 0))
    return pl.pallas_call(
        _out_proj_kernel,
        grid=(t // tm,),
        in_specs=[row(d)] + [row(o.shape[1]) for o in os]
        + [pl.BlockSpec(w.shape, lambda i: (0, 0)) for w in ws],
        out_specs=row(d),
        out_shape=jax.ShapeDtypeStruct((t, d), F32),
        compiler_params=pltpu.CompilerParams(
            dimension_semantics=("arbitrary",), vmem_limit_bytes=VMEM_LIMIT),
        name="out_proj",
    )(x, *os, *ws)


def _silu(g):
    return g * (1.0 / (1.0 + jnp.exp(-g)))


def _ffn_kernel(x_ref, g_ref, wg_ref, wu_ref, wd_ref, y_ref, a_scr):
    x = x_ref[...]
    h = _rms_rows(x, g_ref[...]).astype(BF16)
    f = wg_ref.shape[1]
    for j in range(f // MXU_N):
        sl = slice(j * MXU_N, (j + 1) * MXU_N)
        gate = jnp.dot(h, wg_ref[:, sl], preferred_element_type=F32)
        up = jnp.dot(h, wu_ref[:, sl], preferred_element_type=F32)
        a_scr[:, sl] = (_silu(gate) * up).astype(BF16)
    y_ref[...] = x + jnp.dot(a_scr[...], wd_ref[...], preferred_element_type=F32)


def _ffn(x, g, wg, wu, wd, tm=512):
    t, d = x.shape
    f = wg.shape[1]
    const = lambda a: pl.BlockSpec(a.shape, lambda i: (0, 0), pipeline_mode=pl.Buffered(1))
    return pl.pallas_call(
        _ffn_kernel,
        grid=(t // tm,),
        in_specs=[pl.BlockSpec((tm, d), lambda i: (i, 0)),
                  pl.BlockSpec((1, d), lambda i: (0, 0)),
                  const(wg), const(wu), const(wd)],
        out_specs=pl.BlockSpec((tm, d), lambda i: (i, 0)),
        out_shape=jax.ShapeDtypeStruct((t, d), F32),
        scratch_shapes=[pltpu.VMEM((tm, f), BF16)],
        compiler_params=pltpu.CompilerParams(
            dimension_semantics=("arbitrary",), vmem_limit_bytes=VMEM_LIMIT),
        name="dense_swiglu",
    )(x, g.reshape(1, d), wg, wu, wd)


TOP_K = 2
MOE_TB = 512
MOE_TM = 512
MOE_RC = 256
KEY_STRIDE = 65536.0


def _router_kernel(x_ref, g_ref, wr_ref, h_ref, info_ref, info_t_ref, cnt_ref, carry_scr):
    i = pl.program_id(0)
    tb = x_ref.shape[0]
    lane = lax.broadcasted_iota(jnp.int32, (1, LANES), 1)

    @pl.when(i == 0)
    def _():
        carry_scr[...] = jnp.zeros_like(carry_scr)

    h = _rms_rows(x_ref[...], g_ref[...])
    h_ref[...] = h.astype(BF16)
    logits = jnp.dot(h, wr_ref[...], preferred_element_type=F32,
                     precision=lax.Precision.HIGHEST)
    logits = jnp.where(lane < N_EXPERTS, logits, NEG_INF)
    v1 = jnp.max(logits, axis=-1, keepdims=True)
    i1 = jnp.min(jnp.where(logits == v1, lane, LANES), axis=-1, keepdims=True)
    rest = jnp.where(lane == i1, NEG_INF, logits)
    v2 = jnp.max(rest, axis=-1, keepdims=True)
    i2 = jnp.min(jnp.where(rest == v2, lane, LANES), axis=-1, keepdims=True)
    e2 = jnp.exp(v2 - v1)
    g1 = 1.0 / (1.0 + e2)
    g2 = e2 * g1

    pick1 = lane == i1
    pick2 = lane == i2
    picked = jnp.where(pick1, 1.0, jnp.where(pick2, 1.0, 0.0))
    r = lax.broadcasted_iota(jnp.int32, (tb, tb), 0)
    cidx = lax.broadcasted_iota(jnp.int32, (tb, tb), 1)
    ltri = jnp.where(cidx < r, 1.0, 0.0).astype(BF16)
    before = jnp.dot(ltri, picked.astype(BF16), preferred_element_type=F32) + carry_scr[...]
    rank1 = jnp.sum(jnp.where(pick1, before, 0.0), axis=-1, keepdims=True)
    rank2 = jnp.sum(jnp.where(pick2, before, 0.0), axis=-1, keepdims=True)
    key1 = i1.astype(F32) * KEY_STRIDE + rank1
    key2 = i2.astype(F32) * KEY_STRIDE + rank2
    info = jnp.where(lane == 0, key1, jnp.where(lane == 1, key2,
                     jnp.where(lane == 2, g1, jnp.where(lane == 3, g2, 0.0))))
    info_ref[...] = info
    info_t_ref[...] = info.T[:8, :]
    counts = jnp.sum(picked, axis=0, keepdims=True)
    cnt_ref[...] = jnp.broadcast_to(counts, cnt_ref.shape)
    carry_scr[...] += counts


def _router(x, g, wr):
    t, d = x.shape
    nb = t // MOE_TB
    return pl.pallas_call(
        _router_kernel,
        grid=(nb,),
        in_specs=[pl.BlockSpec((MOE_TB, d), lambda i: (i, 0)),
                  pl.BlockSpec((1, d), lambda i: (0, 0)),
                  pl.BlockSpec(wr.shape, lambda i: (0, 0))],
        out_specs=[pl.BlockSpec((MOE_TB, d), lambda i: (i, 0)),
                   pl.BlockSpec((MOE_TB, LANES), lambda i: (i, 0)),
                   pl.BlockSpec((None, 8, MOE_TB), lambda i: (i, 0, 0)),
                   pl.BlockSpec((None, 8, LANES), lambda i: (i, 0, 0))],
        out_shape=[jax.ShapeDtypeStruct((t, d), BF16),
                   jax.ShapeDtypeStruct((t, LANES), F32),
                   jax.ShapeDtypeStruct((nb, 8, MOE_TB), F32),
                   jax.ShapeDtypeStruct((nb, 8, LANES), F32)],
        scratch_shapes=[pltpu.VMEM((1, LANES), F32)],
        compiler_params=pltpu.CompilerParams(
            dimension_semantics=("arbitrary",), vmem_limit_bytes=VMEM_LIMIT),
        name="moe_router",
    )(x, g.reshape(1, d), wr)


def _moe_plan(block_counts, n_tokens):
    nb = block_counts.shape[0]
    n_tiles = (TOP_K * n_tokens + N_EXPERTS * (MOE_TM - 1)) // MOE_TM
    n_chunks = n_tiles * MOE_TM // MOE_RC
    n_pairs = ((n_chunks + N_EXPERTS * nb + 7) // 8) * 8
    tc = block_counts.astype(jnp.int32)
    cb = jnp.concatenate([jnp.zeros((1, N_EXPERTS), jnp.int32), jnp.cumsum(tc, axis=0)])
    counts = cb[-1]
    padded = ((counts + MOE_TM - 1) // MOE_TM) * MOE_TM
    ends = jnp.cumsum(padded)
    starts = ends - padded
    total = ends[-1]

    tiles = jnp.arange(n_tiles, dtype=jnp.int32)
    n_valid_tiles = total // MOE_TM
    t_valid = (tiles < n_valid_tiles).astype(jnp.int32)
    t_src = jnp.maximum(jnp.minimum(tiles, n_valid_tiles - 1), 0)
    t_exp = jnp.minimum(jnp.sum((t_src * MOE_TM)[:, None] >= ends[None, :], axis=1),
                        N_EXPERTS - 1).astype(jnp.int32)

    r0 = jnp.arange(n_chunks, dtype=jnp.int32) * MOE_RC
    c_exp = jnp.minimum(jnp.sum(r0[:, None] >= ends[None, :], axis=1), N_EXPERTS - 1)
    c_live = r0 < total
    lo = r0 - starts[c_exp]
    base = (c_exp * int(KEY_STRIDE) + lo).astype(jnp.int32)
    cbe = cb[:, c_exp]
    real = c_live[None, :] & (cbe[:-1] < (lo + MOE_RC)[None, :]) & (cbe[1:] > lo[None, :])
    empty = c_live & jnp.logical_not(jnp.any(real, axis=0))
    ov = real | (empty[None, :] & (jnp.arange(nb)[:, None] == 0))
    n_act = jnp.sum(ov).astype(jnp.int32)
    pidx = jnp.arange(n_pairs, dtype=jnp.int32)
    act = pidx < n_act

    def pair_list(flat, inner):
        idx = jnp.nonzero(flat, size=n_pairs, fill_value=0)[0].astype(jnp.int32)
        idx = jnp.where(act, idx, idx[n_act - 1])
        return idx // inner, idx % inner

    def flags(major):
        first = act & ((pidx == 0) | (major != jnp.roll(major, 1)))
        last = act & ((pidx == n_act - 1) | (major != jnp.roll(major, -1)))
        return (act.astype(jnp.int32) + 2 * first.astype(jnp.int32)
                + 4 * last.astype(jnp.int32))

    g_c, g_b = pair_list(ov.T.reshape(-1), nb)
    c_b, c_c = pair_list(ov.reshape(-1), n_chunks)
    g_cnt = jnp.sum(ov, axis=0).astype(jnp.int32)
    g_start = (jnp.cumsum(g_cnt) - g_cnt).astype(jnp.int32)
    chunks = jnp.arange(n_chunks, dtype=jnp.int32)
    g_src = jnp.maximum(jnp.minimum(chunks, total // MOE_RC - 1), 0)
    return dict(n_tiles=n_tiles, n_chunks=n_chunks, n_pairs=n_pairs,
                t_valid=t_valid, t_src=t_src, t_exp=t_exp, base=base,
                g_src=g_src, g_start=g_start, g_cnt=g_cnt, g_b=g_b,
                c_b=c_b, c_c=c_c, c_flags=flags(c_b))


def _gather_kernel(src_ref, start_ref, cnt_ref, pb_ref, base_ref, info_t_ref, h_ref,
                   xs_ref, gs_ref, acc_scr, gsum_scr):
    c = pl.program_id(0)
    tb = info_t_ref.shape[-1]

    @pl.when(src_ref[c] == c)
    def _():
        base = base_ref[c].astype(F32)
        first = start_ref[c]
        row = lax.broadcasted_iota(jnp.int32, (MOE_RC, tb), 0).astype(F32)
        acc_scr[...] = jnp.zeros_like(acc_scr)
        gsum_scr[...] = jnp.zeros_like(gsum_scr)

        def pair(i, carry):
            b = pb_ref[first + i]
            info = info_t_ref[b]
            m1 = (info[0:1, :] - base) == row
            m2 = (info[1:2, :] - base) == row
            sel = jnp.where(m1, 1.0, jnp.where(m2, 1.0, 0.0)).astype(BF16)
            gsum_scr[...] += jnp.sum(
                jnp.where(m1, info[2:3, :], jnp.where(m2, info[3:4, :], 0.0)),
                axis=-1, keepdims=True)
            acc_scr[...] += jnp.dot(sel, h_ref[pl.ds(pl.multiple_of(b * tb, tb), tb), :],
                                    preferred_element_type=F32)
            return carry

        lax.fori_loop(0, cnt_ref[c], pair, 0)
        xs_ref[...] = acc_scr[...].astype(xs_ref.dtype)
        gs_ref[...] = jnp.broadcast_to(gsum_scr[...], gs_ref.shape)


def _moe_gather(plan, info_t, h):
    t, d = h.shape
    rows = plan["n_chunks"] * MOE_RC
    chunk = lambda c, src, st, cn, pb, ba: (src[c], 0)
    grid_spec = pltpu.PrefetchScalarGridSpec(
        num_scalar_prefetch=5,
        grid=(plan["n_chunks"],),
        in_specs=[pl.BlockSpec(info_t.shape, lambda c, src, st, cn, pb, ba: (0, 0, 0),
                               pipeline_mode=pl.Buffered(1)),
                  pl.BlockSpec((t, d), lambda c, src, st, cn, pb, ba: (0, 0),
                               pipeline_mode=pl.Buffered(1))],
        out_specs=[pl.BlockSpec((MOE_RC, d), chunk), pl.BlockSpec((MOE_RC, LANES), chunk)],
        scratch_shapes=[pltpu.VMEM((MOE_RC, d), F32), pltpu.VMEM((MOE_RC, 1), F32)],
    )
    return pl.pallas_call(
        _gather_kernel,
        grid_spec=grid_spec,
        out_shape=[jax.ShapeDtypeStruct((rows, d), BF16),
                   jax.ShapeDtypeStruct((rows, LANES), F32)],
        compiler_params=pltpu.CompilerParams(
            dimension_semantics=("arbitrary",), vmem_limit_bytes=VMEM_LIMIT),
        name="moe_gather",
    )(plan["g_src"], plan["g_start"], plan["g_cnt"], plan["g_b"], plan["base"], info_t, h)


def _experts_kernel(src_ref, exp_ref, val_ref, xs_ref, gs_ref, wg_ref, wu_ref, wd_ref, ys_ref,
                    acc_scr):
    i = pl.program_id(0)
    j = pl.program_id(1)

    @pl.when(val_ref[i] == 1)
    def _():
        x = xs_ref[...]
        gate = jnp.dot(x, wg_ref[...], preferred_element_type=F32)
        up = jnp.dot(x, wu_ref[...], preferred_element_type=F32)
        a = (_silu(gate) * up).astype(BF16)
        y = jnp.dot(a, wd_ref[...], preferred_element_type=F32)

        @pl.when(j == 0)
        def _():
            acc_scr[...] = y

        @pl.when(j > 0)
        def _():
            acc_scr[...] += y

        @pl.when(j == pl.num_programs(1) - 1)
        def _():
            ys_ref[...] = (acc_scr[...] * gs_ref[:, 0:1]).astype(ys_ref.dtype)


def _moe_experts(plan, xs, gs, wg, wu, wd, tf):
    rows, d = xs.shape
    f = wg.shape[2]
    nf = f // tf

    def wcol(i, j, src, ex, val):
        return (ex[i], 0, jnp.where(val[i] == 1, j, nf - 1))

    def wrow(i, j, src, ex, val):
        return (ex[i], jnp.where(val[i] == 1, j, nf - 1), 0)

    tile = lambda i, j, src, ex, val: (src[i], 0)
    grid_spec = pltpu.PrefetchScalarGridSpec(
        num_scalar_prefetch=3,
        grid=(plan["n_tiles"], nf),
        in_specs=[pl.BlockSpec((MOE_TM, d), tile),
                  pl.BlockSpec((MOE_TM, LANES), tile),
                  pl.BlockSpec((None, d, tf), wcol),
                  pl.BlockSpec((None, d, tf), wcol),
                  pl.BlockSpec((None, tf, d), wrow)],
        out_specs=pl.BlockSpec((MOE_TM, d), tile),
        scratch_shapes=[pltpu.VMEM((MOE_TM, d), F32)],
    )
    return pl.pallas_call(
        _experts_kernel,
        grid_spec=grid_spec,
        out_shape=jax.ShapeDtypeStruct((rows, d), BF16),
        compiler_params=pltpu.CompilerParams(
            dimension_semantics=("arbitrary", "arbitrary"), vmem_limit_bytes=VMEM_LIMIT),
        name="moe_experts",
    )(plan["t_src"], plan["t_exp"], plan["t_valid"], xs, gs, wg, wu, wd)


def _combine_kernel(pb_ref, pc_ref, fl_ref, base_ref, info_ref, ys_ref, x_ref, o_ref):
    p = pl.program_id(0)
    fl = fl_ref[p]
    tb = x_ref.shape[0]

    @pl.when((fl // 2) % 2 == 1)
    def _():
        o_ref[...] = x_ref[...]

    @pl.when(fl % 2 == 1)
    def _():
        base = base_ref[pc_ref[p]].astype(F32)
        lane = lax.broadcasted_iota(jnp.int32, (1, LANES), 1)
        info = info_ref[...]
        k1 = jnp.sum(jnp.where(lane == 0, info, 0.0), axis=-1, keepdims=True) - base
        k2 = jnp.sum(jnp.where(lane == 1, info, 0.0), axis=-1, keepdims=True) - base
        col = lax.broadcasted_iota(jnp.int32, (tb, MOE_RC), 1).astype(F32)
        sel = jnp.where(k1 == col, 1.0, jnp.where(k2 == col, 1.0, 0.0)).astype(BF16)
        o_ref[...] += jnp.dot(sel, ys_ref[...], preferred_element_type=F32)


def _moe_combine(plan, info, ys, x):
    t, d = x.shape
    grid_spec = pltpu.PrefetchScalarGridSpec(
        num_scalar_prefetch=4,
        grid=(plan["n_pairs"],),
        in_specs=[pl.BlockSpec((MOE_TB, LANES), lambda p, pb, pc, fl, ba: (pb[p], 0)),
                  pl.BlockSpec((MOE_RC, d), lambda p, pb, pc, fl, ba: (pc[p], 0)),
                  pl.BlockSpec((MOE_TB, d), lambda p, pb, pc, fl, ba: (pb[p], 0))],
        out_specs=pl.BlockSpec((MOE_TB, d), lambda p, pb, pc, fl, ba: (pb[p], 0)),
    )
    return pl.pallas_call(
        _combine_kernel,
        grid_spec=grid_spec,
        out_shape=jax.ShapeDtypeStruct((t, d), F32),
        compiler_params=pltpu.CompilerParams(
            dimension_semantics=("arbitrary",), vmem_limit_bytes=VMEM_LIMIT),
        name="moe_combine",
    )(plan["c_b"], plan["c_c"], plan["c_flags"], plan["base"], info, ys, x)


def _moe(x, g, wr, wg, wu, wd, tf=1792):
    t, _ = x.shape
    h, info, info_t, block_counts = _router(x, g, wr)
    plan = _moe_plan(block_counts[:, 0, :N_EXPERTS], t)
    xs, gs = _moe_gather(plan, info_t, h)
    ys = _moe_experts(plan, xs, gs, wg, wu, wd, tf)
    return _moe_combine(plan, info, ys, x)


def _alibi_slopes():
    s = np.exp2(-8.0 * np.arange(1, N_HEADS + 1) / N_HEADS).astype(np.float32)
    return s[0::2][:A_HEADS], s[1::2][:B_Q_HEADS]


_B_HEAD_ORDER = np.array([0, 4, 1, 5, 2, 6, 3, 7])


def _even_layer(x2d, batch, norm1, w_in, qn_a, kn_a, qn_b, kn_b, sink_b, w_out,
                norm2, w_gate, w_up, w_down):
    t, d = x2d.shape
    s = t // batch
    wa = A_HEADS * HEAD_DIM
    wq_b = B_Q_HEADS * HEAD_DIM
    wkv_b = B_KV_HEADS * HEAD_DIM
    qb0 = 3 * wa
    q_scale = HEAD_DIM ** -0.5

    w_qb = w_in[:, qb0:qb0 + wq_b].reshape(d, B_Q_HEADS, HEAD_DIM)[:, _B_HEAD_ORDER]
    w_in_p = jnp.concatenate(
        [w_in[:, :qb0], w_qb.reshape(d, wq_b), w_in[:, qb0 + wq_b:]], axis=1).astype(BF16)
    ones = jnp.ones((HEAD_DIM,), F32)
    col_gain = jnp.concatenate([
        jnp.tile(qn_a * q_scale, A_HEADS), jnp.tile(kn_a, A_HEADS), jnp.tile(ones, A_HEADS),
        jnp.tile(qn_b * q_scale, B_Q_HEADS), jnp.tile(kn_b, B_KV_HEADS),
        jnp.tile(ones, B_KV_HEADS)])
    cpl = wa // LANES
    norm_chunks = ((True,) * (2 * cpl) + (False,) * cpl + (True,) * (wq_b // LANES)
                   + (True,) * (wkv_b // LANES) + (False,) * (wkv_b // LANES))
    qkv = _norm_proj(x2d, norm1, w_in_p, col_gain, norm_chunks)
    n_cols = qkv.shape[1]
    qkv3 = qkv.reshape(batch, s, n_cols)

    slopes_a, slopes_b = _alibi_slopes()
    par_a = jnp.asarray(slopes_a).reshape(1, A_HEADS)
    radii = {window // (2 * dil) for window, dil in DIL_CONFIGS}
    assert tuple(dil for _, dil in DIL_CONFIGS) == (1, 4, 16) and len(radii) == 1
    oa = _dilated_attention(qkv3, par_a, k_blk0=cpl, v_blk0=2 * cpl, n_pairs=cpl,
                            blk=A_BLOCK, radius=radii.pop())
    par_b = jnp.stack([jnp.asarray(slopes_b[_B_HEAD_ORDER]),
                       sink_b.astype(F32)[_B_HEAD_ORDER]])
    kb = 3 * cpl + wq_b // LANES
    ob = _window_attention(qkv3, par_b, q_blk0=3 * cpl, k_blk=kb, v_blk=kb + wkv_b // LANES,
                           n_pairs=wq_b // LANES, radius=B_WINDOW, blk=B_BLOCK)
    w_out_a = w_out[:wa].astype(BF16)
    w_out_b = w_out[wa:].reshape(B_Q_HEADS, HEAD_DIM, d)[_B_HEAD_ORDER].reshape(wq_b, d)
    x1 = _out_proj(x2d, [oa.reshape(t, wa), ob.reshape(t, wq_b)],
                   [w_out_a, w_out_b.astype(BF16)])
    return _ffn(x1, norm2, w_gate.astype(BF16), w_up.astype(BF16), w_down.astype(BF16))


def _odd_layer(x2d, batch, norm1, w_qkv, qn, kn, rpb, w_out, norm2, w_router,
               w_gate, w_up, w_down):
    t, d = x2d.shape
    s = t // batch
    wc = C_HEADS * HEAD_DIM
    ones = jnp.ones((HEAD_DIM,), F32)
    col_gain = jnp.concatenate([jnp.tile(qn * HEAD_DIM ** -0.5, C_HEADS),
                                jnp.tile(kn, C_HEADS), jnp.tile(ones, C_HEADS)])
    cpl = wc // LANES
    norm_chunks = (True,) * (2 * cpl) + (False,) * cpl
    qkv = _norm_proj(x2d, norm1, w_qkv.astype(BF16), col_gain, norm_chunks)
    o = _na_attention(qkv.reshape(batch, s, 3 * wc), _na_bias_table(rpb))
    x1 = _out_proj(x2d, [o.reshape(t, wc)], [w_out.astype(BF16)])
    wr = jnp.zeros((d, LANES), F32).at[:, :N_EXPERTS].set(w_router.astype(F32))
    return _moe(x1, norm2, wr, w_gate.astype(BF16), w_up.astype(BF16), w_down.astype(BF16))


def kernel(x, ev_norm1, ev_w_in, ev_qn_a, ev_kn_a, ev_qn_b, ev_kn_b, ev_sink_b, ev_w_out, ev_norm2, ev_ffn_gate, ev_ffn_up, ev_ffn_down, od_norm1, od_w_qkv, od_qn, od_kn, od_rpb, od_w_out, od_norm2, od_router, od_exp_gate, od_exp_up, od_exp_down):
    batch, s, d = x.shape
    depth = ev_norm1.shape[0] + od_norm1.shape[0]
    h = x.reshape(batch * s, d)
    for layer in range(depth):
        j = layer // 2
        if layer % 2 == 0:
            h = _even_layer(h, batch, ev_norm1[j], ev_w_in[j], ev_qn_a[j], ev_kn_a[j],
                            ev_qn_b[j], ev_kn_b[j], ev_sink_b[j], ev_w_out[j], ev_norm2[j],
                            ev_ffn_gate[j], ev_ffn_up[j], ev_ffn_down[j])
        else:
            h = _odd_layer(h, batch, od_norm1[j], od_w_qkv[j], od_qn[j], od_kn[j], od_rpb[j],
                           od_w_out[j], od_norm2[j], od_router[j], od_exp_gate[j],
                           od_exp_up[j], od_exp_down[j])
    return h.reshape(batch, s, d)
```

```python
import functools

import numpy as np
import jax
import jax.numpy as jnp
from jax import lax
from jax.experimental import pallas as pl
from jax.experimental.pallas import tpu as pltpu

D_MODEL = 1024
HEAD_DIM = 64
N_HEADS = D_MODEL // HEAD_DIM
A_HEADS = N_HEADS // 2
B_Q_HEADS = N_HEADS // 2
B_KV_HEADS = max(1, B_Q_HEADS // 4)
C_HEADS = N_HEADS
DIL_CONFIGS = ((128, 1), (512, 4), (2048, 16))
A_BLOCK = 128
B_WINDOW = 128
B_BLOCK = 128
GRID_W = 64
NA_ROWS = 8
NA_COLS = 16
N_EXPERTS = 8
RMS_EPS = 1e-6
NEG_INF = -1e30

LANES = 128
MXU_N = 256
VMEM_LIMIT = 56 * 1024 * 1024

ATTN_UNROLL = 8

F32 = jnp.float32
BF16 = jnp.bfloat16


def _lo_mask(shape):
    return lax.broadcasted_iota(jnp.int32, shape, len(shape) - 1) < HEAD_DIM


def _rms_rows(x, g):
    ms = jnp.mean(x * x, axis=-1, keepdims=True)
    return x * lax.rsqrt(ms + RMS_EPS) * g


def _norm_proj_kernel(x_ref, g_ref, w_ref, cg_ref, o_ref, *, norm_chunks):
    h = _rms_rows(x_ref[...], g_ref[...]).astype(BF16)
    n_out = o_ref.shape[-1]
    lo = _lo_mask((1, LANES))
    for j in range(n_out // MXU_N):
        y = jnp.dot(h, w_ref[:, j * MXU_N:(j + 1) * MXU_N], preferred_element_type=F32)
        for half in range(MXU_N // LANES):
            c = j * (MXU_N // LANES) + half
            yc = y[:, half * LANES:(half + 1) * LANES]
            if norm_chunks[c]:
                sq = yc * yc
                s_lo = jnp.sum(jnp.where(lo, sq, 0.0), axis=-1, keepdims=True)
                s_hi = jnp.sum(jnp.where(lo, 0.0, sq), axis=-1, keepdims=True)
                inv = jnp.where(lo, lax.rsqrt(s_lo * (1.0 / HEAD_DIM) + RMS_EPS),
                                lax.rsqrt(s_hi * (1.0 / HEAD_DIM) + RMS_EPS))
                yc = yc * inv * cg_ref[:, c * LANES:(c + 1) * LANES]
            o_ref[:, c * LANES:(c + 1) * LANES] = yc.astype(o_ref.dtype)


def _norm_proj(x, g, w, col_gain, norm_chunks, tm=512):
    t, d = x.shape
    n = w.shape[1]
    return pl.pallas_call(
        functools.partial(_norm_proj_kernel, norm_chunks=norm_chunks),
        grid=(t // tm,),
        in_specs=[
            pl.BlockSpec((tm, d), lambda i: (i, 0)),
            pl.BlockSpec((1, d), lambda i: (0, 0)),
            pl.BlockSpec((d, n), lambda i: (0, 0)),
            pl.BlockSpec((1, n), lambda i: (0, 0)),
        ],
        out_specs=pl.BlockSpec((tm, n), lambda i: (i, 0)),
        out_shape=jax.ShapeDtypeStruct((t, n), BF16),
        compiler_params=pltpu.CompilerParams(
            dimension_semantics=("arbitrary",), vmem_limit_bytes=VMEM_LIMIT),
        name="norm_proj",
    )(x, g.reshape(1, d), w, col_gain.reshape(1, n))


def _window_kernel(par_ref, q_ref, k_ref, v_ref, o_ref, bias_scr, *, seq, blk, radius):
    span = blk + 2 * radius
    c = pl.program_id(1)
    group = ATTN_UNROLL
    row_lo = lax.broadcasted_iota(jnp.int32, (2 * blk, 1), 0) < blk
    slope_col = jnp.where(row_lo, par_ref[0, 2 * c], par_ref[0, 2 * c + 1])
    sink_col = jnp.where(row_lo, par_ref[1, 2 * c], par_ref[1, 2 * c + 1])

    iq = lax.broadcasted_iota(jnp.int32, (2 * blk, span), 0) % blk
    ik = lax.broadcasted_iota(jnp.int32, (2 * blk, span), 1)
    _fill_band_bias(bias_scr, iq, ik, radius, 1, slope_col)
    lo_q = _lo_mask((1, LANES))
    lo = _lo_mask((blk, LANES))

    def body(g, carry):
        starts, variants, values, scores = [], [], [], []
        for u in range(group):
            a = pl.multiple_of((g * group + u) * blk, blk)
            ks = pl.multiple_of(jnp.clip(a - radius, 0, seq - span), radius)
            q = q_ref[pl.ds(a, blk), :]
            zero = jnp.zeros_like(q)
            qm = jnp.concatenate([jnp.where(lo_q, q, zero), jnp.where(lo_q, zero, q)], axis=0)
            k = k_ref[pl.ds(ks, span), :]
            scores.append(lax.dot_general(qm, k, (((1,), (1,)), ((), ())),
                                          preferred_element_type=F32))
            starts.append(a)
            variants.append((a - ks) // radius)
            values.append(v_ref[pl.ds(ks, span), :])
        probs, dens = [], []
        for u in range(group):
            s = scores[u] + bias_scr[variants[u]]
            m = jnp.maximum(jnp.max(s, axis=-1, keepdims=True), sink_col)
            p = jnp.exp(s - m)
            dens.append(jnp.sum(p, axis=-1, keepdims=True) + jnp.exp(sink_col - m))
            probs.append(p.astype(BF16))
        for u in range(group):
            o = jnp.dot(probs[u], values[u], preferred_element_type=F32) * (1.0 / dens[u])
            o_ref[pl.ds(starts[u], blk), :] = jnp.where(lo, o[:blk], o[blk:]).astype(o_ref.dtype)
        return carry

    lax.fori_loop(0, seq // (blk * group), body, 0)


def _window_attention(qkv, params, *, q_blk0, k_blk, v_blk, n_pairs, radius, blk):
    b, s, _ = qkv.shape
    span = blk + 2 * radius
    return pl.pallas_call(
        functools.partial(_window_kernel, seq=s, blk=blk, radius=radius),
        grid=(b, n_pairs),
        in_specs=[
            pl.BlockSpec(memory_space=pltpu.SMEM),
            pl.BlockSpec((None, s, LANES), lambda bi, c: (bi, 0, q_blk0 + c)),
            pl.BlockSpec((None, s, LANES), lambda bi, c: (bi, 0, k_blk)),
            pl.BlockSpec((None, s, LANES), lambda bi, c: (bi, 0, v_blk)),
        ],
        out_specs=pl.BlockSpec((None, s, LANES), lambda bi, c: (bi, 0, c)),
        out_shape=jax.ShapeDtypeStruct((b, s, n_pairs * LANES), BF16),
        scratch_shapes=[pltpu.VMEM((3, 2 * blk, span), F32)],
        compiler_params=pltpu.CompilerParams(
            dimension_semantics=("arbitrary", "arbitrary"), vmem_limit_bytes=VMEM_LIMIT),
        name="window_attn",
    )(params, qkv, qkv, qkv)


PERM_CHUNK = 256
PHASES4 = 4
PIECE = PERM_CHUNK // PHASES4


def _stack_heads(q):
    lo = _lo_mask((1, LANES))
    zero = jnp.zeros_like(q)
    return jnp.concatenate([jnp.where(lo, q, zero), jnp.where(lo, zero, q)], axis=0)


def _merge_heads(x, blk):
    x = jnp.broadcast_to(x, (2 * blk, LANES))
    return jnp.where(_lo_mask((blk, LANES)), x[:blk], x[blk:])


def _attend_group(qs, ks, vs, biases):
    scores = [lax.dot_general(_stack_heads(q), k, (((1,), (1,)), ((), ())),
                              preferred_element_type=F32) for q, k in zip(qs, ks)]
    stats = []
    for s, bias in zip(scores, biases):
        s = s + bias
        m = jnp.max(s, axis=-1, keepdims=True)
        p = jnp.exp(s - m)
        stats.append((p.astype(BF16), m, jnp.sum(p, axis=-1, keepdims=True)))
    return [(jnp.dot(p, v, preferred_element_type=F32), m, l)
            for (p, m, l), v in zip(stats, vs)]


def _fill_band_bias(bias_scr, pos_q, pos_k, radius, pos_scale, slope_col):
    for var in range(3):
        dist = jnp.abs(pos_q + var * radius - pos_k)
        dist_f = (dist * pos_scale).astype(F32)
        bias_scr[var] = jnp.where(dist <= radius, -(slope_col * dist_f), NEG_INF)


def _dilated_kernel(par_ref, q_ref, k_ref, v_ref, o_ref, qp, kp, vp, acc_p, m_p, l_p,
                    bias1, bias4, bias16, *, seq, blk, radius):
    span = blk + 2 * radius
    n_chunk = seq // PERM_CHUNK
    group = ATTN_UNROLL
    c = pl.program_id(1)
    row_lo = lax.broadcasted_iota(jnp.int32, (2 * blk, 1), 0) < blk
    slope_col = jnp.where(row_lo, par_ref[0, 2 * c], par_ref[0, 2 * c + 1])

    def piece_pos(x):
        return PIECE * (x // PIECE) + PHASES4 * (x % 16) + (x % PIECE) // 16

    iq = lax.broadcasted_iota(jnp.int32, (2 * blk, span), 0) % blk
    ik = lax.broadcasted_iota(jnp.int32, (2 * blk, span), 1)
    _fill_band_bias(bias1, iq, ik, radius, 1, slope_col)
    _fill_band_bias(bias4, piece_pos(iq), piece_pos(ik), radius, 4, slope_col)
    _fill_band_bias(bias16, iq, ik, radius, 16, slope_col)

    rr = lax.broadcasted_iota(jnp.int32, (PERM_CHUNK, PERM_CHUNK), 0)
    cc = lax.broadcasted_iota(jnp.int32, (PERM_CHUNK, PERM_CHUNK), 1)
    to_phase = jnp.where(cc == 16 * (rr % 16) + PHASES4 * ((rr // 16) % 4) + rr // PIECE,
                         1.0, 0.0).astype(BF16)
    to_token = jnp.where(cc == PIECE * (rr % PHASES4) + 16 * ((rr // PHASES4) % 4) + rr // 16,
                         1.0, 0.0).astype(BF16)

    def permute_body(ch, carry):
        r0 = pl.multiple_of(ch * PERM_CHUNK, PERM_CHUNK)
        x = jnp.concatenate([q_ref[pl.ds(r0, PERM_CHUNK), :], k_ref[pl.ds(r0, PERM_CHUNK), :],
                             v_ref[pl.ds(r0, PERM_CHUNK), :]], axis=1)
        y = jnp.dot(to_phase, x, preferred_element_type=F32).astype(BF16)
        for p4 in range(PHASES4):
            rows = slice(PIECE * p4, PIECE * (p4 + 1))
            qp[p4, ch] = y[rows, 0:LANES]
            kp[p4, ch] = y[rows, LANES:2 * LANES]
            vp[p4, ch] = y[rows, 2 * LANES:3 * LANES]
        return carry

    lax.fori_loop(0, n_chunk, permute_body, 0)

    pieces_per_blk = blk // PIECE
    n_blk4 = seq // (PHASES4 * blk)

    def dil4_body(g, carry):
        p4 = g // (n_blk4 // group)
        j0 = (g % (n_blk4 // group)) * group
        qs, ks, vs, bs = [], [], [], []
        for u in range(group):
            pc = pieces_per_blk * (j0 + u)
            cs = jnp.clip(pc - radius // PIECE, 0, n_chunk - span // PIECE)
            qs.append(qp[p4, pl.ds(pc, pieces_per_blk)].reshape(blk, LANES))
            ks.append(kp[p4, pl.ds(cs, span // PIECE)].reshape(span, LANES))
            vs.append(vp[p4, pl.ds(cs, span // PIECE)].reshape(span, LANES))
            bs.append(bias4[(pc - cs) // (radius // PIECE)])
        for u, (o, m, l) in enumerate(_attend_group(qs, ks, vs, bs)):
            dst = (p4, pl.ds(pieces_per_blk * (j0 + u), pieces_per_blk))
            acc_p[dst] = _merge_heads(o, blk).reshape(pieces_per_blk, PIECE, LANES)
            m_p[dst] = _merge_heads(m, blk).reshape(pieces_per_blk, PIECE, LANES)
            l_p[dst] = _merge_heads(l, blk).reshape(pieces_per_blk, PIECE, LANES)
        return carry

    lax.fori_loop(0, PHASES4 * n_blk4 // group, dil4_body, 0)

    chunks_per_blk = blk // 16

    def dil16_body(g, carry):
        p4 = g // 2
        qs, ks, vs, bs, dsts = [], [], [], [], []
        for mm in range(2):
            off = pl.multiple_of((2 * (g % 2) + mm) * 16, 16)
            kk = kp[p4, :, pl.ds(off, 16), :].reshape(span, LANES)
            vv = vp[p4, :, pl.ds(off, 16), :].reshape(span, LANES)
            for jb in range(n_chunk // chunks_per_blk):
                dst = (p4, pl.ds(chunks_per_blk * jb, chunks_per_blk), pl.ds(off, 16), slice(None))
                qs.append(qp[dst].reshape(blk, LANES))
                ks.append(kk)
                vs.append(vv)
                bs.append(bias16[jb * (blk // radius)])
                dsts.append(dst)
        for dst, (o, m, l) in zip(dsts, _attend_group(qs, ks, vs, bs)):
            m_old = m_p[dst].reshape(blk, LANES)
            m_new = jnp.maximum(m_old, _merge_heads(m, blk))
            w_old = jnp.exp(m_old - m_new)
            w_new = jnp.exp(_merge_heads(m, blk) - m_new)
            acc = w_old * acc_p[dst].reshape(blk, LANES) + w_new * _merge_heads(o, blk)
            den = w_old * l_p[dst].reshape(blk, LANES) + w_new * _merge_heads(l, blk)
            acc_p[dst] = acc.reshape(chunks_per_blk, 16, LANES)
            l_p[dst] = den.reshape(chunks_per_blk, 16, LANES)
            m_p[dst] = m_new.reshape(chunks_per_blk, 16, LANES)
        return carry

    lax.fori_loop(0, PHASES4 * 2, dil16_body, 0)

    blk_per_chunk = PERM_CHUNK // blk

    def dil1_body(g, carry):
        merged = []
        for cc2 in range(group // blk_per_chunk):
            ch = g * (group // blk_per_chunk) + cc2
            acc = jnp.concatenate([acc_p[p4, ch] for p4 in range(PHASES4)], axis=0)
            den = jnp.concatenate([l_p[p4, ch] for p4 in range(PHASES4)], axis=0)
            mx = jnp.concatenate([m_p[p4, ch] for p4 in range(PHASES4)], axis=0)
            o_ph = (acc * (1.0 / den)).astype(BF16)
            lse = mx + jnp.log(den)
            hi = lse.astype(BF16)
            rem = lse - hi.astype(F32)
            mid = rem.astype(BF16)
            low = (rem - mid.astype(F32)).astype(BF16)
            back = jnp.dot(to_token, jnp.concatenate([o_ph, hi, mid, low], axis=1),
                           preferred_element_type=F32)
            merged.append((back[:, 0:LANES], back[:, LANES:2 * LANES]
                           + back[:, 2 * LANES:3 * LANES] + back[:, 3 * LANES:4 * LANES]))
        qs, ks, vs, bs, starts = [], [], [], [], []
        for u in range(group):
            a = pl.multiple_of((g * group + u) * blk, blk)
            kst = pl.multiple_of(jnp.clip(a - radius, 0, seq - span), radius)
            qs.append(q_ref[pl.ds(a, blk), :])
            ks.append(k_ref[pl.ds(kst, span), :])
            vs.append(v_ref[pl.ds(kst, span), :])
            bs.append(bias1[(a - kst) // radius])
            starts.append(a)
        for u, (o, m, l) in enumerate(_attend_group(qs, ks, vs, bs)):
            o1 = _merge_heads(o * (1.0 / l), blk)
            lse1 = _merge_heads(m + jnp.log(l), blk)
            o2, lse2 = merged[u // blk_per_chunk]
            half = slice((u % blk_per_chunk) * blk, (u % blk_per_chunk + 1) * blk)
            o2, lse2 = o2[half], lse2[half]
            top = jnp.maximum(lse1, lse2)
            e1 = jnp.exp(lse1 - top)
            e2 = jnp.exp(lse2 - top)
            o_ref[pl.ds(starts[u], blk), :] = ((e1 * o1 + e2 * o2)
                                               * (1.0 / (e1 + e2))).astype(o_ref.dtype)
        return carry

    lax.fori_loop(0, seq // (blk * group), dil1_body, 0)


def _dilated_attention(qkv, params, *, k_blk0, v_blk0, n_pairs, blk, radius):
    b, s, _ = qkv.shape
    span = blk + 2 * radius
    n_chunk = s // PERM_CHUNK
    assert s // 16 == span and s % PERM_CHUNK == 0 and blk % PIECE == 0
    phase = lambda dt: pltpu.VMEM((PHASES4, n_chunk, PIECE, LANES), dt)
    return pl.pallas_call(
        functools.partial(_dilated_kernel, seq=s, blk=blk, radius=radius),
        grid=(b, n_pairs),
        in_specs=[
            pl.BlockSpec(memory_space=pltpu.SMEM),
            pl.BlockSpec((None, s, LANES), lambda bi, c: (bi, 0, c)),
            pl.BlockSpec((None, s, LANES), lambda bi, c: (bi, 0, k_blk0 + c)),
            pl.BlockSpec((None, s, LANES), lambda bi, c: (bi, 0, v_blk0 + c)),
        ],
        out_specs=pl.BlockSpec((None, s, LANES), lambda bi, c: (bi, 0, c)),
        out_shape=jax.ShapeDtypeStruct((b, s, n_pairs * LANES), BF16),
        scratch_shapes=[phase(BF16), phase(BF16), phase(BF16), phase(F32), phase(F32), phase(F32)]
        + [pltpu.VMEM((3, 2 * blk, span), F32)] * 3,
        compiler_params=pltpu.CompilerParams(
            dimension_semantics=("arbitrary", "arbitrary"), vmem_limit_bytes=VMEM_LIMIT),
        name="dilated_attn",
    )(params, qkv, qkv, qkv)


def _na_kernel(q_ref, k_ref, v_ref, bias_ref, o_ref, *, rows):
    kh = NA_ROWS
    group = ATTN_UNROLL
    lo_q = _lo_mask((1, LANES))
    lo = _lo_mask((GRID_W, LANES))

    def body(g, carry):
        starts, offs, values, scores = [], [], [], []
        for u in range(group):
            i = g * group + u
            rs = jnp.clip(i - kh // 2, 0, rows - kh)
            qs = pl.multiple_of(i * GRID_W, GRID_W)
            ks = pl.multiple_of(rs * GRID_W, GRID_W)
            q = q_ref[pl.ds(qs, GRID_W), :]
            zero = jnp.zeros_like(q)
            qm = jnp.concatenate([jnp.where(lo_q, q, zero), jnp.where(lo_q, zero, q)], axis=0)
            k = k_ref[pl.ds(ks, kh * GRID_W), :]
            scores.append(lax.dot_general(qm, k, (((1,), (1,)), ((), ())),
                                          preferred_element_type=F32))
            starts.append(qs)
            offs.append(rs - i + (NA_ROWS - 1))
            values.append(v_ref[pl.ds(ks, kh * GRID_W), :])
        probs, dens = [], []
        for u in range(group):
            s = scores[u] + jnp.concatenate(
                [bias_ref[offs[u] + 2 * j] for j in range(kh // 2)], axis=1)
            m = jnp.max(s, axis=-1, keepdims=True)
            p = jnp.exp(s - m)
            dens.append(jnp.sum(p, axis=-1, keepdims=True))
            probs.append(p.astype(BF16))
        for u in range(group):
            o = jnp.dot(probs[u], values[u], preferred_element_type=F32) * (1.0 / dens[u])
            o_ref[pl.ds(starts[u], GRID_W), :] = jnp.where(
                lo, o[:GRID_W], o[GRID_W:]).astype(o_ref.dtype)
        return carry

    lax.fori_loop(0, rows // group, body, 0)


def _na_bias_table(rpb):
    qc = np.arange(GRID_W)[:, None]
    kc = np.arange(GRID_W)[None, :]
    qstart = np.clip(qc - NA_COLS // 2, 0, GRID_W - NA_COLS)
    valid = (kc >= qstart) & (kc < qstart + NA_COLS)
    coff = np.clip(kc - qc, -(NA_COLS - 1), NA_COLS - 1) + NA_COLS - 1
    pick = (coff[:, :, None] == np.arange(2 * NA_COLS - 1)).astype(np.float32)
    t = jnp.einsum("hdj,qkj->hdqk", rpb.astype(F32), jnp.asarray(pick),
                   precision=lax.Precision.HIGHEST)
    t = jnp.where(jnp.asarray(valid), t, NEG_INF)
    t = jnp.concatenate([t[:, :-1], t[:, 1:]], axis=-1)
    n_off = 2 * NA_ROWS - 2
    t = t.reshape(C_HEADS // 2, 2, n_off, GRID_W, 2 * GRID_W).transpose(0, 2, 1, 3, 4)
    return t.reshape(C_HEADS // 2, n_off, 2 * GRID_W, 2 * GRID_W)


def _na_attention(qkv, bias):
    b, s, n_cols = qkv.shape
    n_pairs = C_HEADS // 2
    rows = s // GRID_W
    return pl.pallas_call(
        functools.partial(_na_kernel, rows=rows),
        grid=(b, n_pairs),
        in_specs=[
            pl.BlockSpec((None, s, LANES), lambda bi, c: (bi, 0, c)),
            pl.BlockSpec((None, s, LANES), lambda bi, c: (bi, 0, n_pairs + c)),
            pl.BlockSpec((None, s, LANES), lambda bi, c: (bi, 0, 2 * n_pairs + c)),
            pl.BlockSpec((None,) + bias.shape[1:], lambda bi, c: (c, 0, 0, 0)),
        ],
        out_specs=pl.BlockSpec((None, s, LANES), lambda bi, c: (bi, 0, c)),
        out_shape=jax.ShapeDtypeStruct((b, s, n_pairs * LANES), BF16),
        compiler_params=pltpu.CompilerParams(
            dimension_semantics=("arbitrary", "arbitrary"), vmem_limit_bytes=VMEM_LIMIT),
        name="na_attn",
    )(qkv, qkv, qkv, bias)


def _out_proj_kernel(x_ref, *refs):
    y_ref = refs[-1]
    n = (len(refs) - 1) // 2
    y = x_ref[...]
    for i in range(n):
        y = y + jnp.dot(refs[i][...], refs[n + i][...], preferred_element_type=F32)
    y_ref[...] = y


def _out_proj(x, os, ws, tm=512):
    t, d = x.shape
    row = lambda w: pl.BlockSpec((tm, w), lambda i: (i, 0))
    return pl.pallas_call(
        _out_proj_kernel,
        grid=(t // tm,),
        in_specs=[row(d)] + [row(o.shape[1]) for o in os]
        + [pl.BlockSpec(w.shape, lambda i: (0, 0)) for w in ws],
        out_specs=row(d),
        out_shape=jax.ShapeDtypeStruct((t, d), F32),
        compiler_params=pltpu.CompilerParams(
            dimension_semantics=("arbitrary",), vmem_limit_bytes=VMEM_LIMIT),
        name="out_proj",
    )(x, *os, *ws)


def _silu(g):
    return g * (1.0 / (1.0 + jnp.exp(-g)))


def _ffn_kernel(x_ref, g_ref, wg_ref, wu_ref, wd_ref, y_ref, a_scr):
    x = x_ref[...]
    h = _rms_rows(x, g_ref[...]).astype(BF16)
    f = wg_ref.shape[1]
    for j in range(f // MXU_N):
        sl = slice(j * MXU_N, (j + 1) * MXU_N)
        gate = jnp.dot(h, wg_ref[:, sl], preferred_element_type=F32)
        up = jnp.dot(h, wu_ref[:, sl], preferred_element_type=F32)
        a_scr[:, sl] = (_silu(gate) * up).astype(BF16)
    y_ref[...] = x + jnp.dot(a_scr[...], wd_ref[...], preferred_element_type=F32)


def _ffn(x, g, wg, wu, wd, tm=512):
    t, d = x.shape
    f = wg.shape[1]
    const = lambda a: pl.BlockSpec(a.shape, lambda i: (0, 0), pipeline_mode=pl.Buffered(1))
    return pl.pallas_call(
        _ffn_kernel,
        grid=(t // tm,),
        in_specs=[pl.BlockSpec((tm, d), lambda i: (i, 0)),
                  pl.BlockSpec((1, d), lambda i: (0, 0)),
                  const(wg), const(wu), const(wd)],
        out_specs=pl.BlockSpec((tm, d), lambda i: (i, 0)),
        out_shape=jax.ShapeDtypeStruct((t, d), F32),
        scratch_shapes=[pltpu.VMEM((tm, f), BF16)],
        compiler_params=pltpu.CompilerParams(
            dimension_semantics=("arbitrary",), vmem_limit_bytes=VMEM_LIMIT),
        name="dense_swiglu",
    )(x, g.reshape(1, d), wg, wu, wd)


TOP_K = 2
MOE_TB = 512
MOE_TM = 512
MOE_RC = 256
KEY_STRIDE = 65536.0


def _router_kernel(x_ref, g_ref, wr_ref, h_ref, info_ref, info_t_ref, cnt_ref, carry_scr):
    i = pl.program_id(0)
    tb = x_ref.shape[0]
    lane = lax.broadcasted_iota(jnp.int32, (1, LANES), 1)

    @pl.when(i == 0)
    def _():
        carry_scr[...] = jnp.zeros_like(carry_scr)

    h = _rms_rows(x_ref[...], g_ref[...])
    h_ref[...] = h.astype(BF16)
    logits = jnp.dot(h, wr_ref[...], preferred_element_type=F32,
                     precision=lax.Precision.HIGHEST)
    logits = jnp.where(lane < N_EXPERTS, logits, NEG_INF)
    v1 = jnp.max(logits, axis=-1, keepdims=True)
    i1 = jnp.min(jnp.where(logits == v1, lane, LANES), axis=-1, keepdims=True)
    rest = jnp.where(lane == i1, NEG_INF, logits)
    v2 = jnp.max(rest, axis=-1, keepdims=True)
    i2 = jnp.min(jnp.where(rest == v2, lane, LANES), axis=-1, keepdims=True)
    e2 = jnp.exp(v2 - v1)
    g1 = 1.0 / (1.0 + e2)
    g2 = e2 * g1

    pick1 = lane == i1
    pick2 = lane == i2
    picked = jnp.where(pick1, 1.0, jnp.where(pick2, 1.0, 0.0))
    r = lax.broadcasted_iota(jnp.int32, (tb, tb), 0)
    cidx = lax.broadcasted_iota(jnp.int32, (tb, tb), 1)
    ltri = jnp.where(cidx < r, 1.0, 0.0).astype(BF16)
    before = jnp.dot(ltri, picked.astype(BF16), preferred_element_type=F32) + carry_scr[...]
    rank1 = jnp.sum(jnp.where(pick1, before, 0.0), axis=-1, keepdims=True)
    rank2 = jnp.sum(jnp.where(pick2, before, 0.0), axis=-1, keepdims=True)
    key1 = i1.astype(F32) * KEY_STRIDE + rank1
    key2 = i2.astype(F32) * KEY_STRIDE + rank2
    info = jnp.where(lane == 0, key1, jnp.where(lane == 1, key2,
                     jnp.where(lane == 2, g1, jnp.where(lane == 3, g2, 0.0))))
    info_ref[...] = info
    info_t_ref[...] = info.T[:8, :]
    counts = jnp.sum(picked, axis=0, keepdims=True)
    cnt_ref[...] = jnp.broadcast_to(counts, cnt_ref.shape)
    carry_scr[...] += counts


def _router(x, g, wr):
    t, d = x.shape
    nb = t // MOE_TB
    return pl.pallas_call(
        _router_kernel,
        grid=(nb,),
        in_specs=[pl.BlockSpec((MOE_TB, d), lambda i: (i, 0)),
                  pl.BlockSpec((1, d), lambda i: (0, 0)),
                  pl.BlockSpec(wr.shape, lambda i: (0, 0))],
        out_specs=[pl.BlockSpec((MOE_TB, d), lambda i: (i, 0)),
                   pl.BlockSpec((MOE_TB, LANES), lambda i: (i, 0)),
                   pl.BlockSpec((None, 8, MOE_TB), lambda i: (i, 0, 0)),
                   pl.BlockSpec((None, 8, LANES), lambda i: (i, 0, 0))],
        out_shape=[jax.ShapeDtypeStruct((t, d), BF16),
                   jax.ShapeDtypeStruct((t, LANES), F32),
                   jax.ShapeDtypeStruct((nb, 8, MOE_TB), F32),
                   jax.ShapeDtypeStruct((nb, 8, LANES), F32)],
        scratch_shapes=[pltpu.VMEM((1, LANES), F32)],
        compiler_params=pltpu.CompilerParams(
            dimension_semantics=("arbitrary",), vmem_limit_bytes=VMEM_LIMIT),
        name="moe_router",
    )(x, g.reshape(1, d), wr)


def _moe_plan(block_counts, n_tokens):
    nb = block_counts.shape[0]
    n_tiles = (TOP_K * n_tokens + N_EXPERTS * (MOE_TM - 1)) // MOE_TM
    n_chunks = n_tiles * MOE_TM // MOE_RC
    n_pairs = ((n_chunks + N_EXPERTS * nb + 7) // 8) * 8
    tc = block_counts.astype(jnp.int32)
    cb = jnp.concatenate([jnp.zeros((1, N_EXPERTS), jnp.int32), jnp.cumsum(tc, axis=0)])
    counts = cb[-1]
    padded = ((counts + MOE_TM - 1) // MOE_TM) * MOE_TM
    ends = jnp.cumsum(padded)
    starts = ends - padded
    total = ends[-1]

    tiles = jnp.arange(n_tiles, dtype=jnp.int32)
    n_valid_tiles = total // MOE_TM
    t_valid = (tiles < n_valid_tiles).astype(jnp.int32)
    t_src = jnp.maximum(jnp.minimum(tiles, n_valid_tiles - 1), 0)
    t_exp = jnp.minimum(jnp.sum((t_src * MOE_TM)[:, None] >= ends[None, :], axis=1),
                        N_EXPERTS - 1).astype(jnp.int32)

    r0 = jnp.arange(n_chunks, dtype=jnp.int32) * MOE_RC
    c_exp = jnp.minimum(jnp.sum(r0[:, None] >= ends[None, :], axis=1), N_EXPERTS - 1)
    c_live = r0 < total
    lo = r0 - starts[c_exp]
    base = (c_exp * int(KEY_STRIDE) + lo).astype(jnp.int32)
    cbe = cb[:, c_exp]
    real = c_live[None, :] & (cbe[:-1] < (lo + MOE_RC)[None, :]) & (cbe[1:] > lo[None, :])
    empty = c_live & jnp.logical_not(jnp.any(real, axis=0))
    ov = real | (empty[None, :] & (jnp.arange(nb)[:, None] == 0))
    n_act = jnp.sum(ov).astype(jnp.int32)
    pidx = jnp.arange(n_pairs, dtype=jnp.int32)
    act = pidx < n_act

    def pair_list(flat, inner):
        idx = jnp.nonzero(flat, size=n_pairs, fill_value=0)[0].astype(jnp.int32)
        idx = jnp.where(act, idx, idx[n_act - 1])
        return idx // inner, idx % inner

    _, g_b = pair_list(ov.T.reshape(-1), nb)
    _, c_c = pair_list(ov.reshape(-1), n_chunks)
    g_cnt = jnp.sum(ov, axis=0).astype(jnp.int32)
    g_start = (jnp.cumsum(g_cnt) - g_cnt).astype(jnp.int32)
    c_cnt = jnp.sum(ov, axis=1).astype(jnp.int32)
    c_start = (jnp.cumsum(c_cnt) - c_cnt).astype(jnp.int32)
    chunks = jnp.arange(n_chunks, dtype=jnp.int32)
    g_src = jnp.maximum(jnp.minimum(chunks, total // MOE_RC - 1), 0)
    return dict(n_tiles=n_tiles, n_chunks=n_chunks, t_valid=t_valid, t_src=t_src, t_exp=t_exp,
                base=base, g_src=g_src, g_start=g_start, g_cnt=g_cnt, g_b=g_b,
                c_start=c_start, c_cnt=c_cnt, c_c=c_c)


def _gather_kernel(src_ref, start_ref, cnt_ref, pb_ref, base_ref, info_t_ref, h_ref,
                   xs_ref, gs_ref, acc_scr, gsum_scr):
    c = pl.program_id(0)
    tb = info_t_ref.shape[-1]

    @pl.when(src_ref[c] == c)
    def _():
        first = start_ref[c]
        n = cnt_ref[c]
        row = lax.broadcasted_iota(jnp.int32, (MOE_RC, tb), 0).astype(F32)
        acc_scr[...] = jnp.zeros_like(acc_scr)
        gsum_scr[...] = jnp.zeros_like(gsum_scr)

        def two_blocks(it, carry):
            rows, gate = None, None
            for w in range(2):
                j = 2 * it + w
                b = pb_ref[first + jnp.minimum(j, n - 1)]
                base = jnp.where(j < n, base_ref[c], NO_MATCH_BASE).astype(F32)
                info = info_t_ref[b]
                m1 = (info[0:1, :] - base) == row
                m2 = (info[1:2, :] - base) == row
                sel = jnp.where(m1, 1.0, jnp.where(m2, 1.0, 0.0)).astype(BF16)
                g = jnp.sum(jnp.where(m1, info[2:3, :], jnp.where(m2, info[3:4, :], 0.0)),
                            axis=-1, keepdims=True)
                r = jnp.dot(sel, h_ref[pl.ds(pl.multiple_of(b * tb, tb), tb), :],
                            preferred_element_type=F32)
                rows = r if rows is None else rows + r
                gate = g if gate is None else gate + g
            acc_scr[...] += rows
            gsum_scr[...] += gate
            return carry

        lax.fori_loop(0, (n + 1) // 2, two_blocks, 0)
        xs_ref[...] = acc_scr[...].astype(xs_ref.dtype)
        gs_ref[...] = jnp.broadcast_to(gsum_scr[...], gs_ref.shape)

    @pl.when(src_ref[c] != c)
    def _():
        xs_ref[...] = jnp.zeros_like(xs_ref)
        gs_ref[...] = jnp.zeros_like(gs_ref)


def _moe_gather(plan, info_t, h):
    t, d = h.shape
    rows = plan["n_chunks"] * MOE_RC
    chunk = lambda c, src, st, cn, pb, ba: (c, 0)
    grid_spec = pltpu.PrefetchScalarGridSpec(
        num_scalar_prefetch=5,
        grid=(plan["n_chunks"],),
        in_specs=[pl.BlockSpec(info_t.shape, lambda c, src, st, cn, pb, ba: (0, 0, 0),
                               pipeline_mode=pl.Buffered(1)),
                  pl.BlockSpec((t, d), lambda c, src, st, cn, pb, ba: (0, 0),
                               pipeline_mode=pl.Buffered(1))],
        out_specs=[pl.BlockSpec((MOE_RC, d), chunk), pl.BlockSpec((MOE_RC, LANES), chunk)],
        scratch_shapes=[pltpu.VMEM((MOE_RC, d), F32), pltpu.VMEM((MOE_RC, 1), F32)],
    )
    return pl.pallas_call(
        _gather_kernel,
        grid_spec=grid_spec,
        out_shape=[jax.ShapeDtypeStruct((rows, d), BF16),
                   jax.ShapeDtypeStruct((rows, LANES), F32)],
        compiler_params=pltpu.CompilerParams(
            dimension_semantics=("arbitrary",), vmem_limit_bytes=VMEM_LIMIT),
        name="moe_gather",
    )(plan["g_src"], plan["g_start"], plan["g_cnt"], plan["g_b"], plan["base"], info_t, h)


def _experts_kernel(src_ref, exp_ref, val_ref, xs_ref, gs_ref, wg_ref, wu_ref, wd_ref, ys_ref,
                    acc_scr):
    i = pl.program_id(0)
    j = pl.program_id(1)

    last = j == pl.num_programs(1) - 1

    @pl.when(j == 0)
    def _():
        acc_scr[...] = jnp.zeros_like(acc_scr)

    @pl.when(val_ref[i] == 1)
    def _():
        x = xs_ref[...]
        gate = jnp.dot(x, wg_ref[...], preferred_element_type=F32)
        up = jnp.dot(x, wu_ref[...], preferred_element_type=F32)
        a = (_silu(gate) * up).astype(BF16)
        acc_scr[...] += jnp.dot(a, wd_ref[...], preferred_element_type=F32)

    @pl.when(last)
    def _():
        ys_ref[...] = (acc_scr[...] * gs_ref[:, 0:1]).astype(ys_ref.dtype)


def _moe_experts(plan, xs, gs, wg, wu, wd, tf):
    rows, d = xs.shape
    f = wg.shape[2]
    nf = f // tf

    def wcol(i, j, src, ex, val):
        return (ex[i], 0, jnp.where(val[i] == 1, j, nf - 1))

    def wrow(i, j, src, ex, val):
        return (ex[i], jnp.where(val[i] == 1, j, nf - 1), 0)

    tile = lambda i, j, src, ex, val: (i, 0)
    grid_spec = pltpu.PrefetchScalarGridSpec(
        num_scalar_prefetch=3,
        grid=(plan["n_tiles"], nf),
        in_specs=[pl.BlockSpec((MOE_TM, d), tile),
                  pl.BlockSpec((MOE_TM, LANES), tile),
                  pl.BlockSpec((None, d, tf), wcol),
                  pl.BlockSpec((None, d, tf), wcol),
                  pl.BlockSpec((None, tf, d), wrow)],
        out_specs=pl.BlockSpec((MOE_TM, d), tile),
        scratch_shapes=[pltpu.VMEM((MOE_TM, d), F32)],
    )
    return pl.pallas_call(
        _experts_kernel,
        grid_spec=grid_spec,
        out_shape=jax.ShapeDtypeStruct((rows, d), BF16),
        compiler_params=pltpu.CompilerParams(
            dimension_semantics=("arbitrary", "arbitrary"), vmem_limit_bytes=VMEM_LIMIT),
        name="moe_experts",
    )(plan["t_src"], plan["t_exp"], plan["t_valid"], xs, gs, wg, wu, wd)


NO_MATCH_BASE = -(2 ** 30)


def _combine_kernel(start_ref, cnt_ref, pc_ref, base_ref, info_ref, x_ref, ys_hbm, o_ref,
                    ybuf, sem):
    b = pl.program_id(0)
    n = cnt_ref[b]
    first = start_ref[b]
    tb = x_ref.shape[0]
    n_it = (n + 1) // 2

    def chunk_of(j):
        return pc_ref[first + jnp.minimum(j, n - 1)]

    def copies(it, slot):
        return [pltpu.make_async_copy(
            ys_hbm.at[pl.ds(pl.multiple_of(chunk_of(2 * it + w) * MOE_RC, MOE_RC), MOE_RC), :],
            ybuf.at[slot, w], sem.at[slot, w]) for w in range(2)]

    lane = lax.broadcasted_iota(jnp.int32, (1, LANES), 1)
    info = info_ref[...]
    k1 = jnp.sum(jnp.where(lane == 0, info, 0.0), axis=-1, keepdims=True)
    k2 = jnp.sum(jnp.where(lane == 1, info, 0.0), axis=-1, keepdims=True)
    col = lax.broadcasted_iota(jnp.int32, (tb, MOE_RC), 1).astype(F32)
    o_ref[...] = x_ref[...]

    @pl.when(n > 0)
    def _():
        for cp in copies(0, 0):
            cp.start()

    def body(it, carry):
        slot = it % 2
        for cp in copies(it, slot):
            cp.wait()

        @pl.when(it + 1 < n_it)
        def _():
            for cp in copies(it + 1, 1 - slot):
                cp.start()

        acc = None
        for w in range(2):
            j = 2 * it + w
            base = jnp.where(j < n, base_ref[chunk_of(j)], NO_MATCH_BASE).astype(F32)
            sel = jnp.where(k1 - base == col, 1.0,
                            jnp.where(k2 - base == col, 1.0, 0.0)).astype(BF16)
            part = jnp.dot(sel, ybuf[slot, w], preferred_element_type=F32)
            acc = part if acc is None else acc + part
        o_ref[...] += acc
        return carry

    lax.fori_loop(0, n_it, body, 0)


def _moe_combine(plan, info, ys, x):
    t, d = x.shape
    block = lambda b, st, cn, pc, ba: (b, 0)
    grid_spec = pltpu.PrefetchScalarGridSpec(
        num_scalar_prefetch=4,
        grid=(t // MOE_TB,),
        in_specs=[pl.BlockSpec((MOE_TB, LANES), block),
                  pl.BlockSpec((MOE_TB, d), block),
                  pl.BlockSpec(memory_space=pl.ANY)],
        out_specs=pl.BlockSpec((MOE_TB, d), block),
        scratch_shapes=[pltpu.VMEM((2, 2, MOE_RC, d), BF16), pltpu.SemaphoreType.DMA((2, 2))],
    )
    return pl.pallas_call(
        _combine_kernel,
        grid_spec=grid_spec,
        out_shape=jax.ShapeDtypeStruct((t, d), F32),
        compiler_params=pltpu.CompilerParams(
            dimension_semantics=("arbitrary",), vmem_limit_bytes=VMEM_LIMIT),
        name="moe_combine",
    )(plan["c_start"], plan["c_cnt"], plan["c_c"], plan["base"], info, x, ys)


def _moe(x, g, wr, wg, wu, wd, tf=1792):
    t, _ = x.shape
    h, info, info_t, block_counts = _router(x, g, wr)
    plan = _moe_plan(block_counts[:, 0, :N_EXPERTS], t)
    xs, gs = _moe_gather(plan, info_t, h)
    ys = _moe_experts(plan, xs, gs, wg, wu, wd, tf)
    return _moe_combine(plan, info, ys, x)


def _alibi_slopes():
    s = np.exp2(-8.0 * np.arange(1, N_HEADS + 1) / N_HEADS).astype(np.float32)
    return s[0::2][:A_HEADS], s[1::2][:B_Q_HEADS]


_B_HEAD_ORDER = np.array([0, 4, 1, 5, 2, 6, 3, 7])


def _even_layer(x2d, batch, norm1, w_in, qn_a, kn_a, qn_b, kn_b, sink_b, w_out,
                norm2, w_gate, w_up, w_down):
    t, d = x2d.shape
    s = t // batch
    wa = A_HEADS * HEAD_DIM
    wq_b = B_Q_HEADS * HEAD_DIM
    wkv_b = B_KV_HEADS * HEAD_DIM
    qb0 = 3 * wa
    q_scale = HEAD_DIM ** -0.5

    w_qb = w_in[:, qb0:qb0 + wq_b].reshape(d, B_Q_HEADS, HEAD_DIM)[:, _B_HEAD_ORDER]
    w_in_p = jnp.concatenate(
        [w_in[:, :qb0], w_qb.reshape(d, wq_b), w_in[:, qb0 + wq_b:]], axis=1).astype(BF16)
    ones = jnp.ones((HEAD_DIM,), F32)
    col_gain = jnp.concatenate([
        jnp.tile(qn_a * q_scale, A_HEADS), jnp.tile(kn_a, A_HEADS), jnp.tile(ones, A_HEADS),
        jnp.tile(qn_b * q_scale, B_Q_HEADS), jnp.tile(kn_b, B_KV_HEADS),
        jnp.tile(ones, B_KV_HEADS)])
    cpl = wa // LANES
    norm_chunks = ((True,) * (2 * cpl) + (False,) * cpl + (True,) * (wq_b // LANES)
                   + (True,) * (wkv_b // LANES) + (False,) * (wkv_b // LANES))
    qkv = _norm_proj(x2d, norm1, w_in_p, col_gain, norm_chunks)
    n_cols = qkv.shape[1]
    qkv3 = qkv.reshape(batch, s, n_cols)

    slopes_a, slopes_b = _alibi_slopes()
    par_a = jnp.asarray(slopes_a).reshape(1, A_HEADS)
    radii = {window // (2 * dil) for window, dil in DIL_CONFIGS}
    assert tuple(dil for _, dil in DIL_CONFIGS) == (1, 4, 16) and len(radii) == 1
    oa = _dilated_attention(qkv3, par_a, k_blk0=cpl, v_blk0=2 * cpl, n_pairs=cpl,
                            blk=A_BLOCK, radius=radii.pop())
    par_b = jnp.stack([jnp.asarray(slopes_b[_B_HEAD_ORDER]),
                       sink_b.astype(F32)[_B_HEAD_ORDER]])
    kb = 3 * cpl + wq_b // LANES
    ob = _window_attention(qkv3, par_b, q_blk0=3 * cpl, k_blk=kb, v_blk=kb + wkv_b // LANES,
                           n_pairs=wq_b // LANES, radius=B_WINDOW, blk=B_BLOCK)
    w_out_a = w_out[:wa].astype(BF16)
    w_out_b = w_out[wa:].reshape(B_Q_HEADS, HEAD_DIM, d)[_B_HEAD_ORDER].reshape(wq_b, d)
    x1 = _out_proj(x2d, [oa.reshape(t, wa), ob.reshape(t, wq_b)],
                   [w_out_a, w_out_b.astype(BF16)])
    return _ffn(x1, norm2, w_gate.astype(BF16), w_up.astype(BF16), w_down.astype(BF16))


def _odd_layer(x2d, batch, norm1, w_qkv, qn, kn, rpb, w_out, norm2, w_router,
               w_gate, w_up, w_down):
    t, d = x2d.shape
    s = t // batch
    wc = C_HEADS * HEAD_DIM
    ones = jnp.ones((HEAD_DIM,), F32)
    col_gain = jnp.concatenate([jnp.tile(qn * HEAD_DIM ** -0.5, C_HEADS),
                                jnp.tile(kn, C_HEADS), jnp.tile(ones, C_HEADS)])
    cpl = wc // LANES
    norm_chunks = (True,) * (2 * cpl) + (False,) * cpl
    qkv = _norm_proj(x2d, norm1, w_qkv.astype(BF16), col_gain, norm_chunks)
    o = _na_attention(qkv.reshape(batch, s, 3 * wc), _na_bias_table(rpb))
    x1 = _out_proj(x2d, [o.reshape(t, wc)], [w_out.astype(BF16)])
    wr = jnp.zeros((d, LANES), F32).at[:, :N_EXPERTS].set(w_router.astype(F32))
    return _moe(x1, norm2, wr, w_gate.astype(BF16), w_up.astype(BF16), w_down.astype(BF16))


def kernel(x, ev_norm1, ev_w_in, ev_qn_a, ev_kn_a, ev_qn_b, ev_kn_b, ev_sink_b, ev_w_out, ev_norm2, ev_ffn_gate, ev_ffn_up, ev_ffn_down, od_norm1, od_w_qkv, od_qn, od_kn, od_rpb, od_w_out, od_norm2, od_router, od_exp_gate, od_exp_up, od_exp_down):
    batch, s, d = x.shape
    depth = ev_norm1.shape[0] + od_norm1.shape[0]
    h = x.reshape(batch * s, d)
    for layer in range(depth):
        j = layer // 2
        if layer % 2 == 0:
            h = _even_layer(h, batch, ev_norm1[j], ev_w_in[j], ev_qn_a[j], ev_kn_a[j],
                            ev_qn_b[j], ev_kn_b[j], ev_sink_b[j], ev_w_out[j], ev_norm2[j],
                            ev_ffn_gate[j], ev_ffn_up[j], ev_ffn_down[j])
        else:
            h = _odd_layer(h, batch, od_norm1[j], od_w_qkv[j], od_qn[j], od_kn[j], od_rpb[j],
                           od_w_out[j], od_norm2[j], od_router[j], od_exp_gate[j],
                           od_exp_up[j], od_exp_down[j])
    return h.reshape(batch, s, d)
```

```python
import functools

import numpy as np
import jax
import jax.numpy as jnp
from jax import lax
from jax.experimental import pallas as pl
from jax.experimental.pallas import tpu as pltpu

D_MODEL = 1024
HEAD_DIM = 64
N_HEADS = D_MODEL // HEAD_DIM
A_HEADS = N_HEADS // 2
B_Q_HEADS = N_HEADS // 2
B_KV_HEADS = max(1, B_Q_HEADS // 4)
C_HEADS = N_HEADS
DIL_CONFIGS = ((128, 1), (512, 4), (2048, 16))
A_BLOCK = 128
B_WINDOW = 128
B_BLOCK = 128
GRID_W = 64
NA_ROWS = 8
NA_COLS = 16
N_EXPERTS = 8
RMS_EPS = 1e-6
NEG_INF = -1e30

LANES = 128
MXU_N = 256
VMEM_LIMIT = 56 * 1024 * 1024

ATTN_UNROLL = 8

F32 = jnp.float32
BF16 = jnp.bfloat16


def _lo_mask(shape):
    return lax.broadcasted_iota(jnp.int32, shape, len(shape) - 1) < HEAD_DIM


def _rms_rows(x, g):
    ms = jnp.mean(x * x, axis=-1, keepdims=True)
    return x * lax.rsqrt(ms + RMS_EPS) * g


def _norm_proj_kernel(x_ref, g_ref, w_ref, cg_ref, o_ref, *, norm_chunks):
    h = _rms_rows(x_ref[...], g_ref[...]).astype(BF16)
    n_out = o_ref.shape[-1]
    lo = _lo_mask((1, LANES))
    for j in range(n_out // MXU_N):
        y = jnp.dot(h, w_ref[:, j * MXU_N:(j + 1) * MXU_N], preferred_element_type=F32)
        for half in range(MXU_N // LANES):
            c = j * (MXU_N // LANES) + half
            yc = y[:, half * LANES:(half + 1) * LANES]
            if norm_chunks[c]:
                sq = yc * yc
                s_lo = jnp.sum(jnp.where(lo, sq, 0.0), axis=-1, keepdims=True)
                s_hi = jnp.sum(jnp.where(lo, 0.0, sq), axis=-1, keepdims=True)
                inv = jnp.where(lo, lax.rsqrt(s_lo * (1.0 / HEAD_DIM) + RMS_EPS),
                                lax.rsqrt(s_hi * (1.0 / HEAD_DIM) + RMS_EPS))
                yc = yc * inv * cg_ref[:, c * LANES:(c + 1) * LANES]
            o_ref[:, c * LANES:(c + 1) * LANES] = yc.astype(o_ref.dtype)


def _norm_proj(x, g, w, col_gain, norm_chunks, tm=512):
    t, d = x.shape
    n = w.shape[1]
    return pl.pallas_call(
        functools.partial(_norm_proj_kernel, norm_chunks=norm_chunks),
        grid=(t // tm,),
        in_specs=[
            pl.BlockSpec((tm, d), lambda i: (i, 0)),
            pl.BlockSpec((1, d), lambda i: (0, 0)),
            pl.BlockSpec((d, n), lambda i: (0, 0)),
            pl.BlockSpec((1, n), lambda i: (0, 0)),
        ],
        out_specs=pl.BlockSpec((tm, n), lambda i: (i, 0)),
        out_shape=jax.ShapeDtypeStruct((t, n), BF16),
        compiler_params=pltpu.CompilerParams(
            dimension_semantics=("arbitrary",), vmem_limit_bytes=VMEM_LIMIT),
        name="norm_proj",
    )(x, g.reshape(1, d), w, col_gain.reshape(1, n))


def _window_kernel(par_ref, q_ref, k_ref, v_ref, o_ref, bias_scr, *, seq, blk, radius):
    span = blk + 2 * radius
    c = pl.program_id(1)
    group = ATTN_UNROLL
    row_lo = lax.broadcasted_iota(jnp.int32, (2 * blk, 1), 0) < blk
    slope_col = jnp.where(row_lo, par_ref[0, 2 * c], par_ref[0, 2 * c + 1])
    sink_col = jnp.where(row_lo, par_ref[1, 2 * c], par_ref[1, 2 * c + 1])

    iq = lax.broadcasted_iota(jnp.int32, (2 * blk, span), 0) % blk
    ik = lax.broadcasted_iota(jnp.int32, (2 * blk, span), 1)
    _fill_band_bias(bias_scr, iq, ik, radius, 1, slope_col)
    lo_q = _lo_mask((1, LANES))
    lo = _lo_mask((blk, LANES))

    def body(g, carry):
        starts, variants, values, scores = [], [], [], []
        for u in range(group):
            a = pl.multiple_of((g * group + u) * blk, blk)
            ks = pl.multiple_of(jnp.clip(a - radius, 0, seq - span), radius)
            q = q_ref[pl.ds(a, blk), :]
            zero = jnp.zeros_like(q)
            qm = jnp.concatenate([jnp.where(lo_q, q, zero), jnp.where(lo_q, zero, q)], axis=0)
            k = k_ref[pl.ds(ks, span), :]
            scores.append(lax.dot_general(qm, k, (((1,), (1,)), ((), ())),
                                          preferred_element_type=F32))
            starts.append(a)
            variants.append((a - ks) // radius)
            values.append(v_ref[pl.ds(ks, span), :])
        probs, dens = [], []
        for u in range(group):
            s = scores[u] + bias_scr[variants[u]]
            m = jnp.maximum(jnp.max(s, axis=-1, keepdims=True), sink_col)
            p = jnp.exp(s - m)
            dens.append(jnp.sum(p, axis=-1, keepdims=True) + jnp.exp(sink_col - m))
            probs.append(p.astype(BF16))
        for u in range(group):
            o = jnp.dot(probs[u], values[u], preferred_element_type=F32) * (1.0 / dens[u])
            o_ref[pl.ds(starts[u], blk), :] = jnp.where(lo, o[:blk], o[blk:]).astype(o_ref.dtype)
        return carry

    lax.fori_loop(0, seq // (blk * group), body, 0)


def _window_attention(qkv, params, *, q_blk0, k_blk, v_blk, n_pairs, radius, blk):
    b, s, _ = qkv.shape
    span = blk + 2 * radius
    return pl.pallas_call(
        functools.partial(_window_kernel, seq=s, blk=blk, radius=radius),
        grid=(b, n_pairs),
        in_specs=[
            pl.BlockSpec(memory_space=pltpu.SMEM),
            pl.BlockSpec((None, s, LANES), lambda bi, c: (bi, 0, q_blk0 + c)),
            pl.BlockSpec((None, s, LANES), lambda bi, c: (bi, 0, k_blk)),
            pl.BlockSpec((None, s, LANES), lambda bi, c: (bi, 0, v_blk)),
        ],
        out_specs=pl.BlockSpec((None, s, LANES), lambda bi, c: (bi, 0, c)),
        out_shape=jax.ShapeDtypeStruct((b, s, n_pairs * LANES), BF16),
        scratch_shapes=[pltpu.VMEM((3, 2 * blk, span), F32)],
        compiler_params=pltpu.CompilerParams(
            dimension_semantics=("arbitrary", "arbitrary"), vmem_limit_bytes=VMEM_LIMIT),
        name="window_attn",
    )(params, qkv, qkv, qkv)


PERM_CHUNK = 256
PHASES4 = 4
PIECE = PERM_CHUNK // PHASES4


def _stack_heads(q):
    lo = _lo_mask((1, LANES))
    zero = jnp.zeros_like(q)
    return jnp.concatenate([jnp.where(lo, q, zero), jnp.where(lo, zero, q)], axis=0)


def _merge_heads(x, blk):
    x = jnp.broadcast_to(x, (2 * blk, LANES))
    return jnp.where(_lo_mask((blk, LANES)), x[:blk], x[blk:])


def _attend_group(qs, ks, vs, biases):
    scores = [lax.dot_general(_stack_heads(q), k, (((1,), (1,)), ((), ())),
                              preferred_element_type=F32) for q, k in zip(qs, ks)]
    stats = []
    for s, bias in zip(scores, biases):
        s = s + bias
        m = jnp.max(s, axis=-1, keepdims=True)
        p = jnp.exp(s - m)
        stats.append((p.astype(BF16), m, jnp.sum(p, axis=-1, keepdims=True)))
    return [(jnp.dot(p, v, preferred_element_type=F32), m, l)
            for (p, m, l), v in zip(stats, vs)]


def _fill_band_bias(bias_scr, pos_q, pos_k, radius, pos_scale, slope_col):
    for var in range(3):
        dist = jnp.abs(pos_q + var * radius - pos_k)
        dist_f = (dist * pos_scale).astype(F32)
        bias_scr[var] = jnp.where(dist <= radius, -(slope_col * dist_f), NEG_INF)


def _dilated_kernel(par_ref, q_ref, k_ref, v_ref, o_ref, qp, kp, vp, acc_p, m_p, l_p,
                    bias1, bias4, bias16, *, seq, blk, radius):
    span = blk + 2 * radius
    n_chunk = seq // PERM_CHUNK
    group = ATTN_UNROLL
    c = pl.program_id(1)
    row_lo = lax.broadcasted_iota(jnp.int32, (2 * blk, 1), 0) < blk
    slope_col = jnp.where(row_lo, par_ref[0, 2 * c], par_ref[0, 2 * c + 1])

    def piece_pos(x):
        return PIECE * (x // PIECE) + PHASES4 * (x % 16) + (x % PIECE) // 16

    iq = lax.broadcasted_iota(jnp.int32, (2 * blk, span), 0) % blk
    ik = lax.broadcasted_iota(jnp.int32, (2 * blk, span), 1)
    _fill_band_bias(bias1, iq, ik, radius, 1, slope_col)
    _fill_band_bias(bias4, piece_pos(iq), piece_pos(ik), radius, 4, slope_col)
    _fill_band_bias(bias16, iq, ik, radius, 16, slope_col)

    rr = lax.broadcasted_iota(jnp.int32, (PERM_CHUNK, PERM_CHUNK), 0)
    cc = lax.broadcasted_iota(jnp.int32, (PERM_CHUNK, PERM_CHUNK), 1)
    to_phase = jnp.where(cc == 16 * (rr % 16) + PHASES4 * ((rr // 16) % 4) + rr // PIECE,
                         1.0, 0.0).astype(BF16)
    to_token = jnp.where(cc == PIECE * (rr % PHASES4) + 16 * ((rr // PHASES4) % 4) + rr // 16,
                         1.0, 0.0).astype(BF16)

    def permute_body(ch, carry):
        r0 = pl.multiple_of(ch * PERM_CHUNK, PERM_CHUNK)
        x = jnp.concatenate([q_ref[pl.ds(r0, PERM_CHUNK), :], k_ref[pl.ds(r0, PERM_CHUNK), :],
                             v_ref[pl.ds(r0, PERM_CHUNK), :]], axis=1)
        y = jnp.dot(to_phase, x, preferred_element_type=F32).astype(BF16)
        for p4 in range(PHASES4):
            rows = slice(PIECE * p4, PIECE * (p4 + 1))
            qp[p4, ch] = y[rows, 0:LANES]
            kp[p4, ch] = y[rows, LANES:2 * LANES]
            vp[p4, ch] = y[rows, 2 * LANES:3 * LANES]
        return carry

    lax.fori_loop(0, n_chunk, permute_body, 0)

    pieces_per_blk = blk // PIECE
    n_blk4 = seq // (PHASES4 * blk)

    def dil4_body(g, carry):
        p4 = g // (n_blk4 // group)
        j0 = (g % (n_blk4 // group)) * group
        qs, ks, vs, bs = [], [], [], []
        for u in range(group):
            pc = pieces_per_blk * (j0 + u)
            cs = jnp.clip(pc - radius // PIECE, 0, n_chunk - span // PIECE)
            qs.append(qp[p4, pl.ds(pc, pieces_per_blk)].reshape(blk, LANES))
            ks.append(kp[p4, pl.ds(cs, span // PIECE)].reshape(span, LANES))
            vs.append(vp[p4, pl.ds(cs, span // PIECE)].reshape(span, LANES))
            bs.append(bias4[(pc - cs) // (radius // PIECE)])
        for u, (o, m, l) in enumerate(_attend_group(qs, ks, vs, bs)):
            dst = (p4, pl.ds(pieces_per_blk * (j0 + u), pieces_per_blk))
            acc_p[dst] = _merge_heads(o, blk).reshape(pieces_per_blk, PIECE, LANES)
            m_p[dst] = _merge_heads(m, blk).reshape(pieces_per_blk, PIECE, LANES)
            l_p[dst] = _merge_heads(l, blk).reshape(pieces_per_blk, PIECE, LANES)
        return carry

    lax.fori_loop(0, PHASES4 * n_blk4 // group, dil4_body, 0)

    chunks_per_blk = blk // 16

    def dil16_body(g, carry):
        p4 = g // 2
        qs, ks, vs, bs, dsts = [], [], [], [], []
        for mm in range(2):
            off = pl.multiple_of((2 * (g % 2) + mm) * 16, 16)
            kk = kp[p4, :, pl.ds(off, 16), :].reshape(span, LANES)
            vv = vp[p4, :, pl.ds(off, 16), :].reshape(span, LANES)
            for jb in range(n_chunk // chunks_per_blk):
                dst = (p4, pl.ds(chunks_per_blk * jb, chunks_per_blk), pl.ds(off, 16), slice(None))
                qs.append(qp[dst].reshape(blk, LANES))
                ks.append(kk)
                vs.append(vv)
                bs.append(bias16[jb * (blk // radius)])
                dsts.append(dst)
        for dst, (o, m, l) in zip(dsts, _attend_group(qs, ks, vs, bs)):
            m_old = m_p[dst].reshape(blk, LANES)
            m_new = jnp.maximum(m_old, _merge_heads(m, blk))
            w_old = jnp.exp(m_old - m_new)
            w_new = jnp.exp(_merge_heads(m, blk) - m_new)
            acc = w_old * acc_p[dst].reshape(blk, LANES) + w_new * _merge_heads(o, blk)
            den = w_old * l_p[dst].reshape(blk, LANES) + w_new * _merge_heads(l, blk)
            acc_p[dst] = acc.reshape(chunks_per_blk, 16, LANES)
            l_p[dst] = den.reshape(chunks_per_blk, 16, LANES)
            m_p[dst] = m_new.reshape(chunks_per_blk, 16, LANES)
        return carry

    lax.fori_loop(0, PHASES4 * 2, dil16_body, 0)

    blk_per_chunk = PERM_CHUNK // blk

    def dil1_body(g, carry):
        merged = []
        for cc2 in range(group // blk_per_chunk):
            ch = g * (group // blk_per_chunk) + cc2
            acc = jnp.concatenate([acc_p[p4, ch] for p4 in range(PHASES4)], axis=0)
            den = jnp.concatenate([l_p[p4, ch] for p4 in range(PHASES4)], axis=0)
            mx = jnp.concatenate([m_p[p4, ch] for p4 in range(PHASES4)], axis=0)
            o_ph = (acc * (1.0 / den)).astype(BF16)
            lse = mx + jnp.log(den)
            hi = lse.astype(BF16)
            rem = lse - hi.astype(F32)
            mid = rem.astype(BF16)
            low = (rem - mid.astype(F32)).astype(BF16)
            back = jnp.dot(to_token, jnp.concatenate([o_ph, hi, mid, low], axis=1),
                           preferred_element_type=F32)
            merged.append((back[:, 0:LANES], back[:, LANES:2 * LANES]
                           + back[:, 2 * LANES:3 * LANES] + back[:, 3 * LANES:4 * LANES]))
        qs, ks, vs, bs, starts = [], [], [], [], []
        for u in range(group):
            a = pl.multiple_of((g * group + u) * blk, blk)
            kst = pl.multiple_of(jnp.clip(a - radius, 0, seq - span), radius)
            qs.append(q_ref[pl.ds(a, blk), :])
            ks.append(k_ref[pl.ds(kst, span), :])
            vs.append(v_ref[pl.ds(kst, span), :])
            bs.append(bias1[(a - kst) // radius])
            starts.append(a)
        for u, (o, m, l) in enumerate(_attend_group(qs, ks, vs, bs)):
            o1 = _merge_heads(o * (1.0 / l), blk)
            lse1 = _merge_heads(m + jnp.log(l), blk)
            o2, lse2 = merged[u // blk_per_chunk]
            half = slice((u % blk_per_chunk) * blk, (u % blk_per_chunk + 1) * blk)
            o2, lse2 = o2[half], lse2[half]
            top = jnp.maximum(lse1, lse2)
            e1 = jnp.exp(lse1 - top)
            e2 = jnp.exp(lse2 - top)
            o_ref[pl.ds(starts[u], blk), :] = ((e1 * o1 + e2 * o2)
                                               * (1.0 / (e1 + e2))).astype(o_ref.dtype)
        return carry

    lax.fori_loop(0, seq // (blk * group), dil1_body, 0)


def _dilated_attention(qkv, params, *, k_blk0, v_blk0, n_pairs, blk, radius):
    b, s, _ = qkv.shape
    span = blk + 2 * radius
    n_chunk = s // PERM_CHUNK
    assert s // 16 == span and s % PERM_CHUNK == 0 and blk % PIECE == 0
    phase = lambda dt: pltpu.VMEM((PHASES4, n_chunk, PIECE, LANES), dt)
    return pl.pallas_call(
        functools.partial(_dilated_kernel, seq=s, blk=blk, radius=radius),
        grid=(b, n_pairs),
        in_specs=[
            pl.BlockSpec(memory_space=pltpu.SMEM),
            pl.BlockSpec((None, s, LANES), lambda bi, c: (bi, 0, c)),
            pl.BlockSpec((None, s, LANES), lambda bi, c: (bi, 0, k_blk0 + c)),
            pl.BlockSpec((None, s, LANES), lambda bi, c: (bi, 0, v_blk0 + c)),
        ],
        out_specs=pl.BlockSpec((None, s, LANES), lambda bi, c: (bi, 0, c)),
        out_shape=jax.ShapeDtypeStruct((b, s, n_pairs * LANES), BF16),
        scratch_shapes=[phase(BF16), phase(BF16), phase(BF16), phase(F32), phase(F32), phase(F32)]
        + [pltpu.VMEM((3, 2 * blk, span), F32)] * 3,
        compiler_params=pltpu.CompilerParams(
            dimension_semantics=("arbitrary", "arbitrary"), vmem_limit_bytes=VMEM_LIMIT),
        name="dilated_attn",
    )(params, qkv, qkv, qkv)


def _na_kernel(q_ref, k_ref, v_ref, bias_ref, o_ref, *, rows):
    kh = NA_ROWS
    group = ATTN_UNROLL
    lo_q = _lo_mask((1, LANES))
    lo = _lo_mask((GRID_W, LANES))

    def body(g, carry):
        starts, offs, values, scores = [], [], [], []
        for u in range(group):
            i = g * group + u
            rs = jnp.clip(i - kh // 2, 0, rows - kh)
            qs = pl.multiple_of(i * GRID_W, GRID_W)
            ks = pl.multiple_of(rs * GRID_W, GRID_W)
            q = q_ref[pl.ds(qs, GRID_W), :]
            zero = jnp.zeros_like(q)
            qm = jnp.concatenate([jnp.where(lo_q, q, zero), jnp.where(lo_q, zero, q)], axis=0)
            k = k_ref[pl.ds(ks, kh * GRID_W), :]
            scores.append(lax.dot_general(qm, k, (((1,), (1,)), ((), ())),
                                          preferred_element_type=F32))
            starts.append(qs)
            offs.append(rs - i + (NA_ROWS - 1))
            values.append(v_ref[pl.ds(ks, kh * GRID_W), :])
        probs, dens = [], []
        for u in range(group):
            s = scores[u] + jnp.concatenate(
                [bias_ref[offs[u] + 2 * j] for j in range(kh // 2)], axis=1)
            m = jnp.max(s, axis=-1, keepdims=True)
            p = jnp.exp(s - m)
            dens.append(jnp.sum(p, axis=-1, keepdims=True))
            probs.append(p.astype(BF16))
        for u in range(group):
            o = jnp.dot(probs[u], values[u], preferred_element_type=F32) * (1.0 / dens[u])
            o_ref[pl.ds(starts[u], GRID_W), :] = jnp.where(
                lo, o[:GRID_W], o[GRID_W:]).astype(o_ref.dtype)
        return carry

    lax.fori_loop(0, rows // group, body, 0)


def _na_bias_table(rpb):
    qc = np.arange(GRID_W)[:, None]
    kc = np.arange(GRID_W)[None, :]
    qstart = np.clip(qc - NA_COLS // 2, 0, GRID_W - NA_COLS)
    valid = (kc >= qstart) & (kc < qstart + NA_COLS)
    coff = np.clip(kc - qc, -(NA_COLS - 1), NA_COLS - 1) + NA_COLS - 1
    pick = (coff[:, :, None] == np.arange(2 * NA_COLS - 1)).astype(np.float32)
    t = jnp.einsum("hdj,qkj->hdqk", rpb.astype(F32), jnp.asarray(pick),
                   precision=lax.Precision.HIGHEST)
    t = jnp.where(jnp.asarray(valid), t, NEG_INF)
    t = jnp.concatenate([t[:, :-1], t[:, 1:]], axis=-1)
    n_off = 2 * NA_ROWS - 2
    t = t.reshape(C_HEADS // 2, 2, n_off, GRID_W, 2 * GRID_W).transpose(0, 2, 1, 3, 4)
    return t.reshape(C_HEADS // 2, n_off, 2 * GRID_W, 2 * GRID_W)


def _na_attention(qkv, bias):
    b, s, n_cols = qkv.shape
    n_pairs = C_HEADS // 2
    rows = s // GRID_W
    return pl.pallas_call(
        functools.partial(_na_kernel, rows=rows),
        grid=(b, n_pairs),
        in_specs=[
            pl.BlockSpec((None, s, LANES), lambda bi, c: (bi, 0, c)),
            pl.BlockSpec((None, s, LANES), lambda bi, c: (bi, 0, n_pairs + c)),
            pl.BlockSpec((None, s, LANES), lambda bi, c: (bi, 0, 2 * n_pairs + c)),
            pl.BlockSpec((None,) + bias.shape[1:], lambda bi, c: (c, 0, 0, 0)),
        ],
        out_specs=pl.BlockSpec((None, s, LANES), lambda bi, c: (bi, 0, c)),
        out_shape=jax.ShapeDtypeStruct((b, s, n_pairs * LANES), BF16),
        compiler_params=pltpu.CompilerParams(
            dimension_semantics=("arbitrary", "arbitrary"), vmem_limit_bytes=VMEM_LIMIT),
        name="na_attn",
    )(qkv, qkv, qkv, bias)


def _out_proj_kernel(x_ref, *refs):
    y_ref = refs[-1]
    n = (len(refs) - 1) // 2
    y = x_ref[...]
    for i in range(n):
        y = y + jnp.dot(refs[i][...], refs[n + i][...], preferred_element_type=F32)
    y_ref[...] = y


def _out_proj(x, os, ws, tm=512):
    t, d = x.shape
    row = lambda w: pl.BlockSpec((tm, w), lambda i: (i, 0))
    return pl.pallas_call(
        _out_proj_kernel,
        grid=(t // tm,),
        in_specs=[row(d)] + [row(o.shape[1]) for o in os]
        + [pl.BlockSpec(w.shape, lambda i: (0, 0)) for w in ws],
        out_specs=row(d),
        out_shape=jax.ShapeDtypeStruct((t, d), F32),
        compiler_params=pltpu.CompilerParams(
            dimension_semantics=("arbitrary",), vmem_limit_bytes=VMEM_LIMIT),
        name="out_proj",
    )(x, *os, *ws)


def _silu(g):
    return g * (1.0 / (1.0 + jnp.exp(-g)))


def _ffn_kernel(x_ref, g_ref, wg_ref, wu_ref, wd_ref, y_ref, a_scr):
    x = x_ref[...]
    h = _rms_rows(x, g_ref[...]).astype(BF16)
    f = wg_ref.shape[1]
    for j in range(f // MXU_N):
        sl = slice(j * MXU_N, (j + 1) * MXU_N)
        gate = jnp.dot(h, wg_ref[:, sl], preferred_element_type=F32)
        up = jnp.dot(h, wu_ref[:, sl], preferred_element_type=F32)
        a_scr[:, sl] = (_silu(gate) * up).astype(BF16)
    y_ref[...] = x + jnp.dot(a_scr[...], wd_ref[...], preferred_element_type=F32)


def _ffn(x, g, wg, wu, wd, tm=512):
    t, d = x.shape
    f = wg.shape[1]
    const = lambda a: pl.BlockSpec(a.shape, lambda i: (0, 0), pipeline_mode=pl.Buffered(1))
    return pl.pallas_call(
        _ffn_kernel,
        grid=(t // tm,),
        in_specs=[pl.BlockSpec((tm, d), lambda i: (i, 0)),
                  pl.BlockSpec((1, d), lambda i: (0, 0)),
                  const(wg), const(wu), const(wd)],
        out_specs=pl.BlockSpec((tm, d), lambda i: (i, 0)),
        out_shape=jax.ShapeDtypeStruct((t, d), F32),
        scratch_shapes=[pltpu.VMEM((tm, f), BF16)],
        compiler_params=pltpu.CompilerParams(
            dimension_semantics=("arbitrary",), vmem_limit_bytes=VMEM_LIMIT),
        name="dense_swiglu",
    )(x, g.reshape(1, d), wg, wu, wd)


TOP_K = 2
MOE_TB = 512
MOE_TM = 512
MOE_RC = 256
KEY_STRIDE = 65536.0


def _router_kernel(x_ref, g_ref, wr_ref, h_ref, info_ref, info_t_ref, cnt_ref, carry_scr):
    i = pl.program_id(0)
    tb = x_ref.shape[0]
    lane = lax.broadcasted_iota(jnp.int32, (1, LANES), 1)

    @pl.when(i == 0)
    def _():
        carry_scr[...] = jnp.zeros_like(carry_scr)

    h = _rms_rows(x_ref[...], g_ref[...])
    h_ref[...] = h.astype(BF16)
    logits = jnp.dot(h, wr_ref[...], preferred_element_type=F32,
                     precision=lax.Precision.HIGHEST)
    logits = jnp.where(lane < N_EXPERTS, logits, NEG_INF)
    v1 = jnp.max(logits, axis=-1, keepdims=True)
    i1 = jnp.min(jnp.where(logits == v1, lane, LANES), axis=-1, keepdims=True)
    rest = jnp.where(lane == i1, NEG_INF, logits)
    v2 = jnp.max(rest, axis=-1, keepdims=True)
    i2 = jnp.min(jnp.where(rest == v2, lane, LANES), axis=-1, keepdims=True)
    e2 = jnp.exp(v2 - v1)
    g1 = 1.0 / (1.0 + e2)
    g2 = e2 * g1

    pick1 = lane == i1
    pick2 = lane == i2
    picked = jnp.where(pick1, 1.0, jnp.where(pick2, 1.0, 0.0))
    r = lax.broadcasted_iota(jnp.int32, (tb, tb), 0)
    cidx = lax.broadcasted_iota(jnp.int32, (tb, tb), 1)
    ltri = jnp.where(cidx < r, 1.0, 0.0).astype(BF16)
    before = jnp.dot(ltri, picked.astype(BF16), preferred_element_type=F32) + carry_scr[...]
    rank1 = jnp.sum(jnp.where(pick1, before, 0.0), axis=-1, keepdims=True)
    rank2 = jnp.sum(jnp.where(pick2, before, 0.0), axis=-1, keepdims=True)
    key1 = i1.astype(F32) * KEY_STRIDE + rank1
    key2 = i2.astype(F32) * KEY_STRIDE + rank2
    info = jnp.where(lane == 0, key1, jnp.where(lane == 1, key2,
                     jnp.where(lane == 2, g1, jnp.where(lane == 3, g2, 0.0))))
    info_ref[...] = info
    info_t_ref[...] = info.T[:8, :]
    counts = jnp.sum(picked, axis=0, keepdims=True)
    cnt_ref[...] = jnp.broadcast_to(counts, cnt_ref.shape)
    carry_scr[...] += counts


def _router(x, g, wr):
    t, d = x.shape
    nb = t // MOE_TB
    return pl.pallas_call(
        _router_kernel,
        grid=(nb,),
        in_specs=[pl.BlockSpec((MOE_TB, d), lambda i: (i, 0)),
                  pl.BlockSpec((1, d), lambda i: (0, 0)),
                  pl.BlockSpec(wr.shape, lambda i: (0, 0))],
        out_specs=[pl.BlockSpec((MOE_TB, d), lambda i: (i, 0)),
                   pl.BlockSpec((MOE_TB, LANES), lambda i: (i, 0)),
                   pl.BlockSpec((None, 8, MOE_TB), lambda i: (i, 0, 0)),
                   pl.BlockSpec((None, 8, LANES), lambda i: (i, 0, 0))],
        out_shape=[jax.ShapeDtypeStruct((t, d), BF16),
                   jax.ShapeDtypeStruct((t, LANES), F32),
                   jax.ShapeDtypeStruct((nb, 8, MOE_TB), F32),
                   jax.ShapeDtypeStruct((nb, 8, LANES), F32)],
        scratch_shapes=[pltpu.VMEM((1, LANES), F32)],
        compiler_params=pltpu.CompilerParams(
            dimension_semantics=("arbitrary",), vmem_limit_bytes=VMEM_LIMIT),
        name="moe_router",
    )(x, g.reshape(1, d), wr)


def _moe_plan(block_counts, n_tokens):
    nb = block_counts.shape[0]
    n_tiles = (TOP_K * n_tokens + N_EXPERTS * (MOE_TM - 1)) // MOE_TM
    n_chunks = n_tiles * MOE_TM // MOE_RC
    n_pairs = ((n_chunks + N_EXPERTS * nb + 7) // 8) * 8
    tc = block_counts.astype(jnp.int32)
    cb = jnp.concatenate([jnp.zeros((1, N_EXPERTS), jnp.int32), jnp.cumsum(tc, axis=0)])
    counts = cb[-1]
    padded = ((counts + MOE_TM - 1) // MOE_TM) * MOE_TM
    ends = jnp.cumsum(padded)
    starts = ends - padded
    total = ends[-1]

    tiles = jnp.arange(n_tiles, dtype=jnp.int32)
    n_valid_tiles = total // MOE_TM
    t_valid = (tiles < n_valid_tiles).astype(jnp.int32)
    t_src = jnp.maximum(jnp.minimum(tiles, n_valid_tiles - 1), 0)
    t_exp = jnp.minimum(jnp.sum((t_src * MOE_TM)[:, None] >= ends[None, :], axis=1),
                        N_EXPERTS - 1).astype(jnp.int32)

    r0 = jnp.arange(n_chunks, dtype=jnp.int32) * MOE_RC
    c_exp = jnp.minimum(jnp.sum(r0[:, None] >= ends[None, :], axis=1), N_EXPERTS - 1)
    c_live = r0 < total
    lo = r0 - starts[c_exp]
    base = (c_exp * int(KEY_STRIDE) + lo).astype(jnp.int32)
    cbe = cb[:, c_exp]
    ov = c_live[None, :] & (cbe[:-1] < (lo + MOE_RC)[None, :]) & (cbe[1:] > lo[None, :])
    n_act = jnp.sum(ov).astype(jnp.int32)
    pidx = jnp.arange(n_pairs, dtype=jnp.int32)
    act = pidx < n_act

    def pair_list(flat, inner):
        idx = jnp.nonzero(flat, size=n_pairs, fill_value=0)[0].astype(jnp.int32)
        idx = jnp.where(act, idx, idx[n_act - 1])
        return idx // inner, idx % inner

    _, g_b = pair_list(ov.T.reshape(-1), nb)
    _, c_c = pair_list(ov.reshape(-1), n_chunks)
    g_cnt = jnp.sum(ov, axis=0).astype(jnp.int32)
    g_start = (jnp.cumsum(g_cnt) - g_cnt).astype(jnp.int32)
    c_cnt = jnp.sum(ov, axis=1).astype(jnp.int32)
    c_start = (jnp.cumsum(c_cnt) - c_cnt).astype(jnp.int32)
    chunks = jnp.arange(n_chunks, dtype=jnp.int32)
    g_src = jnp.maximum(jnp.minimum(chunks, total // MOE_RC - 1), 0)
    return dict(n_tiles=n_tiles, n_chunks=n_chunks, t_valid=t_valid, t_src=t_src, t_exp=t_exp,
                base=base, g_src=g_src, g_start=g_start, g_cnt=g_cnt, g_b=g_b,
                c_start=c_start, c_cnt=c_cnt, c_c=c_c)


def _gather_kernel(src_ref, start_ref, cnt_ref, pb_ref, base_ref, info_t_ref, h_ref,
                   xs_ref, gs_ref, acc_scr, gsum_scr):
    c = pl.program_id(0)
    tb = info_t_ref.shape[-1]

    @pl.when(src_ref[c] == c)
    def _():
        first = start_ref[c]
        n = cnt_ref[c]
        row = lax.broadcasted_iota(jnp.int32, (MOE_RC, tb), 0).astype(F32)
        acc_scr[...] = jnp.zeros_like(acc_scr)
        gsum_scr[...] = jnp.zeros_like(gsum_scr)

        def two_blocks(it, carry):
            rows, gate = None, None
            for w in range(2):
                j = 2 * it + w
                b = pb_ref[first + jnp.minimum(j, n - 1)]
                base = jnp.where(j < n, base_ref[c], NO_MATCH_BASE).astype(F32)
                info = info_t_ref[b]
                m1 = (info[0:1, :] - base) == row
                m2 = (info[1:2, :] - base) == row
                sel = jnp.where(m1, 1.0, jnp.where(m2, 1.0, 0.0)).astype(BF16)
                g = jnp.sum(jnp.where(m1, info[2:3, :], jnp.where(m2, info[3:4, :], 0.0)),
                            axis=-1, keepdims=True)
                r = jnp.dot(sel, h_ref[pl.ds(pl.multiple_of(b * tb, tb), tb), :],
                            preferred_element_type=F32)
                rows = r if rows is None else rows + r
                gate = g if gate is None else gate + g
            acc_scr[...] += rows
            gsum_scr[...] += gate
            return carry

        lax.fori_loop(0, (n + 1) // 2, two_blocks, 0)
        xs_ref[...] = acc_scr[...].astype(xs_ref.dtype)
        gs_ref[...] = jnp.broadcast_to(gsum_scr[...], gs_ref.shape)

    @pl.when(src_ref[c] != c)
    def _():
        xs_ref[...] = jnp.zeros_like(xs_ref)
        gs_ref[...] = jnp.zeros_like(gs_ref)


def _moe_gather(plan, info_t, h):
    t, d = h.shape
    rows = plan["n_chunks"] * MOE_RC
    chunk = lambda c, src, st, cn, pb, ba: (c, 0)
    grid_spec = pltpu.PrefetchScalarGridSpec(
        num_scalar_prefetch=5,
        grid=(plan["n_chunks"],),
        in_specs=[pl.BlockSpec(info_t.shape, lambda c, src, st, cn, pb, ba: (0, 0, 0),
                               pipeline_mode=pl.Buffered(1)),
                  pl.BlockSpec((t, d), lambda c, src, st, cn, pb, ba: (0, 0),
                               pipeline_mode=pl.Buffered(1))],
        out_specs=[pl.BlockSpec((MOE_RC, d), chunk), pl.BlockSpec((MOE_RC, LANES), chunk)],
        scratch_shapes=[pltpu.VMEM((MOE_RC, d), F32), pltpu.VMEM((MOE_RC, 1), F32)],
    )
    return pl.pallas_call(
        _gather_kernel,
        grid_spec=grid_spec,
        out_shape=[jax.ShapeDtypeStruct((rows, d), BF16),
                   jax.ShapeDtypeStruct((rows, LANES), F32)],
        compiler_params=pltpu.CompilerParams(
            dimension_semantics=("arbitrary",), vmem_limit_bytes=VMEM_LIMIT),
        name="moe_gather",
    )(plan["g_src"], plan["g_start"], plan["g_cnt"], plan["g_b"], plan["base"], info_t, h)


def _experts_kernel(src_ref, exp_ref, val_ref, xs_ref, gs_ref, wg_ref, wu_ref, wd_ref, ys_ref,
                    acc_scr):
    i = pl.program_id(0)
    j = pl.program_id(1)

    last = j == pl.num_programs(1) - 1

    @pl.when(j == 0)
    def _():
        acc_scr[...] = jnp.zeros_like(acc_scr)

    @pl.when(val_ref[i] == 1)
    def _():
        x = xs_ref[...]
        gate = jnp.dot(x, wg_ref[...], preferred_element_type=F32)
        up = jnp.dot(x, wu_ref[...], preferred_element_type=F32)
        a = (_silu(gate) * up).astype(BF16)
        acc_scr[...] += jnp.dot(a, wd_ref[...], preferred_element_type=F32)

    @pl.when(last)
    def _():
        ys_ref[...] = (acc_scr[...] * gs_ref[:, 0:1]).astype(ys_ref.dtype)


def _moe_experts(plan, xs, gs, wg, wu, wd, tf):
    rows, d = xs.shape
    f = wg.shape[2]
    nf = f // tf

    def wcol(i, j, src, ex, val):
        return (ex[i], 0, jnp.where(val[i] == 1, j, nf - 1))

    def wrow(i, j, src, ex, val):
        return (ex[i], jnp.where(val[i] == 1, j, nf - 1), 0)

    tile = lambda i, j, src, ex, val: (i, 0)
    grid_spec = pltpu.PrefetchScalarGridSpec(
        num_scalar_prefetch=3,
        grid=(plan["n_tiles"], nf),
        in_specs=[pl.BlockSpec((MOE_TM, d), tile),
                  pl.BlockSpec((MOE_TM, LANES), tile),
                  pl.BlockSpec((None, d, tf), wcol),
                  pl.BlockSpec((None, d, tf), wcol),
                  pl.BlockSpec((None, tf, d), wrow)],
        out_specs=pl.BlockSpec((MOE_TM, d), tile),
        scratch_shapes=[pltpu.VMEM((MOE_TM, d), F32)],
    )
    return pl.pallas_call(
        _experts_kernel,
        grid_spec=grid_spec,
        out_shape=jax.ShapeDtypeStruct((rows, d), BF16),
        compiler_params=pltpu.CompilerParams(
            dimension_semantics=("arbitrary", "arbitrary"), vmem_limit_bytes=VMEM_LIMIT),
        name="moe_experts",
    )(plan["t_src"], plan["t_exp"], plan["t_valid"], xs, gs, wg, wu, wd)


NO_MATCH_BASE = -(2 ** 30)


COMBINE_GROUP = 4
COMBINE_MAX_CHUNKS = N_EXPERTS * (MOE_TB // MOE_RC + 1)


def _combine_kernel(start_ref, cnt_ref, pc_ref, base_ref, info_ref, x_ref, ys_hbm, o_ref,
                    ybuf, sem):
    b = pl.program_id(0)
    tb = x_ref.shape[0]
    d = x_ref.shape[1]

    def n_fetch(blk):
        return ((cnt_ref[blk] + COMBINE_GROUP - 1) // COMBINE_GROUP) * COMBINE_GROUP

    def chunk_of(blk, j):
        return pc_ref[start_ref[blk] + jnp.minimum(j, cnt_ref[blk] - 1)]

    def copy(blk, j, slot):
        row0 = pl.multiple_of(chunk_of(blk, j) * MOE_RC, MOE_RC)
        return pltpu.make_async_copy(ys_hbm.at[pl.ds(row0, MOE_RC), :], ybuf.at[slot, j],
                                     sem.at[slot])

    def start_block(blk, slot):
        def issue(j, carry):
            copy(blk, j, slot).start()
            return carry
        lax.fori_loop(0, n_fetch(blk), issue, 0)

    @pl.when(b == 0)
    def _():
        start_block(0, 0)

    @pl.when(b + 1 < pl.num_programs(0))
    def _():
        start_block(b + 1, (b + 1) % 2)

    slot = b % 2
    n = cnt_ref[b]

    def drain(j, carry):
        copy(b, j, slot).wait()
        return carry

    lax.fori_loop(0, n_fetch(b), drain, 0)

    lane = lax.broadcasted_iota(jnp.int32, (1, LANES), 1)
    info = info_ref[...]
    k1 = jnp.sum(jnp.where(lane == 0, info, 0.0), axis=-1, keepdims=True)
    k2 = jnp.sum(jnp.where(lane == 1, info, 0.0), axis=-1, keepdims=True)
    col = lax.broadcasted_iota(jnp.int32, (tb, MOE_RC), 1).astype(F32)
    o_ref[...] = x_ref[...]

    def group(gi, carry):
        sels = []
        for w in range(COMBINE_GROUP):
            j = gi * COMBINE_GROUP + w
            base = jnp.where(j < n, base_ref[chunk_of(b, j)], NO_MATCH_BASE).astype(F32)
            sels.append(jnp.where(k1 - base == col, 1.0,
                                  jnp.where(k2 - base == col, 1.0, 0.0)).astype(BF16))
        rows = ybuf[slot, pl.ds(gi * COMBINE_GROUP, COMBINE_GROUP)]
        o_ref[...] += jnp.dot(jnp.concatenate(sels, axis=1),
                              rows.reshape(COMBINE_GROUP * MOE_RC, d),
                              preferred_element_type=F32)
        return carry

    lax.fori_loop(0, n_fetch(b) // COMBINE_GROUP, group, 0)


def _moe_combine(plan, info, ys, x):
    t, d = x.shape
    block = lambda b, st, cn, pc, ba: (b, 0)
    grid_spec = pltpu.PrefetchScalarGridSpec(
        num_scalar_prefetch=4,
        grid=(t // MOE_TB,),
        in_specs=[pl.BlockSpec((MOE_TB, LANES), block),
                  pl.BlockSpec((MOE_TB, d), block),
                  pl.BlockSpec(memory_space=pl.ANY)],
        out_specs=pl.BlockSpec((MOE_TB, d), block),
        scratch_shapes=[pltpu.VMEM((2, COMBINE_MAX_CHUNKS, MOE_RC, d), BF16),
                        pltpu.SemaphoreType.DMA((2,))],
    )
    return pl.pallas_call(
        _combine_kernel,
        grid_spec=grid_spec,
        out_shape=jax.ShapeDtypeStruct((t, d), F32),
        compiler_params=pltpu.CompilerParams(
            dimension_semantics=("arbitrary",), vmem_limit_bytes=VMEM_LIMIT),
        name="moe_combine",
    )(plan["c_start"], plan["c_cnt"], plan["c_c"], plan["base"], info, x, ys)


def _moe(x, g, wr, wg, wu, wd, tf=1792):
    t, _ = x.shape
    h, info, info_t, block_counts = _router(x, g, wr)
    plan = _moe_plan(block_counts[:, 0, :N_EXPERTS], t)
    xs, gs = _moe_gather(plan, info_t, h)
    ys = _moe_experts(plan, xs, gs, wg, wu, wd, tf)
    return _moe_combine(plan, info, ys, x)


def _alibi_slopes():
    s = np.exp2(-8.0 * np.arange(1, N_HEADS + 1) / N_HEADS).astype(np.float32)
    return s[0::2][:A_HEADS], s[1::2][:B_Q_HEADS]


_B_HEAD_ORDER = np.array([0, 4, 1, 5, 2, 6, 3, 7])


def _even_layer(x2d, batch, norm1, w_in, qn_a, kn_a, qn_b, kn_b, sink_b, w_out,
                norm2, w_gate, w_up, w_down):
    t, d = x2d.shape
    s = t // batch
    wa = A_HEADS * HEAD_DIM
    wq_b = B_Q_HEADS * HEAD_DIM
    wkv_b = B_KV_HEADS * HEAD_DIM
    qb0 = 3 * wa
    q_scale = HEAD_DIM ** -0.5

    w_qb = w_in[:, qb0:qb0 + wq_b].reshape(d, B_Q_HEADS, HEAD_DIM)[:, _B_HEAD_ORDER]
    w_in_p = jnp.concatenate(
        [w_in[:, :qb0], w_qb.reshape(d, wq_b), w_in[:, qb0 + wq_b:]], axis=1).astype(BF16)
    ones = jnp.ones((HEAD_DIM,), F32)
    col_gain = jnp.concatenate([
        jnp.tile(qn_a * q_scale, A_HEADS), jnp.tile(kn_a, A_HEADS), jnp.tile(ones, A_HEADS),
        jnp.tile(qn_b * q_scale, B_Q_HEADS), jnp.tile(kn_b, B_KV_HEADS),
        jnp.tile(ones, B_KV_HEADS)])
    cpl = wa // LANES
    norm_chunks = ((True,) * (2 * cpl) + (False,) * cpl + (True,) * (wq_b // LANES)
                   + (True,) * (wkv_b // LANES) + (False,) * (wkv_b // LANES))
    qkv = _norm_proj(x2d, norm1, w_in_p, col_gain, norm_chunks)
    n_cols = qkv.shape[1]
    qkv3 = qkv.reshape(batch, s, n_cols)

    slopes_a, slopes_b = _alibi_slopes()
    par_a = jnp.asarray(slopes_a).reshape(1, A_HEADS)
    radii = {window // (2 * dil) for window, dil in DIL_CONFIGS}
    assert tuple(dil for _, dil in DIL_CONFIGS) == (1, 4, 16) and len(radii) == 1
    oa = _dilated_attention(qkv3, par_a, k_blk0=cpl, v_blk0=2 * cpl, n_pairs=cpl,
                            blk=A_BLOCK, radius=radii.pop())
    par_b = jnp.stack([jnp.asarray(slopes_b[_B_HEAD_ORDER]),
                       sink_b.astype(F32)[_B_HEAD_ORDER]])
    kb = 3 * cpl + wq_b // LANES
    ob = _window_attention(qkv3, par_b, q_blk0=3 * cpl, k_blk=kb, v_blk=kb + wkv_b // LANES,
                           n_pairs=wq_b // LANES, radius=B_WINDOW, blk=B_BLOCK)
    w_out_a = w_out[:wa].astype(BF16)
    w_out_b = w_out[wa:].reshape(B_Q_HEADS, HEAD_DIM, d)[_B_HEAD_ORDER].reshape(wq_b, d)
    x1 = _out_proj(x2d, [oa.reshape(t, wa), ob.reshape(t, wq_b)],
                   [w_out_a, w_out_b.astype(BF16)])
    return _ffn(x1, norm2, w_gate.astype(BF16), w_up.astype(BF16), w_down.astype(BF16))


def _odd_layer(x2d, batch, norm1, w_qkv, qn, kn, rpb, w_out, norm2, w_router,
               w_gate, w_up, w_down):
    t, d = x2d.shape
    s = t // batch
    wc = C_HEADS * HEAD_DIM
    ones = jnp.ones((HEAD_DIM,), F32)
    col_gain = jnp.concatenate([jnp.tile(qn * HEAD_DIM ** -0.5, C_HEADS),
                                jnp.tile(kn, C_HEADS), jnp.tile(ones, C_HEADS)])
    cpl = wc // LANES
    norm_chunks = (True,) * (2 * cpl) + (False,) * cpl
    qkv = _norm_proj(x2d, norm1, w_qkv.astype(BF16), col_gain, norm_chunks)
    o = _na_attention(qkv.reshape(batch, s, 3 * wc), _na_bias_table(rpb))
    x1 = _out_proj(x2d, [o.reshape(t, wc)], [w_out.astype(BF16)])
    wr = jnp.zeros((d, LANES), F32).at[:, :N_EXPERTS].set(w_router.astype(F32))
    return _moe(x1, norm2, wr, w_gate.astype(BF16), w_up.astype(BF16), w_down.astype(BF16))


def kernel(x, ev_norm1, ev_w_in, ev_qn_a, ev_kn_a, ev_qn_b, ev_kn_b, ev_sink_b, ev_w_out, ev_norm2, ev_ffn_gate, ev_ffn_up, ev_ffn_down, od_norm1, od_w_qkv, od_qn, od_kn, od_rpb, od_w_out, od_norm2, od_router, od_exp_gate, od_exp_up, od_exp_down):
    batch, s, d = x.shape
    depth = ev_norm1.shape[0] + od_norm1.shape[0]
    h = x.reshape(batch * s, d)
    for layer in range(depth):
        j = layer // 2
        if layer % 2 == 0:
            h = _even_layer(h, batch, ev_norm1[j], ev_w_in[j], ev_qn_a[j], ev_kn_a[j],
                            ev_qn_b[j], ev_kn_b[j], ev_sink_b[j], ev_w_out[j], ev_norm2[j],
                            ev_ffn_gate[j], ev_ffn_up[j], ev_ffn_down[j])
        else:
            h = _odd_layer(h, batch, od_norm1[j], od_w_qkv[j], od_qn[j], od_kn[j], od_rpb[j],
                           od_w_out[j], od_norm2[j], od_router[j], od_exp_gate[j],
                           od_exp_up[j], od_exp_down[j])
    return h.reshape(batch, s, d)
```

```python
import functools

import numpy as np
import jax
import jax.numpy as jnp
from jax import lax
from jax.experimental import pallas as pl
from jax.experimental.pallas import tpu as pltpu

D_MODEL = 1024
HEAD_DIM = 64
N_HEADS = D_MODEL // HEAD_DIM
A_HEADS = N_HEADS // 2
B_Q_HEADS = N_HEADS // 2
B_KV_HEADS = max(1, B_Q_HEADS // 4)
C_HEADS = N_HEADS
DIL_CONFIGS = ((128, 1), (512, 4), (2048, 16))
A_BLOCK = 128
B_WINDOW = 128
B_BLOCK = 128
GRID_W = 64
NA_ROWS = 8
NA_COLS = 16
N_EXPERTS = 8
RMS_EPS = 1e-6
NEG_INF = -1e30

LANES = 128
MXU_N = 256
VMEM_LIMIT = 56 * 1024 * 1024

ATTN_UNROLL = 8

F32 = jnp.float32
BF16 = jnp.bfloat16


def _lo_mask(shape):
    return lax.broadcasted_iota(jnp.int32, shape, len(shape) - 1) < HEAD_DIM


def _rms_rows(x, g):
    ms = jnp.mean(x * x, axis=-1, keepdims=True)
    return x * lax.rsqrt(ms + RMS_EPS) * g


def _norm_proj_kernel(x_ref, g_ref, w_ref, cg_ref, o_ref, *, norm_chunks):
    h = _rms_rows(x_ref[...], g_ref[...]).astype(BF16)
    n_out = o_ref.shape[-1]
    lo = _lo_mask((1, LANES))
    for j in range(n_out // MXU_N):
        y = jnp.dot(h, w_ref[:, j * MXU_N:(j + 1) * MXU_N], preferred_element_type=F32)
        for half in range(MXU_N // LANES):
            c = j * (MXU_N // LANES) + half
            yc = y[:, half * LANES:(half + 1) * LANES]
            if norm_chunks[c]:
                sq = yc * yc
                s_lo = jnp.sum(jnp.where(lo, sq, 0.0), axis=-1, keepdims=True)
                s_hi = jnp.sum(jnp.where(lo, 0.0, sq), axis=-1, keepdims=True)
                inv = jnp.where(lo, lax.rsqrt(s_lo * (1.0 / HEAD_DIM) + RMS_EPS),
                                lax.rsqrt(s_hi * (1.0 / HEAD_DIM) + RMS_EPS))
                yc = yc * inv * cg_ref[:, c * LANES:(c + 1) * LANES]
            o_ref[:, c * LANES:(c + 1) * LANES] = yc.astype(o_ref.dtype)


def _norm_proj(x, g, w, col_gain, norm_chunks, tm=512):
    t, d = x.shape
    n = w.shape[1]
    return pl.pallas_call(
        functools.partial(_norm_proj_kernel, norm_chunks=norm_chunks),
        grid=(t // tm,),
        in_specs=[
            pl.BlockSpec((tm, d), lambda i: (i, 0)),
            pl.BlockSpec((1, d), lambda i: (0, 0)),
            pl.BlockSpec((d, n), lambda i: (0, 0)),
            pl.BlockSpec((1, n), lambda i: (0, 0)),
        ],
        out_specs=pl.BlockSpec((tm, n), lambda i: (i, 0)),
        out_shape=jax.ShapeDtypeStruct((t, n), BF16),
        compiler_params=pltpu.CompilerParams(
            dimension_semantics=("arbitrary",), vmem_limit_bytes=VMEM_LIMIT),
        name="norm_proj",
    )(x, g.reshape(1, d), w, col_gain.reshape(1, n))


def _window_kernel(par_ref, q_ref, k_ref, v_ref, o_ref, bias_scr, *, seq, blk, radius):
    span = blk + 2 * radius
    c = pl.program_id(1)
    group = ATTN_UNROLL
    row_lo = lax.broadcasted_iota(jnp.int32, (2 * blk, 1), 0) < blk
    slope_col = jnp.where(row_lo, par_ref[0, 2 * c], par_ref[0, 2 * c + 1])
    sink_col = jnp.where(row_lo, par_ref[1, 2 * c], par_ref[1, 2 * c + 1])

    iq = lax.broadcasted_iota(jnp.int32, (2 * blk, span), 0) % blk
    ik = lax.broadcasted_iota(jnp.int32, (2 * blk, span), 1)
    _fill_band_bias(bias_scr, iq, ik, radius, 1, slope_col)
    lo_q = _lo_mask((1, LANES))
    lo = _lo_mask((blk, LANES))

    def body(g, carry):
        starts, variants, values, scores = [], [], [], []
        for u in range(group):
            a = pl.multiple_of((g * group + u) * blk, blk)
            ks = pl.multiple_of(jnp.clip(a - radius, 0, seq - span), radius)
            q = q_ref[pl.ds(a, blk), :]
            zero = jnp.zeros_like(q)
            qm = jnp.concatenate([jnp.where(lo_q, q, zero), jnp.where(lo_q, zero, q)], axis=0)
            k = k_ref[pl.ds(ks, span), :]
            scores.append(lax.dot_general(qm, k, (((1,), (1,)), ((), ())),
                                          preferred_element_type=F32))
            starts.append(a)
            variants.append((a - ks) // radius)
            values.append(v_ref[pl.ds(ks, span), :])
        probs, dens = [], []
        for u in range(group):
            s = scores[u] + bias_scr[variants[u]]
            m = jnp.maximum(jnp.max(s, axis=-1, keepdims=True), sink_col)
            p = jnp.exp(s - m)
            dens.append(jnp.sum(p, axis=-1, keepdims=True) + jnp.exp(sink_col - m))
            probs.append(p.astype(BF16))
        for u in range(group):
            o = jnp.dot(probs[u], values[u], preferred_element_type=F32) * (1.0 / dens[u])
            o_ref[pl.ds(starts[u], blk), :] = jnp.where(lo, o[:blk], o[blk:]).astype(o_ref.dtype)
        return carry

    lax.fori_loop(0, seq // (blk * group), body, 0)


def _window_attention(qkv, params, *, q_blk0, k_blk, v_blk, n_pairs, radius, blk):
    b, s, _ = qkv.shape
    span = blk + 2 * radius
    return pl.pallas_call(
        functools.partial(_window_kernel, seq=s, blk=blk, radius=radius),
        grid=(b, n_pairs),
        in_specs=[
            pl.BlockSpec(memory_space=pltpu.SMEM),
            pl.BlockSpec((None, s, LANES), lambda bi, c: (bi, 0, q_blk0 + c)),
            pl.BlockSpec((None, s, LANES), lambda bi, c: (bi, 0, k_blk)),
            pl.BlockSpec((None, s, LANES), lambda bi, c: (bi, 0, v_blk)),
        ],
        out_specs=pl.BlockSpec((None, s, LANES), lambda bi, c: (bi, 0, c)),
        out_shape=jax.ShapeDtypeStruct((b, s, n_pairs * LANES), BF16),
        scratch_shapes=[pltpu.VMEM((3, 2 * blk, span), F32)],
        compiler_params=pltpu.CompilerParams(
            dimension_semantics=("arbitrary", "arbitrary"), vmem_limit_bytes=VMEM_LIMIT),
        name="window_attn",
    )(params, qkv, qkv, qkv)


PERM_CHUNK = 256
PHASES4 = 4
PIECE = PERM_CHUNK // PHASES4


def _stack_heads(q):
    lo = _lo_mask((1, LANES))
    zero = jnp.zeros_like(q)
    return jnp.concatenate([jnp.where(lo, q, zero), jnp.where(lo, zero, q)], axis=0)


def _merge_heads(x, blk):
    x = jnp.broadcast_to(x, (2 * blk, LANES))
    return jnp.where(_lo_mask((blk, LANES)), x[:blk], x[blk:])


def _attend_group(qs, ks, vs, biases):
    scores = [lax.dot_general(_stack_heads(q), k, (((1,), (1,)), ((), ())),
                              preferred_element_type=F32) for q, k in zip(qs, ks)]
    stats = []
    for s, bias in zip(scores, biases):
        s = s + bias
        m = jnp.max(s, axis=-1, keepdims=True)
        p = jnp.exp(s - m)
        stats.append((p.astype(BF16), m, jnp.sum(p, axis=-1, keepdims=True)))
    return [(jnp.dot(p, v, preferred_element_type=F32), m, l)
            for (p, m, l), v in zip(stats, vs)]


def _fill_band_bias(bias_scr, pos_q, pos_k, radius, pos_scale, slope_col):
    for var in range(3):
        dist = jnp.abs(pos_q + var * radius - pos_k)
        dist_f = (dist * pos_scale).astype(F32)
        bias_scr[var] = jnp.where(dist <= radius, -(slope_col * dist_f), NEG_INF)


def _dilated_kernel(par_ref, q_ref, k_ref, v_ref, o_ref, qp, kp, vp, acc_p, m_p, l_p,
                    bias1, bias4, bias16, *, seq, blk, radius):
    span = blk + 2 * radius
    n_chunk = seq // PERM_CHUNK
    group = ATTN_UNROLL
    c = pl.program_id(1)
    row_lo = lax.broadcasted_iota(jnp.int32, (2 * blk, 1), 0) < blk
    slope_col = jnp.where(row_lo, par_ref[0, 2 * c], par_ref[0, 2 * c + 1])

    def piece_pos(x):
        return PIECE * (x // PIECE) + PHASES4 * (x % 16) + (x % PIECE) // 16

    iq = lax.broadcasted_iota(jnp.int32, (2 * blk, span), 0) % blk
    ik = lax.broadcasted_iota(jnp.int32, (2 * blk, span), 1)
    _fill_band_bias(bias1, iq, ik, radius, 1, slope_col)
    _fill_band_bias(bias4, piece_pos(iq), piece_pos(ik), radius, 4, slope_col)
    _fill_band_bias(bias16, iq, ik, radius, 16, slope_col)

    rr = lax.broadcasted_iota(jnp.int32, (PERM_CHUNK, PERM_CHUNK), 0)
    cc = lax.broadcasted_iota(jnp.int32, (PERM_CHUNK, PERM_CHUNK), 1)
    to_phase = jnp.where(cc == 16 * (rr % 16) + PHASES4 * ((rr // 16) % 4) + rr // PIECE,
                         1.0, 0.0).astype(BF16)
    to_token = jnp.where(cc == PIECE * (rr % PHASES4) + 16 * ((rr // PHASES4) % 4) + rr // 16,
                         1.0, 0.0).astype(BF16)

    def permute_body(ch, carry):
        r0 = pl.multiple_of(ch * PERM_CHUNK, PERM_CHUNK)
        x = jnp.concatenate([q_ref[pl.ds(r0, PERM_CHUNK), :], k_ref[pl.ds(r0, PERM_CHUNK), :],
                             v_ref[pl.ds(r0, PERM_CHUNK), :]], axis=1)
        y = jnp.dot(to_phase, x, preferred_element_type=F32).astype(BF16)
        for p4 in range(PHASES4):
            rows = slice(PIECE * p4, PIECE * (p4 + 1))
            qp[p4, ch] = y[rows, 0:LANES]
            kp[p4, ch] = y[rows, LANES:2 * LANES]
            vp[p4, ch] = y[rows, 2 * LANES:3 * LANES]
        return carry

    lax.fori_loop(0, n_chunk, permute_body, 0)

    pieces_per_blk = blk // PIECE
    n_blk4 = seq // (PHASES4 * blk)

    def dil4_body(g, carry):
        p4 = g // (n_blk4 // group)
        j0 = (g % (n_blk4 // group)) * group
        qs, ks, vs, bs = [], [], [], []
        for u in range(group):
            pc = pieces_per_blk * (j0 + u)
            cs = jnp.clip(pc - radius // PIECE, 0, n_chunk - span // PIECE)
            qs.append(qp[p4, pl.ds(pc, pieces_per_blk)].reshape(blk, LANES))
            ks.append(kp[p4, pl.ds(cs, span // PIECE)].reshape(span, LANES))
            vs.append(vp[p4, pl.ds(cs, span // PIECE)].reshape(span, LANES))
            bs.append(bias4[(pc - cs) // (radius // PIECE)])
        for u, (o, m, l) in enumerate(_attend_group(qs, ks, vs, bs)):
            dst = (p4, pl.ds(pieces_per_blk * (j0 + u), pieces_per_blk))
            acc_p[dst] = _merge_heads(o, blk).reshape(pieces_per_blk, PIECE, LANES)
            m_p[dst] = _merge_heads(m, blk).reshape(pieces_per_blk, PIECE, LANES)
            l_p[dst] = _merge_heads(l, blk).reshape(pieces_per_blk, PIECE, LANES)
        return carry

    lax.fori_loop(0, PHASES4 * n_blk4 // group, dil4_body, 0)

    chunks_per_blk = blk // 16

    def dil16_body(g, carry):
        p4 = g // 2
        qs, ks, vs, bs, dsts = [], [], [], [], []
        for mm in range(2):
            off = pl.multiple_of((2 * (g % 2) + mm) * 16, 16)
            kk = kp[p4, :, pl.ds(off, 16), :].reshape(span, LANES)
            vv = vp[p4, :, pl.ds(off, 16), :].reshape(span, LANES)
            for jb in range(n_chunk // chunks_per_blk):
                dst = (p4, pl.ds(chunks_per_blk * jb, chunks_per_blk), pl.ds(off, 16), slice(None))
                qs.append(qp[dst].reshape(blk, LANES))
                ks.append(kk)
                vs.append(vv)
                bs.append(bias16[jb * (blk // radius)])
                dsts.append(dst)
        for dst, (o, m, l) in zip(dsts, _attend_group(qs, ks, vs, bs)):
            m_old = m_p[dst].reshape(blk, LANES)
            m_new = jnp.maximum(m_old, _merge_heads(m, blk))
            w_old = jnp.exp(m_old - m_new)
            w_new = jnp.exp(_merge_heads(m, blk) - m_new)
            acc = w_old * acc_p[dst].reshape(blk, LANES) + w_new * _merge_heads(o, blk)
            den = w_old * l_p[dst].reshape(blk, LANES) + w_new * _merge_heads(l, blk)
            acc_p[dst] = acc.reshape(chunks_per_blk, 16, LANES)
            l_p[dst] = den.reshape(chunks_per_blk, 16, LANES)
            m_p[dst] = m_new.reshape(chunks_per_blk, 16, LANES)
        return carry

    lax.fori_loop(0, PHASES4 * 2, dil16_body, 0)

    blk_per_chunk = PERM_CHUNK // blk

    def dil1_body(g, carry):
        merged = []
        for cc2 in range(group // blk_per_chunk):
            ch = g * (group // blk_per_chunk) + cc2
            acc = jnp.concatenate([acc_p[p4, ch] for p4 in range(PHASES4)], axis=0)
            den = jnp.concatenate([l_p[p4, ch] for p4 in range(PHASES4)], axis=0)
            mx = jnp.concatenate([m_p[p4, ch] for p4 in range(PHASES4)], axis=0)
            o_ph = (acc * (1.0 / den)).astype(BF16)
            lse = mx + jnp.log(den)
            hi = lse.astype(BF16)
            rem = lse - hi.astype(F32)
            mid = rem.astype(BF16)
            low = (rem - mid.astype(F32)).astype(BF16)
            back = jnp.dot(to_token, jnp.concatenate([o_ph, hi, mid, low], axis=1),
                           preferred_element_type=F32)
            merged.append((back[:, 0:LANES], back[:, LANES:2 * LANES]
                           + back[:, 2 * LANES:3 * LANES] + back[:, 3 * LANES:4 * LANES]))
        qs, ks, vs, bs, starts = [], [], [], [], []
        for u in range(group):
            a = pl.multiple_of((g * group + u) * blk, blk)
            kst = pl.multiple_of(jnp.clip(a - radius, 0, seq - span), radius)
            qs.append(q_ref[pl.ds(a, blk), :])
            ks.append(k_ref[pl.ds(kst, span), :])
            vs.append(v_ref[pl.ds(kst, span), :])
            bs.append(bias1[(a - kst) // radius])
            starts.append(a)
        for u, (o, m, l) in enumerate(_attend_group(qs, ks, vs, bs)):
            o1 = _merge_heads(o * (1.0 / l), blk)
            lse1 = _merge_heads(m + jnp.log(l), blk)
            o2, lse2 = merged[u // blk_per_chunk]
            half = slice((u % blk_per_chunk) * blk, (u % blk_per_chunk + 1) * blk)
            o2, lse2 = o2[half], lse2[half]
            top = jnp.maximum(lse1, lse2)
            e1 = jnp.exp(lse1 - top)
            e2 = jnp.exp(lse2 - top)
            o_ref[pl.ds(starts[u], blk), :] = ((e1 * o1 + e2 * o2)
                                               * (1.0 / (e1 + e2))).astype(o_ref.dtype)
        return carry

    lax.fori_loop(0, seq // (blk * group), dil1_body, 0)


def _dilated_attention(qkv, params, *, k_blk0, v_blk0, n_pairs, blk, radius):
    b, s, _ = qkv.shape
    span = blk + 2 * radius
    n_chunk = s // PERM_CHUNK
    assert s // 16 == span and s % PERM_CHUNK == 0 and blk % PIECE == 0
    phase = lambda dt: pltpu.VMEM((PHASES4, n_chunk, PIECE, LANES), dt)
    return pl.pallas_call(
        functools.partial(_dilated_kernel, seq=s, blk=blk, radius=radius),
        grid=(b, n_pairs),
        in_specs=[
            pl.BlockSpec(memory_space=pltpu.SMEM),
            pl.BlockSpec((None, s, LANES), lambda bi, c: (bi, 0, c)),
            pl.BlockSpec((None, s, LANES), lambda bi, c: (bi, 0, k_blk0 + c)),
            pl.BlockSpec((None, s, LANES), lambda bi, c: (bi, 0, v_blk0 + c)),
        ],
        out_specs=pl.BlockSpec((None, s, LANES), lambda bi, c: (bi, 0, c)),
        out_shape=jax.ShapeDtypeStruct((b, s, n_pairs * LANES), BF16),
        scratch_shapes=[phase(BF16), phase(BF16), phase(BF16), phase(F32), phase(F32), phase(F32)]
        + [pltpu.VMEM((3, 2 * blk, span), F32)] * 3,
        compiler_params=pltpu.CompilerParams(
            dimension_semantics=("arbitrary", "arbitrary"), vmem_limit_bytes=VMEM_LIMIT),
        name="dilated_attn",
    )(params, qkv, qkv, qkv)


def _na_kernel(q_ref, k_ref, v_ref, bias_ref, o_ref, *, rows):
    kh = NA_ROWS
    group = ATTN_UNROLL
    lo_q = _lo_mask((1, LANES))
    lo = _lo_mask((GRID_W, LANES))

    def body(g, carry):
        starts, offs, values, scores = [], [], [], []
        for u in range(group):
            i = g * group + u
            rs = jnp.clip(i - kh // 2, 0, rows - kh)
            qs = pl.multiple_of(i * GRID_W, GRID_W)
            ks = pl.multiple_of(rs * GRID_W, GRID_W)
            q = q_ref[pl.ds(qs, GRID_W), :]
            zero = jnp.zeros_like(q)
            qm = jnp.concatenate([jnp.where(lo_q, q, zero), jnp.where(lo_q, zero, q)], axis=0)
            k = k_ref[pl.ds(ks, kh * GRID_W), :]
            scores.append(lax.dot_general(qm, k, (((1,), (1,)), ((), ())),
                                          preferred_element_type=F32))
            starts.append(qs)
            offs.append(rs - i + (NA_ROWS - 1))
            values.append(v_ref[pl.ds(ks, kh * GRID_W), :])
        probs, dens = [], []
        for u in range(group):
            s = scores[u] + jnp.concatenate(
                [bias_ref[offs[u] + 2 * j] for j in range(kh // 2)], axis=1)
            m = jnp.max(s, axis=-1, keepdims=True)
            p = jnp.exp(s - m)
            dens.append(jnp.sum(p, axis=-1, keepdims=True))
            probs.append(p.astype(BF16))
        for u in range(group):
            o = jnp.dot(probs[u], values[u], preferred_element_type=F32) * (1.0 / dens[u])
            o_ref[pl.ds(starts[u], GRID_W), :] = jnp.where(
                lo, o[:GRID_W], o[GRID_W:]).astype(o_ref.dtype)
        return carry

    lax.fori_loop(0, rows // group, body, 0)


def _na_bias_table(rpb):
    qc = np.arange(GRID_W)[:, None]
    kc = np.arange(GRID_W)[None, :]
    qstart = np.clip(qc - NA_COLS // 2, 0, GRID_W - NA_COLS)
    valid = (kc >= qstart) & (kc < qstart + NA_COLS)
    coff = np.clip(kc - qc, -(NA_COLS - 1), NA_COLS - 1) + NA_COLS - 1
    pick = (coff[:, :, None] == np.arange(2 * NA_COLS - 1)).astype(np.float32)
    t = jnp.einsum("hdj,qkj->hdqk", rpb.astype(F32), jnp.asarray(pick),
                   precision=lax.Precision.HIGHEST)
    t = jnp.where(jnp.asarray(valid), t, NEG_INF)
    t = jnp.concatenate([t[:, :-1], t[:, 1:]], axis=-1)
    n_off = 2 * NA_ROWS - 2
    t = t.reshape(C_HEADS // 2, 2, n_off, GRID_W, 2 * GRID_W).transpose(0, 2, 1, 3, 4)
    return t.reshape(C_HEADS // 2, n_off, 2 * GRID_W, 2 * GRID_W)


def _na_attention(qkv, bias):
    b, s, n_cols = qkv.shape
    n_pairs = C_HEADS // 2
    rows = s // GRID_W
    return pl.pallas_call(
        functools.partial(_na_kernel, rows=rows),
        grid=(b, n_pairs),
        in_specs=[
            pl.BlockSpec((None, s, LANES), lambda bi, c: (bi, 0, c)),
            pl.BlockSpec((None, s, LANES), lambda bi, c: (bi, 0, n_pairs + c)),
            pl.BlockSpec((None, s, LANES), lambda bi, c: (bi, 0, 2 * n_pairs + c)),
            pl.BlockSpec((None,) + bias.shape[1:], lambda bi, c: (c, 0, 0, 0)),
        ],
        out_specs=pl.BlockSpec((None, s, LANES), lambda bi, c: (bi, 0, c)),
        out_shape=jax.ShapeDtypeStruct((b, s, n_pairs * LANES), BF16),
        compiler_params=pltpu.CompilerParams(
            dimension_semantics=("arbitrary", "arbitrary"), vmem_limit_bytes=VMEM_LIMIT),
        name="na_attn",
    )(qkv, qkv, qkv, bias)


def _silu(g):
    return g * (1.0 / (1.0 + jnp.exp(-g)))


def _ffn_kernel(x_ref, oa_ref, ob_ref, wa_ref, wb_ref, g_ref, wg_ref, wu_ref, wd_ref, y_ref,
                a_scr):
    x = (x_ref[...] + jnp.dot(oa_ref[...], wa_ref[...], preferred_element_type=F32)
         + jnp.dot(ob_ref[...], wb_ref[...], preferred_element_type=F32))
    h = _rms_rows(x, g_ref[...]).astype(BF16)
    f = wg_ref.shape[1]
    for j in range(f // MXU_N):
        sl = slice(j * MXU_N, (j + 1) * MXU_N)
        gate = jnp.dot(h, wg_ref[:, sl], preferred_element_type=F32)
        up = jnp.dot(h, wu_ref[:, sl], preferred_element_type=F32)
        a_scr[:, sl] = (_silu(gate) * up).astype(BF16)
    y_ref[...] = x + jnp.dot(a_scr[...], wd_ref[...], preferred_element_type=F32)


def _ffn(x, oa, ob, wa, wb, g, wg, wu, wd, tm=512):
    t, d = x.shape
    f = wg.shape[1]
    const = lambda a: pl.BlockSpec(a.shape, lambda i: (0, 0), pipeline_mode=pl.Buffered(1))
    row = lambda w: pl.BlockSpec((tm, w), lambda i: (i, 0))
    return pl.pallas_call(
        _ffn_kernel,
        grid=(t // tm,),
        in_specs=[row(d), row(oa.shape[1]), row(ob.shape[1]), const(wa), const(wb),
                  pl.BlockSpec((1, d), lambda i: (0, 0)),
                  const(wg), const(wu), const(wd)],
        out_specs=pl.BlockSpec((tm, d), lambda i: (i, 0)),
        out_shape=jax.ShapeDtypeStruct((t, d), F32),
        scratch_shapes=[pltpu.VMEM((tm, f), BF16)],
        compiler_params=pltpu.CompilerParams(
            dimension_semantics=("arbitrary",), vmem_limit_bytes=VMEM_LIMIT),
        name="dense_swiglu",
    )(x, oa, ob, wa, wb, g.reshape(1, d), wg, wu, wd)


TOP_K = 2
MOE_TB = 512
MOE_TM = 512
MOE_RC = 256
KEY_STRIDE = 65536.0


def _router_kernel(x_ref, o_ref, wo_ref, g_ref, wr_hi_ref, wr_lo_ref, x1_ref, h_ref, info_ref,
                   info_t_ref, cnt_ref, carry_scr, ltri_scr):
    i = pl.program_id(0)
    tb = x_ref.shape[0]
    lane = lax.broadcasted_iota(jnp.int32, (1, LANES), 1)

    @pl.when(i == 0)
    def _():
        carry_scr[...] = jnp.zeros_like(carry_scr)
        r = lax.broadcasted_iota(jnp.int32, (tb, tb), 0)
        cidx = lax.broadcasted_iota(jnp.int32, (tb, tb), 1)
        ltri_scr[...] = jnp.where(cidx < r, 1.0, 0.0).astype(BF16)

    x1 = x_ref[...] + jnp.dot(o_ref[...], wo_ref[...], preferred_element_type=F32)
    x1_ref[...] = x1
    h = _rms_rows(x1, g_ref[...])
    h_hi = h.astype(BF16)
    h_ref[...] = h_hi
    h_lo = (h - h_hi.astype(F32)).astype(BF16)
    logits = (jnp.dot(h_hi, wr_hi_ref[...], preferred_element_type=F32)
              + jnp.dot(h_lo, wr_hi_ref[...], preferred_element_type=F32)
              + jnp.dot(h_hi, wr_lo_ref[...], preferred_element_type=F32))
    logits = jnp.where(lane < N_EXPERTS, logits, NEG_INF)
    v1 = jnp.max(logits, axis=-1, keepdims=True)
    i1 = jnp.min(jnp.where(logits == v1, lane, LANES), axis=-1, keepdims=True)
    rest = jnp.where(lane == i1, NEG_INF, logits)
    v2 = jnp.max(rest, axis=-1, keepdims=True)
    i2 = jnp.min(jnp.where(rest == v2, lane, LANES), axis=-1, keepdims=True)
    e2 = jnp.exp(v2 - v1)
    g1 = 1.0 / (1.0 + e2)
    g2 = e2 * g1

    pick1 = lane == i1
    pick2 = lane == i2
    picked = jnp.where(pick1, 1.0, jnp.where(pick2, 1.0, 0.0))
    before = (jnp.dot(ltri_scr[...], picked.astype(BF16), preferred_element_type=F32)
              + carry_scr[...])
    rank1 = jnp.sum(jnp.where(pick1, before, 0.0), axis=-1, keepdims=True)
    rank2 = jnp.sum(jnp.where(pick2, before, 0.0), axis=-1, keepdims=True)
    key1 = i1.astype(F32) * KEY_STRIDE + rank1
    key2 = i2.astype(F32) * KEY_STRIDE + rank2
    info = jnp.where(lane == 0, key1, jnp.where(lane == 1, key2,
                     jnp.where(lane == 2, g1, jnp.where(lane == 3, g2, 0.0))))
    info_ref[...] = info
    info_t_ref[...] = info.T[:8, :]
    counts = jnp.sum(picked, axis=0, keepdims=True)
    cnt_ref[...] = jnp.broadcast_to(counts, cnt_ref.shape)
    carry_scr[...] += counts


def _router(x, o, wo, g, wr):
    t, d = x.shape
    nb = t // MOE_TB
    wr_hi = wr.astype(BF16)
    wr_lo = (wr - wr_hi.astype(F32)).astype(BF16)
    row = lambda w: pl.BlockSpec((MOE_TB, w), lambda i: (i, 0))
    const = lambda a: pl.BlockSpec(a.shape, lambda i: (0, 0))
    return pl.pallas_call(
        _router_kernel,
        grid=(nb,),
        in_specs=[row(d), row(o.shape[1]), const(wo), pl.BlockSpec((1, d), lambda i: (0, 0)),
                  const(wr_hi), const(wr_lo)],
        out_specs=[row(d), row(d), row(LANES),
                   pl.BlockSpec((None, 8, MOE_TB), lambda i: (i, 0, 0)),
                   pl.BlockSpec((None, 8, LANES), lambda i: (i, 0, 0))],
        out_shape=[jax.ShapeDtypeStruct((t, d), F32),
                   jax.ShapeDtypeStruct((t, d), BF16),
                   jax.ShapeDtypeStruct((t, LANES), F32),
                   jax.ShapeDtypeStruct((nb, 8, MOE_TB), F32),
                   jax.ShapeDtypeStruct((nb, 8, LANES), F32)],
        scratch_shapes=[pltpu.VMEM((1, LANES), F32), pltpu.VMEM((MOE_TB, MOE_TB), BF16)],
        compiler_params=pltpu.CompilerParams(
            dimension_semantics=("arbitrary",), vmem_limit_bytes=VMEM_LIMIT),
        name="moe_router",
    )(x, o, wo, g.reshape(1, d), wr_hi, wr_lo)


def _moe_plan(block_counts, n_tokens):
    nb = block_counts.shape[0]
    n_tiles = (TOP_K * n_tokens + N_EXPERTS * (MOE_TM - 1)) // MOE_TM
    n_chunks = n_tiles * MOE_TM // MOE_RC
    n_pairs = ((n_chunks + N_EXPERTS * nb + 7) // 8) * 8
    tc = block_counts.astype(jnp.int32)
    cb = jnp.concatenate([jnp.zeros((1, N_EXPERTS), jnp.int32), jnp.cumsum(tc, axis=0)])
    counts = cb[-1]
    padded = ((counts + MOE_TM - 1) // MOE_TM) * MOE_TM
    ends = jnp.cumsum(padded)
    starts = ends - padded
    total = ends[-1]

    tiles = jnp.arange(n_tiles, dtype=jnp.int32)
    n_valid_tiles = total // MOE_TM
    t_valid = (tiles < n_valid_tiles).astype(jnp.int32)
    t_src = jnp.maximum(jnp.minimum(tiles, n_valid_tiles - 1), 0)
    t_exp = jnp.minimum(jnp.sum((t_src * MOE_TM)[:, None] >= ends[None, :], axis=1),
                        N_EXPERTS - 1).astype(jnp.int32)

    pidx = jnp.arange(n_pairs, dtype=jnp.int32)

    def chunking(rows_per_chunk, n):
        r0 = jnp.arange(n, dtype=jnp.int32) * rows_per_chunk
        ex = jnp.minimum(jnp.sum(r0[:, None] >= ends[None, :], axis=1), N_EXPERTS - 1)
        live = r0 < total
        lo = r0 - starts[ex]
        cbe = cb[:, ex]
        ov = (live[None, :] & (cbe[:-1] < (lo + rows_per_chunk)[None, :])
              & (cbe[1:] > lo[None, :]))
        return (ex * int(KEY_STRIDE) + lo).astype(jnp.int32), live.astype(jnp.int32), ov

    def pair_list(flat, inner):
        n_act = jnp.sum(flat).astype(jnp.int32)
        idx = jnp.nonzero(flat, size=n_pairs, fill_value=0)[0].astype(jnp.int32)
        idx = jnp.where(pidx < n_act, idx, idx[jnp.maximum(n_act - 1, 0)])
        return idx % inner

    def spans(cnt):
        cnt = cnt.astype(jnp.int32)
        return (jnp.cumsum(cnt) - cnt).astype(jnp.int32), cnt

    base, live, ov = chunking(MOE_RC, n_chunks)
    g_start, g_cnt = spans(jnp.sum(ov, axis=0))
    c_start, c_cnt = spans(jnp.sum(ov, axis=1))
    return dict(n_tiles=n_tiles, n_chunks=n_chunks, t_valid=t_valid, t_exp=t_exp,
                base=base, live=live, g_start=g_start, g_cnt=g_cnt,
                g_b=pair_list(ov.T.reshape(-1), nb),
                c_start=c_start, c_cnt=c_cnt, c_c=pair_list(ov.reshape(-1), n_chunks))


def _gather_kernel(live_ref, start_ref, cnt_ref, pb_ref, base_ref, info_t_ref, h_ref,
                   xs_ref, gs_ref, acc_scr, gsum_scr):
    c = pl.program_id(0)
    tb = info_t_ref.shape[-1]

    @pl.when(live_ref[c] == 1)
    def _():
        first = start_ref[c]
        n = cnt_ref[c]
        row = lax.broadcasted_iota(jnp.int32, (MOE_RC, tb), 0).astype(F32)
        acc_scr[...] = jnp.zeros_like(acc_scr)
        gsum_scr[...] = jnp.zeros_like(gsum_scr)

        def two_blocks(it, carry):
            rows, gate = None, None
            for w in range(2):
                j = 2 * it + w
                b = pb_ref[first + jnp.minimum(j, n - 1)]
                base = jnp.where(j < n, base_ref[c], NO_MATCH_BASE).astype(F32)
                info = info_t_ref[b]
                m1 = (info[0:1, :] - base) == row
                m2 = (info[1:2, :] - base) == row
                sel = jnp.where(m1, 1.0, jnp.where(m2, 1.0, 0.0)).astype(BF16)
                g = jnp.sum(jnp.where(m1, info[2:3, :], jnp.where(m2, info[3:4, :], 0.0)),
                            axis=-1, keepdims=True)
                r = jnp.dot(sel, h_ref[pl.ds(pl.multiple_of(b * tb, tb), tb), :],
                            preferred_element_type=F32)
                rows = r if rows is None else rows + r
                gate = g if gate is None else gate + g
            acc_scr[...] += rows
            gsum_scr[...] += gate
            return carry

        lax.fori_loop(0, (n + 1) // 2, two_blocks, 0)
        xs_ref[...] = acc_scr[...].astype(xs_ref.dtype)
        gs_ref[...] = jnp.broadcast_to(gsum_scr[...], gs_ref.shape)

    @pl.when(live_ref[c] == 0)
    def _():
        xs_ref[...] = jnp.zeros_like(xs_ref)
        gs_ref[...] = jnp.zeros_like(gs_ref)


def _moe_gather(plan, info_t, h):
    t, d = h.shape
    rows = plan["n_chunks"] * MOE_RC
    chunk = lambda c, live, st, cn, pb, ba: (c, 0)
    grid_spec = pltpu.PrefetchScalarGridSpec(
        num_scalar_prefetch=5,
        grid=(plan["n_chunks"],),
        in_specs=[pl.BlockSpec(info_t.shape, lambda c, live, st, cn, pb, ba: (0, 0, 0),
                               pipeline_mode=pl.Buffered(1)),
                  pl.BlockSpec((t, d), lambda c, live, st, cn, pb, ba: (0, 0),
                               pipeline_mode=pl.Buffered(1))],
        out_specs=[pl.BlockSpec((MOE_RC, d), chunk), pl.BlockSpec((MOE_RC, LANES), chunk)],
        scratch_shapes=[pltpu.VMEM((MOE_RC, d), F32), pltpu.VMEM((MOE_RC, 1), F32)],
    )
    return pl.pallas_call(
        _gather_kernel,
        grid_spec=grid_spec,
        out_shape=[jax.ShapeDtypeStruct((rows, d), BF16),
                   jax.ShapeDtypeStruct((rows, LANES), F32)],
        compiler_params=pltpu.CompilerParams(
            dimension_semantics=("arbitrary",), vmem_limit_bytes=VMEM_LIMIT),
        name="moe_gather",
    )(plan["live"], plan["g_start"], plan["g_cnt"], plan["g_b"], plan["base"], info_t, h)


def _experts_kernel(exp_ref, val_ref, xs_ref, gs_ref, wg_ref, wu_ref, wd_ref, ys_ref,
                    acc_scr):
    i = pl.program_id(0)
    j = pl.program_id(1)

    last = j == pl.num_programs(1) - 1

    @pl.when(j == 0)
    def _():
        acc_scr[...] = jnp.zeros_like(acc_scr)

    @pl.when(val_ref[i] == 1)
    def _():
        x = xs_ref[...]
        gate = jnp.dot(x, wg_ref[...], preferred_element_type=F32)
        up = jnp.dot(x, wu_ref[...], preferred_element_type=F32)
        a = (_silu(gate) * up).astype(BF16)
        acc_scr[...] += jnp.dot(a, wd_ref[...], preferred_element_type=F32)

    @pl.when(last)
    def _():
        ys_ref[...] = (acc_scr[...] * gs_ref[:, 0:1]).astype(ys_ref.dtype)


def _moe_experts(plan, xs, gs, wg, wu, wd, tf):
    rows, d = xs.shape
    f = wg.shape[2]
    nf = f // tf

    def wcol(i, j, ex, val):
        return (ex[i], 0, jnp.where(val[i] == 1, j, nf - 1))

    def wrow(i, j, ex, val):
        return (ex[i], jnp.where(val[i] == 1, j, nf - 1), 0)

    tile = lambda i, j, ex, val: (i, 0)
    grid_spec = pltpu.PrefetchScalarGridSpec(
        num_scalar_prefetch=2,
        grid=(plan["n_tiles"], nf),
        in_specs=[pl.BlockSpec((MOE_TM, d), tile),
                  pl.BlockSpec((MOE_TM, LANES), tile),
                  pl.BlockSpec((None, d, tf), wcol),
                  pl.BlockSpec((None, d, tf), wcol),
                  pl.BlockSpec((None, tf, d), wrow)],
        out_specs=pl.BlockSpec((MOE_TM, d), tile),
        scratch_shapes=[pltpu.VMEM((MOE_TM, d), F32)],
    )
    return pl.pallas_call(
        _experts_kernel,
        grid_spec=grid_spec,
        out_shape=jax.ShapeDtypeStruct((rows, d), BF16),
        compiler_params=pltpu.CompilerParams(
            dimension_semantics=("arbitrary", "arbitrary"), vmem_limit_bytes=VMEM_LIMIT),
        name="moe_experts",
    )(plan["t_exp"], plan["t_valid"], xs, gs, wg, wu, wd)


NO_MATCH_BASE = -(2 ** 30)


COMBINE_GROUP = 4
COMBINE_MAX_CHUNKS = N_EXPERTS * (MOE_TB // MOE_RC + 1)


def _combine_kernel(start_ref, cnt_ref, pc_ref, base_ref, info_ref, x_ref, ys_hbm, o_ref,
                    ybuf, sem):
    b = pl.program_id(0)
    tb = x_ref.shape[0]
    d = x_ref.shape[1]

    def n_fetch(blk):
        return ((cnt_ref[blk] + COMBINE_GROUP - 1) // COMBINE_GROUP) * COMBINE_GROUP

    def chunk_of(blk, j):
        return pc_ref[start_ref[blk] + jnp.minimum(j, cnt_ref[blk] - 1)]

    def copy(blk, j, slot):
        row0 = pl.multiple_of(chunk_of(blk, j) * MOE_RC, MOE_RC)
        return pltpu.make_async_copy(ys_hbm.at[pl.ds(row0, MOE_RC), :], ybuf.at[slot, j],
                                     sem.at[slot])

    def start_block(blk, slot):
        def issue(j, carry):
            copy(blk, j, slot).start()
            return carry
        lax.fori_loop(0, n_fetch(blk), issue, 0)

    @pl.when(b == 0)
    def _():
        start_block(0, 0)

    @pl.when(b + 1 < pl.num_programs(0))
    def _():
        start_block(b + 1, (b + 1) % 2)

    slot = b % 2
    n = cnt_ref[b]

    def drain(j, carry):
        copy(b, j, slot).wait()
        return carry

    lax.fori_loop(0, n_fetch(b), drain, 0)

    lane = lax.broadcasted_iota(jnp.int32, (1, LANES), 1)
    info = info_ref[...]
    k1 = jnp.sum(jnp.where(lane == 0, info, 0.0), axis=-1, keepdims=True)
    k2 = jnp.sum(jnp.where(lane == 1, info, 0.0), axis=-1, keepdims=True)
    col = lax.broadcasted_iota(jnp.int32, (tb, MOE_RC), 1).astype(F32)
    o_ref[...] = x_ref[...]

    def group(gi, carry):
        sels = []
        for w in range(COMBINE_GROUP):
            j = gi * COMBINE_GROUP + w
            base = jnp.where(j < n, base_ref[chunk_of(b, j)], NO_MATCH_BASE).astype(F32)
            sels.append(jnp.where(k1 - base == col, 1.0,
                                  jnp.where(k2 - base == col, 1.0, 0.0)).astype(BF16))
        rows = ybuf[slot, pl.ds(gi * COMBINE_GROUP, COMBINE_GROUP)]
        o_ref[...] += jnp.dot(jnp.concatenate(sels, axis=1),
                              rows.reshape(COMBINE_GROUP * MOE_RC, d),
                              preferred_element_type=F32)
        return carry

    lax.fori_loop(0, n_fetch(b) // COMBINE_GROUP, group, 0)


def _moe_combine(plan, info, ys, x):
    t, d = x.shape
    block = lambda b, st, cn, pc, ba: (b, 0)
    grid_spec = pltpu.PrefetchScalarGridSpec(
        num_scalar_prefetch=4,
        grid=(t // MOE_TB,),
        in_specs=[pl.BlockSpec((MOE_TB, LANES), block),
                  pl.BlockSpec((MOE_TB, d), block),
                  pl.BlockSpec(memory_space=pl.ANY)],
        out_specs=pl.BlockSpec((MOE_TB, d), block),
        scratch_shapes=[pltpu.VMEM((2, COMBINE_MAX_CHUNKS, MOE_RC, d), BF16),
                        pltpu.SemaphoreType.DMA((2,))],
    )
    return pl.pallas_call(
        _combine_kernel,
        grid_spec=grid_spec,
        out_shape=jax.ShapeDtypeStruct((t, d), F32),
        compiler_params=pltpu.CompilerParams(
            dimension_semantics=("arbitrary",), vmem_limit_bytes=VMEM_LIMIT),
        name="moe_combine",
    )(plan["c_start"], plan["c_cnt"], plan["c_c"], plan["base"], info, x, ys)


def _moe(x, o, wo, g, wr, wg, wu, wd, tf=1792):
    t, _ = x.shape
    x1, h, info, info_t, block_counts = _router(x, o, wo, g, wr)
    plan = _moe_plan(block_counts[:, 0, :N_EXPERTS], t)
    xs, gs = _moe_gather(plan, info_t, h)
    ys = _moe_experts(plan, xs, gs, wg, wu, wd, tf)
    return _moe_combine(plan, info, ys, x1)


def _alibi_slopes():
    s = np.exp2(-8.0 * np.arange(1, N_HEADS + 1) / N_HEADS).astype(np.float32)
    return s[0::2][:A_HEADS], s[1::2][:B_Q_HEADS]


_B_HEAD_ORDER = np.array([0, 4, 1, 5, 2, 6, 3, 7])


def _even_layer(x2d, batch, norm1, w_in, qn_a, kn_a, qn_b, kn_b, sink_b, w_out,
                norm2, w_gate, w_up, w_down):
    t, d = x2d.shape
    s = t // batch
    wa = A_HEADS * HEAD_DIM
    wq_b = B_Q_HEADS * HEAD_DIM
    wkv_b = B_KV_HEADS * HEAD_DIM
    qb0 = 3 * wa
    q_scale = HEAD_DIM ** -0.5

    w_qb = w_in[:, qb0:qb0 + wq_b].reshape(d, B_Q_HEADS, HEAD_DIM)[:, _B_HEAD_ORDER]
    w_in_p = jnp.concatenate(
        [w_in[:, :qb0], w_qb.reshape(d, wq_b), w_in[:, qb0 + wq_b:]], axis=1).astype(BF16)
    ones = jnp.ones((HEAD_DIM,), F32)
    col_gain = jnp.concatenate([
        jnp.tile(qn_a * q_scale, A_HEADS), jnp.tile(kn_a, A_HEADS), jnp.tile(ones, A_HEADS),
        jnp.tile(qn_b * q_scale, B_Q_HEADS), jnp.tile(kn_b, B_KV_HEADS),
        jnp.tile(ones, B_KV_HEADS)])
    cpl = wa // LANES
    norm_chunks = ((True,) * (2 * cpl) + (False,) * cpl + (True,) * (wq_b // LANES)
                   + (True,) * (wkv_b // LANES) + (False,) * (wkv_b // LANES))
    qkv = _norm_proj(x2d, norm1, w_in_p, col_gain, norm_chunks)
    n_cols = qkv.shape[1]
    qkv3 = qkv.reshape(batch, s, n_cols)

    slopes_a, slopes_b = _alibi_slopes()
    par_a = jnp.asarray(slopes_a).reshape(1, A_HEADS)
    radii = {window // (2 * dil) for window, dil in DIL_CONFIGS}
    assert tuple(dil for _, dil in DIL_CONFIGS) == (1, 4, 16) and len(radii) == 1
    oa = _dilated_attention(qkv3, par_a, k_blk0=cpl, v_blk0=2 * cpl, n_pairs=cpl,
                            blk=A_BLOCK, radius=radii.pop())
    par_b = jnp.stack([jnp.asarray(slopes_b[_B_HEAD_ORDER]),
                       sink_b.astype(F32)[_B_HEAD_ORDER]])
    kb = 3 * cpl + wq_b // LANES
    ob = _window_attention(qkv3, par_b, q_blk0=3 * cpl, k_blk=kb, v_blk=kb + wkv_b // LANES,
                           n_pairs=wq_b // LANES, radius=B_WINDOW, blk=B_BLOCK)
    w_out_a = w_out[:wa].astype(BF16)
    w_out_b = w_out[wa:].reshape(B_Q_HEADS, HEAD_DIM, d)[_B_HEAD_ORDER].reshape(wq_b, d)
    return _ffn(x2d, oa.reshape(t, wa), ob.reshape(t, wq_b), w_out_a, w_out_b.astype(BF16),
                norm2, w_gate.astype(BF16), w_up.astype(BF16), w_down.astype(BF16))


def _odd_layer(x2d, batch, norm1, w_qkv, qn, kn, rpb, w_out, norm2, w_router,
               w_gate, w_up, w_down):
    t, d = x2d.shape
    s = t // batch
    wc = C_HEADS * HEAD_DIM
    ones = jnp.ones((HEAD_DIM,), F32)
    col_gain = jnp.concatenate([jnp.tile(qn * HEAD_DIM ** -0.5, C_HEADS),
                                jnp.tile(kn, C_HEADS), jnp.tile(ones, C_HEADS)])
    cpl = wc // LANES
    norm_chunks = (True,) * (2 * cpl) + (False,) * cpl
    qkv = _norm_proj(x2d, norm1, w_qkv.astype(BF16), col_gain, norm_chunks)
    o = _na_attention(qkv.reshape(batch, s, 3 * wc), _na_bias_table(rpb))
    wr = jnp.zeros((d, LANES), F32).at[:, :N_EXPERTS].set(w_router.astype(F32))
    return _moe(x2d, o.reshape(t, wc), w_out.astype(BF16), norm2, wr, w_gate.astype(BF16),
                w_up.astype(BF16), w_down.astype(BF16))


def kernel(x, ev_norm1, ev_w_in, ev_qn_a, ev_kn_a, ev_qn_b, ev_kn_b, ev_sink_b, ev_w_out, ev_norm2, ev_ffn_gate, ev_ffn_up, ev_ffn_down, od_norm1, od_w_qkv, od_qn, od_kn, od_rpb, od_w_out, od_norm2, od_router, od_exp_gate, od_exp_up, od_exp_down):
    batch, s, d = x.shape
    depth = ev_norm1.shape[0] + od_norm1.shape[0]
    h = x.reshape(batch * s, d)
    for layer in range(depth):
        j = layer // 2
        if layer % 2 == 0:
            h = _even_layer(h, batch, ev_norm1[j], ev_w_in[j], ev_qn_a[j], ev_kn_a[j],
                            ev_qn_b[j], ev_kn_b[j], ev_sink_b[j], ev_w_out[j], ev_norm2[j],
                            ev_ffn_gate[j], ev_ffn_up[j], ev_ffn_down[j])
        else:
            h = _odd_layer(h, batch, od_norm1[j], od_w_qkv[j], od_qn[j], od_kn[j], od_rpb[j],
                           od_w_out[j], od_norm2[j], od_router[j], od_exp_gate[j],
                           od_exp_up[j], od_exp_down[j])
    return h.reshape(batch, s, d)
```

```python
import functools

import numpy as np
import jax
import jax.numpy as jnp
from jax import lax
from jax.experimental import pallas as pl
from jax.experimental.pallas import tpu as pltpu

D_MODEL = 1024
HEAD_DIM = 64
N_HEADS = D_MODEL // HEAD_DIM
A_HEADS = N_HEADS // 2
B_Q_HEADS = N_HEADS // 2
B_KV_HEADS = max(1, B_Q_HEADS // 4)
C_HEADS = N_HEADS
DIL_CONFIGS = ((128, 1), (512, 4), (2048, 16))
A_BLOCK = 128
B_WINDOW = 128
B_BLOCK = 128
GRID_W = 64
NA_ROWS = 8
NA_COLS = 16
N_EXPERTS = 8
RMS_EPS = 1e-6
NEG_INF = -1e30

LANES = 128
MXU_N = 256
VMEM_LIMIT = 56 * 1024 * 1024

ATTN_UNROLL = 8

F32 = jnp.float32
BF16 = jnp.bfloat16


def _lo_mask(shape):
    return lax.broadcasted_iota(jnp.int32, shape, len(shape) - 1) < HEAD_DIM


LOG2E = 1.4426950408889634


def _exp2_probs(s, m):
    return jnp.exp2((s - m).astype(BF16))


def _value_and_sum(p, v):
    ov = jnp.dot(p, jnp.concatenate([v, jnp.ones_like(v)], axis=1), preferred_element_type=F32)
    return ov[:, :LANES], ov[:, LANES:LANES + 1]


def _rms_rows(x, g):
    ms = jnp.mean(x * x, axis=-1, keepdims=True)
    return x * lax.rsqrt(ms + RMS_EPS) * g


def _norm_proj_kernel(x_ref, g_ref, w_ref, cg_ref, o_ref, *, norm_chunks):
    h = _rms_rows(x_ref[...], g_ref[...]).astype(BF16)
    n_out = o_ref.shape[-1]
    lo = _lo_mask((1, LANES))
    for j in range(n_out // MXU_N):
        y = jnp.dot(h, w_ref[:, j * MXU_N:(j + 1) * MXU_N], preferred_element_type=F32)
        for half in range(MXU_N // LANES):
            c = j * (MXU_N // LANES) + half
            yc = y[:, half * LANES:(half + 1) * LANES]
            if norm_chunks[c]:
                sq = yc * yc
                s_lo = jnp.sum(jnp.where(lo, sq, 0.0), axis=-1, keepdims=True)
                s_hi = jnp.sum(jnp.where(lo, 0.0, sq), axis=-1, keepdims=True)
                inv = jnp.where(lo, lax.rsqrt(s_lo * (1.0 / HEAD_DIM) + RMS_EPS),
                                lax.rsqrt(s_hi * (1.0 / HEAD_DIM) + RMS_EPS))
                yc = yc * inv * cg_ref[:, c * LANES:(c + 1) * LANES]
            o_ref[:, c * LANES:(c + 1) * LANES] = yc.astype(o_ref.dtype)


def _norm_proj(x, g, w, col_gain, norm_chunks, tm=512):
    t, d = x.shape
    n = w.shape[1]
    return pl.pallas_call(
        functools.partial(_norm_proj_kernel, norm_chunks=norm_chunks),
        grid=(t // tm,),
        in_specs=[
            pl.BlockSpec((tm, d), lambda i: (i, 0)),
            pl.BlockSpec((1, d), lambda i: (0, 0)),
            pl.BlockSpec((d, n), lambda i: (0, 0)),
            pl.BlockSpec((1, n), lambda i: (0, 0)),
        ],
        out_specs=pl.BlockSpec((tm, n), lambda i: (i, 0)),
        out_shape=jax.ShapeDtypeStruct((t, n), BF16),
        compiler_params=pltpu.CompilerParams(
            dimension_semantics=("arbitrary",), vmem_limit_bytes=VMEM_LIMIT),
        name="norm_proj",
    )(x, g.reshape(1, d), w, col_gain.reshape(1, n))


def _window_kernel(par_ref, q_ref, k_ref, v_ref, o_ref, bias_scr, *, seq, blk, radius):
    span = blk + 2 * radius
    c = pl.program_id(1)
    group = ATTN_UNROLL
    row_lo = lax.broadcasted_iota(jnp.int32, (2 * blk, 1), 0) < blk
    slope_col = jnp.where(row_lo, par_ref[0, 2 * c], par_ref[0, 2 * c + 1])
    sink_col = jnp.where(row_lo, par_ref[1, 2 * c], par_ref[1, 2 * c + 1])

    iq = lax.broadcasted_iota(jnp.int32, (2 * blk, span), 0) % blk
    ik = lax.broadcasted_iota(jnp.int32, (2 * blk, span), 1)
    _fill_band_bias(bias_scr, iq, ik, radius, 1, slope_col)
    lo_q = _lo_mask((1, LANES))
    lo = _lo_mask((blk, LANES))

    def body(g, carry):
        starts, variants, values, scores = [], [], [], []
        for u in range(group):
            a = pl.multiple_of((g * group + u) * blk, blk)
            ks = pl.multiple_of(jnp.clip(a - radius, 0, seq - span), radius)
            q = q_ref[pl.ds(a, blk), :]
            zero = jnp.zeros_like(q)
            qm = jnp.concatenate([jnp.where(lo_q, q, zero), jnp.where(lo_q, zero, q)], axis=0)
            k = k_ref[pl.ds(ks, span), :]
            scores.append(lax.dot_general(qm, k, (((1,), (1,)), ((), ())),
                                          preferred_element_type=F32))
            starts.append(a)
            variants.append((a - ks) // radius)
            values.append(v_ref[pl.ds(ks, span), :])
        probs, sink_terms = [], []
        for u in range(group):
            s = scores[u] + bias_scr[variants[u]]
            m = jnp.maximum(jnp.max(s, axis=-1, keepdims=True), sink_col)
            probs.append(_exp2_probs(s, m))
            sink_terms.append(jnp.exp2(sink_col - m))
        for u in range(group):
            o, l = _value_and_sum(probs[u], values[u])
            o = o * (1.0 / (l + sink_terms[u]))
            o_ref[pl.ds(starts[u], blk), :] = jnp.where(lo, o[:blk], o[blk:]).astype(o_ref.dtype)
        return carry

    lax.fori_loop(0, seq // (blk * group), body, 0)


def _window_attention(qkv, params, *, q_blk0, k_blk, v_blk, n_pairs, radius, blk):
    b, s, _ = qkv.shape
    span = blk + 2 * radius
    return pl.pallas_call(
        functools.partial(_window_kernel, seq=s, blk=blk, radius=radius),
        grid=(b, n_pairs),
        in_specs=[
            pl.BlockSpec(memory_space=pltpu.SMEM),
            pl.BlockSpec((None, s, LANES), lambda bi, c: (bi, 0, q_blk0 + c)),
            pl.BlockSpec((None, s, LANES), lambda bi, c: (bi, 0, k_blk)),
            pl.BlockSpec((None, s, LANES), lambda bi, c: (bi, 0, v_blk)),
        ],
        out_specs=pl.BlockSpec((None, s, LANES), lambda bi, c: (bi, 0, c)),
        out_shape=jax.ShapeDtypeStruct((b, s, n_pairs * LANES), BF16),
        scratch_shapes=[pltpu.VMEM((3, 2 * blk, span), F32)],
        compiler_params=pltpu.CompilerParams(
            dimension_semantics=("arbitrary", "arbitrary"), vmem_limit_bytes=VMEM_LIMIT),
        name="window_attn",
    )(params, qkv, qkv, qkv)


PERM_CHUNK = 256
PHASES4 = 4
PIECE = PERM_CHUNK // PHASES4


def _stack_heads(q):
    lo = _lo_mask((1, LANES))
    zero = jnp.zeros_like(q)
    return jnp.concatenate([jnp.where(lo, q, zero), jnp.where(lo, zero, q)], axis=0)


def _merge_heads(x, blk):
    x = jnp.broadcast_to(x, (2 * blk, LANES))
    return jnp.where(_lo_mask((blk, LANES)), x[:blk], x[blk:])


def _attend_group(qs, ks, vs, biases):
    scores = [lax.dot_general(_stack_heads(q), k, (((1,), (1,)), ((), ())),
                              preferred_element_type=F32) for q, k in zip(qs, ks)]
    stats = []
    for s, bias in zip(scores, biases):
        s = s + bias
        m = jnp.max(s, axis=-1, keepdims=True)
        stats.append((_exp2_probs(s, m), m))
    outs = []
    for (p, m), v in zip(stats, vs):
        o, l = _value_and_sum(p, v)
        outs.append((o, m, l))
    return outs


def _fill_band_bias(bias_scr, pos_q, pos_k, radius, pos_scale, slope_col):
    for var in range(3):
        dist = jnp.abs(pos_q + var * radius - pos_k)
        dist_f = (dist * pos_scale).astype(F32)
        bias_scr[var] = jnp.where(dist <= radius, -(slope_col * dist_f) * LOG2E, NEG_INF)


def _dilated_kernel(par_ref, q_ref, k_ref, v_ref, o_ref, qp, kp, vp, acc_p, m_p, l_p,
                    bias1, bias4, bias16, *, seq, blk, radius):
    span = blk + 2 * radius
    n_chunk = seq // PERM_CHUNK
    group = ATTN_UNROLL
    c = pl.program_id(1)
    row_lo = lax.broadcasted_iota(jnp.int32, (2 * blk, 1), 0) < blk
    slope_col = jnp.where(row_lo, par_ref[0, 2 * c], par_ref[0, 2 * c + 1])

    def piece_pos(x):
        return PIECE * (x // PIECE) + PHASES4 * (x % 16) + (x % PIECE) // 16

    iq = lax.broadcasted_iota(jnp.int32, (2 * blk, span), 0) % blk
    ik = lax.broadcasted_iota(jnp.int32, (2 * blk, span), 1)
    _fill_band_bias(bias1, iq, ik, radius, 1, slope_col)
    _fill_band_bias(bias4, piece_pos(iq), piece_pos(ik), radius, 4, slope_col)
    _fill_band_bias(bias16, iq, ik, radius, 16, slope_col)

    rr = lax.broadcasted_iota(jnp.int32, (PERM_CHUNK, PERM_CHUNK), 0)
    cc = lax.broadcasted_iota(jnp.int32, (PERM_CHUNK, PERM_CHUNK), 1)
    to_phase = jnp.where(cc == 16 * (rr % 16) + PHASES4 * ((rr // 16) % 4) + rr // PIECE,
                         1.0, 0.0).astype(BF16)
    to_token = jnp.where(cc == PIECE * (rr % PHASES4) + 16 * ((rr // PHASES4) % 4) + rr // 16,
                         1.0, 0.0).astype(BF16)

    def permute_body(ch, carry):
        r0 = pl.multiple_of(ch * PERM_CHUNK, PERM_CHUNK)
        x = jnp.concatenate([q_ref[pl.ds(r0, PERM_CHUNK), :], k_ref[pl.ds(r0, PERM_CHUNK), :],
                             v_ref[pl.ds(r0, PERM_CHUNK), :]], axis=1)
        y = jnp.dot(to_phase, x, preferred_element_type=F32).astype(BF16)
        for p4 in range(PHASES4):
            rows = slice(PIECE * p4, PIECE * (p4 + 1))
            qp[p4, ch] = y[rows, 0:LANES]
            kp[p4, ch] = y[rows, LANES:2 * LANES]
            vp[p4, ch] = y[rows, 2 * LANES:3 * LANES]
        return carry

    lax.fori_loop(0, n_chunk, permute_body, 0)

    pieces_per_blk = blk // PIECE
    n_blk4 = seq // (PHASES4 * blk)

    def dil4_body(g, carry):
        p4 = g // (n_blk4 // group)
        j0 = (g % (n_blk4 // group)) * group
        qs, ks, vs, bs = [], [], [], []
        for u in range(group):
            pc = pieces_per_blk * (j0 + u)
            cs = jnp.clip(pc - radius // PIECE, 0, n_chunk - span // PIECE)
            qs.append(qp[p4, pl.ds(pc, pieces_per_blk)].reshape(blk, LANES))
            ks.append(kp[p4, pl.ds(cs, span // PIECE)].reshape(span, LANES))
            vs.append(vp[p4, pl.ds(cs, span // PIECE)].reshape(span, LANES))
            bs.append(bias4[(pc - cs) // (radius // PIECE)])
        for u, (o, m, l) in enumerate(_attend_group(qs, ks, vs, bs)):
            dst = (p4, pl.ds(pieces_per_blk * (j0 + u), pieces_per_blk))
            acc_p[dst] = _merge_heads(o, blk).reshape(pieces_per_blk, PIECE, LANES)
            m_p[dst] = _merge_heads(m, blk).reshape(pieces_per_blk, PIECE, LANES)
            l_p[dst] = _merge_heads(l, blk).reshape(pieces_per_blk, PIECE, LANES)
        return carry

    lax.fori_loop(0, PHASES4 * n_blk4 // group, dil4_body, 0)

    chunks_per_blk = blk // 16

    def dil16_body(g, carry):
        p4 = g // 2
        qs, ks, vs, bs, dsts = [], [], [], [], []
        for mm in range(2):
            off = pl.multiple_of((2 * (g % 2) + mm) * 16, 16)
            kk = kp[p4, :, pl.ds(off, 16), :].reshape(span, LANES)
            vv = vp[p4, :, pl.ds(off, 16), :].reshape(span, LANES)
            for jb in range(n_chunk // chunks_per_blk):
                dst = (p4, pl.ds(chunks_per_blk * jb, chunks_per_blk), pl.ds(off, 16), slice(None))
                qs.append(qp[dst].reshape(blk, LANES))
                ks.append(kk)
                vs.append(vv)
                bs.append(bias16[jb * (blk // radius)])
                dsts.append(dst)
        for dst, (o, m, l) in zip(dsts, _attend_group(qs, ks, vs, bs)):
            m_old = m_p[dst].reshape(blk, LANES)
            m_new = jnp.maximum(m_old, _merge_heads(m, blk))
            w_old = jnp.exp2(m_old - m_new)
            w_new = jnp.exp2(_merge_heads(m, blk) - m_new)
            acc = w_old * acc_p[dst].reshape(blk, LANES) + w_new * _merge_heads(o, blk)
            den = w_old * l_p[dst].reshape(blk, LANES) + w_new * _merge_heads(l, blk)
            acc_p[dst] = acc.reshape(chunks_per_blk, 16, LANES)
            l_p[dst] = den.reshape(chunks_per_blk, 16, LANES)
            m_p[dst] = m_new.reshape(chunks_per_blk, 16, LANES)
        return carry

    lax.fori_loop(0, PHASES4 * 2, dil16_body, 0)

    blk_per_chunk = PERM_CHUNK // blk

    def dil1_body(g, carry):
        merged = []
        for cc2 in range(group // blk_per_chunk):
            ch = g * (group // blk_per_chunk) + cc2
            acc = jnp.concatenate([acc_p[p4, ch] for p4 in range(PHASES4)], axis=0)
            den = jnp.concatenate([l_p[p4, ch] for p4 in range(PHASES4)], axis=0)
            mx = jnp.concatenate([m_p[p4, ch] for p4 in range(PHASES4)], axis=0)
            o_ph = (acc * (1.0 / den)).astype(BF16)
            lse = mx + jnp.log2(den)
            hi = lse.astype(BF16)
            rem = lse - hi.astype(F32)
            mid = rem.astype(BF16)
            low = (rem - mid.astype(F32)).astype(BF16)
            back = jnp.dot(to_token, jnp.concatenate([o_ph, hi, mid, low], axis=1),
                           preferred_element_type=F32)
            merged.append((back[:, 0:LANES], back[:, LANES:2 * LANES]
                           + back[:, 2 * LANES:3 * LANES] + back[:, 3 * LANES:4 * LANES]))
        qs, ks, vs, bs, starts = [], [], [], [], []
        for u in range(group):
            a = pl.multiple_of((g * group + u) * blk, blk)
            kst = pl.multiple_of(jnp.clip(a - radius, 0, seq - span), radius)
            qs.append(q_ref[pl.ds(a, blk), :])
            ks.append(k_ref[pl.ds(kst, span), :])
            vs.append(v_ref[pl.ds(kst, span), :])
            bs.append(bias1[(a - kst) // radius])
            starts.append(a)
        for u, (o, m, l) in enumerate(_attend_group(qs, ks, vs, bs)):
            o1 = _merge_heads(o * (1.0 / l), blk)
            lse1 = _merge_heads(m + jnp.log2(l), blk)
            o2, lse2 = merged[u // blk_per_chunk]
            half = slice((u % blk_per_chunk) * blk, (u % blk_per_chunk + 1) * blk)
            o2, lse2 = o2[half], lse2[half]
            top = jnp.maximum(lse1, lse2)
            e1 = jnp.exp2(lse1 - top)
            e2 = jnp.exp2(lse2 - top)
            o_ref[pl.ds(starts[u], blk), :] = ((e1 * o1 + e2 * o2)
                                               * (1.0 / (e1 + e2))).astype(o_ref.dtype)
        return carry

    lax.fori_loop(0, seq // (blk * group), dil1_body, 0)


def _dilated_attention(qkv, params, *, k_blk0, v_blk0, n_pairs, blk, radius):
    b, s, _ = qkv.shape
    span = blk + 2 * radius
    n_chunk = s // PERM_CHUNK
    assert s // 16 == span and s % PERM_CHUNK == 0 and blk % PIECE == 0
    phase = lambda dt: pltpu.VMEM((PHASES4, n_chunk, PIECE, LANES), dt)
    return pl.pallas_call(
        functools.partial(_dilated_kernel, seq=s, blk=blk, radius=radius),
        grid=(b, n_pairs),
        in_specs=[
            pl.BlockSpec(memory_space=pltpu.SMEM),
            pl.BlockSpec((None, s, LANES), lambda bi, c: (bi, 0, c)),
            pl.BlockSpec((None, s, LANES), lambda bi, c: (bi, 0, k_blk0 + c)),
            pl.BlockSpec((None, s, LANES), lambda bi, c: (bi, 0, v_blk0 + c)),
        ],
        out_specs=pl.BlockSpec((None, s, LANES), lambda bi, c: (bi, 0, c)),
        out_shape=jax.ShapeDtypeStruct((b, s, n_pairs * LANES), BF16),
        scratch_shapes=[phase(BF16), phase(BF16), phase(BF16), phase(F32), phase(F32), phase(F32)]
        + [pltpu.VMEM((3, 2 * blk, span), F32)] * 3,
        compiler_params=pltpu.CompilerParams(
            dimension_semantics=("arbitrary", "arbitrary"), vmem_limit_bytes=VMEM_LIMIT),
        name="dilated_attn",
    )(params, qkv, qkv, qkv)


def _na_kernel(q_ref, k_ref, v_ref, bias_ref, o_ref, *, rows):
    kh = NA_ROWS
    group = ATTN_UNROLL
    lo_q = _lo_mask((1, LANES))
    lo = _lo_mask((GRID_W, LANES))

    def body(g, carry):
        starts, offs, values, scores = [], [], [], []
        for u in range(group):
            i = g * group + u
            rs = jnp.clip(i - kh // 2, 0, rows - kh)
            qs = pl.multiple_of(i * GRID_W, GRID_W)
            ks = pl.multiple_of(rs * GRID_W, GRID_W)
            q = q_ref[pl.ds(qs, GRID_W), :]
            zero = jnp.zeros_like(q)
            qm = jnp.concatenate([jnp.where(lo_q, q, zero), jnp.where(lo_q, zero, q)], axis=0)
            k = k_ref[pl.ds(ks, kh * GRID_W), :]
            scores.append(lax.dot_general(qm, k, (((1,), (1,)), ((), ())),
                                          preferred_element_type=F32))
            starts.append(qs)
            offs.append(rs - i + (NA_ROWS - 1))
            values.append(v_ref[pl.ds(ks, kh * GRID_W), :])
        probs = []
        for u in range(group):
            s = scores[u] + jnp.concatenate(
                [bias_ref[offs[u] + 2 * j] for j in range(kh // 2)], axis=1)
            probs.append(_exp2_probs(s, jnp.max(s, axis=-1, keepdims=True)))
        for u in range(group):
            o, l = _value_and_sum(probs[u], values[u])
            o = o * (1.0 / l)
            o_ref[pl.ds(starts[u], GRID_W), :] = jnp.where(
                lo, o[:GRID_W], o[GRID_W:]).astype(o_ref.dtype)
        return carry

    lax.fori_loop(0, rows // group, body, 0)


def _na_bias_table(rpb):
    qc = np.arange(GRID_W)[:, None]
    kc = np.arange(GRID_W)[None, :]
    qstart = np.clip(qc - NA_COLS // 2, 0, GRID_W - NA_COLS)
    valid = (kc >= qstart) & (kc < qstart + NA_COLS)
    coff = np.clip(kc - qc, -(NA_COLS - 1), NA_COLS - 1) + NA_COLS - 1
    pick = (coff[:, :, None] == np.arange(2 * NA_COLS - 1)).astype(np.float32)
    t = jnp.einsum("hdj,qkj->hdqk", rpb.astype(F32), jnp.asarray(pick),
                   precision=lax.Precision.HIGHEST)
    t = jnp.where(jnp.asarray(valid), t * LOG2E, NEG_INF)
    t = jnp.concatenate([t[:, :-1], t[:, 1:]], axis=-1)
    n_off = 2 * NA_ROWS - 2
    t = t.reshape(C_HEADS // 2, 2, n_off, GRID_W, 2 * GRID_W).transpose(0, 2, 1, 3, 4)
    return t.reshape(C_HEADS // 2, n_off, 2 * GRID_W, 2 * GRID_W)


def _na_attention(qkv, bias):
    b, s, n_cols = qkv.shape
    n_pairs = C_HEADS // 2
    rows = s // GRID_W
    return pl.pallas_call(
        functools.partial(_na_kernel, rows=rows),
        grid=(b, n_pairs),
        in_specs=[
            pl.BlockSpec((None, s, LANES), lambda bi, c: (bi, 0, c)),
            pl.BlockSpec((None, s, LANES), lambda bi, c: (bi, 0, n_pairs + c)),
            pl.BlockSpec((None, s, LANES), lambda bi, c: (bi, 0, 2 * n_pairs + c)),
            pl.BlockSpec((None,) + bias.shape[1:], lambda bi, c: (c, 0, 0, 0)),
        ],
        out_specs=pl.BlockSpec((None, s, LANES), lambda bi, c: (bi, 0, c)),
        out_shape=jax.ShapeDtypeStruct((b, s, n_pairs * LANES), BF16),
        compiler_params=pltpu.CompilerParams(
            dimension_semantics=("arbitrary", "arbitrary"), vmem_limit_bytes=VMEM_LIMIT),
        name="na_attn",
    )(qkv, qkv, qkv, bias)


def _silu(g):
    return g * (1.0 / (1.0 + jnp.exp(-g)))


def _ffn_kernel(x_ref, oa_ref, ob_ref, wa_ref, wb_ref, g_ref, wg_ref, wu_ref, wd_ref, y_ref,
                a_scr):
    x = (x_ref[...] + jnp.dot(oa_ref[...], wa_ref[...], preferred_element_type=F32)
         + jnp.dot(ob_ref[...], wb_ref[...], preferred_element_type=F32))
    h = _rms_rows(x, g_ref[...]).astype(BF16)
    f = wg_ref.shape[1]
    for j in range(f // MXU_N):
        sl = slice(j * MXU_N, (j + 1) * MXU_N)
        gate = jnp.dot(h, wg_ref[:, sl], preferred_element_type=F32)
        up = jnp.dot(h, wu_ref[:, sl], preferred_element_type=F32)
        a_scr[:, sl] = (_silu(gate) * up).astype(BF16)
    y_ref[...] = x + jnp.dot(a_scr[...], wd_ref[...], preferred_element_type=F32)


def _ffn(x, oa, ob, wa, wb, g, wg, wu, wd, tm=512):
    t, d = x.shape
    f = wg.shape[1]
    const = lambda a: pl.BlockSpec(a.shape, lambda i: (0, 0), pipeline_mode=pl.Buffered(1))
    row = lambda w: pl.BlockSpec((tm, w), lambda i: (i, 0))
    return pl.pallas_call(
        _ffn_kernel,
        grid=(t // tm,),
        in_specs=[row(d), row(oa.shape[1]), row(ob.shape[1]), const(wa), const(wb),
                  pl.BlockSpec((1, d), lambda i: (0, 0)),
                  const(wg), const(wu), const(wd)],
        out_specs=pl.BlockSpec((tm, d), lambda i: (i, 0)),
        out_shape=jax.ShapeDtypeStruct((t, d), F32),
        scratch_shapes=[pltpu.VMEM((tm, f), BF16)],
        compiler_params=pltpu.CompilerParams(
            dimension_semantics=("arbitrary",), vmem_limit_bytes=VMEM_LIMIT),
        name="dense_swiglu",
    )(x, oa, ob, wa, wb, g.reshape(1, d), wg, wu, wd)


TOP_K = 2
MOE_TB = 512
MOE_TM = 512
MOE_RC = 256
KEY_STRIDE = 65536.0


def _router_kernel(x_ref, o_ref, wo_ref, g_ref, wr_hi_ref, wr_lo_ref, x1_ref, h_ref, info_ref,
                   info_t_ref, cnt_ref, carry_scr, ltri_scr):
    i = pl.program_id(0)
    tb = x_ref.shape[0]
    lane = lax.broadcasted_iota(jnp.int32, (1, LANES), 1)

    @pl.when(i == 0)
    def _():
        carry_scr[...] = jnp.zeros_like(carry_scr)
        r = lax.broadcasted_iota(jnp.int32, (tb, tb), 0)
        cidx = lax.broadcasted_iota(jnp.int32, (tb, tb), 1)
        ltri_scr[...] = jnp.where(cidx < r, 1.0, 0.0).astype(BF16)

    x1 = x_ref[...] + jnp.dot(o_ref[...], wo_ref[...], preferred_element_type=F32)
    x1_ref[...] = x1
    h = _rms_rows(x1, g_ref[...])
    h_hi = h.astype(BF16)
    h_ref[...] = h_hi
    h_lo = (h - h_hi.astype(F32)).astype(BF16)
    logits = (jnp.dot(h_hi, wr_hi_ref[...], preferred_element_type=F32)
              + jnp.dot(h_lo, wr_hi_ref[...], preferred_element_type=F32)
              + jnp.dot(h_hi, wr_lo_ref[...], preferred_element_type=F32))
    logits = jnp.where(lane < N_EXPERTS, logits, NEG_INF)
    v1 = jnp.max(logits, axis=-1, keepdims=True)
    i1 = jnp.min(jnp.where(logits == v1, lane, LANES), axis=-1, keepdims=True)
    rest = jnp.where(lane == i1, NEG_INF, logits)
    v2 = jnp.max(rest, axis=-1, keepdims=True)
    i2 = jnp.min(jnp.where(rest == v2, lane, LANES), axis=-1, keepdims=True)
    e2 = jnp.exp(v2 - v1)
    g1 = 1.0 / (1.0 + e2)
    g2 = e2 * g1

    pick1 = lane == i1
    pick2 = lane == i2
    picked = jnp.where(pick1, 1.0, jnp.where(pick2, 1.0, 0.0))
    before = (jnp.dot(ltri_scr[...], picked.astype(BF16), preferred_element_type=F32)
              + carry_scr[...])
    rank1 = jnp.sum(jnp.where(pick1, before, 0.0), axis=-1, keepdims=True)
    rank2 = jnp.sum(jnp.where(pick2, before, 0.0), axis=-1, keepdims=True)
    key1 = i1.astype(F32) * KEY_STRIDE + rank1
    key2 = i2.astype(F32) * KEY_STRIDE + rank2
    info = jnp.where(lane == 0, key1, jnp.where(lane == 1, key2,
                     jnp.where(lane == 2, g1, jnp.where(lane == 3, g2, 0.0))))
    info_ref[...] = info
    info_t_ref[...] = info.T[:8, :]
    counts = jnp.sum(picked, axis=0, keepdims=True)
    cnt_ref[...] = jnp.broadcast_to(counts, cnt_ref.shape)
    carry_scr[...] += counts


def _router(x, o, wo, g, wr):
    t, d = x.shape
    nb = t // MOE_TB
    wr_hi = wr.astype(BF16)
    wr_lo = (wr - wr_hi.astype(F32)).astype(BF16)
    row = lambda w: pl.BlockSpec((MOE_TB, w), lambda i: (i, 0))
    const = lambda a: pl.BlockSpec(a.shape, lambda i: (0, 0))
    return pl.pallas_call(
        _router_kernel,
        grid=(nb,),
        in_specs=[row(d), row(o.shape[1]), const(wo), pl.BlockSpec((1, d), lambda i: (0, 0)),
                  const(wr_hi), const(wr_lo)],
        out_specs=[row(d), row(d), row(LANES),
                   pl.BlockSpec((None, 8, MOE_TB), lambda i: (i, 0, 0)),
                   pl.BlockSpec((None, 8, LANES), lambda i: (i, 0, 0))],
        out_shape=[jax.ShapeDtypeStruct((t, d), F32),
                   jax.ShapeDtypeStruct((t, d), BF16),
                   jax.ShapeDtypeStruct((t, LANES), F32),
                   jax.ShapeDtypeStruct((nb, 8, MOE_TB), F32),
                   jax.ShapeDtypeStruct((nb, 8, LANES), F32)],
        scratch_shapes=[pltpu.VMEM((1, LANES), F32), pltpu.VMEM((MOE_TB, MOE_TB), BF16)],
        compiler_params=pltpu.CompilerParams(
            dimension_semantics=("arbitrary",), vmem_limit_bytes=VMEM_LIMIT),
        name="moe_router",
    )(x, o, wo, g.reshape(1, d), wr_hi, wr_lo)


def _moe_plan(block_counts, n_tokens):
    nb = block_counts.shape[0]
    n_tiles = (TOP_K * n_tokens + N_EXPERTS * (MOE_TM - 1)) // MOE_TM
    n_chunks = n_tiles * MOE_TM // MOE_RC
    n_pairs = ((n_chunks + N_EXPERTS * nb + 7) // 8) * 8
    tc = block_counts.astype(jnp.int32)
    cb = jnp.concatenate([jnp.zeros((1, N_EXPERTS), jnp.int32), jnp.cumsum(tc, axis=0)])
    counts = cb[-1]
    padded = ((counts + MOE_TM - 1) // MOE_TM) * MOE_TM
    ends = jnp.cumsum(padded)
    starts = ends - padded
    total = ends[-1]

    tiles = jnp.arange(n_tiles, dtype=jnp.int32)
    n_valid_tiles = total // MOE_TM
    t_valid = (tiles < n_valid_tiles).astype(jnp.int32)
    t_src = jnp.maximum(jnp.minimum(tiles, n_valid_tiles - 1), 0)
    t_exp = jnp.minimum(jnp.sum((t_src * MOE_TM)[:, None] >= ends[None, :], axis=1),
                        N_EXPERTS - 1).astype(jnp.int32)

    pidx = jnp.arange(n_pairs, dtype=jnp.int32)

    def chunking(rows_per_chunk, n):
        r0 = jnp.arange(n, dtype=jnp.int32) * rows_per_chunk
        ex = jnp.minimum(jnp.sum(r0[:, None] >= ends[None, :], axis=1), N_EXPERTS - 1)
        live = r0 < total
        lo = r0 - starts[ex]
        cbe = cb[:, ex]
        ov = (live[None, :] & (cbe[:-1] < (lo + rows_per_chunk)[None, :])
              & (cbe[1:] > lo[None, :]))
        return (ex * int(KEY_STRIDE) + lo).astype(jnp.int32), live.astype(jnp.int32), ov

    def pair_list(flat, inner):
        n_act = jnp.sum(flat).astype(jnp.int32)
        idx = jnp.nonzero(flat, size=n_pairs, fill_value=0)[0].astype(jnp.int32)
        idx = jnp.where(pidx < n_act, idx, idx[jnp.maximum(n_act - 1, 0)])
        return idx % inner

    def spans(cnt):
        cnt = cnt.astype(jnp.int32)
        return (jnp.cumsum(cnt) - cnt).astype(jnp.int32), cnt

    base, live, ov = chunking(MOE_RC, n_chunks)
    g_start, g_cnt = spans(jnp.sum(ov, axis=0))
    c_start, c_cnt = spans(jnp.sum(ov, axis=1))
    return dict(n_tiles=n_tiles, n_chunks=n_chunks, t_valid=t_valid, t_exp=t_exp,
                base=base, live=live, g_start=g_start, g_cnt=g_cnt,
                g_b=pair_list(ov.T.reshape(-1), nb),
                c_start=c_start, c_cnt=c_cnt, c_c=pair_list(ov.reshape(-1), n_chunks))


def _gather_kernel(live_ref, start_ref, cnt_ref, pb_ref, base_ref, info_t_ref, h_ref,
                   xs_ref, gs_ref, acc_scr, gsum_scr):
    c = pl.program_id(0)
    tb = info_t_ref.shape[-1]

    @pl.when(live_ref[c] == 1)
    def _():
        first = start_ref[c]
        n = cnt_ref[c]
        row = lax.broadcasted_iota(jnp.int32, (MOE_RC, tb), 0).astype(F32)
        acc_scr[...] = jnp.zeros_like(acc_scr)
        gsum_scr[...] = jnp.zeros_like(gsum_scr)

        def two_blocks(it, carry):
            rows, gate = None, None
            for w in range(2):
                j = 2 * it + w
                b = pb_ref[first + jnp.minimum(j, n - 1)]
                base = jnp.where(j < n, base_ref[c], NO_MATCH_BASE).astype(F32)
                info = info_t_ref[b]
                m1 = (info[0:1, :] - base) == row
                m2 = (info[1:2, :] - base) == row
                sel = jnp.where(m1, 1.0, jnp.where(m2, 1.0, 0.0)).astype(BF16)
                g = jnp.sum(jnp.where(m1, info[2:3, :], jnp.where(m2, info[3:4, :], 0.0)),
                            axis=-1, keepdims=True)
                r = jnp.dot(sel, h_ref[pl.ds(pl.multiple_of(b * tb, tb), tb), :],
                            preferred_element_type=F32)
                rows = r if rows is None else rows + r
                gate = g if gate is None else gate + g
            acc_scr[...] += rows
            gsum_scr[...] += gate
            return carry

        lax.fori_loop(0, (n + 1) // 2, two_blocks, 0)
        xs_ref[...] = acc_scr[...].astype(xs_ref.dtype)
        gs_ref[...] = jnp.broadcast_to(gsum_scr[...], gs_ref.shape)

    @pl.when(live_ref[c] == 0)
    def _():
        xs_ref[...] = jnp.zeros_like(xs_ref)
        gs_ref[...] = jnp.zeros_like(gs_ref)


def _moe_gather(plan, info_t, h):
    t, d = h.shape
    rows = plan["n_chunks"] * MOE_RC
    chunk = lambda c, live, st, cn, pb, ba: (c, 0)
    grid_spec = pltpu.PrefetchScalarGridSpec(
        num_scalar_prefetch=5,
        grid=(plan["n_chunks"],),
        in_specs=[pl.BlockSpec(info_t.shape, lambda c, live, st, cn, pb, ba: (0, 0, 0),
                               pipeline_mode=pl.Buffered(1)),
                  pl.BlockSpec((t, d), lambda c, live, st, cn, pb, ba: (0, 0),
                               pipeline_mode=pl.Buffered(1))],
        out_specs=[pl.BlockSpec((MOE_RC, d), chunk), pl.BlockSpec((MOE_RC, LANES), chunk)],
        scratch_shapes=[pltpu.VMEM((MOE_RC, d), F32), pltpu.VMEM((MOE_RC, 1), F32)],
    )
    return pl.pallas_call(
        _gather_kernel,
        grid_spec=grid_spec,
        out_shape=[jax.ShapeDtypeStruct((rows, d), BF16),
                   jax.ShapeDtypeStruct((rows, LANES), F32)],
        compiler_params=pltpu.CompilerParams(
            dimension_semantics=("arbitrary",), vmem_limit_bytes=VMEM_LIMIT),
        name="moe_gather",
    )(plan["live"], plan["g_start"], plan["g_cnt"], plan["g_b"], plan["base"], info_t, h)


def _experts_kernel(exp_ref, val_ref, xs_ref, gs_ref, wg_ref, wu_ref, wd_ref, ys_ref,
                    acc_scr):
    i = pl.program_id(0)
    j = pl.program_id(1)

    last = j == pl.num_programs(1) - 1

    @pl.when(j == 0)
    def _():
        acc_scr[...] = jnp.zeros_like(acc_scr)

    @pl.when(val_ref[i] == 1)
    def _():
        x = xs_ref[...]
        gate = jnp.dot(x, wg_ref[...], preferred_element_type=F32)
        up = jnp.dot(x, wu_ref[...], preferred_element_type=F32)
        a = (_silu(gate) * up).astype(BF16)
        acc_scr[...] += jnp.dot(a, wd_ref[...], preferred_element_type=F32)

    @pl.when(last)
    def _():
        ys_ref[...] = (acc_scr[...] * gs_ref[:, 0:1]).astype(ys_ref.dtype)


def _moe_experts(plan, xs, gs, wg, wu, wd, tf):
    rows, d = xs.shape
    f = wg.shape[2]
    nf = f // tf

    def wcol(i, j, ex, val):
        return (ex[i], 0, jnp.where(val[i] == 1, j, nf - 1))

    def wrow(i, j, ex, val):
        return (ex[i], jnp.where(val[i] == 1, j, nf - 1), 0)

    tile = lambda i, j, ex, val: (i, 0)
    grid_spec = pltpu.PrefetchScalarGridSpec(
        num_scalar_prefetch=2,
        grid=(plan["n_tiles"], nf),
        in_specs=[pl.BlockSpec((MOE_TM, d), tile),
                  pl.BlockSpec((MOE_TM, LANES), tile),
                  pl.BlockSpec((None, d, tf), wcol),
                  pl.BlockSpec((None, d, tf), wcol),
                  pl.BlockSpec((None, tf, d), wrow)],
        out_specs=pl.BlockSpec((MOE_TM, d), tile),
        scratch_shapes=[pltpu.VMEM((MOE_TM, d), F32)],
    )
    return pl.pallas_call(
        _experts_kernel,
        grid_spec=grid_spec,
        out_shape=jax.ShapeDtypeStruct((rows, d), BF16),
        compiler_params=pltpu.CompilerParams(
            dimension_semantics=("arbitrary", "arbitrary"), vmem_limit_bytes=VMEM_LIMIT),
        name="moe_experts",
    )(plan["t_exp"], plan["t_valid"], xs, gs, wg, wu, wd)


NO_MATCH_BASE = -(2 ** 30)


COMBINE_GROUP = 4
COMBINE_MAX_CHUNKS = N_EXPERTS * (MOE_TB // MOE_RC + 1)


def _combine_kernel(start_ref, cnt_ref, pc_ref, base_ref, info_ref, x_ref, ys_hbm, o_ref,
                    ybuf, sem):
    b = pl.program_id(0)
    tb = x_ref.shape[0]
    d = x_ref.shape[1]

    def n_fetch(blk):
        return ((cnt_ref[blk] + COMBINE_GROUP - 1) // COMBINE_GROUP) * COMBINE_GROUP

    def chunk_of(blk, j):
        return pc_ref[start_ref[blk] + jnp.minimum(j, cnt_ref[blk] - 1)]

    def copy(blk, j, slot):
        row0 = pl.multiple_of(chunk_of(blk, j) * MOE_RC, MOE_RC)
        return pltpu.make_async_copy(ys_hbm.at[pl.ds(row0, MOE_RC), :], ybuf.at[slot, j],
                                     sem.at[slot])

    def start_block(blk, slot):
        def issue(j, carry):
            copy(blk, j, slot).start()
            return carry
        lax.fori_loop(0, n_fetch(blk), issue, 0)

    @pl.when(b == 0)
    def _():
        start_block(0, 0)

    @pl.when(b + 1 < pl.num_programs(0))
    def _():
        start_block(b + 1, (b + 1) % 2)

    slot = b % 2
    n = cnt_ref[b]

    def drain(j, carry):
        copy(b, j, slot).wait()
        return carry

    lax.fori_loop(0, n_fetch(b), drain, 0)

    lane = lax.broadcasted_iota(jnp.int32, (1, LANES), 1)
    info = info_ref[...]
    k1 = jnp.sum(jnp.where(lane == 0, info, 0.0), axis=-1, keepdims=True)
    k2 = jnp.sum(jnp.where(lane == 1, info, 0.0), axis=-1, keepdims=True)
    col = lax.broadcasted_iota(jnp.int32, (tb, MOE_RC), 1).astype(F32)
    o_ref[...] = x_ref[...]

    def group(gi, carry):
        sels = []
        for w in range(COMBINE_GROUP):
            j = gi * COMBINE_GROUP + w
            base = jnp.where(j < n, base_ref[chunk_of(b, j)], NO_MATCH_BASE).astype(F32)
            sels.append(jnp.where(k1 - base == col, 1.0,
                                  jnp.where(k2 - base == col, 1.0, 0.0)).astype(BF16))
        rows = ybuf[slot, pl.ds(gi * COMBINE_GROUP, COMBINE_GROUP)]
        o_ref[...] += jnp.dot(jnp.concatenate(sels, axis=1),
                              rows.reshape(COMBINE_GROUP * MOE_RC, d),
                              preferred_element_type=F32)
        return carry

    lax.fori_loop(0, n_fetch(b) // COMBINE_GROUP, group, 0)


def _moe_combine(plan, info, ys, x):
    t, d = x.shape
    block = lambda b, st, cn, pc, ba: (b, 0)
    grid_spec = pltpu.PrefetchScalarGridSpec(
        num_scalar_prefetch=4,
        grid=(t // MOE_TB,),
        in_specs=[pl.BlockSpec((MOE_TB, LANES), block),
                  pl.BlockSpec((MOE_TB, d), block),
                  pl.BlockSpec(memory_space=pl.ANY)],
        out_specs=pl.BlockSpec((MOE_TB, d), block),
        scratch_shapes=[pltpu.VMEM((2, COMBINE_MAX_CHUNKS, MOE_RC, d), BF16),
                        pltpu.SemaphoreType.DMA((2,))],
    )
    return pl.pallas_call(
        _combine_kernel,
        grid_spec=grid_spec,
        out_shape=jax.ShapeDtypeStruct((t, d), F32),
        compiler_params=pltpu.CompilerParams(
            dimension_semantics=("arbitrary",), vmem_limit_bytes=VMEM_LIMIT),
        name="moe_combine",
    )(plan["c_start"], plan["c_cnt"], plan["c_c"], plan["base"], info, x, ys)


def _moe(x, o, wo, g, wr, wg, wu, wd, tf=1792):
    t, _ = x.shape
    x1, h, info, info_t, block_counts = _router(x, o, wo, g, wr)
    plan = _moe_plan(block_counts[:, 0, :N_EXPERTS], t)
    xs, gs = _moe_gather(plan, info_t, h)
    ys = _moe_experts(plan, xs, gs, wg, wu, wd, tf)
    return _moe_combine(plan, info, ys, x1)


def _alibi_slopes():
    s = np.exp2(-8.0 * np.arange(1, N_HEADS + 1) / N_HEADS).astype(np.float32)
    return s[0::2][:A_HEADS], s[1::2][:B_Q_HEADS]


_B_HEAD_ORDER = np.array([0, 4, 1, 5, 2, 6, 3, 7])


def _even_layer(x2d, batch, norm1, w_in, qn_a, kn_a, qn_b, kn_b, sink_b, w_out,
                norm2, w_gate, w_up, w_down):
    t, d = x2d.shape
    s = t // batch
    wa = A_HEADS * HEAD_DIM
    wq_b = B_Q_HEADS * HEAD_DIM
    wkv_b = B_KV_HEADS * HEAD_DIM
    qb0 = 3 * wa
    q_scale = HEAD_DIM ** -0.5 * LOG2E

    w_qb = w_in[:, qb0:qb0 + wq_b].reshape(d, B_Q_HEADS, HEAD_DIM)[:, _B_HEAD_ORDER]
    w_in_p = jnp.concatenate(
        [w_in[:, :qb0], w_qb.reshape(d, wq_b), w_in[:, qb0 + wq_b:]], axis=1).astype(BF16)
    ones = jnp.ones((HEAD_DIM,), F32)
    col_gain = jnp.concatenate([
        jnp.tile(qn_a * q_scale, A_HEADS), jnp.tile(kn_a, A_HEADS), jnp.tile(ones, A_HEADS),
        jnp.tile(qn_b * q_scale, B_Q_HEADS), jnp.tile(kn_b, B_KV_HEADS),
        jnp.tile(ones, B_KV_HEADS)])
    cpl = wa // LANES
    norm_chunks = ((True,) * (2 * cpl) + (False,) * cpl + (True,) * (wq_b // LANES)
                   + (True,) * (wkv_b // LANES) + (False,) * (wkv_b // LANES))
    qkv = _norm_proj(x2d, norm1, w_in_p, col_gain, norm_chunks)
    n_cols = qkv.shape[1]
    qkv3 = qkv.reshape(batch, s, n_cols)

    slopes_a, slopes_b = _alibi_slopes()
    par_a = jnp.asarray(slopes_a).reshape(1, A_HEADS)
    radii = {window // (2 * dil) for window, dil in DIL_CONFIGS}
    assert tuple(dil for _, dil in DIL_CONFIGS) == (1, 4, 16) and len(radii) == 1
    oa = _dilated_attention(qkv3, par_a, k_blk0=cpl, v_blk0=2 * cpl, n_pairs=cpl,
                            blk=A_BLOCK, radius=radii.pop())
    par_b = jnp.stack([jnp.asarray(slopes_b[_B_HEAD_ORDER]),
                       sink_b.astype(F32)[_B_HEAD_ORDER] * LOG2E])
    kb = 3 * cpl + wq_b // LANES
    ob = _window_attention(qkv3, par_b, q_blk0=3 * cpl, k_blk=kb, v_blk=kb + wkv_b // LANES,
                           n_pairs=wq_b // LANES, radius=B_WINDOW, blk=B_BLOCK)
    w_out_a = w_out[:wa].astype(BF16)
    w_out_b = w_out[wa:].reshape(B_Q_HEADS, HEAD_DIM, d)[_B_HEAD_ORDER].reshape(wq_b, d)
    return _ffn(x2d, oa.reshape(t, wa), ob.reshape(t, wq_b), w_out_a, w_out_b.astype(BF16),
                norm2, w_gate.astype(BF16), w_up.astype(BF16), w_down.astype(BF16))


def _odd_layer(x2d, batch, norm1, w_qkv, qn, kn, rpb, w_out, norm2, w_router,
               w_gate, w_up, w_down):
    t, d = x2d.shape
    s = t // batch
    wc = C_HEADS * HEAD_DIM
    ones = jnp.ones((HEAD_DIM,), F32)
    col_gain = jnp.concatenate([jnp.tile(qn * (HEAD_DIM ** -0.5 * LOG2E), C_HEADS),
                                jnp.tile(kn, C_HEADS), jnp.tile(ones, C_HEADS)])
    cpl = wc // LANES
    norm_chunks = (True,) * (2 * cpl) + (False,) * cpl
    qkv = _norm_proj(x2d, norm1, w_qkv.astype(BF16), col_gain, norm_chunks)
    o = _na_attention(qkv.reshape(batch, s, 3 * wc), _na_bias_table(rpb))
    wr = jnp.zeros((d, LANES), F32).at[:, :N_EXPERTS].set(w_router.astype(F32))
    return _moe(x2d, o.reshape(t, wc), w_out.astype(BF16), norm2, wr, w_gate.astype(BF16),
                w_up.astype(BF16), w_down.astype(BF16))


def kernel(x, ev_norm1, ev_w_in, ev_qn_a, ev_kn_a, ev_qn_b, ev_kn_b, ev_sink_b, ev_w_out, ev_norm2, ev_ffn_gate, ev_ffn_up, ev_ffn_down, od_norm1, od_w_qkv, od_qn, od_kn, od_rpb, od_w_out, od_norm2, od_router, od_exp_gate, od_exp_up, od_exp_down):
    batch, s, d = x.shape
    depth = ev_norm1.shape[0] + od_norm1.shape[0]
    h = x.reshape(batch * s, d)
    for layer in range(depth):
        j = layer // 2
        if layer % 2 == 0:
            h = _even_layer(h, batch, ev_norm1[j], ev_w_in[j], ev_qn_a[j], ev_kn_a[j],
                            ev_qn_b[j], ev_kn_b[j], ev_sink_b[j], ev_w_out[j], ev_norm2[j],
                            ev_ffn_gate[j], ev_ffn_up[j], ev_ffn_down[j])
        else:
            h = _odd_layer(h, batch, od_norm1[j], od_w_qkv[j], od_qn[j], od_kn[j], od_rpb[j],
                           od_w_out[j], od_norm2[j], od_router[j], od_exp_gate[j],
                           od_exp_up[j], od_exp_down[j])
    return h.reshape(batch, s, d)
```

```python
import functools

import numpy as np
import jax
import jax.numpy as jnp
from jax import lax
from jax.experimental import pallas as pl
from jax.experimental.pallas import tpu as pltpu

D_MODEL = 1024
HEAD_DIM = 64
N_HEADS = D_MODEL // HEAD_DIM
A_HEADS = N_HEADS // 2
B_Q_HEADS = N_HEADS // 2
B_KV_HEADS = max(1, B_Q_HEADS // 4)
C_HEADS = N_HEADS
DIL_CONFIGS = ((128, 1), (512, 4), (2048, 16))
A_BLOCK = 128
B_WINDOW = 128
B_BLOCK = 128
GRID_W = 64
NA_ROWS = 8
NA_COLS = 16
N_EXPERTS = 8
RMS_EPS = 1e-6
NEG_INF = -1e30

LANES = 128
MXU_N = 256
VMEM_LIMIT = 56 * 1024 * 1024

ATTN_UNROLL = 8

F32 = jnp.float32
BF16 = jnp.bfloat16


def _lo_mask(shape):
    return lax.broadcasted_iota(jnp.int32, shape, len(shape) - 1) < HEAD_DIM


LOG2E = 1.4426950408889634


def _exp2_probs(s, m):
    return jnp.exp2(s - m).astype(BF16)


def _value_and_sum(p, v):
    ov = jnp.dot(p, jnp.concatenate([v, jnp.ones_like(v)], axis=1), preferred_element_type=F32)
    return ov[:, :LANES], ov[:, LANES:LANES + 1]


def _rms_rows(x, g):
    ms = jnp.mean(x * x, axis=-1, keepdims=True)
    return x * lax.rsqrt(ms + RMS_EPS) * g


def _norm_proj_kernel(x_ref, g_ref, w_ref, cg_ref, o_ref, *, norm_chunks):
    h = _rms_rows(x_ref[...], g_ref[...]).astype(BF16)
    n_out = o_ref.shape[-1]
    lo = _lo_mask((1, LANES))
    for j in range(n_out // MXU_N):
        y = jnp.dot(h, w_ref[:, j * MXU_N:(j + 1) * MXU_N], preferred_element_type=F32)
        for half in range(MXU_N // LANES):
            c = j * (MXU_N // LANES) + half
            yc = y[:, half * LANES:(half + 1) * LANES]
            if norm_chunks[c]:
                sq = yc * yc
                s_lo = jnp.sum(jnp.where(lo, sq, 0.0), axis=-1, keepdims=True)
                s_hi = jnp.sum(jnp.where(lo, 0.0, sq), axis=-1, keepdims=True)
                inv = jnp.where(lo, lax.rsqrt(s_lo * (1.0 / HEAD_DIM) + RMS_EPS),
                                lax.rsqrt(s_hi * (1.0 / HEAD_DIM) + RMS_EPS))
                yc = yc * inv * cg_ref[:, c * LANES:(c + 1) * LANES]
            o_ref[:, c * LANES:(c + 1) * LANES] = yc.astype(o_ref.dtype)


def _norm_proj(x, g, w, col_gain, norm_chunks, tm=512):
    t, d = x.shape
    n = w.shape[1]
    return pl.pallas_call(
        functools.partial(_norm_proj_kernel, norm_chunks=norm_chunks),
        grid=(t // tm,),
        in_specs=[
            pl.BlockSpec((tm, d), lambda i: (i, 0)),
            pl.BlockSpec((1, d), lambda i: (0, 0)),
            pl.BlockSpec((d, n), lambda i: (0, 0)),
            pl.BlockSpec((1, n), lambda i: (0, 0)),
        ],
        out_specs=pl.BlockSpec((tm, n), lambda i: (i, 0)),
        out_shape=jax.ShapeDtypeStruct((t, n), BF16),
        compiler_params=pltpu.CompilerParams(
            dimension_semantics=("arbitrary",), vmem_limit_bytes=VMEM_LIMIT),
        name="norm_proj",
    )(x, g.reshape(1, d), w, col_gain.reshape(1, n))


def _window_kernel(par_ref, q_ref, k_ref, v_ref, o_ref, bias_scr, *, seq, blk, radius):
    span = blk + 2 * radius
    c = pl.program_id(1)
    group = ATTN_UNROLL
    row_lo = lax.broadcasted_iota(jnp.int32, (2 * blk, 1), 0) < blk
    slope_col = jnp.where(row_lo, par_ref[0, 2 * c], par_ref[0, 2 * c + 1])
    sink_col = jnp.where(row_lo, par_ref[1, 2 * c], par_ref[1, 2 * c + 1])

    iq = lax.broadcasted_iota(jnp.int32, (2 * blk, span), 0) % blk
    ik = lax.broadcasted_iota(jnp.int32, (2 * blk, span), 1)
    _fill_band_bias(bias_scr, iq, ik, radius, 1, slope_col)
    lo_q = _lo_mask((1, LANES))
    lo = _lo_mask((blk, LANES))

    def body(g, carry):
        starts, variants, values, scores = [], [], [], []
        for u in range(group):
            a = pl.multiple_of((g * group + u) * blk, blk)
            ks = pl.multiple_of(jnp.clip(a - radius, 0, seq - span), radius)
            q = q_ref[pl.ds(a, blk), :]
            zero = jnp.zeros_like(q)
            qm = jnp.concatenate([jnp.where(lo_q, q, zero), jnp.where(lo_q, zero, q)], axis=0)
            k = k_ref[pl.ds(ks, span), :]
            scores.append(lax.dot_general(qm, k, (((1,), (1,)), ((), ())),
                                          preferred_element_type=F32))
            starts.append(a)
            variants.append((a - ks) // radius)
            values.append(v_ref[pl.ds(ks, span), :])
        probs, sink_terms = [], []
        for u in range(group):
            s = scores[u] + bias_scr[variants[u]]
            m = jnp.maximum(jnp.max(s, axis=-1, keepdims=True), sink_col)
            probs.append(_exp2_probs(s, m))
            sink_terms.append(jnp.exp2(sink_col - m))
        for u in range(group):
            o, l = _value_and_sum(probs[u], values[u])
            o = o * (1.0 / (l + sink_terms[u]))
            o_ref[pl.ds(starts[u], blk), :] = jnp.where(lo, o[:blk], o[blk:]).astype(o_ref.dtype)
        return carry

    lax.fori_loop(0, seq // (blk * group), body, 0)


def _window_attention(qkv, params, *, q_blk0, k_blk, v_blk, n_pairs, radius, blk):
    b, s, _ = qkv.shape
    span = blk + 2 * radius
    return pl.pallas_call(
        functools.partial(_window_kernel, seq=s, blk=blk, radius=radius),
        grid=(b, n_pairs),
        in_specs=[
            pl.BlockSpec(memory_space=pltpu.SMEM),
            pl.BlockSpec((None, s, LANES), lambda bi, c: (bi, 0, q_blk0 + c)),
            pl.BlockSpec((None, s, LANES), lambda bi, c: (bi, 0, k_blk)),
            pl.BlockSpec((None, s, LANES), lambda bi, c: (bi, 0, v_blk)),
        ],
        out_specs=pl.BlockSpec((None, s, LANES), lambda bi, c: (bi, 0, c)),
        out_shape=jax.ShapeDtypeStruct((b, s, n_pairs * LANES), BF16),
        scratch_shapes=[pltpu.VMEM((3, 2 * blk, span), F32)],
        compiler_params=pltpu.CompilerParams(
            dimension_semantics=("arbitrary", "arbitrary"), vmem_limit_bytes=VMEM_LIMIT),
        name="window_attn",
    )(params, qkv, qkv, qkv)


PERM_CHUNK = 256
PHASES4 = 4
PIECE = PERM_CHUNK // PHASES4


def _stack_heads(q):
    lo = _lo_mask((1, LANES))
    zero = jnp.zeros_like(q)
    return jnp.concatenate([jnp.where(lo, q, zero), jnp.where(lo, zero, q)], axis=0)


def _merge_heads(x, blk):
    x = jnp.broadcast_to(x, (2 * blk, LANES))
    return jnp.where(_lo_mask((blk, LANES)), x[:blk], x[blk:])


def _attend_group(qs, ks, vs, biases):
    scores = [lax.dot_general(_stack_heads(q), k, (((1,), (1,)), ((), ())),
                              preferred_element_type=F32) for q, k in zip(qs, ks)]
    stats = []
    for s, bias in zip(scores, biases):
        s = s + bias
        m = jnp.max(s, axis=-1, keepdims=True)
        stats.append((_exp2_probs(s, m), m))
    outs = []
    for (p, m), v in zip(stats, vs):
        o, l = _value_and_sum(p, v)
        outs.append((o, m, l))
    return outs


def _fill_band_bias(bias_scr, pos_q, pos_k, radius, pos_scale, slope_col):
    for var in range(3):
        dist = jnp.abs(pos_q + var * radius - pos_k)
        dist_f = (dist * pos_scale).astype(F32)
        bias_scr[var] = jnp.where(dist <= radius, -(slope_col * dist_f) * LOG2E, NEG_INF)


def _dilated_kernel(par_ref, q_ref, k_ref, v_ref, o_ref, qp, kp, vp, acc_p, m_p, l_p,
                    bias1, bias4, bias16, *, seq, blk, radius):
    span = blk + 2 * radius
    n_chunk = seq // PERM_CHUNK
    group = ATTN_UNROLL
    c = pl.program_id(1)
    row_lo = lax.broadcasted_iota(jnp.int32, (2 * blk, 1), 0) < blk
    slope_col = jnp.where(row_lo, par_ref[0, 2 * c], par_ref[0, 2 * c + 1])

    def piece_pos(x):
        return PIECE * (x // PIECE) + PHASES4 * (x % 16) + (x % PIECE) // 16

    iq = lax.broadcasted_iota(jnp.int32, (2 * blk, span), 0) % blk
    ik = lax.broadcasted_iota(jnp.int32, (2 * blk, span), 1)
    _fill_band_bias(bias1, iq, ik, radius, 1, slope_col)
    _fill_band_bias(bias4, piece_pos(iq), piece_pos(ik), radius, 4, slope_col)
    _fill_band_bias(bias16, iq, ik, radius, 16, slope_col)

    rr = lax.broadcasted_iota(jnp.int32, (PERM_CHUNK, PERM_CHUNK), 0)
    cc = lax.broadcasted_iota(jnp.int32, (PERM_CHUNK, PERM_CHUNK), 1)
    to_phase = jnp.where(cc == 16 * (rr % 16) + PHASES4 * ((rr // 16) % 4) + rr // PIECE,
                         1.0, 0.0).astype(BF16)
    to_token = jnp.where(cc == PIECE * (rr % PHASES4) + 16 * ((rr // PHASES4) % 4) + rr // 16,
                         1.0, 0.0).astype(BF16)

    def permute_body(ch, carry):
        r0 = pl.multiple_of(ch * PERM_CHUNK, PERM_CHUNK)
        x = jnp.concatenate([q_ref[pl.ds(r0, PERM_CHUNK), :], k_ref[pl.ds(r0, PERM_CHUNK), :],
                             v_ref[pl.ds(r0, PERM_CHUNK), :]], axis=1)
        y = jnp.dot(to_phase, x, preferred_element_type=F32).astype(BF16)
        for p4 in range(PHASES4):
            rows = slice(PIECE * p4, PIECE * (p4 + 1))
            qp[p4, ch] = y[rows, 0:LANES]
            kp[p4, ch] = y[rows, LANES:2 * LANES]
            vp[p4, ch] = y[rows, 2 * LANES:3 * LANES]
        return carry

    lax.fori_loop(0, n_chunk, permute_body, 0)

    pieces_per_blk = blk // PIECE
    n_blk4 = seq // (PHASES4 * blk)

    def dil4_body(g, carry):
        p4 = g // (n_blk4 // group)
        j0 = (g % (n_blk4 // group)) * group
        qs, ks, vs, bs = [], [], [], []
        for u in range(group):
            pc = pieces_per_blk * (j0 + u)
            cs = jnp.clip(pc - radius // PIECE, 0, n_chunk - span // PIECE)
            qs.append(qp[p4, pl.ds(pc, pieces_per_blk)].reshape(blk, LANES))
            ks.append(kp[p4, pl.ds(cs, span // PIECE)].reshape(span, LANES))
            vs.append(vp[p4, pl.ds(cs, span // PIECE)].reshape(span, LANES))
            bs.append(bias4[(pc - cs) // (radius // PIECE)])
        for u, (o, m, l) in enumerate(_attend_group(qs, ks, vs, bs)):
            dst = (p4, pl.ds(pieces_per_blk * (j0 + u), pieces_per_blk))
            acc_p[dst] = _merge_heads(o, blk).reshape(pieces_per_blk, PIECE, LANES)
            m_p[dst] = _merge_heads(m, blk).reshape(pieces_per_blk, PIECE, LANES)
            l_p[dst] = _merge_heads(l, blk).reshape(pieces_per_blk, PIECE, LANES)
        return carry

    lax.fori_loop(0, PHASES4 * n_blk4 // group, dil4_body, 0)

    chunks_per_blk = blk // 16

    def dil16_body(g, carry):
        p4 = g // 2
        qs, ks, vs, bs, dsts = [], [], [], [], []
        for mm in range(2):
            off = pl.multiple_of((2 * (g % 2) + mm) * 16, 16)
            kk = kp[p4, :, pl.ds(off, 16), :].reshape(span, LANES)
            vv = vp[p4, :, pl.ds(off, 16), :].reshape(span, LANES)
            for jb in range(n_chunk // chunks_per_blk):
                dst = (p4, pl.ds(chunks_per_blk * jb, chunks_per_blk), pl.ds(off, 16), slice(None))
                qs.append(qp[dst].reshape(blk, LANES))
                ks.append(kk)
                vs.append(vv)
                bs.append(bias16[jb * (blk // radius)])
                dsts.append(dst)
        for dst, (o, m, l) in zip(dsts, _attend_group(qs, ks, vs, bs)):
            m_old = m_p[dst].reshape(blk, LANES)
            m_new = jnp.maximum(m_old, _merge_heads(m, blk))
            w_old = jnp.exp2(m_old - m_new)
            w_new = jnp.exp2(_merge_heads(m, blk) - m_new)
            acc = w_old * acc_p[dst].reshape(blk, LANES) + w_new * _merge_heads(o, blk)
            den = w_old * l_p[dst].reshape(blk, LANES) + w_new * _merge_heads(l, blk)
            acc_p[dst] = acc.reshape(chunks_per_blk, 16, LANES)
            l_p[dst] = den.reshape(chunks_per_blk, 16, LANES)
            m_p[dst] = m_new.reshape(chunks_per_blk, 16, LANES)
        return carry

    lax.fori_loop(0, PHASES4 * 2, dil16_body, 0)

    blk_per_chunk = PERM_CHUNK // blk

    def dil1_body(g, carry):
        merged = []
        for cc2 in range(group // blk_per_chunk):
            ch = g * (group // blk_per_chunk) + cc2
            acc = jnp.concatenate([acc_p[p4, ch] for p4 in range(PHASES4)], axis=0)
            den = jnp.concatenate([l_p[p4, ch] for p4 in range(PHASES4)], axis=0)
            mx = jnp.concatenate([m_p[p4, ch] for p4 in range(PHASES4)], axis=0)
            o_ph = (acc * (1.0 / den)).astype(BF16)
            lse = mx + jnp.log2(den)
            hi = lse.astype(BF16)
            rem = lse - hi.astype(F32)
            mid = rem.astype(BF16)
            low = (rem - mid.astype(F32)).astype(BF16)
            back = jnp.dot(to_token, jnp.concatenate([o_ph, hi, mid, low], axis=1),
                           preferred_element_type=F32)
            merged.append((back[:, 0:LANES], back[:, LANES:2 * LANES]
                           + back[:, 2 * LANES:3 * LANES] + back[:, 3 * LANES:4 * LANES]))
        qs, ks, vs, bs, starts = [], [], [], [], []
        for u in range(group):
            a = pl.multiple_of((g * group + u) * blk, blk)
            kst = pl.multiple_of(jnp.clip(a - radius, 0, seq - span), radius)
            qs.append(q_ref[pl.ds(a, blk), :])
            ks.append(k_ref[pl.ds(kst, span), :])
            vs.append(v_ref[pl.ds(kst, span), :])
            bs.append(bias1[(a - kst) // radius])
            starts.append(a)
        for u, (o, m, l) in enumerate(_attend_group(qs, ks, vs, bs)):
            o1 = _merge_heads(o * (1.0 / l), blk)
            lse1 = _merge_heads(m + jnp.log2(l), blk)
            o2, lse2 = merged[u // blk_per_chunk]
            half = slice((u % blk_per_chunk) * blk, (u % blk_per_chunk + 1) * blk)
            o2, lse2 = o2[half], lse2[half]
            top = jnp.maximum(lse1, lse2)
            e1 = jnp.exp2(lse1 - top)
            e2 = jnp.exp2(lse2 - top)
            o_ref[pl.ds(starts[u], blk), :] = ((e1 * o1 + e2 * o2)
                                               * (1.0 / (e1 + e2))).astype(o_ref.dtype)
        return carry

    lax.fori_loop(0, seq // (blk * group), dil1_body, 0)


def _dilated_attention(qkv, params, *, k_blk0, v_blk0, n_pairs, blk, radius):
    b, s, _ = qkv.shape
    span = blk + 2 * radius
    n_chunk = s // PERM_CHUNK
    assert s // 16 == span and s % PERM_CHUNK == 0 and blk % PIECE == 0
    phase = lambda dt: pltpu.VMEM((PHASES4, n_chunk, PIECE, LANES), dt)
    return pl.pallas_call(
        functools.partial(_dilated_kernel, seq=s, blk=blk, radius=radius),
        grid=(b, n_pairs),
        in_specs=[
            pl.BlockSpec(memory_space=pltpu.SMEM),
            pl.BlockSpec((None, s, LANES), lambda bi, c: (bi, 0, c)),
            pl.BlockSpec((None, s, LANES), lambda bi, c: (bi, 0, k_blk0 + c)),
            pl.BlockSpec((None, s, LANES), lambda bi, c: (bi, 0, v_blk0 + c)),
        ],
        out_specs=pl.BlockSpec((None, s, LANES), lambda bi, c: (bi, 0, c)),
        out_shape=jax.ShapeDtypeStruct((b, s, n_pairs * LANES), BF16),
        scratch_shapes=[phase(BF16), phase(BF16), phase(BF16), phase(F32), phase(F32), phase(F32)]
        + [pltpu.VMEM((3, 2 * blk, span), F32)] * 3,
        compiler_params=pltpu.CompilerParams(
            dimension_semantics=("arbitrary", "arbitrary"), vmem_limit_bytes=VMEM_LIMIT),
        name="dilated_attn",
    )(params, qkv, qkv, qkv)


def _na_kernel(q_ref, k_ref, v_ref, bias_ref, o_ref, *, rows):
    kh = NA_ROWS
    group = ATTN_UNROLL
    lo_q = _lo_mask((1, LANES))
    lo = _lo_mask((GRID_W, LANES))

    def body(g, carry):
        starts, offs, values, scores = [], [], [], []
        for u in range(group):
            i = g * group + u
            rs = jnp.clip(i - kh // 2, 0, rows - kh)
            qs = pl.multiple_of(i * GRID_W, GRID_W)
            ks = pl.multiple_of(rs * GRID_W, GRID_W)
            q = q_ref[pl.ds(qs, GRID_W), :]
            zero = jnp.zeros_like(q)
            qm = jnp.concatenate([jnp.where(lo_q, q, zero), jnp.where(lo_q, zero, q)], axis=0)
            k = k_ref[pl.ds(ks, kh * GRID_W), :]
            scores.append(lax.dot_general(qm, k, (((1,), (1,)), ((), ())),
                                          preferred_element_type=F32))
            starts.append(qs)
            offs.append(rs - i + (NA_ROWS - 1))
            values.append(v_ref[pl.ds(ks, kh * GRID_W), :])
        probs = []
        for u in range(group):
            s = scores[u] + jnp.concatenate(
                [bias_ref[offs[u] + 2 * j] for j in range(kh // 2)], axis=1)
            probs.append(_exp2_probs(s, jnp.max(s, axis=-1, keepdims=True)))
        for u in range(group):
            o, l = _value_and_sum(probs[u], values[u])
            o = o * (1.0 / l)
            o_ref[pl.ds(starts[u], GRID_W), :] = jnp.where(
                lo, o[:GRID_W], o[GRID_W:]).astype(o_ref.dtype)
        return carry

    lax.fori_loop(0, rows // group, body, 0)


def _na_bias_table(rpb):
    qc = np.arange(GRID_W)[:, None]
    kc = np.arange(GRID_W)[None, :]
    qstart = np.clip(qc - NA_COLS // 2, 0, GRID_W - NA_COLS)
    valid = (kc >= qstart) & (kc < qstart + NA_COLS)
    coff = np.clip(kc - qc, -(NA_COLS - 1), NA_COLS - 1) + NA_COLS - 1
    pick = (coff[:, :, None] == np.arange(2 * NA_COLS - 1)).astype(np.float32)
    t = jnp.einsum("hdj,qkj->hdqk", rpb.astype(F32), jnp.asarray(pick),
                   precision=lax.Precision.HIGHEST)
    t = jnp.where(jnp.asarray(valid), t * LOG2E, NEG_INF)
    t = jnp.concatenate([t[:, :-1], t[:, 1:]], axis=-1)
    n_off = 2 * NA_ROWS - 2
    t = t.reshape(C_HEADS // 2, 2, n_off, GRID_W, 2 * GRID_W).transpose(0, 2, 1, 3, 4)
    return t.reshape(C_HEADS // 2, n_off, 2 * GRID_W, 2 * GRID_W)


def _na_attention(qkv, bias):
    b, s, n_cols = qkv.shape
    n_pairs = C_HEADS // 2
    rows = s // GRID_W
    return pl.pallas_call(
        functools.partial(_na_kernel, rows=rows),
        grid=(b, n_pairs),
        in_specs=[
            pl.BlockSpec((None, s, LANES), lambda bi, c: (bi, 0, c)),
            pl.BlockSpec((None, s, LANES), lambda bi, c: (bi, 0, n_pairs + c)),
            pl.BlockSpec((None, s, LANES), lambda bi, c: (bi, 0, 2 * n_pairs + c)),
            pl.BlockSpec((None,) + bias.shape[1:], lambda bi, c: (c, 0, 0, 0)),
        ],
        out_specs=pl.BlockSpec((None, s, LANES), lambda bi, c: (bi, 0, c)),
        out_shape=jax.ShapeDtypeStruct((b, s, n_pairs * LANES), BF16),
        compiler_params=pltpu.CompilerParams(
            dimension_semantics=("arbitrary", "arbitrary"), vmem_limit_bytes=VMEM_LIMIT),
        name="na_attn",
    )(qkv, qkv, qkv, bias)


def _silu(g):
    return g * (1.0 / (1.0 + jnp.exp(-g)))


def _ffn_kernel(x_ref, oa_ref, ob_ref, wa_ref, wb_ref, g_ref, wg_ref, wu_ref, wd_ref, y_ref,
                a_scr):
    x = (x_ref[...] + jnp.dot(oa_ref[...], wa_ref[...], preferred_element_type=F32)
         + jnp.dot(ob_ref[...], wb_ref[...], preferred_element_type=F32))
    h = _rms_rows(x, g_ref[...]).astype(BF16)
    f = wg_ref.shape[1]
    for j in range(f // MXU_N):
        sl = slice(j * MXU_N, (j + 1) * MXU_N)
        gate = jnp.dot(h, wg_ref[:, sl], preferred_element_type=F32)
        up = jnp.dot(h, wu_ref[:, sl], preferred_element_type=F32)
        a_scr[:, sl] = (_silu(gate) * up).astype(BF16)
    y_ref[...] = x + jnp.dot(a_scr[...], wd_ref[...], preferred_element_type=F32)


def _ffn(x, oa, ob, wa, wb, g, wg, wu, wd, tm=512):
    t, d = x.shape
    f = wg.shape[1]
    const = lambda a: pl.BlockSpec(a.shape, lambda i: (0, 0), pipeline_mode=pl.Buffered(1))
    row = lambda w: pl.BlockSpec((tm, w), lambda i: (i, 0))
    return pl.pallas_call(
        _ffn_kernel,
        grid=(t // tm,),
        in_specs=[row(d), row(oa.shape[1]), row(ob.shape[1]), const(wa), const(wb),
                  pl.BlockSpec((1, d), lambda i: (0, 0)),
                  const(wg), const(wu), const(wd)],
        out_specs=pl.BlockSpec((tm, d), lambda i: (i, 0)),
        out_shape=jax.ShapeDtypeStruct((t, d), F32),
        scratch_shapes=[pltpu.VMEM((tm, f), BF16)],
        compiler_params=pltpu.CompilerParams(
            dimension_semantics=("arbitrary",), vmem_limit_bytes=VMEM_LIMIT),
        name="dense_swiglu",
    )(x, oa, ob, wa, wb, g.reshape(1, d), wg, wu, wd)


TOP_K = 2
MOE_TB = 512
MOE_TM = 512
MOE_RC = 256
KEY_STRIDE = 65536.0


def _router_kernel(x_ref, o_ref, wo_ref, g_ref, wr_hi_ref, wr_lo_ref, x1_ref, h_ref, info_ref,
                   info_t_ref, cnt_ref, carry_scr, ltri_scr):
    i = pl.program_id(0)
    tb = x_ref.shape[0]
    lane = lax.broadcasted_iota(jnp.int32, (1, LANES), 1)

    @pl.when(i == 0)
    def _():
        carry_scr[...] = jnp.zeros_like(carry_scr)
        r = lax.broadcasted_iota(jnp.int32, (tb, tb), 0)
        cidx = lax.broadcasted_iota(jnp.int32, (tb, tb), 1)
        ltri_scr[...] = jnp.where(cidx < r, 1.0, 0.0).astype(BF16)

    x1 = x_ref[...] + jnp.dot(o_ref[...], wo_ref[...], preferred_element_type=F32)
    x1_ref[...] = x1
    h = _rms_rows(x1, g_ref[...])
    h_hi = h.astype(BF16)
    h_ref[...] = h_hi
    h_lo = (h - h_hi.astype(F32)).astype(BF16)
    logits = (jnp.dot(h_hi, wr_hi_ref[...], preferred_element_type=F32)
              + jnp.dot(h_lo, wr_hi_ref[...], preferred_element_type=F32)
              + jnp.dot(h_hi, wr_lo_ref[...], preferred_element_type=F32))
    logits = jnp.where(lane < N_EXPERTS, logits, NEG_INF)
    v1 = jnp.max(logits, axis=-1, keepdims=True)
    i1 = jnp.min(jnp.where(logits == v1, lane, LANES), axis=-1, keepdims=True)
    rest = jnp.where(lane == i1, NEG_INF, logits)
    v2 = jnp.max(rest, axis=-1, keepdims=True)
    i2 = jnp.min(jnp.where(rest == v2, lane, LANES), axis=-1, keepdims=True)
    e2 = jnp.exp(v2 - v1)
    g1 = 1.0 / (1.0 + e2)
    g2 = e2 * g1

    pick1 = lane == i1
    pick2 = lane == i2
    picked = jnp.where(pick1, 1.0, jnp.where(pick2, 1.0, 0.0))
    before = (jnp.dot(ltri_scr[...], picked.astype(BF16), preferred_element_type=F32)
              + carry_scr[...])
    rank1 = jnp.sum(jnp.where(pick1, before, 0.0), axis=-1, keepdims=True)
    rank2 = jnp.sum(jnp.where(pick2, before, 0.0), axis=-1, keepdims=True)
    key1 = i1.astype(F32) * KEY_STRIDE + rank1
    key2 = i2.astype(F32) * KEY_STRIDE + rank2
    info = jnp.where(lane == 0, key1, jnp.where(lane == 1, key2,
                     jnp.where(lane == 2, g1, jnp.where(lane == 3, g2, 0.0))))
    info_ref[...] = info
    info_t_ref[...] = info.T[:8, :]
    counts = jnp.sum(picked, axis=0, keepdims=True)
    cnt_ref[...] = jnp.broadcast_to(counts, cnt_ref.shape)
    carry_scr[...] += counts


def _router(x, o, wo, g, wr):
    t, d = x.shape
    nb = t // MOE_TB
    wr_hi = wr.astype(BF16)
    wr_lo = (wr - wr_hi.astype(F32)).astype(BF16)
    row = lambda w: pl.BlockSpec((MOE_TB, w), lambda i: (i, 0))
    const = lambda a: pl.BlockSpec(a.shape, lambda i: (0, 0))
    return pl.pallas_call(
        _router_kernel,
        grid=(nb,),
        in_specs=[row(d), row(o.shape[1]), const(wo), pl.BlockSpec((1, d), lambda i: (0, 0)),
                  const(wr_hi), const(wr_lo)],
        out_specs=[row(d), row(d), row(LANES),
                   pl.BlockSpec((None, 8, MOE_TB), lambda i: (i, 0, 0)),
                   pl.BlockSpec((None, 8, LANES), lambda i: (i, 0, 0))],
        out_shape=[jax.ShapeDtypeStruct((t, d), F32),
                   jax.ShapeDtypeStruct((t, d), BF16),
                   jax.ShapeDtypeStruct((t, LANES), F32),
                   jax.ShapeDtypeStruct((nb, 8, MOE_TB), F32),
                   jax.ShapeDtypeStruct((nb, 8, LANES), F32)],
        scratch_shapes=[pltpu.VMEM((1, LANES), F32), pltpu.VMEM((MOE_TB, MOE_TB), BF16)],
        compiler_params=pltpu.CompilerParams(
            dimension_semantics=("arbitrary",), vmem_limit_bytes=VMEM_LIMIT),
        name="moe_router",
    )(x, o, wo, g.reshape(1, d), wr_hi, wr_lo)


def _moe_plan(block_counts, n_tokens):
    nb = block_counts.shape[0]
    n_tiles = (TOP_K * n_tokens + N_EXPERTS * (MOE_TM - 1)) // MOE_TM
    n_chunks = n_tiles * MOE_TM // MOE_RC
    tc = block_counts.astype(jnp.int32)
    cb = jnp.concatenate([jnp.zeros((1, N_EXPERTS), jnp.int32), jnp.cumsum(tc, axis=0)])
    counts = cb[-1]
    padded = ((counts + MOE_TM - 1) // MOE_TM) * MOE_TM
    ends = jnp.cumsum(padded)
    starts = ends - padded
    total = ends[-1]

    tiles = jnp.arange(n_tiles, dtype=jnp.int32)
    n_valid_tiles = total // MOE_TM
    t_valid = (tiles < n_valid_tiles).astype(jnp.int32)
    t_src = jnp.maximum(jnp.minimum(tiles, n_valid_tiles - 1), 0)
    t_exp = jnp.minimum(jnp.sum((t_src * MOE_TM)[:, None] >= ends[None, :], axis=1),
                        N_EXPERTS - 1).astype(jnp.int32)

    r0 = jnp.arange(n_chunks, dtype=jnp.int32) * MOE_RC
    ex = jnp.minimum(jnp.sum(r0[:, None] >= ends[None, :], axis=1), N_EXPERTS - 1)
    live = r0 < total
    lo = r0 - starts[ex]
    base = (ex * int(KEY_STRIDE) + lo).astype(jnp.int32)
    cbe = cb[:, ex]
    g_lo = jnp.sum(cbe[1:] <= lo[None, :], axis=0).astype(jnp.int32)
    g_cnt = jnp.sum(live[None, :] & (cbe[:-1] < (lo + MOE_RC)[None, :]) & (cbe[1:] > lo[None, :]),
                    axis=0).astype(jnp.int32)

    row_lo = starts[None, :] + cb[:-1]
    row_hi = starts[None, :] + cb[1:]
    c_first = row_lo // MOE_RC
    c_num = jnp.where(row_hi > row_lo, (row_hi - 1) // MOE_RC - c_first + 1, 0)
    c_end = jnp.cumsum(c_num, axis=1)
    slots = jnp.arange(COMBINE_MAX_CHUNKS, dtype=jnp.int32)
    e_of = jnp.minimum(jnp.sum(slots[None, :, None] >= c_end[:, None, :], axis=2),
                       N_EXPERTS - 1)
    before = jnp.take_along_axis(c_end - c_num, e_of, axis=1)
    c_c = jnp.take_along_axis(c_first, e_of, axis=1) + slots[None, :] - before
    c_cnt = c_end[:, -1].astype(jnp.int32)
    c_c = jnp.clip(c_c, 0, n_chunks - 1).astype(jnp.int32).reshape(-1)
    return dict(n_tiles=n_tiles, n_chunks=n_chunks, t_valid=t_valid, t_exp=t_exp,
                base=base, live=live.astype(jnp.int32), g_lo=g_lo, g_cnt=g_cnt,
                c_cnt=c_cnt, c_c=c_c)


def _gather_kernel(live_ref, lo_ref, cnt_ref, base_ref, info_t_ref, h_ref,
                   xs_ref, gs_ref, acc_scr, gsum_scr):
    c = pl.program_id(0)
    tb = info_t_ref.shape[-1]

    @pl.when(live_ref[c] == 1)
    def _():
        first = lo_ref[c]
        n = cnt_ref[c]
        row = lax.broadcasted_iota(jnp.int32, (MOE_RC, tb), 0).astype(F32)
        acc_scr[...] = jnp.zeros_like(acc_scr)
        gsum_scr[...] = jnp.zeros_like(gsum_scr)

        def two_blocks(it, carry):
            rows, gate = None, None
            for w in range(2):
                j = 2 * it + w
                b = first + jnp.minimum(j, n - 1)
                base = jnp.where(j < n, base_ref[c], NO_MATCH_BASE).astype(F32)
                info = info_t_ref[b]
                m1 = (info[0:1, :] - base) == row
                m2 = (info[1:2, :] - base) == row
                sel = jnp.where(m1, 1.0, jnp.where(m2, 1.0, 0.0)).astype(BF16)
                g = jnp.sum(jnp.where(m1, info[2:3, :], jnp.where(m2, info[3:4, :], 0.0)),
                            axis=-1, keepdims=True)
                r = jnp.dot(sel, h_ref[pl.ds(pl.multiple_of(b * tb, tb), tb), :],
                            preferred_element_type=F32)
                rows = r if rows is None else rows + r
                gate = g if gate is None else gate + g
            acc_scr[...] += rows
            gsum_scr[...] += gate
            return carry

        lax.fori_loop(0, (n + 1) // 2, two_blocks, 0)
        xs_ref[...] = acc_scr[...].astype(xs_ref.dtype)
        gs_ref[...] = jnp.broadcast_to(gsum_scr[...], gs_ref.shape)

    @pl.when(live_ref[c] == 0)
    def _():
        xs_ref[...] = jnp.zeros_like(xs_ref)
        gs_ref[...] = jnp.zeros_like(gs_ref)


def _moe_gather(plan, info_t, h):
    t, d = h.shape
    rows = plan["n_chunks"] * MOE_RC
    chunk = lambda c, live, lo, cn, ba: (c, 0)
    grid_spec = pltpu.PrefetchScalarGridSpec(
        num_scalar_prefetch=4,
        grid=(plan["n_chunks"],),
        in_specs=[pl.BlockSpec(info_t.shape, lambda c, live, lo, cn, ba: (0, 0, 0),
                               pipeline_mode=pl.Buffered(1)),
                  pl.BlockSpec((t, d), lambda c, live, lo, cn, ba: (0, 0),
                               pipeline_mode=pl.Buffered(1))],
        out_specs=[pl.BlockSpec((MOE_RC, d), chunk), pl.BlockSpec((MOE_RC, LANES), chunk)],
        scratch_shapes=[pltpu.VMEM((MOE_RC, d), F32), pltpu.VMEM((MOE_RC, 1), F32)],
    )
    return pl.pallas_call(
        _gather_kernel,
        grid_spec=grid_spec,
        out_shape=[jax.ShapeDtypeStruct((rows, d), BF16),
                   jax.ShapeDtypeStruct((rows, LANES), F32)],
        compiler_params=pltpu.CompilerParams(
            dimension_semantics=("arbitrary",), vmem_limit_bytes=VMEM_LIMIT),
        name="moe_gather",
    )(plan["live"], plan["g_lo"], plan["g_cnt"], plan["base"], info_t, h)


def _experts_kernel(exp_ref, val_ref, xs_ref, gs_ref, wg_ref, wu_ref, wd_ref, ys_ref,
                    acc_scr):
    i = pl.program_id(0)
    j = pl.program_id(1)

    last = j == pl.num_programs(1) - 1

    @pl.when(j == 0)
    def _():
        acc_scr[...] = jnp.zeros_like(acc_scr)

    @pl.when(val_ref[i] == 1)
    def _():
        x = xs_ref[...]
        gate = jnp.dot(x, wg_ref[...], preferred_element_type=F32)
        up = jnp.dot(x, wu_ref[...], preferred_element_type=F32)
        a = (_silu(gate) * up).astype(BF16)
        acc_scr[...] += jnp.dot(a, wd_ref[...], preferred_element_type=F32)

    @pl.when(last)
    def _():
        ys_ref[...] = (acc_scr[...] * gs_ref[:, 0:1]).astype(ys_ref.dtype)


def _moe_experts(plan, xs, gs, wg, wu, wd, tf):
    rows, d = xs.shape
    f = wg.shape[2]
    nf = f // tf

    def wcol(i, j, ex, val):
        return (ex[i], 0, jnp.where(val[i] == 1, j, nf - 1))

    def wrow(i, j, ex, val):
        return (ex[i], jnp.where(val[i] == 1, j, nf - 1), 0)

    tile = lambda i, j, ex, val: (i, 0)
    grid_spec = pltpu.PrefetchScalarGridSpec(
        num_scalar_prefetch=2,
        grid=(plan["n_tiles"], nf),
        in_specs=[pl.BlockSpec((MOE_TM, d), tile),
                  pl.BlockSpec((MOE_TM, LANES), tile),
                  pl.BlockSpec((None, d, tf), wcol),
                  pl.BlockSpec((None, d, tf), wcol),
                  pl.BlockSpec((None, tf, d), wrow)],
        out_specs=pl.BlockSpec((MOE_TM, d), tile),
        scratch_shapes=[pltpu.VMEM((MOE_TM, d), F32)],
    )
    return pl.pallas_call(
        _experts_kernel,
        grid_spec=grid_spec,
        out_shape=jax.ShapeDtypeStruct((rows, d), BF16),
        compiler_params=pltpu.CompilerParams(
            dimension_semantics=("arbitrary", "arbitrary"), vmem_limit_bytes=VMEM_LIMIT),
        name="moe_experts",
    )(plan["t_exp"], plan["t_valid"], xs, gs, wg, wu, wd)


NO_MATCH_BASE = -(2 ** 30)


COMBINE_GROUP = 4
COMBINE_MAX_CHUNKS = N_EXPERTS * (MOE_TB // MOE_RC + 1)


def _combine_kernel(cnt_ref, pc_ref, base_ref, info_ref, x_ref, ys_hbm, o_ref, ybuf, sem):
    b = pl.program_id(0)
    tb = x_ref.shape[0]
    d = x_ref.shape[1]

    def n_fetch(blk):
        return ((cnt_ref[blk] + COMBINE_GROUP - 1) // COMBINE_GROUP) * COMBINE_GROUP

    def chunk_of(blk, j):
        return pc_ref[blk * COMBINE_MAX_CHUNKS + jnp.minimum(j, cnt_ref[blk] - 1)]

    def copy(blk, j, slot):
        row0 = pl.multiple_of(chunk_of(blk, j) * MOE_RC, MOE_RC)
        return pltpu.make_async_copy(ys_hbm.at[pl.ds(row0, MOE_RC), :], ybuf.at[slot, j],
                                     sem.at[slot])

    def start_block(blk, slot):
        def issue(j, carry):
            copy(blk, j, slot).start()
            return carry
        lax.fori_loop(0, n_fetch(blk), issue, 0)

    @pl.when(b == 0)
    def _():
        start_block(0, 0)

    @pl.when(b + 1 < pl.num_programs(0))
    def _():
        start_block(b + 1, (b + 1) % 2)

    slot = b % 2
    n = cnt_ref[b]

    def drain(j, carry):
        copy(b, j, slot).wait()
        return carry

    lax.fori_loop(0, n_fetch(b), drain, 0)

    lane = lax.broadcasted_iota(jnp.int32, (1, LANES), 1)
    info = info_ref[...]
    k1 = jnp.sum(jnp.where(lane == 0, info, 0.0), axis=-1, keepdims=True)
    k2 = jnp.sum(jnp.where(lane == 1, info, 0.0), axis=-1, keepdims=True)
    col = lax.broadcasted_iota(jnp.int32, (tb, MOE_RC), 1).astype(F32)
    o_ref[...] = x_ref[...]

    def group(gi, carry):
        sels = []
        for w in range(COMBINE_GROUP):
            j = gi * COMBINE_GROUP + w
            base = jnp.where(j < n, base_ref[chunk_of(b, j)], NO_MATCH_BASE).astype(F32)
            sels.append(jnp.where(k1 - base == col, 1.0,
                                  jnp.where(k2 - base == col, 1.0, 0.0)).astype(BF16))
        rows = ybuf[slot, pl.ds(gi * COMBINE_GROUP, COMBINE_GROUP)]
        o_ref[...] += jnp.dot(jnp.concatenate(sels, axis=1),
                              rows.reshape(COMBINE_GROUP * MOE_RC, d),
                              preferred_element_type=F32)
        return carry

    lax.fori_loop(0, n_fetch(b) // COMBINE_GROUP, group, 0)


def _moe_combine(plan, info, ys, x):
    t, d = x.shape
    block = lambda b, cn, pc, ba: (b, 0)
    grid_spec = pltpu.PrefetchScalarGridSpec(
        num_scalar_prefetch=3,
        grid=(t // MOE_TB,),
        in_specs=[pl.BlockSpec((MOE_TB, LANES), block),
                  pl.BlockSpec((MOE_TB, d), block),
                  pl.BlockSpec(memory_space=pl.ANY)],
        out_specs=pl.BlockSpec((MOE_TB, d), block),
        scratch_shapes=[pltpu.VMEM((2, COMBINE_MAX_CHUNKS, MOE_RC, d), BF16),
                        pltpu.SemaphoreType.DMA((2,))],
    )
    return pl.pallas_call(
        _combine_kernel,
        grid_spec=grid_spec,
        out_shape=jax.ShapeDtypeStruct((t, d), F32),
        compiler_params=pltpu.CompilerParams(
            dimension_semantics=("arbitrary",), vmem_limit_bytes=VMEM_LIMIT),
        name="moe_combine",
    )(plan["c_cnt"], plan["c_c"], plan["base"], info, x, ys)


def _moe(x, o, wo, g, wr, wg, wu, wd, tf=1792):
    t, _ = x.shape
    x1, h, info, info_t, block_counts = _router(x, o, wo, g, wr)
    plan = _moe_plan(block_counts[:, 0, :N_EXPERTS], t)
    xs, gs = _moe_gather(plan, info_t, h)
    ys = _moe_experts(plan, xs, gs, wg, wu, wd, tf)
    return _moe_combine(plan, info, ys, x1)


def _alibi_slopes():
    s = np.exp2(-8.0 * np.arange(1, N_HEADS + 1) / N_HEADS).astype(np.float32)
    return s[0::2][:A_HEADS], s[1::2][:B_Q_HEADS]


_B_HEAD_ORDER = np.array([0, 4, 1, 5, 2, 6, 3, 7])


def _even_layer(x2d, batch, norm1, w_in, qn_a, kn_a, qn_b, kn_b, sink_b, w_out,
                norm2, w_gate, w_up, w_down):
    t, d = x2d.shape
    s = t // batch
    wa = A_HEADS * HEAD_DIM
    wq_b = B_Q_HEADS * HEAD_DIM
    wkv_b = B_KV_HEADS * HEAD_DIM
    qb0 = 3 * wa
    q_scale = HEAD_DIM ** -0.5 * LOG2E

    w_qb = w_in[:, qb0:qb0 + wq_b].reshape(d, B_Q_HEADS, HEAD_DIM)[:, _B_HEAD_ORDER]
    w_in_p = jnp.concatenate(
        [w_in[:, :qb0], w_qb.reshape(d, wq_b), w_in[:, qb0 + wq_b:]], axis=1).astype(BF16)
    ones = jnp.ones((HEAD_DIM,), F32)
    col_gain = jnp.concatenate([
        jnp.tile(qn_a * q_scale, A_HEADS), jnp.tile(kn_a, A_HEADS), jnp.tile(ones, A_HEADS),
        jnp.tile(qn_b * q_scale, B_Q_HEADS), jnp.tile(kn_b, B_KV_HEADS),
        jnp.tile(ones, B_KV_HEADS)])
    cpl = wa // LANES
    norm_chunks = ((True,) * (2 * cpl) + (False,) * cpl + (True,) * (wq_b // LANES)
                   + (True,) * (wkv_b // LANES) + (False,) * (wkv_b // LANES))
    qkv = _norm_proj(x2d, norm1, w_in_p, col_gain, norm_chunks)
    n_cols = qkv.shape[1]
    qkv3 = qkv.reshape(batch, s, n_cols)

    slopes_a, slopes_b = _alibi_slopes()
    par_a = jnp.asarray(slopes_a).reshape(1, A_HEADS)
    radii = {window // (2 * dil) for window, dil in DIL_CONFIGS}
    assert tuple(dil for _, dil in DIL_CONFIGS) == (1, 4, 16) and len(radii) == 1
    oa = _dilated_attention(qkv3, par_a, k_blk0=cpl, v_blk0=2 * cpl, n_pairs=cpl,
                            blk=A_BLOCK, radius=radii.pop())
    par_b = jnp.stack([jnp.asarray(slopes_b[_B_HEAD_ORDER]),
                       sink_b.astype(F32)[_B_HEAD_ORDER] * LOG2E])
    kb = 3 * cpl + wq_b // LANES
    ob = _window_attention(qkv3, par_b, q_blk0=3 * cpl, k_blk=kb, v_blk=kb + wkv_b // LANES,
                           n_pairs=wq_b // LANES, radius=B_WINDOW, blk=B_BLOCK)
    w_out_a = w_out[:wa].astype(BF16)
    w_out_b = w_out[wa:].reshape(B_Q_HEADS, HEAD_DIM, d)[_B_HEAD_ORDER].reshape(wq_b, d)
    return _ffn(x2d, oa.reshape(t, wa), ob.reshape(t, wq_b), w_out_a, w_out_b.astype(BF16),
                norm2, w_gate.astype(BF16), w_up.astype(BF16), w_down.astype(BF16))


def _odd_layer(x2d, batch, norm1, w_qkv, qn, kn, rpb, w_out, norm2, w_router,
               w_gate, w_up, w_down):
    t, d = x2d.shape
    s = t // batch
    wc = C_HEADS * HEAD_DIM
    ones = jnp.ones((HEAD_DIM,), F32)
    col_gain = jnp.concatenate([jnp.tile(qn * (HEAD_DIM ** -0.5 * LOG2E), C_HEADS),
                                jnp.tile(kn, C_HEADS), jnp.tile(ones, C_HEADS)])
    cpl = wc // LANES
    norm_chunks = (True,) * (2 * cpl) + (False,) * cpl
    qkv = _norm_proj(x2d, norm1, w_qkv.astype(BF16), col_gain, norm_chunks)
    o = _na_attention(qkv.reshape(batch, s, 3 * wc), _na_bias_table(rpb))
    wr = jnp.zeros((d, LANES), F32).at[:, :N_EXPERTS].set(w_router.astype(F32))
    return _moe(x2d, o.reshape(t, wc), w_out.astype(BF16), norm2, wr, w_gate.astype(BF16),
                w_up.astype(BF16), w_down.astype(BF16))


def kernel(x, ev_norm1, ev_w_in, ev_qn_a, ev_kn_a, ev_qn_b, ev_kn_b, ev_sink_b, ev_w_out, ev_norm2, ev_ffn_gate, ev_ffn_up, ev_ffn_down, od_norm1, od_w_qkv, od_qn, od_kn, od_rpb, od_w_out, od_norm2, od_router, od_exp_gate, od_exp_up, od_exp_down):
    batch, s, d = x.shape
    depth = ev_norm1.shape[0] + od_norm1.shape[0]
    h = x.reshape(batch * s, d)
    for layer in range(depth):
        j = layer // 2
        if layer % 2 == 0:
            h = _even_layer(h, batch, ev_norm1[j], ev_w_in[j], ev_qn_a[j], ev_kn_a[j],
                            ev_qn_b[j], ev_kn_b[j], ev_sink_b[j], ev_w_out[j], ev_norm2[j],
                            ev_ffn_gate[j], ev_ffn_up[j], ev_ffn_down[j])
        else:
            h = _odd_layer(h, batch, od_norm1[j], od_w_qkv[j], od_qn[j], od_kn[j], od_rpb[j],
                           od_w_out[j], od_norm2[j], od_router[j], od_exp_gate[j],
                           od_exp_up[j], od_exp_down[j])
    return h.reshape(batch, s, d)
```

```python
import functools

import numpy as np
import jax
import jax.numpy as jnp
from jax import lax
from jax.experimental import pallas as pl
from jax.experimental.pallas import tpu as pltpu

D_MODEL = 1024
HEAD_DIM = 64
N_HEADS = D_MODEL // HEAD_DIM
A_HEADS = N_HEADS // 2
B_Q_HEADS = N_HEADS // 2
B_KV_HEADS = max(1, B_Q_HEADS // 4)
C_HEADS = N_HEADS
DIL_CONFIGS = ((128, 1), (512, 4), (2048, 16))
A_BLOCK = 128
B_WINDOW = 128
B_BLOCK = 128
GRID_W = 64
NA_ROWS = 8
NA_COLS = 16
N_EXPERTS = 8
RMS_EPS = 1e-6
NEG_INF = -1e30

LANES = 128
MXU_N = 256
VMEM_LIMIT = 56 * 1024 * 1024

ATTN_UNROLL = 8

F32 = jnp.float32
BF16 = jnp.bfloat16


def _lo_mask(shape):
    return lax.broadcasted_iota(jnp.int32, shape, len(shape) - 1) < HEAD_DIM


LOG2E = 1.4426950408889634


def _exp2_probs(s, m):
    return jnp.exp2(s - m).astype(BF16)


def _value_and_sum(p, v):
    ov = jnp.dot(p, jnp.concatenate([v, jnp.ones_like(v)], axis=1), preferred_element_type=F32)
    return ov[:, :LANES], ov[:, LANES:LANES + 1]


def _rms_rows(x, g):
    ms = jnp.mean(x * x, axis=-1, keepdims=True)
    return x * lax.rsqrt(ms + RMS_EPS) * g


def _norm_proj_kernel(x_ref, g_ref, w_ref, cg_ref, o_ref, *, norm_chunks):
    h = _rms_rows(x_ref[...], g_ref[...]).astype(BF16)
    n_out = o_ref.shape[-1]
    lo = _lo_mask((1, LANES))
    for j in range(n_out // MXU_N):
        y = jnp.dot(h, w_ref[:, j * MXU_N:(j + 1) * MXU_N], preferred_element_type=F32)
        for half in range(MXU_N // LANES):
            c = j * (MXU_N // LANES) + half
            yc = y[:, half * LANES:(half + 1) * LANES]
            if norm_chunks[c]:
                sq = yc * yc
                s_lo = jnp.sum(jnp.where(lo, sq, 0.0), axis=-1, keepdims=True)
                s_hi = jnp.sum(jnp.where(lo, 0.0, sq), axis=-1, keepdims=True)
                inv = jnp.where(lo, lax.rsqrt(s_lo * (1.0 / HEAD_DIM) + RMS_EPS),
                                lax.rsqrt(s_hi * (1.0 / HEAD_DIM) + RMS_EPS))
                yc = yc * inv * cg_ref[:, c * LANES:(c + 1) * LANES]
            o_ref[:, c * LANES:(c + 1) * LANES] = yc.astype(o_ref.dtype)


def _norm_proj(x, g, w, col_gain, norm_chunks, tm=512):
    t, d = x.shape
    n = w.shape[1]
    return pl.pallas_call(
        functools.partial(_norm_proj_kernel, norm_chunks=norm_chunks),
        grid=(t // tm,),
        in_specs=[
            pl.BlockSpec((tm, d), lambda i: (i, 0)),
            pl.BlockSpec((1, d), lambda i: (0, 0)),
            pl.BlockSpec((d, n), lambda i: (0, 0)),
            pl.BlockSpec((1, n), lambda i: (0, 0)),
        ],
        out_specs=pl.BlockSpec((tm, n), lambda i: (i, 0)),
        out_shape=jax.ShapeDtypeStruct((t, n), BF16),
        compiler_params=pltpu.CompilerParams(
            dimension_semantics=("arbitrary",), vmem_limit_bytes=VMEM_LIMIT),
        name="norm_proj",
    )(x, g.reshape(1, d), w, col_gain.reshape(1, n))


def _window_kernel(par_ref, q_ref, k_ref, v_ref, o_ref, bias_scr, *, seq, blk, radius):
    span = blk + 2 * radius
    c = pl.program_id(1)
    group = ATTN_UNROLL
    row_lo = lax.broadcasted_iota(jnp.int32, (2 * blk, 1), 0) < blk
    slope_col = jnp.where(row_lo, par_ref[0, 2 * c], par_ref[0, 2 * c + 1])
    sink_col = jnp.where(row_lo, par_ref[1, 2 * c], par_ref[1, 2 * c + 1])

    iq = lax.broadcasted_iota(jnp.int32, (2 * blk, span), 0) % blk
    ik = lax.broadcasted_iota(jnp.int32, (2 * blk, span), 1)
    _fill_band_bias(bias_scr, iq, ik, radius, 1, slope_col)
    lo_q = _lo_mask((1, LANES))
    lo = _lo_mask((blk, LANES))

    def body(g, carry):
        starts, variants, values, scores = [], [], [], []
        for u in range(group):
            a = pl.multiple_of((g * group + u) * blk, blk)
            ks = pl.multiple_of(jnp.clip(a - radius, 0, seq - span), radius)
            q = q_ref[pl.ds(a, blk), :]
            zero = jnp.zeros_like(q)
            qm = jnp.concatenate([jnp.where(lo_q, q, zero), jnp.where(lo_q, zero, q)], axis=0)
            k = k_ref[pl.ds(ks, span), :]
            scores.append(lax.dot_general(qm, k, (((1,), (1,)), ((), ())),
                                          preferred_element_type=F32))
            starts.append(a)
            variants.append((a - ks) // radius)
            values.append(v_ref[pl.ds(ks, span), :])
        probs, sink_terms = [], []
        for u in range(group):
            s = scores[u] + bias_scr[variants[u]]
            m = jnp.maximum(jnp.max(s, axis=-1, keepdims=True), sink_col)
            probs.append(_exp2_probs(s, m))
            sink_terms.append(jnp.exp2(sink_col - m))
        for u in range(group):
            o, l = _value_and_sum(probs[u], values[u])
            o = o * (1.0 / (l + sink_terms[u]))
            o_ref[pl.ds(starts[u], blk), :] = jnp.where(lo, o[:blk], o[blk:]).astype(o_ref.dtype)
        return carry

    lax.fori_loop(0, seq // (blk * group), body, 0)


def _window_attention(qkv, params, *, q_blk0, k_blk, v_blk, n_pairs, radius, blk):
    b, s, _ = qkv.shape
    span = blk + 2 * radius
    return pl.pallas_call(
        functools.partial(_window_kernel, seq=s, blk=blk, radius=radius),
        grid=(b, n_pairs),
        in_specs=[
            pl.BlockSpec(memory_space=pltpu.SMEM),
            pl.BlockSpec((None, s, LANES), lambda bi, c: (bi, 0, q_blk0 + c)),
            pl.BlockSpec((None, s, LANES), lambda bi, c: (bi, 0, k_blk)),
            pl.BlockSpec((None, s, LANES), lambda bi, c: (bi, 0, v_blk)),
        ],
        out_specs=pl.BlockSpec((None, s, LANES), lambda bi, c: (bi, 0, c)),
        out_shape=jax.ShapeDtypeStruct((b, s, n_pairs * LANES), BF16),
        scratch_shapes=[pltpu.VMEM((3, 2 * blk, span), F32)],
        compiler_params=pltpu.CompilerParams(
            dimension_semantics=("arbitrary", "arbitrary"), vmem_limit_bytes=VMEM_LIMIT),
        name="window_attn",
    )(params, qkv, qkv, qkv)


PERM_CHUNK = 256
PHASES4 = 4
PIECE = PERM_CHUNK // PHASES4


def _stack_heads(q):
    lo = _lo_mask((1, LANES))
    zero = jnp.zeros_like(q)
    return jnp.concatenate([jnp.where(lo, q, zero), jnp.where(lo, zero, q)], axis=0)


def _merge_heads(x, blk):
    x = jnp.broadcast_to(x, (2 * blk, LANES))
    return jnp.where(_lo_mask((blk, LANES)), x[:blk], x[blk:])


def _attend_group(qs, ks, vs, biases):
    scores = [lax.dot_general(_stack_heads(q), k, (((1,), (1,)), ((), ())),
                              preferred_element_type=F32) for q, k in zip(qs, ks)]
    stats = []
    for s, bias in zip(scores, biases):
        s = s + bias
        m = jnp.max(s, axis=-1, keepdims=True)
        stats.append((_exp2_probs(s, m), m))
    outs = []
    for (p, m), v in zip(stats, vs):
        o, l = _value_and_sum(p, v)
        outs.append((o, m, l))
    return outs


def _fill_band_bias(bias_scr, pos_q, pos_k, radius, pos_scale, slope_col):
    for var in range(3):
        dist = jnp.abs(pos_q + var * radius - pos_k)
        dist_f = (dist * pos_scale).astype(F32)
        bias_scr[var] = jnp.where(dist <= radius, -(slope_col * dist_f) * LOG2E, NEG_INF)


def _dilated_kernel(par_ref, q_ref, k_ref, v_ref, o_ref, qp, kp, vp, acc_p, m_p, l_p,
                    bias1, bias4, bias16, *, seq, blk, radius):
    span = blk + 2 * radius
    n_chunk = seq // PERM_CHUNK
    group = ATTN_UNROLL
    c = pl.program_id(1)
    row_lo = lax.broadcasted_iota(jnp.int32, (2 * blk, 1), 0) < blk
    slope_col = jnp.where(row_lo, par_ref[0, 2 * c], par_ref[0, 2 * c + 1])

    def piece_pos(x):
        return PIECE * (x // PIECE) + PHASES4 * (x % 16) + (x % PIECE) // 16

    iq = lax.broadcasted_iota(jnp.int32, (2 * blk, span), 0) % blk
    ik = lax.broadcasted_iota(jnp.int32, (2 * blk, span), 1)
    _fill_band_bias(bias1, iq, ik, radius, 1, slope_col)
    _fill_band_bias(bias4, piece_pos(iq), piece_pos(ik), radius, 4, slope_col)
    _fill_band_bias(bias16, iq, ik, radius, 16, slope_col)

    rr = lax.broadcasted_iota(jnp.int32, (PERM_CHUNK, PERM_CHUNK), 0)
    cc = lax.broadcasted_iota(jnp.int32, (PERM_CHUNK, PERM_CHUNK), 1)
    to_phase = jnp.where(cc == 16 * (rr % 16) + PHASES4 * ((rr // 16) % 4) + rr // PIECE,
                         1.0, 0.0).astype(BF16)
    to_token = jnp.where(cc == PIECE * (rr % PHASES4) + 16 * ((rr // PHASES4) % 4) + rr // 16,
                         1.0, 0.0).astype(BF16)

    def permute_body(ch, carry):
        r0 = pl.multiple_of(ch * PERM_CHUNK, PERM_CHUNK)
        x = jnp.concatenate([q_ref[pl.ds(r0, PERM_CHUNK), :], k_ref[pl.ds(r0, PERM_CHUNK), :],
                             v_ref[pl.ds(r0, PERM_CHUNK), :]], axis=1)
        y = jnp.dot(to_phase, x, preferred_element_type=F32).astype(BF16)
        for p4 in range(PHASES4):
            rows = slice(PIECE * p4, PIECE * (p4 + 1))
            qp[p4, ch] = y[rows, 0:LANES]
            kp[p4, ch] = y[rows, LANES:2 * LANES]
            vp[p4, ch] = y[rows, 2 * LANES:3 * LANES]
        return carry

    lax.fori_loop(0, n_chunk, permute_body, 0)

    pieces_per_blk = blk // PIECE
    n_blk4 = seq // (PHASES4 * blk)

    def dil4_body(g, carry):
        p4 = g // (n_blk4 // group)
        j0 = (g % (n_blk4 // group)) * group
        qs, ks, vs, bs = [], [], [], []
        for u in range(group):
            pc = pieces_per_blk * (j0 + u)
            cs = jnp.clip(pc - radius // PIECE, 0, n_chunk - span // PIECE)
            qs.append(qp[p4, pl.ds(pc, pieces_per_blk)].reshape(blk, LANES))
            ks.append(kp[p4, pl.ds(cs, span // PIECE)].reshape(span, LANES))
            vs.append(vp[p4, pl.ds(cs, span // PIECE)].reshape(span, LANES))
            bs.append(bias4[(pc - cs) // (radius // PIECE)])
        for u, (o, m, l) in enumerate(_attend_group(qs, ks, vs, bs)):
            dst = (p4, pl.ds(pieces_per_blk * (j0 + u), pieces_per_blk))
            acc_p[dst] = _merge_heads(o, blk).reshape(pieces_per_blk, PIECE, LANES)
            m_p[dst] = _merge_heads(m, blk).reshape(pieces_per_blk, PIECE, LANES)
            l_p[dst] = _merge_heads(l, blk).reshape(pieces_per_blk, PIECE, LANES)
        return carry

    lax.fori_loop(0, PHASES4 * n_blk4 // group, dil4_body, 0)

    chunks_per_blk = blk // 16

    def dil16_body(g, carry):
        p4 = g // 2
        qs, ks, vs, bs, dsts = [], [], [], [], []
        for mm in range(2):
            off = pl.multiple_of((2 * (g % 2) + mm) * 16, 16)
            kk = kp[p4, :, pl.ds(off, 16), :].reshape(span, LANES)
            vv = vp[p4, :, pl.ds(off, 16), :].reshape(span, LANES)
            for jb in range(n_chunk // chunks_per_blk):
                dst = (p4, pl.ds(chunks_per_blk * jb, chunks_per_blk), pl.ds(off, 16), slice(None))
                qs.append(qp[dst].reshape(blk, LANES))
                ks.append(kk)
                vs.append(vv)
                bs.append(bias16[jb * (blk // radius)])
                dsts.append(dst)
        for dst, (o, m, l) in zip(dsts, _attend_group(qs, ks, vs, bs)):
            m_old = m_p[dst].reshape(blk, LANES)
            m_new = jnp.maximum(m_old, _merge_heads(m, blk))
            w_old = jnp.exp2(m_old - m_new)
            w_new = jnp.exp2(_merge_heads(m, blk) - m_new)
            acc = w_old * acc_p[dst].reshape(blk, LANES) + w_new * _merge_heads(o, blk)
            den = w_old * l_p[dst].reshape(blk, LANES) + w_new * _merge_heads(l, blk)
            acc_p[dst] = acc.reshape(chunks_per_blk, 16, LANES)
            l_p[dst] = den.reshape(chunks_per_blk, 16, LANES)
            m_p[dst] = m_new.reshape(chunks_per_blk, 16, LANES)
        return carry

    lax.fori_loop(0, PHASES4 * 2, dil16_body, 0)

    blk_per_chunk = PERM_CHUNK // blk

    def dil1_body(g, carry):
        merged = []
        for cc2 in range(group // blk_per_chunk):
            ch = g * (group // blk_per_chunk) + cc2
            acc = jnp.concatenate([acc_p[p4, ch] for p4 in range(PHASES4)], axis=0)
            den = jnp.concatenate([l_p[p4, ch] for p4 in range(PHASES4)], axis=0)
            mx = jnp.concatenate([m_p[p4, ch] for p4 in range(PHASES4)], axis=0)
            o_ph = (acc * (1.0 / den)).astype(BF16)
            lse = mx + jnp.log2(den)
            hi = lse.astype(BF16)
            rem = lse - hi.astype(F32)
            mid = rem.astype(BF16)
            low = (rem - mid.astype(F32)).astype(BF16)
            back = jnp.dot(to_token, jnp.concatenate([o_ph, hi, mid, low], axis=1),
                           preferred_element_type=F32)
            merged.append((back[:, 0:LANES], back[:, LANES:2 * LANES]
                           + back[:, 2 * LANES:3 * LANES] + back[:, 3 * LANES:4 * LANES]))
        qs, ks, vs, bs, starts = [], [], [], [], []
        for u in range(group):
            a = pl.multiple_of((g * group + u) * blk, blk)
            kst = pl.multiple_of(jnp.clip(a - radius, 0, seq - span), radius)
            qs.append(q_ref[pl.ds(a, blk), :])
            ks.append(k_ref[pl.ds(kst, span), :])
            vs.append(v_ref[pl.ds(kst, span), :])
            bs.append(bias1[(a - kst) // radius])
            starts.append(a)
        for u, (o, m, l) in enumerate(_attend_group(qs, ks, vs, bs)):
            o1 = _merge_heads(o * (1.0 / l), blk)
            lse1 = _merge_heads(m + jnp.log2(l), blk)
            o2, lse2 = merged[u // blk_per_chunk]
            half = slice((u % blk_per_chunk) * blk, (u % blk_per_chunk + 1) * blk)
            o2, lse2 = o2[half], lse2[half]
            top = jnp.maximum(lse1, lse2)
            e1 = jnp.exp2(lse1 - top)
            e2 = jnp.exp2(lse2 - top)
            o_ref[pl.ds(starts[u], blk), :] = ((e1 * o1 + e2 * o2)
                                               * (1.0 / (e1 + e2))).astype(o_ref.dtype)
        return carry

    lax.fori_loop(0, seq // (blk * group), dil1_body, 0)


def _dilated_attention(qkv, params, *, k_blk0, v_blk0, n_pairs, blk, radius):
    b, s, _ = qkv.shape
    span = blk + 2 * radius
    n_chunk = s // PERM_CHUNK
    assert s // 16 == span and s % PERM_CHUNK == 0 and blk % PIECE == 0
    phase = lambda dt: pltpu.VMEM((PHASES4, n_chunk, PIECE, LANES), dt)
    return pl.pallas_call(
        functools.partial(_dilated_kernel, seq=s, blk=blk, radius=radius),
        grid=(b, n_pairs),
        in_specs=[
            pl.BlockSpec(memory_space=pltpu.SMEM),
            pl.BlockSpec((None, s, LANES), lambda bi, c: (bi, 0, c)),
            pl.BlockSpec((None, s, LANES), lambda bi, c: (bi, 0, k_blk0 + c)),
            pl.BlockSpec((None, s, LANES), lambda bi, c: (bi, 0, v_blk0 + c)),
        ],
        out_specs=pl.BlockSpec((None, s, LANES), lambda bi, c: (bi, 0, c)),
        out_shape=jax.ShapeDtypeStruct((b, s, n_pairs * LANES), BF16),
        scratch_shapes=[phase(BF16), phase(BF16), phase(BF16), phase(F32), phase(F32), phase(F32)]
        + [pltpu.VMEM((3, 2 * blk, span), F32)] * 3,
        compiler_params=pltpu.CompilerParams(
            dimension_semantics=("arbitrary", "arbitrary"), vmem_limit_bytes=VMEM_LIMIT),
        name="dilated_attn",
    )(params, qkv, qkv, qkv)


def _na_kernel(q_ref, k_ref, v_ref, bias_ref, o_ref, *, rows):
    kh = NA_ROWS
    group = ATTN_UNROLL
    lo_q = _lo_mask((1, LANES))
    lo = _lo_mask((GRID_W, LANES))

    def body(g, carry):
        starts, offs, values, scores = [], [], [], []
        for u in range(group):
            i = g * group + u
            rs = jnp.clip(i - kh // 2, 0, rows - kh)
            qs = pl.multiple_of(i * GRID_W, GRID_W)
            ks = pl.multiple_of(rs * GRID_W, GRID_W)
            q = q_ref[pl.ds(qs, GRID_W), :]
            zero = jnp.zeros_like(q)
            qm = jnp.concatenate([jnp.where(lo_q, q, zero), jnp.where(lo_q, zero, q)], axis=0)
            k = k_ref[pl.ds(ks, kh * GRID_W), :]
            scores.append(lax.dot_general(qm, k, (((1,), (1,)), ((), ())),
                                          preferred_element_type=F32))
            starts.append(qs)
            offs.append(rs - i + (NA_ROWS - 1))
            values.append(v_ref[pl.ds(ks, kh * GRID_W), :])
        probs = []
        for u in range(group):
            s = scores[u] + jnp.concatenate(
                [bias_ref[offs[u] + 2 * j] for j in range(kh // 2)], axis=1)
            probs.append(_exp2_probs(s, jnp.max(s, axis=-1, keepdims=True)))
        for u in range(group):
            o, l = _value_and_sum(probs[u], values[u])
            o = o * (1.0 / l)
            o_ref[pl.ds(starts[u], GRID_W), :] = jnp.where(
                lo, o[:GRID_W], o[GRID_W:]).astype(o_ref.dtype)
        return carry

    lax.fori_loop(0, rows // group, body, 0)


def _na_bias_table(rpb):
    qc = np.arange(GRID_W)[:, None]
    kc = np.arange(GRID_W)[None, :]
    qstart = np.clip(qc - NA_COLS // 2, 0, GRID_W - NA_COLS)
    valid = (kc >= qstart) & (kc < qstart + NA_COLS)
    coff = np.clip(kc - qc, -(NA_COLS - 1), NA_COLS - 1) + NA_COLS - 1
    pick = (coff[:, :, None] == np.arange(2 * NA_COLS - 1)).astype(np.float32)
    t = jnp.einsum("hdj,qkj->hdqk", rpb.astype(F32), jnp.asarray(pick),
                   precision=lax.Precision.HIGHEST)
    t = jnp.where(jnp.asarray(valid), t * LOG2E, NEG_INF)
    t = jnp.concatenate([t[:, :-1], t[:, 1:]], axis=-1)
    n_off = 2 * NA_ROWS - 2
    t = t.reshape(C_HEADS // 2, 2, n_off, GRID_W, 2 * GRID_W).transpose(0, 2, 1, 3, 4)
    return t.reshape(C_HEADS // 2, n_off, 2 * GRID_W, 2 * GRID_W)


def _na_attention(qkv, bias):
    b, s, n_cols = qkv.shape
    n_pairs = C_HEADS // 2
    rows = s // GRID_W
    return pl.pallas_call(
        functools.partial(_na_kernel, rows=rows),
        grid=(b, n_pairs),
        in_specs=[
            pl.BlockSpec((None, s, LANES), lambda bi, c: (bi, 0, c)),
            pl.BlockSpec((None, s, LANES), lambda bi, c: (bi, 0, n_pairs + c)),
            pl.BlockSpec((None, s, LANES), lambda bi, c: (bi, 0, 2 * n_pairs + c)),
            pl.BlockSpec((None,) + bias.shape[1:], lambda bi, c: (c, 0, 0, 0)),
        ],
        out_specs=pl.BlockSpec((None, s, LANES), lambda bi, c: (bi, 0, c)),
        out_shape=jax.ShapeDtypeStruct((b, s, n_pairs * LANES), BF16),
        compiler_params=pltpu.CompilerParams(
            dimension_semantics=("arbitrary", "arbitrary"), vmem_limit_bytes=VMEM_LIMIT),
        name="na_attn",
    )(qkv, qkv, qkv, bias)


def _silu(g):
    return g * (1.0 / (1.0 + jnp.exp(-g)))


def _ffn_kernel(x_ref, oa_ref, ob_ref, wa_ref, wb_ref, g_ref, wg_ref, wu_ref, wd_ref,
                c0_ref, c1_ref, c2_ref, y_ref, n0_ref, n1_ref, n2_ref, a_scr):
    n0_ref[...] = c0_ref[...].astype(n0_ref.dtype)
    n1_ref[...] = c1_ref[...].astype(n1_ref.dtype)
    n2_ref[...] = c2_ref[...].astype(n2_ref.dtype)
    x = (x_ref[...] + jnp.dot(oa_ref[...], wa_ref[...], preferred_element_type=F32)
         + jnp.dot(ob_ref[...], wb_ref[...], preferred_element_type=F32))
    h = _rms_rows(x, g_ref[...]).astype(BF16)
    f = wg_ref.shape[1]
    for j in range(f // MXU_N):
        sl = slice(j * MXU_N, (j + 1) * MXU_N)
        gate = jnp.dot(h, wg_ref[:, sl], preferred_element_type=F32)
        up = jnp.dot(h, wu_ref[:, sl], preferred_element_type=F32)
        a_scr[:, sl] = (_silu(gate) * up).astype(BF16)
    y_ref[...] = x + jnp.dot(a_scr[...], wd_ref[...], preferred_element_type=F32)


def _ffn(x, oa, ob, wa, wb, g, wg, wu, wd, narrow, tm=256):
    t, d = x.shape
    f = wg.shape[1]
    steps = t // tm
    const = lambda a: pl.BlockSpec(a.shape, lambda i: (0, 0), pipeline_mode=pl.Buffered(1))
    row = lambda w: pl.BlockSpec((tm, w), lambda i: (i, 0))
    slab = lambda a: pl.BlockSpec((a.shape[0] // steps, a.shape[1]), lambda i: (i, 0))
    assert all(a.shape[0] % (16 * steps) == 0 for a in narrow)
    return pl.pallas_call(
        _ffn_kernel,
        grid=(steps,),
        in_specs=[row(d), row(oa.shape[1]), row(ob.shape[1]), const(wa), const(wb),
                  pl.BlockSpec((1, d), lambda i: (0, 0)),
                  const(wg), const(wu), const(wd)] + [slab(a) for a in narrow],
        out_specs=[pl.BlockSpec((tm, d), lambda i: (i, 0))] + [slab(a) for a in narrow],
        out_shape=[jax.ShapeDtypeStruct((t, d), F32)]
        + [jax.ShapeDtypeStruct(a.shape, BF16) for a in narrow],
        scratch_shapes=[pltpu.VMEM((tm, f), BF16)],
        compiler_params=pltpu.CompilerParams(
            dimension_semantics=("arbitrary",), vmem_limit_bytes=VMEM_LIMIT),
        name="dense_swiglu",
    )(x, oa, ob, wa, wb, g.reshape(1, d), wg, wu, wd, *narrow)


TOP_K = 2
MOE_TB = 512
MOE_TM = 512
MOE_RC = 256
KEY_STRIDE = 65536.0


def _router_kernel(x_ref, o_ref, wo_ref, g_ref, wr_hi_ref, wr_lo_ref, x1_ref, h_ref, info_ref,
                   info_t_ref, cnt_ref, carry_scr, ltri_scr):
    i = pl.program_id(0)
    tb = x_ref.shape[0]
    lane = lax.broadcasted_iota(jnp.int32, (1, LANES), 1)

    @pl.when(i == 0)
    def _():
        carry_scr[...] = jnp.zeros_like(carry_scr)
        r = lax.broadcasted_iota(jnp.int32, (tb, tb), 0)
        cidx = lax.broadcasted_iota(jnp.int32, (tb, tb), 1)
        ltri_scr[...] = jnp.where(cidx < r, 1.0, 0.0).astype(BF16)

    x1 = x_ref[...] + jnp.dot(o_ref[...], wo_ref[...], preferred_element_type=F32)
    x1_ref[...] = x1
    h = _rms_rows(x1, g_ref[...])
    h_hi = h.astype(BF16)
    h_ref[...] = h_hi
    h_lo = (h - h_hi.astype(F32)).astype(BF16)
    logits = (jnp.dot(h_hi, wr_hi_ref[...], preferred_element_type=F32)
              + jnp.dot(h_lo, wr_hi_ref[...], preferred_element_type=F32)
              + jnp.dot(h_hi, wr_lo_ref[...], preferred_element_type=F32))
    logits = jnp.where(lane < N_EXPERTS, logits, NEG_INF)
    v1 = jnp.max(logits, axis=-1, keepdims=True)
    i1 = jnp.min(jnp.where(logits == v1, lane, LANES), axis=-1, keepdims=True)
    rest = jnp.where(lane == i1, NEG_INF, logits)
    v2 = jnp.max(rest, axis=-1, keepdims=True)
    i2 = jnp.min(jnp.where(rest == v2, lane, LANES), axis=-1, keepdims=True)
    e2 = jnp.exp(v2 - v1)
    g1 = 1.0 / (1.0 + e2)
    g2 = e2 * g1

    pick1 = lane == i1
    pick2 = lane == i2
    picked = jnp.where(pick1, 1.0, jnp.where(pick2, 1.0, 0.0))
    before = (jnp.dot(ltri_scr[...], picked.astype(BF16), preferred_element_type=F32)
              + carry_scr[...])
    rank1 = jnp.sum(jnp.where(pick1, before, 0.0), axis=-1, keepdims=True)
    rank2 = jnp.sum(jnp.where(pick2, before, 0.0), axis=-1, keepdims=True)
    key1 = i1.astype(F32) * KEY_STRIDE + rank1
    key2 = i2.astype(F32) * KEY_STRIDE + rank2
    info = jnp.where(lane == 0, key1, jnp.where(lane == 1, key2,
                     jnp.where(lane == 2, g1, jnp.where(lane == 3, g2, 0.0))))
    info_ref[...] = info
    info_t_ref[...] = info.T[:8, :]
    counts = jnp.sum(picked, axis=0, keepdims=True)
    cnt_ref[...] = jnp.broadcast_to(counts, cnt_ref.shape)
    carry_scr[...] += counts


def _router(x, o, wo, g, wr):
    t, d = x.shape
    nb = t // MOE_TB
    wr_hi = wr.astype(BF16)
    wr_lo = (wr - wr_hi.astype(F32)).astype(BF16)
    row = lambda w: pl.BlockSpec((MOE_TB, w), lambda i: (i, 0))
    const = lambda a: pl.BlockSpec(a.shape, lambda i: (0, 0))
    return pl.pallas_call(
        _router_kernel,
        grid=(nb,),
        in_specs=[row(d), row(o.shape[1]), const(wo), pl.BlockSpec((1, d), lambda i: (0, 0)),
                  const(wr_hi), const(wr_lo)],
        out_specs=[row(d), row(d), row(LANES),
                   pl.BlockSpec((None, 8, MOE_TB), lambda i: (i, 0, 0)),
                   pl.BlockSpec((None, 8, LANES), lambda i: (i, 0, 0))],
        out_shape=[jax.ShapeDtypeStruct((t, d), F32),
                   jax.ShapeDtypeStruct((t, d), BF16),
                   jax.ShapeDtypeStruct((t, LANES), F32),
                   jax.ShapeDtypeStruct((nb, 8, MOE_TB), F32),
                   jax.ShapeDtypeStruct((nb, 8, LANES), F32)],
        scratch_shapes=[pltpu.VMEM((1, LANES), F32), pltpu.VMEM((MOE_TB, MOE_TB), BF16)],
        compiler_params=pltpu.CompilerParams(
            dimension_semantics=("arbitrary",), vmem_limit_bytes=VMEM_LIMIT),
        name="moe_router",
    )(x, o, wo, g.reshape(1, d), wr_hi, wr_lo)


def _moe_plan(block_counts, n_tokens):
    nb = block_counts.shape[0]
    n_tiles = (TOP_K * n_tokens + N_EXPERTS * (MOE_TM - 1)) // MOE_TM
    n_chunks = n_tiles * MOE_TM // MOE_RC
    tc = block_counts.astype(jnp.int32)
    cb = jnp.concatenate([jnp.zeros((1, N_EXPERTS), jnp.int32), jnp.cumsum(tc, axis=0)])
    counts = cb[-1]
    padded = ((counts + MOE_TM - 1) // MOE_TM) * MOE_TM
    ends = jnp.cumsum(padded)
    starts = ends - padded
    total = ends[-1]

    tiles = jnp.arange(n_tiles, dtype=jnp.int32)
    n_valid_tiles = total // MOE_TM
    t_valid = (tiles < n_valid_tiles).astype(jnp.int32)
    t_src = jnp.maximum(jnp.minimum(tiles, n_valid_tiles - 1), 0)
    t_exp = jnp.minimum(jnp.sum((t_src * MOE_TM)[:, None] >= ends[None, :], axis=1),
                        N_EXPERTS - 1).astype(jnp.int32)

    r0 = jnp.arange(n_chunks, dtype=jnp.int32) * MOE_RC
    ex = jnp.minimum(jnp.sum(r0[:, None] >= ends[None, :], axis=1), N_EXPERTS - 1)
    live = r0 < total
    lo = r0 - starts[ex]
    base = (ex * int(KEY_STRIDE) + lo).astype(jnp.int32)
    cbe = cb[:, ex]
    g_lo = jnp.sum(cbe[1:] <= lo[None, :], axis=0).astype(jnp.int32)
    g_cnt = jnp.sum(live[None, :] & (cbe[:-1] < (lo + MOE_RC)[None, :]) & (cbe[1:] > lo[None, :]),
                    axis=0).astype(jnp.int32)

    row_lo = starts[None, :] + cb[:-1]
    row_hi = starts[None, :] + cb[1:]
    c_first = row_lo // MOE_RC
    c_num = jnp.where(row_hi > row_lo, (row_hi - 1) // MOE_RC - c_first + 1, 0)
    c_end = jnp.cumsum(c_num, axis=1)
    slots = jnp.arange(COMBINE_MAX_CHUNKS, dtype=jnp.int32)
    e_of = jnp.minimum(jnp.sum(slots[None, :, None] >= c_end[:, None, :], axis=2),
                       N_EXPERTS - 1)
    before = jnp.take_along_axis(c_end - c_num, e_of, axis=1)
    c_c = jnp.take_along_axis(c_first, e_of, axis=1) + slots[None, :] - before
    c_cnt = c_end[:, -1].astype(jnp.int32)
    c_c = jnp.clip(c_c, 0, n_chunks - 1).astype(jnp.int32).reshape(-1)
    return dict(n_tiles=n_tiles, n_chunks=n_chunks, t_valid=t_valid, t_exp=t_exp,
                base=base, live=live.astype(jnp.int32), g_lo=g_lo, g_cnt=g_cnt,
                c_cnt=c_cnt, c_c=c_c)


def _gather_kernel(live_ref, lo_ref, cnt_ref, base_ref, info_t_ref, h_ref,
                   xs_ref, gs_ref, acc_scr, gsum_scr):
    c = pl.program_id(0)
    tb = info_t_ref.shape[-1]

    @pl.when(live_ref[c] == 1)
    def _():
        first = lo_ref[c]
        n = cnt_ref[c]
        row = lax.broadcasted_iota(jnp.int32, (MOE_RC, tb), 0).astype(F32)
        acc_scr[...] = jnp.zeros_like(acc_scr)
        gsum_scr[...] = jnp.zeros_like(gsum_scr)

        def two_blocks(it, carry):
            rows, gate = None, None
            for w in range(2):
                j = 2 * it + w
                b = first + jnp.minimum(j, n - 1)
                base = jnp.where(j < n, base_ref[c], NO_MATCH_BASE).astype(F32)
                info = info_t_ref[b]
                m1 = (info[0:1, :] - base) == row
                m2 = (info[1:2, :] - base) == row
                sel = jnp.where(m1, 1.0, jnp.where(m2, 1.0, 0.0)).astype(BF16)
                g = jnp.sum(jnp.where(m1, info[2:3, :], jnp.where(m2, info[3:4, :], 0.0)),
                            axis=-1, keepdims=True)
                r = jnp.dot(sel, h_ref[pl.ds(pl.multiple_of(b * tb, tb), tb), :],
                            preferred_element_type=F32)
                rows = r if rows is None else rows + r
                gate = g if gate is None else gate + g
            acc_scr[...] += rows
            gsum_scr[...] += gate
            return carry

        lax.fori_loop(0, (n + 1) // 2, two_blocks, 0)
        xs_ref[...] = acc_scr[...].astype(xs_ref.dtype)
        gs_ref[...] = jnp.broadcast_to(gsum_scr[...], gs_ref.shape)

    @pl.when(live_ref[c] == 0)
    def _():
        xs_ref[...] = jnp.zeros_like(xs_ref)
        gs_ref[...] = jnp.zeros_like(gs_ref)


def _moe_gather(plan, info_t, h):
    t, d = h.shape
    rows = plan["n_chunks"] * MOE_RC
    chunk = lambda c, live, lo, cn, ba: (c, 0)
    grid_spec = pltpu.PrefetchScalarGridSpec(
        num_scalar_prefetch=4,
        grid=(plan["n_chunks"],),
        in_specs=[pl.BlockSpec(info_t.shape, lambda c, live, lo, cn, ba: (0, 0, 0),
                               pipeline_mode=pl.Buffered(1)),
                  pl.BlockSpec((t, d), lambda c, live, lo, cn, ba: (0, 0),
                               pipeline_mode=pl.Buffered(1))],
        out_specs=[pl.BlockSpec((MOE_RC, d), chunk), pl.BlockSpec((MOE_RC, LANES), chunk)],
        scratch_shapes=[pltpu.VMEM((MOE_RC, d), F32), pltpu.VMEM((MOE_RC, 1), F32)],
    )
    return pl.pallas_call(
        _gather_kernel,
        grid_spec=grid_spec,
        out_shape=[jax.ShapeDtypeStruct((rows, d), BF16),
                   jax.ShapeDtypeStruct((rows, LANES), F32)],
        compiler_params=pltpu.CompilerParams(
            dimension_semantics=("arbitrary",), vmem_limit_bytes=VMEM_LIMIT),
        name="moe_gather",
    )(plan["live"], plan["g_lo"], plan["g_cnt"], plan["base"], info_t, h)


def _experts_kernel(exp_ref, val_ref, xs_ref, gs_ref, wg_ref, wu_ref, wd_ref, ys_ref,
                    acc_scr):
    i = pl.program_id(0)
    j = pl.program_id(1)

    last = j == pl.num_programs(1) - 1

    @pl.when(j == 0)
    def _():
        acc_scr[...] = jnp.zeros_like(acc_scr)

    @pl.when(val_ref[i] == 1)
    def _():
        x = xs_ref[...]
        gate = jnp.dot(x, wg_ref[...], preferred_element_type=F32)
        up = jnp.dot(x, wu_ref[...], preferred_element_type=F32)
        a = (_silu(gate) * up).astype(BF16)
        acc_scr[...] += jnp.dot(a, wd_ref[...], preferred_element_type=F32)

    @pl.when(last)
    def _():
        ys_ref[...] = (acc_scr[...] * gs_ref[:, 0:1]).astype(ys_ref.dtype)


def _moe_experts(plan, xs, gs, wg, wu, wd, tf):
    rows, d = xs.shape
    f = wg.shape[2]
    nf = f // tf

    def wcol(i, j, ex, val):
        return (ex[i], 0, jnp.where(val[i] == 1, j, nf - 1))

    def wrow(i, j, ex, val):
        return (ex[i], jnp.where(val[i] == 1, j, nf - 1), 0)

    tile = lambda i, j, ex, val: (i, 0)
    grid_spec = pltpu.PrefetchScalarGridSpec(
        num_scalar_prefetch=2,
        grid=(plan["n_tiles"], nf),
        in_specs=[pl.BlockSpec((MOE_TM, d), tile),
                  pl.BlockSpec((MOE_TM, LANES), tile),
                  pl.BlockSpec((None, d, tf), wcol),
                  pl.BlockSpec((None, d, tf), wcol),
                  pl.BlockSpec((None, tf, d), wrow)],
        out_specs=pl.BlockSpec((MOE_TM, d), tile),
        scratch_shapes=[pltpu.VMEM((MOE_TM, d), F32)],
    )
    return pl.pallas_call(
        _experts_kernel,
        grid_spec=grid_spec,
        out_shape=jax.ShapeDtypeStruct((rows, d), BF16),
        compiler_params=pltpu.CompilerParams(
            dimension_semantics=("arbitrary", "arbitrary"), vmem_limit_bytes=VMEM_LIMIT),
        name="moe_experts",
    )(plan["t_exp"], plan["t_valid"], xs, gs, wg, wu, wd)


NO_MATCH_BASE = -(2 ** 30)


COMBINE_GROUP = 4
COMBINE_MAX_CHUNKS = N_EXPERTS * (MOE_TB // MOE_RC + 1)


def _combine_kernel(cnt_ref, pc_ref, base_ref, info_ref, x_ref, ys_hbm, o_ref, ybuf, sem):
    b = pl.program_id(0)
    tb = x_ref.shape[0]
    d = x_ref.shape[1]

    def n_fetch(blk):
        return ((cnt_ref[blk] + COMBINE_GROUP - 1) // COMBINE_GROUP) * COMBINE_GROUP

    def chunk_of(blk, j):
        return pc_ref[blk * COMBINE_MAX_CHUNKS + jnp.minimum(j, cnt_ref[blk] - 1)]

    def copy(blk, j, slot):
        row0 = pl.multiple_of(chunk_of(blk, j) * MOE_RC, MOE_RC)
        return pltpu.make_async_copy(ys_hbm.at[pl.ds(row0, MOE_RC), :], ybuf.at[slot, j],
                                     sem.at[slot])

    def start_block(blk, slot):
        def issue(j, carry):
            copy(blk, j, slot).start()
            return carry
        lax.fori_loop(0, n_fetch(blk), issue, 0)

    @pl.when(b == 0)
    def _():
        start_block(0, 0)

    @pl.when(b + 1 < pl.num_programs(0))
    def _():
        start_block(b + 1, (b + 1) % 2)

    slot = b % 2
    n = cnt_ref[b]

    def drain(j, carry):
        copy(b, j, slot).wait()
        return carry

    lax.fori_loop(0, n_fetch(b), drain, 0)

    lane = lax.broadcasted_iota(jnp.int32, (1, LANES), 1)
    info = info_ref[...]
    k1 = jnp.sum(jnp.where(lane == 0, info, 0.0), axis=-1, keepdims=True)
    k2 = jnp.sum(jnp.where(lane == 1, info, 0.0), axis=-1, keepdims=True)
    col = lax.broadcasted_iota(jnp.int32, (tb, MOE_RC), 1).astype(F32)
    o_ref[...] = x_ref[...]

    def group(gi, carry):
        sels = []
        for w in range(COMBINE_GROUP):
            j = gi * COMBINE_GROUP + w
            base = jnp.where(j < n, base_ref[chunk_of(b, j)], NO_MATCH_BASE).astype(F32)
            sels.append(jnp.where(k1 - base == col, 1.0,
                                  jnp.where(k2 - base == col, 1.0, 0.0)).astype(BF16))
        rows = ybuf[slot, pl.ds(gi * COMBINE_GROUP, COMBINE_GROUP)]
        o_ref[...] += jnp.dot(jnp.concatenate(sels, axis=1),
                              rows.reshape(COMBINE_GROUP * MOE_RC, d),
                              preferred_element_type=F32)
        return carry

    lax.fori_loop(0, n_fetch(b) // COMBINE_GROUP, group, 0)


def _moe_combine(plan, info, ys, x):
    t, d = x.shape
    block = lambda b, cn, pc, ba: (b, 0)
    grid_spec = pltpu.PrefetchScalarGridSpec(
        num_scalar_prefetch=3,
        grid=(t // MOE_TB,),
        in_specs=[pl.BlockSpec((MOE_TB, LANES), block),
                  pl.BlockSpec((MOE_TB, d), block),
                  pl.BlockSpec(memory_space=pl.ANY)],
        out_specs=pl.BlockSpec((MOE_TB, d), block),
        scratch_shapes=[pltpu.VMEM((2, COMBINE_MAX_CHUNKS, MOE_RC, d), BF16),
                        pltpu.SemaphoreType.DMA((2,))],
    )
    return pl.pallas_call(
        _combine_kernel,
        grid_spec=grid_spec,
        out_shape=jax.ShapeDtypeStruct((t, d), F32),
        compiler_params=pltpu.CompilerParams(
            dimension_semantics=("arbitrary",), vmem_limit_bytes=VMEM_LIMIT),
        name="moe_combine",
    )(plan["c_cnt"], plan["c_c"], plan["base"], info, x, ys)


def _moe(x, o, wo, g, wr, wg, wu, wd, tf=1792):
    t, _ = x.shape
    x1, h, info, info_t, block_counts = _router(x, o, wo, g, wr)
    plan = _moe_plan(block_counts[:, 0, :N_EXPERTS], t)
    xs, gs = _moe_gather(plan, info_t, h)
    ys = _moe_experts(plan, xs, gs, wg, wu, wd, tf)
    return _moe_combine(plan, info, ys, x1)


def _alibi_slopes():
    s = np.exp2(-8.0 * np.arange(1, N_HEADS + 1) / N_HEADS).astype(np.float32)
    return s[0::2][:A_HEADS], s[1::2][:B_Q_HEADS]


_B_HEAD_ORDER = np.array([0, 4, 1, 5, 2, 6, 3, 7])


def _even_layer(x2d, batch, norm1, w_in, qn_a, kn_a, qn_b, kn_b, sink_b, w_out,
                norm2, w_gate, w_up, w_down, narrow):
    t, d = x2d.shape
    s = t // batch
    wa = A_HEADS * HEAD_DIM
    wq_b = B_Q_HEADS * HEAD_DIM
    wkv_b = B_KV_HEADS * HEAD_DIM
    qb0 = 3 * wa
    q_scale = HEAD_DIM ** -0.5 * LOG2E

    w_qb = w_in[:, qb0:qb0 + wq_b].reshape(d, B_Q_HEADS, HEAD_DIM)[:, _B_HEAD_ORDER]
    w_in_p = jnp.concatenate(
        [w_in[:, :qb0], w_qb.reshape(d, wq_b), w_in[:, qb0 + wq_b:]], axis=1).astype(BF16)
    ones = jnp.ones((HEAD_DIM,), F32)
    col_gain = jnp.concatenate([
        jnp.tile(qn_a * q_scale, A_HEADS), jnp.tile(kn_a, A_HEADS), jnp.tile(ones, A_HEADS),
        jnp.tile(qn_b * q_scale, B_Q_HEADS), jnp.tile(kn_b, B_KV_HEADS),
        jnp.tile(ones, B_KV_HEADS)])
    cpl = wa // LANES
    norm_chunks = ((True,) * (2 * cpl) + (False,) * cpl + (True,) * (wq_b // LANES)
                   + (True,) * (wkv_b // LANES) + (False,) * (wkv_b // LANES))
    qkv = _norm_proj(x2d, norm1, w_in_p, col_gain, norm_chunks)
    n_cols = qkv.shape[1]
    qkv3 = qkv.reshape(batch, s, n_cols)

    slopes_a, slopes_b = _alibi_slopes()
    par_a = jnp.asarray(slopes_a).reshape(1, A_HEADS)
    radii = {window // (2 * dil) for window, dil in DIL_CONFIGS}
    assert tuple(dil for _, dil in DIL_CONFIGS) == (1, 4, 16) and len(radii) == 1
    oa = _dilated_attention(qkv3, par_a, k_blk0=cpl, v_blk0=2 * cpl, n_pairs=cpl,
                            blk=A_BLOCK, radius=radii.pop())
    par_b = jnp.stack([jnp.asarray(slopes_b[_B_HEAD_ORDER]),
                       sink_b.astype(F32)[_B_HEAD_ORDER] * LOG2E])
    kb = 3 * cpl + wq_b // LANES
    ob = _window_attention(qkv3, par_b, q_blk0=3 * cpl, k_blk=kb, v_blk=kb + wkv_b // LANES,
                           n_pairs=wq_b // LANES, radius=B_WINDOW, blk=B_BLOCK)
    w_out_a = w_out[:wa].astype(BF16)
    w_out_b = w_out[wa:].reshape(B_Q_HEADS, HEAD_DIM, d)[_B_HEAD_ORDER].reshape(wq_b, d)
    y, *narrowed = _ffn(x2d, oa.reshape(t, wa), ob.reshape(t, wq_b), w_out_a,
                        w_out_b.astype(BF16), norm2, w_gate.astype(BF16), w_up.astype(BF16),
                        w_down.astype(BF16), narrow)
    return y, narrowed


def _odd_layer(x2d, batch, norm1, w_qkv, qn, kn, rpb, w_out, norm2, w_router,
               w_gate, w_up, w_down):
    t, d = x2d.shape
    s = t // batch
    wc = C_HEADS * HEAD_DIM
    ones = jnp.ones((HEAD_DIM,), F32)
    col_gain = jnp.concatenate([jnp.tile(qn * (HEAD_DIM ** -0.5 * LOG2E), C_HEADS),
                                jnp.tile(kn, C_HEADS), jnp.tile(ones, C_HEADS)])
    cpl = wc // LANES
    norm_chunks = (True,) * (2 * cpl) + (False,) * cpl
    qkv = _norm_proj(x2d, norm1, w_qkv.astype(BF16), col_gain, norm_chunks)
    o = _na_attention(qkv.reshape(batch, s, 3 * wc), _na_bias_table(rpb))
    wr = jnp.zeros((d, LANES), F32).at[:, :N_EXPERTS].set(w_router.astype(F32))
    return _moe(x2d, o.reshape(t, wc), w_out.astype(BF16), norm2, wr, w_gate, w_up, w_down)


def kernel(x, ev_norm1, ev_w_in, ev_qn_a, ev_kn_a, ev_qn_b, ev_kn_b, ev_sink_b, ev_w_out, ev_norm2, ev_ffn_gate, ev_ffn_up, ev_ffn_down, od_norm1, od_w_qkv, od_qn, od_kn, od_rpb, od_w_out, od_norm2, od_router, od_exp_gate, od_exp_up, od_exp_down):
    batch, s, d = x.shape
    depth = ev_norm1.shape[0] + od_norm1.shape[0]
    h = x.reshape(batch * s, d)
    n_e, _, f_e = od_exp_gate.shape[1:]
    experts = None
    for layer in range(depth):
        j = layer // 2
        if layer % 2 == 0:
            if j < od_norm1.shape[0]:
                narrow = [od_exp_gate[j].reshape(n_e * d, f_e), od_exp_up[j].reshape(n_e * d, f_e),
                          od_exp_down[j].reshape(n_e * f_e, d)]
            else:
                narrow = [jnp.zeros((batch * s // 16, LANES), F32)] * 3
            h, narrowed = _even_layer(h, batch, ev_norm1[j], ev_w_in[j], ev_qn_a[j], ev_kn_a[j],
                                      ev_qn_b[j], ev_kn_b[j], ev_sink_b[j], ev_w_out[j],
                                      ev_norm2[j], ev_ffn_gate[j], ev_ffn_up[j], ev_ffn_down[j],
                                      narrow)
            experts = (narrowed[0].reshape(n_e, d, f_e), narrowed[1].reshape(n_e, d, f_e),
                       narrowed[2].reshape(n_e, f_e, d))
        else:
            h = _odd_layer(h, batch, od_norm1[j], od_w_qkv[j], od_qn[j], od_kn[j], od_rpb[j],
                           od_w_out[j], od_norm2[j], od_router[j], *experts)
    return h.reshape(batch, s, d)
```

```python
import functools

import numpy as np
import jax
import jax.numpy as jnp
from jax import lax
from jax.experimental import pallas as pl
from jax.experimental.pallas import tpu as pltpu

D_MODEL = 1024
HEAD_DIM = 64
N_HEADS = D_MODEL // HEAD_DIM
A_HEADS = N_HEADS // 2
B_Q_HEADS = N_HEADS // 2
B_KV_HEADS = max(1, B_Q_HEADS // 4)
C_HEADS = N_HEADS
DIL_CONFIGS = ((128, 1), (512, 4), (2048, 16))
A_BLOCK = 128
B_WINDOW = 128
B_BLOCK = 128
GRID_W = 64
NA_ROWS = 8
NA_COLS = 16
N_EXPERTS = 8
RMS_EPS = 1e-6
NEG_INF = -1e30

LANES = 128
MXU_N = 256
VMEM_LIMIT = 56 * 1024 * 1024

ATTN_UNROLL = 8

F32 = jnp.float32
BF16 = jnp.bfloat16


def _lo_mask(shape):
    return lax.broadcasted_iota(jnp.int32, shape, len(shape) - 1) < HEAD_DIM


LOG2E = 1.4426950408889634


def _exp2_probs(s, m):
    return jnp.exp2(s - m).astype(BF16)


def _value_and_sum(p, v):
    ov = jnp.dot(p, jnp.concatenate([v, jnp.ones_like(v)], axis=1), preferred_element_type=F32)
    return ov[:, :LANES], ov[:, LANES:LANES + 1]


def _rms_rows(x, g):
    ms = jnp.mean(x * x, axis=-1, keepdims=True)
    return x * lax.rsqrt(ms + RMS_EPS) * g


def _norm_proj_kernel(x_ref, g_ref, w_ref, cg_ref, o_ref, *, norm_chunks):
    h = _rms_rows(x_ref[...], g_ref[...]).astype(BF16)
    n_out = o_ref.shape[-1]
    lo = _lo_mask((1, LANES))
    for j in range(n_out // MXU_N):
        y = jnp.dot(h, w_ref[:, j * MXU_N:(j + 1) * MXU_N], preferred_element_type=F32)
        for half in range(MXU_N // LANES):
            c = j * (MXU_N // LANES) + half
            yc = y[:, half * LANES:(half + 1) * LANES]
            if norm_chunks[c]:
                sq = yc * yc
                s_lo = jnp.sum(jnp.where(lo, sq, 0.0), axis=-1, keepdims=True)
                s_hi = jnp.sum(jnp.where(lo, 0.0, sq), axis=-1, keepdims=True)
                inv = jnp.where(lo, lax.rsqrt(s_lo * (1.0 / HEAD_DIM) + RMS_EPS),
                                lax.rsqrt(s_hi * (1.0 / HEAD_DIM) + RMS_EPS))
                yc = yc * inv * cg_ref[:, c * LANES:(c + 1) * LANES]
            o_ref[:, c * LANES:(c + 1) * LANES] = yc.astype(o_ref.dtype)


def _norm_proj(x, g, w, col_gain, norm_chunks, tm=512):
    t, d = x.shape
    n = w.shape[1]
    return pl.pallas_call(
        functools.partial(_norm_proj_kernel, norm_chunks=norm_chunks),
        grid=(t // tm,),
        in_specs=[
            pl.BlockSpec((tm, d), lambda i: (i, 0)),
            pl.BlockSpec((1, d), lambda i: (0, 0)),
            pl.BlockSpec((d, n), lambda i: (0, 0)),
            pl.BlockSpec((1, n), lambda i: (0, 0)),
        ],
        out_specs=pl.BlockSpec((tm, n), lambda i: (i, 0)),
        out_shape=jax.ShapeDtypeStruct((t, n), BF16),
        compiler_params=pltpu.CompilerParams(
            dimension_semantics=("arbitrary",), vmem_limit_bytes=VMEM_LIMIT),
        name="norm_proj",
    )(x, g.reshape(1, d), w, col_gain.reshape(1, n))


def _window_kernel(par_ref, q_ref, k_ref, v_ref, o_ref, bias_scr, *, seq, blk, radius):
    span = blk + 2 * radius
    c = pl.program_id(1)
    group = ATTN_UNROLL
    row_lo = lax.broadcasted_iota(jnp.int32, (2 * blk, 1), 0) < blk
    slope_col = jnp.where(row_lo, par_ref[0, 2 * c], par_ref[0, 2 * c + 1])
    sink_col = jnp.where(row_lo, par_ref[1, 2 * c], par_ref[1, 2 * c + 1])

    iq = lax.broadcasted_iota(jnp.int32, (2 * blk, span), 0) % blk
    ik = lax.broadcasted_iota(jnp.int32, (2 * blk, span), 1)
    _fill_band_bias(bias_scr, iq, ik, radius, 1, slope_col)
    lo_q = _lo_mask((1, LANES))
    lo = _lo_mask((blk, LANES))

    def body(g, carry):
        starts, variants, values, scores = [], [], [], []
        for u in range(group):
            a = pl.multiple_of((g * group + u) * blk, blk)
            ks = pl.multiple_of(jnp.clip(a - radius, 0, seq - span), radius)
            q = q_ref[pl.ds(a, blk), :]
            zero = jnp.zeros_like(q)
            qm = jnp.concatenate([jnp.where(lo_q, q, zero), jnp.where(lo_q, zero, q)], axis=0)
            k = k_ref[pl.ds(ks, span), :]
            scores.append(lax.dot_general(qm, k, (((1,), (1,)), ((), ())),
                                          preferred_element_type=F32))
            starts.append(a)
            variants.append((a - ks) // radius)
            values.append(v_ref[pl.ds(ks, span), :])
        probs, sink_terms = [], []
        for u in range(group):
            s = scores[u] + bias_scr[variants[u]]
            m = jnp.maximum(jnp.max(s, axis=-1, keepdims=True), sink_col)
            probs.append(_exp2_probs(s, m))
            sink_terms.append(jnp.exp2(sink_col - m))
        for u in range(group):
            o, l = _value_and_sum(probs[u], values[u])
            o = o * (1.0 / (l + sink_terms[u]))
            o_ref[pl.ds(starts[u], blk), :] = jnp.where(lo, o[:blk], o[blk:]).astype(o_ref.dtype)
        return carry

    lax.fori_loop(0, seq // (blk * group), body, 0)


def _window_attention(qkv, params, *, q_blk0, k_blk, v_blk, n_pairs, radius, blk):
    b, s, _ = qkv.shape
    span = blk + 2 * radius
    return pl.pallas_call(
        functools.partial(_window_kernel, seq=s, blk=blk, radius=radius),
        grid=(b, n_pairs),
        in_specs=[
            pl.BlockSpec(memory_space=pltpu.SMEM),
            pl.BlockSpec((None, s, LANES), lambda bi, c: (bi, 0, q_blk0 + c)),
            pl.BlockSpec((None, s, LANES), lambda bi, c: (bi, 0, k_blk)),
            pl.BlockSpec((None, s, LANES), lambda bi, c: (bi, 0, v_blk)),
        ],
        out_specs=pl.BlockSpec((None, s, LANES), lambda bi, c: (bi, 0, c)),
        out_shape=jax.ShapeDtypeStruct((b, s, n_pairs * LANES), BF16),
        scratch_shapes=[pltpu.VMEM((3, 2 * blk, span), F32)],
        compiler_params=pltpu.CompilerParams(
            dimension_semantics=("arbitrary", "arbitrary"), vmem_limit_bytes=VMEM_LIMIT),
        name="window_attn",
    )(params, qkv, qkv, qkv)


PERM_CHUNK = 256
PHASES4 = 4
PIECE = PERM_CHUNK // PHASES4


def _stack_heads(q):
    lo = _lo_mask((1, LANES))
    zero = jnp.zeros_like(q)
    return jnp.concatenate([jnp.where(lo, q, zero), jnp.where(lo, zero, q)], axis=0)


def _merge_heads(x, blk):
    x = jnp.broadcast_to(x, (2 * blk, LANES))
    return jnp.where(_lo_mask((blk, LANES)), x[:blk], x[blk:])


def _attend_group(qs, ks, vs, biases):
    scores = [lax.dot_general(_stack_heads(q), k, (((1,), (1,)), ((), ())),
                              preferred_element_type=F32) for q, k in zip(qs, ks)]
    stats = []
    for s, bias in zip(scores, biases):
        s = s + bias
        m = jnp.max(s, axis=-1, keepdims=True)
        stats.append((_exp2_probs(s, m), m))
    outs = []
    for (p, m), v in zip(stats, vs):
        o, l = _value_and_sum(p, v)
        outs.append((o, m, l))
    return outs


def _fill_band_bias(bias_scr, pos_q, pos_k, radius, pos_scale, slope_col):
    for var in range(3):
        dist = jnp.abs(pos_q + var * radius - pos_k)
        dist_f = (dist * pos_scale).astype(F32)
        bias_scr[var] = jnp.where(dist <= radius, -(slope_col * dist_f) * LOG2E, NEG_INF)


def _dilated_kernel(par_ref, q_ref, k_ref, v_ref, o_ref, qp, kp, vp, acc_p, m_p, l_p,
                    bias1, bias4, bias16, *, seq, blk, radius):
    span = blk + 2 * radius
    n_chunk = seq // PERM_CHUNK
    group = ATTN_UNROLL
    c = pl.program_id(1)
    row_lo = lax.broadcasted_iota(jnp.int32, (2 * blk, 1), 0) < blk
    slope_col = jnp.where(row_lo, par_ref[0, 2 * c], par_ref[0, 2 * c + 1])

    def piece_pos(x):
        return PIECE * (x // PIECE) + PHASES4 * (x % 16) + (x % PIECE) // 16

    iq = lax.broadcasted_iota(jnp.int32, (2 * blk, span), 0) % blk
    ik = lax.broadcasted_iota(jnp.int32, (2 * blk, span), 1)
    _fill_band_bias(bias1, iq, ik, radius, 1, slope_col)
    _fill_band_bias(bias4, piece_pos(iq), piece_pos(ik), radius, 4, slope_col)
    _fill_band_bias(bias16, iq, ik, radius, 16, slope_col)

    rr = lax.broadcasted_iota(jnp.int32, (PERM_CHUNK, PERM_CHUNK), 0)
    cc = lax.broadcasted_iota(jnp.int32, (PERM_CHUNK, PERM_CHUNK), 1)
    to_phase = jnp.where(cc == 16 * (rr % 16) + PHASES4 * ((rr // 16) % 4) + rr // PIECE,
                         1.0, 0.0).astype(BF16)
    to_token = jnp.where(cc == PIECE * (rr % PHASES4) + 16 * ((rr // PHASES4) % 4) + rr // 16,
                         1.0, 0.0).astype(BF16)

    def permute_body(ch2, carry):
        for half in range(2):
            ch = 2 * ch2 + half
            r0 = pl.multiple_of(ch * PERM_CHUNK, PERM_CHUNK)
            x = jnp.concatenate([q_ref[pl.ds(r0, PERM_CHUNK), :],
                                 k_ref[pl.ds(r0, PERM_CHUNK), :],
                                 v_ref[pl.ds(r0, PERM_CHUNK), :]], axis=1)
            y = jnp.dot(to_phase, x, preferred_element_type=F32).astype(BF16)
            for p4 in range(PHASES4):
                rows = slice(PIECE * p4, PIECE * (p4 + 1))
                qp[p4, ch] = y[rows, 0:LANES]
                kp[p4, ch] = y[rows, LANES:2 * LANES]
                vp[p4, ch] = y[rows, 2 * LANES:3 * LANES]
        return carry

    lax.fori_loop(0, n_chunk // 2, permute_body, 0)

    pieces_per_blk = blk // PIECE
    n_blk4 = seq // (PHASES4 * blk)

    def dil4_body(g, carry):
        p4 = g // (n_blk4 // group)
        j0 = (g % (n_blk4 // group)) * group
        qs, ks, vs, bs = [], [], [], []
        for u in range(group):
            pc = pieces_per_blk * (j0 + u)
            cs = jnp.clip(pc - radius // PIECE, 0, n_chunk - span // PIECE)
            qs.append(qp[p4, pl.ds(pc, pieces_per_blk)].reshape(blk, LANES))
            ks.append(kp[p4, pl.ds(cs, span // PIECE)].reshape(span, LANES))
            vs.append(vp[p4, pl.ds(cs, span // PIECE)].reshape(span, LANES))
            bs.append(bias4[(pc - cs) // (radius // PIECE)])
        for u, (o, m, l) in enumerate(_attend_group(qs, ks, vs, bs)):
            dst = (p4, pl.ds(pieces_per_blk * (j0 + u), pieces_per_blk))
            acc_p[dst] = _merge_heads(o, blk).reshape(pieces_per_blk, PIECE, LANES)
            m_p[dst] = _merge_heads(m, blk).reshape(pieces_per_blk, PIECE, LANES)
            l_p[dst] = _merge_heads(l, blk).reshape(pieces_per_blk, PIECE, LANES)
        return carry

    lax.fori_loop(0, PHASES4 * n_blk4 // group, dil4_body, 0)

    chunks_per_blk = blk // 16

    def dil16_body(g, carry):
        p4 = g // 2
        qs, ks, vs, bs, dsts = [], [], [], [], []
        for mm in range(2):
            off = pl.multiple_of((2 * (g % 2) + mm) * 16, 16)
            kk = kp[p4, :, pl.ds(off, 16), :].reshape(span, LANES)
            vv = vp[p4, :, pl.ds(off, 16), :].reshape(span, LANES)
            for jb in range(n_chunk // chunks_per_blk):
                dst = (p4, pl.ds(chunks_per_blk * jb, chunks_per_blk), pl.ds(off, 16), slice(None))
                qs.append(qp[dst].reshape(blk, LANES))
                ks.append(kk)
                vs.append(vv)
                bs.append(bias16[jb * (blk // radius)])
                dsts.append(dst)
        for dst, (o, m, l) in zip(dsts, _attend_group(qs, ks, vs, bs)):
            m_old = m_p[dst].reshape(blk, LANES)
            m_new = jnp.maximum(m_old, _merge_heads(m, blk))
            w_old = jnp.exp2(m_old - m_new)
            w_new = jnp.exp2(_merge_heads(m, blk) - m_new)
            acc = w_old * acc_p[dst].reshape(blk, LANES) + w_new * _merge_heads(o, blk)
            den = w_old * l_p[dst].reshape(blk, LANES) + w_new * _merge_heads(l, blk)
            acc_p[dst] = acc.reshape(chunks_per_blk, 16, LANES)
            l_p[dst] = den.reshape(chunks_per_blk, 16, LANES)
            m_p[dst] = m_new.reshape(chunks_per_blk, 16, LANES)
        return carry

    lax.fori_loop(0, PHASES4 * 2, dil16_body, 0)

    blk_per_chunk = PERM_CHUNK // blk

    def dil1_body(g, carry):
        merged = []
        for cc2 in range(group // blk_per_chunk):
            ch = g * (group // blk_per_chunk) + cc2
            acc = jnp.concatenate([acc_p[p4, ch] for p4 in range(PHASES4)], axis=0)
            den = jnp.concatenate([l_p[p4, ch] for p4 in range(PHASES4)], axis=0)
            mx = jnp.concatenate([m_p[p4, ch] for p4 in range(PHASES4)], axis=0)
            o_ph = (acc * (1.0 / den)).astype(BF16)
            lse = mx + jnp.log2(den)
            hi = lse.astype(BF16)
            rem = lse - hi.astype(F32)
            mid = rem.astype(BF16)
            low = (rem - mid.astype(F32)).astype(BF16)
            back = jnp.dot(to_token, jnp.concatenate([o_ph, hi, mid, low], axis=1),
                           preferred_element_type=F32)
            merged.append((back[:, 0:LANES], back[:, LANES:2 * LANES]
                           + back[:, 2 * LANES:3 * LANES] + back[:, 3 * LANES:4 * LANES]))
        qs, ks, vs, bs, starts = [], [], [], [], []
        for u in range(group):
            a = pl.multiple_of((g * group + u) * blk, blk)
            kst = pl.multiple_of(jnp.clip(a - radius, 0, seq - span), radius)
            qs.append(q_ref[pl.ds(a, blk), :])
            ks.append(k_ref[pl.ds(kst, span), :])
            vs.append(v_ref[pl.ds(kst, span), :])
            bs.append(bias1[(a - kst) // radius])
            starts.append(a)
        for u, (o, m, l) in enumerate(_attend_group(qs, ks, vs, bs)):
            o1 = _merge_heads(o * (1.0 / l), blk)
            lse1 = _merge_heads(m + jnp.log2(l), blk)
            o2, lse2 = merged[u // blk_per_chunk]
            half = slice((u % blk_per_chunk) * blk, (u % blk_per_chunk + 1) * blk)
            o2, lse2 = o2[half], lse2[half]
            top = jnp.maximum(lse1, lse2)
            e1 = jnp.exp2(lse1 - top)
            e2 = jnp.exp2(lse2 - top)
            o_ref[pl.ds(starts[u], blk), :] = ((e1 * o1 + e2 * o2)
                                               * (1.0 / (e1 + e2))).astype(o_ref.dtype)
        return carry

    lax.fori_loop(0, seq // (blk * group), dil1_body, 0)


def _dilated_attention(qkv, params, *, k_blk0, v_blk0, n_pairs, blk, radius):
    b, s, _ = qkv.shape
    span = blk + 2 * radius
    n_chunk = s // PERM_CHUNK
    assert s // 16 == span and s % PERM_CHUNK == 0 and blk % PIECE == 0
    phase = lambda dt: pltpu.VMEM((PHASES4, n_chunk, PIECE, LANES), dt)
    return pl.pallas_call(
        functools.partial(_dilated_kernel, seq=s, blk=blk, radius=radius),
        grid=(b, n_pairs),
        in_specs=[
            pl.BlockSpec(memory_space=pltpu.SMEM),
            pl.BlockSpec((None, s, LANES), lambda bi, c: (bi, 0, c)),
            pl.BlockSpec((None, s, LANES), lambda bi, c: (bi, 0, k_blk0 + c)),
            pl.BlockSpec((None, s, LANES), lambda bi, c: (bi, 0, v_blk0 + c)),
        ],
        out_specs=pl.BlockSpec((None, s, LANES), lambda bi, c: (bi, 0, c)),
        out_shape=jax.ShapeDtypeStruct((b, s, n_pairs * LANES), BF16),
        scratch_shapes=[phase(BF16), phase(BF16), phase(BF16), phase(F32), phase(F32), phase(F32)]
        + [pltpu.VMEM((3, 2 * blk, span), F32)] * 3,
        compiler_params=pltpu.CompilerParams(
            dimension_semantics=("arbitrary", "arbitrary"), vmem_limit_bytes=VMEM_LIMIT),
        name="dilated_attn",
    )(params, qkv, qkv, qkv)


def _na_kernel(q_ref, k_ref, v_ref, bias_ref, o_ref, *, rows):
    kh = NA_ROWS
    group = ATTN_UNROLL
    lo_q = _lo_mask((1, LANES))
    lo = _lo_mask((GRID_W, LANES))

    def body(g, carry):
        starts, offs, values, scores = [], [], [], []
        for u in range(group):
            i = g * group + u
            rs = jnp.clip(i - kh // 2, 0, rows - kh)
            qs = pl.multiple_of(i * GRID_W, GRID_W)
            ks = pl.multiple_of(rs * GRID_W, GRID_W)
            q = q_ref[pl.ds(qs, GRID_W), :]
            zero = jnp.zeros_like(q)
            qm = jnp.concatenate([jnp.where(lo_q, q, zero), jnp.where(lo_q, zero, q)], axis=0)
            k = k_ref[pl.ds(ks, kh * GRID_W), :]
            scores.append(lax.dot_general(qm, k, (((1,), (1,)), ((), ())),
                                          preferred_element_type=F32))
            starts.append(qs)
            offs.append(rs - i + (NA_ROWS - 1))
            values.append(v_ref[pl.ds(ks, kh * GRID_W), :])
        probs = []
        for u in range(group):
            s = scores[u] + jnp.concatenate(
                [bias_ref[offs[u] + 2 * j] for j in range(kh // 2)], axis=1)
            probs.append(_exp2_probs(s, jnp.max(s, axis=-1, keepdims=True)))
        for u in range(group):
            o, l = _value_and_sum(probs[u], values[u])
            o = o * (1.0 / l)
            o_ref[pl.ds(starts[u], GRID_W), :] = jnp.where(
                lo, o[:GRID_W], o[GRID_W:]).astype(o_ref.dtype)
        return carry

    lax.fori_loop(0, rows // group, body, 0)


def _na_bias_table(rpb):
    qc = np.arange(GRID_W)[:, None]
    kc = np.arange(GRID_W)[None, :]
    qstart = np.clip(qc - NA_COLS // 2, 0, GRID_W - NA_COLS)
    valid = (kc >= qstart) & (kc < qstart + NA_COLS)
    coff = np.clip(kc - qc, -(NA_COLS - 1), NA_COLS - 1) + NA_COLS - 1
    pick = (coff[:, :, None] == np.arange(2 * NA_COLS - 1)).astype(np.float32)
    t = jnp.einsum("hdj,qkj->hdqk", rpb.astype(F32), jnp.asarray(pick),
                   precision=lax.Precision.HIGHEST)
    t = jnp.where(jnp.asarray(valid), t * LOG2E, NEG_INF)
    t = jnp.concatenate([t[:, :-1], t[:, 1:]], axis=-1)
    n_off = 2 * NA_ROWS - 2
    t = t.reshape(C_HEADS // 2, 2, n_off, GRID_W, 2 * GRID_W).transpose(0, 2, 1, 3, 4)
    return t.reshape(C_HEADS // 2, n_off, 2 * GRID_W, 2 * GRID_W)


def _na_attention(qkv, bias):
    b, s, n_cols = qkv.shape
    n_pairs = C_HEADS // 2
    rows = s // GRID_W
    return pl.pallas_call(
        functools.partial(_na_kernel, rows=rows),
        grid=(b, n_pairs),
        in_specs=[
            pl.BlockSpec((None, s, LANES), lambda bi, c: (bi, 0, c)),
            pl.BlockSpec((None, s, LANES), lambda bi, c: (bi, 0, n_pairs + c)),
            pl.BlockSpec((None, s, LANES), lambda bi, c: (bi, 0, 2 * n_pairs + c)),
            pl.BlockSpec((None,) + bias.shape[1:], lambda bi, c: (c, 0, 0, 0)),
        ],
        out_specs=pl.BlockSpec((None, s, LANES), lambda bi, c: (bi, 0, c)),
        out_shape=jax.ShapeDtypeStruct((b, s, n_pairs * LANES), BF16),
        compiler_params=pltpu.CompilerParams(
            dimension_semantics=("arbitrary", "arbitrary"), vmem_limit_bytes=VMEM_LIMIT),
        name="na_attn",
    )(qkv, qkv, qkv, bias)


def _silu(g):
    return g * (1.0 / (1.0 + jnp.exp(-g)))


def _ffn_kernel(x_ref, oa_ref, ob_ref, wa_ref, wb_ref, g_ref, wg_ref, wu_ref, wd_ref,
                c0_ref, c1_ref, c2_ref, y_ref, n0_ref, n1_ref, n2_ref, a_scr):
    n0_ref[...] = c0_ref[...].astype(n0_ref.dtype)
    n1_ref[...] = c1_ref[...].astype(n1_ref.dtype)
    n2_ref[...] = c2_ref[...].astype(n2_ref.dtype)
    x = (x_ref[...] + jnp.dot(oa_ref[...], wa_ref[...], preferred_element_type=F32)
         + jnp.dot(ob_ref[...], wb_ref[...], preferred_element_type=F32))
    h = _rms_rows(x, g_ref[...]).astype(BF16)
    f = wg_ref.shape[1]
    for j in range(f // MXU_N):
        sl = slice(j * MXU_N, (j + 1) * MXU_N)
        gate = jnp.dot(h, wg_ref[:, sl], preferred_element_type=F32)
        up = jnp.dot(h, wu_ref[:, sl], preferred_element_type=F32)
        a_scr[:, sl] = (_silu(gate) * up).astype(BF16)
    y_ref[...] = x + jnp.dot(a_scr[...], wd_ref[...], preferred_element_type=F32)


def _ffn(x, oa, ob, wa, wb, g, wg, wu, wd, narrow, tm=256):
    t, d = x.shape
    f = wg.shape[1]
    steps = t // tm
    const = lambda a: pl.BlockSpec(a.shape, lambda i: (0, 0), pipeline_mode=pl.Buffered(1))
    row = lambda w: pl.BlockSpec((tm, w), lambda i: (i, 0))
    slab = lambda a: pl.BlockSpec((a.shape[0] // steps, a.shape[1]), lambda i: (i, 0))
    assert all(a.shape[0] % (16 * steps) == 0 for a in narrow)
    return pl.pallas_call(
        _ffn_kernel,
        grid=(steps,),
        in_specs=[row(d), row(oa.shape[1]), row(ob.shape[1]), const(wa), const(wb),
                  pl.BlockSpec((1, d), lambda i: (0, 0)),
                  const(wg), const(wu), const(wd)] + [slab(a) for a in narrow],
        out_specs=[pl.BlockSpec((tm, d), lambda i: (i, 0))] + [slab(a) for a in narrow],
        out_shape=[jax.ShapeDtypeStruct((t, d), F32)]
        + [jax.ShapeDtypeStruct(a.shape, BF16) for a in narrow],
        scratch_shapes=[pltpu.VMEM((tm, f), BF16)],
        compiler_params=pltpu.CompilerParams(
            dimension_semantics=("arbitrary",), vmem_limit_bytes=VMEM_LIMIT),
        name="dense_swiglu",
    )(x, oa, ob, wa, wb, g.reshape(1, d), wg, wu, wd, *narrow)


TOP_K = 2
MOE_TB = 512
MOE_TM = 512
MOE_RC = 256
KEY_STRIDE = 65536.0


def _router_kernel(x_ref, o_ref, wo_ref, g_ref, wr_hi_ref, wr_lo_ref, x1_ref, h_ref, info_ref,
                   info_t_ref, cnt_ref, carry_scr, ltri_scr):
    i = pl.program_id(0)
    tb = x_ref.shape[0]
    parts = 2
    hb = tb // parts
    rows = [slice(p * hb, (p + 1) * hb) for p in range(parts)]
    lane = lax.broadcasted_iota(jnp.int32, (1, LANES), 1)

    @pl.when(i == 0)
    def _():
        carry_scr[...] = jnp.zeros_like(carry_scr)
        r = lax.broadcasted_iota(jnp.int32, (hb, hb), 0)
        cidx = lax.broadcasted_iota(jnp.int32, (hb, hb), 1)
        ltri_scr[...] = jnp.where(cidx < r, 1.0, 0.0).astype(BF16)

    x1s = [x_ref[rw, :] + jnp.dot(o_ref[rw, :], wo_ref[...], preferred_element_type=F32)
           for rw in rows]
    hs = [_rms_rows(x1, g_ref[...]) for x1 in x1s]
    his = [h.astype(BF16) for h in hs]
    los = [(h - hi.astype(F32)).astype(BF16) for h, hi in zip(hs, his)]
    logits_all = [jnp.dot(hi, wr_hi_ref[...], preferred_element_type=F32)
                  + jnp.dot(lo_, wr_hi_ref[...], preferred_element_type=F32)
                  + jnp.dot(hi, wr_lo_ref[...], preferred_element_type=F32)
                  for hi, lo_ in zip(his, los)]
    for rw, x1, hi in zip(rows, x1s, his):
        x1_ref[rw, :] = x1
        h_ref[rw, :] = hi

    picks = []
    for logits in logits_all:
        logits = jnp.where(lane < N_EXPERTS, logits, NEG_INF)
        v1 = jnp.max(logits, axis=-1, keepdims=True)
        i1 = jnp.min(jnp.where(logits == v1, lane, LANES), axis=-1, keepdims=True)
        rest = jnp.where(lane == i1, NEG_INF, logits)
        v2 = jnp.max(rest, axis=-1, keepdims=True)
        i2 = jnp.min(jnp.where(rest == v2, lane, LANES), axis=-1, keepdims=True)
        e2 = jnp.exp(v2 - v1)
        g1 = 1.0 / (1.0 + e2)
        picks.append((i1, i2, g1, e2 * g1))

    pickeds = [jnp.where(lane == i1, 1.0, jnp.where(lane == i2, 1.0, 0.0))
               for i1, i2, _, _ in picks]
    inside = [jnp.dot(ltri_scr[...], pk.astype(BF16), preferred_element_type=F32)
              for pk in pickeds]
    counts = [jnp.sum(pk, axis=0, keepdims=True) for pk in pickeds]
    seen = carry_scr[...]
    for p, (i1, i2, g1, g2) in enumerate(picks):
        before = inside[p] + seen
        rank1 = jnp.sum(jnp.where(lane == i1, before, 0.0), axis=-1, keepdims=True)
        rank2 = jnp.sum(jnp.where(lane == i2, before, 0.0), axis=-1, keepdims=True)
        key1 = i1.astype(F32) * KEY_STRIDE + rank1
        key2 = i2.astype(F32) * KEY_STRIDE + rank2
        info = jnp.where(lane == 0, key1, jnp.where(lane == 1, key2,
                         jnp.where(lane == 2, g1, jnp.where(lane == 3, g2, 0.0))))
        info_ref[rows[p], :] = info
        info_t_ref[:, rows[p]] = info.T[:8, :]
        seen = seen + counts[p]
    cnt_ref[...] = jnp.broadcast_to(seen - carry_scr[...], cnt_ref.shape)
    carry_scr[...] = seen


def _router(x, o, wo, g, wr):
    t, d = x.shape
    nb = t // MOE_TB
    wr_hi = wr.astype(BF16)
    wr_lo = (wr - wr_hi.astype(F32)).astype(BF16)
    row = lambda w: pl.BlockSpec((MOE_TB, w), lambda i: (i, 0))
    const = lambda a: pl.BlockSpec(a.shape, lambda i: (0, 0))
    return pl.pallas_call(
        _router_kernel,
        grid=(nb,),
        in_specs=[row(d), row(o.shape[1]), const(wo), pl.BlockSpec((1, d), lambda i: (0, 0)),
                  const(wr_hi), const(wr_lo)],
        out_specs=[row(d), row(d), row(LANES),
                   pl.BlockSpec((None, 8, MOE_TB), lambda i: (i, 0, 0)),
                   pl.BlockSpec((None, 8, LANES), lambda i: (i, 0, 0))],
        out_shape=[jax.ShapeDtypeStruct((t, d), F32),
                   jax.ShapeDtypeStruct((t, d), BF16),
                   jax.ShapeDtypeStruct((t, LANES), F32),
                   jax.ShapeDtypeStruct((nb, 8, MOE_TB), F32),
                   jax.ShapeDtypeStruct((nb, 8, LANES), F32)],
        scratch_shapes=[pltpu.VMEM((1, LANES), F32),
                        pltpu.VMEM((MOE_TB // 2, MOE_TB // 2), BF16)],
        compiler_params=pltpu.CompilerParams(
            dimension_semantics=("arbitrary",), vmem_limit_bytes=VMEM_LIMIT),
        name="moe_router",
    )(x, o, wo, g.reshape(1, d), wr_hi, wr_lo)


def _moe_plan(block_counts, n_tokens):
    nb = block_counts.shape[0]
    n_tiles = (TOP_K * n_tokens + N_EXPERTS * (MOE_TM - 1)) // MOE_TM
    n_chunks = n_tiles * MOE_TM // MOE_RC
    tc = block_counts.astype(jnp.int32)
    cb = jnp.concatenate([jnp.zeros((1, N_EXPERTS), jnp.int32), jnp.cumsum(tc, axis=0)])
    counts = cb[-1]
    padded = ((counts + MOE_TM - 1) // MOE_TM) * MOE_TM
    ends = jnp.cumsum(padded)
    starts = ends - padded
    total = ends[-1]

    tiles = jnp.arange(n_tiles, dtype=jnp.int32)
    n_valid_tiles = total // MOE_TM
    t_valid = (tiles < n_valid_tiles).astype(jnp.int32)
    t_src = jnp.maximum(jnp.minimum(tiles, n_valid_tiles - 1), 0)
    t_exp = jnp.minimum(jnp.sum((t_src * MOE_TM)[:, None] >= ends[None, :], axis=1),
                        N_EXPERTS - 1).astype(jnp.int32)

    r0 = jnp.arange(n_chunks, dtype=jnp.int32) * MOE_RC
    ex = jnp.minimum(jnp.sum(r0[:, None] >= ends[None, :], axis=1), N_EXPERTS - 1)
    live = r0 < total
    lo = r0 - starts[ex]
    base = (ex * int(KEY_STRIDE) + lo).astype(jnp.int32)
    cbe = cb[:, ex]
    g_lo = jnp.sum(cbe[1:] <= lo[None, :], axis=0).astype(jnp.int32)
    g_cnt = jnp.sum(live[None, :] & (cbe[:-1] < (lo + MOE_RC)[None, :]) & (cbe[1:] > lo[None, :]),
                    axis=0).astype(jnp.int32)

    row_lo = starts[None, :] + cb[:-1]
    row_hi = starts[None, :] + cb[1:]
    c_first = row_lo // MOE_RC
    c_num = jnp.where(row_hi > row_lo, (row_hi - 1) // MOE_RC - c_first + 1, 0)
    c_end = jnp.cumsum(c_num, axis=1)
    slots = jnp.arange(COMBINE_MAX_CHUNKS, dtype=jnp.int32)
    e_of = jnp.minimum(jnp.sum(slots[None, :, None] >= c_end[:, None, :], axis=2),
                       N_EXPERTS - 1)
    before = jnp.take_along_axis(c_end - c_num, e_of, axis=1)
    c_c = jnp.take_along_axis(c_first, e_of, axis=1) + slots[None, :] - before
    c_cnt = c_end[:, -1].astype(jnp.int32)
    c_c = jnp.clip(c_c, 0, n_chunks - 1).astype(jnp.int32).reshape(-1)
    return dict(n_tiles=n_tiles, n_chunks=n_chunks, t_valid=t_valid, t_exp=t_exp,
                base=base, live=live.astype(jnp.int32), g_lo=g_lo, g_cnt=g_cnt,
                c_cnt=c_cnt, c_c=c_c)


def _gather_kernel(live_ref, lo_ref, cnt_ref, base_ref, info_t_ref, h_ref,
                   xs_ref, gs_ref, acc_scr, gsum_scr):
    c = pl.program_id(0)
    tb = info_t_ref.shape[-1]

    @pl.when(live_ref[c] == 1)
    def _():
        first = lo_ref[c]
        n = cnt_ref[c]
        row = lax.broadcasted_iota(jnp.int32, (MOE_RC, tb), 0).astype(F32)
        acc_scr[...] = jnp.zeros_like(acc_scr)
        gsum_scr[...] = jnp.zeros_like(gsum_scr)

        def two_blocks(it, carry):
            rows, gate = None, None
            for w in range(2):
                j = 2 * it + w
                b = first + jnp.minimum(j, n - 1)
                base = jnp.where(j < n, base_ref[c], NO_MATCH_BASE).astype(F32)
                info = info_t_ref[b]
                m1 = (info[0:1, :] - base) == row
                m2 = (info[1:2, :] - base) == row
                sel = jnp.where(m1, 1.0, jnp.where(m2, 1.0, 0.0)).astype(BF16)
                g = jnp.sum(jnp.where(m1, info[2:3, :], jnp.where(m2, info[3:4, :], 0.0)),
                            axis=-1, keepdims=True)
                r = jnp.dot(sel, h_ref[pl.ds(pl.multiple_of(b * tb, tb), tb), :],
                            preferred_element_type=F32)
                rows = r if rows is None else rows + r
                gate = g if gate is None else gate + g
            acc_scr[...] += rows
            gsum_scr[...] += gate
            return carry

        lax.fori_loop(0, (n + 1) // 2, two_blocks, 0)
        xs_ref[...] = acc_scr[...].astype(xs_ref.dtype)
        gs_ref[...] = jnp.broadcast_to(gsum_scr[...], gs_ref.shape)

    @pl.when(live_ref[c] == 0)
    def _():
        xs_ref[...] = jnp.zeros_like(xs_ref)
        gs_ref[...] = jnp.zeros_like(gs_ref)


def _moe_gather(plan, info_t, h):
    t, d = h.shape
    rows = plan["n_chunks"] * MOE_RC
    chunk = lambda c, live, lo, cn, ba: (c, 0)
    grid_spec = pltpu.PrefetchScalarGridSpec(
        num_scalar_prefetch=4,
        grid=(plan["n_chunks"],),
        in_specs=[pl.BlockSpec(info_t.shape, lambda c, live, lo, cn, ba: (0, 0, 0),
                               pipeline_mode=pl.Buffered(1)),
                  pl.BlockSpec((t, d), lambda c, live, lo, cn, ba: (0, 0),
                               pipeline_mode=pl.Buffered(1))],
        out_specs=[pl.BlockSpec((MOE_RC, d), chunk), pl.BlockSpec((MOE_RC, LANES), chunk)],
        scratch_shapes=[pltpu.VMEM((MOE_RC, d), F32), pltpu.VMEM((MOE_RC, 1), F32)],
    )
    return pl.pallas_call(
        _gather_kernel,
        grid_spec=grid_spec,
        out_shape=[jax.ShapeDtypeStruct((rows, d), BF16),
                   jax.ShapeDtypeStruct((rows, LANES), F32)],
        compiler_params=pltpu.CompilerParams(
            dimension_semantics=("arbitrary",), vmem_limit_bytes=VMEM_LIMIT),
        name="moe_gather",
    )(plan["live"], plan["g_lo"], plan["g_cnt"], plan["base"], info_t, h)


def _experts_kernel(exp_ref, val_ref, xs_ref, gs_ref, wg_ref, wu_ref, wd_ref, ys_ref,
                    acc_scr):
    i = pl.program_id(0)
    j = pl.program_id(1)

    last = j == pl.num_programs(1) - 1

    @pl.when(j == 0)
    def _():
        acc_scr[...] = jnp.zeros_like(acc_scr)

    @pl.when(val_ref[i] == 1)
    def _():
        x = xs_ref[...]
        gate = jnp.dot(x, wg_ref[...], preferred_element_type=F32)
        up = jnp.dot(x, wu_ref[...], preferred_element_type=F32)
        a = (_silu(gate) * up).astype(BF16)
        acc_scr[...] += jnp.dot(a, wd_ref[...], preferred_element_type=F32)

    @pl.when(last)
    def _():
        ys_ref[...] = (acc_scr[...] * gs_ref[:, 0:1]).astype(ys_ref.dtype)


def _moe_experts(plan, xs, gs, wg, wu, wd, tf):
    rows, d = xs.shape
    f = wg.shape[2]
    nf = f // tf

    def wcol(i, j, ex, val):
        return (ex[i], 0, jnp.where(val[i] == 1, j, nf - 1))

    def wrow(i, j, ex, val):
        return (ex[i], jnp.where(val[i] == 1, j, nf - 1), 0)

    tile = lambda i, j, ex, val: (i, 0)
    grid_spec = pltpu.PrefetchScalarGridSpec(
        num_scalar_prefetch=2,
        grid=(plan["n_tiles"], nf),
        in_specs=[pl.BlockSpec((MOE_TM, d), tile),
                  pl.BlockSpec((MOE_TM, LANES), tile),
                  pl.BlockSpec((None, d, tf), wcol),
                  pl.BlockSpec((None, d, tf), wcol),
                  pl.BlockSpec((None, tf, d), wrow)],
        out_specs=pl.BlockSpec((MOE_TM, d), tile),
        scratch_shapes=[pltpu.VMEM((MOE_TM, d), F32)],
    )
    return pl.pallas_call(
        _experts_kernel,
        grid_spec=grid_spec,
        out_shape=jax.ShapeDtypeStruct((rows, d), BF16),
        compiler_params=pltpu.CompilerParams(
            dimension_semantics=("arbitrary", "arbitrary"), vmem_limit_bytes=VMEM_LIMIT),
        name="moe_experts",
    )(plan["t_exp"], plan["t_valid"], xs, gs, wg, wu, wd)


NO_MATCH_BASE = -(2 ** 30)


COMBINE_GROUP = 4
COMBINE_MAX_CHUNKS = N_EXPERTS * (MOE_TB // MOE_RC + 1)


def _combine_kernel(cnt_ref, pc_ref, base_ref, info_ref, x_ref, ys_hbm, o_ref, ybuf, sem):
    b = pl.program_id(0)
    tb = x_ref.shape[0]
    d = x_ref.shape[1]

    def n_fetch(blk):
        return ((cnt_ref[blk] + COMBINE_GROUP - 1) // COMBINE_GROUP) * COMBINE_GROUP

    def chunk_of(blk, j):
        return pc_ref[blk * COMBINE_MAX_CHUNKS + jnp.minimum(j, cnt_ref[blk] - 1)]

    def copy(blk, j, slot):
        row0 = pl.multiple_of(chunk_of(blk, j) * MOE_RC, MOE_RC)
        return pltpu.make_async_copy(ys_hbm.at[pl.ds(row0, MOE_RC), :], ybuf.at[slot, j],
                                     sem.at[slot])

    def start_block(blk, slot):
        def issue(j, carry):
            copy(blk, j, slot).start()
            return carry
        lax.fori_loop(0, n_fetch(blk), issue, 0)

    @pl.when(b == 0)
    def _():
        start_block(0, 0)

    @pl.when(b + 1 < pl.num_programs(0))
    def _():
        start_block(b + 1, (b + 1) % 2)

    slot = b % 2
    n = cnt_ref[b]

    def drain(j, carry):
        copy(b, j, slot).wait()
        return carry

    lax.fori_loop(0, n_fetch(b), drain, 0)

    lane = lax.broadcasted_iota(jnp.int32, (1, LANES), 1)
    info = info_ref[...]
    k1 = jnp.sum(jnp.where(lane == 0, info, 0.0), axis=-1, keepdims=True)
    k2 = jnp.sum(jnp.where(lane == 1, info, 0.0), axis=-1, keepdims=True)
    col = lax.broadcasted_iota(jnp.int32, (tb, MOE_RC), 1).astype(F32)
    o_ref[...] = x_ref[...]

    def group(gi, carry):
        sels = []
        for w in range(COMBINE_GROUP):
            j = gi * COMBINE_GROUP + w
            base = jnp.where(j < n, base_ref[chunk_of(b, j)], NO_MATCH_BASE).astype(F32)
            sels.append(jnp.where(k1 - base == col, 1.0,
                                  jnp.where(k2 - base == col, 1.0, 0.0)).astype(BF16))
        rows = ybuf[slot, pl.ds(gi * COMBINE_GROUP, COMBINE_GROUP)]
        o_ref[...] += jnp.dot(jnp.concatenate(sels, axis=1),
                              rows.reshape(COMBINE_GROUP * MOE_RC, d),
                              preferred_element_type=F32)
        return carry

    lax.fori_loop(0, n_fetch(b) // COMBINE_GROUP, group, 0)


def _moe_combine(plan, info, ys, x):
    t, d = x.shape
    block = lambda b, cn, pc, ba: (b, 0)
    grid_spec = pltpu.PrefetchScalarGridSpec(
        num_scalar_prefetch=3,
        grid=(t // MOE_TB,),
        in_specs=[pl.BlockSpec((MOE_TB, LANES), block),
                  pl.BlockSpec((MOE_TB, d), block),
                  pl.BlockSpec(memory_space=pl.ANY)],
        out_specs=pl.BlockSpec((MOE_TB, d), block),
        scratch_shapes=[pltpu.VMEM((2, COMBINE_MAX_CHUNKS, MOE_RC, d), BF16),
                        pltpu.SemaphoreType.DMA((2,))],
    )
    return pl.pallas_call(
        _combine_kernel,
        grid_spec=grid_spec,
        out_shape=jax.ShapeDtypeStruct((t, d), F32),
        compiler_params=pltpu.CompilerParams(
            dimension_semantics=("arbitrary",), vmem_limit_bytes=VMEM_LIMIT),
        name="moe_combine",
    )(plan["c_cnt"], plan["c_c"], plan["base"], info, x, ys)


def _moe(x, o, wo, g, wr, wg, wu, wd, tf=1792):
    t, _ = x.shape
    x1, h, info, info_t, block_counts = _router(x, o, wo, g, wr)
    plan = _moe_plan(block_counts[:, 0, :N_EXPERTS], t)
    xs, gs = _moe_gather(plan, info_t, h)
    ys = _moe_experts(plan, xs, gs, wg, wu, wd, tf)
    return _moe_combine(plan, info, ys, x1)


def _alibi_slopes():
    s = np.exp2(-8.0 * np.arange(1, N_HEADS + 1) / N_HEADS).astype(np.float32)
    return s[0::2][:A_HEADS], s[1::2][:B_Q_HEADS]


_B_HEAD_ORDER = np.array([0, 4, 1, 5, 2, 6, 3, 7])


def _even_layer(x2d, batch, norm1, w_in, qn_a, kn_a, qn_b, kn_b, sink_b, w_out,
                norm2, w_gate, w_up, w_down, narrow):
    t, d = x2d.shape
    s = t // batch
    wa = A_HEADS * HEAD_DIM
    wq_b = B_Q_HEADS * HEAD_DIM
    wkv_b = B_KV_HEADS * HEAD_DIM
    qb0 = 3 * wa
    q_scale = HEAD_DIM ** -0.5 * LOG2E

    w_qb = w_in[:, qb0:qb0 + wq_b].reshape(d, B_Q_HEADS, HEAD_DIM)[:, _B_HEAD_ORDER]
    w_in_p = jnp.concatenate(
        [w_in[:, :qb0], w_qb.reshape(d, wq_b), w_in[:, qb0 + wq_b:]], axis=1).astype(BF16)
    ones = jnp.ones((HEAD_DIM,), F32)
    col_gain = jnp.concatenate([
        jnp.tile(qn_a * q_scale, A_HEADS), jnp.tile(kn_a, A_HEADS), jnp.tile(ones, A_HEADS),
        jnp.tile(qn_b * q_scale, B_Q_HEADS), jnp.tile(kn_b, B_KV_HEADS),
        jnp.tile(ones, B_KV_HEADS)])
    cpl = wa // LANES
    norm_chunks = ((True,) * (2 * cpl) + (False,) * cpl + (True,) * (wq_b // LANES)
                   + (True,) * (wkv_b // LANES) + (False,) * (wkv_b // LANES))
    qkv = _norm_proj(x2d, norm1, w_in_p, col_gain, norm_chunks)
    n_cols = qkv.shape[1]
    qkv3 = qkv.reshape(batch, s, n_cols)

    slopes_a, slopes_b = _alibi_slopes()
    par_a = jnp.asarray(slopes_a).reshape(1, A_HEADS)
    radii = {window // (2 * dil) for window, dil in DIL_CONFIGS}
    assert tuple(dil for _, dil in DIL_CONFIGS) == (1, 4, 16) and len(radii) == 1
    oa = _dilated_attention(qkv3, par_a, k_blk0=cpl, v_blk0=2 * cpl, n_pairs=cpl,
                            blk=A_BLOCK, radius=radii.pop())
    par_b = jnp.stack([jnp.asarray(slopes_b[_B_HEAD_ORDER]),
                       sink_b.astype(F32)[_B_HEAD_ORDER] * LOG2E])
    kb = 3 * cpl + wq_b // LANES
    ob = _window_attention(qkv3, par_b, q_blk0=3 * cpl, k_blk=kb, v_blk=kb + wkv_b // LANES,
                           n_pairs=wq_b // LANES, radius=B_WINDOW, blk=B_BLOCK)
    w_out_a = w_out[:wa].astype(BF16)
    w_out_b = w_out[wa:].reshape(B_Q_HEADS, HEAD_DIM, d)[_B_HEAD_ORDER].reshape(wq_b, d)
    y, *narrowed = _ffn(x2d, oa.reshape(t, wa), ob.reshape(t, wq_b), w_out_a,
                        w_out_b.astype(BF16), norm2, w_gate.astype(BF16), w_up.astype(BF16),
                        w_down.astype(BF16), narrow)
    return y, narrowed


def _odd_layer(x2d, batch, norm1, w_qkv, qn, kn, rpb, w_out, norm2, w_router,
               w_gate, w_up, w_down):
    t, d = x2d.shape
    s = t // batch
    wc = C_HEADS * HEAD_DIM
    ones = jnp.ones((HEAD_DIM,), F32)
    col_gain = jnp.concatenate([jnp.tile(qn * (HEAD_DIM ** -0.5 * LOG2E), C_HEADS),
                                jnp.tile(kn, C_HEADS), jnp.tile(ones, C_HEADS)])
    cpl = wc // LANES
    norm_chunks = (True,) * (2 * cpl) + (False,) * cpl
    qkv = _norm_proj(x2d, norm1, w_qkv.astype(BF16), col_gain, norm_chunks)
    o = _na_attention(qkv.reshape(batch, s, 3 * wc), _na_bias_table(rpb))
    wr = jnp.zeros((d, LANES), F32).at[:, :N_EXPERTS].set(w_router.astype(F32))
    return _moe(x2d, o.reshape(t, wc), w_out.astype(BF16), norm2, wr, w_gate, w_up, w_down)


def kernel(x, ev_norm1, ev_w_in, ev_qn_a, ev_kn_a, ev_qn_b, ev_kn_b, ev_sink_b, ev_w_out, ev_norm2, ev_ffn_gate, ev_ffn_up, ev_ffn_down, od_norm1, od_w_qkv, od_qn, od_kn, od_rpb, od_w_out, od_norm2, od_router, od_exp_gate, od_exp_up, od_exp_down):
    batch, s, d = x.shape
    depth = ev_norm1.shape[0] + od_norm1.shape[0]
    h = x.reshape(batch * s, d)
    n_e, _, f_e = od_exp_gate.shape[1:]
    experts = None
    for layer in range(depth):
        j = layer // 2
        if layer % 2 == 0:
            if j < od_norm1.shape[0]:
                narrow = [od_exp_gate[j].reshape(n_e * d, f_e), od_exp_up[j].reshape(n_e * d, f_e),
                          od_exp_down[j].reshape(n_e * f_e, d)]
            else:
                narrow = [jnp.zeros((batch * s // 16, LANES), F32)] * 3
            h, narrowed = _even_layer(h, batch, ev_norm1[j], ev_w_in[j], ev_qn_a[j], ev_kn_a[j],
                                      ev_qn_b[j], ev_kn_b[j], ev_sink_b[j], ev_w_out[j],
                                      ev_norm2[j], ev_ffn_gate[j], ev_ffn_up[j], ev_ffn_down[j],
                                      narrow)
            experts = (narrowed[0].reshape(n_e, d, f_e), narrowed[1].reshape(n_e, d, f_e),
                       narrowed[2].reshape(n_e, f_e, d))
        else:
            h = _odd_layer(h, batch, od_norm1[j], od_w_qkv[j], od_qn[j], od_kn[j], od_rpb[j],
                           od_w_out[j], od_norm2[j], od_router[j], *experts)
    return h.reshape(batch, s, d)
```

```python
import functools

import numpy as np
import jax
import jax.numpy as jnp
from jax import lax
from jax.experimental import pallas as pl
from jax.experimental.pallas import tpu as pltpu

D_MODEL = 1024
HEAD_DIM = 64
N_HEADS = D_MODEL // HEAD_DIM
A_HEADS = N_HEADS // 2
B_Q_HEADS = N_HEADS // 2
B_KV_HEADS = max(1, B_Q_HEADS // 4)
C_HEADS = N_HEADS
DIL_CONFIGS = ((128, 1), (512, 4), (2048, 16))
A_BLOCK = 128
B_WINDOW = 128
B_BLOCK = 128
GRID_W = 64
NA_ROWS = 8
NA_COLS = 16
N_EXPERTS = 8
RMS_EPS = 1e-6
NEG_INF = -1e30

LANES = 128
MXU_N = 256
VMEM_LIMIT = 56 * 1024 * 1024

ATTN_UNROLL = 8

F32 = jnp.float32
BF16 = jnp.bfloat16


def _lo_mask(shape):
    return lax.broadcasted_iota(jnp.int32, shape, len(shape) - 1) < HEAD_DIM


LOG2E = 1.4426950408889634


def _exp2_probs(s, m):
    return jnp.exp2(s - m).astype(BF16)


def _value_and_sum(p, v):
    ov = jnp.dot(p, jnp.concatenate([v, jnp.ones_like(v)], axis=1), preferred_element_type=F32)
    return ov[:, :LANES], ov[:, LANES:LANES + 1]


def _rms_rows(x, g):
    ms = jnp.mean(x * x, axis=-1, keepdims=True)
    return x * lax.rsqrt(ms + RMS_EPS) * g


def _norm_proj_kernel(x_ref, g_ref, w_ref, cg_ref, o_ref, *, norm_chunks):
    h = _rms_rows(x_ref[...], g_ref[...]).astype(BF16)
    n_out = o_ref.shape[-1]
    lo = _lo_mask((1, LANES))
    for j in range(n_out // MXU_N):
        y = jnp.dot(h, w_ref[:, j * MXU_N:(j + 1) * MXU_N], preferred_element_type=F32)
        for half in range(MXU_N // LANES):
            c = j * (MXU_N // LANES) + half
            yc = y[:, half * LANES:(half + 1) * LANES]
            if norm_chunks[c]:
                sq = yc * yc
                s_lo = jnp.sum(jnp.where(lo, sq, 0.0), axis=-1, keepdims=True)
                s_hi = jnp.sum(jnp.where(lo, 0.0, sq), axis=-1, keepdims=True)
                inv = jnp.where(lo, lax.rsqrt(s_lo * (1.0 / HEAD_DIM) + RMS_EPS),
                                lax.rsqrt(s_hi * (1.0 / HEAD_DIM) + RMS_EPS))
                yc = yc * inv * cg_ref[:, c * LANES:(c + 1) * LANES]
            o_ref[:, c * LANES:(c + 1) * LANES] = yc.astype(o_ref.dtype)


def _norm_proj(x, g, w, col_gain, norm_chunks, tm=512):
    t, d = x.shape
    n = w.shape[1]
    return pl.pallas_call(
        functools.partial(_norm_proj_kernel, norm_chunks=norm_chunks),
        grid=(t // tm,),
        in_specs=[
            pl.BlockSpec((tm, d), lambda i: (i, 0)),
            pl.BlockSpec((1, d), lambda i: (0, 0)),
            pl.BlockSpec((d, n), lambda i: (0, 0)),
            pl.BlockSpec((1, n), lambda i: (0, 0)),
        ],
        out_specs=pl.BlockSpec((tm, n), lambda i: (i, 0)),
        out_shape=jax.ShapeDtypeStruct((t, n), BF16),
        compiler_params=pltpu.CompilerParams(
            dimension_semantics=("arbitrary",), vmem_limit_bytes=VMEM_LIMIT),
        name="norm_proj",
    )(x, g.reshape(1, d), w, col_gain.reshape(1, n))


def _window_kernel(par_ref, q_ref, k_ref, v_ref, o_ref, bias_scr, *, seq, blk, radius):
    span = blk + 2 * radius
    c = pl.program_id(1)
    group = ATTN_UNROLL
    row_lo = lax.broadcasted_iota(jnp.int32, (2 * blk, 1), 0) < blk
    slope_col = jnp.where(row_lo, par_ref[0, 2 * c], par_ref[0, 2 * c + 1])
    sink_col = jnp.where(row_lo, par_ref[1, 2 * c], par_ref[1, 2 * c + 1])

    iq = lax.broadcasted_iota(jnp.int32, (2 * blk, span), 0) % blk
    ik = lax.broadcasted_iota(jnp.int32, (2 * blk, span), 1)
    _fill_band_bias(bias_scr, iq, ik, radius, 1, slope_col)
    lo_q = _lo_mask((1, LANES))
    lo = _lo_mask((blk, LANES))

    def body(g, carry):
        starts, variants, values, scores = [], [], [], []
        for u in range(group):
            a = pl.multiple_of((g * group + u) * blk, blk)
            ks = pl.multiple_of(jnp.clip(a - radius, 0, seq - span), radius)
            q = q_ref[pl.ds(a, blk), :]
            zero = jnp.zeros_like(q)
            qm = jnp.concatenate([jnp.where(lo_q, q, zero), jnp.where(lo_q, zero, q)], axis=0)
            k = k_ref[pl.ds(ks, span), :]
            scores.append(lax.dot_general(qm, k, (((1,), (1,)), ((), ())),
                                          preferred_element_type=F32))
            starts.append(a)
            variants.append((a - ks) // radius)
            values.append(v_ref[pl.ds(ks, span), :])
        probs, sink_terms = [], []
        for u in range(group):
            s = scores[u] + bias_scr[variants[u]]
            m = jnp.maximum(jnp.max(s, axis=-1, keepdims=True), sink_col)
            probs.append(_exp2_probs(s, m))
            sink_terms.append(jnp.exp2(sink_col - m))
        for u in range(group):
            o, l = _value_and_sum(probs[u], values[u])
            o = o * (1.0 / (l + sink_terms[u]))
            o_ref[pl.ds(starts[u], blk), :] = jnp.where(lo, o[:blk], o[blk:]).astype(o_ref.dtype)
        return carry

    lax.fori_loop(0, seq // (blk * group), body, 0)


def _window_attention(qkv, params, *, q_blk0, k_blk, v_blk, n_pairs, radius, blk):
    b, s, _ = qkv.shape
    span = blk + 2 * radius
    return pl.pallas_call(
        functools.partial(_window_kernel, seq=s, blk=blk, radius=radius),
        grid=(b, n_pairs),
        in_specs=[
            pl.BlockSpec(memory_space=pltpu.SMEM),
            pl.BlockSpec((None, s, LANES), lambda bi, c: (bi, 0, q_blk0 + c)),
            pl.BlockSpec((None, s, LANES), lambda bi, c: (bi, 0, k_blk)),
            pl.BlockSpec((None, s, LANES), lambda bi, c: (bi, 0, v_blk)),
        ],
        out_specs=pl.BlockSpec((None, s, LANES), lambda bi, c: (bi, 0, c)),
        out_shape=jax.ShapeDtypeStruct((b, s, n_pairs * LANES), BF16),
        scratch_shapes=[pltpu.VMEM((3, 2 * blk, span), F32)],
        compiler_params=pltpu.CompilerParams(
            dimension_semantics=("arbitrary", "arbitrary"), vmem_limit_bytes=VMEM_LIMIT),
        name="window_attn",
    )(params, qkv, qkv, qkv)


PERM_CHUNK = 256
PHASES4 = 4
PIECE = PERM_CHUNK // PHASES4


def _stack_heads(q):
    lo = _lo_mask((1, LANES))
    zero = jnp.zeros_like(q)
    return jnp.concatenate([jnp.where(lo, q, zero), jnp.where(lo, zero, q)], axis=0)


def _merge_heads(x, blk):
    x = jnp.broadcast_to(x, (2 * blk, LANES))
    return jnp.where(_lo_mask((blk, LANES)), x[:blk], x[blk:])


def _attend_group(qs, ks, vs, biases):
    scores = [lax.dot_general(_stack_heads(q), k, (((1,), (1,)), ((), ())),
                              preferred_element_type=F32) for q, k in zip(qs, ks)]
    stats = []
    for s, bias in zip(scores, biases):
        s = s + bias
        m = jnp.max(s, axis=-1, keepdims=True)
        stats.append((_exp2_probs(s, m), m))
    outs = []
    for (p, m), v in zip(stats, vs):
        o, l = _value_and_sum(p, v)
        outs.append((o, m, l))
    return outs


def _fill_band_bias(bias_scr, pos_q, pos_k, radius, pos_scale, slope_col):
    for var in range(3):
        dist = jnp.abs(pos_q + var * radius - pos_k)
        dist_f = (dist * pos_scale).astype(F32)
        bias_scr[var] = jnp.where(dist <= radius, -(slope_col * dist_f) * LOG2E, NEG_INF)


def _dilated_kernel(par_ref, q_ref, k_ref, v_ref, o_ref, qp, kp, vp, acc_p, m_p, l_p,
                    bias1, bias4, bias16, *, seq, blk, radius):
    span = blk + 2 * radius
    n_chunk = seq // PERM_CHUNK
    group = ATTN_UNROLL
    c = pl.program_id(1)
    row_lo = lax.broadcasted_iota(jnp.int32, (2 * blk, 1), 0) < blk
    slope_col = jnp.where(row_lo, par_ref[0, 2 * c], par_ref[0, 2 * c + 1])

    def piece_pos(x):
        return PIECE * (x // PIECE) + PHASES4 * (x % 16) + (x % PIECE) // 16

    iq = lax.broadcasted_iota(jnp.int32, (2 * blk, span), 0) % blk
    ik = lax.broadcasted_iota(jnp.int32, (2 * blk, span), 1)
    _fill_band_bias(bias1, iq, ik, radius, 1, slope_col)
    _fill_band_bias(bias4, piece_pos(iq), piece_pos(ik), radius, 4, slope_col)
    _fill_band_bias(bias16, iq, ik, radius, 16, slope_col)

    rr = lax.broadcasted_iota(jnp.int32, (PERM_CHUNK, PERM_CHUNK), 0)
    cc = lax.broadcasted_iota(jnp.int32, (PERM_CHUNK, PERM_CHUNK), 1)
    to_phase = jnp.where(cc == 16 * (rr % 16) + PHASES4 * ((rr // 16) % 4) + rr // PIECE,
                         1.0, 0.0).astype(BF16)
    to_token = jnp.where(cc == PIECE * (rr % PHASES4) + 16 * ((rr // PHASES4) % 4) + rr // 16,
                         1.0, 0.0).astype(BF16)

    def permute_body(ch2, carry):
        for half in range(2):
            ch = 2 * ch2 + half
            r0 = pl.multiple_of(ch * PERM_CHUNK, PERM_CHUNK)
            x = jnp.concatenate([q_ref[pl.ds(r0, PERM_CHUNK), :],
                                 k_ref[pl.ds(r0, PERM_CHUNK), :],
                                 v_ref[pl.ds(r0, PERM_CHUNK), :]], axis=1)
            y = jnp.dot(to_phase, x, preferred_element_type=F32).astype(BF16)
            for p4 in range(PHASES4):
                rows = slice(PIECE * p4, PIECE * (p4 + 1))
                qp[p4, ch] = y[rows, 0:LANES]
                kp[p4, ch] = y[rows, LANES:2 * LANES]
                vp[p4, ch] = y[rows, 2 * LANES:3 * LANES]
        return carry

    lax.fori_loop(0, n_chunk // 2, permute_body, 0)

    pieces_per_blk = blk // PIECE
    n_blk4 = seq // (PHASES4 * blk)

    def dil4_body(g, carry):
        p4 = g // (n_blk4 // group)
        j0 = (g % (n_blk4 // group)) * group
        qs, ks, vs, bs = [], [], [], []
        for u in range(group):
            pc = pieces_per_blk * (j0 + u)
            cs = jnp.clip(pc - radius // PIECE, 0, n_chunk - span // PIECE)
            qs.append(qp[p4, pl.ds(pc, pieces_per_blk)].reshape(blk, LANES))
            ks.append(kp[p4, pl.ds(cs, span // PIECE)].reshape(span, LANES))
            vs.append(vp[p4, pl.ds(cs, span // PIECE)].reshape(span, LANES))
            bs.append(bias4[(pc - cs) // (radius // PIECE)])
        for u, (o, m, l) in enumerate(_attend_group(qs, ks, vs, bs)):
            dst = (p4, pl.ds(pieces_per_blk * (j0 + u), pieces_per_blk))
            acc_p[dst] = _merge_heads(o, blk).reshape(pieces_per_blk, PIECE, LANES)
            m_p[dst] = _merge_heads(m, blk).reshape(pieces_per_blk, PIECE, LANES)
            l_p[dst] = _merge_heads(l, blk).reshape(pieces_per_blk, PIECE, LANES)
        return carry

    lax.fori_loop(0, PHASES4 * n_blk4 // group, dil4_body, 0)

    chunks_per_blk = blk // 16

    def dil16_body(p4, carry):
        qs, ks, vs, bs, dsts = [], [], [], [], []
        for mm in range(PIECE // 16):
            off = mm * 16
            kk = kp[p4, :, pl.ds(off, 16), :].reshape(span, LANES)
            vv = vp[p4, :, pl.ds(off, 16), :].reshape(span, LANES)
            for jb in range(n_chunk // chunks_per_blk):
                dst = (p4, pl.ds(chunks_per_blk * jb, chunks_per_blk), pl.ds(off, 16), slice(None))
                qs.append(qp[dst].reshape(blk, LANES))
                ks.append(kk)
                vs.append(vv)
                bs.append(bias16[jb * (blk // radius)])
                dsts.append(dst)
        for dst, (o, m, l) in zip(dsts, _attend_group(qs, ks, vs, bs)):
            m_old = m_p[dst].reshape(blk, LANES)
            m_new = jnp.maximum(m_old, _merge_heads(m, blk))
            w_old = jnp.exp2(m_old - m_new)
            w_new = jnp.exp2(_merge_heads(m, blk) - m_new)
            acc = w_old * acc_p[dst].reshape(blk, LANES) + w_new * _merge_heads(o, blk)
            den = w_old * l_p[dst].reshape(blk, LANES) + w_new * _merge_heads(l, blk)
            acc_p[dst] = acc.reshape(chunks_per_blk, 16, LANES)
            l_p[dst] = den.reshape(chunks_per_blk, 16, LANES)
            m_p[dst] = m_new.reshape(chunks_per_blk, 16, LANES)
        return carry

    lax.fori_loop(0, PHASES4, dil16_body, 0)

    blk_per_chunk = PERM_CHUNK // blk

    def dil1_body(g, carry):
        merged = []
        for cc2 in range(group // blk_per_chunk):
            ch = g * (group // blk_per_chunk) + cc2
            acc = jnp.concatenate([acc_p[p4, ch] for p4 in range(PHASES4)], axis=0)
            den = jnp.concatenate([l_p[p4, ch] for p4 in range(PHASES4)], axis=0)
            mx = jnp.concatenate([m_p[p4, ch] for p4 in range(PHASES4)], axis=0)
            o_ph = (acc * (1.0 / den)).astype(BF16)
            lse = mx + jnp.log2(den)
            hi = lse.astype(BF16)
            rem = lse - hi.astype(F32)
            mid = rem.astype(BF16)
            low = (rem - mid.astype(F32)).astype(BF16)
            back = jnp.dot(to_token, jnp.concatenate([o_ph, hi, mid, low], axis=1),
                           preferred_element_type=F32)
            merged.append((back[:, 0:LANES], back[:, LANES:2 * LANES]
                           + back[:, 2 * LANES:3 * LANES] + back[:, 3 * LANES:4 * LANES]))
        qs, ks, vs, bs, starts = [], [], [], [], []
        for u in range(group):
            a = pl.multiple_of((g * group + u) * blk, blk)
            kst = pl.multiple_of(jnp.clip(a - radius, 0, seq - span), radius)
            qs.append(q_ref[pl.ds(a, blk), :])
            ks.append(k_ref[pl.ds(kst, span), :])
            vs.append(v_ref[pl.ds(kst, span), :])
            bs.append(bias1[(a - kst) // radius])
            starts.append(a)
        for u, (o, m, l) in enumerate(_attend_group(qs, ks, vs, bs)):
            o1 = _merge_heads(o * (1.0 / l), blk)
            lse1 = _merge_heads(m + jnp.log2(l), blk)
            o2, lse2 = merged[u // blk_per_chunk]
            half = slice((u % blk_per_chunk) * blk, (u % blk_per_chunk + 1) * blk)
            o2, lse2 = o2[half], lse2[half]
            top = jnp.maximum(lse1, lse2)
            e1 = jnp.exp2(lse1 - top)
            e2 = jnp.exp2(lse2 - top)
            o_ref[pl.ds(starts[u], blk), :] = ((e1 * o1 + e2 * o2)
                                               * (1.0 / (e1 + e2))).astype(o_ref.dtype)
        return carry

    lax.fori_loop(0, seq // (blk * group), dil1_body, 0)


def _dilated_attention(qkv, params, *, k_blk0, v_blk0, n_pairs, blk, radius):
    b, s, _ = qkv.shape
    span = blk + 2 * radius
    n_chunk = s // PERM_CHUNK
    assert s // 16 == span and s % PERM_CHUNK == 0 and blk % PIECE == 0
    phase = lambda dt: pltpu.VMEM((PHASES4, n_chunk, PIECE, LANES), dt)
    return pl.pallas_call(
        functools.partial(_dilated_kernel, seq=s, blk=blk, radius=radius),
        grid=(b, n_pairs),
        in_specs=[
            pl.BlockSpec(memory_space=pltpu.SMEM),
            pl.BlockSpec((None, s, LANES), lambda bi, c: (bi, 0, c)),
            pl.BlockSpec((None, s, LANES), lambda bi, c: (bi, 0, k_blk0 + c)),
            pl.BlockSpec((None, s, LANES), lambda bi, c: (bi, 0, v_blk0 + c)),
        ],
        out_specs=pl.BlockSpec((None, s, LANES), lambda bi, c: (bi, 0, c)),
        out_shape=jax.ShapeDtypeStruct((b, s, n_pairs * LANES), BF16),
        scratch_shapes=[phase(BF16), phase(BF16), phase(BF16), phase(F32), phase(F32), phase(F32)]
        + [pltpu.VMEM((3, 2 * blk, span), F32)] * 3,
        compiler_params=pltpu.CompilerParams(
            dimension_semantics=("arbitrary", "arbitrary"), vmem_limit_bytes=VMEM_LIMIT),
        name="dilated_attn",
    )(params, qkv, qkv, qkv)


def _na_kernel(q_ref, k_ref, v_ref, bias_ref, o_ref, *, rows):
    kh = NA_ROWS
    group = ATTN_UNROLL
    lo_q = _lo_mask((1, LANES))
    lo = _lo_mask((GRID_W, LANES))

    def body(g, carry):
        starts, offs, values, scores = [], [], [], []
        for u in range(group):
            i = g * group + u
            rs = jnp.clip(i - kh // 2, 0, rows - kh)
            qs = pl.multiple_of(i * GRID_W, GRID_W)
            ks = pl.multiple_of(rs * GRID_W, GRID_W)
            q = q_ref[pl.ds(qs, GRID_W), :]
            zero = jnp.zeros_like(q)
            qm = jnp.concatenate([jnp.where(lo_q, q, zero), jnp.where(lo_q, zero, q)], axis=0)
            k = k_ref[pl.ds(ks, kh * GRID_W), :]
            scores.append(lax.dot_general(qm, k, (((1,), (1,)), ((), ())),
                                          preferred_element_type=F32))
            starts.append(qs)
            offs.append(rs - i + (NA_ROWS - 1))
            values.append(v_ref[pl.ds(ks, kh * GRID_W), :])
        probs = []
        for u in range(group):
            s = scores[u] + jnp.concatenate(
                [bias_ref[offs[u] + 2 * j] for j in range(kh // 2)], axis=1)
            probs.append(_exp2_probs(s, jnp.max(s, axis=-1, keepdims=True)))
        for u in range(group):
            o, l = _value_and_sum(probs[u], values[u])
            o = o * (1.0 / l)
            o_ref[pl.ds(starts[u], GRID_W), :] = jnp.where(
                lo, o[:GRID_W], o[GRID_W:]).astype(o_ref.dtype)
        return carry

    lax.fori_loop(0, rows // group, body, 0)


def _na_bias_table(rpb):
    qc = np.arange(GRID_W)[:, None]
    kc = np.arange(GRID_W)[None, :]
    qstart = np.clip(qc - NA_COLS // 2, 0, GRID_W - NA_COLS)
    valid = (kc >= qstart) & (kc < qstart + NA_COLS)
    coff = np.clip(kc - qc, -(NA_COLS - 1), NA_COLS - 1) + NA_COLS - 1
    pick = (coff[:, :, None] == np.arange(2 * NA_COLS - 1)).astype(np.float32)
    t = jnp.einsum("hdj,qkj->hdqk", rpb.astype(F32), jnp.asarray(pick),
                   precision=lax.Precision.HIGHEST)
    t = jnp.where(jnp.asarray(valid), t * LOG2E, NEG_INF)
    t = jnp.concatenate([t[:, :-1], t[:, 1:]], axis=-1)
    n_off = 2 * NA_ROWS - 2
    t = t.reshape(C_HEADS // 2, 2, n_off, GRID_W, 2 * GRID_W).transpose(0, 2, 1, 3, 4)
    return t.reshape(C_HEADS // 2, n_off, 2 * GRID_W, 2 * GRID_W)


def _na_attention(qkv, bias):
    b, s, n_cols = qkv.shape
    n_pairs = C_HEADS // 2
    rows = s // GRID_W
    return pl.pallas_call(
        functools.partial(_na_kernel, rows=rows),
        grid=(b, n_pairs),
        in_specs=[
            pl.BlockSpec((None, s, LANES), lambda bi, c: (bi, 0, c)),
            pl.BlockSpec((None, s, LANES), lambda bi, c: (bi, 0, n_pairs + c)),
            pl.BlockSpec((None, s, LANES), lambda bi, c: (bi, 0, 2 * n_pairs + c)),
            pl.BlockSpec((None,) + bias.shape[1:], lambda bi, c: (c, 0, 0, 0)),
        ],
        out_specs=pl.BlockSpec((None, s, LANES), lambda bi, c: (bi, 0, c)),
        out_shape=jax.ShapeDtypeStruct((b, s, n_pairs * LANES), BF16),
        compiler_params=pltpu.CompilerParams(
            dimension_semantics=("arbitrary", "arbitrary"), vmem_limit_bytes=VMEM_LIMIT),
        name="na_attn",
    )(qkv, qkv, qkv, bias)


def _silu(g):
    return g * (1.0 / (1.0 + jnp.exp(-g)))


def _ffn_kernel(x_ref, oa_ref, ob_ref, wa_ref, wb_ref, g_ref, wg_ref, wu_ref, wd_ref,
                c0_ref, c1_ref, c2_ref, y_ref, n0_ref, n1_ref, n2_ref, a_scr):
    n0_ref[...] = c0_ref[...].astype(n0_ref.dtype)
    n1_ref[...] = c1_ref[...].astype(n1_ref.dtype)
    n2_ref[...] = c2_ref[...].astype(n2_ref.dtype)
    x = (x_ref[...] + jnp.dot(oa_ref[...], wa_ref[...], preferred_element_type=F32)
         + jnp.dot(ob_ref[...], wb_ref[...], preferred_element_type=F32))
    h = _rms_rows(x, g_ref[...]).astype(BF16)
    f = wg_ref.shape[1]
    for j in range(f // MXU_N):
        sl = slice(j * MXU_N, (j + 1) * MXU_N)
        gate = jnp.dot(h, wg_ref[:, sl], preferred_element_type=F32)
        up = jnp.dot(h, wu_ref[:, sl], preferred_element_type=F32)
        a_scr[:, sl] = (_silu(gate) * up).astype(BF16)
    y_ref[...] = x + jnp.dot(a_scr[...], wd_ref[...], preferred_element_type=F32)


def _ffn(x, oa, ob, wa, wb, g, wg, wu, wd, narrow, tm=256):
    t, d = x.shape
    f = wg.shape[1]
    steps = t // tm
    const = lambda a: pl.BlockSpec(a.shape, lambda i: (0, 0), pipeline_mode=pl.Buffered(1))
    row = lambda w: pl.BlockSpec((tm, w), lambda i: (i, 0))
    slab = lambda a: pl.BlockSpec((a.shape[0] // steps, a.shape[1]), lambda i: (i, 0))
    assert all(a.shape[0] % (16 * steps) == 0 for a in narrow)
    return pl.pallas_call(
        _ffn_kernel,
        grid=(steps,),
        in_specs=[row(d), row(oa.shape[1]), row(ob.shape[1]), const(wa), const(wb),
                  pl.BlockSpec((1, d), lambda i: (0, 0)),
                  const(wg), const(wu), const(wd)] + [slab(a) for a in narrow],
        out_specs=[pl.BlockSpec((tm, d), lambda i: (i, 0))] + [slab(a) for a in narrow],
        out_shape=[jax.ShapeDtypeStruct((t, d), F32)]
        + [jax.ShapeDtypeStruct(a.shape, BF16) for a in narrow],
        scratch_shapes=[pltpu.VMEM((tm, f), BF16)],
        compiler_params=pltpu.CompilerParams(
            dimension_semantics=("arbitrary",), vmem_limit_bytes=VMEM_LIMIT),
        name="dense_swiglu",
    )(x, oa, ob, wa, wb, g.reshape(1, d), wg, wu, wd, *narrow)


TOP_K = 2
MOE_TB = 512
MOE_TM = 512
MOE_RC = 256
KEY_STRIDE = 65536.0


def _router_kernel(x_ref, o_ref, wo_ref, g_ref, wr_hi_ref, wr_lo_ref, x1_ref, h_ref, info_ref,
                   info_t_ref, cnt_ref, carry_scr, ltri_scr):
    i = pl.program_id(0)
    tb = x_ref.shape[0]
    parts = 2
    hb = tb // parts
    rows = [slice(p * hb, (p + 1) * hb) for p in range(parts)]
    lane = lax.broadcasted_iota(jnp.int32, (1, LANES), 1)

    @pl.when(i == 0)
    def _():
        carry_scr[...] = jnp.zeros_like(carry_scr)
        r = lax.broadcasted_iota(jnp.int32, (hb, hb), 0)
        cidx = lax.broadcasted_iota(jnp.int32, (hb, hb), 1)
        ltri_scr[...] = jnp.where(cidx < r, 1.0, 0.0).astype(BF16)

    x1s = [x_ref[rw, :] + jnp.dot(o_ref[rw, :], wo_ref[...], preferred_element_type=F32)
           for rw in rows]
    hs = [_rms_rows(x1, g_ref[...]) for x1 in x1s]
    his = [h.astype(BF16) for h in hs]
    los = [(h - hi.astype(F32)).astype(BF16) for h, hi in zip(hs, his)]
    logits_all = [jnp.dot(hi, wr_hi_ref[...], preferred_element_type=F32)
                  + jnp.dot(lo_, wr_hi_ref[...], preferred_element_type=F32)
                  + jnp.dot(hi, wr_lo_ref[...], preferred_element_type=F32)
                  for hi, lo_ in zip(his, los)]
    for rw, x1, hi in zip(rows, x1s, his):
        x1_ref[rw, :] = x1
        h_ref[rw, :] = hi

    picks = []
    for logits in logits_all:
        logits = jnp.where(lane < N_EXPERTS, logits, NEG_INF)
        v1 = jnp.max(logits, axis=-1, keepdims=True)
        i1 = jnp.min(jnp.where(logits == v1, lane, LANES), axis=-1, keepdims=True)
        rest = jnp.where(lane == i1, NEG_INF, logits)
        v2 = jnp.max(rest, axis=-1, keepdims=True)
        i2 = jnp.min(jnp.where(rest == v2, lane, LANES), axis=-1, keepdims=True)
        e2 = jnp.exp(v2 - v1)
        g1 = 1.0 / (1.0 + e2)
        picks.append((i1, i2, g1, e2 * g1))

    pickeds = [jnp.where(lane == i1, 1.0, jnp.where(lane == i2, 1.0, 0.0))
               for i1, i2, _, _ in picks]
    inside = [jnp.dot(ltri_scr[...], pk.astype(BF16), preferred_element_type=F32)
              for pk in pickeds]
    counts = [jnp.sum(pk, axis=0, keepdims=True) for pk in pickeds]
    seen = carry_scr[...]
    for p, (i1, i2, g1, g2) in enumerate(picks):
        before = inside[p] + seen
        rank1 = jnp.sum(jnp.where(lane == i1, before, 0.0), axis=-1, keepdims=True)
        rank2 = jnp.sum(jnp.where(lane == i2, before, 0.0), axis=-1, keepdims=True)
        key1 = i1.astype(F32) * KEY_STRIDE + rank1
        key2 = i2.astype(F32) * KEY_STRIDE + rank2
        info = jnp.where(lane == 0, key1, jnp.where(lane == 1, key2,
                         jnp.where(lane == 2, g1, jnp.where(lane == 3, g2, 0.0))))
        info_ref[rows[p], :] = info
        info_t_ref[:, rows[p]] = info.T[:8, :]
        seen = seen + counts[p]
    cnt_ref[...] = jnp.broadcast_to(seen - carry_scr[...], cnt_ref.shape)
    carry_scr[...] = seen


def _router(x, o, wo, g, wr):
    t, d = x.shape
    nb = t // MOE_TB
    wr_hi = wr.astype(BF16)
    wr_lo = (wr - wr_hi.astype(F32)).astype(BF16)
    row = lambda w: pl.BlockSpec((MOE_TB, w), lambda i: (i, 0))
    const = lambda a: pl.BlockSpec(a.shape, lambda i: (0, 0))
    return pl.pallas_call(
        _router_kernel,
        grid=(nb,),
        in_specs=[row(d), row(o.shape[1]), const(wo), pl.BlockSpec((1, d), lambda i: (0, 0)),
                  const(wr_hi), const(wr_lo)],
        out_specs=[row(d), row(d), row(LANES),
                   pl.BlockSpec((None, 8, MOE_TB), lambda i: (i, 0, 0)),
                   pl.BlockSpec((None, 8, LANES), lambda i: (i, 0, 0))],
        out_shape=[jax.ShapeDtypeStruct((t, d), F32),
                   jax.ShapeDtypeStruct((t, d), BF16),
                   jax.ShapeDtypeStruct((t, LANES), F32),
                   jax.ShapeDtypeStruct((nb, 8, MOE_TB), F32),
                   jax.ShapeDtypeStruct((nb, 8, LANES), F32)],
        scratch_shapes=[pltpu.VMEM((1, LANES), F32),
                        pltpu.VMEM((MOE_TB // 2, MOE_TB // 2), BF16)],
        compiler_params=pltpu.CompilerParams(
            dimension_semantics=("arbitrary",), vmem_limit_bytes=VMEM_LIMIT),
        name="moe_router",
    )(x, o, wo, g.reshape(1, d), wr_hi, wr_lo)


def _moe_plan(block_counts, n_tokens):
    nb = block_counts.shape[0]
    n_tiles = (TOP_K * n_tokens + N_EXPERTS * (MOE_TM - 1)) // MOE_TM
    n_chunks = n_tiles * MOE_TM // MOE_RC
    tc = block_counts.astype(jnp.int32)
    cb = jnp.concatenate([jnp.zeros((1, N_EXPERTS), jnp.int32), jnp.cumsum(tc, axis=0)])
    counts = cb[-1]
    padded = ((counts + MOE_TM - 1) // MOE_TM) * MOE_TM
    ends = jnp.cumsum(padded)
    starts = ends - padded
    total = ends[-1]

    tiles = jnp.arange(n_tiles, dtype=jnp.int32)
    n_valid_tiles = total // MOE_TM
    t_valid = (tiles < n_valid_tiles).astype(jnp.int32)
    t_src = jnp.maximum(jnp.minimum(tiles, n_valid_tiles - 1), 0)
    t_exp = jnp.minimum(jnp.sum((t_src * MOE_TM)[:, None] >= ends[None, :], axis=1),
                        N_EXPERTS - 1).astype(jnp.int32)

    r0 = jnp.arange(n_chunks, dtype=jnp.int32) * MOE_RC
    ex = jnp.minimum(jnp.sum(r0[:, None] >= ends[None, :], axis=1), N_EXPERTS - 1)
    live = r0 < total
    lo = r0 - starts[ex]
    base = (ex * int(KEY_STRIDE) + lo).astype(jnp.int32)
    cbe = cb[:, ex]
    g_lo = jnp.sum(cbe[1:] <= lo[None, :], axis=0).astype(jnp.int32)
    g_cnt = jnp.sum(live[None, :] & (cbe[:-1] < (lo + MOE_RC)[None, :]) & (cbe[1:] > lo[None, :]),
                    axis=0).astype(jnp.int32)

    row_lo = starts[None, :] + cb[:-1]
    row_hi = starts[None, :] + cb[1:]
    c_first = row_lo // MOE_RC
    c_num = jnp.where(row_hi > row_lo, (row_hi - 1) // MOE_RC - c_first + 1, 0)
    c_end = jnp.cumsum(c_num, axis=1)
    slots = jnp.arange(COMBINE_MAX_CHUNKS, dtype=jnp.int32)
    e_of = jnp.minimum(jnp.sum(slots[None, :, None] >= c_end[:, None, :], axis=2),
                       N_EXPERTS - 1)
    before = jnp.take_along_axis(c_end - c_num, e_of, axis=1)
    c_c = jnp.take_along_axis(c_first, e_of, axis=1) + slots[None, :] - before
    c_cnt = c_end[:, -1].astype(jnp.int32)
    c_c = jnp.clip(c_c, 0, n_chunks - 1).astype(jnp.int32).reshape(-1)
    return dict(n_tiles=n_tiles, n_chunks=n_chunks, t_valid=t_valid, t_exp=t_exp,
                base=base, live=live.astype(jnp.int32), g_lo=g_lo, g_cnt=g_cnt,
                c_cnt=c_cnt, c_c=c_c)


def _gather_kernel(live_ref, lo_ref, cnt_ref, base_ref, info_t_ref, h_ref,
                   xs_ref, gs_ref, acc_scr, gsum_scr):
    tb = info_t_ref.shape[-1]
    per_tile = xs_ref.shape[0] // MOE_RC
    row = lax.broadcasted_iota(jnp.int32, (MOE_RC, tb), 0).astype(F32)

    def chunk_body(k, carry):
        c = pl.program_id(0) * per_tile + k
        dst = pl.ds(pl.multiple_of(k * MOE_RC, MOE_RC), MOE_RC)

        @pl.when(live_ref[c] == 1)
        def _():
            first = lo_ref[c]
            n = cnt_ref[c]
            acc_scr[...] = jnp.zeros_like(acc_scr)
            gsum_scr[...] = jnp.zeros_like(gsum_scr)

            def two_blocks(it, inner):
                rows, gate = None, None
                for w in range(2):
                    j = 2 * it + w
                    b = first + jnp.minimum(j, n - 1)
                    base = jnp.where(j < n, base_ref[c], NO_MATCH_BASE).astype(F32)
                    info = info_t_ref[b]
                    m1 = (info[0:1, :] - base) == row
                    m2 = (info[1:2, :] - base) == row
                    sel = jnp.where(m1, 1.0, jnp.where(m2, 1.0, 0.0)).astype(BF16)
                    g = jnp.sum(jnp.where(m1, info[2:3, :], jnp.where(m2, info[3:4, :], 0.0)),
                                axis=-1, keepdims=True)
                    r = jnp.dot(sel, h_ref[pl.ds(pl.multiple_of(b * tb, tb), tb), :],
                                preferred_element_type=F32)
                    rows = r if rows is None else rows + r
                    gate = g if gate is None else gate + g
                acc_scr[...] += rows
                gsum_scr[...] += gate
                return inner

            lax.fori_loop(0, (n + 1) // 2, two_blocks, 0)
            xs_ref[dst, :] = acc_scr[...].astype(xs_ref.dtype)
            gs_ref[dst, :] = jnp.broadcast_to(gsum_scr[...], (MOE_RC, gs_ref.shape[1]))

        @pl.when(live_ref[c] == 0)
        def _():
            xs_ref[dst, :] = jnp.zeros((MOE_RC, xs_ref.shape[1]), xs_ref.dtype)
            gs_ref[dst, :] = jnp.zeros((MOE_RC, gs_ref.shape[1]), gs_ref.dtype)

        return carry

    lax.fori_loop(0, per_tile, chunk_body, 0)


def _moe_gather(plan, info_t, h):
    t, d = h.shape
    rows = plan["n_chunks"] * MOE_RC
    tile = lambda i, live, lo, cn, ba: (i, 0)
    grid_spec = pltpu.PrefetchScalarGridSpec(
        num_scalar_prefetch=4,
        grid=(plan["n_tiles"],),
        in_specs=[pl.BlockSpec(info_t.shape, lambda c, live, lo, cn, ba: (0, 0, 0),
                               pipeline_mode=pl.Buffered(1)),
                  pl.BlockSpec((t, d), lambda c, live, lo, cn, ba: (0, 0),
                               pipeline_mode=pl.Buffered(1))],
        out_specs=[pl.BlockSpec((MOE_TM, d), tile), pl.BlockSpec((MOE_TM, LANES), tile)],
        scratch_shapes=[pltpu.VMEM((MOE_RC, d), F32), pltpu.VMEM((MOE_RC, 1), F32)],
    )
    return pl.pallas_call(
        _gather_kernel,
        grid_spec=grid_spec,
        out_shape=[jax.ShapeDtypeStruct((rows, d), BF16),
                   jax.ShapeDtypeStruct((rows, LANES), F32)],
        compiler_params=pltpu.CompilerParams(
            dimension_semantics=("arbitrary",), vmem_limit_bytes=VMEM_LIMIT),
        name="moe_gather",
    )(plan["live"], plan["g_lo"], plan["g_cnt"], plan["base"], info_t, h)


def _experts_kernel(exp_ref, val_ref, xs_ref, gs_ref, wg_ref, wu_ref, wd_ref, ys_ref,
                    acc_scr):
    i = pl.program_id(0)
    j = pl.program_id(1)

    last = j == pl.num_programs(1) - 1

    @pl.when(j == 0)
    def _():
        acc_scr[...] = jnp.zeros_like(acc_scr)

    @pl.when(val_ref[i] == 1)
    def _():
        x = xs_ref[...]
        gate = jnp.dot(x, wg_ref[...], preferred_element_type=F32)
        up = jnp.dot(x, wu_ref[...], preferred_element_type=F32)
        a = (_silu(gate) * up).astype(BF16)
        acc_scr[...] += jnp.dot(a, wd_ref[...], preferred_element_type=F32)

    @pl.when(last)
    def _():
        ys_ref[...] = (acc_scr[...] * gs_ref[:, 0:1]).astype(ys_ref.dtype)


def _moe_experts(plan, xs, gs, wg, wu, wd, tf):
    rows, d = xs.shape
    f = wg.shape[2]
    nf = f // tf

    def wcol(i, j, ex, val):
        return (ex[i], 0, jnp.where(val[i] == 1, j, nf - 1))

    def wrow(i, j, ex, val):
        return (ex[i], jnp.where(val[i] == 1, j, nf - 1), 0)

    tile = lambda i, j, ex, val: (i, 0)
    grid_spec = pltpu.PrefetchScalarGridSpec(
        num_scalar_prefetch=2,
        grid=(plan["n_tiles"], nf),
        in_specs=[pl.BlockSpec((MOE_TM, d), tile),
                  pl.BlockSpec((MOE_TM, LANES), tile),
                  pl.BlockSpec((None, d, tf), wcol),
                  pl.BlockSpec((None, d, tf), wcol),
                  pl.BlockSpec((None, tf, d), wrow)],
        out_specs=pl.BlockSpec((MOE_TM, d), tile),
        scratch_shapes=[pltpu.VMEM((MOE_TM, d), F32)],
    )
    return pl.pallas_call(
        _experts_kernel,
        grid_spec=grid_spec,
        out_shape=jax.ShapeDtypeStruct((rows, d), BF16),
        compiler_params=pltpu.CompilerParams(
            dimension_semantics=("arbitrary", "arbitrary"), vmem_limit_bytes=VMEM_LIMIT),
        name="moe_experts",
    )(plan["t_exp"], plan["t_valid"], xs, gs, wg, wu, wd)


NO_MATCH_BASE = -(2 ** 30)


COMBINE_GROUP = 4
COMBINE_MAX_CHUNKS = N_EXPERTS * (MOE_TB // MOE_RC + 1)


def _combine_kernel(cnt_ref, pc_ref, base_ref, info_ref, x_ref, ys_hbm, o_ref, ybuf, sem):
    b = pl.program_id(0)
    tb = x_ref.shape[0]
    d = x_ref.shape[1]

    def n_fetch(blk):
        return ((cnt_ref[blk] + COMBINE_GROUP - 1) // COMBINE_GROUP) * COMBINE_GROUP

    def chunk_of(blk, j):
        return pc_ref[blk * COMBINE_MAX_CHUNKS + jnp.minimum(j, cnt_ref[blk] - 1)]

    def copy(blk, j, slot):
        row0 = pl.multiple_of(chunk_of(blk, j) * MOE_RC, MOE_RC)
        return pltpu.make_async_copy(ys_hbm.at[pl.ds(row0, MOE_RC), :], ybuf.at[slot, j],
                                     sem.at[slot])

    def start_block(blk, slot):
        def issue(j, carry):
            copy(blk, j, slot).start()
            return carry
        lax.fori_loop(0, n_fetch(blk), issue, 0)

    @pl.when(b == 0)
    def _():
        start_block(0, 0)

    @pl.when(b + 1 < pl.num_programs(0))
    def _():
        start_block(b + 1, (b + 1) % 2)

    slot = b % 2
    n = cnt_ref[b]

    def drain(j, carry):
        copy(b, j, slot).wait()
        return carry

    lax.fori_loop(0, n_fetch(b), drain, 0)

    lane = lax.broadcasted_iota(jnp.int32, (1, LANES), 1)
    info = info_ref[...]
    k1 = jnp.sum(jnp.where(lane == 0, info, 0.0), axis=-1, keepdims=True)
    k2 = jnp.sum(jnp.where(lane == 1, info, 0.0), axis=-1, keepdims=True)
    col = lax.broadcasted_iota(jnp.int32, (tb, MOE_RC), 1).astype(F32)
    o_ref[...] = x_ref[...]

    def group(gi, carry):
        sels = []
        for w in range(COMBINE_GROUP):
            j = gi * COMBINE_GROUP + w
            base = jnp.where(j < n, base_ref[chunk_of(b, j)], NO_MATCH_BASE).astype(F32)
            sels.append(jnp.where(k1 - base == col, 1.0,
                                  jnp.where(k2 - base == col, 1.0, 0.0)).astype(BF16))
        rows = ybuf[slot, pl.ds(gi * COMBINE_GROUP, COMBINE_GROUP)]
        o_ref[...] += jnp.dot(jnp.concatenate(sels, axis=1),
                              rows.reshape(COMBINE_GROUP * MOE_RC, d),
                              preferred_element_type=F32)
        return carry

    lax.fori_loop(0, n_fetch(b) // COMBINE_GROUP, group, 0)


def _moe_combine(plan, info, ys, x):
    t, d = x.shape
    block = lambda b, cn, pc, ba: (b, 0)
    grid_spec = pltpu.PrefetchScalarGridSpec(
        num_scalar_prefetch=3,
        grid=(t // MOE_TB,),
        in_specs=[pl.BlockSpec((MOE_TB, LANES), block),
                  pl.BlockSpec((MOE_TB, d), block),
                  pl.BlockSpec(memory_space=pl.ANY)],
        out_specs=pl.BlockSpec((MOE_TB, d), block),
        scratch_shapes=[pltpu.VMEM((2, COMBINE_MAX_CHUNKS, MOE_RC, d), BF16),
                        pltpu.SemaphoreType.DMA((2,))],
    )
    return pl.pallas_call(
        _combine_kernel,
        grid_spec=grid_spec,
        out_shape=jax.ShapeDtypeStruct((t, d), F32),
        compiler_params=pltpu.CompilerParams(
            dimension_semantics=("arbitrary",), vmem_limit_bytes=VMEM_LIMIT),
        name="moe_combine",
    )(plan["c_cnt"], plan["c_c"], plan["base"], info, x, ys)


def _moe(x, o, wo, g, wr, wg, wu, wd, tf=1792):
    t, _ = x.shape
    x1, h, info, info_t, block_counts = _router(x, o, wo, g, wr)
    plan = _moe_plan(block_counts[:, 0, :N_EXPERTS], t)
    xs, gs = _moe_gather(plan, info_t, h)
    ys = _moe_experts(plan, xs, gs, wg, wu, wd, tf)
    return _moe_combine(plan, info, ys, x1)


def _alibi_slopes():
    s = np.exp2(-8.0 * np.arange(1, N_HEADS + 1) / N_HEADS).astype(np.float32)
    return s[0::2][:A_HEADS], s[1::2][:B_Q_HEADS]


_B_HEAD_ORDER = np.array([0, 4, 1, 5, 2, 6, 3, 7])


def _even_layer(x2d, batch, norm1, w_in, qn_a, kn_a, qn_b, kn_b, sink_b, w_out,
                norm2, w_gate, w_up, w_down, narrow):
    t, d = x2d.shape
    s = t // batch
    wa = A_HEADS * HEAD_DIM
    wq_b = B_Q_HEADS * HEAD_DIM
    wkv_b = B_KV_HEADS * HEAD_DIM
    qb0 = 3 * wa
    q_scale = HEAD_DIM ** -0.5 * LOG2E

    w_qb = w_in[:, qb0:qb0 + wq_b].reshape(d, B_Q_HEADS, HEAD_DIM)[:, _B_HEAD_ORDER]
    w_in_p = jnp.concatenate(
        [w_in[:, :qb0], w_qb.reshape(d, wq_b), w_in[:, qb0 + wq_b:]], axis=1).astype(BF16)
    ones = jnp.ones((HEAD_DIM,), F32)
    col_gain = jnp.concatenate([
        jnp.tile(qn_a * q_scale, A_HEADS), jnp.tile(kn_a, A_HEADS), jnp.tile(ones, A_HEADS),
        jnp.tile(qn_b * q_scale, B_Q_HEADS), jnp.tile(kn_b, B_KV_HEADS),
        jnp.tile(ones, B_KV_HEADS)])
    cpl = wa // LANES
    norm_chunks = ((True,) * (2 * cpl) + (False,) * cpl + (True,) * (wq_b // LANES)
                   + (True,) * (wkv_b // LANES) + (False,) * (wkv_b // LANES))
    qkv = _norm_proj(x2d, norm1, w_in_p, col_gain, norm_chunks)
    n_cols = qkv.shape[1]
    qkv3 = qkv.reshape(batch, s, n_cols)

    slopes_a, slopes_b = _alibi_slopes()
    par_a = jnp.asarray(slopes_a).reshape(1, A_HEADS)
    radii = {window // (2 * dil) for window, dil in DIL_CONFIGS}
    assert tuple(dil for _, dil in DIL_CONFIGS) == (1, 4, 16) and len(radii) == 1
    oa = _dilated_attention(qkv3, par_a, k_blk0=cpl, v_blk0=2 * cpl, n_pairs=cpl,
                            blk=A_BLOCK, radius=radii.pop())
    par_b = jnp.stack([jnp.asarray(slopes_b[_B_HEAD_ORDER]),
                       sink_b.astype(F32)[_B_HEAD_ORDER] * LOG2E])
    kb = 3 * cpl + wq_b // LANES
    ob = _window_attention(qkv3, par_b, q_blk0=3 * cpl, k_blk=kb, v_blk=kb + wkv_b // LANES,
                           n_pairs=wq_b // LANES, radius=B_WINDOW, blk=B_BLOCK)
    w_out_a = w_out[:wa].astype(BF16)
    w_out_b = w_out[wa:].reshape(B_Q_HEADS, HEAD_DIM, d)[_B_HEAD_ORDER].reshape(wq_b, d)
    y, *narrowed = _ffn(x2d, oa.reshape(t, wa), ob.reshape(t, wq_b), w_out_a,
                        w_out_b.astype(BF16), norm2, w_gate.astype(BF16), w_up.astype(BF16),
                        w_down.astype(BF16), narrow)
    return y, narrowed


def _odd_layer(x2d, batch, norm1, w_qkv, qn, kn, rpb, w_out, norm2, w_router,
               w_gate, w_up, w_down):
    t, d = x2d.shape
    s = t // batch
    wc = C_HEADS * HEAD_DIM
    ones = jnp.ones((HEAD_DIM,), F32)
    col_gain = jnp.concatenate([jnp.tile(qn * (HEAD_DIM ** -0.5 * LOG2E), C_HEADS),
                                jnp.tile(kn, C_HEADS), jnp.tile(ones, C_HEADS)])
    cpl = wc // LANES
    norm_chunks = (True,) * (2 * cpl) + (False,) * cpl
    qkv = _norm_proj(x2d, norm1, w_qkv.astype(BF16), col_gain, norm_chunks)
    o = _na_attention(qkv.reshape(batch, s, 3 * wc), _na_bias_table(rpb))
    wr = jnp.zeros((d, LANES), F32).at[:, :N_EXPERTS].set(w_router.astype(F32))
    return _moe(x2d, o.reshape(t, wc), w_out.astype(BF16), norm2, wr, w_gate, w_up, w_down)


def kernel(x, ev_norm1, ev_w_in, ev_qn_a, ev_kn_a, ev_qn_b, ev_kn_b, ev_sink_b, ev_w_out, ev_norm2, ev_ffn_gate, ev_ffn_up, ev_ffn_down, od_norm1, od_w_qkv, od_qn, od_kn, od_rpb, od_w_out, od_norm2, od_router, od_exp_gate, od_exp_up, od_exp_down):
    batch, s, d = x.shape
    depth = ev_norm1.shape[0] + od_norm1.shape[0]
    h = x.reshape(batch * s, d)
    n_e, _, f_e = od_exp_gate.shape[1:]
    experts = None
    for layer in range(depth):
        j = layer // 2
        if layer % 2 == 0:
            if j < od_norm1.shape[0]:
                narrow = [od_exp_gate[j].reshape(n_e * d, f_e), od_exp_up[j].reshape(n_e * d, f_e),
                          od_exp_down[j].reshape(n_e * f_e, d)]
            else:
                narrow = [jnp.zeros((batch * s // 16, LANES), F32)] * 3
            h, narrowed = _even_layer(h, batch, ev_norm1[j], ev_w_in[j], ev_qn_a[j], ev_kn_a[j],
                                      ev_qn_b[j], ev_kn_b[j], ev_sink_b[j], ev_w_out[j],
                                      ev_norm2[j], ev_ffn_gate[j], ev_ffn_up[j], ev_ffn_down[j],
                                      narrow)
            experts = (narrowed[0].reshape(n_e, d, f_e), narrowed[1].reshape(n_e, d, f_e),
                       narrowed[2].reshape(n_e, f_e, d))
        else:
            h = _odd_layer(h, batch, od_norm1[j], od_w_qkv[j], od_qn[j], od_kn[j], od_rpb[j],
                           od_w_out[j], od_norm2[j], od_router[j], *experts)
    return h.reshape(batch, s, d)
```

```python
import functools

import numpy as np
import jax
import jax.numpy as jnp
from jax import lax
from jax.experimental import pallas as pl
from jax.experimental.pallas import tpu as pltpu

D_MODEL = 1024
HEAD_DIM = 64
N_HEADS = D_MODEL // HEAD_DIM
A_HEADS = N_HEADS // 2
B_Q_HEADS = N_HEADS // 2
B_KV_HEADS = max(1, B_Q_HEADS // 4)
C_HEADS = N_HEADS
DIL_CONFIGS = ((128, 1), (512, 4), (2048, 16))
A_BLOCK = 128
B_WINDOW = 128
B_BLOCK = 128
GRID_W = 64
NA_ROWS = 8
NA_COLS = 16
N_EXPERTS = 8
RMS_EPS = 1e-6
NEG_INF = -1e30

LANES = 128
MXU_N = 256
VMEM_LIMIT = 56 * 1024 * 1024

ATTN_UNROLL = 8

F32 = jnp.float32
BF16 = jnp.bfloat16


def _lo_mask(shape):
    return lax.broadcasted_iota(jnp.int32, shape, len(shape) - 1) < HEAD_DIM


LOG2E = 1.4426950408889634


def _exp2_probs(s, m):
    return jnp.exp2(s - m).astype(BF16)


def _value_and_sum(p, v):
    ov = jnp.dot(p, jnp.concatenate([v, jnp.ones_like(v)], axis=1), preferred_element_type=F32)
    return ov[:, :LANES], ov[:, LANES:LANES + 1]


def _rms_rows(x, g):
    ms = jnp.mean(x * x, axis=-1, keepdims=True)
    return x * lax.rsqrt(ms + RMS_EPS) * g


def _norm_proj_kernel(x_ref, g_ref, w_ref, cg_ref, o_ref, *, norm_chunks):
    h = _rms_rows(x_ref[...], g_ref[...]).astype(BF16)
    n_out = o_ref.shape[-1]
    lo = _lo_mask((1, LANES))
    for j in range(n_out // MXU_N):
        y = jnp.dot(h, w_ref[:, j * MXU_N:(j + 1) * MXU_N], preferred_element_type=F32)
        for half in range(MXU_N // LANES):
            c = j * (MXU_N // LANES) + half
            yc = y[:, half * LANES:(half + 1) * LANES]
            if norm_chunks[c]:
                sq = yc * yc
                s_lo = jnp.sum(jnp.where(lo, sq, 0.0), axis=-1, keepdims=True)
                s_hi = jnp.sum(jnp.where(lo, 0.0, sq), axis=-1, keepdims=True)
                inv = jnp.where(lo, lax.rsqrt(s_lo * (1.0 / HEAD_DIM) + RMS_EPS),
                                lax.rsqrt(s_hi * (1.0 / HEAD_DIM) + RMS_EPS))
                yc = yc * inv * cg_ref[:, c * LANES:(c + 1) * LANES]
            o_ref[:, c * LANES:(c + 1) * LANES] = yc.astype(o_ref.dtype)


def _norm_proj(x, g, w, col_gain, norm_chunks, tm=512):
    t, d = x.shape
    n = w.shape[1]
    return pl.pallas_call(
        functools.partial(_norm_proj_kernel, norm_chunks=norm_chunks),
        grid=(t // tm,),
        in_specs=[
            pl.BlockSpec((tm, d), lambda i: (i, 0)),
            pl.BlockSpec((1, d), lambda i: (0, 0)),
            pl.BlockSpec((d, n), lambda i: (0, 0)),
            pl.BlockSpec((1, n), lambda i: (0, 0)),
        ],
        out_specs=pl.BlockSpec((tm, n), lambda i: (i, 0)),
        out_shape=jax.ShapeDtypeStruct((t, n), BF16),
        compiler_params=pltpu.CompilerParams(
            dimension_semantics=("arbitrary",), vmem_limit_bytes=VMEM_LIMIT),
        name="norm_proj",
    )(x, g.reshape(1, d), w, col_gain.reshape(1, n))


def _window_kernel(par_ref, q_ref, k_ref, v_ref, o_ref, bias_scr, *, seq, blk, radius):
    span = blk + 2 * radius
    c = pl.program_id(1)
    group = ATTN_UNROLL
    row_lo = lax.broadcasted_iota(jnp.int32, (2 * blk, 1), 0) < blk
    slope_col = jnp.where(row_lo, par_ref[0, 2 * c], par_ref[0, 2 * c + 1])
    sink_col = jnp.where(row_lo, par_ref[1, 2 * c], par_ref[1, 2 * c + 1])

    iq = lax.broadcasted_iota(jnp.int32, (2 * blk, span), 0) % blk
    ik = lax.broadcasted_iota(jnp.int32, (2 * blk, span), 1)
    _fill_band_bias(bias_scr, iq, ik, radius, 1, slope_col)
    lo_q = _lo_mask((1, LANES))
    lo = _lo_mask((blk, LANES))

    def body(g, carry):
        starts, variants, values, scores = [], [], [], []
        for u in range(group):
            a = pl.multiple_of((g * group + u) * blk, blk)
            ks = pl.multiple_of(jnp.clip(a - radius, 0, seq - span), radius)
            q = q_ref[pl.ds(a, blk), :]
            zero = jnp.zeros_like(q)
            qm = jnp.concatenate([jnp.where(lo_q, q, zero), jnp.where(lo_q, zero, q)], axis=0)
            k = k_ref[pl.ds(ks, span), :]
            scores.append(lax.dot_general(qm, k, (((1,), (1,)), ((), ())),
                                          preferred_element_type=F32))
            starts.append(a)
            variants.append((a - ks) // radius)
            values.append(v_ref[pl.ds(ks, span), :])
        probs, sink_terms = [], []
        for u in range(group):
            s = scores[u] + bias_scr[variants[u]]
            m = jnp.maximum(jnp.max(s, axis=-1, keepdims=True), sink_col)
            probs.append(_exp2_probs(s, m))
            sink_terms.append(jnp.exp2(sink_col - m))
        for u in range(group):
            o, l = _value_and_sum(probs[u], values[u])
            o = o * (1.0 / (l + sink_terms[u]))
            o_ref[pl.ds(starts[u], blk), :] = jnp.where(lo, o[:blk], o[blk:]).astype(o_ref.dtype)
        return carry

    lax.fori_loop(0, seq // (blk * group), body, 0)


def _window_attention(qkv, params, *, q_blk0, k_blk, v_blk, n_pairs, radius, blk):
    b, s, _ = qkv.shape
    span = blk + 2 * radius
    return pl.pallas_call(
        functools.partial(_window_kernel, seq=s, blk=blk, radius=radius),
        grid=(b, n_pairs),
        in_specs=[
            pl.BlockSpec(memory_space=pltpu.SMEM),
            pl.BlockSpec((None, s, LANES), lambda bi, c: (bi, 0, q_blk0 + c)),
            pl.BlockSpec((None, s, LANES), lambda bi, c: (bi, 0, k_blk)),
            pl.BlockSpec((None, s, LANES), lambda bi, c: (bi, 0, v_blk)),
        ],
        out_specs=pl.BlockSpec((None, s, LANES), lambda bi, c: (bi, 0, c)),
        out_shape=jax.ShapeDtypeStruct((b, s, n_pairs * LANES), BF16),
        scratch_shapes=[pltpu.VMEM((3, 2 * blk, span), F32)],
        compiler_params=pltpu.CompilerParams(
            dimension_semantics=("arbitrary", "arbitrary"), vmem_limit_bytes=VMEM_LIMIT),
        name="window_attn",
    )(params, qkv, qkv, qkv)


PERM_CHUNK = 256
PHASES4 = 4
PIECE = PERM_CHUNK // PHASES4


def _stack_heads(q):
    lo = _lo_mask((1, LANES))
    zero = jnp.zeros_like(q)
    return jnp.concatenate([jnp.where(lo, q, zero), jnp.where(lo, zero, q)], axis=0)


def _merge_heads(x, blk):
    x = jnp.broadcast_to(x, (2 * blk, LANES))
    return jnp.where(_lo_mask((blk, LANES)), x[:blk], x[blk:])


def _attend_group(qs, ks, vs, biases):
    scores = [lax.dot_general(_stack_heads(q), k, (((1,), (1,)), ((), ())),
                              preferred_element_type=F32) for q, k in zip(qs, ks)]
    stats = []
    for s, bias in zip(scores, biases):
        s = s + bias
        m = jnp.max(s, axis=-1, keepdims=True)
        stats.append((_exp2_probs(s, m), m))
    outs = []
    for (p, m), v in zip(stats, vs):
        o, l = _value_and_sum(p, v)
        outs.append((o, m, l))
    return outs


def _fill_band_bias(bias_scr, pos_q, pos_k, radius, pos_scale, slope_col):
    for var in range(3):
        dist = jnp.abs(pos_q + var * radius - pos_k)
        dist_f = (dist * pos_scale).astype(F32)
        bias_scr[var] = jnp.where(dist <= radius, -(slope_col * dist_f) * LOG2E, NEG_INF)


def _dilated_kernel(par_ref, q_ref, k_ref, v_ref, o_ref, qp, kp, vp, acc_p, m_p, l_p,
                    bias1, bias4, bias16, *, seq, blk, radius):
    span = blk + 2 * radius
    n_chunk = seq // PERM_CHUNK
    group = ATTN_UNROLL
    c = pl.program_id(1)
    row_lo = lax.broadcasted_iota(jnp.int32, (2 * blk, 1), 0) < blk
    slope_col = jnp.where(row_lo, par_ref[0, 2 * c], par_ref[0, 2 * c + 1])

    def piece_pos(x):
        return PIECE * (x // PIECE) + PHASES4 * (x % 16) + (x % PIECE) // 16

    iq = lax.broadcasted_iota(jnp.int32, (2 * blk, span), 0) % blk
    ik = lax.broadcasted_iota(jnp.int32, (2 * blk, span), 1)
    _fill_band_bias(bias1, iq, ik, radius, 1, slope_col)
    _fill_band_bias(bias4, piece_pos(iq), piece_pos(ik), radius, 4, slope_col)
    _fill_band_bias(bias16, iq, ik, radius, 16, slope_col)

    rr = lax.broadcasted_iota(jnp.int32, (PERM_CHUNK, PERM_CHUNK), 0)
    cc = lax.broadcasted_iota(jnp.int32, (PERM_CHUNK, PERM_CHUNK), 1)
    to_phase = jnp.where(cc == 16 * (rr % 16) + PHASES4 * ((rr // 16) % 4) + rr // PIECE,
                         1.0, 0.0).astype(BF16)
    to_token = jnp.where(cc == PIECE * (rr % PHASES4) + 16 * ((rr // PHASES4) % 4) + rr // 16,
                         1.0, 0.0).astype(BF16)

    def permute_body(ch2, carry):
        for half in range(2):
            ch = 2 * ch2 + half
            r0 = pl.multiple_of(ch * PERM_CHUNK, PERM_CHUNK)
            x = jnp.concatenate([q_ref[pl.ds(r0, PERM_CHUNK), :],
                                 k_ref[pl.ds(r0, PERM_CHUNK), :],
                                 v_ref[pl.ds(r0, PERM_CHUNK), :]], axis=1)
            y = jnp.dot(to_phase, x, preferred_element_type=F32).astype(BF16)
            for p4 in range(PHASES4):
                rows = slice(PIECE * p4, PIECE * (p4 + 1))
                qp[p4, ch] = y[rows, 0:LANES]
                kp[p4, ch] = y[rows, LANES:2 * LANES]
                vp[p4, ch] = y[rows, 2 * LANES:3 * LANES]
        return carry

    lax.fori_loop(0, n_chunk // 2, permute_body, 0)

    pieces_per_blk = blk // PIECE
    n_blk4 = seq // (PHASES4 * blk)

    def dil4_body(g, carry):
        p4 = g // (n_blk4 // group)
        j0 = (g % (n_blk4 // group)) * group
        qs, ks, vs, bs = [], [], [], []
        for u in range(group):
            pc = pieces_per_blk * (j0 + u)
            cs = jnp.clip(pc - radius // PIECE, 0, n_chunk - span // PIECE)
            qs.append(qp[p4, pl.ds(pc, pieces_per_blk)].reshape(blk, LANES))
            ks.append(kp[p4, pl.ds(cs, span // PIECE)].reshape(span, LANES))
            vs.append(vp[p4, pl.ds(cs, span // PIECE)].reshape(span, LANES))
            bs.append(bias4[(pc - cs) // (radius // PIECE)])
        for u, (o, m, l) in enumerate(_attend_group(qs, ks, vs, bs)):
            dst = (p4, pl.ds(pieces_per_blk * (j0 + u), pieces_per_blk))
            acc_p[dst] = _merge_heads(o, blk).reshape(pieces_per_blk, PIECE, LANES)
            m_p[dst] = _merge_heads(m, blk).reshape(pieces_per_blk, PIECE, LANES)
            l_p[dst] = _merge_heads(l, blk).reshape(pieces_per_blk, PIECE, LANES)
        return carry

    lax.fori_loop(0, PHASES4 * n_blk4 // group, dil4_body, 0)

    chunks_per_blk = blk // 16

    def dil16_body(p4, carry):
        qs, ks, vs, bs, dsts = [], [], [], [], []
        for mm in range(PIECE // 16):
            off = mm * 16
            kk = kp[p4, :, pl.ds(off, 16), :].reshape(span, LANES)
            vv = vp[p4, :, pl.ds(off, 16), :].reshape(span, LANES)
            for jb in range(n_chunk // chunks_per_blk):
                dst = (p4, pl.ds(chunks_per_blk * jb, chunks_per_blk), pl.ds(off, 16), slice(None))
                qs.append(qp[dst].reshape(blk, LANES))
                ks.append(kk)
                vs.append(vv)
                bs.append(bias16[jb * (blk // radius)])
                dsts.append(dst)
        for dst, (o, m, l) in zip(dsts, _attend_group(qs, ks, vs, bs)):
            m_old = m_p[dst].reshape(blk, LANES)
            m_new = jnp.maximum(m_old, _merge_heads(m, blk))
            w_old = jnp.exp2(m_old - m_new)
            w_new = jnp.exp2(_merge_heads(m, blk) - m_new)
            acc = w_old * acc_p[dst].reshape(blk, LANES) + w_new * _merge_heads(o, blk)
            den = w_old * l_p[dst].reshape(blk, LANES) + w_new * _merge_heads(l, blk)
            acc_p[dst] = acc.reshape(chunks_per_blk, 16, LANES)
            l_p[dst] = den.reshape(chunks_per_blk, 16, LANES)
            m_p[dst] = m_new.reshape(chunks_per_blk, 16, LANES)
        return carry

    lax.fori_loop(0, PHASES4, dil16_body, 0)

    blk_per_chunk = PERM_CHUNK // blk

    def dil1_body(g, carry):
        merged = []
        for cc2 in range(group // blk_per_chunk):
            ch = g * (group // blk_per_chunk) + cc2
            acc = jnp.concatenate([acc_p[p4, ch] for p4 in range(PHASES4)], axis=0)
            den = jnp.concatenate([l_p[p4, ch] for p4 in range(PHASES4)], axis=0)
            mx = jnp.concatenate([m_p[p4, ch] for p4 in range(PHASES4)], axis=0)
            o_ph = (acc * (1.0 / den)).astype(BF16)
            lse = mx + jnp.log2(den)
            hi = lse.astype(BF16)
            rem = lse - hi.astype(F32)
            mid = rem.astype(BF16)
            low = (rem - mid.astype(F32)).astype(BF16)
            back = jnp.dot(to_token, jnp.concatenate([o_ph, hi, mid, low], axis=1),
                           preferred_element_type=F32)
            merged.append((back[:, 0:LANES], back[:, LANES:2 * LANES]
                           + back[:, 2 * LANES:3 * LANES] + back[:, 3 * LANES:4 * LANES]))
        qs, ks, vs, bs, starts = [], [], [], [], []
        for u in range(group):
            a = pl.multiple_of((g * group + u) * blk, blk)
            kst = pl.multiple_of(jnp.clip(a - radius, 0, seq - span), radius)
            qs.append(q_ref[pl.ds(a, blk), :])
            ks.append(k_ref[pl.ds(kst, span), :])
            vs.append(v_ref[pl.ds(kst, span), :])
            bs.append(bias1[(a - kst) // radius])
            starts.append(a)
        for u, (o, m, l) in enumerate(_attend_group(qs, ks, vs, bs)):
            o1 = _merge_heads(o * (1.0 / l), blk)
            lse1 = _merge_heads(m + jnp.log2(l), blk)
            o2, lse2 = merged[u // blk_per_chunk]
            half = slice((u % blk_per_chunk) * blk, (u % blk_per_chunk + 1) * blk)
            o2, lse2 = o2[half], lse2[half]
            top = jnp.maximum(lse1, lse2)
            e1 = jnp.exp2(lse1 - top)
            e2 = jnp.exp2(lse2 - top)
            o_ref[pl.ds(starts[u], blk), :] = ((e1 * o1 + e2 * o2)
                                               * (1.0 / (e1 + e2))).astype(o_ref.dtype)
        return carry

    lax.fori_loop(0, seq // (blk * group), dil1_body, 0)


def _dilated_attention(qkv, params, *, k_blk0, v_blk0, n_pairs, blk, radius):
    b, s, _ = qkv.shape
    span = blk + 2 * radius
    n_chunk = s // PERM_CHUNK
    assert s // 16 == span and s % PERM_CHUNK == 0 and blk % PIECE == 0
    phase = lambda dt: pltpu.VMEM((PHASES4, n_chunk, PIECE, LANES), dt)
    return pl.pallas_call(
        functools.partial(_dilated_kernel, seq=s, blk=blk, radius=radius),
        grid=(b, n_pairs),
        in_specs=[
            pl.BlockSpec(memory_space=pltpu.SMEM),
            pl.BlockSpec((None, s, LANES), lambda bi, c: (bi, 0, c)),
            pl.BlockSpec((None, s, LANES), lambda bi, c: (bi, 0, k_blk0 + c)),
            pl.BlockSpec((None, s, LANES), lambda bi, c: (bi, 0, v_blk0 + c)),
        ],
        out_specs=pl.BlockSpec((None, s, LANES), lambda bi, c: (bi, 0, c)),
        out_shape=jax.ShapeDtypeStruct((b, s, n_pairs * LANES), BF16),
        scratch_shapes=[phase(BF16), phase(BF16), phase(BF16), phase(F32), phase(F32), phase(F32)]
        + [pltpu.VMEM((3, 2 * blk, span), F32)] * 3,
        compiler_params=pltpu.CompilerParams(
            dimension_semantics=("arbitrary", "arbitrary"), vmem_limit_bytes=VMEM_LIMIT),
        name="dilated_attn",
    )(params, qkv, qkv, qkv)


def _na_kernel(q_ref, k_ref, v_ref, bias_ref, o_ref, *, rows):
    kh = NA_ROWS
    group = ATTN_UNROLL
    lo_q = _lo_mask((1, LANES))
    lo = _lo_mask((GRID_W, LANES))

    def body(g, carry):
        starts, offs, values, scores = [], [], [], []
        for u in range(group):
            i = g * group + u
            rs = jnp.clip(i - kh // 2, 0, rows - kh)
            qs = pl.multiple_of(i * GRID_W, GRID_W)
            ks = pl.multiple_of(rs * GRID_W, GRID_W)
            q = q_ref[pl.ds(qs, GRID_W), :]
            zero = jnp.zeros_like(q)
            qm = jnp.concatenate([jnp.where(lo_q, q, zero), jnp.where(lo_q, zero, q)], axis=0)
            k = k_ref[pl.ds(ks, kh * GRID_W), :]
            scores.append(lax.dot_general(qm, k, (((1,), (1,)), ((), ())),
                                          preferred_element_type=F32))
            starts.append(qs)
            offs.append(rs - i + (NA_ROWS - 1))
            values.append(v_ref[pl.ds(ks, kh * GRID_W), :])
        probs = []
        for u in range(group):
            s = scores[u] + jnp.concatenate(
                [bias_ref[offs[u] + 2 * j] for j in range(kh // 2)], axis=1)
            probs.append(_exp2_probs(s, jnp.max(s, axis=-1, keepdims=True)))
        for u in range(group):
            o, l = _value_and_sum(probs[u], values[u])
            o = o * (1.0 / l)
            o_ref[pl.ds(starts[u], GRID_W), :] = jnp.where(
                lo, o[:GRID_W], o[GRID_W:]).astype(o_ref.dtype)
        return carry

    lax.fori_loop(0, rows // group, body, 0)


def _na_bias_table(rpb):
    qc = np.arange(GRID_W)[:, None]
    kc = np.arange(GRID_W)[None, :]
    qstart = np.clip(qc - NA_COLS // 2, 0, GRID_W - NA_COLS)
    valid = (kc >= qstart) & (kc < qstart + NA_COLS)
    coff = np.clip(kc - qc, -(NA_COLS - 1), NA_COLS - 1) + NA_COLS - 1
    pick = (coff[:, :, None] == np.arange(2 * NA_COLS - 1)).astype(np.float32)
    n_off = 2 * NA_ROWS - 2
    r = rpb.astype(F32) * LOG2E
    r = jnp.stack([r[:, :-1], r[:, 1:]], axis=2)
    r = r.reshape(C_HEADS // 2, 2, n_off, 2, 2 * NA_COLS - 1)
    t = jnp.einsum("phdsj,qkj->pdhqsk", r, jnp.asarray(pick), precision=lax.Precision.HIGHEST)
    t = jnp.where(jnp.asarray(valid)[:, None, :], t, NEG_INF)
    return t.reshape(C_HEADS // 2, n_off, 2 * GRID_W, 2 * GRID_W)


def _na_attention(qkv, bias):
    b, s, n_cols = qkv.shape
    n_pairs = C_HEADS // 2
    rows = s // GRID_W
    return pl.pallas_call(
        functools.partial(_na_kernel, rows=rows),
        grid=(b, n_pairs),
        in_specs=[
            pl.BlockSpec((None, s, LANES), lambda bi, c: (bi, 0, c)),
            pl.BlockSpec((None, s, LANES), lambda bi, c: (bi, 0, n_pairs + c)),
            pl.BlockSpec((None, s, LANES), lambda bi, c: (bi, 0, 2 * n_pairs + c)),
            pl.BlockSpec((None,) + bias.shape[1:], lambda bi, c: (c, 0, 0, 0)),
        ],
        out_specs=pl.BlockSpec((None, s, LANES), lambda bi, c: (bi, 0, c)),
        out_shape=jax.ShapeDtypeStruct((b, s, n_pairs * LANES), BF16),
        compiler_params=pltpu.CompilerParams(
            dimension_semantics=("arbitrary", "arbitrary"), vmem_limit_bytes=VMEM_LIMIT),
        name="na_attn",
    )(qkv, qkv, qkv, bias)


def _silu(g):
    return g * (1.0 / (1.0 + jnp.exp(-g)))


def _ffn_kernel(x_ref, oa_ref, ob_ref, wa_ref, wb_ref, g_ref, wg_ref, wu_ref, wd_ref,
                c0_ref, c1_ref, c2_ref, y_ref, n0_ref, n1_ref, n2_ref, a_scr):
    n0_ref[...] = c0_ref[...].astype(n0_ref.dtype)
    n1_ref[...] = c1_ref[...].astype(n1_ref.dtype)
    n2_ref[...] = c2_ref[...].astype(n2_ref.dtype)
    x = (x_ref[...] + jnp.dot(oa_ref[...], wa_ref[...], preferred_element_type=F32)
         + jnp.dot(ob_ref[...], wb_ref[...], preferred_element_type=F32))
    h = _rms_rows(x, g_ref[...]).astype(BF16)
    f = wg_ref.shape[1]
    for j in range(f // MXU_N):
        sl = slice(j * MXU_N, (j + 1) * MXU_N)
        gate = jnp.dot(h, wg_ref[:, sl], preferred_element_type=F32)
        up = jnp.dot(h, wu_ref[:, sl], preferred_element_type=F32)
        a_scr[:, sl] = (_silu(gate) * up).astype(BF16)
    y_ref[...] = x + jnp.dot(a_scr[...], wd_ref[...], preferred_element_type=F32)


def _ffn(x, oa, ob, wa, wb, g, wg, wu, wd, narrow, tm=256):
    t, d = x.shape
    f = wg.shape[1]
    steps = t // tm
    const = lambda a: pl.BlockSpec(a.shape, lambda i: (0, 0), pipeline_mode=pl.Buffered(1))
    row = lambda w: pl.BlockSpec((tm, w), lambda i: (i, 0))
    slab = lambda a: pl.BlockSpec((a.shape[0] // steps, a.shape[1]), lambda i: (i, 0))
    assert all(a.shape[0] % (16 * steps) == 0 for a in narrow)
    return pl.pallas_call(
        _ffn_kernel,
        grid=(steps,),
        in_specs=[row(d), row(oa.shape[1]), row(ob.shape[1]), const(wa), const(wb),
                  pl.BlockSpec((1, d), lambda i: (0, 0)),
                  const(wg), const(wu), const(wd)] + [slab(a) for a in narrow],
        out_specs=[pl.BlockSpec((tm, d), lambda i: (i, 0))] + [slab(a) for a in narrow],
        out_shape=[jax.ShapeDtypeStruct((t, d), F32)]
        + [jax.ShapeDtypeStruct(a.shape, BF16) for a in narrow],
        scratch_shapes=[pltpu.VMEM((tm, f), BF16)],
        compiler_params=pltpu.CompilerParams(
            dimension_semantics=("arbitrary",), vmem_limit_bytes=VMEM_LIMIT),
        name="dense_swiglu",
    )(x, oa, ob, wa, wb, g.reshape(1, d), wg, wu, wd, *narrow)


TOP_K = 2
MOE_TB = 512
MOE_TM = 512
MOE_RC = 256
KEY_STRIDE = 65536.0


def _router_kernel(x_ref, o_ref, wo_ref, g_ref, wr_hi_ref, wr_lo_ref, x1_ref, h_ref, info_ref,
                   info_t_ref, cnt_ref, carry_scr, ltri_scr):
    i = pl.program_id(0)
    tb = x_ref.shape[0]
    parts = 2
    hb = tb // parts
    rows = [slice(p * hb, (p + 1) * hb) for p in range(parts)]
    lane = lax.broadcasted_iota(jnp.int32, (1, LANES), 1)

    @pl.when(i == 0)
    def _():
        carry_scr[...] = jnp.zeros_like(carry_scr)
        r = lax.broadcasted_iota(jnp.int32, (hb, hb), 0)
        cidx = lax.broadcasted_iota(jnp.int32, (hb, hb), 1)
        ltri_scr[...] = jnp.where(cidx < r, 1.0, 0.0).astype(BF16)

    x1s = [x_ref[rw, :] + jnp.dot(o_ref[rw, :], wo_ref[...], preferred_element_type=F32)
           for rw in rows]
    hs = [_rms_rows(x1, g_ref[...]) for x1 in x1s]
    his = [h.astype(BF16) for h in hs]
    los = [(h - hi.astype(F32)).astype(BF16) for h, hi in zip(hs, his)]
    logits_all = [jnp.dot(hi, wr_hi_ref[...], preferred_element_type=F32)
                  + jnp.dot(lo_, wr_hi_ref[...], preferred_element_type=F32)
                  + jnp.dot(hi, wr_lo_ref[...], preferred_element_type=F32)
                  for hi, lo_ in zip(his, los)]
    for rw, x1, hi in zip(rows, x1s, his):
        x1_ref[rw, :] = x1
        h_ref[rw, :] = hi

    picks = []
    for logits in logits_all:
        logits = jnp.where(lane < N_EXPERTS, logits, NEG_INF)
        v1 = jnp.max(logits, axis=-1, keepdims=True)
        i1 = jnp.min(jnp.where(logits == v1, lane, LANES), axis=-1, keepdims=True)
        rest = jnp.where(lane == i1, NEG_INF, logits)
        v2 = jnp.max(rest, axis=-1, keepdims=True)
        i2 = jnp.min(jnp.where(rest == v2, lane, LANES), axis=-1, keepdims=True)
        e2 = jnp.exp(v2 - v1)
        g1 = 1.0 / (1.0 + e2)
        picks.append((i1, i2, g1, e2 * g1))

    pickeds = [jnp.where(lane == i1, 1.0, jnp.where(lane == i2, 1.0, 0.0))
               for i1, i2, _, _ in picks]
    inside = [jnp.dot(ltri_scr[...], pk.astype(BF16), preferred_element_type=F32)
              for pk in pickeds]
    counts = [jnp.sum(pk, axis=0, keepdims=True) for pk in pickeds]
    seen = carry_scr[...]
    for p, (i1, i2, g1, g2) in enumerate(picks):
        before = inside[p] + seen
        rank1 = jnp.sum(jnp.where(lane == i1, before, 0.0), axis=-1, keepdims=True)
        rank2 = jnp.sum(jnp.where(lane == i2, before, 0.0), axis=-1, keepdims=True)
        key1 = i1.astype(F32) * KEY_STRIDE + rank1
        key2 = i2.astype(F32) * KEY_STRIDE + rank2
        info = jnp.where(lane == 0, key1, jnp.where(lane == 1, key2,
                         jnp.where(lane == 2, g1, jnp.where(lane == 3, g2, 0.0))))
        info_ref[rows[p], :] = info
        info_t_ref[:, rows[p]] = info.T[:8, :]
        seen = seen + counts[p]
    cnt_ref[...] = jnp.broadcast_to(seen - carry_scr[...], cnt_ref.shape)
    carry_scr[...] = seen


def _router(x, o, wo, g, wr):
    t, d = x.shape
    nb = t // MOE_TB
    wr_hi = wr.astype(BF16)
    wr_lo = (wr - wr_hi.astype(F32)).astype(BF16)
    row = lambda w: pl.BlockSpec((MOE_TB, w), lambda i: (i, 0))
    const = lambda a: pl.BlockSpec(a.shape, lambda i: (0, 0))
    return pl.pallas_call(
        _router_kernel,
        grid=(nb,),
        in_specs=[row(d), row(o.shape[1]), const(wo), pl.BlockSpec((1, d), lambda i: (0, 0)),
                  const(wr_hi), const(wr_lo)],
        out_specs=[row(d), row(d), row(LANES),
                   pl.BlockSpec((None, 8, MOE_TB), lambda i: (i, 0, 0)),
                   pl.BlockSpec((None, 8, LANES), lambda i: (i, 0, 0))],
        out_shape=[jax.ShapeDtypeStruct((t, d), F32),
                   jax.ShapeDtypeStruct((t, d), BF16),
                   jax.ShapeDtypeStruct((t, LANES), F32),
                   jax.ShapeDtypeStruct((nb, 8, MOE_TB), F32),
                   jax.ShapeDtypeStruct((nb, 8, LANES), F32)],
        scratch_shapes=[pltpu.VMEM((1, LANES), F32),
                        pltpu.VMEM((MOE_TB // 2, MOE_TB // 2), BF16)],
        compiler_params=pltpu.CompilerParams(
            dimension_semantics=("arbitrary",), vmem_limit_bytes=VMEM_LIMIT),
        name="moe_router",
    )(x, o, wo, g.reshape(1, d), wr_hi, wr_lo)


def _moe_plan(block_counts, n_tokens):
    nb = block_counts.shape[0]
    n_tiles = (TOP_K * n_tokens + N_EXPERTS * (MOE_TM - 1)) // MOE_TM
    n_chunks = n_tiles * MOE_TM // MOE_RC
    tc = block_counts.astype(jnp.int32)
    cb = jnp.concatenate([jnp.zeros((1, N_EXPERTS), jnp.int32), jnp.cumsum(tc, axis=0)])
    counts = cb[-1]
    padded = ((counts + MOE_TM - 1) // MOE_TM) * MOE_TM
    ends = jnp.cumsum(padded)
    starts = ends - padded
    total = ends[-1]

    tiles = jnp.arange(n_tiles, dtype=jnp.int32)
    n_valid_tiles = total // MOE_TM
    t_valid = (tiles < n_valid_tiles).astype(jnp.int32)
    t_src = jnp.maximum(jnp.minimum(tiles, n_valid_tiles - 1), 0)
    t_exp = jnp.minimum(jnp.sum((t_src * MOE_TM)[:, None] >= ends[None, :], axis=1),
                        N_EXPERTS - 1).astype(jnp.int32)

    r0 = jnp.arange(n_chunks, dtype=jnp.int32) * MOE_RC
    ex = jnp.minimum(jnp.sum(r0[:, None] >= ends[None, :], axis=1), N_EXPERTS - 1)
    live = r0 < total
    lo = r0 - starts[ex]
    base = (ex * int(KEY_STRIDE) + lo).astype(jnp.int32)
    cbe = cb[:, ex]
    g_lo = jnp.sum(cbe[1:] <= lo[None, :], axis=0).astype(jnp.int32)
    g_cnt = jnp.sum(live[None, :] & (cbe[:-1] < (lo + MOE_RC)[None, :]) & (cbe[1:] > lo[None, :]),
                    axis=0).astype(jnp.int32)

    row_lo = starts[None, :] + cb[:-1]
    row_hi = starts[None, :] + cb[1:]
    c_first = row_lo // MOE_RC
    c_num = jnp.where(row_hi > row_lo, (row_hi - 1) // MOE_RC - c_first + 1, 0)
    c_end = jnp.cumsum(c_num, axis=1)
    slots = jnp.arange(COMBINE_MAX_CHUNKS, dtype=jnp.int32)
    e_of = jnp.minimum(jnp.sum(slots[None, :, None] >= c_end[:, None, :], axis=2),
                       N_EXPERTS - 1)
    owner = e_of[:, :, None] == jnp.arange(N_EXPERTS)[None, None, :]
    c_c = slots[None, :] + jnp.sum(
        jnp.where(owner, (c_first - (c_end - c_num))[:, None, :], 0), axis=2)
    c_cnt = c_end[:, -1].astype(jnp.int32)
    c_c = jnp.clip(c_c, 0, n_chunks - 1).astype(jnp.int32).reshape(-1)
    return dict(n_tiles=n_tiles, n_chunks=n_chunks, t_valid=t_valid, t_exp=t_exp,
                base=base, live=live.astype(jnp.int32), g_lo=g_lo, g_cnt=g_cnt,
                c_cnt=c_cnt, c_c=c_c)


def _gather_kernel(live_ref, lo_ref, cnt_ref, base_ref, info_t_ref, h_ref,
                   xs_ref, gs_ref, acc_scr, gsum_scr):
    tb = info_t_ref.shape[-1]
    per_tile = xs_ref.shape[0] // MOE_RC
    row = lax.broadcasted_iota(jnp.int32, (MOE_RC, tb), 0).astype(F32)

    def chunk_body(k, carry):
        c = pl.program_id(0) * per_tile + k
        dst = pl.ds(pl.multiple_of(k * MOE_RC, MOE_RC), MOE_RC)

        @pl.when(live_ref[c] == 1)
        def _():
            first = lo_ref[c]
            n = cnt_ref[c]
            acc_scr[...] = jnp.zeros_like(acc_scr)
            gsum_scr[...] = jnp.zeros_like(gsum_scr)

            def two_blocks(it, inner):
                rows, gate = None, None
                for w in range(2):
                    j = 2 * it + w
                    b = first + jnp.minimum(j, n - 1)
                    base = jnp.where(j < n, base_ref[c], NO_MATCH_BASE).astype(F32)
                    info = info_t_ref[b]
                    m1 = (info[0:1, :] - base) == row
                    m2 = (info[1:2, :] - base) == row
                    sel = jnp.where(m1, 1.0, jnp.where(m2, 1.0, 0.0)).astype(BF16)
                    g = jnp.sum(jnp.where(m1, info[2:3, :], jnp.where(m2, info[3:4, :], 0.0)),
                                axis=-1, keepdims=True)
                    r = jnp.dot(sel, h_ref[pl.ds(pl.multiple_of(b * tb, tb), tb), :],
                                preferred_element_type=F32)
                    rows = r if rows is None else rows + r
                    gate = g if gate is None else gate + g
                acc_scr[...] += rows
                gsum_scr[...] += gate
                return inner

            lax.fori_loop(0, (n + 1) // 2, two_blocks, 0)
            xs_ref[dst, :] = acc_scr[...].astype(xs_ref.dtype)
            gs_ref[dst, :] = jnp.broadcast_to(gsum_scr[...], (MOE_RC, gs_ref.shape[1]))

        @pl.when(live_ref[c] == 0)
        def _():
            xs_ref[dst, :] = jnp.zeros((MOE_RC, xs_ref.shape[1]), xs_ref.dtype)
            gs_ref[dst, :] = jnp.zeros((MOE_RC, gs_ref.shape[1]), gs_ref.dtype)

        return carry

    lax.fori_loop(0, per_tile, chunk_body, 0)


def _moe_gather(plan, info_t, h):
    t, d = h.shape
    rows = plan["n_chunks"] * MOE_RC
    tile = lambda i, live, lo, cn, ba: (i, 0)
    grid_spec = pltpu.PrefetchScalarGridSpec(
        num_scalar_prefetch=4,
        grid=(plan["n_tiles"],),
        in_specs=[pl.BlockSpec(info_t.shape, lambda c, live, lo, cn, ba: (0, 0, 0),
                               pipeline_mode=pl.Buffered(1)),
                  pl.BlockSpec((t, d), lambda c, live, lo, cn, ba: (0, 0),
                               pipeline_mode=pl.Buffered(1))],
        out_specs=[pl.BlockSpec((MOE_TM, d), tile), pl.BlockSpec((MOE_TM, LANES), tile)],
        scratch_shapes=[pltpu.VMEM((MOE_RC, d), F32), pltpu.VMEM((MOE_RC, 1), F32)],
    )
    return pl.pallas_call(
        _gather_kernel,
        grid_spec=grid_spec,
        out_shape=[jax.ShapeDtypeStruct((rows, d), BF16),
                   jax.ShapeDtypeStruct((rows, LANES), F32)],
        compiler_params=pltpu.CompilerParams(
            dimension_semantics=("arbitrary",), vmem_limit_bytes=VMEM_LIMIT),
        name="moe_gather",
    )(plan["live"], plan["g_lo"], plan["g_cnt"], plan["base"], info_t, h)


def _experts_kernel(exp_ref, val_ref, xs_ref, gs_ref, wg_ref, wu_ref, wd_ref, ys_ref,
                    acc_scr):
    i = pl.program_id(0)
    j = pl.program_id(1)

    last = j == pl.num_programs(1) - 1

    @pl.when(j == 0)
    def _():
        acc_scr[...] = jnp.zeros_like(acc_scr)

    @pl.when(val_ref[i] == 1)
    def _():
        x = xs_ref[...]
        gate = jnp.dot(x, wg_ref[...], preferred_element_type=F32)
        up = jnp.dot(x, wu_ref[...], preferred_element_type=F32)
        a = (_silu(gate) * up).astype(BF16)
        acc_scr[...] += jnp.dot(a, wd_ref[...], preferred_element_type=F32)

    @pl.when(last)
    def _():
        ys_ref[...] = (acc_scr[...] * gs_ref[:, 0:1]).astype(ys_ref.dtype)


def _moe_experts(plan, xs, gs, wg, wu, wd, tf):
    rows, d = xs.shape
    f = wg.shape[2]
    nf = f // tf

    def wcol(i, j, ex, val):
        return (ex[i], 0, jnp.where(val[i] == 1, j, nf - 1))

    def wrow(i, j, ex, val):
        return (ex[i], jnp.where(val[i] == 1, j, nf - 1), 0)

    tile = lambda i, j, ex, val: (i, 0)
    grid_spec = pltpu.PrefetchScalarGridSpec(
        num_scalar_prefetch=2,
        grid=(plan["n_tiles"], nf),
        in_specs=[pl.BlockSpec((MOE_TM, d), tile),
                  pl.BlockSpec((MOE_TM, LANES), tile),
                  pl.BlockSpec((None, d, tf), wcol),
                  pl.BlockSpec((None, d, tf), wcol),
                  pl.BlockSpec((None, tf, d), wrow)],
        out_specs=pl.BlockSpec((MOE_TM, d), tile),
        scratch_shapes=[pltpu.VMEM((MOE_TM, d), F32)],
    )
    return pl.pallas_call(
        _experts_kernel,
        grid_spec=grid_spec,
        out_shape=jax.ShapeDtypeStruct((rows, d), BF16),
        compiler_params=pltpu.CompilerParams(
            dimension_semantics=("arbitrary", "arbitrary"), vmem_limit_bytes=VMEM_LIMIT),
        name="moe_experts",
    )(plan["t_exp"], plan["t_valid"], xs, gs, wg, wu, wd)


NO_MATCH_BASE = -(2 ** 30)


COMBINE_GROUP = 4
COMBINE_MAX_CHUNKS = N_EXPERTS * (MOE_TB // MOE_RC + 1)


def _combine_kernel(cnt_ref, pc_ref, base_ref, info_ref, x_ref, ys_hbm, o_ref, ybuf, sem):
    b = pl.program_id(0)
    tb = x_ref.shape[0]
    d = x_ref.shape[1]

    def n_fetch(blk):
        return ((cnt_ref[blk] + COMBINE_GROUP - 1) // COMBINE_GROUP) * COMBINE_GROUP

    def chunk_of(blk, j):
        return pc_ref[blk * COMBINE_MAX_CHUNKS + jnp.minimum(j, cnt_ref[blk] - 1)]

    def copy(blk, j, slot):
        row0 = pl.multiple_of(chunk_of(blk, j) * MOE_RC, MOE_RC)
        return pltpu.make_async_copy(ys_hbm.at[pl.ds(row0, MOE_RC), :], ybuf.at[slot, j],
                                     sem.at[slot])

    def start_block(blk, slot):
        def issue(j, carry):
            copy(blk, j, slot).start()
            return carry
        lax.fori_loop(0, n_fetch(blk), issue, 0)

    @pl.when(b == 0)
    def _():
        start_block(0, 0)

    @pl.when(b + 1 < pl.num_programs(0))
    def _():
        start_block(b + 1, (b + 1) % 2)

    slot = b % 2
    n = cnt_ref[b]

    def drain(j, carry):
        copy(b, j, slot).wait()
        return carry

    lax.fori_loop(0, n_fetch(b), drain, 0)

    lane = lax.broadcasted_iota(jnp.int32, (1, LANES), 1)
    info = info_ref[...]
    k1 = jnp.sum(jnp.where(lane == 0, info, 0.0), axis=-1, keepdims=True)
    k2 = jnp.sum(jnp.where(lane == 1, info, 0.0), axis=-1, keepdims=True)
    col = lax.broadcasted_iota(jnp.int32, (tb, MOE_RC), 1).astype(F32)
    o_ref[...] = x_ref[...]

    def group(gi, carry):
        sels = []
        for w in range(COMBINE_GROUP):
            j = gi * COMBINE_GROUP + w
            base = jnp.where(j < n, base_ref[chunk_of(b, j)], NO_MATCH_BASE).astype(F32)
            sels.append(jnp.where(k1 - base == col, 1.0,
                                  jnp.where(k2 - base == col, 1.0, 0.0)).astype(BF16))
        rows = ybuf[slot, pl.ds(gi * COMBINE_GROUP, COMBINE_GROUP)]
        o_ref[...] += jnp.dot(jnp.concatenate(sels, axis=1),
                              rows.reshape(COMBINE_GROUP * MOE_RC, d),
                              preferred_element_type=F32)
        return carry

    lax.fori_loop(0, n_fetch(b) // COMBINE_GROUP, group, 0)


def _moe_combine(plan, info, ys, x):
    t, d = x.shape
    block = lambda b, cn, pc, ba: (b, 0)
    grid_spec = pltpu.PrefetchScalarGridSpec(
        num_scalar_prefetch=3,
        grid=(t // MOE_TB,),
        in_specs=[pl.BlockSpec((MOE_TB, LANES), block),
                  pl.BlockSpec((MOE_TB, d), block),
                  pl.BlockSpec(memory_space=pl.ANY)],
        out_specs=pl.BlockSpec((MOE_TB, d), block),
        scratch_shapes=[pltpu.VMEM((2, COMBINE_MAX_CHUNKS, MOE_RC, d), BF16),
                        pltpu.SemaphoreType.DMA((2,))],
    )
    return pl.pallas_call(
        _combine_kernel,
        grid_spec=grid_spec,
        out_shape=jax.ShapeDtypeStruct((t, d), F32),
        compiler_params=pltpu.CompilerParams(
            dimension_semantics=("arbitrary",), vmem_limit_bytes=VMEM_LIMIT),
        name="moe_combine",
    )(plan["c_cnt"], plan["c_c"], plan["base"], info, x, ys)


def _moe(x, o, wo, g, wr, wg, wu, wd, tf=1792):
    t, _ = x.shape
    x1, h, info, info_t, block_counts = _router(x, o, wo, g, wr)
    plan = _moe_plan(block_counts[:, 0, :N_EXPERTS], t)
    xs, gs = _moe_gather(plan, info_t, h)
    ys = _moe_experts(plan, xs, gs, wg, wu, wd, tf)
    return _moe_combine(plan, info, ys, x1)


def _alibi_slopes():
    s = np.exp2(-8.0 * np.arange(1, N_HEADS + 1) / N_HEADS).astype(np.float32)
    return s[0::2][:A_HEADS], s[1::2][:B_Q_HEADS]


_B_HEAD_ORDER = np.array([0, 4, 1, 5, 2, 6, 3, 7])


def _even_layer(x2d, batch, norm1, w_in, qn_a, kn_a, qn_b, kn_b, sink_b, w_out,
                norm2, w_gate, w_up, w_down, narrow):
    t, d = x2d.shape
    s = t // batch
    wa = A_HEADS * HEAD_DIM
    wq_b = B_Q_HEADS * HEAD_DIM
    wkv_b = B_KV_HEADS * HEAD_DIM
    qb0 = 3 * wa
    q_scale = HEAD_DIM ** -0.5 * LOG2E

    w_qb = w_in[:, qb0:qb0 + wq_b].reshape(d, B_Q_HEADS, HEAD_DIM)[:, _B_HEAD_ORDER]
    w_in_p = jnp.concatenate(
        [w_in[:, :qb0], w_qb.reshape(d, wq_b), w_in[:, qb0 + wq_b:]], axis=1).astype(BF16)
    ones = jnp.ones((HEAD_DIM,), F32)
    col_gain = jnp.concatenate([
        jnp.tile(qn_a * q_scale, A_HEADS), jnp.tile(kn_a, A_HEADS), jnp.tile(ones, A_HEADS),
        jnp.tile(qn_b * q_scale, B_Q_HEADS), jnp.tile(kn_b, B_KV_HEADS),
        jnp.tile(ones, B_KV_HEADS)])
    cpl = wa // LANES
    norm_chunks = ((True,) * (2 * cpl) + (False,) * cpl + (True,) * (wq_b // LANES)
                   + (True,) * (wkv_b // LANES) + (False,) * (wkv_b // LANES))
    qkv = _norm_proj(x2d, norm1, w_in_p, col_gain, norm_chunks)
    n_cols = qkv.shape[1]
    qkv3 = qkv.reshape(batch, s, n_cols)

    slopes_a, slopes_b = _alibi_slopes()
    par_a = jnp.asarray(slopes_a).reshape(1, A_HEADS)
    radii = {window // (2 * dil) for window, dil in DIL_CONFIGS}
    assert tuple(dil for _, dil in DIL_CONFIGS) == (1, 4, 16) and len(radii) == 1
    oa = _dilated_attention(qkv3, par_a, k_blk0=cpl, v_blk0=2 * cpl, n_pairs=cpl,
                            blk=A_BLOCK, radius=radii.pop())
    par_b = jnp.stack([jnp.asarray(slopes_b[_B_HEAD_ORDER]),
                       sink_b.astype(F32)[_B_HEAD_ORDER] * LOG2E])
    kb = 3 * cpl + wq_b // LANES
    ob = _window_attention(qkv3, par_b, q_blk0=3 * cpl, k_blk=kb, v_blk=kb + wkv_b // LANES,
                           n_pairs=wq_b // LANES, radius=B_WINDOW, blk=B_BLOCK)
    w_out_a = w_out[:wa].astype(BF16)
    w_out_b = w_out[wa:].reshape(B_Q_HEADS, HEAD_DIM, d)[_B_HEAD_ORDER].reshape(wq_b, d)
    y, *narrowed = _ffn(x2d, oa.reshape(t, wa), ob.reshape(t, wq_b), w_out_a,
                        w_out_b.astype(BF16), norm2, w_gate.astype(BF16), w_up.astype(BF16),
                        w_down.astype(BF16), narrow)
    return y, narrowed


def _odd_layer(x2d, batch, norm1, w_qkv, qn, kn, rpb, w_out, norm2, w_router,
               w_gate, w_up, w_down):
    t, d = x2d.shape
    s = t // batch
    wc = C_HEADS * HEAD_DIM
    ones = jnp.ones((HEAD_DIM,), F32)
    col_gain = jnp.concatenate([jnp.tile(qn * (HEAD_DIM ** -0.5 * LOG2E), C_HEADS),
                                jnp.tile(kn, C_HEADS), jnp.tile(ones, C_HEADS)])
    cpl = wc // LANES
    norm_chunks = (True,) * (2 * cpl) + (False,) * cpl
    qkv = _norm_proj(x2d, norm1, w_qkv.astype(BF16), col_gain, norm_chunks)
    o = _na_attention(qkv.reshape(batch, s, 3 * wc), _na_bias_table(rpb))
    wr = jnp.zeros((d, LANES), F32).at[:, :N_EXPERTS].set(w_router.astype(F32))
    return _moe(x2d, o.reshape(t, wc), w_out.astype(BF16), norm2, wr, w_gate, w_up, w_down)


def kernel(x, ev_norm1, ev_w_in, ev_qn_a, ev_kn_a, ev_qn_b, ev_kn_b, ev_sink_b, ev_w_out, ev_norm2, ev_ffn_gate, ev_ffn_up, ev_ffn_down, od_norm1, od_w_qkv, od_qn, od_kn, od_rpb, od_w_out, od_norm2, od_router, od_exp_gate, od_exp_up, od_exp_down):
    batch, s, d = x.shape
    depth = ev_norm1.shape[0] + od_norm1.shape[0]
    h = x.reshape(batch * s, d)
    n_e, _, f_e = od_exp_gate.shape[1:]
    experts = None
    for layer in range(depth):
        j = layer // 2
        if layer % 2 == 0:
            if j < od_norm1.shape[0]:
                narrow = [od_exp_gate[j].reshape(n_e * d, f_e), od_exp_up[j].reshape(n_e * d, f_e),
                          od_exp_down[j].reshape(n_e * f_e, d)]
            else:
                narrow = [jnp.zeros((batch * s // 16, LANES), F32)] * 3
            h, narrowed = _even_layer(h, batch, ev_norm1[j], ev_w_in[j], ev_qn_a[j], ev_kn_a[j],
                                      ev_qn_b[j], ev_kn_b[j], ev_sink_b[j], ev_w_out[j],
                                      ev_norm2[j], ev_ffn_gate[j], ev_ffn_up[j], ev_ffn_down[j],
                                      narrow)
            experts = (narrowed[0].reshape(n_e, d, f_e), narrowed[1].reshape(n_e, d, f_e),
                       narrowed[2].reshape(n_e, f_e, d))
        else:
            h = _odd_layer(h, batch, od_norm1[j], od_w_qkv[j], od_qn[j], od_kn[j], od_rpb[j],
                           od_w_out[j], od_norm2[j], od_router[j], *experts)
    return h.reshape(batch, s, d)
```

```python
import functools

import numpy as np
import jax
import jax.numpy as jnp
from jax import lax
from jax.experimental import pallas as pl
from jax.experimental.pallas import tpu as pltpu

D_MODEL = 1024
HEAD_DIM = 64
N_HEADS = D_MODEL // HEAD_DIM
A_HEADS = N_HEADS // 2
B_Q_HEADS = N_HEADS // 2
B_KV_HEADS = max(1, B_Q_HEADS // 4)
C_HEADS = N_HEADS
DIL_CONFIGS = ((128, 1), (512, 4), (2048, 16))
A_BLOCK = 128
B_WINDOW = 128
B_BLOCK = 128
GRID_W = 64
NA_ROWS = 8
NA_COLS = 16
N_EXPERTS = 8
RMS_EPS = 1e-6
NEG_INF = -1e30

LANES = 128
MXU_N = 256
VMEM_LIMIT = 56 * 1024 * 1024

ATTN_UNROLL = 8

F32 = jnp.float32
BF16 = jnp.bfloat16


def _lo_mask(shape):
    return lax.broadcasted_iota(jnp.int32, shape, len(shape) - 1) < HEAD_DIM


LOG2E = 1.4426950408889634


def _exp2_probs(s, m):
    return jnp.exp2(s - m).astype(BF16)


def _value_and_sum(p, v):
    ov = jnp.dot(p, jnp.concatenate([v, jnp.ones_like(v)], axis=1), preferred_element_type=F32)
    return ov[:, :LANES], ov[:, LANES:LANES + 1]


def _rms_rows(x, g):
    ms = jnp.mean(x * x, axis=-1, keepdims=True)
    return x * lax.rsqrt(ms + RMS_EPS) * g


def _norm_proj_kernel(x_ref, g_ref, w_ref, cg_ref, o_ref, *, norm_chunks):
    h = _rms_rows(x_ref[...], g_ref[...]).astype(BF16)
    n_out = o_ref.shape[-1]
    lo = _lo_mask((1, LANES))
    for j in range(n_out // MXU_N):
        y = jnp.dot(h, w_ref[:, j * MXU_N:(j + 1) * MXU_N], preferred_element_type=F32)
        for half in range(MXU_N // LANES):
            c = j * (MXU_N // LANES) + half
            yc = y[:, half * LANES:(half + 1) * LANES]
            if norm_chunks[c]:
                sq = yc * yc
                s_lo = jnp.sum(jnp.where(lo, sq, 0.0), axis=-1, keepdims=True)
                s_hi = jnp.sum(jnp.where(lo, 0.0, sq), axis=-1, keepdims=True)
                inv = jnp.where(lo, lax.rsqrt(s_lo * (1.0 / HEAD_DIM) + RMS_EPS),
                                lax.rsqrt(s_hi * (1.0 / HEAD_DIM) + RMS_EPS))
                yc = yc * inv * cg_ref[:, c * LANES:(c + 1) * LANES]
            o_ref[:, c * LANES:(c + 1) * LANES] = yc.astype(o_ref.dtype)


def _norm_proj(x, g, w, col_gain, norm_chunks, tm=512):
    t, d = x.shape
    n = w.shape[1]
    return pl.pallas_call(
        functools.partial(_norm_proj_kernel, norm_chunks=norm_chunks),
        grid=(t // tm,),
        in_specs=[
            pl.BlockSpec((tm, d), lambda i: (i, 0)),
            pl.BlockSpec((1, d), lambda i: (0, 0)),
            pl.BlockSpec((d, n), lambda i: (0, 0)),
            pl.BlockSpec((1, n), lambda i: (0, 0)),
        ],
        out_specs=pl.BlockSpec((tm, n), lambda i: (i, 0)),
        out_shape=jax.ShapeDtypeStruct((t, n), BF16),
        compiler_params=pltpu.CompilerParams(
            dimension_semantics=("arbitrary",), vmem_limit_bytes=VMEM_LIMIT),
        name="norm_proj",
    )(x, g.reshape(1, d), w, col_gain.reshape(1, n))


def _window_kernel(par_ref, q_ref, k_ref, v_ref, o_ref, bias_scr, *, seq, blk, radius):
    span = blk + 2 * radius
    c = pl.program_id(1)
    group = ATTN_UNROLL
    row_lo = lax.broadcasted_iota(jnp.int32, (2 * blk, 1), 0) < blk
    slope_col = jnp.where(row_lo, par_ref[0, 2 * c], par_ref[0, 2 * c + 1])
    sink_col = jnp.where(row_lo, par_ref[1, 2 * c], par_ref[1, 2 * c + 1])

    iq = lax.broadcasted_iota(jnp.int32, (2 * blk, span), 0) % blk
    ik = lax.broadcasted_iota(jnp.int32, (2 * blk, span), 1)
    _fill_band_bias(bias_scr, iq, ik, radius, 1, slope_col)
    lo_q = _lo_mask((1, LANES))
    lo = _lo_mask((blk, LANES))

    def body(g, carry):
        starts, variants, values, scores = [], [], [], []
        for u in range(group):
            a = pl.multiple_of((g * group + u) * blk, blk)
            ks = pl.multiple_of(jnp.clip(a - radius, 0, seq - span), radius)
            q = q_ref[pl.ds(a, blk), :]
            zero = jnp.zeros_like(q)
            qm = jnp.concatenate([jnp.where(lo_q, q, zero), jnp.where(lo_q, zero, q)], axis=0)
            k = k_ref[pl.ds(ks, span), :]
            scores.append(lax.dot_general(qm, k, (((1,), (1,)), ((), ())),
                                          preferred_element_type=F32))
            starts.append(a)
            variants.append((a - ks) // radius)
            values.append(v_ref[pl.ds(ks, span), :])
        probs, sink_terms = [], []
        for u in range(group):
            s = scores[u] + bias_scr[variants[u]]
            m = jnp.maximum(jnp.max(s, axis=-1, keepdims=True), sink_col)
            probs.append(_exp2_probs(s, m))
            sink_terms.append(jnp.exp2(sink_col - m))
        for u in range(group):
            o, l = _value_and_sum(probs[u], values[u])
            o = o * (1.0 / (l + sink_terms[u]))
            o_ref[pl.ds(starts[u], blk), :] = jnp.where(lo, o[:blk], o[blk:]).astype(o_ref.dtype)
        return carry

    lax.fori_loop(0, seq // (blk * group), body, 0)


def _window_attention(qkv, params, *, q_blk0, k_blk, v_blk, n_pairs, radius, blk):
    b, s, _ = qkv.shape
    span = blk + 2 * radius
    return pl.pallas_call(
        functools.partial(_window_kernel, seq=s, blk=blk, radius=radius),
        grid=(b, n_pairs),
        in_specs=[
            pl.BlockSpec(memory_space=pltpu.SMEM),
            pl.BlockSpec((None, s, LANES), lambda bi, c: (bi, 0, q_blk0 + c)),
            pl.BlockSpec((None, s, LANES), lambda bi, c: (bi, 0, k_blk)),
            pl.BlockSpec((None, s, LANES), lambda bi, c: (bi, 0, v_blk)),
        ],
        out_specs=pl.BlockSpec((None, s, LANES), lambda bi, c: (bi, 0, c)),
        out_shape=jax.ShapeDtypeStruct((b, s, n_pairs * LANES), BF16),
        scratch_shapes=[pltpu.VMEM((3, 2 * blk, span), F32)],
        compiler_params=pltpu.CompilerParams(
            dimension_semantics=("arbitrary", "arbitrary"), vmem_limit_bytes=VMEM_LIMIT),
        name="window_attn",
    )(params, qkv, qkv, qkv)


PERM_CHUNK = 256
PHASES4 = 4
PIECE = PERM_CHUNK // PHASES4


def _stack_heads(q):
    lo = _lo_mask((1, LANES))
    zero = jnp.zeros_like(q)
    return jnp.concatenate([jnp.where(lo, q, zero), jnp.where(lo, zero, q)], axis=0)


def _merge_heads(x, blk):
    x = jnp.broadcast_to(x, (2 * blk, LANES))
    return jnp.where(_lo_mask((blk, LANES)), x[:blk], x[blk:])


def _attend_group(qs, ks, vs, biases):
    scores = [lax.dot_general(_stack_heads(q), k, (((1,), (1,)), ((), ())),
                              preferred_element_type=F32) for q, k in zip(qs, ks)]
    stats = []
    for s, bias in zip(scores, biases):
        s = s + bias
        m = jnp.max(s, axis=-1, keepdims=True)
        stats.append((_exp2_probs(s, m), m))
    outs = []
    for (p, m), v in zip(stats, vs):
        o, l = _value_and_sum(p, v)
        outs.append((o, m, l))
    return outs


def _fill_band_bias(bias_scr, pos_q, pos_k, radius, pos_scale, slope_col):
    for var in range(3):
        dist = jnp.abs(pos_q + var * radius - pos_k)
        dist_f = (dist * pos_scale).astype(F32)
        bias_scr[var] = jnp.where(dist <= radius, -(slope_col * dist_f) * LOG2E, NEG_INF)


def _dilated_kernel(par_ref, q_ref, k_ref, v_ref, o_ref, qp, kp, vp, acc_p, m_p, l_p,
                    bias1, bias4, bias16, *, seq, blk, radius):
    span = blk + 2 * radius
    n_chunk = seq // PERM_CHUNK
    group = ATTN_UNROLL
    c = pl.program_id(1)
    row_lo = lax.broadcasted_iota(jnp.int32, (2 * blk, 1), 0) < blk
    slope_col = jnp.where(row_lo, par_ref[0, 2 * c], par_ref[0, 2 * c + 1])

    def piece_pos(x):
        return PIECE * (x // PIECE) + PHASES4 * (x % 16) + (x % PIECE) // 16

    iq = lax.broadcasted_iota(jnp.int32, (2 * blk, span), 0) % blk
    ik = lax.broadcasted_iota(jnp.int32, (2 * blk, span), 1)
    _fill_band_bias(bias1, iq, ik, radius, 1, slope_col)
    _fill_band_bias(bias4, piece_pos(iq), piece_pos(ik), radius, 4, slope_col)
    _fill_band_bias(bias16, iq, ik, radius, 16, slope_col)

    rr = lax.broadcasted_iota(jnp.int32, (PERM_CHUNK, PERM_CHUNK), 0)
    cc = lax.broadcasted_iota(jnp.int32, (PERM_CHUNK, PERM_CHUNK), 1)
    to_phase = jnp.where(cc == 16 * (rr % 16) + PHASES4 * ((rr // 16) % 4) + rr // PIECE,
                         1.0, 0.0).astype(BF16)
    to_token = jnp.where(cc == PIECE * (rr % PHASES4) + 16 * ((rr // PHASES4) % 4) + rr // 16,
                         1.0, 0.0).astype(BF16)

    def permute_body(ch2, carry):
        for half in range(2):
            ch = 2 * ch2 + half
            r0 = pl.multiple_of(ch * PERM_CHUNK, PERM_CHUNK)
            x = jnp.concatenate([q_ref[pl.ds(r0, PERM_CHUNK), :],
                                 k_ref[pl.ds(r0, PERM_CHUNK), :],
                                 v_ref[pl.ds(r0, PERM_CHUNK), :]], axis=1)
            y = jnp.dot(to_phase, x, preferred_element_type=F32).astype(BF16)
            for p4 in range(PHASES4):
                rows = slice(PIECE * p4, PIECE * (p4 + 1))
                qp[p4, ch] = y[rows, 0:LANES]
                kp[p4, ch] = y[rows, LANES:2 * LANES]
                vp[p4, ch] = y[rows, 2 * LANES:3 * LANES]
        return carry

    lax.fori_loop(0, n_chunk // 2, permute_body, 0)

    pieces_per_blk = blk // PIECE
    n_blk4 = seq // (PHASES4 * blk)

    def dil4_body(g, carry):
        p4 = g // (n_blk4 // group)
        j0 = (g % (n_blk4 // group)) * group
        qs, ks, vs, bs = [], [], [], []
        for u in range(group):
            pc = pieces_per_blk * (j0 + u)
            cs = jnp.clip(pc - radius // PIECE, 0, n_chunk - span // PIECE)
            qs.append(qp[p4, pl.ds(pc, pieces_per_blk)].reshape(blk, LANES))
            ks.append(kp[p4, pl.ds(cs, span // PIECE)].reshape(span, LANES))
            vs.append(vp[p4, pl.ds(cs, span // PIECE)].reshape(span, LANES))
            bs.append(bias4[(pc - cs) // (radius // PIECE)])
        for u, (o, m, l) in enumerate(_attend_group(qs, ks, vs, bs)):
            dst = (p4, pl.ds(pieces_per_blk * (j0 + u), pieces_per_blk))
            acc_p[dst] = _merge_heads(o, blk).reshape(pieces_per_blk, PIECE, LANES)
            m_p[dst] = _merge_heads(m, blk).reshape(pieces_per_blk, PIECE, LANES)
            l_p[dst] = _merge_heads(l, blk).reshape(pieces_per_blk, PIECE, LANES)
        return carry

    lax.fori_loop(0, PHASES4 * n_blk4 // group, dil4_body, 0)

    chunks_per_blk = blk // 16

    def dil16_body(p4, carry):
        qs, ks, vs, bs, dsts = [], [], [], [], []
        for mm in range(PIECE // 16):
            off = mm * 16
            kk = kp[p4, :, pl.ds(off, 16), :].reshape(span, LANES)
            vv = vp[p4, :, pl.ds(off, 16), :].reshape(span, LANES)
            for jb in range(n_chunk // chunks_per_blk):
                dst = (p4, pl.ds(chunks_per_blk * jb, chunks_per_blk), pl.ds(off, 16), slice(None))
                qs.append(qp[dst].reshape(blk, LANES))
                ks.append(kk)
                vs.append(vv)
                bs.append(bias16[jb * (blk // radius)])
                dsts.append(dst)
        for dst, (o, m, l) in zip(dsts, _attend_group(qs, ks, vs, bs)):
            m_old = m_p[dst].reshape(blk, LANES)
            m_new = jnp.maximum(m_old, _merge_heads(m, blk))
            w_old = jnp.exp2(m_old - m_new)
            w_new = jnp.exp2(_merge_heads(m, blk) - m_new)
            acc = w_old * acc_p[dst].reshape(blk, LANES) + w_new * _merge_heads(o, blk)
            den = w_old * l_p[dst].reshape(blk, LANES) + w_new * _merge_heads(l, blk)
            acc_p[dst] = acc.reshape(chunks_per_blk, 16, LANES)
            l_p[dst] = den.reshape(chunks_per_blk, 16, LANES)
            m_p[dst] = m_new.reshape(chunks_per_blk, 16, LANES)
        return carry

    lax.fori_loop(0, PHASES4, dil16_body, 0)

    blk_per_chunk = PERM_CHUNK // blk

    def dil1_body(g, carry):
        merged = []
        for cc2 in range(group // blk_per_chunk):
            ch = g * (group // blk_per_chunk) + cc2
            acc = jnp.concatenate([acc_p[p4, ch] for p4 in range(PHASES4)], axis=0)
            den = jnp.concatenate([l_p[p4, ch] for p4 in range(PHASES4)], axis=0)
            mx = jnp.concatenate([m_p[p4, ch] for p4 in range(PHASES4)], axis=0)
            o_ph = (acc * (1.0 / den)).astype(BF16)
            lse = mx + jnp.log2(den)
            hi = lse.astype(BF16)
            rem = lse - hi.astype(F32)
            mid = rem.astype(BF16)
            low = (rem - mid.astype(F32)).astype(BF16)
            back = jnp.dot(to_token, jnp.concatenate([o_ph, hi, mid, low], axis=1),
                           preferred_element_type=F32)
            merged.append((back[:, 0:LANES], back[:, LANES:2 * LANES]
                           + back[:, 2 * LANES:3 * LANES] + back[:, 3 * LANES:4 * LANES]))
        qs, ks, vs, bs, starts = [], [], [], [], []
        for u in range(group):
            a = pl.multiple_of((g * group + u) * blk, blk)
            kst = pl.multiple_of(jnp.clip(a - radius, 0, seq - span), radius)
            qs.append(q_ref[pl.ds(a, blk), :])
            ks.append(k_ref[pl.ds(kst, span), :])
            vs.append(v_ref[pl.ds(kst, span), :])
            bs.append(bias1[(a - kst) // radius])
            starts.append(a)
        for u, (o, m, l) in enumerate(_attend_group(qs, ks, vs, bs)):
            o1 = _merge_heads(o * (1.0 / l), blk)
            lse1 = _merge_heads(m + jnp.log2(l), blk)
            o2, lse2 = merged[u // blk_per_chunk]
            half = slice((u % blk_per_chunk) * blk, (u % blk_per_chunk + 1) * blk)
            o2, lse2 = o2[half], lse2[half]
            top = jnp.maximum(lse1, lse2)
            e1 = jnp.exp2(lse1 - top)
            e2 = jnp.exp2(lse2 - top)
            o_ref[pl.ds(starts[u], blk), :] = ((e1 * o1 + e2 * o2)
                                               * (1.0 / (e1 + e2))).astype(o_ref.dtype)
        return carry

    lax.fori_loop(0, seq // (blk * group), dil1_body, 0)


def _dilated_attention(qkv, params, *, k_blk0, v_blk0, n_pairs, blk, radius):
    b, s, _ = qkv.shape
    span = blk + 2 * radius
    n_chunk = s // PERM_CHUNK
    assert s // 16 == span and s % PERM_CHUNK == 0 and blk % PIECE == 0
    phase = lambda dt: pltpu.VMEM((PHASES4, n_chunk, PIECE, LANES), dt)
    return pl.pallas_call(
        functools.partial(_dilated_kernel, seq=s, blk=blk, radius=radius),
        grid=(b, n_pairs),
        in_specs=[
            pl.BlockSpec(memory_space=pltpu.SMEM),
            pl.BlockSpec((None, s, LANES), lambda bi, c: (bi, 0, c)),
            pl.BlockSpec((None, s, LANES), lambda bi, c: (bi, 0, k_blk0 + c)),
            pl.BlockSpec((None, s, LANES), lambda bi, c: (bi, 0, v_blk0 + c)),
        ],
        out_specs=pl.BlockSpec((None, s, LANES), lambda bi, c: (bi, 0, c)),
        out_shape=jax.ShapeDtypeStruct((b, s, n_pairs * LANES), BF16),
        scratch_shapes=[phase(BF16), phase(BF16), phase(BF16), phase(F32), phase(F32), phase(F32)]
        + [pltpu.VMEM((3, 2 * blk, span), F32)] * 3,
        compiler_params=pltpu.CompilerParams(
            dimension_semantics=("arbitrary", "arbitrary"), vmem_limit_bytes=VMEM_LIMIT),
        name="dilated_attn",
    )(params, qkv, qkv, qkv)


def _na_kernel(q_ref, k_ref, v_ref, bias_ref, o_ref, *, rows):
    kh = NA_ROWS
    group = ATTN_UNROLL
    lo_q = _lo_mask((1, LANES))
    lo = _lo_mask((GRID_W, LANES))

    def body(g, carry):
        starts, offs, values, scores = [], [], [], []
        for u in range(group):
            i = g * group + u
            rs = jnp.clip(i - kh // 2, 0, rows - kh)
            qs = pl.multiple_of(i * GRID_W, GRID_W)
            ks = pl.multiple_of(rs * GRID_W, GRID_W)
            q = q_ref[pl.ds(qs, GRID_W), :]
            zero = jnp.zeros_like(q)
            qm = jnp.concatenate([jnp.where(lo_q, q, zero), jnp.where(lo_q, zero, q)], axis=0)
            k = k_ref[pl.ds(ks, kh * GRID_W), :]
            scores.append(lax.dot_general(qm, k, (((1,), (1,)), ((), ())),
                                          preferred_element_type=F32))
            starts.append(qs)
            offs.append(rs - i + (NA_ROWS - 1))
            values.append(v_ref[pl.ds(ks, kh * GRID_W), :])
        probs = []
        for u in range(group):
            s = scores[u] + jnp.concatenate(
                [bias_ref[offs[u] + 2 * j] for j in range(kh // 2)], axis=1)
            probs.append(_exp2_probs(s, jnp.max(s, axis=-1, keepdims=True)))
        for u in range(group):
            o, l = _value_and_sum(probs[u], values[u])
            o = o * (1.0 / l)
            o_ref[pl.ds(starts[u], GRID_W), :] = jnp.where(
                lo, o[:GRID_W], o[GRID_W:]).astype(o_ref.dtype)
        return carry

    lax.fori_loop(0, rows // group, body, 0)


def _na_bias_table(rpb):
    qc = np.arange(GRID_W)[:, None]
    kc = np.arange(GRID_W)[None, :]
    qstart = np.clip(qc - NA_COLS // 2, 0, GRID_W - NA_COLS)
    valid = (kc >= qstart) & (kc < qstart + NA_COLS)
    coff = np.clip(kc - qc, -(NA_COLS - 1), NA_COLS - 1) + NA_COLS - 1
    n_rel = 2 * NA_COLS - 1
    n_off = 2 * NA_ROWS - 2
    pick = (coff[:, :, None] == np.arange(n_rel)).astype(np.float32)
    pick2 = np.zeros((GRID_W, 2, GRID_W, 2, n_rel), np.float32)
    pick2[:, 0, :, 0] = pick
    pick2[:, 1, :, 1] = pick
    pick2 = pick2.reshape(GRID_W, 2 * GRID_W, 2 * n_rel)
    r = rpb.astype(F32) * LOG2E
    r = jnp.concatenate([r[:, :-1], r[:, 1:]], axis=-1)
    r = r.reshape(C_HEADS // 2, 2, n_off, 2 * n_rel)
    t = jnp.einsum("phdm,qnm->pdhqn", r, jnp.asarray(pick2), precision=lax.Precision.HIGHEST)
    t = jnp.where(jnp.asarray(np.tile(valid, (1, 2))), t, NEG_INF)
    return t.reshape(C_HEADS // 2, n_off, 2 * GRID_W, 2 * GRID_W)


def _na_attention(qkv, bias):
    b, s, n_cols = qkv.shape
    n_pairs = C_HEADS // 2
    rows = s // GRID_W
    return pl.pallas_call(
        functools.partial(_na_kernel, rows=rows),
        grid=(b, n_pairs),
        in_specs=[
            pl.BlockSpec((None, s, LANES), lambda bi, c: (bi, 0, c)),
            pl.BlockSpec((None, s, LANES), lambda bi, c: (bi, 0, n_pairs + c)),
            pl.BlockSpec((None, s, LANES), lambda bi, c: (bi, 0, 2 * n_pairs + c)),
            pl.BlockSpec((None,) + bias.shape[1:], lambda bi, c: (c, 0, 0, 0)),
        ],
        out_specs=pl.BlockSpec((None, s, LANES), lambda bi, c: (bi, 0, c)),
        out_shape=jax.ShapeDtypeStruct((b, s, n_pairs * LANES), BF16),
        compiler_params=pltpu.CompilerParams(
            dimension_semantics=("arbitrary", "arbitrary"), vmem_limit_bytes=VMEM_LIMIT),
        name="na_attn",
    )(qkv, qkv, qkv, bias)


def _silu(g):
    return g * (1.0 / (1.0 + jnp.exp(-g)))


def _ffn_kernel(x_ref, oa_ref, ob_ref, wa_ref, wb_ref, g_ref, wg_ref, wu_ref, wd_ref,
                c0_ref, c1_ref, c2_ref, y_ref, n0_ref, n1_ref, n2_ref, a_scr):
    n0_ref[...] = c0_ref[...].astype(n0_ref.dtype)
    n1_ref[...] = c1_ref[...].astype(n1_ref.dtype)
    n2_ref[...] = c2_ref[...].astype(n2_ref.dtype)
    x = (x_ref[...] + jnp.dot(oa_ref[...], wa_ref[...], preferred_element_type=F32)
         + jnp.dot(ob_ref[...], wb_ref[...], preferred_element_type=F32))
    h = _rms_rows(x, g_ref[...]).astype(BF16)
    f = wg_ref.shape[1]
    for j in range(f // MXU_N):
        sl = slice(j * MXU_N, (j + 1) * MXU_N)
        gate = jnp.dot(h, wg_ref[:, sl], preferred_element_type=F32)
        up = jnp.dot(h, wu_ref[:, sl], preferred_element_type=F32)
        a_scr[:, sl] = (_silu(gate) * up).astype(BF16)
    y_ref[...] = x + jnp.dot(a_scr[...], wd_ref[...], preferred_element_type=F32)


def _ffn(x, oa, ob, wa, wb, g, wg, wu, wd, narrow, tm=256):
    t, d = x.shape
    f = wg.shape[1]
    steps = t // tm
    const = lambda a: pl.BlockSpec(a.shape, lambda i: (0, 0), pipeline_mode=pl.Buffered(1))
    row = lambda w: pl.BlockSpec((tm, w), lambda i: (i, 0))
    slab = lambda a: pl.BlockSpec((a.shape[0] // steps, a.shape[1]), lambda i: (i, 0))
    assert all(a.shape[0] % (16 * steps) == 0 for a in narrow)
    return pl.pallas_call(
        _ffn_kernel,
        grid=(steps,),
        in_specs=[row(d), row(oa.shape[1]), row(ob.shape[1]), const(wa), const(wb),
                  pl.BlockSpec((1, d), lambda i: (0, 0)),
                  const(wg), const(wu), const(wd)] + [slab(a) for a in narrow],
        out_specs=[pl.BlockSpec((tm, d), lambda i: (i, 0))] + [slab(a) for a in narrow],
        out_shape=[jax.ShapeDtypeStruct((t, d), F32)]
        + [jax.ShapeDtypeStruct(a.shape, BF16) for a in narrow],
        scratch_shapes=[pltpu.VMEM((tm, f), BF16)],
        compiler_params=pltpu.CompilerParams(
            dimension_semantics=("arbitrary",), vmem_limit_bytes=VMEM_LIMIT),
        name="dense_swiglu",
    )(x, oa, ob, wa, wb, g.reshape(1, d), wg, wu, wd, *narrow)


TOP_K = 2
MOE_TB = 512
MOE_TM = 512
MOE_RC = 256
KEY_STRIDE = 65536.0


def _router_kernel(x_ref, o_ref, wo_ref, g_ref, wr_hi_ref, wr_lo_ref, x1_ref, h_ref, info_ref,
                   info_t_ref, cnt_ref, carry_scr, ltri_scr):
    i = pl.program_id(0)
    tb = x_ref.shape[0]
    parts = 2
    hb = tb // parts
    rows = [slice(p * hb, (p + 1) * hb) for p in range(parts)]
    lane = lax.broadcasted_iota(jnp.int32, (1, LANES), 1)

    @pl.when(i == 0)
    def _():
        carry_scr[...] = jnp.zeros_like(carry_scr)
        r = lax.broadcasted_iota(jnp.int32, (hb, hb), 0)
        cidx = lax.broadcasted_iota(jnp.int32, (hb, hb), 1)
        ltri_scr[...] = jnp.where(cidx < r, 1.0, 0.0).astype(BF16)

    x1s = [x_ref[rw, :] + jnp.dot(o_ref[rw, :], wo_ref[...], preferred_element_type=F32)
           for rw in rows]
    hs = [_rms_rows(x1, g_ref[...]) for x1 in x1s]
    his = [h.astype(BF16) for h in hs]
    los = [(h - hi.astype(F32)).astype(BF16) for h, hi in zip(hs, his)]
    logits_all = [jnp.dot(hi, wr_hi_ref[...], preferred_element_type=F32)
                  + jnp.dot(lo_, wr_hi_ref[...], preferred_element_type=F32)
                  + jnp.dot(hi, wr_lo_ref[...], preferred_element_type=F32)
                  for hi, lo_ in zip(his, los)]
    for rw, x1, hi in zip(rows, x1s, his):
        x1_ref[rw, :] = x1
        h_ref[rw, :] = hi

    picks = []
    for logits in logits_all:
        logits = jnp.where(lane < N_EXPERTS, logits, NEG_INF)
        v1 = jnp.max(logits, axis=-1, keepdims=True)
        i1 = jnp.min(jnp.where(logits == v1, lane, LANES), axis=-1, keepdims=True)
        rest = jnp.where(lane == i1, NEG_INF, logits)
        v2 = jnp.max(rest, axis=-1, keepdims=True)
        i2 = jnp.min(jnp.where(rest == v2, lane, LANES), axis=-1, keepdims=True)
        e2 = jnp.exp(v2 - v1)
        g1 = 1.0 / (1.0 + e2)
        picks.append((i1, i2, g1, e2 * g1))

    pickeds = [jnp.where(lane == i1, 1.0, jnp.where(lane == i2, 1.0, 0.0))
               for i1, i2, _, _ in picks]
    inside = [jnp.dot(ltri_scr[...], pk.astype(BF16), preferred_element_type=F32)
              for pk in pickeds]
    counts = [jnp.sum(pk, axis=0, keepdims=True) for pk in pickeds]
    seen = carry_scr[...]
    for p, (i1, i2, g1, g2) in enumerate(picks):
        before = inside[p] + seen
        rank1 = jnp.sum(jnp.where(lane == i1, before, 0.0), axis=-1, keepdims=True)
        rank2 = jnp.sum(jnp.where(lane == i2, before, 0.0), axis=-1, keepdims=True)
        key1 = i1.astype(F32) * KEY_STRIDE + rank1
        key2 = i2.astype(F32) * KEY_STRIDE + rank2
        info = jnp.where(lane == 0, key1, jnp.where(lane == 1, key2,
                         jnp.where(lane == 2, g1, jnp.where(lane == 3, g2, 0.0))))
        info_ref[rows[p], :] = info
        info_t_ref[:, rows[p]] = info.T[:8, :]
        seen = seen + counts[p]
    cnt_ref[...] = jnp.broadcast_to(seen - carry_scr[...], cnt_ref.shape)
    carry_scr[...] = seen


def _router(x, o, wo, g, wr):
    t, d = x.shape
    nb = t // MOE_TB
    wr_hi = wr.astype(BF16)
    wr_lo = (wr - wr_hi.astype(F32)).astype(BF16)
    row = lambda w: pl.BlockSpec((MOE_TB, w), lambda i: (i, 0))
    const = lambda a: pl.BlockSpec(a.shape, lambda i: (0, 0))
    return pl.pallas_call(
        _router_kernel,
        grid=(nb,),
        in_specs=[row(d), row(o.shape[1]), const(wo), pl.BlockSpec((1, d), lambda i: (0, 0)),
                  const(wr_hi), const(wr_lo)],
        out_specs=[row(d), row(d), row(LANES),
                   pl.BlockSpec((None, 8, MOE_TB), lambda i: (i, 0, 0)),
                   pl.BlockSpec((None, 8, LANES), lambda i: (i, 0, 0))],
        out_shape=[jax.ShapeDtypeStruct((t, d), F32),
                   jax.ShapeDtypeStruct((t, d), BF16),
                   jax.ShapeDtypeStruct((t, LANES), F32),
                   jax.ShapeDtypeStruct((nb, 8, MOE_TB), F32),
                   jax.ShapeDtypeStruct((nb, 8, LANES), F32)],
        scratch_shapes=[pltpu.VMEM((1, LANES), F32),
                        pltpu.VMEM((MOE_TB // 2, MOE_TB // 2), BF16)],
        compiler_params=pltpu.CompilerParams(
            dimension_semantics=("arbitrary",), vmem_limit_bytes=VMEM_LIMIT),
        name="moe_router",
    )(x, o, wo, g.reshape(1, d), wr_hi, wr_lo)


def _moe_plan(block_counts, n_tokens):
    nb = block_counts.shape[0]
    n_tiles = (TOP_K * n_tokens + N_EXPERTS * (MOE_TM - 1)) // MOE_TM
    n_chunks = n_tiles * MOE_TM // MOE_RC
    tc = block_counts.astype(jnp.int32)
    cb = jnp.concatenate([jnp.zeros((1, N_EXPERTS), jnp.int32), jnp.cumsum(tc, axis=0)])
    counts = cb[-1]
    padded = ((counts + MOE_TM - 1) // MOE_TM) * MOE_TM
    ends = jnp.cumsum(padded)
    starts = ends - padded
    total = ends[-1]

    tiles = jnp.arange(n_tiles, dtype=jnp.int32)
    n_valid_tiles = total // MOE_TM
    t_valid = (tiles < n_valid_tiles).astype(jnp.int32)
    t_src = jnp.maximum(jnp.minimum(tiles, n_valid_tiles - 1), 0)
    t_exp = jnp.minimum(jnp.sum((t_src * MOE_TM)[:, None] >= ends[None, :], axis=1),
                        N_EXPERTS - 1).astype(jnp.int32)

    r0 = jnp.arange(n_chunks, dtype=jnp.int32) * MOE_RC
    ex = jnp.minimum(jnp.sum(r0[:, None] >= ends[None, :], axis=1), N_EXPERTS - 1)
    live = r0 < total
    lo = r0 - starts[ex]
    base = (ex * int(KEY_STRIDE) + lo).astype(jnp.int32)
    cbe = cb[:, ex]
    g_lo = jnp.sum(cbe[1:] <= lo[None, :], axis=0).astype(jnp.int32)
    g_cnt = jnp.sum(live[None, :] & (cbe[:-1] < (lo + MOE_RC)[None, :]) & (cbe[1:] > lo[None, :]),
                    axis=0).astype(jnp.int32)

    row_lo = starts[None, :] + cb[:-1]
    row_hi = starts[None, :] + cb[1:]
    c_first = row_lo // MOE_RC
    c_num = jnp.where(row_hi > row_lo, (row_hi - 1) // MOE_RC - c_first + 1, 0)
    c_end = jnp.cumsum(c_num, axis=1)
    slots = jnp.arange(COMBINE_MAX_CHUNKS, dtype=jnp.int32)
    e_of = jnp.minimum(jnp.sum(slots[None, :, None] >= c_end[:, None, :], axis=2),
                       N_EXPERTS - 1)
    owner = e_of[:, :, None] == jnp.arange(N_EXPERTS)[None, None, :]
    c_c = slots[None, :] + jnp.sum(
        jnp.where(owner, (c_first - (c_end - c_num))[:, None, :], 0), axis=2)
    c_cnt = c_end[:, -1].astype(jnp.int32)
    c_c = jnp.clip(c_c, 0, n_chunks - 1).astype(jnp.int32).reshape(-1)
    return dict(n_tiles=n_tiles, n_chunks=n_chunks, t_valid=t_valid, t_exp=t_exp,
                base=base, live=live.astype(jnp.int32), g_lo=g_lo, g_cnt=g_cnt,
                c_cnt=c_cnt, c_c=c_c)


def _gather_kernel(live_ref, lo_ref, cnt_ref, base_ref, info_t_ref, h_ref,
                   xs_ref, gs_ref, acc_scr, gsum_scr):
    tb = info_t_ref.shape[-1]
    per_tile = xs_ref.shape[0] // MOE_RC
    row = lax.broadcasted_iota(jnp.int32, (MOE_RC, tb), 0).astype(F32)

    def chunk_body(k, carry):
        c = pl.program_id(0) * per_tile + k
        dst = pl.ds(pl.multiple_of(k * MOE_RC, MOE_RC), MOE_RC)

        @pl.when(live_ref[c] == 1)
        def _():
            first = lo_ref[c]
            n = cnt_ref[c]
            acc_scr[...] = jnp.zeros_like(acc_scr)
            gsum_scr[...] = jnp.zeros_like(gsum_scr)

            def two_blocks(it, inner):
                rows, gate = None, None
                for w in range(2):
                    j = 2 * it + w
                    b = first + jnp.minimum(j, n - 1)
                    base = jnp.where(j < n, base_ref[c], NO_MATCH_BASE).astype(F32)
                    info = info_t_ref[b]
                    m1 = (info[0:1, :] - base) == row
                    m2 = (info[1:2, :] - base) == row
                    sel = jnp.where(m1, 1.0, jnp.where(m2, 1.0, 0.0)).astype(BF16)
                    g = jnp.sum(jnp.where(m1, info[2:3, :], jnp.where(m2, info[3:4, :], 0.0)),
                                axis=-1, keepdims=True)
                    r = jnp.dot(sel, h_ref[pl.ds(pl.multiple_of(b * tb, tb), tb), :],
                                preferred_element_type=F32)
                    rows = r if rows is None else rows + r
                    gate = g if gate is None else gate + g
                acc_scr[...] += rows
                gsum_scr[...] += gate
                return inner

            lax.fori_loop(0, (n + 1) // 2, two_blocks, 0)
            xs_ref[dst, :] = acc_scr[...].astype(xs_ref.dtype)
            gs_ref[dst, :] = jnp.broadcast_to(gsum_scr[...], (MOE_RC, gs_ref.shape[1]))

        @pl.when(live_ref[c] == 0)
        def _():
            xs_ref[dst, :] = jnp.zeros((MOE_RC, xs_ref.shape[1]), xs_ref.dtype)
            gs_ref[dst, :] = jnp.zeros((MOE_RC, gs_ref.shape[1]), gs_ref.dtype)

        return carry

    lax.fori_loop(0, per_tile, chunk_body, 0)


def _moe_gather(plan, info_t, h):
    t, d = h.shape
    rows = plan["n_chunks"] * MOE_RC
    tile = lambda i, live, lo, cn, ba: (i, 0)
    grid_spec = pltpu.PrefetchScalarGridSpec(
        num_scalar_prefetch=4,
        grid=(plan["n_tiles"],),
        in_specs=[pl.BlockSpec(info_t.shape, lambda c, live, lo, cn, ba: (0, 0, 0),
                               pipeline_mode=pl.Buffered(1)),
                  pl.BlockSpec((t, d), lambda c, live, lo, cn, ba: (0, 0),
                               pipeline_mode=pl.Buffered(1))],
        out_specs=[pl.BlockSpec((MOE_TM, d), tile), pl.BlockSpec((MOE_TM, LANES), tile)],
        scratch_shapes=[pltpu.VMEM((MOE_RC, d), F32), pltpu.VMEM((MOE_RC, 1), F32)],
    )
    return pl.pallas_call(
        _gather_kernel,
        grid_spec=grid_spec,
        out_shape=[jax.ShapeDtypeStruct((rows, d), BF16),
                   jax.ShapeDtypeStruct((rows, LANES), F32)],
        compiler_params=pltpu.CompilerParams(
            dimension_semantics=("arbitrary",), vmem_limit_bytes=VMEM_LIMIT),
        name="moe_gather",
    )(plan["live"], plan["g_lo"], plan["g_cnt"], plan["base"], info_t, h)


def _experts_kernel(exp_ref, val_ref, xs_ref, gs_ref, wg_ref, wu_ref, wd_ref, ys_ref,
                    acc_scr):
    i = pl.program_id(0)
    j = pl.program_id(1)

    last = j == pl.num_programs(1) - 1

    @pl.when(j == 0)
    def _():
        acc_scr[...] = jnp.zeros_like(acc_scr)

    @pl.when(val_ref[i] == 1)
    def _():
        x = xs_ref[...]
        gate = jnp.dot(x, wg_ref[...], preferred_element_type=F32)
        up = jnp.dot(x, wu_ref[...], preferred_element_type=F32)
        a = (_silu(gate) * up).astype(BF16)
        acc_scr[...] += jnp.dot(a, wd_ref[...], preferred_element_type=F32)

    @pl.when(last)
    def _():
        ys_ref[...] = (acc_scr[...] * gs_ref[:, 0:1]).astype(ys_ref.dtype)


def _moe_experts(plan, xs, gs, wg, wu, wd, tf):
    rows, d = xs.shape
    f = wg.shape[2]
    nf = f // tf

    def wcol(i, j, ex, val):
        return (ex[i], 0, jnp.where(val[i] == 1, j, nf - 1))

    def wrow(i, j, ex, val):
        return (ex[i], jnp.where(val[i] == 1, j, nf - 1), 0)

    tile = lambda i, j, ex, val: (i, 0)
    grid_spec = pltpu.PrefetchScalarGridSpec(
        num_scalar_prefetch=2,
        grid=(plan["n_tiles"], nf),
        in_specs=[pl.BlockSpec((MOE_TM, d), tile),
                  pl.BlockSpec((MOE_TM, LANES), tile),
                  pl.BlockSpec((None, d, tf), wcol),
                  pl.BlockSpec((None, d, tf), wcol),
                  pl.BlockSpec((None, tf, d), wrow)],
        out_specs=pl.BlockSpec((MOE_TM, d), tile),
        scratch_shapes=[pltpu.VMEM((MOE_TM, d), F32)],
    )
    return pl.pallas_call(
        _experts_kernel,
        grid_spec=grid_spec,
        out_shape=jax.ShapeDtypeStruct((rows, d), BF16),
        compiler_params=pltpu.CompilerParams(
            dimension_semantics=("arbitrary", "arbitrary"), vmem_limit_bytes=VMEM_LIMIT),
        name="moe_experts",
    )(plan["t_exp"], plan["t_valid"], xs, gs, wg, wu, wd)


NO_MATCH_BASE = -(2 ** 30)


COMBINE_GROUP = 4
COMBINE_MAX_CHUNKS = N_EXPERTS * (MOE_TB // MOE_RC + 1)


def _combine_kernel(cnt_ref, pc_ref, base_ref, info_ref, x_ref, ys_hbm, o_ref, ybuf, sem):
    b = pl.program_id(0)
    tb = x_ref.shape[0]
    d = x_ref.shape[1]

    def n_fetch(blk):
        return ((cnt_ref[blk] + COMBINE_GROUP - 1) // COMBINE_GROUP) * COMBINE_GROUP

    def chunk_of(blk, j):
        return pc_ref[blk * COMBINE_MAX_CHUNKS + jnp.minimum(j, cnt_ref[blk] - 1)]

    def copy(blk, j, slot):
        row0 = pl.multiple_of(chunk_of(blk, j) * MOE_RC, MOE_RC)
        return pltpu.make_async_copy(ys_hbm.at[pl.ds(row0, MOE_RC), :], ybuf.at[slot, j],
                                     sem.at[slot])

    def start_block(blk, slot):
        def issue(j, carry):
            copy(blk, j, slot).start()
            return carry
        lax.fori_loop(0, n_fetch(blk), issue, 0)

    @pl.when(b == 0)
    def _():
        start_block(0, 0)

    @pl.when(b + 1 < pl.num_programs(0))
    def _():
        start_block(b + 1, (b + 1) % 2)

    slot = b % 2
    n = cnt_ref[b]

    def drain(j, carry):
        copy(b, j, slot).wait()
        return carry

    lax.fori_loop(0, n_fetch(b), drain, 0)

    lane = lax.broadcasted_iota(jnp.int32, (1, LANES), 1)
    info = info_ref[...]
    k1 = jnp.sum(jnp.where(lane == 0, info, 0.0), axis=-1, keepdims=True)
    k2 = jnp.sum(jnp.where(lane == 1, info, 0.0), axis=-1, keepdims=True)
    col = lax.broadcasted_iota(jnp.int32, (tb, MOE_RC), 1).astype(F32)
    o_ref[...] = x_ref[...]

    def group(gi, carry):
        sels = []
        for w in range(COMBINE_GROUP):
            j = gi * COMBINE_GROUP + w
            base = jnp.where(j < n, base_ref[chunk_of(b, j)], NO_MATCH_BASE).astype(F32)
            sels.append(jnp.where(k1 - base == col, 1.0,
                                  jnp.where(k2 - base == col, 1.0, 0.0)).astype(BF16))
        rows = ybuf[slot, pl.ds(gi * COMBINE_GROUP, COMBINE_GROUP)]
        o_ref[...] += jnp.dot(jnp.concatenate(sels, axis=1),
                              rows.reshape(COMBINE_GROUP * MOE_RC, d),
                              preferred_element_type=F32)
        return carry

    lax.fori_loop(0, n_fetch(b) // COMBINE_GROUP, group, 0)


def _moe_combine(plan, info, ys, x):
    t, d = x.shape
    block = lambda b, cn, pc, ba: (b, 0)
    grid_spec = pltpu.PrefetchScalarGridSpec(
        num_scalar_prefetch=3,
        grid=(t // MOE_TB,),
        in_specs=[pl.BlockSpec((MOE_TB, LANES), block),
                  pl.BlockSpec((MOE_TB, d), block),
                  pl.BlockSpec(memory_space=pl.ANY)],
        out_specs=pl.BlockSpec((MOE_TB, d), block),
        scratch_shapes=[pltpu.VMEM((2, COMBINE_MAX_CHUNKS, MOE_RC, d), BF16),
                        pltpu.SemaphoreType.DMA((2,))],
    )
    return pl.pallas_call(
        _combine_kernel,
        grid_spec=grid_spec,
        out_shape=jax.ShapeDtypeStruct((t, d), F32),
        compiler_params=pltpu.CompilerParams(
            dimension_semantics=("arbitrary",), vmem_limit_bytes=VMEM_LIMIT),
        name="moe_combine",
    )(plan["c_cnt"], plan["c_c"], plan["base"], info, x, ys)


def _moe(x, o, wo, g, wr, wg, wu, wd, tf=1792):
    t, _ = x.shape
    x1, h, info, info_t, block_counts = _router(x, o, wo, g, wr)
    plan = _moe_plan(block_counts[:, 0, :N_EXPERTS], t)
    xs, gs = _moe_gather(plan, info_t, h)
    ys = _moe_experts(plan, xs, gs, wg, wu, wd, tf)
    return _moe_combine(plan, info, ys, x1)


def _alibi_slopes():
    s = np.exp2(-8.0 * np.arange(1, N_HEADS + 1) / N_HEADS).astype(np.float32)
    return s[0::2][:A_HEADS], s[1::2][:B_Q_HEADS]


_B_HEAD_ORDER = np.array([0, 4, 1, 5, 2, 6, 3, 7])


def _even_layer(x2d, batch, norm1, w_in, qn_a, kn_a, qn_b, kn_b, sink_b, w_out,
                norm2, w_gate, w_up, w_down, narrow):
    t, d = x2d.shape
    s = t // batch
    wa = A_HEADS * HEAD_DIM
    wq_b = B_Q_HEADS * HEAD_DIM
    wkv_b = B_KV_HEADS * HEAD_DIM
    qb0 = 3 * wa
    q_scale = HEAD_DIM ** -0.5 * LOG2E

    w_qb = w_in[:, qb0:qb0 + wq_b].reshape(d, B_Q_HEADS, HEAD_DIM)[:, _B_HEAD_ORDER]
    w_in_p = jnp.concatenate(
        [w_in[:, :qb0], w_qb.reshape(d, wq_b), w_in[:, qb0 + wq_b:]], axis=1).astype(BF16)
    ones = jnp.ones((HEAD_DIM,), F32)
    col_gain = jnp.concatenate([
        jnp.tile(qn_a * q_scale, A_HEADS), jnp.tile(kn_a, A_HEADS), jnp.tile(ones, A_HEADS),
        jnp.tile(qn_b * q_scale, B_Q_HEADS), jnp.tile(kn_b, B_KV_HEADS),
        jnp.tile(ones, B_KV_HEADS)])
    cpl = wa // LANES
    norm_chunks = ((True,) * (2 * cpl) + (False,) * cpl + (True,) * (wq_b // LANES)
                   + (True,) * (wkv_b // LANES) + (False,) * (wkv_b // LANES))
    qkv = _norm_proj(x2d, norm1, w_in_p, col_gain, norm_chunks)
    n_cols = qkv.shape[1]
    qkv3 = qkv.reshape(batch, s, n_cols)

    slopes_a, slopes_b = _alibi_slopes()
    par_a = jnp.asarray(slopes_a).reshape(1, A_HEADS)
    radii = {window // (2 * dil) for window, dil in DIL_CONFIGS}
    assert tuple(dil for _, dil in DIL_CONFIGS) == (1, 4, 16) and len(radii) == 1
    oa = _dilated_attention(qkv3, par_a, k_blk0=cpl, v_blk0=2 * cpl, n_pairs=cpl,
                            blk=A_BLOCK, radius=radii.pop())
    par_b = jnp.stack([jnp.asarray(slopes_b[_B_HEAD_ORDER]),
                       sink_b.astype(F32)[_B_HEAD_ORDER] * LOG2E])
    kb = 3 * cpl + wq_b // LANES
    ob = _window_attention(qkv3, par_b, q_blk0=3 * cpl, k_blk=kb, v_blk=kb + wkv_b // LANES,
                           n_pairs=wq_b // LANES, radius=B_WINDOW, blk=B_BLOCK)
    w_out_a = w_out[:wa].astype(BF16)
    w_out_b = w_out[wa:].reshape(B_Q_HEADS, HEAD_DIM, d)[_B_HEAD_ORDER].reshape(wq_b, d)
    y, *narrowed = _ffn(x2d, oa.reshape(t, wa), ob.reshape(t, wq_b), w_out_a,
                        w_out_b.astype(BF16), norm2, w_gate.astype(BF16), w_up.astype(BF16),
                        w_down.astype(BF16), narrow)
    return y, narrowed


def _odd_layer(x2d, batch, norm1, w_qkv, qn, kn, rpb, w_out, norm2, w_router,
               w_gate, w_up, w_down):
    t, d = x2d.shape
    s = t // batch
    wc = C_HEADS * HEAD_DIM
    ones = jnp.ones((HEAD_DIM,), F32)
    col_gain = jnp.concatenate([jnp.tile(qn * (HEAD_DIM ** -0.5 * LOG2E), C_HEADS),
                                jnp.tile(kn, C_HEADS), jnp.tile(ones, C_HEADS)])
    cpl = wc // LANES
    norm_chunks = (True,) * (2 * cpl) + (False,) * cpl
    qkv = _norm_proj(x2d, norm1, w_qkv.astype(BF16), col_gain, norm_chunks)
    o = _na_attention(qkv.reshape(batch, s, 3 * wc), _na_bias_table(rpb))
    wr = jnp.zeros((d, LANES), F32).at[:, :N_EXPERTS].set(w_router.astype(F32))
    return _moe(x2d, o.reshape(t, wc), w_out.astype(BF16), norm2, wr, w_gate, w_up, w_down)


def kernel(x, ev_norm1, ev_w_in, ev_qn_a, ev_kn_a, ev_qn_b, ev_kn_b, ev_sink_b, ev_w_out, ev_norm2, ev_ffn_gate, ev_ffn_up, ev_ffn_down, od_norm1, od_w_qkv, od_qn, od_kn, od_rpb, od_w_out, od_norm2, od_router, od_exp_gate, od_exp_up, od_exp_down):
    batch, s, d = x.shape
    depth = ev_norm1.shape[0] + od_norm1.shape[0]
    h = x.reshape(batch * s, d)
    n_e, _, f_e = od_exp_gate.shape[1:]
    experts = None
    for layer in range(depth):
        j = layer // 2
        if layer % 2 == 0:
            if j < od_norm1.shape[0]:
                narrow = [od_exp_gate[j].reshape(n_e * d, f_e), od_exp_up[j].reshape(n_e * d, f_e),
                          od_exp_down[j].reshape(n_e * f_e, d)]
            else:
                narrow = [jnp.zeros((batch * s // 16, LANES), F32)] * 3
            h, narrowed = _even_layer(h, batch, ev_norm1[j], ev_w_in[j], ev_qn_a[j], ev_kn_a[j],
                                      ev_qn_b[j], ev_kn_b[j], ev_sink_b[j], ev_w_out[j],
                                      ev_norm2[j], ev_ffn_gate[j], ev_ffn_up[j], ev_ffn_down[j],
                                      narrow)
            experts = (narrowed[0].reshape(n_e, d, f_e), narrowed[1].reshape(n_e, d, f_e),
                       narrowed[2].reshape(n_e, f_e, d))
        else:
            h = _odd_layer(h, batch, od_norm1[j], od_w_qkv[j], od_qn[j], od_kn[j], od_rpb[j],
                           od_w_out[j], od_norm2[j], od_router[j], *experts)
    return h.reshape(batch, s, d)
```

```python
import functools

import numpy as np
import jax
import jax.numpy as jnp
from jax import lax
from jax.experimental import pallas as pl
from jax.experimental.pallas import tpu as pltpu

D_MODEL = 1024
HEAD_DIM = 64
N_HEADS = D_MODEL // HEAD_DIM
A_HEADS = N_HEADS // 2
B_Q_HEADS = N_HEADS // 2
B_KV_HEADS = max(1, B_Q_HEADS // 4)
C_HEADS = N_HEADS
DIL_CONFIGS = ((128, 1), (512, 4), (2048, 16))
A_BLOCK = 128
B_WINDOW = 128
B_BLOCK = 128
GRID_W = 64
NA_ROWS = 8
NA_COLS = 16
N_EXPERTS = 8
RMS_EPS = 1e-6
NEG_INF = -1e30

LANES = 128
MXU_N = 256
VMEM_LIMIT = 56 * 1024 * 1024

ATTN_UNROLL = 8

F32 = jnp.float32
BF16 = jnp.bfloat16


def _lo_mask(shape):
    return lax.broadcasted_iota(jnp.int32, shape, len(shape) - 1) < HEAD_DIM


LOG2E = 1.4426950408889634


def _exp2_probs(s, m):
    return jnp.exp2(s - m).astype(BF16)


def _value_and_sum(p, v):
    ov = jnp.dot(p, jnp.concatenate([v, jnp.ones_like(v)], axis=1), preferred_element_type=F32)
    return ov[:, :LANES], ov[:, LANES:LANES + 1]


def _rms_rows(x, g):
    ms = jnp.mean(x * x, axis=-1, keepdims=True)
    return x * lax.rsqrt(ms + RMS_EPS) * g


def _norm_proj_kernel(x_ref, g_ref, w_ref, cg_ref, o_ref, *, norm_chunks):
    h = _rms_rows(x_ref[...], g_ref[...]).astype(BF16)
    n_out = o_ref.shape[-1]
    lo = _lo_mask((1, LANES))
    for j in range(n_out // MXU_N):
        y = jnp.dot(h, w_ref[:, j * MXU_N:(j + 1) * MXU_N], preferred_element_type=F32)
        for half in range(MXU_N // LANES):
            c = j * (MXU_N // LANES) + half
            yc = y[:, half * LANES:(half + 1) * LANES]
            if norm_chunks[c]:
                sq = yc * yc
                s_lo = jnp.sum(jnp.where(lo, sq, 0.0), axis=-1, keepdims=True)
                s_hi = jnp.sum(jnp.where(lo, 0.0, sq), axis=-1, keepdims=True)
                inv = jnp.where(lo, lax.rsqrt(s_lo * (1.0 / HEAD_DIM) + RMS_EPS),
                                lax.rsqrt(s_hi * (1.0 / HEAD_DIM) + RMS_EPS))
                yc = yc * inv * cg_ref[:, c * LANES:(c + 1) * LANES]
            o_ref[:, c * LANES:(c + 1) * LANES] = yc.astype(o_ref.dtype)


def _norm_proj(x, g, w, col_gain, norm_chunks, tm=512):
    t, d = x.shape
    n = w.shape[1]
    return pl.pallas_call(
        functools.partial(_norm_proj_kernel, norm_chunks=norm_chunks),
        grid=(t // tm,),
        in_specs=[
            pl.BlockSpec((tm, d), lambda i: (i, 0)),
            pl.BlockSpec((1, d), lambda i: (0, 0)),
            pl.BlockSpec((d, n), lambda i: (0, 0)),
            pl.BlockSpec((1, n), lambda i: (0, 0)),
        ],
        out_specs=pl.BlockSpec((tm, n), lambda i: (i, 0)),
        out_shape=jax.ShapeDtypeStruct((t, n), BF16),
        compiler_params=pltpu.CompilerParams(
            dimension_semantics=("arbitrary",), vmem_limit_bytes=VMEM_LIMIT),
        name="norm_proj",
    )(x, g.reshape(1, d), w, col_gain.reshape(1, n))


def _window_kernel(par_ref, q_ref, k_ref, v_ref, o_ref, bias_scr, *, seq, blk, radius):
    span = blk + 2 * radius
    c = pl.program_id(0)
    group = ATTN_UNROLL
    row_lo = lax.broadcasted_iota(jnp.int32, (2 * blk, 1), 0) < blk
    slope_col = jnp.where(row_lo, par_ref[0, 2 * c], par_ref[0, 2 * c + 1])
    sink_col = jnp.where(row_lo, par_ref[1, 2 * c], par_ref[1, 2 * c + 1])

    iq = lax.broadcasted_iota(jnp.int32, (2 * blk, span), 0) % blk
    ik = lax.broadcasted_iota(jnp.int32, (2 * blk, span), 1)

    @pl.when(pl.program_id(1) == 0)
    def _():
        _fill_band_bias(bias_scr, iq, ik, radius, 1, slope_col)

    lo_q = _lo_mask((1, LANES))
    lo = _lo_mask((blk, LANES))

    def body(g, carry):
        starts, variants, values, scores = [], [], [], []
        for u in range(group):
            a = pl.multiple_of((g * group + u) * blk, blk)
            ks = pl.multiple_of(jnp.clip(a - radius, 0, seq - span), radius)
            q = q_ref[pl.ds(a, blk), :]
            zero = jnp.zeros_like(q)
            qm = jnp.concatenate([jnp.where(lo_q, q, zero), jnp.where(lo_q, zero, q)], axis=0)
            k = k_ref[pl.ds(ks, span), :]
            scores.append(lax.dot_general(qm, k, (((1,), (1,)), ((), ())),
                                          preferred_element_type=F32))
            starts.append(a)
            variants.append((a - ks) // radius)
            values.append(v_ref[pl.ds(ks, span), :])
        probs, sink_terms = [], []
        for u in range(group):
            s = scores[u] + bias_scr[variants[u]]
            m = jnp.maximum(jnp.max(s, axis=-1, keepdims=True), sink_col)
            probs.append(_exp2_probs(s, m))
            sink_terms.append(jnp.exp2(sink_col - m))
        for u in range(group):
            o, l = _value_and_sum(probs[u], values[u])
            o = o * (1.0 / (l + sink_terms[u]))
            o_ref[pl.ds(starts[u], blk), :] = jnp.where(lo, o[:blk], o[blk:]).astype(o_ref.dtype)
        return carry

    lax.fori_loop(0, seq // (blk * group), body, 0)


def _window_attention(qkv, params, *, q_blk0, k_blk, v_blk, n_pairs, radius, blk):
    b, s, _ = qkv.shape
    span = blk + 2 * radius
    return pl.pallas_call(
        functools.partial(_window_kernel, seq=s, blk=blk, radius=radius),
        grid=(n_pairs, b),
        in_specs=[
            pl.BlockSpec(memory_space=pltpu.SMEM),
            pl.BlockSpec((None, s, LANES), lambda c, bi: (bi, 0, q_blk0 + c)),
            pl.BlockSpec((None, s, LANES), lambda c, bi: (bi, 0, k_blk)),
            pl.BlockSpec((None, s, LANES), lambda c, bi: (bi, 0, v_blk)),
        ],
        out_specs=pl.BlockSpec((None, s, LANES), lambda c, bi: (bi, 0, c)),
        out_shape=jax.ShapeDtypeStruct((b, s, n_pairs * LANES), BF16),
        scratch_shapes=[pltpu.VMEM((3, 2 * blk, span), F32)],
        compiler_params=pltpu.CompilerParams(
            dimension_semantics=("arbitrary", "arbitrary"), vmem_limit_bytes=VMEM_LIMIT),
        name="window_attn",
    )(params, qkv, qkv, qkv)


PERM_CHUNK = 256
PHASES4 = 4
PIECE = PERM_CHUNK // PHASES4


def _stack_heads(q):
    lo = _lo_mask((1, LANES))
    zero = jnp.zeros_like(q)
    return jnp.concatenate([jnp.where(lo, q, zero), jnp.where(lo, zero, q)], axis=0)


def _merge_heads(x, blk):
    x = jnp.broadcast_to(x, (2 * blk, LANES))
    return jnp.where(_lo_mask((blk, LANES)), x[:blk], x[blk:])


def _attend_group(qs, ks, vs, biases):
    scores = [lax.dot_general(_stack_heads(q), k, (((1,), (1,)), ((), ())),
                              preferred_element_type=F32) for q, k in zip(qs, ks)]
    stats = []
    for s, bias in zip(scores, biases):
        s = s + bias
        m = jnp.max(s, axis=-1, keepdims=True)
        stats.append((_exp2_probs(s, m), m))
    outs = []
    for (p, m), v in zip(stats, vs):
        o, l = _value_and_sum(p, v)
        outs.append((o, m, l))
    return outs


def _fill_band_bias(bias_scr, pos_q, pos_k, radius, pos_scale, slope_col):
    for var in range(3):
        dist = jnp.abs(pos_q + var * radius - pos_k)
        dist_f = (dist * pos_scale).astype(F32)
        bias_scr[var] = jnp.where(dist <= radius, -(slope_col * dist_f) * LOG2E, NEG_INF)


def _dilated_kernel(par_ref, q_ref, k_ref, v_ref, o_ref, qp, kp, vp, acc_p, m_p, l_p,
                    bias1, bias4, bias16, *, seq, blk, radius):
    span = blk + 2 * radius
    n_chunk = seq // PERM_CHUNK
    group = ATTN_UNROLL
    c = pl.program_id(0)
    row_lo = lax.broadcasted_iota(jnp.int32, (2 * blk, 1), 0) < blk
    slope_col = jnp.where(row_lo, par_ref[0, 2 * c], par_ref[0, 2 * c + 1])

    def piece_pos(x):
        return PIECE * (x // PIECE) + PHASES4 * (x % 16) + (x % PIECE) // 16

    iq = lax.broadcasted_iota(jnp.int32, (2 * blk, span), 0) % blk
    ik = lax.broadcasted_iota(jnp.int32, (2 * blk, span), 1)

    @pl.when(pl.program_id(1) == 0)
    def _():
        _fill_band_bias(bias1, iq, ik, radius, 1, slope_col)
        _fill_band_bias(bias4, piece_pos(iq), piece_pos(ik), radius, 4, slope_col)
        _fill_band_bias(bias16, iq, ik, radius, 16, slope_col)

    rr = lax.broadcasted_iota(jnp.int32, (PERM_CHUNK, PERM_CHUNK), 0)
    cc = lax.broadcasted_iota(jnp.int32, (PERM_CHUNK, PERM_CHUNK), 1)
    to_phase = jnp.where(cc == 16 * (rr % 16) + PHASES4 * ((rr // 16) % 4) + rr // PIECE,
                         1.0, 0.0).astype(BF16)
    to_token = jnp.where(cc == PIECE * (rr % PHASES4) + 16 * ((rr // PHASES4) % 4) + rr // 16,
                         1.0, 0.0).astype(BF16)

    def permute_body(ch2, carry):
        for half in range(2):
            ch = 2 * ch2 + half
            r0 = pl.multiple_of(ch * PERM_CHUNK, PERM_CHUNK)
            x = jnp.concatenate([q_ref[pl.ds(r0, PERM_CHUNK), :],
                                 k_ref[pl.ds(r0, PERM_CHUNK), :],
                                 v_ref[pl.ds(r0, PERM_CHUNK), :]], axis=1)
            y = jnp.dot(to_phase, x, preferred_element_type=F32).astype(BF16)
            for p4 in range(PHASES4):
                rows = slice(PIECE * p4, PIECE * (p4 + 1))
                qp[p4, ch] = y[rows, 0:LANES]
                kp[p4, ch] = y[rows, LANES:2 * LANES]
                vp[p4, ch] = y[rows, 2 * LANES:3 * LANES]
        return carry

    lax.fori_loop(0, n_chunk // 2, permute_body, 0)

    pieces_per_blk = blk // PIECE
    n_blk4 = seq // (PHASES4 * blk)

    def dil4_body(g, carry):
        p4 = g // (n_blk4 // group)
        j0 = (g % (n_blk4 // group)) * group
        qs, ks, vs, bs = [], [], [], []
        for u in range(group):
            pc = pieces_per_blk * (j0 + u)
            cs = jnp.clip(pc - radius // PIECE, 0, n_chunk - span // PIECE)
            qs.append(qp[p4, pl.ds(pc, pieces_per_blk)].reshape(blk, LANES))
            ks.append(kp[p4, pl.ds(cs, span // PIECE)].reshape(span, LANES))
            vs.append(vp[p4, pl.ds(cs, span // PIECE)].reshape(span, LANES))
            bs.append(bias4[(pc - cs) // (radius // PIECE)])
        for u, (o, m, l) in enumerate(_attend_group(qs, ks, vs, bs)):
            dst = (p4, pl.ds(pieces_per_blk * (j0 + u), pieces_per_blk))
            acc_p[dst] = _merge_heads(o, blk).reshape(pieces_per_blk, PIECE, LANES)
            m_p[dst] = _merge_heads(m, blk).reshape(pieces_per_blk, PIECE, LANES)
            l_p[dst] = _merge_heads(l, blk).reshape(pieces_per_blk, PIECE, LANES)
        return carry

    lax.fori_loop(0, PHASES4 * n_blk4 // group, dil4_body, 0)

    chunks_per_blk = blk // 16

    def dil16_body(p4, carry):
        qs, ks, vs, bs, dsts = [], [], [], [], []
        for mm in range(PIECE // 16):
            off = mm * 16
            kk = kp[p4, :, pl.ds(off, 16), :].reshape(span, LANES)
            vv = vp[p4, :, pl.ds(off, 16), :].reshape(span, LANES)
            for jb in range(n_chunk // chunks_per_blk):
                dst = (p4, pl.ds(chunks_per_blk * jb, chunks_per_blk), pl.ds(off, 16), slice(None))
                qs.append(qp[dst].reshape(blk, LANES))
                ks.append(kk)
                vs.append(vv)
                bs.append(bias16[jb * (blk // radius)])
                dsts.append(dst)
        for dst, (o, m, l) in zip(dsts, _attend_group(qs, ks, vs, bs)):
            m_old = m_p[dst].reshape(blk, LANES)
            m_new = jnp.maximum(m_old, _merge_heads(m, blk))
            w_old = jnp.exp2(m_old - m_new)
            w_new = jnp.exp2(_merge_heads(m, blk) - m_new)
            acc = w_old * acc_p[dst].reshape(blk, LANES) + w_new * _merge_heads(o, blk)
            den = w_old * l_p[dst].reshape(blk, LANES) + w_new * _merge_heads(l, blk)
            acc_p[dst] = acc.reshape(chunks_per_blk, 16, LANES)
            l_p[dst] = den.reshape(chunks_per_blk, 16, LANES)
            m_p[dst] = m_new.reshape(chunks_per_blk, 16, LANES)
        return carry

    lax.fori_loop(0, PHASES4, dil16_body, 0)

    blk_per_chunk = PERM_CHUNK // blk

    def dil1_body(g, carry):
        merged = []
        for cc2 in range(group // blk_per_chunk):
            ch = g * (group // blk_per_chunk) + cc2
            acc = jnp.concatenate([acc_p[p4, ch] for p4 in range(PHASES4)], axis=0)
            den = jnp.concatenate([l_p[p4, ch] for p4 in range(PHASES4)], axis=0)
            mx = jnp.concatenate([m_p[p4, ch] for p4 in range(PHASES4)], axis=0)
            o_ph = (acc * (1.0 / den)).astype(BF16)
            lse = mx + jnp.log2(den)
            hi = lse.astype(BF16)
            rem = lse - hi.astype(F32)
            mid = rem.astype(BF16)
            low = (rem - mid.astype(F32)).astype(BF16)
            back = jnp.dot(to_token, jnp.concatenate([o_ph, hi, mid, low], axis=1),
                           preferred_element_type=F32)
            merged.append((back[:, 0:LANES], back[:, LANES:2 * LANES]
                           + back[:, 2 * LANES:3 * LANES] + back[:, 3 * LANES:4 * LANES]))
        qs, ks, vs, bs, starts = [], [], [], [], []
        for u in range(group):
            a = pl.multiple_of((g * group + u) * blk, blk)
            kst = pl.multiple_of(jnp.clip(a - radius, 0, seq - span), radius)
            qs.append(q_ref[pl.ds(a, blk), :])
            ks.append(k_ref[pl.ds(kst, span), :])
            vs.append(v_ref[pl.ds(kst, span), :])
            bs.append(bias1[(a - kst) // radius])
            starts.append(a)
        for u, (o, m, l) in enumerate(_attend_group(qs, ks, vs, bs)):
            o1 = _merge_heads(o * (1.0 / l), blk)
            lse1 = _merge_heads(m + jnp.log2(l), blk)
            o2, lse2 = merged[u // blk_per_chunk]
            half = slice((u % blk_per_chunk) * blk, (u % blk_per_chunk + 1) * blk)
            o2, lse2 = o2[half], lse2[half]
            top = jnp.maximum(lse1, lse2)
            e1 = jnp.exp2(lse1 - top)
            e2 = jnp.exp2(lse2 - top)
            o_ref[pl.ds(starts[u], blk), :] = ((e1 * o1 + e2 * o2)
                                               * (1.0 / (e1 + e2))).astype(o_ref.dtype)
        return carry

    lax.fori_loop(0, seq // (blk * group), dil1_body, 0)


def _dilated_attention(qkv, params, *, k_blk0, v_blk0, n_pairs, blk, radius):
    b, s, _ = qkv.shape
    span = blk + 2 * radius
    n_chunk = s // PERM_CHUNK
    assert s // 16 == span and s % PERM_CHUNK == 0 and blk % PIECE == 0
    phase = lambda dt: pltpu.VMEM((PHASES4, n_chunk, PIECE, LANES), dt)
    return pl.pallas_call(
        functools.partial(_dilated_kernel, seq=s, blk=blk, radius=radius),
        grid=(n_pairs, b),
        in_specs=[
            pl.BlockSpec(memory_space=pltpu.SMEM),
            pl.BlockSpec((None, s, LANES), lambda c, bi: (bi, 0, c)),
            pl.BlockSpec((None, s, LANES), lambda c, bi: (bi, 0, k_blk0 + c)),
            pl.BlockSpec((None, s, LANES), lambda c, bi: (bi, 0, v_blk0 + c)),
        ],
        out_specs=pl.BlockSpec((None, s, LANES), lambda c, bi: (bi, 0, c)),
        out_shape=jax.ShapeDtypeStruct((b, s, n_pairs * LANES), BF16),
        scratch_shapes=[phase(BF16), phase(BF16), phase(BF16), phase(F32), phase(F32), phase(F32)]
        + [pltpu.VMEM((3, 2 * blk, span), F32)] * 3,
        compiler_params=pltpu.CompilerParams(
            dimension_semantics=("arbitrary", "arbitrary"), vmem_limit_bytes=VMEM_LIMIT),
        name="dilated_attn",
    )(params, qkv, qkv, qkv)


def _na_kernel(q_ref, k_ref, v_ref, bias_ref, o_ref, *, rows):
    kh = NA_ROWS
    group = ATTN_UNROLL
    lo_q = _lo_mask((1, LANES))
    lo = _lo_mask((GRID_W, LANES))

    def body(g, carry):
        starts, offs, values, scores = [], [], [], []
        for u in range(group):
            i = g * group + u
            rs = jnp.clip(i - kh // 2, 0, rows - kh)
            qs = pl.multiple_of(i * GRID_W, GRID_W)
            ks = pl.multiple_of(rs * GRID_W, GRID_W)
            q = q_ref[pl.ds(qs, GRID_W), :]
            zero = jnp.zeros_like(q)
            qm = jnp.concatenate([jnp.where(lo_q, q, zero), jnp.where(lo_q, zero, q)], axis=0)
            k = k_ref[pl.ds(ks, kh * GRID_W), :]
            scores.append(lax.dot_general(qm, k, (((1,), (1,)), ((), ())),
                                          preferred_element_type=F32))
            starts.append(qs)
            offs.append(rs - i + (NA_ROWS - 1))
            values.append(v_ref[pl.ds(ks, kh * GRID_W), :])
        probs = []
        for u in range(group):
            s = scores[u] + jnp.concatenate(
                [bias_ref[offs[u] + 2 * j] for j in range(kh // 2)], axis=1)
            probs.append(_exp2_probs(s, jnp.max(s, axis=-1, keepdims=True)))
        for u in range(group):
            o, l = _value_and_sum(probs[u], values[u])
            o = o * (1.0 / l)
            o_ref[pl.ds(starts[u], GRID_W), :] = jnp.where(
                lo, o[:GRID_W], o[GRID_W:]).astype(o_ref.dtype)
        return carry

    lax.fori_loop(0, rows // group, body, 0)


def _na_bias_table(rpb):
    qc = np.arange(GRID_W)[:, None]
    kc = np.arange(GRID_W)[None, :]
    qstart = np.clip(qc - NA_COLS // 2, 0, GRID_W - NA_COLS)
    valid = (kc >= qstart) & (kc < qstart + NA_COLS)
    coff = np.clip(kc - qc, -(NA_COLS - 1), NA_COLS - 1) + NA_COLS - 1
    n_rel = 2 * NA_COLS - 1
    n_off = 2 * NA_ROWS - 2
    pick = (coff[:, :, None] == np.arange(n_rel)).astype(np.float32)
    pick2 = np.zeros((GRID_W, 2, GRID_W, 2, n_rel), np.float32)
    pick2[:, 0, :, 0] = pick
    pick2[:, 1, :, 1] = pick
    pick2 = pick2.reshape(GRID_W, 2 * GRID_W, 2 * n_rel)
    r = rpb.astype(F32) * LOG2E
    r = jnp.concatenate([r[:, :-1], r[:, 1:]], axis=-1)
    r = r.reshape(C_HEADS // 2, 2, n_off, 2 * n_rel)
    t = jnp.einsum("phdm,qnm->pdhqn", r, jnp.asarray(pick2), precision=lax.Precision.HIGHEST)
    t = jnp.where(jnp.asarray(np.tile(valid, (1, 2))), t, NEG_INF)
    return t.reshape(C_HEADS // 2, n_off, 2 * GRID_W, 2 * GRID_W)


def _na_attention(qkv, bias):
    b, s, n_cols = qkv.shape
    n_pairs = C_HEADS // 2
    rows = s // GRID_W
    return pl.pallas_call(
        functools.partial(_na_kernel, rows=rows),
        grid=(n_pairs, b),
        in_specs=[
            pl.BlockSpec((None, s, LANES), lambda c, bi: (bi, 0, c)),
            pl.BlockSpec((None, s, LANES), lambda c, bi: (bi, 0, n_pairs + c)),
            pl.BlockSpec((None, s, LANES), lambda c, bi: (bi, 0, 2 * n_pairs + c)),
            pl.BlockSpec((None,) + bias.shape[1:], lambda c, bi: (c, 0, 0, 0)),
        ],
        out_specs=pl.BlockSpec((None, s, LANES), lambda c, bi: (bi, 0, c)),
        out_shape=jax.ShapeDtypeStruct((b, s, n_pairs * LANES), BF16),
        compiler_params=pltpu.CompilerParams(
            dimension_semantics=("arbitrary", "arbitrary"), vmem_limit_bytes=VMEM_LIMIT),
        name="na_attn",
    )(qkv, qkv, qkv, bias)


def _silu(g):
    return g * (1.0 / (1.0 + jnp.exp(-g)))


def _ffn_kernel(x_ref, oa_ref, ob_ref, wa_ref, wb_ref, g_ref, wg_ref, wu_ref, wd_ref,
                c0_ref, c1_ref, c2_ref, y_ref, n0_ref, n1_ref, n2_ref, a_scr):
    n0_ref[...] = c0_ref[...].astype(n0_ref.dtype)
    n1_ref[...] = c1_ref[...].astype(n1_ref.dtype)
    n2_ref[...] = c2_ref[...].astype(n2_ref.dtype)
    x = (x_ref[...] + jnp.dot(oa_ref[...], wa_ref[...], preferred_element_type=F32)
         + jnp.dot(ob_ref[...], wb_ref[...], preferred_element_type=F32))
    h = _rms_rows(x, g_ref[...]).astype(BF16)
    f = wg_ref.shape[1]
    for j in range(f // MXU_N):
        sl = slice(j * MXU_N, (j + 1) * MXU_N)
        gate = jnp.dot(h, wg_ref[:, sl], preferred_element_type=F32)
        up = jnp.dot(h, wu_ref[:, sl], preferred_element_type=F32)
        a_scr[:, sl] = (_silu(gate) * up).astype(BF16)
    y_ref[...] = x + jnp.dot(a_scr[...], wd_ref[...], preferred_element_type=F32)


def _ffn(x, oa, ob, wa, wb, g, wg, wu, wd, narrow, tm=256):
    t, d = x.shape
    f = wg.shape[1]
    steps = t // tm
    const = lambda a: pl.BlockSpec(a.shape, lambda i: (0, 0), pipeline_mode=pl.Buffered(1))
    row = lambda w: pl.BlockSpec((tm, w), lambda i: (i, 0))
    slab = lambda a: pl.BlockSpec((a.shape[0] // steps, a.shape[1]), lambda i: (i, 0))
    assert all(a.shape[0] % (16 * steps) == 0 for a in narrow)
    return pl.pallas_call(
        _ffn_kernel,
        grid=(steps,),
        in_specs=[row(d), row(oa.shape[1]), row(ob.shape[1]), const(wa), const(wb),
                  pl.BlockSpec((1, d), lambda i: (0, 0)),
                  const(wg), const(wu), const(wd)] + [slab(a) for a in narrow],
        out_specs=[pl.BlockSpec((tm, d), lambda i: (i, 0))] + [slab(a) for a in narrow],
        out_shape=[jax.ShapeDtypeStruct((t, d), F32)]
        + [jax.ShapeDtypeStruct(a.shape, BF16) for a in narrow],
        scratch_shapes=[pltpu.VMEM((tm, f), BF16)],
        compiler_params=pltpu.CompilerParams(
            dimension_semantics=("arbitrary",), vmem_limit_bytes=VMEM_LIMIT),
        name="dense_swiglu",
    )(x, oa, ob, wa, wb, g.reshape(1, d), wg, wu, wd, *narrow)


TOP_K = 2
MOE_TB = 512
MOE_TM = 512
MOE_RC = 256
KEY_STRIDE = 65536.0


def _router_kernel(x_ref, o_ref, wo_ref, g_ref, wr_hi_ref, wr_lo_ref, x1_ref, h_ref, info_ref,
                   info_t_ref, cnt_ref, carry_scr, ltri_scr):
    i = pl.program_id(0)
    tb = x_ref.shape[0]
    parts = 2
    hb = tb // parts
    rows = [slice(p * hb, (p + 1) * hb) for p in range(parts)]
    lane = lax.broadcasted_iota(jnp.int32, (1, LANES), 1)

    @pl.when(i == 0)
    def _():
        carry_scr[...] = jnp.zeros_like(carry_scr)
        r = lax.broadcasted_iota(jnp.int32, (hb, hb), 0)
        cidx = lax.broadcasted_iota(jnp.int32, (hb, hb), 1)
        ltri_scr[...] = jnp.where(cidx < r, 1.0, 0.0).astype(BF16)

    x1s = [x_ref[rw, :] + jnp.dot(o_ref[rw, :], wo_ref[...], preferred_element_type=F32)
           for rw in rows]
    hs = [_rms_rows(x1, g_ref[...]) for x1 in x1s]
    his = [h.astype(BF16) for h in hs]
    los = [(h - hi.astype(F32)).astype(BF16) for h, hi in zip(hs, his)]
    logits_all = [jnp.dot(hi, wr_hi_ref[...], preferred_element_type=F32)
                  + jnp.dot(lo_, wr_hi_ref[...], preferred_element_type=F32)
                  + jnp.dot(hi, wr_lo_ref[...], preferred_element_type=F32)
                  for hi, lo_ in zip(his, los)]
    for rw, x1, hi in zip(rows, x1s, his):
        x1_ref[rw, :] = x1
        h_ref[rw, :] = hi

    picks = []
    for logits in logits_all:
        logits = jnp.where(lane < N_EXPERTS, logits, NEG_INF)
        v1 = jnp.max(logits, axis=-1, keepdims=True)
        i1 = jnp.min(jnp.where(logits == v1, lane, LANES), axis=-1, keepdims=True)
        rest = jnp.where(lane == i1, NEG_INF, logits)
        v2 = jnp.max(rest, axis=-1, keepdims=True)
        i2 = jnp.min(jnp.where(rest == v2, lane, LANES), axis=-1, keepdims=True)
        e2 = jnp.exp(v2 - v1)
        g1 = 1.0 / (1.0 + e2)
        picks.append((i1, i2, g1, e2 * g1))

    pickeds = [jnp.where(lane == i1, 1.0, jnp.where(lane == i2, 1.0, 0.0))
               for i1, i2, _, _ in picks]
    inside = [jnp.dot(ltri_scr[...], pk.astype(BF16), preferred_element_type=F32)
              for pk in pickeds]
    counts = [jnp.sum(pk, axis=0, keepdims=True) for pk in pickeds]
    seen = carry_scr[...]
    for p, (i1, i2, g1, g2) in enumerate(picks):
        before = inside[p] + seen
        rank1 = jnp.sum(jnp.where(lane == i1, before, 0.0), axis=-1, keepdims=True)
        rank2 = jnp.sum(jnp.where(lane == i2, before, 0.0), axis=-1, keepdims=True)
        key1 = i1.astype(F32) * KEY_STRIDE + rank1
        key2 = i2.astype(F32) * KEY_STRIDE + rank2
        info = jnp.where(lane == 0, key1, jnp.where(lane == 1, key2,
                         jnp.where(lane == 2, g1, jnp.where(lane == 3, g2, 0.0))))
        info_ref[rows[p], :] = info
        info_t_ref[:, rows[p]] = info.T[:8, :]
        seen = seen + counts[p]
    cnt_ref[...] = jnp.broadcast_to(seen - carry_scr[...], cnt_ref.shape)
    carry_scr[...] = seen


def _router(x, o, wo, g, wr):
    t, d = x.shape
    nb = t // MOE_TB
    wr_hi = wr.astype(BF16)
    wr_lo = (wr - wr_hi.astype(F32)).astype(BF16)
    row = lambda w: pl.BlockSpec((MOE_TB, w), lambda i: (i, 0))
    const = lambda a: pl.BlockSpec(a.shape, lambda i: (0, 0))
    return pl.pallas_call(
        _router_kernel,
        grid=(nb,),
        in_specs=[row(d), row(o.shape[1]), const(wo), pl.BlockSpec((1, d), lambda i: (0, 0)),
                  const(wr_hi), const(wr_lo)],
        out_specs=[row(d), row(d), row(LANES),
                   pl.BlockSpec((None, 8, MOE_TB), lambda i: (i, 0, 0)),
                   pl.BlockSpec((None, 8, LANES), lambda i: (i, 0, 0))],
        out_shape=[jax.ShapeDtypeStruct((t, d), F32),
                   jax.ShapeDtypeStruct((t, d), BF16),
                   jax.ShapeDtypeStruct((t, LANES), F32),
                   jax.ShapeDtypeStruct((nb, 8, MOE_TB), F32),
                   jax.ShapeDtypeStruct((nb, 8, LANES), F32)],
        scratch_shapes=[pltpu.VMEM((1, LANES), F32),
                        pltpu.VMEM((MOE_TB // 2, MOE_TB // 2), BF16)],
        compiler_params=pltpu.CompilerParams(
            dimension_semantics=("arbitrary",), vmem_limit_bytes=VMEM_LIMIT),
        name="moe_router",
    )(x, o, wo, g.reshape(1, d), wr_hi, wr_lo)


def _moe_plan(block_counts, n_tokens):
    nb = block_counts.shape[0]
    n_tiles = (TOP_K * n_tokens + N_EXPERTS * (MOE_TM - 1)) // MOE_TM
    n_chunks = n_tiles * MOE_TM // MOE_RC
    tc = block_counts.astype(jnp.int32)
    cb = jnp.concatenate([jnp.zeros((1, N_EXPERTS), jnp.int32), jnp.cumsum(tc, axis=0)])
    counts = cb[-1]
    padded = ((counts + MOE_TM - 1) // MOE_TM) * MOE_TM
    ends = jnp.cumsum(padded)
    starts = ends - padded
    total = ends[-1]

    tiles = jnp.arange(n_tiles, dtype=jnp.int32)
    n_valid_tiles = total // MOE_TM
    t_valid = (tiles < n_valid_tiles).astype(jnp.int32)
    t_src = jnp.maximum(jnp.minimum(tiles, n_valid_tiles - 1), 0)
    t_exp = jnp.minimum(jnp.sum((t_src * MOE_TM)[:, None] >= ends[None, :], axis=1),
                        N_EXPERTS - 1).astype(jnp.int32)

    r0 = jnp.arange(n_chunks, dtype=jnp.int32) * MOE_RC
    ex = jnp.minimum(jnp.sum(r0[:, None] >= ends[None, :], axis=1), N_EXPERTS - 1)
    live = r0 < total
    lo = r0 - starts[ex]
    base = (ex * int(KEY_STRIDE) + lo).astype(jnp.int32)
    cbe = cb[:, ex]
    g_lo = jnp.sum(cbe[1:] <= lo[None, :], axis=0).astype(jnp.int32)
    g_cnt = jnp.sum(live[None, :] & (cbe[:-1] < (lo + MOE_RC)[None, :]) & (cbe[1:] > lo[None, :]),
                    axis=0).astype(jnp.int32)

    row_lo = starts[None, :] + cb[:-1]
    row_hi = starts[None, :] + cb[1:]
    c_first = row_lo // MOE_RC
    c_num = jnp.where(row_hi > row_lo, (row_hi - 1) // MOE_RC - c_first + 1, 0)
    c_end = jnp.cumsum(c_num, axis=1)
    slots = jnp.arange(COMBINE_MAX_CHUNKS, dtype=jnp.int32)
    e_of = jnp.minimum(jnp.sum(slots[None, :, None] >= c_end[:, None, :], axis=2),
                       N_EXPERTS - 1)
    owner = e_of[:, :, None] == jnp.arange(N_EXPERTS)[None, None, :]
    c_c = slots[None, :] + jnp.sum(
        jnp.where(owner, (c_first - (c_end - c_num))[:, None, :], 0), axis=2)
    c_cnt = c_end[:, -1].astype(jnp.int32)
    c_c = jnp.clip(c_c, 0, n_chunks - 1).astype(jnp.int32).reshape(-1)
    return dict(n_tiles=n_tiles, n_chunks=n_chunks, t_valid=t_valid, t_exp=t_exp,
                base=base, live=live.astype(jnp.int32), g_lo=g_lo, g_cnt=g_cnt,
                c_cnt=c_cnt, c_c=c_c)


def _gather_kernel(live_ref, lo_ref, cnt_ref, base_ref, info_t_ref, h_ref,
                   xs_ref, gs_ref, acc_scr, gsum_scr):
    tb = info_t_ref.shape[-1]
    per_tile = xs_ref.shape[0] // MOE_RC
    row = lax.broadcasted_iota(jnp.int32, (MOE_RC, tb), 0).astype(F32)

    def chunk_body(k, carry):
        c = pl.program_id(0) * per_tile + k
        dst = pl.ds(pl.multiple_of(k * MOE_RC, MOE_RC), MOE_RC)

        @pl.when(live_ref[c] == 1)
        def _():
            first = lo_ref[c]
            n = cnt_ref[c]
            acc_scr[...] = jnp.zeros_like(acc_scr)
            gsum_scr[...] = jnp.zeros_like(gsum_scr)

            def two_blocks(it, inner):
                rows, gate = None, None
                for w in range(2):
                    j = 2 * it + w
                    b = first + jnp.minimum(j, n - 1)
                    base = jnp.where(j < n, base_ref[c], NO_MATCH_BASE).astype(F32)
                    info = info_t_ref[b]
                    m1 = (info[0:1, :] - base) == row
                    m2 = (info[1:2, :] - base) == row
                    sel = jnp.where(m1, 1.0, jnp.where(m2, 1.0, 0.0)).astype(BF16)
                    g = jnp.sum(jnp.where(m1, info[2:3, :], jnp.where(m2, info[3:4, :], 0.0)),
                                axis=-1, keepdims=True)
                    r = jnp.dot(sel, h_ref[pl.ds(pl.multiple_of(b * tb, tb), tb), :],
                                preferred_element_type=F32)
                    rows = r if rows is None else rows + r
                    gate = g if gate is None else gate + g
                acc_scr[...] += rows
                gsum_scr[...] += gate
                return inner

            lax.fori_loop(0, (n + 1) // 2, two_blocks, 0)
            xs_ref[dst, :] = acc_scr[...].astype(xs_ref.dtype)
            gs_ref[dst, :] = jnp.broadcast_to(gsum_scr[...], (MOE_RC, gs_ref.shape[1]))

        @pl.when(live_ref[c] == 0)
        def _():
            xs_ref[dst, :] = jnp.zeros((MOE_RC, xs_ref.shape[1]), xs_ref.dtype)
            gs_ref[dst, :] = jnp.zeros((MOE_RC, gs_ref.shape[1]), gs_ref.dtype)

        return carry

    lax.fori_loop(0, per_tile, chunk_body, 0)


def _moe_gather(plan, info_t, h):
    t, d = h.shape
    rows = plan["n_chunks"] * MOE_RC
    tile = lambda i, live, lo, cn, ba: (i, 0)
    grid_spec = pltpu.PrefetchScalarGridSpec(
        num_scalar_prefetch=4,
        grid=(plan["n_tiles"],),
        in_specs=[pl.BlockSpec(info_t.shape, lambda c, live, lo, cn, ba: (0, 0, 0),
                               pipeline_mode=pl.Buffered(1)),
                  pl.BlockSpec((t, d), lambda c, live, lo, cn, ba: (0, 0),
                               pipeline_mode=pl.Buffered(1))],
        out_specs=[pl.BlockSpec((MOE_TM, d), tile), pl.BlockSpec((MOE_TM, LANES), tile)],
        scratch_shapes=[pltpu.VMEM((MOE_RC, d), F32), pltpu.VMEM((MOE_RC, 1), F32)],
    )
    return pl.pallas_call(
        _gather_kernel,
        grid_spec=grid_spec,
        out_shape=[jax.ShapeDtypeStruct((rows, d), BF16),
                   jax.ShapeDtypeStruct((rows, LANES), F32)],
        compiler_params=pltpu.CompilerParams(
            dimension_semantics=("arbitrary",), vmem_limit_bytes=VMEM_LIMIT),
        name="moe_gather",
    )(plan["live"], plan["g_lo"], plan["g_cnt"], plan["base"], info_t, h)


def _experts_kernel(exp_ref, val_ref, xs_ref, gs_ref, wg_ref, wu_ref, wd_ref, ys_ref,
                    acc_scr):
    i = pl.program_id(0)
    j = pl.program_id(1)

    last = j == pl.num_programs(1) - 1

    @pl.when(j == 0)
    def _():
        acc_scr[...] = jnp.zeros_like(acc_scr)

    @pl.when(val_ref[i] == 1)
    def _():
        x = xs_ref[...]
        gate = jnp.dot(x, wg_ref[...], preferred_element_type=F32)
        up = jnp.dot(x, wu_ref[...], preferred_element_type=F32)
        a = (_silu(gate) * up).astype(BF16)
        acc_scr[...] += jnp.dot(a, wd_ref[...], preferred_element_type=F32)

    @pl.when(last)
    def _():
        ys_ref[...] = (acc_scr[...] * gs_ref[:, 0:1]).astype(ys_ref.dtype)


def _moe_experts(plan, xs, gs, wg, wu, wd, tf):
    rows, d = xs.shape
    f = wg.shape[2]
    nf = f // tf

    def wcol(i, j, ex, val):
        return (ex[i], 0, jnp.where(val[i] == 1, j, nf - 1))

    def wrow(i, j, ex, val):
        return (ex[i], jnp.where(val[i] == 1, j, nf - 1), 0)

    tile = lambda i, j, ex, val: (i, 0)
    grid_spec = pltpu.PrefetchScalarGridSpec(
        num_scalar_prefetch=2,
        grid=(plan["n_tiles"], nf),
        in_specs=[pl.BlockSpec((MOE_TM, d), tile),
                  pl.BlockSpec((MOE_TM, LANES), tile),
                  pl.BlockSpec((None, d, tf), wcol),
                  pl.BlockSpec((None, d, tf), wcol),
                  pl.BlockSpec((None, tf, d), wrow)],
        out_specs=pl.BlockSpec((MOE_TM, d), tile),
        scratch_shapes=[pltpu.VMEM((MOE_TM, d), F32)],
    )
    return pl.pallas_call(
        _experts_kernel,
        grid_spec=grid_spec,
        out_shape=jax.ShapeDtypeStruct((rows, d), BF16),
        compiler_params=pltpu.CompilerParams(
            dimension_semantics=("arbitrary", "arbitrary"), vmem_limit_bytes=VMEM_LIMIT),
        name="moe_experts",
    )(plan["t_exp"], plan["t_valid"], xs, gs, wg, wu, wd)


NO_MATCH_BASE = -(2 ** 30)


COMBINE_GROUP = 4
COMBINE_MAX_CHUNKS = N_EXPERTS * (MOE_TB // MOE_RC + 1)


def _combine_kernel(cnt_ref, pc_ref, base_ref, info_ref, x_ref, ys_hbm, o_ref, ybuf, sem):
    b = pl.program_id(0)
    tb = x_ref.shape[0]
    d = x_ref.shape[1]

    def n_fetch(blk):
        return ((cnt_ref[blk] + COMBINE_GROUP - 1) // COMBINE_GROUP) * COMBINE_GROUP

    def chunk_of(blk, j):
        return pc_ref[blk * COMBINE_MAX_CHUNKS + jnp.minimum(j, cnt_ref[blk] - 1)]

    def copy(blk, j, slot):
        row0 = pl.multiple_of(chunk_of(blk, j) * MOE_RC, MOE_RC)
        return pltpu.make_async_copy(ys_hbm.at[pl.ds(row0, MOE_RC), :], ybuf.at[slot, j],
                                     sem.at[slot])

    def start_block(blk, slot):
        def issue(j, carry):
            copy(blk, j, slot).start()
            return carry
        lax.fori_loop(0, n_fetch(blk), issue, 0)

    @pl.when(b == 0)
    def _():
        start_block(0, 0)

    @pl.when(b + 1 < pl.num_programs(0))
    def _():
        start_block(b + 1, (b + 1) % 2)

    slot = b % 2
    n = cnt_ref[b]

    def drain(j, carry):
        copy(b, j, slot).wait()
        return carry

    lax.fori_loop(0, n_fetch(b), drain, 0)

    lane = lax.broadcasted_iota(jnp.int32, (1, LANES), 1)
    info = info_ref[...]
    k1 = jnp.sum(jnp.where(lane == 0, info, 0.0), axis=-1, keepdims=True)
    k2 = jnp.sum(jnp.where(lane == 1, info, 0.0), axis=-1, keepdims=True)
    col = lax.broadcasted_iota(jnp.int32, (tb, MOE_RC), 1).astype(F32)
    o_ref[...] = x_ref[...]

    def group(gi, carry):
        sels = []
        for w in range(COMBINE_GROUP):
            j = gi * COMBINE_GROUP + w
            base = jnp.where(j < n, base_ref[chunk_of(b, j)], NO_MATCH_BASE).astype(F32)
            sels.append(jnp.where(k1 - base == col, 1.0,
                                  jnp.where(k2 - base == col, 1.0, 0.0)).astype(BF16))
        rows = ybuf[slot, pl.ds(gi * COMBINE_GROUP, COMBINE_GROUP)]
        o_ref[...] += jnp.dot(jnp.concatenate(sels, axis=1),
                              rows.reshape(COMBINE_GROUP * MOE_RC, d),
                              preferred_element_type=F32)
        return carry

    lax.fori_loop(0, n_fetch(b) // COMBINE_GROUP, group, 0)


def _moe_combine(plan, info, ys, x):
    t, d = x.shape
    block = lambda b, cn, pc, ba: (b, 0)
    grid_spec = pltpu.PrefetchScalarGridSpec(
        num_scalar_prefetch=3,
        grid=(t // MOE_TB,),
        in_specs=[pl.BlockSpec((MOE_TB, LANES), block),
                  pl.BlockSpec((MOE_TB, d), block),
                  pl.BlockSpec(memory_space=pl.ANY)],
        out_specs=pl.BlockSpec((MOE_TB, d), block),
        scratch_shapes=[pltpu.VMEM((2, COMBINE_MAX_CHUNKS, MOE_RC, d), BF16),
                        pltpu.SemaphoreType.DMA((2,))],
    )
    return pl.pallas_call(
        _combine_kernel,
        grid_spec=grid_spec,
        out_shape=jax.ShapeDtypeStruct((t, d), F32),
        compiler_params=pltpu.CompilerParams(
            dimension_semantics=("arbitrary",), vmem_limit_bytes=VMEM_LIMIT),
        name="moe_combine",
    )(plan["c_cnt"], plan["c_c"], plan["base"], info, x, ys)


def _moe(x, o, wo, g, wr, wg, wu, wd, tf=1792):
    t, _ = x.shape
    x1, h, info, info_t, block_counts = _router(x, o, wo, g, wr)
    plan = _moe_plan(block_counts[:, 0, :N_EXPERTS], t)
    xs, gs = _moe_gather(plan, info_t, h)
    ys = _moe_experts(plan, xs, gs, wg, wu, wd, tf)
    return _moe_combine(plan, info, ys, x1)


def _alibi_slopes():
    s = np.exp2(-8.0 * np.arange(1, N_HEADS + 1) / N_HEADS).astype(np.float32)
    return s[0::2][:A_HEADS], s[1::2][:B_Q_HEADS]


_B_HEAD_ORDER = np.array([0, 4, 1, 5, 2, 6, 3, 7])


def _even_layer(x2d, batch, norm1, w_in, qn_a, kn_a, qn_b, kn_b, sink_b, w_out,
                norm2, w_gate, w_up, w_down, narrow):
    t, d = x2d.shape
    s = t // batch
    wa = A_HEADS * HEAD_DIM
    wq_b = B_Q_HEADS * HEAD_DIM
    wkv_b = B_KV_HEADS * HEAD_DIM
    qb0 = 3 * wa
    q_scale = HEAD_DIM ** -0.5 * LOG2E

    w_qb = w_in[:, qb0:qb0 + wq_b].reshape(d, B_Q_HEADS, HEAD_DIM)[:, _B_HEAD_ORDER]
    w_in_p = jnp.concatenate(
        [w_in[:, :qb0], w_qb.reshape(d, wq_b), w_in[:, qb0 + wq_b:]], axis=1).astype(BF16)
    ones = jnp.ones((HEAD_DIM,), F32)
    col_gain = jnp.concatenate([
        jnp.tile(qn_a * q_scale, A_HEADS), jnp.tile(kn_a, A_HEADS), jnp.tile(ones, A_HEADS),
        jnp.tile(qn_b * q_scale, B_Q_HEADS), jnp.tile(kn_b, B_KV_HEADS),
        jnp.tile(ones, B_KV_HEADS)])
    cpl = wa // LANES
    norm_chunks = ((True,) * (2 * cpl) + (False,) * cpl + (True,) * (wq_b // LANES)
                   + (True,) * (wkv_b // LANES) + (False,) * (wkv_b // LANES))
    qkv = _norm_proj(x2d, norm1, w_in_p, col_gain, norm_chunks)
    n_cols = qkv.shape[1]
    qkv3 = qkv.reshape(batch, s, n_cols)

    slopes_a, slopes_b = _alibi_slopes()
    par_a = jnp.asarray(slopes_a).reshape(1, A_HEADS)
    radii = {window // (2 * dil) for window, dil in DIL_CONFIGS}
    assert tuple(dil for _, dil in DIL_CONFIGS) == (1, 4, 16) and len(radii) == 1
    oa = _dilated_attention(qkv3, par_a, k_blk0=cpl, v_blk0=2 * cpl, n_pairs=cpl,
                            blk=A_BLOCK, radius=radii.pop())
    par_b = jnp.stack([jnp.asarray(slopes_b[_B_HEAD_ORDER]),
                       sink_b.astype(F32)[_B_HEAD_ORDER] * LOG2E])
    kb = 3 * cpl + wq_b // LANES
    ob = _window_attention(qkv3, par_b, q_blk0=3 * cpl, k_blk=kb, v_blk=kb + wkv_b // LANES,
                           n_pairs=wq_b // LANES, radius=B_WINDOW, blk=B_BLOCK)
    w_out_a = w_out[:wa].astype(BF16)
    w_out_b = w_out[wa:].reshape(B_Q_HEADS, HEAD_DIM, d)[_B_HEAD_ORDER].reshape(wq_b, d)
    y, *narrowed = _ffn(x2d, oa.reshape(t, wa), ob.reshape(t, wq_b), w_out_a,
                        w_out_b.astype(BF16), norm2, w_gate.astype(BF16), w_up.astype(BF16),
                        w_down.astype(BF16), narrow)
    return y, narrowed


def _odd_layer(x2d, batch, norm1, w_qkv, qn, kn, rpb, w_out, norm2, w_router,
               w_gate, w_up, w_down):
    t, d = x2d.shape
    s = t // batch
    wc = C_HEADS * HEAD_DIM
    ones = jnp.ones((HEAD_DIM,), F32)
    col_gain = jnp.concatenate([jnp.tile(qn * (HEAD_DIM ** -0.5 * LOG2E), C_HEADS),
                                jnp.tile(kn, C_HEADS), jnp.tile(ones, C_HEADS)])
    cpl = wc // LANES
    norm_chunks = (True,) * (2 * cpl) + (False,) * cpl
    qkv = _norm_proj(x2d, norm1, w_qkv.astype(BF16), col_gain, norm_chunks)
    o = _na_attention(qkv.reshape(batch, s, 3 * wc), _na_bias_table(rpb))
    wr = jnp.zeros((d, LANES), F32).at[:, :N_EXPERTS].set(w_router.astype(F32))
    return _moe(x2d, o.reshape(t, wc), w_out.astype(BF16), norm2, wr, w_gate, w_up, w_down)


def kernel(x, ev_norm1, ev_w_in, ev_qn_a, ev_kn_a, ev_qn_b, ev_kn_b, ev_sink_b, ev_w_out, ev_norm2, ev_ffn_gate, ev_ffn_up, ev_ffn_down, od_norm1, od_w_qkv, od_qn, od_kn, od_rpb, od_w_out, od_norm2, od_router, od_exp_gate, od_exp_up, od_exp_down):
    batch, s, d = x.shape
    depth = ev_norm1.shape[0] + od_norm1.shape[0]
    h = x.reshape(batch * s, d)
    n_e, _, f_e = od_exp_gate.shape[1:]
    experts = None
    for layer in range(depth):
        j = layer // 2
        if layer % 2 == 0:
            if j < od_norm1.shape[0]:
                narrow = [od_exp_gate[j].reshape(n_e * d, f_e), od_exp_up[j].reshape(n_e * d, f_e),
                          od_exp_down[j].reshape(n_e * f_e, d)]
            else:
                narrow = [jnp.zeros((batch * s // 16, LANES), F32)] * 3
            h, narrowed = _even_layer(h, batch, ev_norm1[j], ev_w_in[j], ev_qn_a[j], ev_kn_a[j],
                                      ev_qn_b[j], ev_kn_b[j], ev_sink_b[j], ev_w_out[j],
                                      ev_norm2[j], ev_ffn_gate[j], ev_ffn_up[j], ev_ffn_down[j],
                                      narrow)
            experts = (narrowed[0].reshape(n_e, d, f_e), narrowed[1].reshape(n_e, d, f_e),
                       narrowed[2].reshape(n_e, f_e, d))
        else:
            h = _odd_layer(h, batch, od_norm1[j], od_w_qkv[j], od_qn[j], od_kn[j], od_rpb[j],
                           od_w_out[j], od_norm2[j], od_router[j], *experts)
    return h.reshape(batch, s, d)
```

```python
import functools

import numpy as np
import jax
import jax.numpy as jnp
from jax import lax
from jax.experimental import pallas as pl
from jax.experimental.pallas import tpu as pltpu

D_MODEL = 1024
HEAD_DIM = 64
N_HEADS = D_MODEL // HEAD_DIM
A_HEADS = N_HEADS // 2
B_Q_HEADS = N_HEADS // 2
B_KV_HEADS = max(1, B_Q_HEADS // 4)
C_HEADS = N_HEADS
DIL_CONFIGS = ((128, 1), (512, 4), (2048, 16))
A_BLOCK = 128
B_WINDOW = 128
B_BLOCK = 128
GRID_W = 64
NA_ROWS = 8
NA_COLS = 16
N_EXPERTS = 8
RMS_EPS = 1e-6
NEG_INF = -1e30

LANES = 128
MXU_N = 256
VMEM_LIMIT = 56 * 1024 * 1024

ATTN_UNROLL = 8

F32 = jnp.float32
BF16 = jnp.bfloat16


def _lo_mask(shape):
    return lax.broadcasted_iota(jnp.int32, shape, len(shape) - 1) < HEAD_DIM


LOG2E = 1.4426950408889634


def _exp2_probs(s, m):
    return jnp.exp2(s - m).astype(BF16)


def _value_and_sum(p, v):
    ov = jnp.dot(p, jnp.concatenate([v, jnp.ones_like(v)], axis=1), preferred_element_type=F32)
    return ov[:, :LANES], ov[:, LANES:LANES + 1]


def _rms_rows(x, g):
    ms = jnp.mean(x * x, axis=-1, keepdims=True)
    return x * lax.rsqrt(ms + RMS_EPS) * g


def _norm_proj_kernel(x_ref, g_ref, w_ref, cg_ref, *refs, norm_chunks):
    n_narrow = (len(refs) - 1) // 2
    o_ref = refs[n_narrow]
    for src, dst in zip(refs[:n_narrow], refs[n_narrow + 1:]):
        dst[...] = src[...].astype(dst.dtype)
    h = _rms_rows(x_ref[...], g_ref[...]).astype(BF16)
    n_out = o_ref.shape[-1]
    lo = _lo_mask((1, LANES))
    for j in range(n_out // MXU_N):
        y = jnp.dot(h, w_ref[:, j * MXU_N:(j + 1) * MXU_N], preferred_element_type=F32)
        for half in range(MXU_N // LANES):
            c = j * (MXU_N // LANES) + half
            yc = y[:, half * LANES:(half + 1) * LANES]
            if norm_chunks[c]:
                sq = yc * yc
                s_lo = jnp.sum(jnp.where(lo, sq, 0.0), axis=-1, keepdims=True)
                s_hi = jnp.sum(jnp.where(lo, 0.0, sq), axis=-1, keepdims=True)
                inv = jnp.where(lo, lax.rsqrt(s_lo * (1.0 / HEAD_DIM) + RMS_EPS),
                                lax.rsqrt(s_hi * (1.0 / HEAD_DIM) + RMS_EPS))
                yc = yc * inv * cg_ref[:, c * LANES:(c + 1) * LANES]
            o_ref[:, c * LANES:(c + 1) * LANES] = yc.astype(o_ref.dtype)


def _norm_proj(x, g, w, col_gain, norm_chunks, narrow=(), tm=512):
    t, d = x.shape
    n = w.shape[1]
    steps = t // tm
    assert all(a.shape[0] % (16 * steps) == 0 for a in narrow)
    slab = lambda a: pl.BlockSpec((a.shape[0] // steps, a.shape[1]), lambda i: (i, 0))
    return pl.pallas_call(
        functools.partial(_norm_proj_kernel, norm_chunks=norm_chunks),
        grid=(steps,),
        in_specs=[
            pl.BlockSpec((tm, d), lambda i: (i, 0)),
            pl.BlockSpec((1, d), lambda i: (0, 0)),
            pl.BlockSpec((d, n), lambda i: (0, 0)),
            pl.BlockSpec((1, n), lambda i: (0, 0)),
        ] + [slab(a) for a in narrow],
        out_specs=[pl.BlockSpec((tm, n), lambda i: (i, 0))] + [slab(a) for a in narrow],
        out_shape=[jax.ShapeDtypeStruct((t, n), BF16)]
        + [jax.ShapeDtypeStruct(a.shape, BF16) for a in narrow],
        compiler_params=pltpu.CompilerParams(
            dimension_semantics=("arbitrary",), vmem_limit_bytes=VMEM_LIMIT),
        name="norm_proj",
    )(x, g.reshape(1, d), w, col_gain.reshape(1, n), *narrow)


def _window_kernel(par_ref, q_ref, k_ref, v_ref, o_ref, bias_scr, *, seq, blk, radius):
    span = blk + 2 * radius
    c = pl.program_id(0)
    group = ATTN_UNROLL
    row_lo = lax.broadcasted_iota(jnp.int32, (2 * blk, 1), 0) < blk
    slope_col = jnp.where(row_lo, par_ref[0, 2 * c], par_ref[0, 2 * c + 1])
    sink_col = jnp.where(row_lo, par_ref[1, 2 * c], par_ref[1, 2 * c + 1])

    iq = lax.broadcasted_iota(jnp.int32, (2 * blk, span), 0) % blk
    ik = lax.broadcasted_iota(jnp.int32, (2 * blk, span), 1)

    @pl.when(pl.program_id(1) == 0)
    def _():
        _fill_band_bias(bias_scr, iq, ik, radius, 1, slope_col)

    lo_q = _lo_mask((1, LANES))
    lo = _lo_mask((blk, LANES))

    def body(g, carry):
        starts, variants, values, scores = [], [], [], []
        for u in range(group):
            a = pl.multiple_of((g * group + u) * blk, blk)
            ks = pl.multiple_of(jnp.clip(a - radius, 0, seq - span), radius)
            q = q_ref[pl.ds(a, blk), :]
            zero = jnp.zeros_like(q)
            qm = jnp.concatenate([jnp.where(lo_q, q, zero), jnp.where(lo_q, zero, q)], axis=0)
            k = k_ref[pl.ds(ks, span), :]
            scores.append(lax.dot_general(qm, k, (((1,), (1,)), ((), ())),
                                          preferred_element_type=F32))
            starts.append(a)
            variants.append((a - ks) // radius)
            values.append(v_ref[pl.ds(ks, span), :])
        probs, sink_terms = [], []
        for u in range(group):
            s = scores[u] + bias_scr[variants[u]]
            m = jnp.maximum(jnp.max(s, axis=-1, keepdims=True), sink_col)
            probs.append(_exp2_probs(s, m))
            sink_terms.append(jnp.exp2(sink_col - m))
        for u in range(group):
            o, l = _value_and_sum(probs[u], values[u])
            o = o * (1.0 / (l + sink_terms[u]))
            o_ref[pl.ds(starts[u], blk), :] = jnp.where(lo, o[:blk], o[blk:]).astype(o_ref.dtype)
        return carry

    lax.fori_loop(0, seq // (blk * group), body, 0)


def _window_attention(qkv, params, *, q_blk0, k_blk, v_blk, n_pairs, radius, blk):
    b, s, _ = qkv.shape
    span = blk + 2 * radius
    return pl.pallas_call(
        functools.partial(_window_kernel, seq=s, blk=blk, radius=radius),
        grid=(n_pairs, b),
        in_specs=[
            pl.BlockSpec(memory_space=pltpu.SMEM),
            pl.BlockSpec((None, s, LANES), lambda c, bi: (bi, 0, q_blk0 + c)),
            pl.BlockSpec((None, s, LANES), lambda c, bi: (bi, 0, k_blk)),
            pl.BlockSpec((None, s, LANES), lambda c, bi: (bi, 0, v_blk)),
        ],
        out_specs=pl.BlockSpec((None, s, LANES), lambda c, bi: (bi, 0, c)),
        out_shape=jax.ShapeDtypeStruct((b, s, n_pairs * LANES), BF16),
        scratch_shapes=[pltpu.VMEM((3, 2 * blk, span), F32)],
        compiler_params=pltpu.CompilerParams(
            dimension_semantics=("arbitrary", "arbitrary"), vmem_limit_bytes=VMEM_LIMIT),
        name="window_attn",
    )(params, qkv, qkv, qkv)


PERM_CHUNK = 256
PHASES4 = 4
PIECE = PERM_CHUNK // PHASES4


def _stack_heads(q):
    lo = _lo_mask((1, LANES))
    zero = jnp.zeros_like(q)
    return jnp.concatenate([jnp.where(lo, q, zero), jnp.where(lo, zero, q)], axis=0)


def _merge_heads(x, blk):
    x = jnp.broadcast_to(x, (2 * blk, LANES))
    return jnp.where(_lo_mask((blk, LANES)), x[:blk], x[blk:])


def _attend_group(qs, ks, vs, biases):
    scores = [lax.dot_general(_stack_heads(q), k, (((1,), (1,)), ((), ())),
                              preferred_element_type=F32) for q, k in zip(qs, ks)]
    stats = []
    for s, bias in zip(scores, biases):
        s = s + bias
        m = jnp.max(s, axis=-1, keepdims=True)
        stats.append((_exp2_probs(s, m), m))
    outs = []
    for (p, m), v in zip(stats, vs):
        o, l = _value_and_sum(p, v)
        outs.append((o, m, l))
    return outs


def _fill_band_bias(bias_scr, pos_q, pos_k, radius, pos_scale, slope_col):
    for var in range(3):
        dist = jnp.abs(pos_q + var * radius - pos_k)
        dist_f = (dist * pos_scale).astype(F32)
        bias_scr[var] = jnp.where(dist <= radius, -(slope_col * dist_f) * LOG2E, NEG_INF)


def _dilated_kernel(par_ref, q_ref, k_ref, v_ref, o_ref, qp, kp, vp, acc_p, m_p, l_p,
                    bias1, bias4, bias16, *, seq, blk, radius):
    span = blk + 2 * radius
    n_chunk = seq // PERM_CHUNK
    group = ATTN_UNROLL
    c = pl.program_id(0)
    row_lo = lax.broadcasted_iota(jnp.int32, (2 * blk, 1), 0) < blk
    slope_col = jnp.where(row_lo, par_ref[0, 2 * c], par_ref[0, 2 * c + 1])

    def piece_pos(x):
        return PIECE * (x // PIECE) + PHASES4 * (x % 16) + (x % PIECE) // 16

    iq = lax.broadcasted_iota(jnp.int32, (2 * blk, span), 0) % blk
    ik = lax.broadcasted_iota(jnp.int32, (2 * blk, span), 1)

    @pl.when(pl.program_id(1) == 0)
    def _():
        _fill_band_bias(bias1, iq, ik, radius, 1, slope_col)
        _fill_band_bias(bias4, piece_pos(iq), piece_pos(ik), radius, 4, slope_col)
        _fill_band_bias(bias16, iq, ik, radius, 16, slope_col)

    rr = lax.broadcasted_iota(jnp.int32, (PERM_CHUNK, PERM_CHUNK), 0)
    cc = lax.broadcasted_iota(jnp.int32, (PERM_CHUNK, PERM_CHUNK), 1)
    to_phase = jnp.where(cc == 16 * (rr % 16) + PHASES4 * ((rr // 16) % 4) + rr // PIECE,
                         1.0, 0.0).astype(BF16)
    to_token = jnp.where(cc == PIECE * (rr % PHASES4) + 16 * ((rr // PHASES4) % 4) + rr // 16,
                         1.0, 0.0).astype(BF16)

    def permute_body(ch2, carry):
        for half in range(2):
            ch = 2 * ch2 + half
            r0 = pl.multiple_of(ch * PERM_CHUNK, PERM_CHUNK)
            x = jnp.concatenate([q_ref[pl.ds(r0, PERM_CHUNK), :],
                                 k_ref[pl.ds(r0, PERM_CHUNK), :],
                                 v_ref[pl.ds(r0, PERM_CHUNK), :]], axis=1)
            y = jnp.dot(to_phase, x, preferred_element_type=F32).astype(BF16)
            for p4 in range(PHASES4):
                rows = slice(PIECE * p4, PIECE * (p4 + 1))
                qp[p4, ch] = y[rows, 0:LANES]
                kp[p4, ch] = y[rows, LANES:2 * LANES]
                vp[p4, ch] = y[rows, 2 * LANES:3 * LANES]
        return carry

    lax.fori_loop(0, n_chunk // 2, permute_body, 0)

    pieces_per_blk = blk // PIECE
    n_blk4 = seq // (PHASES4 * blk)

    def dil4_body(g, carry):
        p4 = g // (n_blk4 // group)
        j0 = (g % (n_blk4 // group)) * group
        qs, ks, vs, bs = [], [], [], []
        for u in range(group):
            pc = pieces_per_blk * (j0 + u)
            cs = jnp.clip(pc - radius // PIECE, 0, n_chunk - span // PIECE)
            qs.append(qp[p4, pl.ds(pc, pieces_per_blk)].reshape(blk, LANES))
            ks.append(kp[p4, pl.ds(cs, span // PIECE)].reshape(span, LANES))
            vs.append(vp[p4, pl.ds(cs, span // PIECE)].reshape(span, LANES))
            bs.append(bias4[(pc - cs) // (radius // PIECE)])
        for u, (o, m, l) in enumerate(_attend_group(qs, ks, vs, bs)):
            dst = (p4, pl.ds(pieces_per_blk * (j0 + u), pieces_per_blk))
            acc_p[dst] = _merge_heads(o, blk).reshape(pieces_per_blk, PIECE, LANES)
            m_p[dst] = _merge_heads(m, blk).reshape(pieces_per_blk, PIECE, LANES)
            l_p[dst] = _merge_heads(l, blk).reshape(pieces_per_blk, PIECE, LANES)
        return carry

    lax.fori_loop(0, PHASES4 * n_blk4 // group, dil4_body, 0)

    chunks_per_blk = blk // 16

    def dil16_body(p4, carry):
        qs, ks, vs, bs, dsts = [], [], [], [], []
        for mm in range(PIECE // 16):
            off = mm * 16
            kk = kp[p4, :, pl.ds(off, 16), :].reshape(span, LANES)
            vv = vp[p4, :, pl.ds(off, 16), :].reshape(span, LANES)
            for jb in range(n_chunk // chunks_per_blk):
                dst = (p4, pl.ds(chunks_per_blk * jb, chunks_per_blk), pl.ds(off, 16), slice(None))
                qs.append(qp[dst].reshape(blk, LANES))
                ks.append(kk)
                vs.append(vv)
                bs.append(bias16[jb * (blk // radius)])
                dsts.append(dst)
        for dst, (o, m, l) in zip(dsts, _attend_group(qs, ks, vs, bs)):
            m_old = m_p[dst].reshape(blk, LANES)
            m_new = jnp.maximum(m_old, _merge_heads(m, blk))
            w_old = jnp.exp2(m_old - m_new)
            w_new = jnp.exp2(_merge_heads(m, blk) - m_new)
            acc = w_old * acc_p[dst].reshape(blk, LANES) + w_new * _merge_heads(o, blk)
            den = w_old * l_p[dst].reshape(blk, LANES) + w_new * _merge_heads(l, blk)
            acc_p[dst] = acc.reshape(chunks_per_blk, 16, LANES)
            l_p[dst] = den.reshape(chunks_per_blk, 16, LANES)
            m_p[dst] = m_new.reshape(chunks_per_blk, 16, LANES)
        return carry

    lax.fori_loop(0, PHASES4, dil16_body, 0)

    blk_per_chunk = PERM_CHUNK // blk

    def dil1_body(g, carry):
        merged = []
        for cc2 in range(group // blk_per_chunk):
            ch = g * (group // blk_per_chunk) + cc2
            acc = jnp.concatenate([acc_p[p4, ch] for p4 in range(PHASES4)], axis=0)
            den = jnp.concatenate([l_p[p4, ch] for p4 in range(PHASES4)], axis=0)
            mx = jnp.concatenate([m_p[p4, ch] for p4 in range(PHASES4)], axis=0)
            o_ph = (acc * (1.0 / den)).astype(BF16)
            lse = mx + jnp.log2(den)
            hi = lse.astype(BF16)
            rem = lse - hi.astype(F32)
            mid = rem.astype(BF16)
            low = (rem - mid.astype(F32)).astype(BF16)
            back = jnp.dot(to_token, jnp.concatenate([o_ph, hi, mid, low], axis=1),
                           preferred_element_type=F32)
            merged.append((back[:, 0:LANES], back[:, LANES:2 * LANES]
                           + back[:, 2 * LANES:3 * LANES] + back[:, 3 * LANES:4 * LANES]))
        qs, ks, vs, bs, starts = [], [], [], [], []
        for u in range(group):
            a = pl.multiple_of((g * group + u) * blk, blk)
            kst = pl.multiple_of(jnp.clip(a - radius, 0, seq - span), radius)
            qs.append(q_ref[pl.ds(a, blk), :])
            ks.append(k_ref[pl.ds(kst, span), :])
            vs.append(v_ref[pl.ds(kst, span), :])
            bs.append(bias1[(a - kst) // radius])
            starts.append(a)
        for u, (o, m, l) in enumerate(_attend_group(qs, ks, vs, bs)):
            o1 = _merge_heads(o * (1.0 / l), blk)
            lse1 = _merge_heads(m + jnp.log2(l), blk)
            o2, lse2 = merged[u // blk_per_chunk]
            half = slice((u % blk_per_chunk) * blk, (u % blk_per_chunk + 1) * blk)
            o2, lse2 = o2[half], lse2[half]
            top = jnp.maximum(lse1, lse2)
            e1 = jnp.exp2(lse1 - top)
            e2 = jnp.exp2(lse2 - top)
            o_ref[pl.ds(starts[u], blk), :] = ((e1 * o1 + e2 * o2)
                                               * (1.0 / (e1 + e2))).astype(o_ref.dtype)
        return carry

    lax.fori_loop(0, seq // (blk * group), dil1_body, 0)


def _dilated_attention(qkv, params, *, k_blk0, v_blk0, n_pairs, blk, radius):
    b, s, _ = qkv.shape
    span = blk + 2 * radius
    n_chunk = s // PERM_CHUNK
    assert s // 16 == span and s % PERM_CHUNK == 0 and blk % PIECE == 0
    phase = lambda dt: pltpu.VMEM((PHASES4, n_chunk, PIECE, LANES), dt)
    return pl.pallas_call(
        functools.partial(_dilated_kernel, seq=s, blk=blk, radius=radius),
        grid=(n_pairs, b),
        in_specs=[
            pl.BlockSpec(memory_space=pltpu.SMEM),
            pl.BlockSpec((None, s, LANES), lambda c, bi: (bi, 0, c)),
            pl.BlockSpec((None, s, LANES), lambda c, bi: (bi, 0, k_blk0 + c)),
            pl.BlockSpec((None, s, LANES), lambda c, bi: (bi, 0, v_blk0 + c)),
        ],
        out_specs=pl.BlockSpec((None, s, LANES), lambda c, bi: (bi, 0, c)),
        out_shape=jax.ShapeDtypeStruct((b, s, n_pairs * LANES), BF16),
        scratch_shapes=[phase(BF16), phase(BF16), phase(BF16), phase(F32), phase(F32), phase(F32)]
        + [pltpu.VMEM((3, 2 * blk, span), F32)] * 3,
        compiler_params=pltpu.CompilerParams(
            dimension_semantics=("arbitrary", "arbitrary"), vmem_limit_bytes=VMEM_LIMIT),
        name="dilated_attn",
    )(params, qkv, qkv, qkv)


def _na_kernel(q_ref, k_ref, v_ref, bias_ref, o_ref, *, rows):
    kh = NA_ROWS
    group = ATTN_UNROLL
    lo_q = _lo_mask((1, LANES))
    lo = _lo_mask((GRID_W, LANES))

    def body(g, carry):
        starts, offs, values, scores = [], [], [], []
        for u in range(group):
            i = g * group + u
            rs = jnp.clip(i - kh // 2, 0, rows - kh)
            qs = pl.multiple_of(i * GRID_W, GRID_W)
            ks = pl.multiple_of(rs * GRID_W, GRID_W)
            q = q_ref[pl.ds(qs, GRID_W), :]
            zero = jnp.zeros_like(q)
            qm = jnp.concatenate([jnp.where(lo_q, q, zero), jnp.where(lo_q, zero, q)], axis=0)
            k = k_ref[pl.ds(ks, kh * GRID_W), :]
            scores.append(lax.dot_general(qm, k, (((1,), (1,)), ((), ())),
                                          preferred_element_type=F32))
            starts.append(qs)
            offs.append(rs - i + (NA_ROWS - 1))
            values.append(v_ref[pl.ds(ks, kh * GRID_W), :])
        probs = []
        for u in range(group):
            s = scores[u] + jnp.concatenate(
                [bias_ref[offs[u] + 2 * j] for j in range(kh // 2)], axis=1)
            probs.append(_exp2_probs(s, jnp.max(s, axis=-1, keepdims=True)))
        for u in range(group):
            o, l = _value_and_sum(probs[u], values[u])
            o = o * (1.0 / l)
            o_ref[pl.ds(starts[u], GRID_W), :] = jnp.where(
                lo, o[:GRID_W], o[GRID_W:]).astype(o_ref.dtype)
        return carry

    lax.fori_loop(0, rows // group, body, 0)


def _na_bias_table(rpb):
    qc = np.arange(GRID_W)[:, None]
    kc = np.arange(GRID_W)[None, :]
    qstart = np.clip(qc - NA_COLS // 2, 0, GRID_W - NA_COLS)
    valid = (kc >= qstart) & (kc < qstart + NA_COLS)
    coff = np.clip(kc - qc, -(NA_COLS - 1), NA_COLS - 1) + NA_COLS - 1
    n_rel = 2 * NA_COLS - 1
    n_off = 2 * NA_ROWS - 2
    pick = (coff[:, :, None] == np.arange(n_rel)).astype(np.float32)
    pick2 = np.zeros((GRID_W, 2, GRID_W, 2, n_rel), np.float32)
    pick2[:, 0, :, 0] = pick
    pick2[:, 1, :, 1] = pick
    pick2 = pick2.reshape(GRID_W, 2 * GRID_W, 2 * n_rel)
    r = rpb.astype(F32) * LOG2E
    r = jnp.concatenate([r[:, :-1], r[:, 1:]], axis=-1)
    r = r.reshape(C_HEADS // 2, 2, n_off, 2 * n_rel)
    t = jnp.einsum("phdm,qnm->pdhqn", r, jnp.asarray(pick2), precision=lax.Precision.HIGHEST)
    t = jnp.where(jnp.asarray(np.tile(valid, (1, 2))), t, NEG_INF)
    return t.reshape(C_HEADS // 2, n_off, 2 * GRID_W, 2 * GRID_W)


def _na_attention(qkv, bias):
    b, s, n_cols = qkv.shape
    n_pairs = C_HEADS // 2
    rows = s // GRID_W
    return pl.pallas_call(
        functools.partial(_na_kernel, rows=rows),
        grid=(n_pairs, b),
        in_specs=[
            pl.BlockSpec((None, s, LANES), lambda c, bi: (bi, 0, c)),
            pl.BlockSpec((None, s, LANES), lambda c, bi: (bi, 0, n_pairs + c)),
            pl.BlockSpec((None, s, LANES), lambda c, bi: (bi, 0, 2 * n_pairs + c)),
            pl.BlockSpec((None,) + bias.shape[1:], lambda c, bi: (c, 0, 0, 0)),
        ],
        out_specs=pl.BlockSpec((None, s, LANES), lambda c, bi: (bi, 0, c)),
        out_shape=jax.ShapeDtypeStruct((b, s, n_pairs * LANES), BF16),
        compiler_params=pltpu.CompilerParams(
            dimension_semantics=("arbitrary", "arbitrary"), vmem_limit_bytes=VMEM_LIMIT),
        name="na_attn",
    )(qkv, qkv, qkv, bias)


def _silu(g):
    return g * (1.0 / (1.0 + jnp.exp(-g)))


def _ffn_kernel(x_ref, oa_ref, ob_ref, wa_ref, wb_ref, g_ref, wg_ref, wu_ref, wd_ref,
                c0_ref, c1_ref, c2_ref, y_ref, n0_ref, n1_ref, n2_ref, a_scr):
    n0_ref[...] = c0_ref[...].astype(n0_ref.dtype)
    n1_ref[...] = c1_ref[...].astype(n1_ref.dtype)
    n2_ref[...] = c2_ref[...].astype(n2_ref.dtype)
    x = (x_ref[...] + jnp.dot(oa_ref[...], wa_ref[...], preferred_element_type=F32)
         + jnp.dot(ob_ref[...], wb_ref[...], preferred_element_type=F32))
    h = _rms_rows(x, g_ref[...]).astype(BF16)
    f = wg_ref.shape[1]
    for j in range(f // MXU_N):
        sl = slice(j * MXU_N, (j + 1) * MXU_N)
        gate = jnp.dot(h, wg_ref[:, sl], preferred_element_type=F32)
        up = jnp.dot(h, wu_ref[:, sl], preferred_element_type=F32)
        a_scr[:, sl] = (_silu(gate) * up).astype(BF16)
    y_ref[...] = x + jnp.dot(a_scr[...], wd_ref[...], preferred_element_type=F32)


def _ffn(x, oa, ob, wa, wb, g, wg, wu, wd, narrow, tm=256):
    t, d = x.shape
    f = wg.shape[1]
    steps = t // tm
    const = lambda a: pl.BlockSpec(a.shape, lambda i: (0, 0), pipeline_mode=pl.Buffered(1))
    row = lambda w: pl.BlockSpec((tm, w), lambda i: (i, 0))
    slab = lambda a: pl.BlockSpec((a.shape[0] // steps, a.shape[1]), lambda i: (i, 0))
    assert all(a.shape[0] % (16 * steps) == 0 for a in narrow)
    return pl.pallas_call(
        _ffn_kernel,
        grid=(steps,),
        in_specs=[row(d), row(oa.shape[1]), row(ob.shape[1]), const(wa), const(wb),
                  pl.BlockSpec((1, d), lambda i: (0, 0)),
                  const(wg), const(wu), const(wd)] + [slab(a) for a in narrow],
        out_specs=[pl.BlockSpec((tm, d), lambda i: (i, 0))] + [slab(a) for a in narrow],
        out_shape=[jax.ShapeDtypeStruct((t, d), F32)]
        + [jax.ShapeDtypeStruct(a.shape, BF16) for a in narrow],
        scratch_shapes=[pltpu.VMEM((tm, f), BF16)],
        compiler_params=pltpu.CompilerParams(
            dimension_semantics=("arbitrary",), vmem_limit_bytes=VMEM_LIMIT),
        name="dense_swiglu",
    )(x, oa, ob, wa, wb, g.reshape(1, d), wg, wu, wd, *narrow)


TOP_K = 2
MOE_TB = 512
MOE_TM = 512
MOE_RC = 256
KEY_STRIDE = 65536.0


def _router_kernel(x_ref, o_ref, wo_ref, g_ref, wr_hi_ref, wr_lo_ref, x1_ref, h_ref, info_ref,
                   info_t_ref, cnt_ref, carry_scr, ltri_scr):
    i = pl.program_id(0)
    tb = x_ref.shape[0]
    parts = 2
    hb = tb // parts
    rows = [slice(p * hb, (p + 1) * hb) for p in range(parts)]
    lane = lax.broadcasted_iota(jnp.int32, (1, LANES), 1)

    @pl.when(i == 0)
    def _():
        carry_scr[...] = jnp.zeros_like(carry_scr)
        r = lax.broadcasted_iota(jnp.int32, (hb, hb), 0)
        cidx = lax.broadcasted_iota(jnp.int32, (hb, hb), 1)
        ltri_scr[...] = jnp.where(cidx < r, 1.0, 0.0).astype(BF16)

    x1s = [x_ref[rw, :] + jnp.dot(o_ref[rw, :], wo_ref[...], preferred_element_type=F32)
           for rw in rows]
    hs = [_rms_rows(x1, g_ref[...]) for x1 in x1s]
    his = [h.astype(BF16) for h in hs]
    los = [(h - hi.astype(F32)).astype(BF16) for h, hi in zip(hs, his)]
    logits_all = [jnp.dot(hi, wr_hi_ref[...], preferred_element_type=F32)
                  + jnp.dot(lo_, wr_hi_ref[...], preferred_element_type=F32)
                  + jnp.dot(hi, wr_lo_ref[...], preferred_element_type=F32)
                  for hi, lo_ in zip(his, los)]
    for rw, x1, hi in zip(rows, x1s, his):
        x1_ref[rw, :] = x1
        h_ref[rw, :] = hi

    picks = []
    for logits in logits_all:
        logits = jnp.where(lane < N_EXPERTS, logits, NEG_INF)
        v1 = jnp.max(logits, axis=-1, keepdims=True)
        i1 = jnp.min(jnp.where(logits == v1, lane, LANES), axis=-1, keepdims=True)
        rest = jnp.where(lane == i1, NEG_INF, logits)
        v2 = jnp.max(rest, axis=-1, keepdims=True)
        i2 = jnp.min(jnp.where(rest == v2, lane, LANES), axis=-1, keepdims=True)
        e2 = jnp.exp(v2 - v1)
        g1 = 1.0 / (1.0 + e2)
        picks.append((i1, i2, g1, e2 * g1))

    pickeds = [jnp.where(lane == i1, 1.0, jnp.where(lane == i2, 1.0, 0.0))
               for i1, i2, _, _ in picks]
    inside = [jnp.dot(ltri_scr[...], pk.astype(BF16), preferred_element_type=F32)
              for pk in pickeds]
    counts = [jnp.sum(pk, axis=0, keepdims=True) for pk in pickeds]
    seen = carry_scr[...]
    for p, (i1, i2, g1, g2) in enumerate(picks):
        before = inside[p] + seen
        rank1 = jnp.sum(jnp.where(lane == i1, before, 0.0), axis=-1, keepdims=True)
        rank2 = jnp.sum(jnp.where(lane == i2, before, 0.0), axis=-1, keepdims=True)
        key1 = i1.astype(F32) * KEY_STRIDE + rank1
        key2 = i2.astype(F32) * KEY_STRIDE + rank2
        info = jnp.where(lane == 0, key1, jnp.where(lane == 1, key2,
                         jnp.where(lane == 2, g1, jnp.where(lane == 3, g2, 0.0))))
        info_ref[rows[p], :] = info
        info_t_ref[:, rows[p]] = info.T[:8, :]
        seen = seen + counts[p]
    cnt_ref[...] = jnp.broadcast_to(seen - carry_scr[...], cnt_ref.shape)
    carry_scr[...] = seen


def _router(x, o, wo, g, wr):
    t, d = x.shape
    nb = t // MOE_TB
    wr_hi = wr.astype(BF16)
    wr_lo = (wr - wr_hi.astype(F32)).astype(BF16)
    row = lambda w: pl.BlockSpec((MOE_TB, w), lambda i: (i, 0))
    const = lambda a: pl.BlockSpec(a.shape, lambda i: (0, 0))
    return pl.pallas_call(
        _router_kernel,
        grid=(nb,),
        in_specs=[row(d), row(o.shape[1]), const(wo), pl.BlockSpec((1, d), lambda i: (0, 0)),
                  const(wr_hi), const(wr_lo)],
        out_specs=[row(d), row(d), row(LANES),
                   pl.BlockSpec((None, 8, MOE_TB), lambda i: (i, 0, 0)),
                   pl.BlockSpec((None, 8, LANES), lambda i: (i, 0, 0))],
        out_shape=[jax.ShapeDtypeStruct((t, d), F32),
                   jax.ShapeDtypeStruct((t, d), BF16),
                   jax.ShapeDtypeStruct((t, LANES), F32),
                   jax.ShapeDtypeStruct((nb, 8, MOE_TB), F32),
                   jax.ShapeDtypeStruct((nb, 8, LANES), F32)],
        scratch_shapes=[pltpu.VMEM((1, LANES), F32),
                        pltpu.VMEM((MOE_TB // 2, MOE_TB // 2), BF16)],
        compiler_params=pltpu.CompilerParams(
            dimension_semantics=("arbitrary",), vmem_limit_bytes=VMEM_LIMIT),
        name="moe_router",
    )(x, o, wo, g.reshape(1, d), wr_hi, wr_lo)


def _moe_plan(block_counts, n_tokens):
    nb = block_counts.shape[0]
    n_tiles = (TOP_K * n_tokens + N_EXPERTS * (MOE_TM - 1)) // MOE_TM
    n_chunks = n_tiles * MOE_TM // MOE_RC
    tc = block_counts.astype(jnp.int32)
    cb = jnp.concatenate([jnp.zeros((1, N_EXPERTS), jnp.int32), jnp.cumsum(tc, axis=0)])
    counts = cb[-1]
    padded = ((counts + MOE_TM - 1) // MOE_TM) * MOE_TM
    ends = jnp.cumsum(padded)
    starts = ends - padded
    total = ends[-1]

    tiles = jnp.arange(n_tiles, dtype=jnp.int32)
    n_valid_tiles = total // MOE_TM
    t_valid = (tiles < n_valid_tiles).astype(jnp.int32)
    t_src = jnp.maximum(jnp.minimum(tiles, n_valid_tiles - 1), 0)
    t_exp = jnp.minimum(jnp.sum((t_src * MOE_TM)[:, None] >= ends[None, :], axis=1),
                        N_EXPERTS - 1).astype(jnp.int32)

    r0 = jnp.arange(n_chunks, dtype=jnp.int32) * MOE_RC
    ex = jnp.minimum(jnp.sum(r0[:, None] >= ends[None, :], axis=1), N_EXPERTS - 1)
    live = r0 < total
    lo = r0 - starts[ex]
    base = (ex * int(KEY_STRIDE) + lo).astype(jnp.int32)
    cbe = cb[:, ex]
    g_lo = jnp.sum(cbe[1:] <= lo[None, :], axis=0).astype(jnp.int32)
    g_cnt = jnp.sum(live[None, :] & (cbe[:-1] < (lo + MOE_RC)[None, :]) & (cbe[1:] > lo[None, :]),
                    axis=0).astype(jnp.int32)

    row_lo = starts[None, :] + cb[:-1]
    row_hi = starts[None, :] + cb[1:]
    c_first = row_lo // MOE_RC
    c_num = jnp.where(row_hi > row_lo, (row_hi - 1) // MOE_RC - c_first + 1, 0)
    c_end = jnp.cumsum(c_num, axis=1)
    slots = jnp.arange(COMBINE_MAX_CHUNKS, dtype=jnp.int32)
    e_of = jnp.minimum(jnp.sum(slots[None, :, None] >= c_end[:, None, :], axis=2),
                       N_EXPERTS - 1)
    owner = e_of[:, :, None] == jnp.arange(N_EXPERTS)[None, None, :]
    c_c = slots[None, :] + jnp.sum(
        jnp.where(owner, (c_first - (c_end - c_num))[:, None, :], 0), axis=2)
    c_cnt = c_end[:, -1].astype(jnp.int32)
    c_c = jnp.clip(c_c, 0, n_chunks - 1).astype(jnp.int32).reshape(-1)
    return dict(n_tiles=n_tiles, n_chunks=n_chunks, t_valid=t_valid, t_exp=t_exp,
                base=base, live=live.astype(jnp.int32), g_lo=g_lo, g_cnt=g_cnt,
                c_cnt=c_cnt, c_c=c_c)


def _gather_kernel(live_ref, lo_ref, cnt_ref, base_ref, info_t_ref, h_ref,
                   xs_ref, gs_ref, acc_scr, gsum_scr):
    tb = info_t_ref.shape[-1]
    per_tile = xs_ref.shape[0] // MOE_RC
    row = lax.broadcasted_iota(jnp.int32, (MOE_RC, tb), 0).astype(F32)

    def chunk_body(k, carry):
        c = pl.program_id(0) * per_tile + k
        dst = pl.ds(pl.multiple_of(k * MOE_RC, MOE_RC), MOE_RC)

        @pl.when(live_ref[c] == 1)
        def _():
            first = lo_ref[c]
            n = cnt_ref[c]
            acc_scr[...] = jnp.zeros_like(acc_scr)
            gsum_scr[...] = jnp.zeros_like(gsum_scr)

            def two_blocks(it, inner):
                rows, gate = None, None
                for w in range(2):
                    j = 2 * it + w
                    b = first + jnp.minimum(j, n - 1)
                    base = jnp.where(j < n, base_ref[c], NO_MATCH_BASE).astype(F32)
                    info = info_t_ref[b]
                    m1 = (info[0:1, :] - base) == row
                    m2 = (info[1:2, :] - base) == row
                    sel = jnp.where(m1, 1.0, jnp.where(m2, 1.0, 0.0)).astype(BF16)
                    g = jnp.sum(jnp.where(m1, info[2:3, :], jnp.where(m2, info[3:4, :], 0.0)),
                                axis=-1, keepdims=True)
                    r = jnp.dot(sel, h_ref[pl.ds(pl.multiple_of(b * tb, tb), tb), :],
                                preferred_element_type=F32)
                    rows = r if rows is None else rows + r
                    gate = g if gate is None else gate + g
                acc_scr[...] += rows
                gsum_scr[...] += gate
                return inner

            lax.fori_loop(0, (n + 1) // 2, two_blocks, 0)
            xs_ref[dst, :] = acc_scr[...].astype(xs_ref.dtype)
            gs_ref[dst, :] = jnp.broadcast_to(gsum_scr[...], (MOE_RC, gs_ref.shape[1]))

        @pl.when(live_ref[c] == 0)
        def _():
            xs_ref[dst, :] = jnp.zeros((MOE_RC, xs_ref.shape[1]), xs_ref.dtype)
            gs_ref[dst, :] = jnp.zeros((MOE_RC, gs_ref.shape[1]), gs_ref.dtype)

        return carry

    lax.fori_loop(0, per_tile, chunk_body, 0)


def _moe_gather(plan, info_t, h):
    t, d = h.shape
    rows = plan["n_chunks"] * MOE_RC
    tile = lambda i, live, lo, cn, ba: (i, 0)
    grid_spec = pltpu.PrefetchScalarGridSpec(
        num_scalar_prefetch=4,
        grid=(plan["n_tiles"],),
        in_specs=[pl.BlockSpec(info_t.shape, lambda c, live, lo, cn, ba: (0, 0, 0),
                               pipeline_mode=pl.Buffered(1)),
                  pl.BlockSpec((t, d), lambda c, live, lo, cn, ba: (0, 0),
                               pipeline_mode=pl.Buffered(1))],
        out_specs=[pl.BlockSpec((MOE_TM, d), tile), pl.BlockSpec((MOE_TM, LANES), tile)],
        scratch_shapes=[pltpu.VMEM((MOE_RC, d), F32), pltpu.VMEM((MOE_RC, 1), F32)],
    )
    return pl.pallas_call(
        _gather_kernel,
        grid_spec=grid_spec,
        out_shape=[jax.ShapeDtypeStruct((rows, d), BF16),
                   jax.ShapeDtypeStruct((rows, LANES), F32)],
        compiler_params=pltpu.CompilerParams(
            dimension_semantics=("arbitrary",), vmem_limit_bytes=VMEM_LIMIT),
        name="moe_gather",
    )(plan["live"], plan["g_lo"], plan["g_cnt"], plan["base"], info_t, h)


def _experts_kernel(exp_ref, val_ref, xs_ref, gs_ref, wg_ref, wu_ref, wd_ref, ys_ref,
                    acc_scr):
    i = pl.program_id(0)
    j = pl.program_id(1)

    last = j == pl.num_programs(1) - 1

    @pl.when(j == 0)
    def _():
        acc_scr[...] = jnp.zeros_like(acc_scr)

    @pl.when(val_ref[i] == 1)
    def _():
        x = xs_ref[...]
        gate = jnp.dot(x, wg_ref[...], preferred_element_type=F32)
        up = jnp.dot(x, wu_ref[...], preferred_element_type=F32)
        a = (_silu(gate) * up).astype(BF16)
        acc_scr[...] += jnp.dot(a, wd_ref[...], preferred_element_type=F32)

    @pl.when(last)
    def _():
        ys_ref[...] = (acc_scr[...] * gs_ref[:, 0:1]).astype(ys_ref.dtype)


def _moe_experts(plan, xs, gs, wg, wu, wd, tf):
    rows, d = xs.shape
    f = wg.shape[2]
    nf = f // tf

    def wcol(i, j, ex, val):
        return (ex[i], 0, jnp.where(val[i] == 1, j, nf - 1))

    def wrow(i, j, ex, val):
        return (ex[i], jnp.where(val[i] == 1, j, nf - 1), 0)

    tile = lambda i, j, ex, val: (i, 0)
    grid_spec = pltpu.PrefetchScalarGridSpec(
        num_scalar_prefetch=2,
        grid=(plan["n_tiles"], nf),
        in_specs=[pl.BlockSpec((MOE_TM, d), tile),
                  pl.BlockSpec((MOE_TM, LANES), tile),
                  pl.BlockSpec((None, d, tf), wcol),
                  pl.BlockSpec((None, d, tf), wcol),
                  pl.BlockSpec((None, tf, d), wrow)],
        out_specs=pl.BlockSpec((MOE_TM, d), tile),
        scratch_shapes=[pltpu.VMEM((MOE_TM, d), F32)],
    )
    return pl.pallas_call(
        _experts_kernel,
        grid_spec=grid_spec,
        out_shape=jax.ShapeDtypeStruct((rows, d), BF16),
        compiler_params=pltpu.CompilerParams(
            dimension_semantics=("arbitrary", "arbitrary"), vmem_limit_bytes=VMEM_LIMIT),
        name="moe_experts",
    )(plan["t_exp"], plan["t_valid"], xs, gs, wg, wu, wd)


NO_MATCH_BASE = -(2 ** 30)


COMBINE_GROUP = 4
COMBINE_MAX_CHUNKS = N_EXPERTS * (MOE_TB // MOE_RC + 1)


def _combine_kernel(cnt_ref, pc_ref, base_ref, info_ref, x_ref, ys_hbm, o_ref, ybuf, sem):
    b = pl.program_id(0)
    tb = x_ref.shape[0]
    d = x_ref.shape[1]

    def n_fetch(blk):
        return ((cnt_ref[blk] + COMBINE_GROUP - 1) // COMBINE_GROUP) * COMBINE_GROUP

    def chunk_of(blk, j):
        return pc_ref[blk * COMBINE_MAX_CHUNKS + jnp.minimum(j, cnt_ref[blk] - 1)]

    def copy(blk, j, slot):
        row0 = pl.multiple_of(chunk_of(blk, j) * MOE_RC, MOE_RC)
        return pltpu.make_async_copy(ys_hbm.at[pl.ds(row0, MOE_RC), :], ybuf.at[slot, j],
                                     sem.at[slot])

    def start_block(blk, slot):
        def issue(j, carry):
            copy(blk, j, slot).start()
            return carry
        lax.fori_loop(0, n_fetch(blk), issue, 0)

    @pl.when(b == 0)
    def _():
        start_block(0, 0)

    @pl.when(b + 1 < pl.num_programs(0))
    def _():
        start_block(b + 1, (b + 1) % 2)

    slot = b % 2
    n = cnt_ref[b]

    def drain(j, carry):
        copy(b, j, slot).wait()
        return carry

    lax.fori_loop(0, n_fetch(b), drain, 0)

    lane = lax.broadcasted_iota(jnp.int32, (1, LANES), 1)
    info = info_ref[...]
    k1 = jnp.sum(jnp.where(lane == 0, info, 0.0), axis=-1, keepdims=True)
    k2 = jnp.sum(jnp.where(lane == 1, info, 0.0), axis=-1, keepdims=True)
    col = lax.broadcasted_iota(jnp.int32, (tb, MOE_RC), 1).astype(F32)
    o_ref[...] = x_ref[...]

    def group(gi, carry):
        sels = []
        for w in range(COMBINE_GROUP):
            j = gi * COMBINE_GROUP + w
            base = jnp.where(j < n, base_ref[chunk_of(b, j)], NO_MATCH_BASE).astype(F32)
            sels.append(jnp.where(k1 - base == col, 1.0,
                                  jnp.where(k2 - base == col, 1.0, 0.0)).astype(BF16))
        rows = ybuf[slot, pl.ds(gi * COMBINE_GROUP, COMBINE_GROUP)]
        o_ref[...] += jnp.dot(jnp.concatenate(sels, axis=1),
                              rows.reshape(COMBINE_GROUP * MOE_RC, d),
                              preferred_element_type=F32)
        return carry

    lax.fori_loop(0, n_fetch(b) // COMBINE_GROUP, group, 0)


def _moe_combine(plan, info, ys, x):
    t, d = x.shape
    block = lambda b, cn, pc, ba: (b, 0)
    grid_spec = pltpu.PrefetchScalarGridSpec(
        num_scalar_prefetch=3,
        grid=(t // MOE_TB,),
        in_specs=[pl.BlockSpec((MOE_TB, LANES), block),
                  pl.BlockSpec((MOE_TB, d), block),
                  pl.BlockSpec(memory_space=pl.ANY)],
        out_specs=pl.BlockSpec((MOE_TB, d), block),
        scratch_shapes=[pltpu.VMEM((2, COMBINE_MAX_CHUNKS, MOE_RC, d), BF16),
                        pltpu.SemaphoreType.DMA((2,))],
    )
    return pl.pallas_call(
        _combine_kernel,
        grid_spec=grid_spec,
        out_shape=jax.ShapeDtypeStruct((t, d), F32),
        compiler_params=pltpu.CompilerParams(
            dimension_semantics=("arbitrary",), vmem_limit_bytes=VMEM_LIMIT),
        name="moe_combine",
    )(plan["c_cnt"], plan["c_c"], plan["base"], info, x, ys)


def _moe(x, o, wo, g, wr, wg, wu, wd, tf=1792):
    t, _ = x.shape
    x1, h, info, info_t, block_counts = _router(x, o, wo, g, wr)
    plan = _moe_plan(block_counts[:, 0, :N_EXPERTS], t)
    xs, gs = _moe_gather(plan, info_t, h)
    ys = _moe_experts(plan, xs, gs, wg, wu, wd, tf)
    return _moe_combine(plan, info, ys, x1)


def _alibi_slopes():
    s = np.exp2(-8.0 * np.arange(1, N_HEADS + 1) / N_HEADS).astype(np.float32)
    return s[0::2][:A_HEADS], s[1::2][:B_Q_HEADS]


_B_HEAD_ORDER = np.array([0, 4, 1, 5, 2, 6, 3, 7])


def _even_layer(x2d, batch, norm1, w_in, qn_a, kn_a, qn_b, kn_b, sink_b, w_out,
                norm2, w_gate, w_up, w_down, narrow, narrow_first):
    t, d = x2d.shape
    s = t // batch
    wa = A_HEADS * HEAD_DIM
    wq_b = B_Q_HEADS * HEAD_DIM
    wkv_b = B_KV_HEADS * HEAD_DIM
    qb0 = 3 * wa
    q_scale = HEAD_DIM ** -0.5 * LOG2E

    w_qb = w_in[:, qb0:qb0 + wq_b].reshape(d, B_Q_HEADS, HEAD_DIM)[:, _B_HEAD_ORDER]
    w_in_p = jnp.concatenate(
        [w_in[:, :qb0], w_qb.reshape(d, wq_b), w_in[:, qb0 + wq_b:]], axis=1).astype(BF16)
    ones = jnp.ones((HEAD_DIM,), F32)
    col_gain = jnp.concatenate([
        jnp.tile(qn_a * q_scale, A_HEADS), jnp.tile(kn_a, A_HEADS), jnp.tile(ones, A_HEADS),
        jnp.tile(qn_b * q_scale, B_Q_HEADS), jnp.tile(kn_b, B_KV_HEADS),
        jnp.tile(ones, B_KV_HEADS)])
    cpl = wa // LANES
    norm_chunks = ((True,) * (2 * cpl) + (False,) * cpl + (True,) * (wq_b // LANES)
                   + (True,) * (wkv_b // LANES) + (False,) * (wkv_b // LANES))
    qkv, w_gate_b, w_up_b, *first_b = _norm_proj(x2d, norm1, w_in_p, col_gain, norm_chunks,
                                               narrow=[w_gate, w_up] + list(narrow_first))
    n_cols = qkv.shape[1]
    qkv3 = qkv.reshape(batch, s, n_cols)

    slopes_a, slopes_b = _alibi_slopes()
    par_a = jnp.asarray(slopes_a).reshape(1, A_HEADS)
    radii = {window // (2 * dil) for window, dil in DIL_CONFIGS}
    assert tuple(dil for _, dil in DIL_CONFIGS) == (1, 4, 16) and len(radii) == 1
    oa = _dilated_attention(qkv3, par_a, k_blk0=cpl, v_blk0=2 * cpl, n_pairs=cpl,
                            blk=A_BLOCK, radius=radii.pop())
    par_b = jnp.stack([jnp.asarray(slopes_b[_B_HEAD_ORDER]),
                       sink_b.astype(F32)[_B_HEAD_ORDER] * LOG2E])
    kb = 3 * cpl + wq_b // LANES
    ob = _window_attention(qkv3, par_b, q_blk0=3 * cpl, k_blk=kb, v_blk=kb + wkv_b // LANES,
                           n_pairs=wq_b // LANES, radius=B_WINDOW, blk=B_BLOCK)
    w_out_a = w_out[:wa].astype(BF16)
    w_out_b = w_out[wa:].reshape(B_Q_HEADS, HEAD_DIM, d)[_B_HEAD_ORDER].reshape(wq_b, d)
    y, *narrowed = _ffn(x2d, oa.reshape(t, wa), ob.reshape(t, wq_b), w_out_a,
                        w_out_b.astype(BF16), norm2, w_gate_b, w_up_b, w_down.astype(BF16),
                        narrow)
    return y, narrowed, first_b


def _odd_layer(x2d, batch, norm1, w_qkv, qn, kn, rpb, w_out, norm2, w_router,
               w_gate, w_up, w_down):
    t, d = x2d.shape
    s = t // batch
    wc = C_HEADS * HEAD_DIM
    ones = jnp.ones((HEAD_DIM,), F32)
    col_gain = jnp.concatenate([jnp.tile(qn * (HEAD_DIM ** -0.5 * LOG2E), C_HEADS),
                                jnp.tile(kn, C_HEADS), jnp.tile(ones, C_HEADS)])
    cpl = wc // LANES
    norm_chunks = (True,) * (2 * cpl) + (False,) * cpl
    (qkv,) = _norm_proj(x2d, norm1, w_qkv.astype(BF16), col_gain, norm_chunks)
    o = _na_attention(qkv.reshape(batch, s, 3 * wc), _na_bias_table(rpb))
    wr = jnp.zeros((d, LANES), F32).at[:, :N_EXPERTS].set(w_router.astype(F32))
    return _moe(x2d, o.reshape(t, wc), w_out.astype(BF16), norm2, wr, w_gate, w_up, w_down)


def kernel(x, ev_norm1, ev_w_in, ev_qn_a, ev_kn_a, ev_qn_b, ev_kn_b, ev_sink_b, ev_w_out, ev_norm2, ev_ffn_gate, ev_ffn_up, ev_ffn_down, od_norm1, od_w_qkv, od_qn, od_kn, od_rpb, od_w_out, od_norm2, od_router, od_exp_gate, od_exp_up, od_exp_down):
    batch, s, d = x.shape
    depth = ev_norm1.shape[0] + od_norm1.shape[0]
    h = x.reshape(batch * s, d)
    n_e, _, f_e = od_exp_gate.shape[1:]
    experts, mixer_w = None, None
    for layer in range(depth):
        j = layer // 2
        if layer % 2 == 0:
            if j < od_norm1.shape[0]:
                narrow = [od_exp_gate[j].reshape(n_e * d, f_e), od_exp_up[j].reshape(n_e * d, f_e),
                          od_exp_down[j].reshape(n_e * f_e, d)]
            else:
                narrow = [jnp.zeros((batch * s // 16, LANES), F32)] * 3
            first = [od_w_qkv[j], od_w_out[j]] if j < od_norm1.shape[0] else []
            h, narrowed, mixer_w = _even_layer(
                h, batch, ev_norm1[j], ev_w_in[j], ev_qn_a[j], ev_kn_a[j], ev_qn_b[j],
                ev_kn_b[j], ev_sink_b[j], ev_w_out[j], ev_norm2[j], ev_ffn_gate[j],
                ev_ffn_up[j], ev_ffn_down[j], narrow, first)
            experts = (narrowed[0].reshape(n_e, d, f_e), narrowed[1].reshape(n_e, d, f_e),
                       narrowed[2].reshape(n_e, f_e, d))
        else:
            w_qkv, w_out = mixer_w if mixer_w else (od_w_qkv[j], od_w_out[j])
            h = _odd_layer(h, batch, od_norm1[j], w_qkv, od_qn[j], od_kn[j], od_rpb[j],
                           w_out, od_norm2[j], od_router[j], *experts)
    return h.reshape(batch, s, d)
```

```python
import functools

import numpy as np
import jax
import jax.numpy as jnp
from jax import lax
from jax.experimental import pallas as pl
from jax.experimental.pallas import tpu as pltpu

D_MODEL = 1024
HEAD_DIM = 64
N_HEADS = D_MODEL // HEAD_DIM
A_HEADS = N_HEADS // 2
B_Q_HEADS = N_HEADS // 2
B_KV_HEADS = max(1, B_Q_HEADS // 4)
C_HEADS = N_HEADS
DIL_CONFIGS = ((128, 1), (512, 4), (2048, 16))
A_BLOCK = 128
B_WINDOW = 128
B_BLOCK = 128
GRID_W = 64
NA_ROWS = 8
NA_COLS = 16
N_EXPERTS = 8
RMS_EPS = 1e-6
NEG_INF = -1e30

LANES = 128
MXU_N = 256
VMEM_LIMIT = 56 * 1024 * 1024

ATTN_UNROLL = 8

F32 = jnp.float32
BF16 = jnp.bfloat16


def _lo_mask(shape):
    return lax.broadcasted_iota(jnp.int32, shape, len(shape) - 1) < HEAD_DIM


LOG2E = 1.4426950408889634


def _exp2_probs(s, m):
    return jnp.exp2(s - m).astype(BF16)


def _value_and_sum(p, v):
    ov = jnp.dot(p, jnp.concatenate([v, jnp.ones_like(v)], axis=1), preferred_element_type=F32)
    return ov[:, :LANES], ov[:, LANES:LANES + 1]


def _rms_rows(x, g):
    ms = jnp.mean(x * x, axis=-1, keepdims=True)
    return x * lax.rsqrt(ms + RMS_EPS) * g


def _norm_proj_kernel(x_ref, g_ref, w_ref, cg_ref, o_ref, *, norm_chunks):
    h = _rms_rows(x_ref[...], g_ref[...]).astype(BF16)
    n_out = o_ref.shape[-1]
    lo = _lo_mask((1, LANES))
    for j in range(n_out // MXU_N):
        y = jnp.dot(h, w_ref[:, j * MXU_N:(j + 1) * MXU_N], preferred_element_type=F32)
        for half in range(MXU_N // LANES):
            c = j * (MXU_N // LANES) + half
            yc = y[:, half * LANES:(half + 1) * LANES]
            if norm_chunks[c]:
                sq = yc * yc
                s_lo = jnp.sum(jnp.where(lo, sq, 0.0), axis=-1, keepdims=True)
                s_hi = jnp.sum(jnp.where(lo, 0.0, sq), axis=-1, keepdims=True)
                inv = jnp.where(lo, lax.rsqrt(s_lo * (1.0 / HEAD_DIM) + RMS_EPS),
                                lax.rsqrt(s_hi * (1.0 / HEAD_DIM) + RMS_EPS))
                yc = yc * inv * cg_ref[:, c * LANES:(c + 1) * LANES]
            o_ref[:, c * LANES:(c + 1) * LANES] = yc.astype(o_ref.dtype)


def _norm_proj(x, g, w, col_gain, norm_chunks, tm=512):
    t, d = x.shape
    n = w.shape[1]
    return pl.pallas_call(
        functools.partial(_norm_proj_kernel, norm_chunks=norm_chunks),
        grid=(t // tm,),
        in_specs=[
            pl.BlockSpec((tm, d), lambda i: (i, 0)),
            pl.BlockSpec((1, d), lambda i: (0, 0)),
            pl.BlockSpec((d, n), lambda i: (0, 0)),
            pl.BlockSpec((1, n), lambda i: (0, 0)),
        ],
        out_specs=pl.BlockSpec((tm, n), lambda i: (i, 0)),
        out_shape=jax.ShapeDtypeStruct((t, n), BF16),
        compiler_params=pltpu.CompilerParams(
            dimension_semantics=("arbitrary",), vmem_limit_bytes=VMEM_LIMIT),
        name="norm_proj",
    )(x, g.reshape(1, d), w, col_gain.reshape(1, n))


def _window_kernel(par_ref, q_ref, k_ref, v_ref, o_ref, bias_scr, *, seq, blk, radius):
    span = blk + 2 * radius
    c = pl.program_id(0)
    group = ATTN_UNROLL
    row_lo = lax.broadcasted_iota(jnp.int32, (2 * blk, 1), 0) < blk
    slope_col = jnp.where(row_lo, par_ref[0, 2 * c], par_ref[0, 2 * c + 1])
    sink_col = jnp.where(row_lo, par_ref[1, 2 * c], par_ref[1, 2 * c + 1])

    iq = lax.broadcasted_iota(jnp.int32, (2 * blk, span), 0) % blk
    ik = lax.broadcasted_iota(jnp.int32, (2 * blk, span), 1)

    @pl.when(pl.program_id(1) == 0)
    def _():
        _fill_band_bias(bias_scr, iq, ik, radius, 1, slope_col)

    lo_q = _lo_mask((1, LANES))
    lo = _lo_mask((blk, LANES))

    def body(g, carry):
        starts, variants, values, scores = [], [], [], []
        for u in range(group):
            a = pl.multiple_of((g * group + u) * blk, blk)
            ks = pl.multiple_of(jnp.clip(a - radius, 0, seq - span), radius)
            q = q_ref[pl.ds(a, blk), :]
            zero = jnp.zeros_like(q)
            qm = jnp.concatenate([jnp.where(lo_q, q, zero), jnp.where(lo_q, zero, q)], axis=0)
            k = k_ref[pl.ds(ks, span), :]
            scores.append(lax.dot_general(qm, k, (((1,), (1,)), ((), ())),
                                          preferred_element_type=F32))
            starts.append(a)
            variants.append((a - ks) // radius)
            values.append(v_ref[pl.ds(ks, span), :])
        probs, sink_terms = [], []
        for u in range(group):
            s = scores[u] + bias_scr[variants[u]]
            m = jnp.maximum(jnp.max(s, axis=-1, keepdims=True), sink_col)
            probs.append(_exp2_probs(s, m))
            sink_terms.append(jnp.exp2(sink_col - m))
        for u in range(group):
            o, l = _value_and_sum(probs[u], values[u])
            o = o * (1.0 / (l + sink_terms[u]))
            o_ref[pl.ds(starts[u], blk), :] = jnp.where(lo, o[:blk], o[blk:]).astype(o_ref.dtype)
        return carry

    lax.fori_loop(0, seq // (blk * group), body, 0)


def _window_attention(qkv, params, *, q_blk0, k_blk, v_blk, n_pairs, radius, blk):
    b, s, _ = qkv.shape
    span = blk + 2 * radius
    return pl.pallas_call(
        functools.partial(_window_kernel, seq=s, blk=blk, radius=radius),
        grid=(n_pairs, b),
        in_specs=[
            pl.BlockSpec(memory_space=pltpu.SMEM),
            pl.BlockSpec((None, s, LANES), lambda c, bi: (bi, 0, q_blk0 + c)),
            pl.BlockSpec((None, s, LANES), lambda c, bi: (bi, 0, k_blk)),
            pl.BlockSpec((None, s, LANES), lambda c, bi: (bi, 0, v_blk)),
        ],
        out_specs=pl.BlockSpec((None, s, LANES), lambda c, bi: (bi, 0, c)),
        out_shape=jax.ShapeDtypeStruct((b, s, n_pairs * LANES), BF16),
        scratch_shapes=[pltpu.VMEM((3, 2 * blk, span), F32)],
        compiler_params=pltpu.CompilerParams(
            dimension_semantics=("arbitrary", "arbitrary"), vmem_limit_bytes=VMEM_LIMIT),
        name="window_attn",
    )(params, qkv, qkv, qkv)


PERM_CHUNK = 256
PHASES4 = 4
PIECE = PERM_CHUNK // PHASES4


def _stack_heads(q):
    lo = _lo_mask((1, LANES))
    zero = jnp.zeros_like(q)
    return jnp.concatenate([jnp.where(lo, q, zero), jnp.where(lo, zero, q)], axis=0)


def _merge_heads(x, blk):
    x = jnp.broadcast_to(x, (2 * blk, LANES))
    return jnp.where(_lo_mask((blk, LANES)), x[:blk], x[blk:])


def _attend_group(qs, ks, vs, biases):
    scores = [lax.dot_general(_stack_heads(q), k, (((1,), (1,)), ((), ())),
                              preferred_element_type=F32) for q, k in zip(qs, ks)]
    stats = []
    for s, bias in zip(scores, biases):
        s = s + bias
        m = jnp.max(s, axis=-1, keepdims=True)
        stats.append((_exp2_probs(s, m), m))
    outs = []
    for (p, m), v in zip(stats, vs):
        o, l = _value_and_sum(p, v)
        outs.append((o, m, l))
    return outs


def _fill_band_bias(bias_scr, pos_q, pos_k, radius, pos_scale, slope_col):
    for var in range(3):
        dist = jnp.abs(pos_q + var * radius - pos_k)
        dist_f = (dist * pos_scale).astype(F32)
        bias_scr[var] = jnp.where(dist <= radius, -(slope_col * dist_f) * LOG2E, NEG_INF)


def _dilated_kernel(par_ref, q_ref, k_ref, v_ref, o_ref, qp, kp, vp, acc_p, m_p, l_p,
                    bias1, bias4, bias16, *, seq, blk, radius):
    span = blk + 2 * radius
    n_chunk = seq // PERM_CHUNK
    group = ATTN_UNROLL
    c = pl.program_id(0)
    row_lo = lax.broadcasted_iota(jnp.int32, (2 * blk, 1), 0) < blk
    slope_col = jnp.where(row_lo, par_ref[0, 2 * c], par_ref[0, 2 * c + 1])

    def piece_pos(x):
        return PIECE * (x // PIECE) + PHASES4 * (x % 16) + (x % PIECE) // 16

    iq = lax.broadcasted_iota(jnp.int32, (2 * blk, span), 0) % blk
    ik = lax.broadcasted_iota(jnp.int32, (2 * blk, span), 1)

    @pl.when(pl.program_id(1) == 0)
    def _():
        _fill_band_bias(bias1, iq, ik, radius, 1, slope_col)
        _fill_band_bias(bias4, piece_pos(iq), piece_pos(ik), radius, 4, slope_col)
        _fill_band_bias(bias16, iq, ik, radius, 16, slope_col)

    rr = lax.broadcasted_iota(jnp.int32, (PERM_CHUNK, PERM_CHUNK), 0)
    cc = lax.broadcasted_iota(jnp.int32, (PERM_CHUNK, PERM_CHUNK), 1)
    to_phase = jnp.where(cc == 16 * (rr % 16) + PHASES4 * ((rr // 16) % 4) + rr // PIECE,
                         1.0, 0.0).astype(BF16)
    to_token = jnp.where(cc == PIECE * (rr % PHASES4) + 16 * ((rr // PHASES4) % 4) + rr // 16,
                         1.0, 0.0).astype(BF16)

    def permute_body(ch2, carry):
        for half in range(2):
            ch = 2 * ch2 + half
            r0 = pl.multiple_of(ch * PERM_CHUNK, PERM_CHUNK)
            x = jnp.concatenate([q_ref[pl.ds(r0, PERM_CHUNK), :],
                                 k_ref[pl.ds(r0, PERM_CHUNK), :],
                                 v_ref[pl.ds(r0, PERM_CHUNK), :]], axis=1)
            y = jnp.dot(to_phase, x, preferred_element_type=F32).astype(BF16)
            for p4 in range(PHASES4):
                rows = slice(PIECE * p4, PIECE * (p4 + 1))
                qp[p4, ch] = y[rows, 0:LANES]
                kp[p4, ch] = y[rows, LANES:2 * LANES]
                vp[p4, ch] = y[rows, 2 * LANES:3 * LANES]
        return carry

    lax.fori_loop(0, n_chunk // 2, permute_body, 0)

    pieces_per_blk = blk // PIECE
    n_blk4 = seq // (PHASES4 * blk)

    def dil4_body(g, carry):
        p4 = g // (n_blk4 // group)
        j0 = (g % (n_blk4 // group)) * group
        qs, ks, vs, bs = [], [], [], []
        for u in range(group):
            pc = pieces_per_blk * (j0 + u)
            cs = jnp.clip(pc - radius // PIECE, 0, n_chunk - span // PIECE)
            qs.append(qp[p4, pl.ds(pc, pieces_per_blk)].reshape(blk, LANES))
            ks.append(kp[p4, pl.ds(cs, span // PIECE)].reshape(span, LANES))
            vs.append(vp[p4, pl.ds(cs, span // PIECE)].reshape(span, LANES))
            bs.append(bias4[(pc - cs) // (radius // PIECE)])
        for u, (o, m, l) in enumerate(_attend_group(qs, ks, vs, bs)):
            dst = (p4, pl.ds(pieces_per_blk * (j0 + u), pieces_per_blk))
            acc_p[dst] = _merge_heads(o, blk).reshape(pieces_per_blk, PIECE, LANES)
            m_p[dst] = _merge_heads(m, blk).reshape(pieces_per_blk, PIECE, LANES)
            l_p[dst] = _merge_heads(l, blk).reshape(pieces_per_blk, PIECE, LANES)
        return carry

    lax.fori_loop(0, PHASES4 * n_blk4 // group, dil4_body, 0)

    chunks_per_blk = blk // 16

    def dil16_body(p4, carry):
        qs, ks, vs, bs, dsts = [], [], [], [], []
        for mm in range(PIECE // 16):
            off = mm * 16
            kk = kp[p4, :, pl.ds(off, 16), :].reshape(span, LANES)
            vv = vp[p4, :, pl.ds(off, 16), :].reshape(span, LANES)
            for jb in range(n_chunk // chunks_per_blk):
                dst = (p4, pl.ds(chunks_per_blk * jb, chunks_per_blk), pl.ds(off, 16), slice(None))
                qs.append(qp[dst].reshape(blk, LANES))
                ks.append(kk)
                vs.append(vv)
                bs.append(bias16[jb * (blk // radius)])
                dsts.append(dst)
        for dst, (o, m, l) in zip(dsts, _attend_group(qs, ks, vs, bs)):
            m_old = m_p[dst].reshape(blk, LANES)
            m_new = jnp.maximum(m_old, _merge_heads(m, blk))
            w_old = jnp.exp2(m_old - m_new)
            w_new = jnp.exp2(_merge_heads(m, blk) - m_new)
            acc = w_old * acc_p[dst].reshape(blk, LANES) + w_new * _merge_heads(o, blk)
            den = w_old * l_p[dst].reshape(blk, LANES) + w_new * _merge_heads(l, blk)
            acc_p[dst] = acc.reshape(chunks_per_blk, 16, LANES)
            l_p[dst] = den.reshape(chunks_per_blk, 16, LANES)
            m_p[dst] = m_new.reshape(chunks_per_blk, 16, LANES)
        return carry

    lax.fori_loop(0, PHASES4, dil16_body, 0)

    blk_per_chunk = PERM_CHUNK // blk

    def dil1_body(g, carry):
        merged = []
        for cc2 in range(group // blk_per_chunk):
            ch = g * (group // blk_per_chunk) + cc2
            acc = jnp.concatenate([acc_p[p4, ch] for p4 in range(PHASES4)], axis=0)
            den = jnp.concatenate([l_p[p4, ch] for p4 in range(PHASES4)], axis=0)
            mx = jnp.concatenate([m_p[p4, ch] for p4 in range(PHASES4)], axis=0)
            o_ph = (acc * (1.0 / den)).astype(BF16)
            lse = mx + jnp.log2(den)
            hi = lse.astype(BF16)
            rem = lse - hi.astype(F32)
            mid = rem.astype(BF16)
            low = (rem - mid.astype(F32)).astype(BF16)
            back = jnp.dot(to_token, jnp.concatenate([o_ph, hi, mid, low], axis=1),
                           preferred_element_type=F32)
            merged.append((back[:, 0:LANES], back[:, LANES:2 * LANES]
                           + back[:, 2 * LANES:3 * LANES] + back[:, 3 * LANES:4 * LANES]))
        qs, ks, vs, bs, starts = [], [], [], [], []
        for u in range(group):
            a = pl.multiple_of((g * group + u) * blk, blk)
            kst = pl.multiple_of(jnp.clip(a - radius, 0, seq - span), radius)
            qs.append(q_ref[pl.ds(a, blk), :])
            ks.append(k_ref[pl.ds(kst, span), :])
            vs.append(v_ref[pl.ds(kst, span), :])
            bs.append(bias1[(a - kst) // radius])
            starts.append(a)
        for u, (o, m, l) in enumerate(_attend_group(qs, ks, vs, bs)):
            o1 = _merge_heads(o * (1.0 / l), blk)
            lse1 = _merge_heads(m + jnp.log2(l), blk)
            o2, lse2 = merged[u // blk_per_chunk]
            half = slice((u % blk_per_chunk) * blk, (u % blk_per_chunk + 1) * blk)
            o2, lse2 = o2[half], lse2[half]
            top = jnp.maximum(lse1, lse2)
            e1 = jnp.exp2(lse1 - top)
            e2 = jnp.exp2(lse2 - top)
            o_ref[pl.ds(starts[u], blk), :] = ((e1 * o1 + e2 * o2)
                                               * (1.0 / (e1 + e2))).astype(o_ref.dtype)
        return carry

    lax.fori_loop(0, seq // (blk * group), dil1_body, 0)


def _dilated_attention(qkv, params, *, k_blk0, v_blk0, n_pairs, blk, radius):
    b, s, _ = qkv.shape
    span = blk + 2 * radius
    n_chunk = s // PERM_CHUNK
    assert s // 16 == span and s % PERM_CHUNK == 0 and blk % PIECE == 0
    phase = lambda dt: pltpu.VMEM((PHASES4, n_chunk, PIECE, LANES), dt)
    return pl.pallas_call(
        functools.partial(_dilated_kernel, seq=s, blk=blk, radius=radius),
        grid=(n_pairs, b),
        in_specs=[
            pl.BlockSpec(memory_space=pltpu.SMEM),
            pl.BlockSpec((None, s, LANES), lambda c, bi: (bi, 0, c)),
            pl.BlockSpec((None, s, LANES), lambda c, bi: (bi, 0, k_blk0 + c)),
            pl.BlockSpec((None, s, LANES), lambda c, bi: (bi, 0, v_blk0 + c)),
        ],
        out_specs=pl.BlockSpec((None, s, LANES), lambda c, bi: (bi, 0, c)),
        out_shape=jax.ShapeDtypeStruct((b, s, n_pairs * LANES), BF16),
        scratch_shapes=[phase(BF16), phase(BF16), phase(BF16), phase(F32), phase(F32), phase(F32)]
        + [pltpu.VMEM((3, 2 * blk, span), F32)] * 3,
        compiler_params=pltpu.CompilerParams(
            dimension_semantics=("arbitrary", "arbitrary"), vmem_limit_bytes=VMEM_LIMIT),
        name="dilated_attn",
    )(params, qkv, qkv, qkv)


def _na_kernel(q_ref, k_ref, v_ref, bias_ref, o_ref, *, rows):
    kh = NA_ROWS
    group = ATTN_UNROLL
    lo_q = _lo_mask((1, LANES))
    lo = _lo_mask((GRID_W, LANES))

    def body(g, carry):
        starts, offs, values, scores = [], [], [], []
        for u in range(group):
            i = g * group + u
            rs = jnp.clip(i - kh // 2, 0, rows - kh)
            qs = pl.multiple_of(i * GRID_W, GRID_W)
            ks = pl.multiple_of(rs * GRID_W, GRID_W)
            q = q_ref[pl.ds(qs, GRID_W), :]
            zero = jnp.zeros_like(q)
            qm = jnp.concatenate([jnp.where(lo_q, q, zero), jnp.where(lo_q, zero, q)], axis=0)
            k = k_ref[pl.ds(ks, kh * GRID_W), :]
            scores.append(lax.dot_general(qm, k, (((1,), (1,)), ((), ())),
                                          preferred_element_type=F32))
            starts.append(qs)
            offs.append(rs - i + (NA_ROWS - 1))
            values.append(v_ref[pl.ds(ks, kh * GRID_W), :])
        probs = []
        for u in range(group):
            s = scores[u] + jnp.concatenate(
                [bias_ref[offs[u] + 2 * j] for j in range(kh // 2)], axis=1)
            probs.append(_exp2_probs(s, jnp.max(s, axis=-1, keepdims=True)))
        for u in range(group):
            o, l = _value_and_sum(probs[u], values[u])
            o = o * (1.0 / l)
            o_ref[pl.ds(starts[u], GRID_W), :] = jnp.where(
                lo, o[:GRID_W], o[GRID_W:]).astype(o_ref.dtype)
        return carry

    lax.fori_loop(0, rows // group, body, 0)


def _na_bias_table(rpb):
    qc = np.arange(GRID_W)[:, None]
    kc = np.arange(GRID_W)[None, :]
    qstart = np.clip(qc - NA_COLS // 2, 0, GRID_W - NA_COLS)
    valid = (kc >= qstart) & (kc < qstart + NA_COLS)
    coff = np.clip(kc - qc, -(NA_COLS - 1), NA_COLS - 1) + NA_COLS - 1
    n_rel = 2 * NA_COLS - 1
    n_off = 2 * NA_ROWS - 2
    pick = (coff[:, :, None] == np.arange(n_rel)).astype(np.float32)
    pick2 = np.zeros((GRID_W, 2, GRID_W, 2, n_rel), np.float32)
    pick2[:, 0, :, 0] = pick
    pick2[:, 1, :, 1] = pick
    pick2 = pick2.reshape(GRID_W, 2 * GRID_W, 2 * n_rel)
    r = rpb.astype(F32) * LOG2E
    r = jnp.concatenate([r[:, :-1], r[:, 1:]], axis=-1)
    r = r.reshape(C_HEADS // 2, 2, n_off, 2 * n_rel)
    t = jnp.einsum("phdm,qnm->pdhqn", r, jnp.asarray(pick2), precision=lax.Precision.HIGHEST)
    t = jnp.where(jnp.asarray(np.tile(valid, (1, 2))), t, NEG_INF)
    return t.reshape(C_HEADS // 2, n_off, 2 * GRID_W, 2 * GRID_W)


def _na_attention(qkv, bias):
    b, s, n_cols = qkv.shape
    n_pairs = C_HEADS // 2
    rows = s // GRID_W
    return pl.pallas_call(
        functools.partial(_na_kernel, rows=rows),
        grid=(n_pairs, b),
        in_specs=[
            pl.BlockSpec((None, s, LANES), lambda c, bi: (bi, 0, c)),
            pl.BlockSpec((None, s, LANES), lambda c, bi: (bi, 0, n_pairs + c)),
            pl.BlockSpec((None, s, LANES), lambda c, bi: (bi, 0, 2 * n_pairs + c)),
            pl.BlockSpec((None,) + bias.shape[1:], lambda c, bi: (c, 0, 0, 0)),
        ],
        out_specs=pl.BlockSpec((None, s, LANES), lambda c, bi: (bi, 0, c)),
        out_shape=jax.ShapeDtypeStruct((b, s, n_pairs * LANES), BF16),
        compiler_params=pltpu.CompilerParams(
            dimension_semantics=("arbitrary", "arbitrary"), vmem_limit_bytes=VMEM_LIMIT),
        name="na_attn",
    )(qkv, qkv, qkv, bias)


def _silu(g):
    return g * (1.0 / (1.0 + jnp.exp(-g)))


def _ffn_kernel(x_ref, oa_ref, ob_ref, wa_ref, wb_ref, g_ref, wg_ref, wu_ref, wd_ref,
                c0_ref, c1_ref, c2_ref, y_ref, n0_ref, n1_ref, n2_ref, a_scr):
    n0_ref[...] = c0_ref[...].astype(n0_ref.dtype)
    n1_ref[...] = c1_ref[...].astype(n1_ref.dtype)
    n2_ref[...] = c2_ref[...].astype(n2_ref.dtype)
    x = (x_ref[...] + jnp.dot(oa_ref[...], wa_ref[...], preferred_element_type=F32)
         + jnp.dot(ob_ref[...], wb_ref[...], preferred_element_type=F32))
    h = _rms_rows(x, g_ref[...]).astype(BF16)
    f = wg_ref.shape[1]
    for j in range(f // MXU_N):
        sl = slice(j * MXU_N, (j + 1) * MXU_N)
        gate = jnp.dot(h, wg_ref[:, sl], preferred_element_type=F32)
        up = jnp.dot(h, wu_ref[:, sl], preferred_element_type=F32)
        a_scr[:, sl] = (_silu(gate) * up).astype(BF16)
    y_ref[...] = x + jnp.dot(a_scr[...], wd_ref[...], preferred_element_type=F32)


def _ffn(x, oa, ob, wa, wb, g, wg, wu, wd, narrow, tm=256):
    t, d = x.shape
    f = wg.shape[1]
    steps = t // tm
    const = lambda a: pl.BlockSpec(a.shape, lambda i: (0, 0), pipeline_mode=pl.Buffered(1))
    row = lambda w: pl.BlockSpec((tm, w), lambda i: (i, 0))
    slab = lambda a: pl.BlockSpec((a.shape[0] // steps, a.shape[1]), lambda i: (i, 0))
    assert all(a.shape[0] % (16 * steps) == 0 for a in narrow)
    return pl.pallas_call(
        _ffn_kernel,
        grid=(steps,),
        in_specs=[row(d), row(oa.shape[1]), row(ob.shape[1]), const(wa), const(wb),
                  pl.BlockSpec((1, d), lambda i: (0, 0)),
                  const(wg), const(wu), const(wd)] + [slab(a) for a in narrow],
        out_specs=[pl.BlockSpec((tm, d), lambda i: (i, 0))] + [slab(a) for a in narrow],
        out_shape=[jax.ShapeDtypeStruct((t, d), F32)]
        + [jax.ShapeDtypeStruct(a.shape, BF16) for a in narrow],
        scratch_shapes=[pltpu.VMEM((tm, f), BF16)],
        compiler_params=pltpu.CompilerParams(
            dimension_semantics=("arbitrary",), vmem_limit_bytes=VMEM_LIMIT),
        name="dense_swiglu",
    )(x, oa, ob, wa, wb, g.reshape(1, d), wg, wu, wd, *narrow)


TOP_K = 2
MOE_TB = 512
MOE_TM = 512
MOE_RC = 256
GATHER_WINDOW = 3
KEY_STRIDE = 65536.0


def _router_kernel(x_ref, o_ref, wo_ref, g_ref, wr_hi_ref, wr_lo_ref, x1_ref, h_ref, info_ref,
                   info_t_ref, cnt_ref, carry_scr, ltri_scr):
    i = pl.program_id(0)
    tb = x_ref.shape[0]
    parts = 2
    hb = tb // parts
    rows = [slice(p * hb, (p + 1) * hb) for p in range(parts)]
    lane = lax.broadcasted_iota(jnp.int32, (1, LANES), 1)

    @pl.when(i == 0)
    def _():
        carry_scr[...] = jnp.zeros_like(carry_scr)
        r = lax.broadcasted_iota(jnp.int32, (hb, hb), 0)
        cidx = lax.broadcasted_iota(jnp.int32, (hb, hb), 1)
        ltri_scr[...] = jnp.where(cidx < r, 1.0, 0.0).astype(BF16)

    x1s = [x_ref[rw, :] + jnp.dot(o_ref[rw, :], wo_ref[...], preferred_element_type=F32)
           for rw in rows]
    hs = [_rms_rows(x1, g_ref[...]) for x1 in x1s]
    his = [h.astype(BF16) for h in hs]
    los = [(h - hi.astype(F32)).astype(BF16) for h, hi in zip(hs, his)]
    logits_all = [jnp.dot(hi, wr_hi_ref[...], preferred_element_type=F32)
                  + jnp.dot(lo_, wr_hi_ref[...], preferred_element_type=F32)
                  + jnp.dot(hi, wr_lo_ref[...], preferred_element_type=F32)
                  for hi, lo_ in zip(his, los)]
    for rw, x1, hi in zip(rows, x1s, his):
        x1_ref[rw, :] = x1
        h_ref[rw, :] = hi

    picks = []
    for logits in logits_all:
        logits = jnp.where(lane < N_EXPERTS, logits, NEG_INF)
        v1 = jnp.max(logits, axis=-1, keepdims=True)
        i1 = jnp.min(jnp.where(logits == v1, lane, LANES), axis=-1, keepdims=True)
        rest = jnp.where(lane == i1, NEG_INF, logits)
        v2 = jnp.max(rest, axis=-1, keepdims=True)
        i2 = jnp.min(jnp.where(rest == v2, lane, LANES), axis=-1, keepdims=True)
        e2 = jnp.exp(v2 - v1)
        g1 = 1.0 / (1.0 + e2)
        picks.append((i1, i2, g1, e2 * g1))

    pickeds = [jnp.where(lane == i1, 1.0, jnp.where(lane == i2, 1.0, 0.0))
               for i1, i2, _, _ in picks]
    inside = [jnp.dot(ltri_scr[...], pk.astype(BF16), preferred_element_type=F32)
              for pk in pickeds]
    counts = [jnp.sum(pk, axis=0, keepdims=True) for pk in pickeds]
    seen = carry_scr[...]
    for p, (i1, i2, g1, g2) in enumerate(picks):
        before = inside[p] + seen
        rank1 = jnp.sum(jnp.where(lane == i1, before, 0.0), axis=-1, keepdims=True)
        rank2 = jnp.sum(jnp.where(lane == i2, before, 0.0), axis=-1, keepdims=True)
        key1 = i1.astype(F32) * KEY_STRIDE + rank1
        key2 = i2.astype(F32) * KEY_STRIDE + rank2
        info = jnp.where(lane == 0, key1, jnp.where(lane == 1, key2,
                         jnp.where(lane == 2, g1, jnp.where(lane == 3, g2, 0.0))))
        info_ref[rows[p], :] = info
        info_t_ref[:, rows[p]] = info.T[:8, :]
        seen = seen + counts[p]
    cnt_ref[...] = jnp.broadcast_to(seen - carry_scr[...], cnt_ref.shape)
    carry_scr[...] = seen


def _router(x, o, wo, g, wr):
    t, d = x.shape
    nb = t // MOE_TB
    wr_hi = wr.astype(BF16)
    wr_lo = (wr - wr_hi.astype(F32)).astype(BF16)
    row = lambda w: pl.BlockSpec((MOE_TB, w), lambda i: (i, 0))
    const = lambda a: pl.BlockSpec(a.shape, lambda i: (0, 0))
    return pl.pallas_call(
        _router_kernel,
        grid=(nb,),
        in_specs=[row(d), row(o.shape[1]), const(wo), pl.BlockSpec((1, d), lambda i: (0, 0)),
                  const(wr_hi), const(wr_lo)],
        out_specs=[row(d), row(d), row(LANES),
                   pl.BlockSpec((None, 8, MOE_TB), lambda i: (i, 0, 0)),
                   pl.BlockSpec((None, 8, LANES), lambda i: (i, 0, 0))],
        out_shape=[jax.ShapeDtypeStruct((t, d), F32),
                   jax.ShapeDtypeStruct((t, d), BF16),
                   jax.ShapeDtypeStruct((t, LANES), F32),
                   jax.ShapeDtypeStruct((nb, 8, MOE_TB), F32),
                   jax.ShapeDtypeStruct((nb, 8, LANES), F32)],
        scratch_shapes=[pltpu.VMEM((1, LANES), F32),
                        pltpu.VMEM((MOE_TB // 2, MOE_TB // 2), BF16)],
        compiler_params=pltpu.CompilerParams(
            dimension_semantics=("arbitrary",), vmem_limit_bytes=VMEM_LIMIT),
        name="moe_router",
    )(x, o, wo, g.reshape(1, d), wr_hi, wr_lo)


def _moe_plan(block_counts, n_tokens):
    nb = block_counts.shape[0]
    n_tiles = (TOP_K * n_tokens + N_EXPERTS * (MOE_TM - 1)) // MOE_TM
    n_chunks = n_tiles * MOE_TM // MOE_RC
    tc = block_counts.astype(jnp.int32)
    cb = jnp.concatenate([jnp.zeros((1, N_EXPERTS), jnp.int32), jnp.cumsum(tc, axis=0)])
    counts = cb[-1]
    padded = ((counts + MOE_TM - 1) // MOE_TM) * MOE_TM
    ends = jnp.cumsum(padded)
    starts = ends - padded
    total = ends[-1]

    tiles = jnp.arange(n_tiles, dtype=jnp.int32)
    n_valid_tiles = total // MOE_TM
    t_valid = (tiles < n_valid_tiles).astype(jnp.int32)
    t_src = jnp.maximum(jnp.minimum(tiles, n_valid_tiles - 1), 0)
    t_exp = jnp.minimum(jnp.sum((t_src * MOE_TM)[:, None] >= ends[None, :], axis=1),
                        N_EXPERTS - 1).astype(jnp.int32)

    r0 = jnp.arange(n_chunks, dtype=jnp.int32) * MOE_RC
    ex = jnp.minimum(jnp.sum(r0[:, None] >= ends[None, :], axis=1), N_EXPERTS - 1)
    live = r0 < total
    lo = r0 - starts[ex]
    base = (ex * int(KEY_STRIDE) + lo).astype(jnp.int32)
    cbe = cb[:, ex]
    g_lo = jnp.sum(cbe[1:] <= lo[None, :], axis=0).astype(jnp.int32)
    g_cnt = jnp.sum(live[None, :] & (cbe[:-1] < (lo + MOE_RC)[None, :]) & (cbe[1:] > lo[None, :]),
                    axis=0).astype(jnp.int32)

    row_lo = starts[None, :] + cb[:-1]
    row_hi = starts[None, :] + cb[1:]
    c_first = row_lo // MOE_RC
    c_num = jnp.where(row_hi > row_lo, (row_hi - 1) // MOE_RC - c_first + 1, 0)
    c_end = jnp.cumsum(c_num, axis=1)
    slots = jnp.arange(COMBINE_MAX_CHUNKS, dtype=jnp.int32)
    e_of = jnp.minimum(jnp.sum(slots[None, :, None] >= c_end[:, None, :], axis=2),
                       N_EXPERTS - 1)
    owner = e_of[:, :, None] == jnp.arange(N_EXPERTS)[None, None, :]
    c_c = slots[None, :] + jnp.sum(
        jnp.where(owner, (c_first - (c_end - c_num))[:, None, :], 0), axis=2)
    c_cnt = c_end[:, -1].astype(jnp.int32)
    c_c = jnp.clip(c_c, 0, n_chunks - 1).astype(jnp.int32).reshape(-1)
    return dict(n_tiles=n_tiles, n_chunks=n_chunks, t_valid=t_valid, t_exp=t_exp,
                base=base, live=live.astype(jnp.int32), g_lo=g_lo, g_cnt=g_cnt,
                c_cnt=c_cnt, c_c=c_c)


def _gather_kernel(live_ref, lo_ref, cnt_ref, base_ref, info_t_ref, h_ref,
                   xs_ref, gs_ref, acc_scr, gsum_scr):
    tb = info_t_ref.shape[-1]
    per_tile = xs_ref.shape[0] // MOE_RC
    row = lax.broadcasted_iota(jnp.int32, (MOE_RC, tb), 0).astype(F32)

    def chunk_body(k, carry):
        c = pl.program_id(0) * per_tile + k
        dst = pl.ds(pl.multiple_of(k * MOE_RC, MOE_RC), MOE_RC)

        @pl.when(live_ref[c] == 1)
        def _():
            first = lo_ref[c]
            n = cnt_ref[c]
            acc_scr[...] = jnp.zeros_like(acc_scr)
            gsum_scr[...] = jnp.zeros_like(gsum_scr)

            n_blocks = h_ref.shape[0] // tb

            def window(it, inner):
                want = first + GATHER_WINDOW * it
                b0 = jnp.minimum(want, n_blocks - GATHER_WINDOW)
                sels, gate = [], None
                for w in range(GATHER_WINDOW):
                    b = b0 + w
                    ok = (b >= want) & (b < first + n)
                    base = jnp.where(ok, base_ref[c], NO_MATCH_BASE).astype(F32)
                    info = info_t_ref[b]
                    m1 = (info[0:1, :] - base) == row
                    m2 = (info[1:2, :] - base) == row
                    sels.append(jnp.where(m1, 1.0, jnp.where(m2, 1.0, 0.0)).astype(BF16))
                    g = jnp.sum(jnp.where(m1, info[2:3, :], jnp.where(m2, info[3:4, :], 0.0)),
                                axis=-1, keepdims=True)
                    gate = g if gate is None else gate + g
                slab = h_ref[pl.ds(pl.multiple_of(b0 * tb, tb), GATHER_WINDOW * tb), :]
                acc_scr[...] += jnp.dot(jnp.concatenate(sels, axis=1), slab,
                                        preferred_element_type=F32)
                gsum_scr[...] += gate
                return inner

            lax.fori_loop(0, (n + GATHER_WINDOW - 1) // GATHER_WINDOW, window, 0)
            xs_ref[dst, :] = acc_scr[...].astype(xs_ref.dtype)
            gs_ref[dst, :] = jnp.broadcast_to(gsum_scr[...], (MOE_RC, gs_ref.shape[1]))

        @pl.when(live_ref[c] == 0)
        def _():
            xs_ref[dst, :] = jnp.zeros((MOE_RC, xs_ref.shape[1]), xs_ref.dtype)
            gs_ref[dst, :] = jnp.zeros((MOE_RC, gs_ref.shape[1]), gs_ref.dtype)

        return carry

    lax.fori_loop(0, per_tile, chunk_body, 0)


def _moe_gather(plan, info_t, h):
    t, d = h.shape
    rows = plan["n_chunks"] * MOE_RC
    tile = lambda i, live, lo, cn, ba: (i, 0)
    grid_spec = pltpu.PrefetchScalarGridSpec(
        num_scalar_prefetch=4,
        grid=(plan["n_tiles"],),
        in_specs=[pl.BlockSpec(info_t.shape, lambda c, live, lo, cn, ba: (0, 0, 0),
                               pipeline_mode=pl.Buffered(1)),
                  pl.BlockSpec((t, d), lambda c, live, lo, cn, ba: (0, 0),
                               pipeline_mode=pl.Buffered(1))],
        out_specs=[pl.BlockSpec((MOE_TM, d), tile), pl.BlockSpec((MOE_TM, LANES), tile)],
        scratch_shapes=[pltpu.VMEM((MOE_RC, d), F32), pltpu.VMEM((MOE_RC, 1), F32)],
    )
    return pl.pallas_call(
        _gather_kernel,
        grid_spec=grid_spec,
        out_shape=[jax.ShapeDtypeStruct((rows, d), BF16),
                   jax.ShapeDtypeStruct((rows, LANES), F32)],
        compiler_params=pltpu.CompilerParams(
            dimension_semantics=("arbitrary",), vmem_limit_bytes=VMEM_LIMIT),
        name="moe_gather",
    )(plan["live"], plan["g_lo"], plan["g_cnt"], plan["base"], info_t, h)


def _experts_kernel(exp_ref, val_ref, xs_ref, gs_ref, wg_ref, wu_ref, wd_ref, ys_ref,
                    acc_scr):
    i = pl.program_id(0)
    j = pl.program_id(1)

    last = j == pl.num_programs(1) - 1

    @pl.when(j == 0)
    def _():
        acc_scr[...] = jnp.zeros_like(acc_scr)

    @pl.when(val_ref[i] == 1)
    def _():
        x = xs_ref[...]
        gate = jnp.dot(x, wg_ref[...], preferred_element_type=F32)
        up = jnp.dot(x, wu_ref[...], preferred_element_type=F32)
        a = (_silu(gate) * up).astype(BF16)
        acc_scr[...] += jnp.dot(a, wd_ref[...], preferred_element_type=F32)

    @pl.when(last)
    def _():
        ys_ref[...] = (acc_scr[...] * gs_ref[:, 0:1]).astype(ys_ref.dtype)


def _moe_experts(plan, xs, gs, wg, wu, wd, tf):
    rows, d = xs.shape
    f = wg.shape[2]
    nf = f // tf

    def wcol(i, j, ex, val):
        return (ex[i], 0, jnp.where(val[i] == 1, j, nf - 1))

    def wrow(i, j, ex, val):
        return (ex[i], jnp.where(val[i] == 1, j, nf - 1), 0)

    tile = lambda i, j, ex, val: (i, 0)
    grid_spec = pltpu.PrefetchScalarGridSpec(
        num_scalar_prefetch=2,
        grid=(plan["n_tiles"], nf),
        in_specs=[pl.BlockSpec((MOE_TM, d), tile),
                  pl.BlockSpec((MOE_TM, LANES), tile),
                  pl.BlockSpec((None, d, tf), wcol),
                  pl.BlockSpec((None, d, tf), wcol),
                  pl.BlockSpec((None, tf, d), wrow)],
        out_specs=pl.BlockSpec((MOE_TM, d), tile),
        scratch_shapes=[pltpu.VMEM((MOE_TM, d), F32)],
    )
    return pl.pallas_call(
        _experts_kernel,
        grid_spec=grid_spec,
        out_shape=jax.ShapeDtypeStruct((rows, d), BF16),
        compiler_params=pltpu.CompilerParams(
            dimension_semantics=("arbitrary", "arbitrary"), vmem_limit_bytes=VMEM_LIMIT),
        name="moe_experts",
    )(plan["t_exp"], plan["t_valid"], xs, gs, wg, wu, wd)


NO_MATCH_BASE = -(2 ** 30)


COMBINE_GROUP = 4
COMBINE_MAX_CHUNKS = N_EXPERTS * (MOE_TB // MOE_RC + 1)


def _combine_kernel(cnt_ref, pc_ref, base_ref, info_ref, x_ref, ys_hbm, o_ref, ybuf, sem):
    b = pl.program_id(0)
    tb = x_ref.shape[0]
    d = x_ref.shape[1]

    def n_fetch(blk):
        return ((cnt_ref[blk] + COMBINE_GROUP - 1) // COMBINE_GROUP) * COMBINE_GROUP

    def chunk_of(blk, j):
        return pc_ref[blk * COMBINE_MAX_CHUNKS + jnp.minimum(j, cnt_ref[blk] - 1)]

    def copy(blk, j, slot):
        row0 = pl.multiple_of(chunk_of(blk, j) * MOE_RC, MOE_RC)
        return pltpu.make_async_copy(ys_hbm.at[pl.ds(row0, MOE_RC), :], ybuf.at[slot, j],
                                     sem.at[slot])

    def start_block(blk, slot):
        def issue(j, carry):
            copy(blk, j, slot).start()
            return carry
        lax.fori_loop(0, n_fetch(blk), issue, 0)

    @pl.when(b == 0)
    def _():
        start_block(0, 0)

    @pl.when(b + 1 < pl.num_programs(0))
    def _():
        start_block(b + 1, (b + 1) % 2)

    slot = b % 2
    n = cnt_ref[b]

    def drain(j, carry):
        copy(b, j, slot).wait()
        return carry

    lax.fori_loop(0, n_fetch(b), drain, 0)

    lane = lax.broadcasted_iota(jnp.int32, (1, LANES), 1)
    info = info_ref[...]
    k1 = jnp.sum(jnp.where(lane == 0, info, 0.0), axis=-1, keepdims=True)
    k2 = jnp.sum(jnp.where(lane == 1, info, 0.0), axis=-1, keepdims=True)
    col = lax.broadcasted_iota(jnp.int32, (tb, MOE_RC), 1).astype(F32)
    o_ref[...] = x_ref[...]

    def group(gi, carry):
        sels = []
        for w in range(COMBINE_GROUP):
            j = gi * COMBINE_GROUP + w
            base = jnp.where(j < n, base_ref[chunk_of(b, j)], NO_MATCH_BASE).astype(F32)
            sels.append(jnp.where(k1 - base == col, 1.0,
                                  jnp.where(k2 - base == col, 1.0, 0.0)).astype(BF16))
        rows = ybuf[slot, pl.ds(gi * COMBINE_GROUP, COMBINE_GROUP)]
        o_ref[...] += jnp.dot(jnp.concatenate(sels, axis=1),
                              rows.reshape(COMBINE_GROUP * MOE_RC, d),
                              preferred_element_type=F32)
        return carry

    lax.fori_loop(0, n_fetch(b) // COMBINE_GROUP, group, 0)


def _moe_combine(plan, info, ys, x):
    t, d = x.shape
    block = lambda b, cn, pc, ba: (b, 0)
    grid_spec = pltpu.PrefetchScalarGridSpec(
        num_scalar_prefetch=3,
        grid=(t // MOE_TB,),
        in_specs=[pl.BlockSpec((MOE_TB, LANES), block),
                  pl.BlockSpec((MOE_TB, d), block),
                  pl.BlockSpec(memory_space=pl.ANY)],
        out_specs=pl.BlockSpec((MOE_TB, d), block),
        scratch_shapes=[pltpu.VMEM((2, COMBINE_MAX_CHUNKS, MOE_RC, d), BF16),
                        pltpu.SemaphoreType.DMA((2,))],
    )
    return pl.pallas_call(
        _combine_kernel,
        grid_spec=grid_spec,
        out_shape=jax.ShapeDtypeStruct((t, d), F32),
        compiler_params=pltpu.CompilerParams(
            dimension_semantics=("arbitrary",), vmem_limit_bytes=VMEM_LIMIT),
        name="moe_combine",
    )(plan["c_cnt"], plan["c_c"], plan["base"], info, x, ys)


def _moe(x, o, wo, g, wr, wg, wu, wd, tf=1792):
    t, _ = x.shape
    x1, h, info, info_t, block_counts = _router(x, o, wo, g, wr)
    plan = _moe_plan(block_counts[:, 0, :N_EXPERTS], t)
    xs, gs = _moe_gather(plan, info_t, h)
    ys = _moe_experts(plan, xs, gs, wg, wu, wd, tf)
    return _moe_combine(plan, info, ys, x1)


def _alibi_slopes():
    s = np.exp2(-8.0 * np.arange(1, N_HEADS + 1) / N_HEADS).astype(np.float32)
    return s[0::2][:A_HEADS], s[1::2][:B_Q_HEADS]


_B_HEAD_ORDER = np.array([0, 4, 1, 5, 2, 6, 3, 7])


def _even_layer(x2d, batch, norm1, w_in, qn_a, kn_a, qn_b, kn_b, sink_b, w_out,
                norm2, w_gate, w_up, w_down, narrow):
    t, d = x2d.shape
    s = t // batch
    wa = A_HEADS * HEAD_DIM
    wq_b = B_Q_HEADS * HEAD_DIM
    wkv_b = B_KV_HEADS * HEAD_DIM
    qb0 = 3 * wa
    q_scale = HEAD_DIM ** -0.5 * LOG2E

    w_qb = w_in[:, qb0:qb0 + wq_b].reshape(d, B_Q_HEADS, HEAD_DIM)[:, _B_HEAD_ORDER]
    w_in_p = jnp.concatenate(
        [w_in[:, :qb0], w_qb.reshape(d, wq_b), w_in[:, qb0 + wq_b:]], axis=1).astype(BF16)
    ones = jnp.ones((HEAD_DIM,), F32)
    col_gain = jnp.concatenate([
        jnp.tile(qn_a * q_scale, A_HEADS), jnp.tile(kn_a, A_HEADS), jnp.tile(ones, A_HEADS),
        jnp.tile(qn_b * q_scale, B_Q_HEADS), jnp.tile(kn_b, B_KV_HEADS),
        jnp.tile(ones, B_KV_HEADS)])
    cpl = wa // LANES
    norm_chunks = ((True,) * (2 * cpl) + (False,) * cpl + (True,) * (wq_b // LANES)
                   + (True,) * (wkv_b // LANES) + (False,) * (wkv_b // LANES))
    qkv = _norm_proj(x2d, norm1, w_in_p, col_gain, norm_chunks)
    n_cols = qkv.shape[1]
    qkv3 = qkv.reshape(batch, s, n_cols)

    slopes_a, slopes_b = _alibi_slopes()
    par_a = jnp.asarray(slopes_a).reshape(1, A_HEADS)
    radii = {window // (2 * dil) for window, dil in DIL_CONFIGS}
    assert tuple(dil for _, dil in DIL_CONFIGS) == (1, 4, 16) and len(radii) == 1
    oa = _dilated_attention(qkv3, par_a, k_blk0=cpl, v_blk0=2 * cpl, n_pairs=cpl,
                            blk=A_BLOCK, radius=radii.pop())
    par_b = jnp.stack([jnp.asarray(slopes_b[_B_HEAD_ORDER]),
                       sink_b.astype(F32)[_B_HEAD_ORDER] * LOG2E])
    kb = 3 * cpl + wq_b // LANES
    ob = _window_attention(qkv3, par_b, q_blk0=3 * cpl, k_blk=kb, v_blk=kb + wkv_b // LANES,
                           n_pairs=wq_b // LANES, radius=B_WINDOW, blk=B_BLOCK)
    w_out_a = w_out[:wa].astype(BF16)
    w_out_b = w_out[wa:].reshape(B_Q_HEADS, HEAD_DIM, d)[_B_HEAD_ORDER].reshape(wq_b, d)
    y, *narrowed = _ffn(x2d, oa.reshape(t, wa), ob.reshape(t, wq_b), w_out_a,
                        w_out_b.astype(BF16), norm2, w_gate.astype(BF16), w_up.astype(BF16),
                        w_down.astype(BF16), narrow)
    return y, narrowed


def _odd_layer(x2d, batch, norm1, w_qkv, qn, kn, rpb, w_out, norm2, w_router,
               w_gate, w_up, w_down):
    t, d = x2d.shape
    s = t // batch
    wc = C_HEADS * HEAD_DIM
    ones = jnp.ones((HEAD_DIM,), F32)
    col_gain = jnp.concatenate([jnp.tile(qn * (HEAD_DIM ** -0.5 * LOG2E), C_HEADS),
                                jnp.tile(kn, C_HEADS), jnp.tile(ones, C_HEADS)])
    cpl = wc // LANES
    norm_chunks = (True,) * (2 * cpl) + (False,) * cpl
    qkv = _norm_proj(x2d, norm1, w_qkv.astype(BF16), col_gain, norm_chunks)
    o = _na_attention(qkv.reshape(batch, s, 3 * wc), _na_bias_table(rpb))
    wr = jnp.zeros((d, LANES), F32).at[:, :N_EXPERTS].set(w_router.astype(F32))
    return _moe(x2d, o.reshape(t, wc), w_out.astype(BF16), norm2, wr, w_gate, w_up, w_down)


def kernel(x, ev_norm1, ev_w_in, ev_qn_a, ev_kn_a, ev_qn_b, ev_kn_b, ev_sink_b, ev_w_out, ev_norm2, ev_ffn_gate, ev_ffn_up, ev_ffn_down, od_norm1, od_w_qkv, od_qn, od_kn, od_rpb, od_w_out, od_norm2, od_router, od_exp_gate, od_exp_up, od_exp_down):
    batch, s, d = x.shape
    depth = ev_norm1.shape[0] + od_norm1.shape[0]
    h = x.reshape(batch * s, d)
    n_e, _, f_e = od_exp_gate.shape[1:]
    experts = None
    for layer in range(depth):
        j = layer // 2
        if layer % 2 == 0:
            if j < od_norm1.shape[0]:
                narrow = [od_exp_gate[j].reshape(n_e * d, f_e), od_exp_up[j].reshape(n_e * d, f_e),
                          od_exp_down[j].reshape(n_e * f_e, d)]
            else:
                narrow = [jnp.zeros((batch * s // 16, LANES), F32)] * 3
            h, narrowed = _even_layer(h, batch, ev_norm1[j], ev_w_in[j], ev_qn_a[j], ev_kn_a[j],
                                      ev_qn_b[j], ev_kn_b[j], ev_sink_b[j], ev_w_out[j],
                                      ev_norm2[j], ev_ffn_gate[j], ev_ffn_up[j], ev_ffn_down[j],
                                      narrow)
            experts = (narrowed[0].reshape(n_e, d, f_e), narrowed[1].reshape(n_e, d, f_e),
                       narrowed[2].reshape(n_e, f_e, d))
        else:
            h = _odd_layer(h, batch, od_norm1[j], od_w_qkv[j], od_qn[j], od_kn[j], od_rpb[j],
                           od_w_out[j], od_norm2[j], od_router[j], *experts)
    return h.reshape(batch, s, d)
```

```python
import functools

import numpy as np
import jax
import jax.numpy as jnp
from jax import lax
from jax.experimental import pallas as pl
from jax.experimental.pallas import tpu as pltpu

D_MODEL = 1024
HEAD_DIM = 64
N_HEADS = D_MODEL // HEAD_DIM
A_HEADS = N_HEADS // 2
B_Q_HEADS = N_HEADS // 2
B_KV_HEADS = max(1, B_Q_HEADS // 4)
C_HEADS = N_HEADS
DIL_CONFIGS = ((128, 1), (512, 4), (2048, 16))
A_BLOCK = 128
B_WINDOW = 128
B_BLOCK = 128
GRID_W = 64
NA_ROWS = 8
NA_COLS = 16
N_EXPERTS = 8
RMS_EPS = 1e-6
NEG_INF = -1e30

LANES = 128
MXU_N = 256
VMEM_LIMIT = 56 * 1024 * 1024

ATTN_UNROLL = 8

F32 = jnp.float32
BF16 = jnp.bfloat16


def _lo_mask(shape):
    return lax.broadcasted_iota(jnp.int32, shape, len(shape) - 1) < HEAD_DIM


LOG2E = 1.4426950408889634


def _exp2_probs(s, m):
    return jnp.exp2(s - m).astype(BF16)


def _value_and_sum(p, v):
    ov = jnp.dot(p, jnp.concatenate([v, jnp.ones_like(v)], axis=1), preferred_element_type=F32)
    return ov[:, :LANES], ov[:, LANES:LANES + 1]


def _rms_rows(x, g):
    ms = jnp.mean(x * x, axis=-1, keepdims=True)
    return x * lax.rsqrt(ms + RMS_EPS) * g


def _norm_proj_kernel(x_ref, g_ref, w_ref, cg_ref, o_ref, *, norm_chunks):
    h = _rms_rows(x_ref[...], g_ref[...]).astype(BF16)
    n_out = o_ref.shape[-1]
    lo = _lo_mask((1, LANES))
    for j in range(n_out // MXU_N):
        y = jnp.dot(h, w_ref[:, j * MXU_N:(j + 1) * MXU_N], preferred_element_type=F32)
        for half in range(MXU_N // LANES):
            c = j * (MXU_N // LANES) + half
            yc = y[:, half * LANES:(half + 1) * LANES]
            if norm_chunks[c]:
                sq = yc * yc
                s_lo = jnp.sum(jnp.where(lo, sq, 0.0), axis=-1, keepdims=True)
                s_hi = jnp.sum(jnp.where(lo, 0.0, sq), axis=-1, keepdims=True)
                inv = jnp.where(lo, lax.rsqrt(s_lo * (1.0 / HEAD_DIM) + RMS_EPS),
                                lax.rsqrt(s_hi * (1.0 / HEAD_DIM) + RMS_EPS))
                yc = yc * inv * cg_ref[:, c * LANES:(c + 1) * LANES]
            o_ref[:, c * LANES:(c + 1) * LANES] = yc.astype(o_ref.dtype)


def _norm_proj(x, g, w, col_gain, norm_chunks, tm=512):
    t, d = x.shape
    n = w.shape[1]
    return pl.pallas_call(
        functools.partial(_norm_proj_kernel, norm_chunks=norm_chunks),
        grid=(t // tm,),
        in_specs=[
            pl.BlockSpec((tm, d), lambda i: (i, 0)),
            pl.BlockSpec((1, d), lambda i: (0, 0)),
            pl.BlockSpec((d, n), lambda i: (0, 0)),
            pl.BlockSpec((1, n), lambda i: (0, 0)),
        ],
        out_specs=pl.BlockSpec((tm, n), lambda i: (i, 0)),
        out_shape=jax.ShapeDtypeStruct((t, n), BF16),
        compiler_params=pltpu.CompilerParams(
            dimension_semantics=("arbitrary",), vmem_limit_bytes=VMEM_LIMIT),
        name="norm_proj",
    )(x, g.reshape(1, d), w, col_gain.reshape(1, n))


def _window_kernel(par_ref, q_ref, k_ref, v_ref, o_ref, bias_scr, *, seq, blk, radius):
    span = blk + 2 * radius
    c = pl.program_id(0)
    group = ATTN_UNROLL
    row_lo = lax.broadcasted_iota(jnp.int32, (2 * blk, 1), 0) < blk
    slope_col = jnp.where(row_lo, par_ref[0, 2 * c], par_ref[0, 2 * c + 1])
    sink_col = jnp.where(row_lo, par_ref[1, 2 * c], par_ref[1, 2 * c + 1])

    iq = lax.broadcasted_iota(jnp.int32, (2 * blk, span), 0) % blk
    ik = lax.broadcasted_iota(jnp.int32, (2 * blk, span), 1)

    @pl.when(pl.program_id(1) == 0)
    def _():
        _fill_band_bias(bias_scr, iq, ik, radius, 1, slope_col)

    lo_q = _lo_mask((1, LANES))
    lo = _lo_mask((blk, LANES))

    def body(g, carry):
        starts, variants, values, scores = [], [], [], []
        for u in range(group):
            a = pl.multiple_of((g * group + u) * blk, blk)
            ks = pl.multiple_of(jnp.clip(a - radius, 0, seq - span), radius)
            q = q_ref[pl.ds(a, blk), :]
            zero = jnp.zeros_like(q)
            qm = jnp.concatenate([jnp.where(lo_q, q, zero), jnp.where(lo_q, zero, q)], axis=0)
            k = k_ref[pl.ds(ks, span), :]
            scores.append(lax.dot_general(qm, k, (((1,), (1,)), ((), ())),
                                          preferred_element_type=F32))
            starts.append(a)
            variants.append((a - ks) // radius)
            values.append(v_ref[pl.ds(ks, span), :])
        probs, sink_terms = [], []
        for u in range(group):
            s = scores[u] + bias_scr[variants[u]]
            m = jnp.maximum(jnp.max(s, axis=-1, keepdims=True), sink_col)
            probs.append(_exp2_probs(s, m))
            sink_terms.append(jnp.exp2(sink_col - m))
        for u in range(group):
            o, l = _value_and_sum(probs[u], values[u])
            o = o * (1.0 / (l + sink_terms[u]))
            o_ref[pl.ds(starts[u], blk), :] = jnp.where(lo, o[:blk], o[blk:]).astype(o_ref.dtype)
        return carry

    lax.fori_loop(0, seq // (blk * group), body, 0)


def _window_attention(qkv, params, *, q_blk0, k_blk, v_blk, n_pairs, radius, blk):
    b, s, _ = qkv.shape
    span = blk + 2 * radius
    return pl.pallas_call(
        functools.partial(_window_kernel, seq=s, blk=blk, radius=radius),
        grid=(n_pairs, b),
        in_specs=[
            pl.BlockSpec(memory_space=pltpu.SMEM),
            pl.BlockSpec((None, s, LANES), lambda c, bi: (bi, 0, q_blk0 + c)),
            pl.BlockSpec((None, s, LANES), lambda c, bi: (bi, 0, k_blk)),
            pl.BlockSpec((None, s, LANES), lambda c, bi: (bi, 0, v_blk)),
        ],
        out_specs=pl.BlockSpec((None, s, LANES), lambda c, bi: (bi, 0, c)),
        out_shape=jax.ShapeDtypeStruct((b, s, n_pairs * LANES), BF16),
        scratch_shapes=[pltpu.VMEM((3, 2 * blk, span), F32)],
        compiler_params=pltpu.CompilerParams(
            dimension_semantics=("arbitrary", "arbitrary"), vmem_limit_bytes=VMEM_LIMIT),
        name="window_attn",
    )(params, qkv, qkv, qkv)


PERM_CHUNK = 256
PHASES4 = 4
PIECE = PERM_CHUNK // PHASES4


def _stack_heads(q):
    lo = _lo_mask((1, LANES))
    zero = jnp.zeros_like(q)
    return jnp.concatenate([jnp.where(lo, q, zero), jnp.where(lo, zero, q)], axis=0)


def _merge_heads(x, blk):
    x = jnp.broadcast_to(x, (2 * blk, LANES))
    return jnp.where(_lo_mask((blk, LANES)), x[:blk], x[blk:])


def _attend_group(qs, ks, vs, biases):
    scores = [lax.dot_general(_stack_heads(q), k, (((1,), (1,)), ((), ())),
                              preferred_element_type=F32) for q, k in zip(qs, ks)]
    stats = []
    for s, bias in zip(scores, biases):
        s = s + bias
        m = jnp.max(s, axis=-1, keepdims=True)
        stats.append((_exp2_probs(s, m), m))
    outs = []
    for (p, m), v in zip(stats, vs):
        o, l = _value_and_sum(p, v)
        outs.append((o, m, l))
    return outs


def _fill_band_bias(bias_scr, pos_q, pos_k, radius, pos_scale, slope_col):
    for var in range(3):
        dist = jnp.abs(pos_q + var * radius - pos_k)
        dist_f = (dist * pos_scale).astype(F32)
        bias_scr[var] = jnp.where(dist <= radius, -(slope_col * dist_f) * LOG2E, NEG_INF)


def _dilated_kernel(par_ref, q_ref, k_ref, v_ref, o_ref, qp, kp, vp, acc_p, m_p, l_p,
                    bias1, bias4, bias16, *, seq, blk, radius):
    span = blk + 2 * radius
    n_chunk = seq // PERM_CHUNK
    group = ATTN_UNROLL
    c = pl.program_id(0)
    row_lo = lax.broadcasted_iota(jnp.int32, (2 * blk, 1), 0) < blk
    slope_col = jnp.where(row_lo, par_ref[0, 2 * c], par_ref[0, 2 * c + 1])

    def piece_pos(x):
        return PIECE * (x // PIECE) + PHASES4 * (x % 16) + (x % PIECE) // 16

    iq = lax.broadcasted_iota(jnp.int32, (2 * blk, span), 0) % blk
    ik = lax.broadcasted_iota(jnp.int32, (2 * blk, span), 1)

    @pl.when(pl.program_id(1) == 0)
    def _():
        _fill_band_bias(bias1, iq, ik, radius, 1, slope_col)
        _fill_band_bias(bias4, piece_pos(iq), piece_pos(ik), radius, 4, slope_col)
        _fill_band_bias(bias16, iq, ik, radius, 16, slope_col)

    rr = lax.broadcasted_iota(jnp.int32, (PERM_CHUNK, PERM_CHUNK), 0)
    cc = lax.broadcasted_iota(jnp.int32, (PERM_CHUNK, PERM_CHUNK), 1)
    to_phase = jnp.where(cc == 16 * (rr % 16) + PHASES4 * ((rr // 16) % 4) + rr // PIECE,
                         1.0, 0.0).astype(BF16)
    to_token = jnp.where(cc == PIECE * (rr % PHASES4) + 16 * ((rr // PHASES4) % 4) + rr // 16,
                         1.0, 0.0).astype(BF16)

    def permute_body(ch2, carry):
        for half in range(2):
            ch = 2 * ch2 + half
            r0 = pl.multiple_of(ch * PERM_CHUNK, PERM_CHUNK)
            x = jnp.concatenate([q_ref[pl.ds(r0, PERM_CHUNK), :],
                                 k_ref[pl.ds(r0, PERM_CHUNK), :],
                                 v_ref[pl.ds(r0, PERM_CHUNK), :]], axis=1)
            y = jnp.dot(to_phase, x, preferred_element_type=F32).astype(BF16)
            for p4 in range(PHASES4):
                rows = slice(PIECE * p4, PIECE * (p4 + 1))
                qp[p4, ch] = y[rows, 0:LANES]
                kp[p4, ch] = y[rows, LANES:2 * LANES]
                vp[p4, ch] = y[rows, 2 * LANES:3 * LANES]
        return carry

    lax.fori_loop(0, n_chunk // 2, permute_body, 0)

    pieces_per_blk = blk // PIECE
    n_blk4 = seq // (PHASES4 * blk)

    def dil4_body(g, carry):
        p4 = g // (n_blk4 // group)
        j0 = (g % (n_blk4 // group)) * group
        qs, ks, vs, bs = [], [], [], []
        for u in range(group):
            pc = pieces_per_blk * (j0 + u)
            cs = jnp.clip(pc - radius // PIECE, 0, n_chunk - span // PIECE)
            qs.append(qp[p4, pl.ds(pc, pieces_per_blk)].reshape(blk, LANES))
            ks.append(kp[p4, pl.ds(cs, span // PIECE)].reshape(span, LANES))
            vs.append(vp[p4, pl.ds(cs, span // PIECE)].reshape(span, LANES))
            bs.append(bias4[(pc - cs) // (radius // PIECE)])
        for u, (o, m, l) in enumerate(_attend_group(qs, ks, vs, bs)):
            dst = (p4, pl.ds(pieces_per_blk * (j0 + u), pieces_per_blk))
            acc_p[dst] = _merge_heads(o, blk).reshape(pieces_per_blk, PIECE, LANES)
            m_p[dst] = _merge_heads(m, blk).reshape(pieces_per_blk, PIECE, LANES)
            l_p[dst] = _merge_heads(l, blk).reshape(pieces_per_blk, PIECE, LANES)
        return carry

    lax.fori_loop(0, PHASES4 * n_blk4 // group, dil4_body, 0)

    chunks_per_blk = blk // 16

    def dil16_body(p4, carry):
        qs, ks, vs, bs, dsts = [], [], [], [], []
        for mm in range(PIECE // 16):
            off = mm * 16
            kk = kp[p4, :, pl.ds(off, 16), :].reshape(span, LANES)
            vv = vp[p4, :, pl.ds(off, 16), :].reshape(span, LANES)
            for jb in range(n_chunk // chunks_per_blk):
                dst = (p4, pl.ds(chunks_per_blk * jb, chunks_per_blk), pl.ds(off, 16), slice(None))
                qs.append(qp[dst].reshape(blk, LANES))
                ks.append(kk)
                vs.append(vv)
                bs.append(bias16[jb * (blk // radius)])
                dsts.append(dst)
        for dst, (o, m, l) in zip(dsts, _attend_group(qs, ks, vs, bs)):
            m_old = m_p[dst].reshape(blk, LANES)
            m_new = jnp.maximum(m_old, _merge_heads(m, blk))
            w_old = jnp.exp2(m_old - m_new)
            w_new = jnp.exp2(_merge_heads(m, blk) - m_new)
            acc = w_old * acc_p[dst].reshape(blk, LANES) + w_new * _merge_heads(o, blk)
            den = w_old * l_p[dst].reshape(blk, LANES) + w_new * _merge_heads(l, blk)
            acc_p[dst] = acc.reshape(chunks_per_blk, 16, LANES)
            l_p[dst] = den.reshape(chunks_per_blk, 16, LANES)
            m_p[dst] = m_new.reshape(chunks_per_blk, 16, LANES)
        return carry

    lax.fori_loop(0, PHASES4, dil16_body, 0)

    blk_per_chunk = PERM_CHUNK // blk

    def dil1_body(g, carry):
        merged = []
        for cc2 in range(group // blk_per_chunk):
            ch = g * (group // blk_per_chunk) + cc2
            acc = jnp.concatenate([acc_p[p4, ch] for p4 in range(PHASES4)], axis=0)
            den = jnp.concatenate([l_p[p4, ch] for p4 in range(PHASES4)], axis=0)
            mx = jnp.concatenate([m_p[p4, ch] for p4 in range(PHASES4)], axis=0)
            o_ph = (acc * (1.0 / den)).astype(BF16)
            lse = mx + jnp.log2(den)
            hi = lse.astype(BF16)
            rem = lse - hi.astype(F32)
            mid = rem.astype(BF16)
            low = (rem - mid.astype(F32)).astype(BF16)
            back = jnp.dot(to_token, jnp.concatenate([o_ph, hi, mid, low], axis=1),
                           preferred_element_type=F32)
            merged.append((back[:, 0:LANES], back[:, LANES:2 * LANES]
                           + back[:, 2 * LANES:3 * LANES] + back[:, 3 * LANES:4 * LANES]))
        qs, ks, vs, bs, starts = [], [], [], [], []
        for u in range(group):
            a = pl.multiple_of((g * group + u) * blk, blk)
            kst = pl.multiple_of(jnp.clip(a - radius, 0, seq - span), radius)
            qs.append(q_ref[pl.ds(a, blk), :])
            ks.append(k_ref[pl.ds(kst, span), :])
            vs.append(v_ref[pl.ds(kst, span), :])
            bs.append(bias1[(a - kst) // radius])
            starts.append(a)
        for u, (o, m, l) in enumerate(_attend_group(qs, ks, vs, bs)):
            o1 = _merge_heads(o * (1.0 / l), blk)
            lse1 = _merge_heads(m + jnp.log2(l), blk)
            o2, lse2 = merged[u // blk_per_chunk]
            half = slice((u % blk_per_chunk) * blk, (u % blk_per_chunk + 1) * blk)
            o2, lse2 = o2[half], lse2[half]
            top = jnp.maximum(lse1, lse2)
            e1 = jnp.exp2(lse1 - top)
            e2 = jnp.exp2(lse2 - top)
            o_ref[pl.ds(starts[u], blk), :] = ((e1 * o1 + e2 * o2)
                                               * (1.0 / (e1 + e2))).astype(o_ref.dtype)
        return carry

    lax.fori_loop(0, seq // (blk * group), dil1_body, 0)


def _dilated_attention(qkv, params, *, k_blk0, v_blk0, n_pairs, blk, radius):
    b, s, _ = qkv.shape
    span = blk + 2 * radius
    n_chunk = s // PERM_CHUNK
    assert s // 16 == span and s % PERM_CHUNK == 0 and blk % PIECE == 0
    phase = lambda dt: pltpu.VMEM((PHASES4, n_chunk, PIECE, LANES), dt)
    return pl.pallas_call(
        functools.partial(_dilated_kernel, seq=s, blk=blk, radius=radius),
        grid=(n_pairs, b),
        in_specs=[
            pl.BlockSpec(memory_space=pltpu.SMEM),
            pl.BlockSpec((None, s, LANES), lambda c, bi: (bi, 0, c)),
            pl.BlockSpec((None, s, LANES), lambda c, bi: (bi, 0, k_blk0 + c)),
            pl.BlockSpec((None, s, LANES), lambda c, bi: (bi, 0, v_blk0 + c)),
        ],
        out_specs=pl.BlockSpec((None, s, LANES), lambda c, bi: (bi, 0, c)),
        out_shape=jax.ShapeDtypeStruct((b, s, n_pairs * LANES), BF16),
        scratch_shapes=[phase(BF16), phase(BF16), phase(BF16), phase(F32), phase(F32), phase(F32)]
        + [pltpu.VMEM((3, 2 * blk, span), F32)] * 3,
        compiler_params=pltpu.CompilerParams(
            dimension_semantics=("arbitrary", "arbitrary"), vmem_limit_bytes=VMEM_LIMIT),
        name="dilated_attn",
    )(params, qkv, qkv, qkv)


def _na_kernel(q_ref, k_ref, v_ref, bias_ref, o_ref, *, rows):
    kh = NA_ROWS
    group = ATTN_UNROLL
    lo_q = _lo_mask((1, LANES))
    lo = _lo_mask((GRID_W, LANES))

    def body(g, carry):
        starts, offs, values, scores = [], [], [], []
        for u in range(group):
            i = g * group + u
            rs = jnp.clip(i - kh // 2, 0, rows - kh)
            qs = pl.multiple_of(i * GRID_W, GRID_W)
            ks = pl.multiple_of(rs * GRID_W, GRID_W)
            q = q_ref[pl.ds(qs, GRID_W), :]
            zero = jnp.zeros_like(q)
            qm = jnp.concatenate([jnp.where(lo_q, q, zero), jnp.where(lo_q, zero, q)], axis=0)
            k = k_ref[pl.ds(ks, kh * GRID_W), :]
            scores.append(lax.dot_general(qm, k, (((1,), (1,)), ((), ())),
                                          preferred_element_type=F32))
            starts.append(qs)
            offs.append(rs - i + (NA_ROWS - 1))
            values.append(v_ref[pl.ds(ks, kh * GRID_W), :])
        probs = []
        for u in range(group):
            s = scores[u] + jnp.concatenate(
                [bias_ref[offs[u] + 2 * j] for j in range(kh // 2)], axis=1)
            probs.append(_exp2_probs(s, jnp.max(s, axis=-1, keepdims=True)))
        for u in range(group):
            o, l = _value_and_sum(probs[u], values[u])
            o = o * (1.0 / l)
            o_ref[pl.ds(starts[u], GRID_W), :] = jnp.where(
                lo, o[:GRID_W], o[GRID_W:]).astype(o_ref.dtype)
        return carry

    lax.fori_loop(0, rows // group, body, 0)


def _na_bias_table(rpb):
    qc = np.arange(GRID_W)[:, None]
    kc = np.arange(GRID_W)[None, :]
    qstart = np.clip(qc - NA_COLS // 2, 0, GRID_W - NA_COLS)
    valid = (kc >= qstart) & (kc < qstart + NA_COLS)
    coff = np.clip(kc - qc, -(NA_COLS - 1), NA_COLS - 1) + NA_COLS - 1
    n_rel = 2 * NA_COLS - 1
    n_off = 2 * NA_ROWS - 2
    pick = (coff[:, :, None] == np.arange(n_rel)).astype(np.float32)
    pick2 = np.zeros((GRID_W, 2, GRID_W, 2, n_rel), np.float32)
    pick2[:, 0, :, 0] = pick
    pick2[:, 1, :, 1] = pick
    pick2 = pick2.reshape(GRID_W, 2 * GRID_W, 2 * n_rel)
    r = rpb.astype(F32) * LOG2E
    r = jnp.concatenate([r[:, :-1], r[:, 1:]], axis=-1)
    r = r.reshape(C_HEADS // 2, 2, n_off, 2 * n_rel)
    t = jnp.einsum("phdm,qnm->pdhqn", r, jnp.asarray(pick2), precision=lax.Precision.HIGHEST)
    t = jnp.where(jnp.asarray(np.tile(valid, (1, 2))), t, NEG_INF)
    return t.reshape(C_HEADS // 2, n_off, 2 * GRID_W, 2 * GRID_W)


def _na_attention(qkv, bias):
    b, s, n_cols = qkv.shape
    n_pairs = C_HEADS // 2
    rows = s // GRID_W
    return pl.pallas_call(
        functools.partial(_na_kernel, rows=rows),
        grid=(n_pairs, b),
        in_specs=[
            pl.BlockSpec((None, s, LANES), lambda c, bi: (bi, 0, c)),
            pl.BlockSpec((None, s, LANES), lambda c, bi: (bi, 0, n_pairs + c)),
            pl.BlockSpec((None, s, LANES), lambda c, bi: (bi, 0, 2 * n_pairs + c)),
            pl.BlockSpec((None,) + bias.shape[1:], lambda c, bi: (c, 0, 0, 0)),
        ],
        out_specs=pl.BlockSpec((None, s, LANES), lambda c, bi: (bi, 0, c)),
        out_shape=jax.ShapeDtypeStruct((b, s, n_pairs * LANES), BF16),
        compiler_params=pltpu.CompilerParams(
            dimension_semantics=("arbitrary", "arbitrary"), vmem_limit_bytes=VMEM_LIMIT),
        name="na_attn",
    )(qkv, qkv, qkv, bias)


def _silu(g):
    return g * (1.0 / (1.0 + jnp.exp(-g)))


def _ffn_kernel(x_ref, oa_ref, ob_ref, wa_ref, wb_ref, g_ref, wg_ref, wu_ref, wd_ref,
                c0_ref, c1_ref, c2_ref, y_ref, n0_ref, n1_ref, n2_ref, a_scr):
    n0_ref[...] = c0_ref[...].astype(n0_ref.dtype)
    n1_ref[...] = c1_ref[...].astype(n1_ref.dtype)
    n2_ref[...] = c2_ref[...].astype(n2_ref.dtype)
    x = (x_ref[...] + jnp.dot(oa_ref[...], wa_ref[...], preferred_element_type=F32)
         + jnp.dot(ob_ref[...], wb_ref[...], preferred_element_type=F32))
    h = _rms_rows(x, g_ref[...]).astype(BF16)
    f = wg_ref.shape[1]
    for j in range(f // MXU_N):
        sl = slice(j * MXU_N, (j + 1) * MXU_N)
        gate = jnp.dot(h, wg_ref[:, sl], preferred_element_type=F32)
        up = jnp.dot(h, wu_ref[:, sl], preferred_element_type=F32)
        a_scr[:, sl] = (_silu(gate) * up).astype(BF16)
    y_ref[...] = x + jnp.dot(a_scr[...], wd_ref[...], preferred_element_type=F32)


def _ffn(x, oa, ob, wa, wb, g, wg, wu, wd, narrow, tm=256):
    t, d = x.shape
    f = wg.shape[1]
    steps = t // tm
    const = lambda a: pl.BlockSpec(a.shape, lambda i: (0, 0), pipeline_mode=pl.Buffered(1))
    row = lambda w: pl.BlockSpec((tm, w), lambda i: (i, 0))
    slab = lambda a: pl.BlockSpec((a.shape[0] // steps, a.shape[1]), lambda i: (i, 0))
    assert all(a.shape[0] % (16 * steps) == 0 for a in narrow)
    return pl.pallas_call(
        _ffn_kernel,
        grid=(steps,),
        in_specs=[row(d), row(oa.shape[1]), row(ob.shape[1]), const(wa), const(wb),
                  pl.BlockSpec((1, d), lambda i: (0, 0)),
                  const(wg), const(wu), const(wd)] + [slab(a) for a in narrow],
        out_specs=[pl.BlockSpec((tm, d), lambda i: (i, 0))] + [slab(a) for a in narrow],
        out_shape=[jax.ShapeDtypeStruct((t, d), F32)]
        + [jax.ShapeDtypeStruct(a.shape, BF16) for a in narrow],
        scratch_shapes=[pltpu.VMEM((tm, f), BF16)],
        compiler_params=pltpu.CompilerParams(
            dimension_semantics=("arbitrary",), vmem_limit_bytes=VMEM_LIMIT),
        name="dense_swiglu",
    )(x, oa, ob, wa, wb, g.reshape(1, d), wg, wu, wd, *narrow)


TOP_K = 2
MOE_TB = 512
MOE_TM = 512
MOE_RC = 256
GATHER_WINDOW = 3
KEY_STRIDE = 65536.0


def _router_kernel(x_ref, o_ref, wo_ref, g_ref, wr_hi_ref, wr_lo_ref, x1_ref, h_ref, info_ref,
                   info_t_ref, cnt_ref, carry_scr, ltri_scr):
    i = pl.program_id(0)
    tb = x_ref.shape[0]
    parts = 2
    hb = tb // parts
    rows = [slice(p * hb, (p + 1) * hb) for p in range(parts)]
    lane = lax.broadcasted_iota(jnp.int32, (1, LANES), 1)

    @pl.when(i == 0)
    def _():
        carry_scr[...] = jnp.zeros_like(carry_scr)
        r = lax.broadcasted_iota(jnp.int32, (hb, hb), 0)
        cidx = lax.broadcasted_iota(jnp.int32, (hb, hb), 1)
        ltri_scr[...] = jnp.where(cidx < r, 1.0, 0.0).astype(BF16)

    x1s = [x_ref[rw, :] + jnp.dot(o_ref[rw, :], wo_ref[...], preferred_element_type=F32)
           for rw in rows]
    hs = [_rms_rows(x1, g_ref[...]) for x1 in x1s]
    his = [h.astype(BF16) for h in hs]
    los = [(h - hi.astype(F32)).astype(BF16) for h, hi in zip(hs, his)]
    logits_all = [jnp.dot(hi, wr_hi_ref[...], preferred_element_type=F32)
                  + jnp.dot(lo_, wr_hi_ref[...], preferred_element_type=F32)
                  + jnp.dot(hi, wr_lo_ref[...], preferred_element_type=F32)
                  for hi, lo_ in zip(his, los)]
    for rw, x1, hi in zip(rows, x1s, his):
        x1_ref[rw, :] = x1
        h_ref[rw, :] = hi

    picks = []
    for logits in logits_all:
        logits = jnp.where(lane < N_EXPERTS, logits, NEG_INF)
        v1 = jnp.max(logits, axis=-1, keepdims=True)
        i1 = jnp.min(jnp.where(logits == v1, lane, LANES), axis=-1, keepdims=True)
        rest = jnp.where(lane == i1, NEG_INF, logits)
        v2 = jnp.max(rest, axis=-1, keepdims=True)
        i2 = jnp.min(jnp.where(rest == v2, lane, LANES), axis=-1, keepdims=True)
        e2 = jnp.exp(v2 - v1)
        g1 = 1.0 / (1.0 + e2)
        picks.append((i1, i2, g1, e2 * g1))

    pickeds = [jnp.where(lane == i1, 1.0, jnp.where(lane == i2, 1.0, 0.0))
               for i1, i2, _, _ in picks]
    inside = [jnp.dot(ltri_scr[...], pk.astype(BF16), preferred_element_type=F32)
              for pk in pickeds]
    counts = [jnp.sum(pk, axis=0, keepdims=True) for pk in pickeds]
    seen = carry_scr[...]
    for p, (i1, i2, g1, g2) in enumerate(picks):
        before = inside[p] + seen
        rank1 = jnp.sum(jnp.where(lane == i1, before, 0.0), axis=-1, keepdims=True)
        rank2 = jnp.sum(jnp.where(lane == i2, before, 0.0), axis=-1, keepdims=True)
        key1 = i1.astype(F32) * KEY_STRIDE + rank1
        key2 = i2.astype(F32) * KEY_STRIDE + rank2
        info = jnp.where(lane == 0, key1, jnp.where(lane == 1, key2,
                         jnp.where(lane == 2, g1, jnp.where(lane == 3, g2, 0.0))))
        info_ref[rows[p], :] = info
        info_t_ref[:, rows[p]] = info.T[:8, :]
        seen = seen + counts[p]
    cnt_ref[...] = jnp.broadcast_to(seen - carry_scr[...], cnt_ref.shape)
    carry_scr[...] = seen


def _router(x, o, wo, g, wr):
    t, d = x.shape
    nb = t // MOE_TB
    wr_hi = wr.astype(BF16)
    wr_lo = (wr - wr_hi.astype(F32)).astype(BF16)
    row = lambda w: pl.BlockSpec((MOE_TB, w), lambda i: (i, 0))
    const = lambda a: pl.BlockSpec(a.shape, lambda i: (0, 0))
    return pl.pallas_call(
        _router_kernel,
        grid=(nb,),
        in_specs=[row(d), row(o.shape[1]), const(wo), pl.BlockSpec((1, d), lambda i: (0, 0)),
                  const(wr_hi), const(wr_lo)],
        out_specs=[row(d), row(d), row(LANES),
                   pl.BlockSpec((None, 8, MOE_TB), lambda i: (i, 0, 0)),
                   pl.BlockSpec((None, 8, LANES), lambda i: (i, 0, 0))],
        out_shape=[jax.ShapeDtypeStruct((t, d), F32),
                   jax.ShapeDtypeStruct((t, d), BF16),
                   jax.ShapeDtypeStruct((t, LANES), F32),
                   jax.ShapeDtypeStruct((nb, 8, MOE_TB), F32),
                   jax.ShapeDtypeStruct((nb, 8, LANES), F32)],
        scratch_shapes=[pltpu.VMEM((1, LANES), F32),
                        pltpu.VMEM((MOE_TB // 2, MOE_TB // 2), BF16)],
        compiler_params=pltpu.CompilerParams(
            dimension_semantics=("arbitrary",), vmem_limit_bytes=VMEM_LIMIT),
        name="moe_router",
    )(x, o, wo, g.reshape(1, d), wr_hi, wr_lo)


def _moe_plan(block_counts, n_tokens):
    nb = block_counts.shape[0]
    n_tiles = (TOP_K * n_tokens + N_EXPERTS * (MOE_TM - 1)) // MOE_TM
    n_chunks = n_tiles * MOE_TM // MOE_RC
    tc = block_counts.astype(jnp.int32)
    cb = jnp.concatenate([jnp.zeros((1, N_EXPERTS), jnp.int32), jnp.cumsum(tc, axis=0)])
    counts = cb[-1]
    padded = ((counts + MOE_TM - 1) // MOE_TM) * MOE_TM
    ends = jnp.cumsum(padded)
    starts = ends - padded
    total = ends[-1]

    tiles = jnp.arange(n_tiles, dtype=jnp.int32)
    n_valid_tiles = total // MOE_TM
    t_valid = (tiles < n_valid_tiles).astype(jnp.int32)
    t_src = jnp.maximum(jnp.minimum(tiles, n_valid_tiles - 1), 0)
    t_exp = jnp.minimum(jnp.sum((t_src * MOE_TM)[:, None] >= ends[None, :], axis=1),
                        N_EXPERTS - 1).astype(jnp.int32)

    r0 = jnp.arange(n_chunks, dtype=jnp.int32) * MOE_RC
    ex = jnp.minimum(jnp.sum(r0[:, None] >= ends[None, :], axis=1), N_EXPERTS - 1)
    live = r0 < total
    lo = r0 - starts[ex]
    base = (ex * int(KEY_STRIDE) + lo).astype(jnp.int32)
    cbe = cb[:, ex]
    g_lo = jnp.sum(cbe[1:] <= lo[None, :], axis=0).astype(jnp.int32)
    g_cnt = jnp.sum(live[None, :] & (cbe[:-1] < (lo + MOE_RC)[None, :]) & (cbe[1:] > lo[None, :]),
                    axis=0).astype(jnp.int32)

    row_lo = starts[None, :] + cb[:-1]
    row_hi = starts[None, :] + cb[1:]
    c_first = row_lo // MOE_RC
    c_num = jnp.where(row_hi > row_lo, (row_hi - 1) // MOE_RC - c_first + 1, 0)
    c_end = jnp.cumsum(c_num, axis=1)
    slots = jnp.arange(COMBINE_MAX_CHUNKS, dtype=jnp.int32)
    e_of = jnp.minimum(jnp.sum(slots[None, :, None] >= c_end[:, None, :], axis=2),
                       N_EXPERTS - 1)
    owner = e_of[:, :, None] == jnp.arange(N_EXPERTS)[None, None, :]
    c_c = slots[None, :] + jnp.sum(
        jnp.where(owner, (c_first - (c_end - c_num))[:, None, :], 0), axis=2)
    c_cnt = c_end[:, -1].astype(jnp.int32)
    c_c = jnp.clip(c_c, 0, n_chunks - 1).astype(jnp.int32).reshape(-1)
    return dict(n_tiles=n_tiles, n_chunks=n_chunks, t_valid=t_valid, t_exp=t_exp,
                base=base, live=live.astype(jnp.int32), g_lo=g_lo, g_cnt=g_cnt,
                c_cnt=c_cnt, c_c=c_c)


def _gather_kernel(live_ref, lo_ref, cnt_ref, base_ref, info_t_ref, h_ref,
                   xs_ref, gs_ref, acc_scr, gsum_scr):
    tb = info_t_ref.shape[-1]
    per_tile = xs_ref.shape[0] // MOE_RC
    row = lax.broadcasted_iota(jnp.int32, (MOE_RC, tb), 0).astype(F32)

    def chunk_body(k, carry):
        c = pl.program_id(0) * per_tile + k
        dst = pl.ds(pl.multiple_of(k * MOE_RC, MOE_RC), MOE_RC)

        @pl.when(live_ref[c] == 1)
        def _():
            first = lo_ref[c]
            n = cnt_ref[c]
            n_blocks = h_ref.shape[0] // tb

            def window(it):
                want = first + GATHER_WINDOW * it
                b0 = jnp.minimum(want, n_blocks - GATHER_WINDOW)
                sels, gate = [], None
                for w in range(GATHER_WINDOW):
                    b = b0 + w
                    ok = (b >= want) & (b < first + n)
                    base = jnp.where(ok, base_ref[c], NO_MATCH_BASE).astype(F32)
                    info = info_t_ref[b]
                    m1 = (info[0:1, :] - base) == row
                    m2 = (info[1:2, :] - base) == row
                    sels.append(jnp.where(m1, 1.0, jnp.where(m2, 1.0, 0.0)).astype(BF16))
                    g = jnp.sum(jnp.where(m1, info[2:3, :], jnp.where(m2, info[3:4, :], 0.0)),
                                axis=-1, keepdims=True)
                    gate = g if gate is None else gate + g
                slab = h_ref[pl.ds(pl.multiple_of(b0 * tb, tb), GATHER_WINDOW * tb), :]
                return jnp.dot(jnp.concatenate(sels, axis=1), slab,
                               preferred_element_type=F32), gate

            acc_scr[...], gsum_scr[...] = window(0)

            def more(it, inner):
                rows, gate = window(it)
                acc_scr[...] += rows
                gsum_scr[...] += gate
                return inner

            lax.fori_loop(1, (n + GATHER_WINDOW - 1) // GATHER_WINDOW, more, 0)
            xs_ref[dst, :] = acc_scr[...].astype(xs_ref.dtype)
            gs_ref[dst, :] = jnp.broadcast_to(gsum_scr[...], (MOE_RC, gs_ref.shape[1]))

        @pl.when(live_ref[c] == 0)
        def _():
            xs_ref[dst, :] = jnp.zeros((MOE_RC, xs_ref.shape[1]), xs_ref.dtype)
            gs_ref[dst, :] = jnp.zeros((MOE_RC, gs_ref.shape[1]), gs_ref.dtype)

        return carry

    lax.fori_loop(0, per_tile, chunk_body, 0)


def _moe_gather(plan, info_t, h):
    t, d = h.shape
    rows = plan["n_chunks"] * MOE_RC
    tile = lambda i, live, lo, cn, ba: (i, 0)
    grid_spec = pltpu.PrefetchScalarGridSpec(
        num_scalar_prefetch=4,
        grid=(plan["n_tiles"],),
        in_specs=[pl.BlockSpec(info_t.shape, lambda c, live, lo, cn, ba: (0, 0, 0),
                               pipeline_mode=pl.Buffered(1)),
                  pl.BlockSpec((t, d), lambda c, live, lo, cn, ba: (0, 0),
                               pipeline_mode=pl.Buffered(1))],
        out_specs=[pl.BlockSpec((MOE_TM, d), tile), pl.BlockSpec((MOE_TM, LANES), tile)],
        scratch_shapes=[pltpu.VMEM((MOE_RC, d), F32), pltpu.VMEM((MOE_RC, 1), F32)],
    )
    return pl.pallas_call(
        _gather_kernel,
        grid_spec=grid_spec,
        out_shape=[jax.ShapeDtypeStruct((rows, d), BF16),
                   jax.ShapeDtypeStruct((rows, LANES), F32)],
        compiler_params=pltpu.CompilerParams(
            dimension_semantics=("arbitrary",), vmem_limit_bytes=VMEM_LIMIT),
        name="moe_gather",
    )(plan["live"], plan["g_lo"], plan["g_cnt"], plan["base"], info_t, h)


def _experts_kernel(exp_ref, val_ref, xs_ref, gs_ref, wg_ref, wu_ref, wd_ref, ys_ref,
                    acc_scr):
    i = pl.program_id(0)
    j = pl.program_id(1)

    last = j == pl.num_programs(1) - 1

    @pl.when(j == 0)
    def _():
        acc_scr[...] = jnp.zeros_like(acc_scr)

    @pl.when(val_ref[i] == 1)
    def _():
        x = xs_ref[...]
        gate = jnp.dot(x, wg_ref[...], preferred_element_type=F32)
        up = jnp.dot(x, wu_ref[...], preferred_element_type=F32)
        a = (_silu(gate) * up).astype(BF16)
        acc_scr[...] += jnp.dot(a, wd_ref[...], preferred_element_type=F32)

    @pl.when(last)
    def _():
        ys_ref[...] = (acc_scr[...] * gs_ref[:, 0:1]).astype(ys_ref.dtype)


def _moe_experts(plan, xs, gs, wg, wu, wd, tf):
    rows, d = xs.shape
    f = wg.shape[2]
    nf = f // tf

    def wcol(i, j, ex, val):
        return (ex[i], 0, jnp.where(val[i] == 1, j, nf - 1))

    def wrow(i, j, ex, val):
        return (ex[i], jnp.where(val[i] == 1, j, nf - 1), 0)

    tile = lambda i, j, ex, val: (i, 0)
    grid_spec = pltpu.PrefetchScalarGridSpec(
        num_scalar_prefetch=2,
        grid=(plan["n_tiles"], nf),
        in_specs=[pl.BlockSpec((MOE_TM, d), tile),
                  pl.BlockSpec((MOE_TM, LANES), tile),
                  pl.BlockSpec((None, d, tf), wcol),
                  pl.BlockSpec((None, d, tf), wcol),
                  pl.BlockSpec((None, tf, d), wrow)],
        out_specs=pl.BlockSpec((MOE_TM, d), tile),
        scratch_shapes=[pltpu.VMEM((MOE_TM, d), F32)],
    )
    return pl.pallas_call(
        _experts_kernel,
        grid_spec=grid_spec,
        out_shape=jax.ShapeDtypeStruct((rows, d), BF16),
        compiler_params=pltpu.CompilerParams(
            dimension_semantics=("arbitrary", "arbitrary"), vmem_limit_bytes=VMEM_LIMIT),
        name="moe_experts",
    )(plan["t_exp"], plan["t_valid"], xs, gs, wg, wu, wd)


NO_MATCH_BASE = -(2 ** 30)


COMBINE_GROUP = 4
COMBINE_MAX_CHUNKS = N_EXPERTS * (MOE_TB // MOE_RC + 1)


def _combine_kernel(cnt_ref, pc_ref, base_ref, info_ref, x_ref, ys_hbm, o_ref, ybuf, sem):
    b = pl.program_id(0)
    tb = x_ref.shape[0]
    d = x_ref.shape[1]

    def n_fetch(blk):
        return ((cnt_ref[blk] + COMBINE_GROUP - 1) // COMBINE_GROUP) * COMBINE_GROUP

    def chunk_of(blk, j):
        return pc_ref[blk * COMBINE_MAX_CHUNKS + jnp.minimum(j, cnt_ref[blk] - 1)]

    def copy(blk, j, slot):
        row0 = pl.multiple_of(chunk_of(blk, j) * MOE_RC, MOE_RC)
        return pltpu.make_async_copy(ys_hbm.at[pl.ds(row0, MOE_RC), :], ybuf.at[slot, j],
                                     sem.at[slot])

    def start_block(blk, slot):
        def issue(j, carry):
            copy(blk, j, slot).start()
            return carry
        lax.fori_loop(0, n_fetch(blk), issue, 0)

    @pl.when(b == 0)
    def _():
        start_block(0, 0)

    @pl.when(b + 1 < pl.num_programs(0))
    def _():
        start_block(b + 1, (b + 1) % 2)

    slot = b % 2
    n = cnt_ref[b]

    def drain(j, carry):
        copy(b, j, slot).wait()
        return carry

    lax.fori_loop(0, n_fetch(b), drain, 0)

    lane = lax.broadcasted_iota(jnp.int32, (1, LANES), 1)
    info = info_ref[...]
    k1 = jnp.sum(jnp.where(lane == 0, info, 0.0), axis=-1, keepdims=True)
    k2 = jnp.sum(jnp.where(lane == 1, info, 0.0), axis=-1, keepdims=True)
    col = lax.broadcasted_iota(jnp.int32, (tb, MOE_RC), 1).astype(F32)
    o_ref[...] = x_ref[...]

    def group(gi, carry):
        sels = []
        for w in range(COMBINE_GROUP):
            j = gi * COMBINE_GROUP + w
            base = jnp.where(j < n, base_ref[chunk_of(b, j)], NO_MATCH_BASE).astype(F32)
            sels.append(jnp.where(k1 - base == col, 1.0,
                                  jnp.where(k2 - base == col, 1.0, 0.0)).astype(BF16))
        rows = ybuf[slot, pl.ds(gi * COMBINE_GROUP, COMBINE_GROUP)]
        o_ref[...] += jnp.dot(jnp.concatenate(sels, axis=1),
                              rows.reshape(COMBINE_GROUP * MOE_RC, d),
                              preferred_element_type=F32)
        return carry

    lax.fori_loop(0, n_fetch(b) // COMBINE_GROUP, group, 0)


def _moe_combine(plan, info, ys, x):
    t, d = x.shape
    block = lambda b, cn, pc, ba: (b, 0)
    grid_spec = pltpu.PrefetchScalarGridSpec(
        num_scalar_prefetch=3,
        grid=(t // MOE_TB,),
        in_specs=[pl.BlockSpec((MOE_TB, LANES), block),
                  pl.BlockSpec((MOE_TB, d), block),
                  pl.BlockSpec(memory_space=pl.ANY)],
        out_specs=pl.BlockSpec((MOE_TB, d), block),
        scratch_shapes=[pltpu.VMEM((2, COMBINE_MAX_CHUNKS, MOE_RC, d), BF16),
                        pltpu.SemaphoreType.DMA((2,))],
    )
    return pl.pallas_call(
        _combine_kernel,
        grid_spec=grid_spec,
        out_shape=jax.ShapeDtypeStruct((t, d), F32),
        compiler_params=pltpu.CompilerParams(
            dimension_semantics=("arbitrary",), vmem_limit_bytes=VMEM_LIMIT),
        name="moe_combine",
    )(plan["c_cnt"], plan["c_c"], plan["base"], info, x, ys)


def _moe(x, o, wo, g, wr, wg, wu, wd, tf=1792):
    t, _ = x.shape
    x1, h, info, info_t, block_counts = _router(x, o, wo, g, wr)
    plan = _moe_plan(block_counts[:, 0, :N_EXPERTS], t)
    xs, gs = _moe_gather(plan, info_t, h)
    ys = _moe_experts(plan, xs, gs, wg, wu, wd, tf)
    return _moe_combine(plan, info, ys, x1)


def _alibi_slopes():
    s = np.exp2(-8.0 * np.arange(1, N_HEADS + 1) / N_HEADS).astype(np.float32)
    return s[0::2][:A_HEADS], s[1::2][:B_Q_HEADS]


_B_HEAD_ORDER = np.array([0, 4, 1, 5, 2, 6, 3, 7])


def _even_layer(x2d, batch, norm1, w_in, qn_a, kn_a, qn_b, kn_b, sink_b, w_out,
                norm2, w_gate, w_up, w_down, narrow):
    t, d = x2d.shape
    s = t // batch
    wa = A_HEADS * HEAD_DIM
    wq_b = B_Q_HEADS * HEAD_DIM
    wkv_b = B_KV_HEADS * HEAD_DIM
    qb0 = 3 * wa
    q_scale = HEAD_DIM ** -0.5 * LOG2E

    w_qb = w_in[:, qb0:qb0 + wq_b].reshape(d, B_Q_HEADS, HEAD_DIM)[:, _B_HEAD_ORDER]
    w_in_p = jnp.concatenate(
        [w_in[:, :qb0], w_qb.reshape(d, wq_b), w_in[:, qb0 + wq_b:]], axis=1).astype(BF16)
    ones = jnp.ones((HEAD_DIM,), F32)
    col_gain = jnp.concatenate([
        jnp.tile(qn_a * q_scale, A_HEADS), jnp.tile(kn_a, A_HEADS), jnp.tile(ones, A_HEADS),
        jnp.tile(qn_b * q_scale, B_Q_HEADS), jnp.tile(kn_b, B_KV_HEADS),
        jnp.tile(ones, B_KV_HEADS)])
    cpl = wa // LANES
    norm_chunks = ((True,) * (2 * cpl) + (False,) * cpl + (True,) * (wq_b // LANES)
                   + (True,) * (wkv_b // LANES) + (False,) * (wkv_b // LANES))
    qkv = _norm_proj(x2d, norm1, w_in_p, col_gain, norm_chunks)
    n_cols = qkv.shape[1]
    qkv3 = qkv.reshape(batch, s, n_cols)

    slopes_a, slopes_b = _alibi_slopes()
    par_a = jnp.asarray(slopes_a).reshape(1, A_HEADS)
    radii = {window // (2 * dil) for window, dil in DIL_CONFIGS}
    assert tuple(dil for _, dil in DIL_CONFIGS) == (1, 4, 16) and len(radii) == 1
    oa = _dilated_attention(qkv3, par_a, k_blk0=cpl, v_blk0=2 * cpl, n_pairs=cpl,
                            blk=A_BLOCK, radius=radii.pop())
    par_b = jnp.stack([jnp.asarray(slopes_b[_B_HEAD_ORDER]),
                       sink_b.astype(F32)[_B_HEAD_ORDER] * LOG2E])
    kb = 3 * cpl + wq_b // LANES
    ob = _window_attention(qkv3, par_b, q_blk0=3 * cpl, k_blk=kb, v_blk=kb + wkv_b // LANES,
                           n_pairs=wq_b // LANES, radius=B_WINDOW, blk=B_BLOCK)
    w_out_a = w_out[:wa].astype(BF16)
    w_out_b = w_out[wa:].reshape(B_Q_HEADS, HEAD_DIM, d)[_B_HEAD_ORDER].reshape(wq_b, d)
    y, *narrowed = _ffn(x2d, oa.reshape(t, wa), ob.reshape(t, wq_b), w_out_a,
                        w_out_b.astype(BF16), norm2, w_gate.astype(BF16), w_up.astype(BF16),
                        w_down.astype(BF16), narrow)
    return y, narrowed


def _odd_layer(x2d, batch, norm1, w_qkv, qn, kn, rpb, w_out, norm2, w_router,
               w_gate, w_up, w_down):
    t, d = x2d.shape
    s = t // batch
    wc = C_HEADS * HEAD_DIM
    ones = jnp.ones((HEAD_DIM,), F32)
    col_gain = jnp.concatenate([jnp.tile(qn * (HEAD_DIM ** -0.5 * LOG2E), C_HEADS),
                                jnp.tile(kn, C_HEADS), jnp.tile(ones, C_HEADS)])
    cpl = wc // LANES
    norm_chunks = (True,) * (2 * cpl) + (False,) * cpl
    qkv = _norm_proj(x2d, norm1, w_qkv.astype(BF16), col_gain, norm_chunks)
    o = _na_attention(qkv.reshape(batch, s, 3 * wc), _na_bias_table(rpb))
    wr = jnp.zeros((d, LANES), F32).at[:, :N_EXPERTS].set(w_router.astype(F32))
    return _moe(x2d, o.reshape(t, wc), w_out.astype(BF16), norm2, wr, w_gate, w_up, w_down)


def kernel(x, ev_norm1, ev_w_in, ev_qn_a, ev_kn_a, ev_qn_b, ev_kn_b, ev_sink_b, ev_w_out, ev_norm2, ev_ffn_gate, ev_ffn_up, ev_ffn_down, od_norm1, od_w_qkv, od_qn, od_kn, od_rpb, od_w_out, od_norm2, od_router, od_exp_gate, od_exp_up, od_exp_down):
    batch, s, d = x.shape
    depth = ev_norm1.shape[0] + od_norm1.shape[0]
    h = x.reshape(batch * s, d)
    n_e, _, f_e = od_exp_gate.shape[1:]
    experts = None
    for layer in range(depth):
        j = layer // 2
        if layer % 2 == 0:
            if j < od_norm1.shape[0]:
                narrow = [od_exp_gate[j].reshape(n_e * d, f_e), od_exp_up[j].reshape(n_e * d, f_e),
                          od_exp_down[j].reshape(n_e * f_e, d)]
            else:
                narrow = [jnp.zeros((batch * s // 16, LANES), F32)] * 3
            h, narrowed = _even_layer(h, batch, ev_norm1[j], ev_w_in[j], ev_qn_a[j], ev_kn_a[j],
                                      ev_qn_b[j], ev_kn_b[j], ev_sink_b[j], ev_w_out[j],
                                      ev_norm2[j], ev_ffn_gate[j], ev_ffn_up[j], ev_ffn_down[j],
                                      narrow)
            experts = (narrowed[0].reshape(n_e, d, f_e), narrowed[1].reshape(n_e, d, f_e),
                       narrowed[2].reshape(n_e, f_e, d))
        else:
            h = _odd_layer(h, batch, od_norm1[j], od_w_qkv[j], od_qn[j], od_kn[j], od_rpb[j],
                           od_w_out[j], od_norm2[j], od_router[j], *experts)
    return h.reshape(batch, s, d)
```
